```python
import math
import jax, jax.numpy as jnp
from jax import lax
import numpy as np

D_MODEL = 2048
BATCH = 8
SEQ = 4096
DEPTH = 4

N_META = 16
BLOCK = 128
WINDOW = 128
PAD_FRONT = BLOCK - N_META
MIX_WIDTH = D_MODEL
ATT_HEAD_DIM = 128
ATT_WIDTH = MIX_WIDTH // 2
ATT_HEADS = ATT_WIDTH // ATT_HEAD_DIM
ATT_KV_HEADS = 2
KV_WIDTH = ATT_KV_HEADS * ATT_HEAD_DIM
ROT_DIM = ATT_HEAD_DIM // 4
ROPE_THETA = 500000.0
RET_WIDTH = MIX_WIDTH - ATT_WIDTH
RET_HEAD_DIM = 256
RET_HEADS = RET_WIDTH // RET_HEAD_DIM
RET_THETA = 10000.0
D_FF = -(-8 * D_MODEL // (3 * 256)) * 256
SPLITS = [ATT_WIDTH, KV_WIDTH, KV_WIDTH, RET_WIDTH, RET_WIDTH, RET_WIDTH, RET_WIDTH]
SPLIT_IDX = [int(s) for s in np.cumsum(SPLITS)[:-1]]
IN_COLS = int(sum(SPLITS))
EPS = 1e-6
NEG = -1e30

kernel_name = "hymba_style_swa_retention_encoder"


def rms_norm(x, g):
    xf = x.astype(jnp.float32)
    y = xf * lax.rsqrt(jnp.mean(xf * xf, axis=-1, keepdims=True) + EPS)
    return (y * g.astype(jnp.float32)).astype(x.dtype)


def rope(x, pos, theta, rot_dim):
    half = rot_dim // 2
    inv = theta ** (-jnp.arange(half, dtype=jnp.float32) / half)
    ang = pos.astype(jnp.float32)[:, None] * inv[None, :]
    cos = jnp.cos(ang).astype(x.dtype)
    sin = jnp.sin(ang).astype(x.dtype)
    x1, x2, rest = x[..., :half], x[..., half:rot_dim], x[..., rot_dim:]
    return jnp.concatenate([x1 * cos - x2 * sin, x2 * cos + x1 * sin, rest], axis=-1)


def band_blocks(t):
    b_, h_, n_, d_ = t.shape
    nb = n_ // BLOCK
    tb = t.reshape(b_, h_, nb, BLOCK, d_)
    z = jnp.zeros_like(tb[:, :, :1])
    prev = jnp.concatenate([z, tb[:, :, :-1]], axis=2)
    nxt = jnp.concatenate([tb[:, :, 1:], z], axis=2)
    return jnp.concatenate([prev, tb, nxt], axis=3)


def windowed_sink_attention(q, k, v, sink):
    b_, hq, n_, dh = q.shape
    hkv = k.shape[1]
    g_ = hq // hkv
    nb = n_ // BLOCK
    qb = q.reshape(b_, hkv, g_, nb, BLOCK, dh)
    kb, vb = band_blocks(k), band_blocks(v)
    km, vm = k[:, :, PAD_FRONT:BLOCK], v[:, :, PAD_FRONT:BLOCK]
    qi = jnp.arange(nb)[:, None] * BLOCK + jnp.arange(BLOCK)[None, :]
    kj = (jnp.arange(nb)[:, None] - 1) * BLOCK + jnp.arange(3 * BLOCK)[None, :]
    band_ok = ((jnp.abs(qi[:, :, None] - kj[:, None, :]) <= WINDOW)
               & (kj[:, None, :] >= PAD_FRONT) & (kj[:, None, :] < n_))
    mj = PAD_FRONT + jnp.arange(N_META)
    meta_ok = jnp.abs(qi[:, :, None] - mj[None, None, :]) > WINDOW
    scale = dh ** -0.5
    s_band = jnp.einsum('bkgnqd,bknjd->bkgnqj', qb, kb).astype(jnp.float32) * scale
    s_meta = jnp.einsum('bkgnqd,bkmd->bkgnqm', qb, km).astype(jnp.float32) * scale
    s_sink = jnp.broadcast_to(sink.astype(jnp.float32).reshape(hkv, g_)[None, :, :, None, None, None],
                              s_band.shape[:-1] + (1,))
    s = jnp.concatenate([jnp.where(band_ok, s_band, NEG), jnp.where(meta_ok, s_meta, NEG), s_sink], axis=-1)
    p = jax.nn.softmax(s, axis=-1).astype(v.dtype)
    nk = 3 * BLOCK
    o = (jnp.einsum('bkgnqj,bknjd->bkgnqd', p[..., :nk], vb)
         + jnp.einsum('bkgnqm,bkmd->bkgnqd', p[..., nk:nk + N_META], vm))
    return o.reshape(b_, hq, n_, dh)


def retention_direction(q, k, v, log_gamma, include_diag):
    b_, h_, n_, dk = q.shape
    dv = v.shape[-1]
    nc = n_ // BLOCK
    qc = q.reshape(b_, h_, nc, BLOCK, dk)
    kc = k.reshape(b_, h_, nc, BLOCK, dk)
    vc = v.reshape(b_, h_, nc, BLOCK, dv)
    lg = log_gamma[:, None]
    idx = jnp.arange(BLOCK, dtype=jnp.float32)
    diff = idx[:, None] - idx[None, :]
    keep = (diff >= 0) if include_diag else (diff > 0)
    dmask = jnp.where(keep[None], jnp.exp(lg[:, :, None] * jnp.maximum(diff, 0.0)[None]), 0.0)
    scores = jnp.einsum('bhcid,bhcjd->bhcij', qc, kc) * dmask[None, :, None]
    intra = jnp.einsum('bhcij,bhcjv->bhciv', scores, vc)
    zeta = jnp.exp(lg * (BLOCK - 1 - idx)[None, :])
    kv_chunk = jnp.einsum('bhcjd,hj,bhcjv->bhcdv', kc, zeta, vc)
    decay_chunk = jnp.exp(log_gamma * BLOCK)[None, :, None, None]

    def step(state, kv_c):
        return state * decay_chunk + kv_c, state

    _, states = lax.scan(step, jnp.zeros((b_, h_, dk, dv), jnp.float32), jnp.moveaxis(kv_chunk, 2, 0))
    states = jnp.moveaxis(states, 0, 2)
    xi = jnp.exp(lg * (idx + 1.0)[None, :])
    cross = jnp.einsum('bhcid,hi,bhcdv->bhciv', qc, xi, states)
    return (intra + cross).reshape(b_, h_, n_, dv)


def bidirectional_retention(q, k, v, raw_fwd, raw_bwd):
    lg_f = -jnp.exp(raw_fwd.astype(jnp.float32))
    lg_b = -jnp.exp(raw_bwd.astype(jnp.float32))
    qf, kf, vf = q.astype(jnp.float32), k.astype(jnp.float32), v.astype(jnp.float32)
    fwd = retention_direction(qf, kf, vf, lg_f, True)
    bwd = retention_direction(qf[:, :, ::-1], kf[:, :, ::-1], vf[:, :, ::-1], lg_b, False)[:, :, ::-1]
    return fwd + bwd


def to_heads(t, n_heads, dh):
    b_, n_, _ = t.shape
    return t.reshape(b_, n_, n_heads, dh).transpose(0, 2, 1, 3)


def _fwd_setup_inputs(seed: int = 0) -> dict:
    key = jax.random.key(seed)
    ks = jax.random.split(key, 16)
    f32 = jnp.float32

    def nrm(k, shape, scale):
        return jax.random.normal(k, shape, f32) * scale

    def gain(k, shape):
        return 1.0 + 0.02 * jax.random.normal(k, shape, f32)

    base_decay = np.log(-np.log(1.0 - 2.0 ** (-5.0 - np.arange(RET_HEADS)))).astype(np.float32)
    return {
        "x": nrm(ks[0], (BATCH, SEQ, D_MODEL), 1.0),
        "meta_tokens": nrm(ks[1], (N_META, D_MODEL), 1.0),
        "w_in": nrm(ks[2], (DEPTH, D_MODEL, IN_COLS), D_MODEL ** -0.5),
        "w_out": nrm(ks[3], (DEPTH, MIX_WIDTH, D_MODEL), MIX_WIDTH ** -0.5),
        "attn_sink": nrm(ks[4], (DEPTH, ATT_HEADS), 0.5),
        "ret_decay_fwd": jnp.asarray(base_decay)[None, :] + nrm(ks[5], (DEPTH, RET_HEADS), 0.01),
        "ret_decay_bwd": jnp.asarray(base_decay)[None, :] + nrm(ks[6], (DEPTH, RET_HEADS), 0.01),
        "ret_norm": gain(ks[7], (DEPTH, RET_WIDTH)),
        "norm_mix_pre": gain(ks[8], (DEPTH, D_MODEL)),
        "norm_mix_post": gain(ks[9], (DEPTH, D_MODEL)),
        "w_gate": nrm(ks[10], (DEPTH, D_MODEL, D_FF), D_MODEL ** -0.5),
        "w_up": nrm(ks[11], (DEPTH, D_MODEL, D_FF), D_MODEL ** -0.5),
        "w_down": nrm(ks[12], (DEPTH, D_FF, D_MODEL), D_FF ** -0.5),
        "norm_ffn_pre": gain(ks[13], (DEPTH, D_MODEL)),
        "norm_ffn_post": gain(ks[14], (DEPTH, D_MODEL)),
    }


def _fwd_reference(x, meta_tokens, w_in, w_out, attn_sink, ret_decay_fwd, ret_decay_bwd, ret_norm,
              norm_mix_pre, norm_mix_post, w_gate, w_up, w_down, norm_ffn_pre, norm_ffn_post):
    b_ = x.shape[0]
    meta = jnp.broadcast_to(meta_tokens.astype(x.dtype)[None], (b_, N_META, D_MODEL))
    h = jnp.concatenate([meta, x], axis=1)
    n_pad = PAD_FRONT + h.shape[1]
    pos = jnp.arange(n_pad) - PAD_FRONT
    for l in range(DEPTH):
        u = rms_norm(h, norm_mix_pre[l])
        proj = jnp.einsum('bnd,dc->bnc', u, w_in[l])
        proj = jnp.pad(proj, ((0, 0), (PAD_FRONT, 0), (0, 0)))
        aq, ak, av, rq, rk, rv, rg = jnp.split(proj, SPLIT_IDX, axis=-1)
        aq = rope(to_heads(aq, ATT_HEADS, ATT_HEAD_DIM), pos, ROPE_THETA, ROT_DIM)
        ak = rope(to_heads(ak, ATT_KV_HEADS, ATT_HEAD_DIM), pos, ROPE_THETA, ROT_DIM)
        av = to_heads(av, ATT_KV_HEADS, ATT_HEAD_DIM)
        att = windowed_sink_attention(aq, ak, av, attn_sink[l])
        att = att.transpose(0, 2, 1, 3).reshape(b_, n_pad, ATT_WIDTH)
        rq = rope(to_heads(rq, RET_HEADS, RET_HEAD_DIM), pos, RET_THETA, RET_HEAD_DIM)
        rk = rope(to_heads(rk, RET_HEADS, RET_HEAD_DIM), pos, RET_THETA, RET_HEAD_DIM) * (RET_HEAD_DIM ** -0.5)
        rv = to_heads(rv, RET_HEADS, RET_HEAD_DIM)
        ret = bidirectional_retention(rq, rk, rv, ret_decay_fwd[l], ret_decay_bwd[l])
        ret = rms_norm(ret.transpose(0, 2, 1, 3), ret_norm[l].reshape(RET_HEADS, RET_HEAD_DIM))
        ret = ret.reshape(b_, n_pad, RET_WIDTH).astype(h.dtype)
        ret = jax.nn.silu(rg) * ret
        mixed = jnp.concatenate([att, ret], axis=-1)[:, PAD_FRONT:]
        mixed = jnp.einsum('bnc,cd->bnd', mixed, w_out[l])
        h = h + rms_norm(mixed, norm_mix_post[l])
        u = rms_norm(h, norm_ffn_pre[l])
        f = jax.nn.silu(jnp.einsum('bnd,df->bnf', u, w_gate[l])) * jnp.einsum('bnd,df->bnf', u, w_up[l])
        f = jnp.einsum('bnf,fd->bnd', f, w_down[l])
        h = h + rms_norm(f, norm_ffn_post[l])
    return h[:, N_META:]


import jax as _jax
import jax.numpy as _jnp

TWIN_FORMAT = 'train_step'
FWD_PARAMS = ['x', 'meta_tokens', 'w_in', 'w_out', 'attn_sink', 'ret_decay_fwd', 'ret_decay_bwd', 'ret_norm', 'norm_mix_pre', 'norm_mix_post', 'w_gate', 'w_up', 'w_down', 'norm_ffn_pre', 'norm_ffn_post']
TWIN_WEIGHTS = ['meta_tokens', 'w_in', 'w_out', 'attn_sink', 'ret_decay_fwd', 'ret_decay_bwd', 'ret_norm', 'norm_mix_pre', 'norm_mix_post', 'w_gate', 'w_up', 'w_down', 'norm_ffn_pre', 'norm_ffn_post']
TWIN_DIFF_INPUT = 'x'
TWIN_INPUTS = ['x', 'meta_tokens', 'w_in', 'w_out', 'attn_sink', 'ret_decay_fwd', 'ret_decay_bwd', 'ret_norm', 'norm_mix_pre', 'norm_mix_post', 'w_gate', 'w_up', 'w_down', 'norm_ffn_pre', 'norm_ffn_post', 'loss_target', 'm_meta_tokens', 'm_w_in', 'm_w_out', 'm_attn_sink', 'm_ret_decay_fwd', 'm_ret_decay_bwd', 'm_ret_norm', 'm_norm_mix_pre', 'm_norm_mix_post', 'm_w_gate', 'm_w_up', 'm_w_down', 'm_norm_ffn_pre', 'm_norm_ffn_post', 'v_meta_tokens', 'v_w_in', 'v_w_out', 'v_attn_sink', 'v_ret_decay_fwd', 'v_ret_decay_bwd', 'v_ret_norm', 'v_norm_mix_pre', 'v_norm_mix_post', 'v_w_gate', 'v_w_up', 'v_w_down', 'v_norm_ffn_pre', 'v_norm_ffn_post']
TWIN_OUTPUTS = ['loss', 'grad_x', 'grad_meta_tokens', 'grad_w_in', 'grad_w_out', 'grad_attn_sink', 'grad_ret_decay_fwd', 'grad_ret_decay_bwd', 'grad_ret_norm', 'grad_norm_mix_pre', 'grad_norm_mix_post', 'grad_w_gate', 'grad_w_up', 'grad_w_down', 'grad_norm_ffn_pre', 'grad_norm_ffn_post', 'delta_meta_tokens', 'delta_w_in', 'delta_w_out', 'delta_attn_sink', 'delta_ret_decay_fwd', 'delta_ret_decay_bwd', 'delta_ret_norm', 'delta_norm_mix_pre', 'delta_norm_mix_post', 'delta_w_gate', 'delta_w_up', 'delta_w_down', 'delta_norm_ffn_pre', 'delta_norm_ffn_post', 'new_m_meta_tokens', 'new_m_w_in', 'new_m_w_out', 'new_m_attn_sink', 'new_m_ret_decay_fwd', 'new_m_ret_decay_bwd', 'new_m_ret_norm', 'new_m_norm_mix_pre', 'new_m_norm_mix_post', 'new_m_w_gate', 'new_m_w_up', 'new_m_w_down', 'new_m_norm_ffn_pre', 'new_m_norm_ffn_post', 'new_v_meta_tokens', 'new_v_w_in', 'new_v_w_out', 'new_v_attn_sink', 'new_v_ret_decay_fwd', 'new_v_ret_decay_bwd', 'new_v_ret_norm', 'new_v_norm_mix_pre', 'new_v_norm_mix_post', 'new_v_w_gate', 'new_v_w_up', 'new_v_w_down', 'new_v_norm_ffn_pre', 'new_v_norm_ffn_post']
TWIN_LEAF_KINDS = {'loss': 'loss', 'grad_x': 'grad_x', 'grad_meta_tokens': 'grad_w', 'grad_w_in': 'grad_w', 'grad_w_out': 'grad_w', 'grad_attn_sink': 'grad_w', 'grad_ret_decay_fwd': 'grad_w', 'grad_ret_decay_bwd': 'grad_w', 'grad_ret_norm': 'grad_w', 'grad_norm_mix_pre': 'grad_w', 'grad_norm_mix_post': 'grad_w', 'grad_w_gate': 'grad_w', 'grad_w_up': 'grad_w', 'grad_w_down': 'grad_w', 'grad_norm_ffn_pre': 'grad_w', 'grad_norm_ffn_post': 'grad_w', 'delta_meta_tokens': 'delta_w', 'delta_w_in': 'delta_w', 'delta_w_out': 'delta_w', 'delta_attn_sink': 'delta_w', 'delta_ret_decay_fwd': 'delta_w', 'delta_ret_decay_bwd': 'delta_w', 'delta_ret_norm': 'delta_w', 'delta_norm_mix_pre': 'delta_w', 'delta_norm_mix_post': 'delta_w', 'delta_w_gate': 'delta_w', 'delta_w_up': 'delta_w', 'delta_w_down': 'delta_w', 'delta_norm_ffn_pre': 'delta_w', 'delta_norm_ffn_post': 'delta_w', 'new_m_meta_tokens': 'new_m', 'new_m_w_in': 'new_m', 'new_m_w_out': 'new_m', 'new_m_attn_sink': 'new_m', 'new_m_ret_decay_fwd': 'new_m', 'new_m_ret_decay_bwd': 'new_m', 'new_m_ret_norm': 'new_m', 'new_m_norm_mix_pre': 'new_m', 'new_m_norm_mix_post': 'new_m', 'new_m_w_gate': 'new_m', 'new_m_w_up': 'new_m', 'new_m_w_down': 'new_m', 'new_m_norm_ffn_pre': 'new_m', 'new_m_norm_ffn_post': 'new_m', 'new_v_meta_tokens': 'new_v', 'new_v_w_in': 'new_v', 'new_v_w_out': 'new_v', 'new_v_attn_sink': 'new_v', 'new_v_ret_decay_fwd': 'new_v', 'new_v_ret_decay_bwd': 'new_v', 'new_v_ret_norm': 'new_v', 'new_v_norm_mix_pre': 'new_v', 'new_v_norm_mix_post': 'new_v', 'new_v_w_gate': 'new_v', 'new_v_w_up': 'new_v', 'new_v_w_down': 'new_v', 'new_v_norm_ffn_pre': 'new_v', 'new_v_norm_ffn_post': 'new_v'}


def _forward(args):
    return _fwd_reference(*[args[k] for k in FWD_PARAMS])


def _output_shape():
    def fwd():
        inp = _fwd_setup_inputs(0)
        return _fwd_reference(*[inp[k] for k in FWD_PARAMS])
    out = _jax.eval_shape(fwd)
    return out.shape, out.dtype

N_MICROBATCH = 1
ADAM_LR = 0.001
ADAM_B1 = 0.9
ADAM_B2 = 0.999
ADAM_EPS = 1e-08
ADAM_WD = 0.01
ADAM_STEP = 10
PER_EXAMPLE_BATCH_AXIS = {'x': 0, 'loss_target': 0}
SHARED_INPUTS = []
_WEIGHT_DTYPES = {'meta_tokens': _jnp.float32, 'w_in': _jnp.float32, 'w_out': _jnp.float32, 'attn_sink': _jnp.float32, 'ret_decay_fwd': _jnp.float32, 'ret_decay_bwd': _jnp.float32, 'ret_norm': _jnp.float32, 'norm_mix_pre': _jnp.float32, 'norm_mix_post': _jnp.float32, 'w_gate': _jnp.float32, 'w_up': _jnp.float32, 'w_down': _jnp.float32, 'norm_ffn_pre': _jnp.float32, 'norm_ffn_post': _jnp.float32}
MOMENT_SCALE = {'meta_tokens': 1.392875e-01, 'w_in': 1.076957e+00, 'w_out': 8.971616e-01, 'attn_sink': 1.357361e-02, 'ret_decay_fwd': 9.656792e+00, 'ret_decay_bwd': 1.466722e+01, 'ret_norm': 1.248809e+00, 'norm_mix_pre': 1.879457e+00, 'norm_mix_post': 1.579257e+01, 'w_gate': 4.061490e-01, 'w_up': 4.115230e-01, 'w_down': 6.855905e-01, 'norm_ffn_pre': 9.685316e-01, 'norm_ffn_post': 1.591063e+01}


def _to_microbatches(a, axis):
    t = _jnp.moveaxis(a, axis, 0)
    t = t.reshape((N_MICROBATCH, t.shape[0] // N_MICROBATCH) + t.shape[1:])
    return _jnp.moveaxis(t, 1, axis + 1)


def setup_inputs(seed: int = 0) -> dict:
    inp = _fwd_setup_inputs(seed)
    key = _jax.random.fold_in(_jax.random.key(seed), 7919)
    shape, _ = _output_shape()
    out = dict(inp)
    out["loss_target"] = _jax.random.normal(_jax.random.fold_in(key, 0), shape, _jnp.float32)
    for i, name in enumerate(TWIN_WEIGHTS):
        w = inp[name].astype(_jnp.float32)
        if MOMENT_SCALE is None:
            s = _jnp.sqrt(_jnp.mean(_jnp.square(w)) + 1e-30)
        else:
            s = MOMENT_SCALE[name]
        km, kv = _jax.random.split(_jax.random.fold_in(key, i + 1))
        out[name] = w
        out["m_" + name] = s * _jax.random.normal(km, w.shape, _jnp.float32)
        out["v_" + name] = (s * s) * _jax.random.uniform(kv, w.shape, _jnp.float32, 0.5, 1.5)
    if N_MICROBATCH > 1:
        for name, axis in PER_EXAMPLE_BATCH_AXIS.items():
            out[name] = _to_microbatches(out[name], axis)
    return {'x': out['x'], 'meta_tokens': out['meta_tokens'], 'w_in': out['w_in'], 'w_out': out['w_out'], 'attn_sink': out['attn_sink'], 'ret_decay_fwd': out['ret_decay_fwd'], 'ret_decay_bwd': out['ret_decay_bwd'], 'ret_norm': out['ret_norm'], 'norm_mix_pre': out['norm_mix_pre'], 'norm_mix_post': out['norm_mix_post'], 'w_gate': out['w_gate'], 'w_up': out['w_up'], 'w_down': out['w_down'], 'norm_ffn_pre': out['norm_ffn_pre'], 'norm_ffn_post': out['norm_ffn_post'], 'loss_target': out['loss_target'], 'm_meta_tokens': out['m_meta_tokens'], 'm_w_in': out['m_w_in'], 'm_w_out': out['m_w_out'], 'm_attn_sink': out['m_attn_sink'], 'm_ret_decay_fwd': out['m_ret_decay_fwd'], 'm_ret_decay_bwd': out['m_ret_decay_bwd'], 'm_ret_norm': out['m_ret_norm'], 'm_norm_mix_pre': out['m_norm_mix_pre'], 'm_norm_mix_post': out['m_norm_mix_post'], 'm_w_gate': out['m_w_gate'], 'm_w_up': out['m_w_up'], 'm_w_down': out['m_w_down'], 'm_norm_ffn_pre': out['m_norm_ffn_pre'], 'm_norm_ffn_post': out['m_norm_ffn_post'], 'v_meta_tokens': out['v_meta_tokens'], 'v_w_in': out['v_w_in'], 'v_w_out': out['v_w_out'], 'v_attn_sink': out['v_attn_sink'], 'v_ret_decay_fwd': out['v_ret_decay_fwd'], 'v_ret_decay_bwd': out['v_ret_decay_bwd'], 'v_ret_norm': out['v_ret_norm'], 'v_norm_mix_pre': out['v_norm_mix_pre'], 'v_norm_mix_post': out['v_norm_mix_post'], 'v_w_gate': out['v_w_gate'], 'v_w_up': out['v_w_up'], 'v_w_down': out['v_w_down'], 'v_norm_ffn_pre': out['v_norm_ffn_pre'], 'v_norm_ffn_post': out['v_norm_ffn_post']}


def _loss(weights, diff, rest, loss_target):
    with _jax.named_scope("forward"):
        args = {**rest, TWIN_DIFF_INPUT: diff, **{k: w.astype(_WEIGHT_DTYPES[k]) for k, w in weights.items()}}
        y = _forward(args)
    with _jax.named_scope("loss_head"):
        err = _jnp.square(y.astype(_jnp.float32) - loss_target)
        return 0.5 * _jnp.sum(_jnp.mean(err, axis=-1)) if err.ndim else 0.5 * err


def _adamw(w, g, m, v):
    m = ADAM_B1 * m + (1.0 - ADAM_B1) * g
    v = ADAM_B2 * v + (1.0 - ADAM_B2) * _jnp.square(g)
    m_hat = m / (1.0 - ADAM_B1 ** ADAM_STEP)
    v_hat = v / (1.0 - ADAM_B2 ** ADAM_STEP)
    delta = -ADAM_LR * (m_hat / (_jnp.sqrt(v_hat) + ADAM_EPS) + ADAM_WD * w)
    return delta, m, v


def reference(x, meta_tokens, w_in, w_out, attn_sink, ret_decay_fwd, ret_decay_bwd, ret_norm, norm_mix_pre, norm_mix_post, w_gate, w_up, w_down, norm_ffn_pre, norm_ffn_post, loss_target, m_meta_tokens, m_w_in, m_w_out, m_attn_sink, m_ret_decay_fwd, m_ret_decay_bwd, m_ret_norm, m_norm_mix_pre, m_norm_mix_post, m_w_gate, m_w_up, m_w_down, m_norm_ffn_pre, m_norm_ffn_post, v_meta_tokens, v_w_in, v_w_out, v_attn_sink, v_ret_decay_fwd, v_ret_decay_bwd, v_ret_norm, v_norm_mix_pre, v_norm_mix_post, v_w_gate, v_w_up, v_w_down, v_norm_ffn_pre, v_norm_ffn_post):
    given = dict(x=x, meta_tokens=meta_tokens, w_in=w_in, w_out=w_out, attn_sink=attn_sink, ret_decay_fwd=ret_decay_fwd, ret_decay_bwd=ret_decay_bwd, ret_norm=ret_norm, norm_mix_pre=norm_mix_pre, norm_mix_post=norm_mix_post, w_gate=w_gate, w_up=w_up, w_down=w_down, norm_ffn_pre=norm_ffn_pre, norm_ffn_post=norm_ffn_post, loss_target=loss_target, m_meta_tokens=m_meta_tokens, m_w_in=m_w_in, m_w_out=m_w_out, m_attn_sink=m_attn_sink, m_ret_decay_fwd=m_ret_decay_fwd, m_ret_decay_bwd=m_ret_decay_bwd, m_ret_norm=m_ret_norm, m_norm_mix_pre=m_norm_mix_pre, m_norm_mix_post=m_norm_mix_post, m_w_gate=m_w_gate, m_w_up=m_w_up, m_w_down=m_w_down, m_norm_ffn_pre=m_norm_ffn_pre, m_norm_ffn_post=m_norm_ffn_post, v_meta_tokens=v_meta_tokens, v_w_in=v_w_in, v_w_out=v_w_out, v_attn_sink=v_attn_sink, v_ret_decay_fwd=v_ret_decay_fwd, v_ret_decay_bwd=v_ret_decay_bwd, v_ret_norm=v_ret_norm, v_norm_mix_pre=v_norm_mix_pre, v_norm_mix_post=v_norm_mix_post, v_w_gate=v_w_gate, v_w_up=v_w_up, v_w_down=v_w_down, v_norm_ffn_pre=v_norm_ffn_pre, v_norm_ffn_post=v_norm_ffn_post)
    weights = {n: given[n] for n in TWIN_WEIGHTS}
    shared = {n: given[n] for n in SHARED_INPUTS}
    per_example = {n: given[n] for n in ['x']}
    grad_fn = _jax.value_and_grad(_loss, argnums=(0, 1))

    def one_microbatch(ex, loss_target):
        ex = dict(ex)
        diff = ex.pop(TWIN_DIFF_INPUT)
        return grad_fn(weights, diff, {**shared, **ex}, loss_target)

    if N_MICROBATCH == 1:
        loss, (grad_w, grad_x) = one_microbatch(per_example, given["loss_target"])
    else:
        def body(carry, xs):
            loss_sum, grad_sum = carry
            l_k, (gw_k, gx_k) = one_microbatch(xs[0], xs[1])
            with _jax.named_scope("update"):
                return (loss_sum + l_k, _jax.tree.map(_jnp.add, grad_sum, gw_k)), gx_k

        init = (_jnp.zeros((), _jnp.float32), _jax.tree.map(_jnp.zeros_like, weights))
        (loss, grad_w), grad_x = _jax.lax.scan(body, init, (per_example, given["loss_target"]))
    with _jax.named_scope("update"):
        delta_w, new_m, new_v = {}, {}, {}
        for n in TWIN_WEIGHTS:
            delta_w[n], new_m[n], new_v[n] = _adamw(weights[n], grad_w[n], given["m_" + n], given["v_" + n])
    return (loss, grad_x, *[grad_w[n] for n in TWIN_WEIGHTS], *[delta_w[n] for n in TWIN_WEIGHTS],
            *[new_m[n] for n in TWIN_WEIGHTS], *[new_v[n] for n in TWIN_WEIGHTS])
```

```python
import jax
import jax.numpy as jnp
import numpy as np
from jax import lax
from jax.experimental import pallas as pl
from jax.experimental.pallas import tpu as pltpu

F32, BF16 = jnp.float32, jnp.bfloat16

D_MODEL = 2048
N_META = 16
BLOCK = 128
WINDOW = 128
PAD_FRONT = BLOCK - N_META
ATT_HEAD_DIM = 128
ATT_WIDTH = D_MODEL // 2
ATT_HEADS = ATT_WIDTH // ATT_HEAD_DIM
ATT_KV_HEADS = 2
ATT_GROUP = ATT_HEADS // ATT_KV_HEADS
KV_WIDTH = ATT_KV_HEADS * ATT_HEAD_DIM
ROT_DIM = ATT_HEAD_DIM // 4
ROPE_THETA = 500000.0
RET_WIDTH = D_MODEL - ATT_WIDTH
RET_HEAD_DIM = 256
RET_HEADS = RET_WIDTH // RET_HEAD_DIM
RET_THETA = 10000.0
D_FF = 5632
IN_COLS = ATT_WIDTH + 2 * KV_WIDTH + 4 * RET_WIDTH
N_DEV = 8
SHARD_COLS = IN_COLS // N_DEV
EPS = 1e-6
NEG = -1e30
RET_K_SCALE = RET_HEAD_DIM ** -0.5
ATT_SCALE = ATT_HEAD_DIM ** -0.5

COL_AK = ATT_WIDTH // ATT_HEAD_DIM
COL_AV256 = (ATT_WIDTH + KV_WIDTH) // 256
COL_RQ = (ATT_WIDTH + 2 * KV_WIDTH) // RET_HEAD_DIM
COL_RK = COL_RQ + RET_HEADS
COL_RV = COL_RK + RET_HEADS
COL_RG = COL_RV + RET_HEADS

ADAM_LR, ADAM_B1, ADAM_B2, ADAM_EPS, ADAM_WD, ADAM_STEP = 0.001, 0.9, 0.999, 1e-08, 0.01, 10

ROW_MIX_PRE, ROW_MIX_POST, ROW_FFN_PRE, ROW_FFN_POST, ROW_RET_NORM, ROW_MISC, ROW_META, SMALL_ROWS = 0, 4, 8, 12, 16, 18, 32, 48
MISC_SINK, MISC_DF, MISC_DB, MISC_LOSS = 0, 32, 48, 64
ADAMW_TILE_ELEMS = 128 * 1024

MESH = pl.DeviceIdType.MESH
ANY = pl.BlockSpec(memory_space=pl.ANY)


def _row_tile(n, cap):
    for t in range(cap - cap % 16, 0, -16):
        if n % t == 0:
            return t
    raise ValueError(n)


def _sds(shape, dtype):
    return jax.ShapeDtypeStruct(shape, dtype)


def _silu(x):
    return x * jax.nn.sigmoid(x)


def _dsilu(x):
    s = jax.nn.sigmoid(x)
    return s * (1.0 + x * (1.0 - s))


def _norm_fwd(x, g, res, out_dtype, name):
    n, d = x.shape
    tr = _row_tile(n, 384)

    def body(*refs):
        if res is None:
            x_ref, g_ref, o_ref = refs
        else:
            x_ref, g_ref, r_ref, o_ref = refs
        xv = x_ref[...]
        r = lax.rsqrt(jnp.mean(xv * xv, axis=-1, keepdims=True) + EPS)
        y = xv * r * g_ref[...]
        if res is not None:
            y = y + r_ref[...]
        o_ref[...] = y.astype(o_ref.dtype)

    row = pl.BlockSpec((tr, d), lambda i: (i, 0))
    ins = [row, pl.BlockSpec((1, d), lambda i: (0, 0))] + ([row] if res is not None else [])
    args = (x, g) + ((res,) if res is not None else ())
    return pl.pallas_call(body, grid=(n // tr,), in_specs=ins, out_specs=row, out_shape=_sds((n, d), out_dtype), name=name)(*args)


def _norm_bwd(dy, x, g, res, out_dtype, name):
    n, d = x.shape
    tr = _row_tile(n, 384)

    def body(*refs):
        if res is None:
            dy_ref, x_ref, g_ref, dx_ref, dg_ref = refs
        else:
            dy_ref, x_ref, g_ref, r_ref, dx_ref, dg_ref = refs
        i = pl.program_id(0)
        xv = x_ref[...]
        r = lax.rsqrt(jnp.mean(xv * xv, axis=-1, keepdims=True) + EPS)
        xhat = xv * r
        dyf = dy_ref[...].astype(F32)
        gdy = dyf * g_ref[...]
        dx = r * (gdy - xhat * jnp.mean(gdy * xhat, axis=-1, keepdims=True))
        if res is not None:
            dx = dx + r_ref[...]
        dx_ref[...] = dx.astype(dx_ref.dtype)

        @pl.when(i == 0)
        def _():
            dg_ref[...] = jnp.zeros_like(dg_ref)

        dg_ref[...] += jnp.sum(dyf * xhat, axis=0, keepdims=True)

    row = pl.BlockSpec((tr, d), lambda i: (i, 0))
    vec = pl.BlockSpec((1, d), lambda i: (0, 0))
    ins = [row, row, vec] + ([row] if res is not None else [])
    args = (dy, x, g) + ((res,) if res is not None else ())
    return pl.pallas_call(body, grid=(n // tr,), in_specs=ins, out_specs=(row, vec),
                          out_shape=(_sds((n, d), out_dtype), _sds((1, d), F32)), name=name,
                          compiler_params=pltpu.CompilerParams(dimension_semantics=("arbitrary",)))(*args)


def _mm(a, b, *, ta, tb, grid, a_blk, a_map, b_blk, b_map, o_blk, o_map, o_shape, o_dtype, name, acc=None):
    nk = grid[2]
    dims = (((0,) if ta else (1,), (1,) if tb else (0,)), ((), ()))

    def body(*refs):
        if acc is None:
            a_ref, b_ref, o_ref = refs[:3]
            c_ref = None
        else:
            a_ref, b_ref, c_ref, o_ref = refs[:4]
        part = lax.dot_general(a_ref[...], b_ref[...], dims, preferred_element_type=F32)
        if nk == 1:
            if c_ref is not None:
                part = part + c_ref[...].astype(F32)
            o_ref[...] = part.astype(o_ref.dtype)
            return
        acc_ref = refs[-1]
        k = pl.program_id(2)

        @pl.when(k == 0)
        def _():
            acc_ref[...] = jnp.zeros_like(acc_ref) if c_ref is None else c_ref[...].astype(F32)

        acc_ref[...] += part

        @pl.when(k == nk - 1)
        def _():
            o_ref[...] = acc_ref[...].astype(o_ref.dtype)

    ins = [pl.BlockSpec(a_blk, a_map), pl.BlockSpec(b_blk, b_map)]
    args = [a, b]
    if acc is not None:
        ins.append(pl.BlockSpec(o_blk, o_map))
        args.append(acc)
    return pl.pallas_call(body, grid=grid, in_specs=ins, out_specs=pl.BlockSpec(o_blk, o_map), out_shape=_sds(o_shape, o_dtype),
                          scratch_shapes=[pltpu.VMEM(o_blk, F32)] if nk > 1 else [], name=name,
                          compiler_params=pltpu.CompilerParams(dimension_semantics=("parallel", "parallel", "arbitrary")))(*args)


TOKEN_TILE = 1056
WIDE_TILE = 1408


def _mm_nn(x, w, o_dtype, tn, tk, name):
    n, k = x.shape
    tm = _row_tile(n, TOKEN_TILE)
    return _mm(x, w, ta=False, tb=False, grid=(n // tm, w.shape[1] // tn, k // tk), a_blk=(tm, tk), a_map=lambda i, j, kk: (i, kk),
               b_blk=(tk, tn), b_map=lambda i, j, kk: (kk, j), o_blk=(tm, tn), o_map=lambda i, j, kk: (i, j),
               o_shape=(n, w.shape[1]), o_dtype=o_dtype, name=name)


def _mm_nt(dy, w, o_dtype, tn, tk, name, acc=None):
    n, k = dy.shape
    tm = _row_tile(n, TOKEN_TILE)
    return _mm(dy, w, ta=False, tb=True, grid=(n // tm, w.shape[0] // tn, k // tk), a_blk=(tm, tk), a_map=lambda i, j, kk: (i, kk),
               b_blk=(tn, tk), b_map=lambda i, j, kk: (j, kk), o_blk=(tm, tn), o_map=lambda i, j, kk: (i, j),
               o_shape=(n, w.shape[0]), o_dtype=o_dtype, name=name, acc=acc)


def _mm_tn(x, dy, tm, tn, name):
    n, m = x.shape
    tk = _row_tile(n, TOKEN_TILE)
    return _mm(x, dy, ta=True, tb=False, grid=(m // tm, dy.shape[1] // tn, n // tk), a_blk=(tk, tm), a_map=lambda i, j, kk: (kk, i),
               b_blk=(tk, tn), b_map=lambda i, j, kk: (kk, j), o_blk=(tm, tn), o_map=lambda i, j, kk: (i, j),
               o_shape=(m, dy.shape[1]), o_dtype=BF16, name=name)


def _rope_tables(n):
    pos = (jnp.arange(n) - PAD_FRONT).astype(F32)
    half = ROT_DIM // 2
    ang = pos[:, None] * (ROPE_THETA ** (-jnp.arange(half, dtype=F32) / half))[None, :]
    c, s = jnp.cos(ang), jnp.sin(ang)
    rest = ATT_HEAD_DIM - ROT_DIM
    cos_a = jnp.concatenate([c, c, jnp.ones((n, rest), F32)], axis=1)
    sin_a = jnp.concatenate([-s, s, jnp.zeros((n, rest), F32)], axis=1)
    half = RET_HEAD_DIM // 2
    ang = pos[:, None] * (RET_THETA ** (-jnp.arange(half, dtype=F32) / half))[None, :]
    c, s = jnp.cos(ang), jnp.sin(ang)
    perm = np.zeros((ATT_HEAD_DIM, ATT_HEAD_DIM), np.float32)
    for i in range(ROT_DIM):
        perm[(i + ROT_DIM // 2) % ROT_DIM, i] = 1.0
    return cos_a, sin_a, jnp.concatenate([c, c], axis=1), jnp.concatenate([-s, s], axis=1), jnp.asarray(perm, BF16)


def _rope_att(x, col0, heads, cos, sin, perm, name):
    n = x.shape[0]
    tr = _row_tile(n, 1056)

    def body(x_ref, c_ref, s_ref, p_ref, o_ref):
        xb = x_ref[...].astype(BF16)
        sw = jnp.dot(xb, p_ref[...], preferred_element_type=F32)
        o_ref[...] = (xb.astype(F32) * c_ref[...] + sw * s_ref[...]).astype(o_ref.dtype)

    hd = ATT_HEAD_DIM
    tab = pl.BlockSpec((tr, hd), lambda i, h: (i, 0))
    return pl.pallas_call(body, grid=(n // tr, heads),
                          in_specs=[pl.BlockSpec((tr, hd), lambda i, h: (i, col0 + h)), tab, tab, pl.BlockSpec((hd, hd), lambda i, h: (0, 0))],
                          out_specs=pl.BlockSpec((tr, hd), lambda i, h: (i, h)), out_shape=_sds((n, heads * hd), BF16), name=name)(x, cos, sin, perm)


def _rope_ret(x, col0, cos, sin, name):
    p, n, _ = x.shape
    tr = _row_tile(n, 1056)
    hd = RET_HEAD_DIM

    def body(*refs):
        x_ref, o_ref = refs[0], refs[-1]
        xv = x_ref[0].astype(F32)
        for q in range(1, p):
            xv = xv + x_ref[q].astype(F32)
        if cos is not None:
            sw = jnp.concatenate([xv[:, hd // 2:], xv[:, :hd // 2]], axis=1)
            xv = xv * refs[1][...] + sw * refs[2][...]
        o_ref[...] = xv.astype(o_ref.dtype)

    tab = pl.BlockSpec((tr, hd), lambda i, h: (i, 0))
    ins = [pl.BlockSpec((p, tr, hd), lambda i, h: (0, i, col0 + h))] + ([tab, tab] if cos is not None else [])
    args = (x,) + ((cos, sin) if cos is not None else ())
    return pl.pallas_call(body, grid=(n // tr, RET_HEADS), in_specs=ins, out_specs=pl.BlockSpec((tr, hd), lambda i, h: (i, h)),
                          out_shape=_sds((n, RET_WIDTH), BF16), name=name)(*args)


def _att_mask(nblk, n_tot):
    row = lax.broadcasted_iota(jnp.int32, (BLOCK, 4 * BLOCK), 0)
    col = lax.broadcasted_iota(jnp.int32, (BLOCK, 4 * BLOCK), 1)
    qi = nblk * BLOCK + row
    seg = col // BLOCK
    cj = col % BLOCK
    kj = (nblk - 1 + seg) * BLOCK + cj
    band = (jnp.abs(qi - kj) <= WINDOW) & (kj >= PAD_FRONT) & (kj < n_tot) & (seg < 3)
    meta = (seg == 3) & (cj >= PAD_FRONT) & (jnp.abs(qi - cj) > WINDOW)
    return band | meta


def _att_specs(nb, v_col):
    kv = lambda f, cb: pl.BlockSpec((BLOCK, KV_WIDTH), lambda n: (f(n), cb))
    prev, own, nxt, first = (lambda n: jnp.maximum(n - 1, 0)), (lambda n: n), (lambda n: jnp.minimum(n + 1, nb - 1)), (lambda n: 0)
    return [kv(f, 0) for f in (prev, own, nxt, first)] + [kv(f, v_col) for f in (prev, own, nxt, first)]


def _att_probs(s, ok, snk):
    s = jnp.where(ok, s, NEG)
    m = jnp.maximum(jnp.max(s, axis=-1, keepdims=True), snk)
    p = jnp.exp(s - m)
    ps = jnp.exp(snk - m)
    inv = 1.0 / (jnp.sum(p, axis=-1, keepdims=True) + ps)
    return p * inv, ps * inv


def _att_fwd(q, k, proj, sink_b, name):
    n = q.shape[0]
    nb = n // BLOCK
    hd = ATT_HEAD_DIM

    def body(q_ref, kp, ko, kn, km, vp, vo, vn, vm, sink_ref, o_ref):
        nblk = pl.program_id(0)
        ok = _att_mask(nblk, n)
        keep = (nblk * BLOCK + lax.broadcasted_iota(jnp.int32, (BLOCK, 1), 0)) >= PAD_FRONT
        for kh in range(ATT_KV_HEADS):
            cs = slice(kh * hd, (kh + 1) * hd)
            kk = jnp.concatenate([r[:, cs] for r in (kp, ko, kn, km)], axis=0)
            vv = jnp.concatenate([r[:, cs] for r in (vp, vo, vn, vm)], axis=0)
            heads = [kh * ATT_GROUP + g for g in range(ATT_GROUP)]
            q4 = jnp.concatenate([q_ref[:, h * hd:(h + 1) * hd] for h in heads], axis=0)
            s = lax.dot_general(q4, kk, (((1,), (1,)), ((), ())), preferred_element_type=F32) * ATT_SCALE
            ps = []
            for g, h in enumerate(heads):
                p, _ = _att_probs(s[g * BLOCK:(g + 1) * BLOCK], ok, sink_ref[h:h + 1, 0:1])
                ps.append(p)
            o = jnp.dot(jnp.concatenate(ps, axis=0).astype(BF16), vv, preferred_element_type=F32)
            for g, h in enumerate(heads):
                o_ref[:, h * hd:(h + 1) * hd] = jnp.where(keep, o[g * BLOCK:(g + 1) * BLOCK], 0.0).astype(o_ref.dtype)

    qspec = pl.BlockSpec((BLOCK, ATT_WIDTH), lambda i: (i, 0))
    return pl.pallas_call(body, grid=(nb,), in_specs=[qspec] + _att_specs(nb, COL_AV256) + [pl.BlockSpec((ATT_HEADS, 128), lambda i: (0, 0))],
                          out_specs=qspec, out_shape=_sds((n, ATT_WIDTH), BF16), name=name)(q, k, k, k, k, proj, proj, proj, proj, sink_b)


def _att_bwd(q, k, proj, sink_b, dmixed, name):
    n = q.shape[0]
    nb = n // BLOCK
    hd = ATT_HEAD_DIM

    def body(q_ref, kp, ko, kn, km, vp, vo, vn, vm, sink_ref, do_ref, dq_ref, dk_ref, dv_ref, dsink_ref):
        nblk = pl.program_id(0)

        @pl.when(nblk == 0)
        def _():
            dk_ref[...] = jnp.zeros_like(dk_ref)
            dv_ref[...] = jnp.zeros_like(dv_ref)
            dsink_ref[...] = jnp.zeros_like(dsink_ref)

        ok = _att_mask(nblk, n)
        rows = [jnp.maximum(nblk - 1, 0), nblk, jnp.minimum(nblk + 1, nb - 1), 0]
        for kh in range(ATT_KV_HEADS):
            cs = slice(kh * hd, (kh + 1) * hd)
            kk = jnp.concatenate([r[:, cs] for r in (kp, ko, kn, km)], axis=0)
            vv = jnp.concatenate([r[:, cs] for r in (vp, vo, vn, vm)], axis=0)
            heads = [kh * ATT_GROUP + g for g in range(ATT_GROUP)]
            q4 = jnp.concatenate([q_ref[:, h * hd:(h + 1) * hd] for h in heads], axis=0)
            do4 = jnp.concatenate([do_ref[:, h * hd:(h + 1) * hd] for h in heads], axis=0)
            s = lax.dot_general(q4, kk, (((1,), (1,)), ((), ())), preferred_element_type=F32) * ATT_SCALE
            dp = lax.dot_general(do4, vv, (((1,), (1,)), ((), ())), preferred_element_type=F32)
            ps, dss = [], []
            for g, h in enumerate(heads):
                p, psink = _att_probs(s[g * BLOCK:(g + 1) * BLOCK], ok, sink_ref[h:h + 1, 0:1])
                dpg = dp[g * BLOCK:(g + 1) * BLOCK]
                delta = jnp.sum(p * dpg, axis=-1, keepdims=True)
                ps.append(p)
                dss.append(p * (dpg - delta) * ATT_SCALE)
                dsink_ref[h:h + 1, :] = dsink_ref[h:h + 1, :] - jnp.sum(psink * delta, axis=0, keepdims=True)
            ds = jnp.concatenate(dss, axis=0).astype(BF16)
            pb = jnp.concatenate(ps, axis=0).astype(BF16)
            dq = jnp.dot(ds, kk, preferred_element_type=F32)
            for g, h in enumerate(heads):
                dq_ref[:, h * hd:(h + 1) * hd] = dq[g * BLOCK:(g + 1) * BLOCK].astype(dq_ref.dtype)
            dk = lax.dot_general(ds, q4, (((0,), (0,)), ((), ())), preferred_element_type=F32)
            dv = lax.dot_general(pb, do4, (((0,), (0,)), ((), ())), preferred_element_type=F32)
            for seg, r in enumerate(rows):
                at = (pl.ds(pl.multiple_of(r * BLOCK, BLOCK), BLOCK), cs)
                dk_ref[at] += dk[seg * BLOCK:(seg + 1) * BLOCK]
                dv_ref[at] += dv[seg * BLOCK:(seg + 1) * BLOCK]

    qspec = pl.BlockSpec((BLOCK, ATT_WIDTH), lambda i: (i, 0))
    whole = pl.BlockSpec((n, KV_WIDTH), lambda i: (0, 0))
    sinks = pl.BlockSpec((ATT_HEADS, 128), lambda i: (0, 0))
    return pl.pallas_call(body, grid=(nb,), in_specs=[qspec] + _att_specs(nb, COL_AV256) + [sinks, qspec], out_specs=(qspec, whole, whole, sinks),
                          out_shape=(_sds((n, ATT_WIDTH), BF16), _sds((n, KV_WIDTH), F32), _sds((n, KV_WIDTH), F32), _sds((ATT_HEADS, 128), F32)),
                          name=name, compiler_params=pltpu.CompilerParams(dimension_semantics=("arbitrary",)))(
                              q, k, k, k, k, proj, proj, proj, proj, sink_b, dmixed)


def _ret_decay(lg, d):
    a = lax.broadcasted_iota(jnp.int32, (BLOCK, 1), 0)
    b = lax.broadcasted_iota(jnp.int32, (1, BLOCK), 1)
    t_col = a + d * (BLOCK - 1 - 2 * a)
    t_row = b + d * (BLOCK - 1 - 2 * b)
    diff = t_col - t_row
    dist = jnp.maximum(diff, 0).astype(F32)
    dmask = jnp.where(diff >= d, jnp.exp(lg * dist), 0.0)
    tf = t_col.astype(F32)
    xi = jnp.exp(lg * (tf + 1.0))
    zeta = jnp.exp(lg * (BLOCK - 1.0 - tf))
    gam = jnp.exp(jnp.full((1, 1), BLOCK, F32) * lg)
    return dmask, dist, xi, zeta, gam, tf


def _ret_fwd(q, k, proj, lg, name):
    n = q.shape[0]
    nc = n // BLOCK
    hd = RET_HEAD_DIM
    chunk = lambda d, c: c + d * (nc - 1 - 2 * c)

    def body(lg_ref, q_ref, k_ref, v_ref, o_ref, st_ref, s_ref):
        h, d, c = pl.program_id(0), pl.program_id(1), pl.program_id(2)

        @pl.when(c == 0)
        def _():
            s_ref[...] = jnp.zeros_like(s_ref)

        dmask, _, xi, zeta, gam, _ = _ret_decay(lg_ref[h, d], d)
        qv = q_ref[...]
        kf = k_ref[...].astype(F32) * RET_K_SCALE
        vv = v_ref[...]
        s = lax.dot_general(qv, kf.astype(BF16), (((1,), (1,)), ((), ())), preferred_element_type=F32)
        sb = s_ref[...]
        o_ref[...] = (jnp.dot((s * dmask).astype(BF16), vv, preferred_element_type=F32)
                      + jnp.dot((qv.astype(F32) * xi).astype(BF16), sb.astype(BF16), preferred_element_type=F32))
        st_ref[...] = sb
        s_ref[...] = gam * sb + lax.dot_general((kf * zeta).astype(BF16), vv, (((0,), (0,)), ((), ())), preferred_element_type=F32)

    blk = lambda col: pl.BlockSpec((BLOCK, hd), lambda h, d, c: (chunk(d, c), col + h))
    return pl.pallas_call(
        body, grid=(RET_HEADS, 2, nc), in_specs=[pl.BlockSpec(memory_space=pltpu.SMEM), blk(0), blk(0), blk(COL_RV)],
        out_specs=(pl.BlockSpec((None, BLOCK, hd), lambda h, d, c: (d, chunk(d, c), h)),
                   pl.BlockSpec((None, None, None, hd, hd), lambda h, d, c: (h, d, c, 0, 0))),
        out_shape=(_sds((2, n, RET_WIDTH), F32), _sds((RET_HEADS, 2, nc, hd, hd), F32)), scratch_shapes=[pltpu.VMEM((hd, hd), F32)], name=name,
        compiler_params=pltpu.CompilerParams(dimension_semantics=("parallel", "parallel", "arbitrary")))(lg, q, k, proj)


def _ret_bwd(q, k, proj, lg, do, states, name):
    n = q.shape[0]
    nc = n // BLOCK
    hd = RET_HEAD_DIM
    chunk = lambda d, r: (nc - 1 - r) + d * (2 * r - (nc - 1))

    def body(lg_ref, q_ref, k_ref, v_ref, do_ref, st_ref, dq_ref, dk_ref, dv_ref, dlg_ref, ds_ref):
        h, d, r = pl.program_id(0), pl.program_id(1), pl.program_id(2)

        @pl.when(r == 0)
        def _():
            ds_ref[...] = jnp.zeros_like(ds_ref)
            dlg_ref[...] = jnp.zeros_like(dlg_ref)

        dmask, dist, xi, zeta, gam, tf = _ret_decay(lg_ref[h, d], d)
        qv, vv, dov = q_ref[...], v_ref[...], do_ref[...]
        qf = qv.astype(F32)
        kf = k_ref[...].astype(F32) * RET_K_SCALE
        kb = kf.astype(BF16)
        sc = st_ref[...]
        dsn = ds_ref[...]
        nt = (((1,), (1,)), ((), ()))
        tn = (((0,), (0,)), ((), ()))
        s = lax.dot_general(qv, kb, nt, preferred_element_type=F32)
        dsc = lax.dot_general(dov, vv, nt, preferred_element_type=F32) * dmask
        dsb = dsc.astype(BF16)
        dq_c = xi * lax.dot_general(dov, sc.astype(BF16), nt, preferred_element_type=F32)
        dk_c = zeta * lax.dot_general(vv, dsn.astype(BF16), nt, preferred_element_type=F32)
        dq = jnp.dot(dsb, kb, preferred_element_type=F32) + dq_c
        dk = lax.dot_general(dsb, qv, tn, preferred_element_type=F32) + dk_c
        dv = (lax.dot_general((s * dmask).astype(BF16), dov, tn, preferred_element_type=F32)
              + jnp.dot((kf * zeta).astype(BF16), dsn.astype(BF16), preferred_element_type=F32))
        ds_ref[...] = gam * dsn + lax.dot_general((qf * xi).astype(BF16), dov, tn, preferred_element_type=F32)
        dlg = (jnp.sum(dsc * s * dist, keepdims=True)
               + jnp.sum((tf + 1.0) * jnp.sum(qf * dq_c, axis=-1, keepdims=True), keepdims=True)
               + jnp.sum((BLOCK - 1.0 - tf) * jnp.sum(kf * dk_c, axis=-1, keepdims=True), keepdims=True)
               + BLOCK * gam * jnp.sum(dsn * sc, keepdims=True))
        dlg_ref[...] += dlg
        row = lax.broadcasted_iota(jnp.int32, (BLOCK, 1), 0) + chunk(d, r) * BLOCK
        keep = row >= PAD_FRONT
        dq_ref[...] = dq
        dk_ref[...] = jnp.where(keep, dk * RET_K_SCALE, 0.0)
        dv_ref[...] = jnp.where(keep, dv, 0.0)

    blk = lambda col: pl.BlockSpec((BLOCK, hd), lambda h, d, r: (chunk(d, r), col + h))
    plane = pl.BlockSpec((None, BLOCK, hd), lambda h, d, r: (d, chunk(d, r), h))
    return pl.pallas_call(
        body, grid=(RET_HEADS, 2, nc),
        in_specs=[pl.BlockSpec(memory_space=pltpu.SMEM), blk(0), blk(0), blk(COL_RV), blk(0),
                  pl.BlockSpec((None, None, None, hd, hd), lambda h, d, r: (h, d, nc - 1 - r, 0, 0))],
        out_specs=(plane, plane, plane, pl.BlockSpec((None, None, 8, 128), lambda h, d, r: (h, d, 0, 0))),
        out_shape=(_sds((2, n, RET_WIDTH), F32),) * 3 + (_sds((RET_HEADS, 2, 8, 128), F32),), scratch_shapes=[pltpu.VMEM((hd, hd), F32)], name=name,
        compiler_params=pltpu.CompilerParams(dimension_semantics=("parallel", "parallel", "arbitrary")))(lg, q, k, proj, do, states)


def _retgate_fwd(o, proj, gain, name):
    _, n, _ = o.shape
    tr = _row_tile(n, 1056)
    hd = RET_HEAD_DIM

    def body(o_ref, rg_ref, g_ref, y_ref):
        ov = o_ref[0] + o_ref[1]
        r = lax.rsqrt(jnp.mean(ov * ov, axis=-1, keepdims=True) + EPS)
        y_ref[...] = (_silu(rg_ref[...].astype(F32)) * (ov * r * g_ref[...])).astype(y_ref.dtype)

    return pl.pallas_call(body, grid=(n // tr, RET_HEADS),
                          in_specs=[pl.BlockSpec((2, tr, hd), lambda i, h: (0, i, h)), pl.BlockSpec((tr, hd), lambda i, h: (i, COL_RG + h)),
                                    pl.BlockSpec((1, hd), lambda i, h: (0, h))],
                          out_specs=pl.BlockSpec((tr, hd), lambda i, h: (i, h)), out_shape=_sds((n, RET_WIDTH), BF16), name=name)(o, proj, gain)


def _retgate_bwd(dmixed, o, proj, gain, name):
    _, n, _ = o.shape
    tr = _row_tile(n, 1056)
    hd = RET_HEAD_DIM

    def body(dy_ref, o_ref, rg_ref, g_ref, do_ref, drg_ref, dg_ref):
        i = pl.program_id(1)
        ov = o_ref[0] + o_ref[1]
        r = lax.rsqrt(jnp.mean(ov * ov, axis=-1, keepdims=True) + EPS)
        xhat = ov * r
        rg = rg_ref[...].astype(F32)
        dy = dy_ref[...].astype(F32)
        drg_ref[...] = (dy * (xhat * g_ref[...]) * _dsilu(rg)).astype(drg_ref.dtype)
        dn = dy * _silu(rg)
        dxh = dn * g_ref[...]
        do_ref[...] = (r * (dxh - xhat * jnp.mean(dxh * xhat, axis=-1, keepdims=True))).astype(do_ref.dtype)

        @pl.when(i == 0)
        def _():
            dg_ref[...] = jnp.zeros_like(dg_ref)

        dg_ref[...] += jnp.sum(dn * xhat, axis=0, keepdims=True)

    tile = pl.BlockSpec((tr, hd), lambda h, i: (i, h))
    vec = pl.BlockSpec((1, hd), lambda h, i: (0, h))
    return pl.pallas_call(body, grid=(RET_HEADS, n // tr),
                          in_specs=[pl.BlockSpec((tr, hd), lambda h, i: (i, RET_HEADS + h)), pl.BlockSpec((2, tr, hd), lambda h, i: (0, i, h)),
                                    pl.BlockSpec((tr, hd), lambda h, i: (i, COL_RG + h)), vec],
                          out_specs=(tile, tile, vec), out_shape=(_sds((n, RET_WIDTH), BF16), _sds((n, RET_WIDTH), BF16), _sds((1, RET_WIDTH), F32)),
                          name=name, compiler_params=pltpu.CompilerParams(dimension_semantics=("parallel", "arbitrary")))(dmixed, o, proj, gain)


def _swiglu_fwd(gate, up, name):
    n, f = gate.shape
    tr, tc = _row_tile(n, 384), f // 2

    def body(g_ref, u_ref, o_ref):
        o_ref[...] = (_silu(g_ref[...].astype(F32)) * u_ref[...].astype(F32)).astype(o_ref.dtype)

    t = pl.BlockSpec((tr, tc), lambda i, j: (i, j))
    return pl.pallas_call(body, grid=(n // tr, 2), in_specs=[t, t], out_specs=t, out_shape=_sds((n, f), BF16), name=name)(gate, up)


def _swiglu_bwd(df, gate, up, name):
    n, f = gate.shape
    tr, tc = _row_tile(n, 384), f // 2

    def body(d_ref, g_ref, u_ref, dg_ref, du_ref):
        d, g = d_ref[...].astype(F32), g_ref[...].astype(F32)
        dg_ref[...] = (d * u_ref[...].astype(F32) * _dsilu(g)).astype(dg_ref.dtype)
        du_ref[...] = (d * _silu(g)).astype(du_ref.dtype)

    t = pl.BlockSpec((tr, tc), lambda i, j: (i, j))
    return pl.pallas_call(body, grid=(n // tr, 2), in_specs=[t, t, t], out_specs=(t, t), out_shape=(_sds((n, f), BF16),) * 2, name=name)(df, gate, up)


def _loss_head(h, target, name):
    n, d = h.shape
    nb = n // BLOCK

    def body(h_ref, t_ref, dh_ref, l_ref):
        i = pl.program_id(0)

        @pl.when(i == 0)
        def _():
            l_ref[...] = jnp.zeros_like(l_ref)
            dh_ref[...] = jnp.zeros_like(dh_ref)

        @pl.when(i > 0)
        def _():
            e = h_ref[...] - t_ref[...]
            dh_ref[...] = e * (1.0 / d)
            l_ref[...] += 0.5 * jnp.sum(jnp.mean(e * e, axis=-1, keepdims=True), keepdims=True)

    blk = pl.BlockSpec((BLOCK, d), lambda i: (i, 0))
    return pl.pallas_call(body, grid=(nb,), in_specs=[blk, pl.BlockSpec((BLOCK, d), lambda i: (jnp.maximum(i - 1, 0), 0))],
                          out_specs=(blk, pl.BlockSpec((8, 128), lambda i: (0, 0))), out_shape=(_sds((n, d), F32), _sds((8, 128), F32)), name=name,
                          compiler_params=pltpu.CompilerParams(dimension_semantics=("arbitrary",)))(h, target)


def _adamw(parts, w, m, v, name):
    s, r, c = parts.shape
    tr = _row_tile(r, max(16, (ADAMW_TILE_ELEMS // c) // 16 * 16))
    b1c, b2c = 1.0 - ADAM_B1 ** ADAM_STEP, 1.0 - ADAM_B2 ** ADAM_STEP

    def body(p_ref, w_ref, m_ref, v_ref, g_ref, d_ref, mo_ref, vo_ref):
        g = p_ref[0].astype(F32)
        for q in range(1, s):
            g = g + p_ref[q].astype(F32)
        mn = ADAM_B1 * m_ref[...] + (1.0 - ADAM_B1) * g
        vn = ADAM_B2 * v_ref[...] + (1.0 - ADAM_B2) * jnp.square(g)
        g_ref[...] = g
        mo_ref[...] = mn
        vo_ref[...] = vn
        d_ref[...] = -ADAM_LR * ((mn / b1c) / (jnp.sqrt(vn / b2c) + ADAM_EPS) + ADAM_WD * w_ref[...])

    t = pl.BlockSpec((tr, c), lambda i: (i, 0))
    return pl.pallas_call(body, grid=(r // tr,), in_specs=[pl.BlockSpec((s, tr, c), lambda i: (0, i, 0)), t, t, t], out_specs=(t, t, t, t),
                          out_shape=(_sds((r, c), F32),) * 4, name=name)(parts, w, m, v)


def _allgather(xs, name):
    na = len(xs)

    def body(*refs):
        x_refs, o_refs = refs[:na], refs[na:2 * na]
        send, recv, lsem = refs[2 * na:]
        x, y, c = lax.axis_index("x"), lax.axis_index("y"), lax.axis_index("c")
        me, sib = (x, y, c), (x, y, 1 - c)
        chips = [(1 - x, y), (x, 1 - y), (1 - x, 1 - y)]
        slot = lambda p: 4 * p[0] + 2 * p[1] + p[2]

        def copy(a, k, block, to, src=None):
            dst = o_refs[a].at[slot(block)]
            return pltpu.make_async_remote_copy(src_ref=dst if src is None else src, dst_ref=dst, send_sem=send.at[a, k], recv_sem=recv.at[a, k],
                                                device_id=to, device_id_type=MESH)

        mine = [pltpu.make_async_copy(x_refs[a], o_refs[a].at[slot(me)], lsem.at[a]) for a in range(na)]
        for cp in mine:
            cp.start()
        first = []
        for a in range(na):
            first.append(copy(a, 0, me, sib, src=x_refs[a]))
            first += [copy(a, 1 + j, me, (*chip, c), src=x_refs[a]) for j, chip in enumerate(chips)]
        for cp in first:
            cp.start()
        passed = []
        for j, chip in enumerate(chips):
            for a in range(na):
                copy(a, 1 + j, (*chip, c), me).wait_recv()
                passed.append(copy(a, 4 + j, (*chip, c), sib))
                passed[-1].start()
        for a in range(na):
            copy(a, 0, sib, me).wait_recv()
            for j, chip in enumerate(chips):
                copy(a, 4 + j, (*chip, 1 - c), me).wait_recv()
        for cp in first + passed:
            cp.wait_send()
        for cp in mine:
            cp.wait()

    return pl.pallas_call(body, in_specs=[ANY] * na, out_specs=[ANY] * na, out_shape=[_sds((N_DEV,) + t.shape, t.dtype) for t in xs],
                          scratch_shapes=[pltpu.SemaphoreType.DMA((na, 7)), pltpu.SemaphoreType.DMA((na, 7)), pltpu.SemaphoreType.DMA((na,))],
                          name=name)(*xs)


def _exchange(groups, name):
    flat = [(a, l, t) for a, g in enumerate(groups) for l, t in enumerate(g)]
    nf, ng = len(flat), len(groups)

    def body(*refs):
        src, out = refs[:nf], refs[nf:nf + ng]
        send, recv, lsem = refs[nf + ng:]
        x, y, c = lax.axis_index("x"), lax.axis_index("y"), lax.axis_index("c")
        me = 4 * x + 2 * y + c
        copies = []
        for i, (a, l, _) in enumerate(flat):
            cp = pltpu.make_async_copy(src[i].at[me], out[a].at[me, l], lsem.at[i])
            cp.start()
            copies.append(cp)
            for r in range(1, N_DEV):
                px, py, pc = (1 - x if r & 4 else x), (1 - y if r & 2 else y), (1 - c if r & 1 else c)
                peer = 4 * px + 2 * py + pc
                cp = pltpu.make_async_remote_copy(src_ref=src[i].at[peer], dst_ref=out[a].at[me, l], send_sem=send.at[i, r - 1],
                                                  recv_sem=recv.at[i, r - 1], device_id=(px, py, pc), device_id_type=MESH)
                cp.start()
                copies.append(pltpu.make_async_remote_copy(src_ref=src[i].at[peer], dst_ref=out[a].at[peer, l], send_sem=send.at[i, r - 1],
                                                           recv_sem=recv.at[i, r - 1], device_id=(px, py, pc), device_id_type=MESH))
        for cp in copies:
            cp.wait()

    return pl.pallas_call(body, in_specs=[ANY] * nf, out_specs=[ANY] * ng,
                          out_shape=[_sds((N_DEV, len(g)) + g[0].shape[1:], g[0].dtype) for g in groups],
                          scratch_shapes=[pltpu.SemaphoreType.DMA((nf, 7)), pltpu.SemaphoreType.DMA((nf, 7)), pltpu.SemaphoreType.DMA((nf,))],
                          name=name)(*[t for _, _, t in flat])


def _local_step(x, meta, target, wi, wo, wg, wu, wd, sink, dec_f, dec_b, ret_norm, n_mix_pre, n_mix_post, n_ffn_pre, n_ffn_post):
    depth = len(wi)
    d = D_MODEL
    h = jnp.concatenate([jnp.zeros((PAD_FRONT, d), F32), meta, x], axis=0)
    n = h.shape[0]
    cos_a, sin_a, cos_r, sin_r, perm = _rope_tables(n)
    lg_all = jnp.stack([-jnp.exp(dec_f), -jnp.exp(dec_b)], axis=-1)
    saved = []
    for l in range(depth):
        t = f"l{l}_"
        sink_b = jnp.broadcast_to(sink[l][:, None], (ATT_HEADS, 128))
        u = _norm_fwd(h, n_mix_pre[l][None], None, BF16, t + "norm_mix_pre")
        proj = _mm_nn(u, wi[l], BF16, WIDE_TILE, d, t + "proj")
        aq = _rope_att(proj, 0, ATT_HEADS, cos_a, sin_a, perm, t + "rope_aq")
        ak = _rope_att(proj, COL_AK, ATT_KV_HEADS, cos_a, sin_a, perm, t + "rope_ak")
        att = _att_fwd(aq, ak, proj, sink_b, t + "att")
        proj3 = proj[None]
        rq = _rope_ret(proj3, COL_RQ, cos_r, sin_r, t + "rope_rq")
        rk = _rope_ret(proj3, COL_RK, cos_r, sin_r, t + "rope_rk")
        o_ret, states = _ret_fwd(rq, rk, proj, lg_all[l], t + "ret")
        retg = _retgate_fwd(o_ret, proj, ret_norm[l][None], t + "retgate")
        mixed = jnp.concatenate([att, retg], axis=1)
        mo = _mm_nn(mixed, wo[l], F32, 1024, d, t + "out_proj")
        h_mid = _norm_fwd(mo, n_mix_post[l][None], h, F32, t + "norm_mix_post")
        u2 = _norm_fwd(h_mid, n_ffn_pre[l][None], None, BF16, t + "norm_ffn_pre")
        gate = _mm_nn(u2, wg[l], BF16, WIDE_TILE, d, t + "gate")
        up = _mm_nn(u2, wu[l], BF16, WIDE_TILE, d, t + "up")
        f = _swiglu_fwd(gate, up, t + "swiglu")
        dn = _mm_nn(f, wd[l], F32, 1024, WIDE_TILE, t + "down")
        h_out = _norm_fwd(dn, n_ffn_post[l][None], h_mid, F32, t + "norm_ffn_post")
        saved.append(dict(h=h, u=u, proj=proj, aq=aq, ak=ak, rq=rq, rk=rk, o_ret=o_ret, states=states, mixed=mixed, mo=mo, h_mid=h_mid, u2=u2,
                          gate=gate, up=up, f=f, dn=dn, sink_b=sink_b))
        h = h_out

    dh, loss_part = _loss_head(h, target, "loss_head")
    gw = dict(wi=[None] * depth, wo=[None] * depth, wg=[None] * depth, wu=[None] * depth, wd=[None] * depth)
    gs = dict(sink=[None] * depth, dec_f=[None] * depth, dec_b=[None] * depth, ret_norm=[None] * depth, mix_pre=[None] * depth,
              mix_post=[None] * depth, ffn_pre=[None] * depth, ffn_post=[None] * depth)
    for l in reversed(range(depth)):
        t = f"l{l}_b_"
        sv = saved[l]
        proj = sv["proj"]
        d_dn, gs["ffn_post"][l] = _norm_bwd(dh, sv["dn"], n_ffn_post[l][None], None, BF16, t + "norm_ffn_post")
        d_f = _mm_nt(d_dn, wd[l], BF16, WIDE_TILE, d, t + "d_f")
        gw["wd"][l] = _mm_tn(sv["f"], d_dn, WIDE_TILE, 1024, t + "dw_down")
        d_gate, d_up = _swiglu_bwd(d_f, sv["gate"], sv["up"], t + "swiglu")
        du2 = _mm_nt(d_gate, wg[l], F32, 1024, WIDE_TILE, t + "du2_gate")
        du2 = _mm_nt(d_up, wu[l], F32, 1024, WIDE_TILE, t + "du2_up", acc=du2)
        gw["wg"][l] = _mm_tn(sv["u2"], d_gate, 1024, WIDE_TILE, t + "dw_gate")
        gw["wu"][l] = _mm_tn(sv["u2"], d_up, 1024, WIDE_TILE, t + "dw_up")
        dh, gs["ffn_pre"][l] = _norm_bwd(du2, sv["h_mid"], n_ffn_pre[l][None], dh, F32, t + "norm_ffn_pre")
        d_mo, gs["mix_post"][l] = _norm_bwd(dh, sv["mo"], n_mix_post[l][None], None, BF16, t + "norm_mix_post")
        d_mixed = _mm_nt(d_mo, wo[l], BF16, 1024, d, t + "d_mixed")
        gw["wo"][l] = _mm_tn(sv["mixed"], d_mo, 1024, 1024, t + "dw_out")
        d_o, d_rg, gs["ret_norm"][l] = _retgate_bwd(d_mixed, sv["o_ret"], proj, ret_norm[l][None], t + "retgate")
        dq_r, dk_r, dv_r, dlg = _ret_bwd(sv["rq"], sv["rk"], proj, lg_all[l], d_o, sv["states"], t + "ret")
        draw = dlg[:, :, 0, 0] * lg_all[l]
        gs["dec_f"][l], gs["dec_b"][l] = draw[:, 0], draw[:, 1]
        dq_a, dk_a, dv_a, dsink = _att_bwd(sv["aq"], sv["ak"], proj, sv["sink_b"], d_mixed, t + "att")
        gs["sink"][l] = dsink[:, 0]
        dproj = jnp.concatenate([
            _rope_att(dq_a, 0, ATT_HEADS, cos_a, -sin_a, perm, t + "rope_aq"),
            _rope_att(dk_a, 0, ATT_KV_HEADS, cos_a, -sin_a, perm, t + "rope_ak"),
            dv_a.astype(BF16),
            _rope_ret(dq_r, 0, cos_r, -sin_r, t + "rope_rq"),
            _rope_ret(dk_r, 0, cos_r, -sin_r, t + "rope_rk"),
            _rope_ret(dv_r, 0, None, None, t + "sum_rv"),
            d_rg], axis=1)
        du = _mm_nt(dproj, wi[l], F32, 1024, WIDE_TILE, t + "du")
        gw["wi"][l] = _mm_tn(sv["u"], dproj, 1024, WIDE_TILE, t + "dw_in")
        dh, gs["mix_pre"][l] = _norm_bwd(du, sv["h"], n_mix_pre[l][None], dh, F32, t + "norm_mix_pre")
    return loss_part[0, 0], dh, gw, gs


def _cols_from_blocks(g):
    return g.transpose(1, 0, 2).reshape(g.shape[1], N_DEV * SHARD_COLS)


def _blocks_from_cols(w):
    return w.reshape(w.shape[0], N_DEV, SHARD_COLS).transpose(1, 0, 2)


def _pack_small(mix_pre, mix_post, ffn_pre, ffn_post, ret_norm, sink, dec_f, dec_b, loss, meta):
    d = D_MODEL
    misc = jnp.zeros((d,), F32)
    misc = lax.dynamic_update_slice(misc, sink.reshape(-1), (MISC_SINK,))
    misc = lax.dynamic_update_slice(misc, dec_f.reshape(-1), (MISC_DF,))
    misc = lax.dynamic_update_slice(misc, dec_b.reshape(-1), (MISC_DB,))
    misc = lax.dynamic_update_slice(misc, jnp.reshape(loss, (1,)), (MISC_LOSS,))
    return jnp.concatenate([mix_pre, mix_post, ffn_pre, ffn_post, ret_norm.reshape(ROW_MISC - ROW_RET_NORM, d), misc[None],
                            jnp.zeros((ROW_META - ROW_MISC - 1, d), F32), meta], axis=0)


def _unpack_small(p, depth):
    misc = p[ROW_MISC]
    return dict(mix_pre=p[ROW_MIX_PRE:ROW_MIX_PRE + depth], mix_post=p[ROW_MIX_POST:ROW_MIX_POST + depth], ffn_pre=p[ROW_FFN_PRE:ROW_FFN_PRE + depth],
                ffn_post=p[ROW_FFN_POST:ROW_FFN_POST + depth], ret_norm=p[ROW_RET_NORM:ROW_MISC].reshape(depth, RET_WIDTH),
                sink=misc[MISC_SINK:MISC_SINK + depth * ATT_HEADS].reshape(depth, ATT_HEADS),
                dec_f=misc[MISC_DF:MISC_DF + depth * RET_HEADS].reshape(depth, RET_HEADS),
                dec_b=misc[MISC_DB:MISC_DB + depth * RET_HEADS].reshape(depth, RET_HEADS), loss=misc[MISC_LOSS])


def kernel(x, meta_tokens, w_in, w_out, attn_sink, ret_decay_fwd, ret_decay_bwd, ret_norm, norm_mix_pre, norm_mix_post, w_gate, w_up, w_down, norm_ffn_pre, norm_ffn_post, loss_target, m_meta_tokens, m_w_in, m_w_out, m_attn_sink, m_ret_decay_fwd, m_ret_decay_bwd, m_ret_norm, m_norm_mix_pre, m_norm_mix_post, m_w_gate, m_w_up, m_w_down, m_norm_ffn_pre, m_norm_ffn_post, v_meta_tokens, v_w_in, v_w_out, v_attn_sink, v_ret_decay_fwd, v_ret_decay_bwd, v_ret_norm, v_norm_mix_pre, v_norm_mix_post, v_w_gate, v_w_up, v_w_down, v_norm_ffn_pre, v_norm_ffn_post):
    depth, d = w_in.shape[0], D_MODEL
    me = 4 * lax.axis_index("x") + 2 * lax.axis_index("y") + lax.axis_index("c")
    zero = jnp.zeros((), F32)

    meta_g, = _allgather([meta_tokens], "gather_meta")
    meta = meta_g.transpose(1, 0, 2).reshape(N_META, d)
    wi, wo, wg, wu, wd = [], [], [], [], []
    for l in range(depth):
        g = _allgather([w_in[l].astype(BF16), w_gate[l].astype(BF16), w_up[l].astype(BF16), w_down[l].astype(BF16), w_out[l].astype(BF16)],
                       f"gather_weights_l{l}")
        wi.append(_cols_from_blocks(g[0]))
        wg.append(_cols_from_blocks(g[1]))
        wu.append(_cols_from_blocks(g[2]))
        wd.append(g[3].reshape(D_FF, d))
        wo.append(g[4].reshape(d, d))

    loss_part, dh, gw, gs = _local_step(x[0], meta, loss_target[0], wi, wo, wg, wu, wd, attn_sink, ret_decay_fwd, ret_decay_bwd, ret_norm,
                                        norm_mix_pre, norm_mix_post, norm_ffn_pre, norm_ffn_post)
    grad_x = dh[BLOCK:][None]

    r_wi, r_wg, r_wu, r_wd, r_wo = _exchange([
        [_blocks_from_cols(t) for t in gw["wi"]], [_blocks_from_cols(t) for t in gw["wg"]],
        [_blocks_from_cols(t) for t in gw["wu"]], [t.reshape(N_DEV, SHARD_COLS, d) for t in gw["wd"]],
        [t.reshape(N_DEV, d // N_DEV, d) for t in gw["wo"]]], "exchange_grads")
    st = lambda xs: jnp.stack([t.reshape(-1) if t.ndim == 1 else t[0] for t in xs])
    small = _pack_small(st(gs["mix_pre"]), st(gs["mix_post"]), st(gs["ffn_pre"]), st(gs["ffn_post"]), st(gs["ret_norm"]), st(gs["sink"]),
                        st(gs["dec_f"]), st(gs["dec_b"]), loss_part, dh[PAD_FRONT:BLOCK])
    small_g, = _allgather([small], "gather_small")

    def big(parts, w, m, v, name):
        shp = w.shape
        flat = lambda t: t.reshape(-1, shp[-1])
        outs = _adamw(parts.reshape(N_DEV, -1, shp[-1]), flat(w), flat(m), flat(v), name)
        return [o.reshape(shp) for o in outs]

    o_wi = big(r_wi, w_in, m_w_in, v_w_in, "adamw_w_in")
    o_wo = big(r_wo, w_out, m_w_out, v_w_out, "adamw_w_out")
    o_wg = big(r_wg, w_gate, m_w_gate, v_w_gate, "adamw_w_gate")
    o_wu = big(r_wu, w_up, m_w_up, v_w_up, "adamw_w_up")
    o_wd = big(r_wd, w_down, m_w_down, v_w_down, "adamw_w_down")
    zmeta = jnp.zeros((N_META, d), F32)
    packs = [_pack_small(a[0], a[1], a[2], a[3], a[4], a[5], a[6], a[7], zero, zmeta) for a in (
        (norm_mix_pre, norm_mix_post, norm_ffn_pre, norm_ffn_post, ret_norm, attn_sink, ret_decay_fwd, ret_decay_bwd),
        (m_norm_mix_pre, m_norm_mix_post, m_norm_ffn_pre, m_norm_ffn_post, m_ret_norm, m_attn_sink, m_ret_decay_fwd, m_ret_decay_bwd),
        (v_norm_mix_pre, v_norm_mix_post, v_norm_ffn_pre, v_norm_ffn_post, v_ret_norm, v_attn_sink, v_ret_decay_fwd, v_ret_decay_bwd))]
    o_small = [_unpack_small(o, depth) for o in _adamw(small_g, packs[0], packs[1], packs[2], "adamw_small")]
    meta_parts = lax.dynamic_slice(small_g, (0, ROW_META, me * (d // N_DEV)), (N_DEV, N_META, d // N_DEV))
    o_meta = _adamw(meta_parts, meta_tokens, m_meta_tokens, v_meta_tokens, "adamw_meta")

    outs = []
    for i in range(4):
        s = o_small[i]
        outs += [o_meta[i], o_wi[i], o_wo[i], s["sink"], s["dec_f"], s["dec_b"], s["ret_norm"], s["mix_pre"], s["mix_post"], o_wg[i], o_wu[i],
                 o_wd[i], s["ffn_pre"], s["ffn_post"]]
    return (o_small[0]["loss"], grad_x, *outs)
```

```python
import jax
import jax.numpy as jnp
import numpy as np
from jax import lax
from jax.experimental import pallas as pl
from jax.experimental.pallas import tpu as pltpu

F32, BF16 = jnp.float32, jnp.bfloat16

D_MODEL = 2048
N_META = 16
BLOCK = 128
WINDOW = 128
PAD_FRONT = BLOCK - N_META
ATT_HEAD_DIM = 128
ATT_WIDTH = D_MODEL // 2
ATT_HEADS = ATT_WIDTH // ATT_HEAD_DIM
ATT_KV_HEADS = 2
ATT_GROUP = ATT_HEADS // ATT_KV_HEADS
KV_WIDTH = ATT_KV_HEADS * ATT_HEAD_DIM
ROT_DIM = ATT_HEAD_DIM // 4
ROPE_THETA = 500000.0
RET_WIDTH = D_MODEL - ATT_WIDTH
RET_HEAD_DIM = 256
RET_HEADS = RET_WIDTH // RET_HEAD_DIM
RET_THETA = 10000.0
D_FF = 5632
IN_COLS = ATT_WIDTH + 2 * KV_WIDTH + 4 * RET_WIDTH
N_DEV = 8
SHARD_COLS = IN_COLS // N_DEV
EPS = 1e-6
NEG = -1e30
RET_K_SCALE = RET_HEAD_DIM ** -0.5
ATT_SCALE = ATT_HEAD_DIM ** -0.5

COL_AK = ATT_WIDTH // ATT_HEAD_DIM
COL_AV256 = (ATT_WIDTH + KV_WIDTH) // 256
COL_RQ = (ATT_WIDTH + 2 * KV_WIDTH) // RET_HEAD_DIM
COL_RK = COL_RQ + RET_HEADS
COL_RV = COL_RK + RET_HEADS
COL_RG = COL_RV + RET_HEADS

ADAM_LR, ADAM_B1, ADAM_B2, ADAM_EPS, ADAM_WD, ADAM_STEP = 0.001, 0.9, 0.999, 1e-08, 0.01, 10

ROW_MIX_PRE, ROW_MIX_POST, ROW_FFN_PRE, ROW_FFN_POST, ROW_RET_NORM, ROW_SINK, ROW_DEC_F, ROW_DEC_B, ROW_LOSS, ROW_META, SMALL_ROWS = (
    0, 8, 16, 24, 32, 40, 48, 56, 64, 72, 96)
ADAMW_TILE_ELEMS = 128 * 1024

MESH = pl.DeviceIdType.MESH
ANY = pl.BlockSpec(memory_space=pl.ANY)


def _row_tile(n, cap):
    for t in range(cap - cap % 16, 0, -16):
        if n % t == 0:
            return t
    raise ValueError(n)


def _sds(shape, dtype):
    return jax.ShapeDtypeStruct(shape, dtype)


def _silu(x):
    return x * jax.nn.sigmoid(x)


def _dsilu(x):
    s = jax.nn.sigmoid(x)
    return s * (1.0 + x * (1.0 - s))


def _norm_fwd(x, g, res, out_dtype, name):
    n, d = x.shape
    tr = _row_tile(n, 384)

    def body(*refs):
        if res is None:
            x_ref, g_ref, o_ref = refs
        else:
            x_ref, g_ref, r_ref, o_ref = refs
        xv = x_ref[...]
        r = lax.rsqrt(jnp.mean(xv * xv, axis=-1, keepdims=True) + EPS)
        y = xv * r * g_ref[...]
        if res is not None:
            y = y + r_ref[...]
        o_ref[...] = y.astype(o_ref.dtype)

    row = pl.BlockSpec((tr, d), lambda i: (i, 0))
    ins = [row, pl.BlockSpec((1, d), lambda i: (0, 0))] + ([row] if res is not None else [])
    args = (x, g) + ((res,) if res is not None else ())
    return pl.pallas_call(body, grid=(n // tr,), in_specs=ins, out_specs=row, out_shape=_sds((n, d), out_dtype), name=name)(*args)


def _norm_bwd(dy, x, g, res, out_dtype, name):
    n, d = x.shape
    tr = _row_tile(n, 384)

    def body(*refs):
        if res is None:
            dy_ref, x_ref, g_ref, dx_ref, dg_ref = refs
        else:
            dy_ref, x_ref, g_ref, r_ref, dx_ref, dg_ref = refs
        i = pl.program_id(0)
        xv = x_ref[...]
        r = lax.rsqrt(jnp.mean(xv * xv, axis=-1, keepdims=True) + EPS)
        xhat = xv * r
        dyf = dy_ref[...].astype(F32)
        gdy = dyf * g_ref[...]
        dx = r * (gdy - xhat * jnp.mean(gdy * xhat, axis=-1, keepdims=True))
        if res is not None:
            dx = dx + r_ref[...]
        dx_ref[...] = dx.astype(dx_ref.dtype)

        @pl.when(i == 0)
        def _():
            dg_ref[...] = jnp.zeros_like(dg_ref)

        dg_ref[...] += jnp.sum(dyf * xhat, axis=0, keepdims=True)

    row = pl.BlockSpec((tr, d), lambda i: (i, 0))
    vec = pl.BlockSpec((1, d), lambda i: (0, 0))
    ins = [row, row, vec] + ([row] if res is not None else [])
    args = (dy, x, g) + ((res,) if res is not None else ())
    return pl.pallas_call(body, grid=(n // tr,), in_specs=ins, out_specs=(row, vec),
                          out_shape=(_sds((n, d), out_dtype), _sds((1, d), F32)), name=name,
                          compiler_params=pltpu.CompilerParams(dimension_semantics=("arbitrary",)))(*args)


def _mm(a, b, *, ta, tb, grid, a_blk, a_map, b_blk, b_map, o_blk, o_map, o_shape, o_dtype, name, acc=None):
    nk = grid[2]
    dims = (((0,) if ta else (1,), (1,) if tb else (0,)), ((), ()))

    def body(*refs):
        if acc is None:
            a_ref, b_ref, o_ref = refs[:3]
            c_ref = None
        else:
            a_ref, b_ref, c_ref, o_ref = refs[:4]
        part = lax.dot_general(a_ref[...], b_ref[...], dims, preferred_element_type=F32)
        if nk == 1:
            if c_ref is not None:
                part = part + c_ref[...].astype(F32)
            o_ref[...] = part.astype(o_ref.dtype)
            return
        acc_ref = refs[-1]
        k = pl.program_id(2)

        @pl.when(k == 0)
        def _():
            acc_ref[...] = jnp.zeros_like(acc_ref) if c_ref is None else c_ref[...].astype(F32)

        acc_ref[...] += part

        @pl.when(k == nk - 1)
        def _():
            o_ref[...] = acc_ref[...].astype(o_ref.dtype)

    ins = [pl.BlockSpec(a_blk, a_map), pl.BlockSpec(b_blk, b_map)]
    args = [a, b]
    if acc is not None:
        ins.append(pl.BlockSpec(o_blk, o_map))
        args.append(acc)
    return pl.pallas_call(body, grid=grid, in_specs=ins, out_specs=pl.BlockSpec(o_blk, o_map), out_shape=_sds(o_shape, o_dtype),
                          scratch_shapes=[pltpu.VMEM(o_blk, F32)] if nk > 1 else [], name=name,
                          compiler_params=pltpu.CompilerParams(dimension_semantics=("parallel", "parallel", "arbitrary")))(*args)


TOKEN_TILE = 1056
WIDE_TILE = 1408


def _mm_nn(x, w, o_dtype, tn, tk, name):
    n, k = x.shape
    tm = _row_tile(n, TOKEN_TILE)
    return _mm(x, w, ta=False, tb=False, grid=(n // tm, w.shape[1] // tn, k // tk), a_blk=(tm, tk), a_map=lambda i, j, kk: (i, kk),
               b_blk=(tk, tn), b_map=lambda i, j, kk: (kk, j), o_blk=(tm, tn), o_map=lambda i, j, kk: (i, j),
               o_shape=(n, w.shape[1]), o_dtype=o_dtype, name=name)


def _mm_nt(dy, w, o_dtype, tn, tk, name, acc=None):
    n, k = dy.shape
    tm = _row_tile(n, TOKEN_TILE)
    return _mm(dy, w, ta=False, tb=True, grid=(n // tm, w.shape[0] // tn, k // tk), a_blk=(tm, tk), a_map=lambda i, j, kk: (i, kk),
               b_blk=(tn, tk), b_map=lambda i, j, kk: (j, kk), o_blk=(tm, tn), o_map=lambda i, j, kk: (i, j),
               o_shape=(n, w.shape[0]), o_dtype=o_dtype, name=name, acc=acc)


def _mm_tn(x, dy, tm, tn, name):
    n, m = x.shape
    tk = _row_tile(n, TOKEN_TILE)
    return _mm(x, dy, ta=True, tb=False, grid=(m // tm, dy.shape[1] // tn, n // tk), a_blk=(tk, tm), a_map=lambda i, j, kk: (kk, i),
               b_blk=(tk, tn), b_map=lambda i, j, kk: (kk, j), o_blk=(tm, tn), o_map=lambda i, j, kk: (i, j),
               o_shape=(m, dy.shape[1]), o_dtype=BF16, name=name)


def _rope_tables(n):
    pos = (jnp.arange(n) - PAD_FRONT).astype(F32)
    half = ROT_DIM // 2
    ang = pos[:, None] * (ROPE_THETA ** (-jnp.arange(half, dtype=F32) / half))[None, :]
    c, s = jnp.cos(ang), jnp.sin(ang)
    rest = ATT_HEAD_DIM - ROT_DIM
    cos_a = jnp.concatenate([c, c, jnp.ones((n, rest), F32)], axis=1)
    sin_a = jnp.concatenate([-s, s, jnp.zeros((n, rest), F32)], axis=1)
    half = RET_HEAD_DIM // 2
    ang = pos[:, None] * (RET_THETA ** (-jnp.arange(half, dtype=F32) / half))[None, :]
    c, s = jnp.cos(ang), jnp.sin(ang)
    perm = np.zeros((ATT_HEAD_DIM, ATT_HEAD_DIM), np.float32)
    for i in range(ROT_DIM):
        perm[(i + ROT_DIM // 2) % ROT_DIM, i] = 1.0
    return cos_a, sin_a, jnp.concatenate([c, c], axis=1), jnp.concatenate([-s, s], axis=1), jnp.asarray(perm, BF16)


def _rope_att(x, col0, heads, cos, sin, perm, name):
    n = x.shape[0]
    tr = _row_tile(n, 1056)

    def body(x_ref, c_ref, s_ref, p_ref, o_ref):
        xb = x_ref[...].astype(BF16)
        sw = jnp.dot(xb, p_ref[...], preferred_element_type=F32)
        o_ref[...] = (xb.astype(F32) * c_ref[...] + sw * s_ref[...]).astype(o_ref.dtype)

    hd = ATT_HEAD_DIM
    tab = pl.BlockSpec((tr, hd), lambda i, h: (i, 0))
    return pl.pallas_call(body, grid=(n // tr, heads),
                          in_specs=[pl.BlockSpec((tr, hd), lambda i, h: (i, col0 + h)), tab, tab, pl.BlockSpec((hd, hd), lambda i, h: (0, 0))],
                          out_specs=pl.BlockSpec((tr, hd), lambda i, h: (i, h)), out_shape=_sds((n, heads * hd), BF16), name=name)(x, cos, sin, perm)


def _rope_ret(x, col0, cos, sin, name):
    p, n, _ = x.shape
    tr = _row_tile(n, 1056)
    hd = RET_HEAD_DIM

    def body(*refs):
        x_ref, o_ref = refs[0], refs[-1]
        xv = x_ref[0].astype(F32)
        for q in range(1, p):
            xv = xv + x_ref[q].astype(F32)
        if cos is not None:
            sw = jnp.concatenate([xv[:, hd // 2:], xv[:, :hd // 2]], axis=1)
            xv = xv * refs[1][...] + sw * refs[2][...]
        o_ref[...] = xv.astype(o_ref.dtype)

    tab = pl.BlockSpec((tr, hd), lambda i, h: (i, 0))
    ins = [pl.BlockSpec((p, tr, hd), lambda i, h: (0, i, col0 + h))] + ([tab, tab] if cos is not None else [])
    args = (x,) + ((cos, sin) if cos is not None else ())
    return pl.pallas_call(body, grid=(n // tr, RET_HEADS), in_specs=ins, out_specs=pl.BlockSpec((tr, hd), lambda i, h: (i, h)),
                          out_shape=_sds((n, RET_WIDTH), BF16), name=name)(*args)


def _att_mask(nblk, n_tot):
    row = lax.broadcasted_iota(jnp.int32, (BLOCK, 4 * BLOCK), 0)
    col = lax.broadcasted_iota(jnp.int32, (BLOCK, 4 * BLOCK), 1)
    qi = nblk * BLOCK + row
    seg = col // BLOCK
    cj = col % BLOCK
    kj = (nblk - 1 + seg) * BLOCK + cj
    band = (jnp.abs(qi - kj) <= WINDOW) & (kj >= PAD_FRONT) & (kj < n_tot) & (seg < 3)
    meta = (seg == 3) & (cj >= PAD_FRONT) & (jnp.abs(qi - cj) > WINDOW)
    return band | meta


def _att_specs(nb, v_col):
    kv = lambda f, cb: pl.BlockSpec((BLOCK, KV_WIDTH), lambda n: (f(n), cb))
    prev, own, nxt, first = (lambda n: jnp.maximum(n - 1, 0)), (lambda n: n), (lambda n: jnp.minimum(n + 1, nb - 1)), (lambda n: 0)
    return [kv(f, 0) for f in (prev, own, nxt, first)] + [kv(f, v_col) for f in (prev, own, nxt, first)]


def _att_probs(s, ok, snk):
    s = jnp.where(ok, s, NEG)
    m = jnp.maximum(jnp.max(s, axis=-1, keepdims=True), snk)
    p = jnp.exp(s - m)
    ps = jnp.exp(snk - m)
    inv = 1.0 / (jnp.sum(p, axis=-1, keepdims=True) + ps)
    return p * inv, ps * inv


def _att_fwd(q, k, proj, sink_b, name):
    n = q.shape[0]
    nb = n // BLOCK
    hd = ATT_HEAD_DIM

    def body(q_ref, kp, ko, kn, km, vp, vo, vn, vm, sink_ref, o_ref):
        nblk = pl.program_id(0)
        ok = _att_mask(nblk, n)
        keep = (nblk * BLOCK + lax.broadcasted_iota(jnp.int32, (BLOCK, 1), 0)) >= PAD_FRONT
        for kh in range(ATT_KV_HEADS):
            cs = slice(kh * hd, (kh + 1) * hd)
            kk = jnp.concatenate([r[:, cs] for r in (kp, ko, kn, km)], axis=0)
            vv = jnp.concatenate([r[:, cs] for r in (vp, vo, vn, vm)], axis=0)
            heads = [kh * ATT_GROUP + g for g in range(ATT_GROUP)]
            q4 = jnp.concatenate([q_ref[:, h * hd:(h + 1) * hd] for h in heads], axis=0)
            s = lax.dot_general(q4, kk, (((1,), (1,)), ((), ())), preferred_element_type=F32) * ATT_SCALE
            ps = []
            for g, h in enumerate(heads):
                p, _ = _att_probs(s[g * BLOCK:(g + 1) * BLOCK], ok, sink_ref[h:h + 1, 0:1])
                ps.append(p)
            o = jnp.dot(jnp.concatenate(ps, axis=0).astype(BF16), vv, preferred_element_type=F32)
            for g, h in enumerate(heads):
                o_ref[:, h * hd:(h + 1) * hd] = jnp.where(keep, o[g * BLOCK:(g + 1) * BLOCK], 0.0).astype(o_ref.dtype)

    qspec = pl.BlockSpec((BLOCK, ATT_WIDTH), lambda i: (i, 0))
    return pl.pallas_call(body, grid=(nb,), in_specs=[qspec] + _att_specs(nb, COL_AV256) + [pl.BlockSpec((ATT_HEADS, 128), lambda i: (0, 0))],
                          out_specs=qspec, out_shape=_sds((n, ATT_WIDTH), BF16), name=name)(q, k, k, k, k, proj, proj, proj, proj, sink_b)


def _att_bwd(q, k, proj, sink_b, dmixed, name):
    n = q.shape[0]
    nb = n // BLOCK
    hd = ATT_HEAD_DIM

    def body(q_ref, kp, ko, kn, km, vp, vo, vn, vm, sink_ref, do_ref, dq_ref, dk_ref, dv_ref, dsink_ref):
        nblk = pl.program_id(0)

        @pl.when(nblk == 0)
        def _():
            dk_ref[...] = jnp.zeros_like(dk_ref)
            dv_ref[...] = jnp.zeros_like(dv_ref)
            dsink_ref[...] = jnp.zeros_like(dsink_ref)

        ok = _att_mask(nblk, n)
        rows = [jnp.maximum(nblk - 1, 0), nblk, jnp.minimum(nblk + 1, nb - 1), 0]
        for kh in range(ATT_KV_HEADS):
            cs = slice(kh * hd, (kh + 1) * hd)
            kk = jnp.concatenate([r[:, cs] for r in (kp, ko, kn, km)], axis=0)
            vv = jnp.concatenate([r[:, cs] for r in (vp, vo, vn, vm)], axis=0)
            heads = [kh * ATT_GROUP + g for g in range(ATT_GROUP)]
            q4 = jnp.concatenate([q_ref[:, h * hd:(h + 1) * hd] for h in heads], axis=0)
            do4 = jnp.concatenate([do_ref[:, h * hd:(h + 1) * hd] for h in heads], axis=0)
            s = lax.dot_general(q4, kk, (((1,), (1,)), ((), ())), preferred_element_type=F32) * ATT_SCALE
            dp = lax.dot_general(do4, vv, (((1,), (1,)), ((), ())), preferred_element_type=F32)
            ps, dss = [], []
            for g, h in enumerate(heads):
                p, psink = _att_probs(s[g * BLOCK:(g + 1) * BLOCK], ok, sink_ref[h:h + 1, 0:1])
                dpg = dp[g * BLOCK:(g + 1) * BLOCK]
                delta = jnp.sum(p * dpg, axis=-1, keepdims=True)
                ps.append(p)
                dss.append(p * (dpg - delta) * ATT_SCALE)
                dsink_ref[h:h + 1, :] = dsink_ref[h:h + 1, :] - jnp.sum(psink * delta, axis=0, keepdims=True)
            ds = jnp.concatenate(dss, axis=0).astype(BF16)
            pb = jnp.concatenate(ps, axis=0).astype(BF16)
            dq = jnp.dot(ds, kk, preferred_element_type=F32)
            for g, h in enumerate(heads):
                dq_ref[:, h * hd:(h + 1) * hd] = dq[g * BLOCK:(g + 1) * BLOCK].astype(dq_ref.dtype)
            dk = lax.dot_general(ds, q4, (((0,), (0,)), ((), ())), preferred_element_type=F32)
            dv = lax.dot_general(pb, do4, (((0,), (0,)), ((), ())), preferred_element_type=F32)
            for seg, r in enumerate(rows):
                at = (pl.ds(pl.multiple_of(r * BLOCK, BLOCK), BLOCK), cs)
                dk_ref[at] += dk[seg * BLOCK:(seg + 1) * BLOCK]
                dv_ref[at] += dv[seg * BLOCK:(seg + 1) * BLOCK]

    qspec = pl.BlockSpec((BLOCK, ATT_WIDTH), lambda i: (i, 0))
    whole = pl.BlockSpec((n, KV_WIDTH), lambda i: (0, 0))
    sinks = pl.BlockSpec((ATT_HEADS, 128), lambda i: (0, 0))
    return pl.pallas_call(body, grid=(nb,), in_specs=[qspec] + _att_specs(nb, COL_AV256) + [sinks, qspec], out_specs=(qspec, whole, whole, sinks),
                          out_shape=(_sds((n, ATT_WIDTH), BF16), _sds((n, KV_WIDTH), F32), _sds((n, KV_WIDTH), F32), _sds((ATT_HEADS, 128), F32)),
                          name=name, compiler_params=pltpu.CompilerParams(dimension_semantics=("arbitrary",)))(
                              q, k, k, k, k, proj, proj, proj, proj, sink_b, dmixed)


def _ret_decay(lg, d):
    a = lax.broadcasted_iota(jnp.int32, (BLOCK, 1), 0)
    b = lax.broadcasted_iota(jnp.int32, (1, BLOCK), 1)
    t_col = a + d * (BLOCK - 1 - 2 * a)
    t_row = b + d * (BLOCK - 1 - 2 * b)
    diff = t_col - t_row
    dist = jnp.maximum(diff, 0).astype(F32)
    dmask = jnp.where(diff >= d, jnp.exp(lg * dist), 0.0)
    tf = t_col.astype(F32)
    xi = jnp.exp(lg * (tf + 1.0))
    zeta = jnp.exp(lg * (BLOCK - 1.0 - tf))
    gam = jnp.exp(jnp.full((1, 1), BLOCK, F32) * lg)
    return dmask, dist, xi, zeta, gam, tf


def _ret_fwd(q, k, proj, lg, name):
    n = q.shape[0]
    nc = n // BLOCK
    hd = RET_HEAD_DIM
    chunk = lambda d, c: c + d * (nc - 1 - 2 * c)

    def body(lg_ref, q_ref, k_ref, v_ref, o_ref, st_ref, s_ref):
        h, d, c = pl.program_id(0), pl.program_id(1), pl.program_id(2)

        @pl.when(c == 0)
        def _():
            s_ref[...] = jnp.zeros_like(s_ref)

        dmask, _, xi, zeta, gam, _ = _ret_decay(lg_ref[h, d], d)
        qv = q_ref[...]
        kf = k_ref[...].astype(F32) * RET_K_SCALE
        vv = v_ref[...]
        s = lax.dot_general(qv, kf.astype(BF16), (((1,), (1,)), ((), ())), preferred_element_type=F32)
        sb = s_ref[...]
        o_ref[...] = (jnp.dot((s * dmask).astype(BF16), vv, preferred_element_type=F32)
                      + jnp.dot((qv.astype(F32) * xi).astype(BF16), sb.astype(BF16), preferred_element_type=F32))
        st_ref[...] = sb
        s_ref[...] = gam * sb + lax.dot_general((kf * zeta).astype(BF16), vv, (((0,), (0,)), ((), ())), preferred_element_type=F32)

    blk = lambda col: pl.BlockSpec((BLOCK, hd), lambda h, d, c: (chunk(d, c), col + h))
    return pl.pallas_call(
        body, grid=(RET_HEADS, 2, nc), in_specs=[pl.BlockSpec(memory_space=pltpu.SMEM), blk(0), blk(0), blk(COL_RV)],
        out_specs=(pl.BlockSpec((None, BLOCK, hd), lambda h, d, c: (d, chunk(d, c), h)),
                   pl.BlockSpec((None, None, None, hd, hd), lambda h, d, c: (h, d, c, 0, 0))),
        out_shape=(_sds((2, n, RET_WIDTH), F32), _sds((RET_HEADS, 2, nc, hd, hd), F32)), scratch_shapes=[pltpu.VMEM((hd, hd), F32)], name=name,
        compiler_params=pltpu.CompilerParams(dimension_semantics=("parallel", "parallel", "arbitrary")))(lg, q, k, proj)


def _ret_bwd(q, k, proj, lg, do, states, name):
    n = q.shape[0]
    nc = n // BLOCK
    hd = RET_HEAD_DIM
    chunk = lambda d, r: (nc - 1 - r) + d * (2 * r - (nc - 1))

    def body(lg_ref, q_ref, k_ref, v_ref, do_ref, st_ref, dq_ref, dk_ref, dv_ref, dlg_ref, ds_ref):
        h, d, r = pl.program_id(0), pl.program_id(1), pl.program_id(2)

        @pl.when(r == 0)
        def _():
            ds_ref[...] = jnp.zeros_like(ds_ref)
            dlg_ref[...] = jnp.zeros_like(dlg_ref)

        dmask, dist, xi, zeta, gam, tf = _ret_decay(lg_ref[h, d], d)
        qv, vv, dov = q_ref[...], v_ref[...], do_ref[...]
        qf = qv.astype(F32)
        kf = k_ref[...].astype(F32) * RET_K_SCALE
        kb = kf.astype(BF16)
        sc = st_ref[...]
        dsn = ds_ref[...]
        nt = (((1,), (1,)), ((), ()))
        tn = (((0,), (0,)), ((), ()))
        s = lax.dot_general(qv, kb, nt, preferred_element_type=F32)
        dsc = lax.dot_general(dov, vv, nt, preferred_element_type=F32) * dmask
        dsb = dsc.astype(BF16)
        dq_c = xi * lax.dot_general(dov, sc.astype(BF16), nt, preferred_element_type=F32)
        dk_c = zeta * lax.dot_general(vv, dsn.astype(BF16), nt, preferred_element_type=F32)
        dq = jnp.dot(dsb, kb, preferred_element_type=F32) + dq_c
        dk = lax.dot_general(dsb, qv, tn, preferred_element_type=F32) + dk_c
        dv = (lax.dot_general((s * dmask).astype(BF16), dov, tn, preferred_element_type=F32)
              + jnp.dot((kf * zeta).astype(BF16), dsn.astype(BF16), preferred_element_type=F32))
        ds_ref[...] = gam * dsn + lax.dot_general((qf * xi).astype(BF16), dov, tn, preferred_element_type=F32)
        dlg = (jnp.sum(dsc * s * dist, keepdims=True)
               + jnp.sum((tf + 1.0) * jnp.sum(qf * dq_c, axis=-1, keepdims=True), keepdims=True)
               + jnp.sum((BLOCK - 1.0 - tf) * jnp.sum(kf * dk_c, axis=-1, keepdims=True), keepdims=True)
               + BLOCK * gam * jnp.sum(dsn * sc, keepdims=True))
        dlg_ref[...] += dlg
        row = lax.broadcasted_iota(jnp.int32, (BLOCK, 1), 0) + chunk(d, r) * BLOCK
        keep = row >= PAD_FRONT
        dq_ref[...] = dq
        dk_ref[...] = jnp.where(keep, dk * RET_K_SCALE, 0.0)
        dv_ref[...] = jnp.where(keep, dv, 0.0)

    blk = lambda col: pl.BlockSpec((BLOCK, hd), lambda h, d, r: (chunk(d, r), col + h))
    plane = pl.BlockSpec((None, BLOCK, hd), lambda h, d, r: (d, chunk(d, r), h))
    return pl.pallas_call(
        body, grid=(RET_HEADS, 2, nc),
        in_specs=[pl.BlockSpec(memory_space=pltpu.SMEM), blk(0), blk(0), blk(COL_RV), blk(0),
                  pl.BlockSpec((None, None, None, hd, hd), lambda h, d, r: (h, d, nc - 1 - r, 0, 0))],
        out_specs=(plane, plane, plane, pl.BlockSpec((None, None, 8, 128), lambda h, d, r: (h, d, 0, 0))),
        out_shape=(_sds((2, n, RET_WIDTH), F32),) * 3 + (_sds((RET_HEADS, 2, 8, 128), F32),), scratch_shapes=[pltpu.VMEM((hd, hd), F32)], name=name,
        compiler_params=pltpu.CompilerParams(dimension_semantics=("parallel", "parallel", "arbitrary")))(lg, q, k, proj, do, states)


def _retgate_fwd(o, proj, gain, name):
    _, n, _ = o.shape
    tr = _row_tile(n, 1056)
    hd = RET_HEAD_DIM

    def body(o_ref, rg_ref, g_ref, y_ref):
        ov = o_ref[0] + o_ref[1]
        r = lax.rsqrt(jnp.mean(ov * ov, axis=-1, keepdims=True) + EPS)
        y_ref[...] = (_silu(rg_ref[...].astype(F32)) * (ov * r * g_ref[...])).astype(y_ref.dtype)

    return pl.pallas_call(body, grid=(n // tr, RET_HEADS),
                          in_specs=[pl.BlockSpec((2, tr, hd), lambda i, h: (0, i, h)), pl.BlockSpec((tr, hd), lambda i, h: (i, COL_RG + h)),
                                    pl.BlockSpec((1, hd), lambda i, h: (0, h))],
                          out_specs=pl.BlockSpec((tr, hd), lambda i, h: (i, h)), out_shape=_sds((n, RET_WIDTH), BF16), name=name)(o, proj, gain)


def _retgate_bwd(dmixed, o, proj, gain, name):
    _, n, _ = o.shape
    tr = _row_tile(n, 1056)
    hd = RET_HEAD_DIM

    def body(dy_ref, o_ref, rg_ref, g_ref, do_ref, drg_ref, dg_ref):
        i = pl.program_id(1)
        ov = o_ref[0] + o_ref[1]
        r = lax.rsqrt(jnp.mean(ov * ov, axis=-1, keepdims=True) + EPS)
        xhat = ov * r
        rg = rg_ref[...].astype(F32)
        dy = dy_ref[...].astype(F32)
        drg_ref[...] = (dy * (xhat * g_ref[...]) * _dsilu(rg)).astype(drg_ref.dtype)
        dn = dy * _silu(rg)
        dxh = dn * g_ref[...]
        do_ref[...] = (r * (dxh - xhat * jnp.mean(dxh * xhat, axis=-1, keepdims=True))).astype(do_ref.dtype)

        @pl.when(i == 0)
        def _():
            dg_ref[...] = jnp.zeros_like(dg_ref)

        dg_ref[...] += jnp.sum(dn * xhat, axis=0, keepdims=True)

    tile = pl.BlockSpec((tr, hd), lambda h, i: (i, h))
    vec = pl.BlockSpec((1, hd), lambda h, i: (0, h))
    return pl.pallas_call(body, grid=(RET_HEADS, n // tr),
                          in_specs=[pl.BlockSpec((tr, hd), lambda h, i: (i, RET_HEADS + h)), pl.BlockSpec((2, tr, hd), lambda h, i: (0, i, h)),
                                    pl.BlockSpec((tr, hd), lambda h, i: (i, COL_RG + h)), vec],
                          out_specs=(tile, tile, vec), out_shape=(_sds((n, RET_WIDTH), BF16), _sds((n, RET_WIDTH), BF16), _sds((1, RET_WIDTH), F32)),
                          name=name, compiler_params=pltpu.CompilerParams(dimension_semantics=("parallel", "arbitrary")))(dmixed, o, proj, gain)


def _swiglu_fwd(gate, up, name):
    n, f = gate.shape
    tr, tc = _row_tile(n, 384), f // 2

    def body(g_ref, u_ref, o_ref):
        o_ref[...] = (_silu(g_ref[...].astype(F32)) * u_ref[...].astype(F32)).astype(o_ref.dtype)

    t = pl.BlockSpec((tr, tc), lambda i, j: (i, j))
    return pl.pallas_call(body, grid=(n // tr, 2), in_specs=[t, t], out_specs=t, out_shape=_sds((n, f), BF16), name=name)(gate, up)


def _swiglu_bwd(df, gate, up, name):
    n, f = gate.shape
    tr, tc = _row_tile(n, 384), f // 2

    def body(d_ref, g_ref, u_ref, dg_ref, du_ref):
        d, g = d_ref[...].astype(F32), g_ref[...].astype(F32)
        dg_ref[...] = (d * u_ref[...].astype(F32) * _dsilu(g)).astype(dg_ref.dtype)
        du_ref[...] = (d * _silu(g)).astype(du_ref.dtype)

    t = pl.BlockSpec((tr, tc), lambda i, j: (i, j))
    return pl.pallas_call(body, grid=(n // tr, 2), in_specs=[t, t, t], out_specs=(t, t), out_shape=(_sds((n, f), BF16),) * 2, name=name)(df, gate, up)


def _loss_head(h, target, name):
    n, d = h.shape
    nb = n // BLOCK

    def body(h_ref, t_ref, dh_ref, l_ref):
        i = pl.program_id(0)

        @pl.when(i == 0)
        def _():
            l_ref[...] = jnp.zeros_like(l_ref)
            dh_ref[...] = jnp.zeros_like(dh_ref)

        @pl.when(i > 0)
        def _():
            e = h_ref[...] - t_ref[...]
            dh_ref[...] = e * (1.0 / d)
            l_ref[...] += 0.5 * jnp.sum(jnp.mean(e * e, axis=-1, keepdims=True), keepdims=True)

    blk = pl.BlockSpec((BLOCK, d), lambda i: (i, 0))
    return pl.pallas_call(body, grid=(nb,), in_specs=[blk, pl.BlockSpec((BLOCK, d), lambda i: (jnp.maximum(i - 1, 0), 0))],
                          out_specs=(blk, pl.BlockSpec((8, 128), lambda i: (0, 0))), out_shape=(_sds((n, d), F32), _sds((8, 128), F32)), name=name,
                          compiler_params=pltpu.CompilerParams(dimension_semantics=("arbitrary",)))(h, target)


def _adamw(parts, w, m, v, name, sel=None, layer=None, prev=None):
    s, (r, c) = parts.shape[0], parts.shape[-2:]
    tr = _row_tile(r, max(16, (ADAMW_TILE_ELEMS // c) // 16 * 16))
    b1c, b2c = 1.0 - ADAM_B1 ** ADAM_STEP, 1.0 - ADAM_B2 ** ADAM_STEP

    def body(p_ref, w_ref, m_ref, v_ref, *rest):
        g_ref, d_ref, mo_ref, vo_ref = rest[-4:]
        g = p_ref[0].astype(F32)
        for q in range(1, s):
            g = g + p_ref[q].astype(F32)
        mn = ADAM_B1 * m_ref[...] + (1.0 - ADAM_B1) * g
        vn = ADAM_B2 * v_ref[...] + (1.0 - ADAM_B2) * jnp.square(g)
        g_ref[...] = g
        mo_ref[...] = mn
        vo_ref[...] = vn
        d_ref[...] = -ADAM_LR * ((mn / b1c) / (jnp.sqrt(vn / b2c) + ADAM_EPS) + ADAM_WD * w_ref[...])

    pspec = (pl.BlockSpec((s, tr, c), lambda i: (0, i, 0)) if sel is None else pl.BlockSpec((s, None, tr, c), lambda i: (0, sel, i, 0)))
    if layer is None:
        t = pl.BlockSpec((tr, c), lambda i: (i, 0))
        return pl.pallas_call(body, grid=(r // tr,), in_specs=[pspec, t, t, t], out_specs=(t, t, t, t), out_shape=(_sds((r, c), F32),) * 4,
                              name=name)(parts, w, m, v)
    t = pl.BlockSpec((None, tr, c), lambda i: (layer, i, 0))
    prev = prev if prev is not None else tuple(lax.empty(w.shape, F32) for _ in range(4))
    return pl.pallas_call(body, grid=(r // tr,), in_specs=[pspec, t, t, t] + [ANY] * 4, out_specs=(t, t, t, t), out_shape=(_sds(w.shape, F32),) * 4,
                          input_output_aliases={4 + i: i for i in range(4)}, name=name)(parts, w, m, v, *prev)


def _allgather(xs, name):
    na = len(xs)

    def body(*refs):
        x_refs, o_refs = refs[:na], refs[na:2 * na]
        send, recv, lsem = refs[2 * na:]
        x, y, c = lax.axis_index("x"), lax.axis_index("y"), lax.axis_index("c")
        me, sib = (x, y, c), (x, y, 1 - c)
        chips = [(1 - x, y), (x, 1 - y), (1 - x, 1 - y)]
        slot = lambda p: 4 * p[0] + 2 * p[1] + p[2]

        def copy(a, k, block, to, src=None):
            dst = o_refs[a].at[slot(block)]
            return pltpu.make_async_remote_copy(src_ref=dst if src is None else src, dst_ref=dst, send_sem=send.at[a, k], recv_sem=recv.at[a, k],
                                                device_id=to, device_id_type=MESH)

        mine = [pltpu.make_async_copy(x_refs[a], o_refs[a].at[slot(me)], lsem.at[a]) for a in range(na)]
        for cp in mine:
            cp.start()
        first = []
        for a in range(na):
            first.append(copy(a, 0, me, sib, src=x_refs[a]))
            first += [copy(a, 1 + j, me, (*chip, c), src=x_refs[a]) for j, chip in enumerate(chips)]
        for cp in first:
            cp.start()
        passed = []
        for j, chip in enumerate(chips):
            for a in range(na):
                copy(a, 1 + j, (*chip, c), me).wait_recv()
                passed.append(copy(a, 4 + j, (*chip, c), sib))
                passed[-1].start()
        for a in range(na):
            copy(a, 0, sib, me).wait_recv()
            for j, chip in enumerate(chips):
                copy(a, 4 + j, (*chip, 1 - c), me).wait_recv()
        for cp in first + passed:
            cp.wait_send()
        for cp in mine:
            cp.wait()

    return pl.pallas_call(body, in_specs=[ANY] * na, out_specs=[ANY] * na, out_shape=[_sds((N_DEV,) + t.shape, t.dtype) for t in xs],
                          scratch_shapes=[pltpu.SemaphoreType.DMA((na, 7)), pltpu.SemaphoreType.DMA((na, 7)), pltpu.SemaphoreType.DMA((na,))],
                          name=name)(*xs)


HBM = pl.BlockSpec(memory_space=pltpu.HBM)
SEM = pl.BlockSpec(memory_space=pltpu.SEMAPHORE)
EFFECT = pltpu.SideEffectType.DATAFLOW_SIDE_EFFECTING


def _split_copies(x_refs, land_refs, send, recv, scatter, landing):
    x, y, c = lax.axis_index("x"), lax.axis_index("y"), lax.axis_index("c")
    me = 4 * x + 2 * y + c
    out = []
    for a in range(len(x_refs)):
        for r in range(1, N_DEV):
            peer = ((1 - x if r & 4 else x), (1 - y if r & 2 else y), (1 - c if r & 1 else c))
            slot = 4 * peer[0] + 2 * peer[1] + peer[2]
            out.append(pltpu.make_async_remote_copy(src_ref=x_refs[a].at[slot] if scatter else x_refs[a],
                                                    dst_ref=land_refs[a].at[slot if landing else me], send_sem=send.at[7 * a + r - 1],
                                                    recv_sem=recv.at[7 * a + r - 1], device_id=peer, device_id_type=MESH))
    return out


def _send_start(xs, scatter, after, name):
    na = len(xs)
    lands = [lax.empty(t.shape if scatter else (N_DEV,) + t.shape, t.dtype) for t in xs]

    def body(*refs):
        x_refs, land_refs = refs[:na], refs[na:2 * na]
        send, recv, token = refs[2 * na + 1], refs[2 * na + 2], refs[-1]
        for cp in _split_copies(x_refs, land_refs, send, recv, scatter, False):
            cp.start()
        token[...] = jnp.zeros_like(token)

    hbm = lambda t: pltpu.with_memory_space_constraint(t, pltpu.HBM)
    outs = pl.pallas_call(
        body, name=name,
        out_shape=(pltpu.SemaphoreType.DMA((7 * na,)), pltpu.SemaphoreType.DMA((7 * na,)), *[pltpu.HBM(t.shape, t.dtype) for t in xs],
                   *[pltpu.HBM(t.shape, t.dtype) for t in lands], _sds((8, 128), F32)),
        in_specs=[HBM] * (2 * na) + [ANY], out_specs=(SEM, SEM, *[HBM] * (2 * na), pl.BlockSpec(memory_space=pltpu.VMEM)),
        input_output_aliases={i: 2 + i for i in range(2 * na)},
        compiler_params=pltpu.CompilerParams(has_side_effects=EFFECT))(*[hbm(t) for t in xs], *[hbm(t) for t in lands], after)
    return outs[0], outs[1], list(outs[2:2 + na]), list(outs[2 + na:2 + 2 * na]), outs[-1]


def _send_wait(started, scatter, after, name):
    send, recv, xs, lands, _ = started
    na = len(xs)

    def body(*refs):
        for cp in _split_copies(refs[:na], refs[na:2 * na], refs[2 * na], refs[2 * na + 1], scatter, True):
            cp.wait_send()
            cp.wait_recv()

    outs = pl.pallas_call(body, name=name, out_shape=tuple(pltpu.HBM(t.shape, t.dtype) for t in xs + lands),
                          in_specs=[HBM] * (2 * na) + [SEM, SEM, ANY], out_specs=[HBM] * (2 * na),
                          input_output_aliases={i: i for i in range(2 * na)},
                          compiler_params=pltpu.CompilerParams(has_side_effects=EFFECT))(*xs, *lands, send, recv, after)
    return list(outs[:na]), list(outs[na:])


def _local_step(x, meta, target, weights_fn, grads_fn, sink, dec_f, dec_b, ret_norm, n_mix_pre, n_mix_post, n_ffn_pre, n_ffn_post):
    depth = n_mix_pre.shape[0]
    d = D_MODEL
    h = jnp.concatenate([jnp.zeros((PAD_FRONT, d), F32), meta, x], axis=0)
    n = h.shape[0]
    cos_a, sin_a, cos_r, sin_r, perm = _rope_tables(n)
    lg_all = jnp.stack([-jnp.exp(dec_f), -jnp.exp(dec_b)], axis=-1)
    saved = []
    for l in range(depth):
        t = f"l{l}_"
        sink_b = jnp.broadcast_to(sink[l][:, None], (ATT_HEADS, 128))
        wi, wo, wg, wu, wd, tok = weights_fn(l, h)
        u = _norm_fwd(h, (n_mix_pre[l] + tok)[None], None, BF16, t + "norm_mix_pre")
        proj = _mm_nn(u, wi, BF16, WIDE_TILE, d, t + "proj")
        aq = _rope_att(proj, 0, ATT_HEADS, cos_a, sin_a, perm, t + "rope_aq")
        ak = _rope_att(proj, COL_AK, ATT_KV_HEADS, cos_a, sin_a, perm, t + "rope_ak")
        att = _att_fwd(aq, ak, proj, sink_b, t + "att")
        proj3 = proj[None]
        rq = _rope_ret(proj3, COL_RQ, cos_r, sin_r, t + "rope_rq")
        rk = _rope_ret(proj3, COL_RK, cos_r, sin_r, t + "rope_rk")
        o_ret, states = _ret_fwd(rq, rk, proj, lg_all[l], t + "ret")
        retg = _retgate_fwd(o_ret, proj, ret_norm[l][None], t + "retgate")
        mixed = jnp.concatenate([att, retg], axis=1)
        mo = _mm_nn(mixed, wo, F32, 1024, d, t + "out_proj")
        h_mid = _norm_fwd(mo, n_mix_post[l][None], h, F32, t + "norm_mix_post")
        u2 = _norm_fwd(h_mid, n_ffn_pre[l][None], None, BF16, t + "norm_ffn_pre")
        gate = _mm_nn(u2, wg, BF16, WIDE_TILE, d, t + "gate")
        up = _mm_nn(u2, wu, BF16, WIDE_TILE, d, t + "up")
        f = _swiglu_fwd(gate, up, t + "swiglu")
        dn = _mm_nn(f, wd, F32, 1024, WIDE_TILE, t + "down")
        h_out = _norm_fwd(dn, n_ffn_post[l][None], h_mid, F32, t + "norm_ffn_post")
        saved.append(dict(h=h, u=u, proj=proj, aq=aq, ak=ak, rq=rq, rk=rk, o_ret=o_ret, states=states, mixed=mixed, mo=mo, h_mid=h_mid, u2=u2,
                          gate=gate, up=up, f=f, dn=dn, sink_b=sink_b, wi=wi, wo=wo, wg=wg, wu=wu, wd=wd))
        h = h_out

    dh, loss_part = _loss_head(h, target, "loss_head")
    gs = dict(sink=[None] * depth, dec_f=[None] * depth, dec_b=[None] * depth, ret_norm=[None] * depth, mix_pre=[None] * depth,
              mix_post=[None] * depth, ffn_pre=[None] * depth, ffn_post=[None] * depth)
    for l in reversed(range(depth)):
        t = f"l{l}_b_"
        sv = saved[l]
        proj = sv["proj"]
        d_dn, gs["ffn_post"][l] = _norm_bwd(dh, sv["dn"], n_ffn_post[l][None], None, BF16, t + "norm_ffn_post")
        gw = {}
        d_f = _mm_nt(d_dn, sv["wd"], BF16, WIDE_TILE, d, t + "d_f")
        gw["wd"] = _mm_tn(sv["f"], d_dn, WIDE_TILE, 1024, t + "dw_down")
        d_gate, d_up = _swiglu_bwd(d_f, sv["gate"], sv["up"], t + "swiglu")
        du2 = _mm_nt(d_gate, sv["wg"], F32, 1024, WIDE_TILE, t + "du2_gate")
        du2 = _mm_nt(d_up, sv["wu"], F32, 1024, WIDE_TILE, t + "du2_up", acc=du2)
        gw["wg"] = _mm_tn(sv["u2"], d_gate, 1024, WIDE_TILE, t + "dw_gate")
        gw["wu"] = _mm_tn(sv["u2"], d_up, 1024, WIDE_TILE, t + "dw_up")
        dh, gs["ffn_pre"][l] = _norm_bwd(du2, sv["h_mid"], n_ffn_pre[l][None], dh, F32, t + "norm_ffn_pre")
        d_mo, gs["mix_post"][l] = _norm_bwd(dh, sv["mo"], n_mix_post[l][None], None, BF16, t + "norm_mix_post")
        d_mixed = _mm_nt(d_mo, sv["wo"], BF16, 1024, d, t + "d_mixed")
        gw["wo"] = _mm_tn(sv["mixed"], d_mo, 1024, 1024, t + "dw_out")
        d_o, d_rg, gs["ret_norm"][l] = _retgate_bwd(d_mixed, sv["o_ret"], proj, ret_norm[l][None], t + "retgate")
        dq_r, dk_r, dv_r, dlg = _ret_bwd(sv["rq"], sv["rk"], proj, lg_all[l], d_o, sv["states"], t + "ret")
        draw = dlg[:, :, 0, 0] * lg_all[l]
        gs["dec_f"][l], gs["dec_b"][l] = draw[:, 0], draw[:, 1]
        dq_a, dk_a, dv_a, dsink = _att_bwd(sv["aq"], sv["ak"], proj, sv["sink_b"], d_mixed, t + "att")
        gs["sink"][l] = dsink[:, 0]
        dproj = jnp.concatenate([
            _rope_att(dq_a, 0, ATT_HEADS, cos_a, -sin_a, perm, t + "rope_aq"),
            _rope_att(dk_a, 0, ATT_KV_HEADS, cos_a, -sin_a, perm, t + "rope_ak"),
            dv_a.astype(BF16),
            _rope_ret(dq_r, 0, cos_r, -sin_r, t + "rope_rq"),
            _rope_ret(dk_r, 0, cos_r, -sin_r, t + "rope_rk"),
            _rope_ret(dv_r, 0, None, None, t + "sum_rv"),
            d_rg], axis=1)
        du = _mm_nt(dproj, sv["wi"], F32, 1024, WIDE_TILE, t + "du")
        gw["wi"] = _mm_tn(sv["u"], dproj, 1024, WIDE_TILE, t + "dw_in")
        dh, gs["mix_pre"][l] = _norm_bwd(du, sv["h"], n_mix_pre[l][None], dh, F32, t + "norm_mix_pre")
        grads_fn(l, gw, dh)
    return loss_part[0, 0], dh, gs


def _cols_from_blocks(g):
    return g.transpose(1, 0, 2).reshape(g.shape[1], N_DEV * SHARD_COLS)


def _blocks_from_cols(w):
    return w.reshape(w.shape[0], N_DEV, SHARD_COLS).transpose(1, 0, 2)


def _pack_small(mix_pre, mix_post, ffn_pre, ffn_post, ret_norm, sink, dec_f, dec_b, loss, meta):
    d = D_MODEL

    def tile(a, rows=8):
        a = jnp.reshape(a, (-1, a.shape[-1])) if a.ndim else jnp.reshape(a, (1, 1))
        return jnp.pad(a, ((0, rows - a.shape[0]), (0, d - a.shape[1])))

    return jnp.concatenate([tile(mix_pre), tile(mix_post), tile(ffn_pre), tile(ffn_post), tile(ret_norm.reshape(-1, d)), tile(sink), tile(dec_f),
                            tile(dec_b), tile(loss), tile(meta, SMALL_ROWS - ROW_META)], axis=0)


def _unpack_small(p, depth):
    rows = lambda r0, cols: p[r0:r0 + depth, :cols]
    return dict(mix_pre=rows(ROW_MIX_PRE, D_MODEL), mix_post=rows(ROW_MIX_POST, D_MODEL), ffn_pre=rows(ROW_FFN_PRE, D_MODEL),
                ffn_post=rows(ROW_FFN_POST, D_MODEL), ret_norm=p[ROW_RET_NORM:ROW_RET_NORM + depth * RET_WIDTH // D_MODEL].reshape(depth, RET_WIDTH),
                sink=rows(ROW_SINK, ATT_HEADS), dec_f=rows(ROW_DEC_F, RET_HEADS), dec_b=rows(ROW_DEC_B, RET_HEADS), loss=p[ROW_LOSS, 0])


def kernel(x, meta_tokens, w_in, w_out, attn_sink, ret_decay_fwd, ret_decay_bwd, ret_norm, norm_mix_pre, norm_mix_post, w_gate, w_up, w_down, norm_ffn_pre, norm_ffn_post, loss_target, m_meta_tokens, m_w_in, m_w_out, m_attn_sink, m_ret_decay_fwd, m_ret_decay_bwd, m_ret_norm, m_norm_mix_pre, m_norm_mix_post, m_w_gate, m_w_up, m_w_down, m_norm_ffn_pre, m_norm_ffn_post, v_meta_tokens, v_w_in, v_w_out, v_attn_sink, v_ret_decay_fwd, v_ret_decay_bwd, v_ret_norm, v_norm_mix_pre, v_norm_mix_post, v_w_gate, v_w_up, v_w_down, v_norm_ffn_pre, v_norm_ffn_post):
    depth, d = w_in.shape[0], D_MODEL
    me = 4 * lax.axis_index("x") + 2 * lax.axis_index("y") + lax.axis_index("c")
    zero = jnp.zeros((), F32)

    meta_g, = _allgather([meta_tokens], "gather_meta")
    meta = meta_g.transpose(1, 0, 2).reshape(N_META, d)

    def shards(l):
        return [jnp.stack([w_in[l], w_gate[l], w_up[l]]).astype(BF16), w_down[l].astype(BF16), w_out[l].astype(BF16)]

    gathers = {0: _send_start(shards(0), False, meta_g, "gather_start_l0")}
    if depth > 1:
        gathers[1] = _send_start(shards(1), False, gathers[0][4], "gather_start_l1")

    def weights_fn(l, h):
        if 1 <= l < depth - 1:
            gathers[l + 1] = _send_start(shards(l + 1), False, h, f"gather_start_l{l + 1}")
        after = gathers[l + 1][4] if l + 1 in gathers else h
        xs, lands = _send_wait(gathers.pop(l), False, after, f"gather_wait_l{l}")
        cols, rows_d, rows_o = [lax.dynamic_update_slice(g, t[None], (me,) + (0,) * t.ndim) for t, g in zip(xs, lands)]
        return (_cols_from_blocks(cols[:, 0]), rows_o.reshape(d, d), _cols_from_blocks(cols[:, 1]), _cols_from_blocks(cols[:, 2]),
                rows_d.reshape(D_FF, d), zero)

    exchanges, adam = {}, {}
    big = dict(wi=(w_in, m_w_in, v_w_in), wg=(w_gate, m_w_gate, v_w_gate), wu=(w_up, m_w_up, v_w_up), wd=(w_down, m_w_down, v_w_down),
               wo=(w_out, m_w_out, v_w_out))

    def finish(l, after):
        xs, lands = _send_wait(exchanges.pop(l), True, after, f"exchange_wait_l{l}")
        cols, rows_d, rows_o = [lax.dynamic_update_slice(g, lax.dynamic_slice_in_dim(t, me, 1, 0), (me,) + (0,) * (t.ndim - 1))
                                for t, g in zip(xs, lands)]
        for kind, parts, sel in (("wi", cols, 0), ("wg", cols, 1), ("wu", cols, 2), ("wd", rows_d, None), ("wo", rows_o, None)):
            adam[kind] = _adamw(parts, *big[kind], f"adamw_{kind}_l{l}", sel=sel, layer=l, prev=adam.get(kind))

    def grads_fn(l, gw, dh_l):
        packed = [jnp.stack([_blocks_from_cols(gw["wi"]), _blocks_from_cols(gw["wg"]), _blocks_from_cols(gw["wu"])], axis=1),
                  gw["wd"].reshape(N_DEV, SHARD_COLS, d), gw["wo"].reshape(N_DEV, d // N_DEV, d)]
        exchanges[l] = _send_start(packed, True, dh_l, f"exchange_start_l{l}")
        if l + 1 < depth:
            finish(l + 1, exchanges[l][4])

    loss_part, dh, gs = _local_step(x[0], meta, loss_target[0], weights_fn, grads_fn, attn_sink, ret_decay_fwd, ret_decay_bwd, ret_norm,
                                    norm_mix_pre, norm_mix_post, norm_ffn_pre, norm_ffn_post)
    grad_x = dh[BLOCK:][None]

    st = lambda xs: jnp.stack([t.reshape(-1) if t.ndim == 1 else t[0] for t in xs])
    small = _pack_small(st(gs["mix_pre"]), st(gs["mix_post"]), st(gs["ffn_pre"]), st(gs["ffn_post"]), st(gs["ret_norm"]), st(gs["sink"]),
                        st(gs["dec_f"]), st(gs["dec_b"]), loss_part, dh[PAD_FRONT:BLOCK])
    small_g, = _allgather([small], "gather_small")
    finish(0, small_g)
    o_wi, o_wo, o_wg, o_wu, o_wd = adam["wi"], adam["wo"], adam["wg"], adam["wu"], adam["wd"]
    zmeta = jnp.zeros((N_META, d), F32)
    packs = [_pack_small(a[0], a[1], a[2], a[3], a[4], a[5], a[6], a[7], zero, zmeta) for a in (
        (norm_mix_pre, norm_mix_post, norm_ffn_pre, norm_ffn_post, ret_norm, attn_sink, ret_decay_fwd, ret_decay_bwd),
        (m_norm_mix_pre, m_norm_mix_post, m_norm_ffn_pre, m_norm_ffn_post, m_ret_norm, m_attn_sink, m_ret_decay_fwd, m_ret_decay_bwd),
        (v_norm_mix_pre, v_norm_mix_post, v_norm_ffn_pre, v_norm_ffn_post, v_ret_norm, v_attn_sink, v_ret_decay_fwd, v_ret_decay_bwd))]
    o_small = [_unpack_small(o, depth) for o in _adamw(small_g, packs[0], packs[1], packs[2], "adamw_small")]
    meta_parts = lax.dynamic_slice(small_g, (0, ROW_META, me * (d // N_DEV)), (N_DEV, N_META, d // N_DEV))
    o_meta = _adamw(meta_parts, meta_tokens, m_meta_tokens, v_meta_tokens, "adamw_meta")

    outs = []
    for i in range(4):
        s = o_small[i]
        outs += [o_meta[i], o_wi[i], o_wo[i], s["sink"], s["dec_f"], s["dec_b"], s["ret_norm"], s["mix_pre"], s["mix_post"], o_wg[i], o_wu[i],
                 o_wd[i], s["ffn_pre"], s["ffn_post"]]
    return (o_small[0]["loss"], grad_x, *outs)
```

```python
import jax
import jax.numpy as jnp
import numpy as np
from jax import lax
from jax.experimental import pallas as pl
from jax.experimental.pallas import tpu as pltpu

F32, BF16 = jnp.float32, jnp.bfloat16

D_MODEL = 2048
N_META = 16
BLOCK = 128
WINDOW = 128
PAD_FRONT = BLOCK - N_META
ATT_HEAD_DIM = 128
ATT_WIDTH = D_MODEL // 2
ATT_HEADS = ATT_WIDTH // ATT_HEAD_DIM
ATT_KV_HEADS = 2
ATT_GROUP = ATT_HEADS // ATT_KV_HEADS
KV_WIDTH = ATT_KV_HEADS * ATT_HEAD_DIM
ROT_DIM = ATT_HEAD_DIM // 4
ROPE_THETA = 500000.0
RET_WIDTH = D_MODEL - ATT_WIDTH
RET_HEAD_DIM = 256
RET_HEADS = RET_WIDTH // RET_HEAD_DIM
RET_THETA = 10000.0
D_FF = 5632
IN_COLS = ATT_WIDTH + 2 * KV_WIDTH + 4 * RET_WIDTH
N_DEV = 8
SHARD_COLS = IN_COLS // N_DEV
EPS = 1e-6
NEG = -1e30
RET_K_SCALE = RET_HEAD_DIM ** -0.5
ATT_SCALE = ATT_HEAD_DIM ** -0.5

COL_AK = ATT_WIDTH // ATT_HEAD_DIM
COL_AV256 = (ATT_WIDTH + KV_WIDTH) // 256
COL_RQ = (ATT_WIDTH + 2 * KV_WIDTH) // RET_HEAD_DIM
COL_RK = COL_RQ + RET_HEADS
COL_RV = COL_RK + RET_HEADS
COL_RG = COL_RV + RET_HEADS

ADAM_LR, ADAM_B1, ADAM_B2, ADAM_EPS, ADAM_WD, ADAM_STEP = 0.001, 0.9, 0.999, 1e-08, 0.01, 10

ROW_MIX_PRE, ROW_MIX_POST, ROW_FFN_PRE, ROW_FFN_POST, ROW_RET_NORM, ROW_SINK, ROW_DEC_F, ROW_DEC_B, ROW_LOSS, ROW_META, SMALL_ROWS = (
    0, 8, 16, 24, 32, 40, 48, 56, 64, 72, 96)
ADAMW_TILE_ELEMS = 128 * 1024

MESH = pl.DeviceIdType.MESH
ANY = pl.BlockSpec(memory_space=pl.ANY)


def _row_tile(n, cap):
    for t in range(cap - cap % 16, 0, -16):
        if n % t == 0:
            return t
    raise ValueError(n)


def _sds(shape, dtype):
    return jax.ShapeDtypeStruct(shape, dtype)


def _silu(x):
    return x * jax.nn.sigmoid(x)


def _dsilu(x):
    s = jax.nn.sigmoid(x)
    return s * (1.0 + x * (1.0 - s))


def _norm_fwd(x, g, res, out_dtype, name):
    n, d = x.shape
    tr = _row_tile(n, 384)

    def body(*refs):
        if res is None:
            x_ref, g_ref, o_ref = refs
        else:
            x_ref, g_ref, r_ref, o_ref = refs
        xv = x_ref[...]
        r = lax.rsqrt(jnp.mean(xv * xv, axis=-1, keepdims=True) + EPS)
        y = xv * r * g_ref[...]
        if res is not None:
            y = y + r_ref[...]
        o_ref[...] = y.astype(o_ref.dtype)

    row = pl.BlockSpec((tr, d), lambda i: (i, 0))
    ins = [row, pl.BlockSpec((1, d), lambda i: (0, 0))] + ([row] if res is not None else [])
    args = (x, g) + ((res,) if res is not None else ())
    return pl.pallas_call(body, grid=(n // tr,), in_specs=ins, out_specs=row, out_shape=_sds((n, d), out_dtype), name=name)(*args)


def _norm_bwd(dy, x, g, res, out_dtype, name):
    n, d = x.shape
    tr = _row_tile(n, 384)

    def body(*refs):
        if res is None:
            dy_ref, x_ref, g_ref, dx_ref, dg_ref = refs
        else:
            dy_ref, x_ref, g_ref, r_ref, dx_ref, dg_ref = refs
        i = pl.program_id(0)
        xv = x_ref[...]
        r = lax.rsqrt(jnp.mean(xv * xv, axis=-1, keepdims=True) + EPS)
        xhat = xv * r
        dyf = dy_ref[...].astype(F32)
        gdy = dyf * g_ref[...]
        dx = r * (gdy - xhat * jnp.mean(gdy * xhat, axis=-1, keepdims=True))
        if res is not None:
            dx = dx + r_ref[...]
        dx_ref[...] = dx.astype(dx_ref.dtype)

        @pl.when(i == 0)
        def _():
            dg_ref[...] = jnp.zeros_like(dg_ref)

        dg_ref[...] += jnp.sum(dyf * xhat, axis=0, keepdims=True)

    row = pl.BlockSpec((tr, d), lambda i: (i, 0))
    vec = pl.BlockSpec((1, d), lambda i: (0, 0))
    ins = [row, row, vec] + ([row] if res is not None else [])
    args = (dy, x, g) + ((res,) if res is not None else ())
    return pl.pallas_call(body, grid=(n // tr,), in_specs=ins, out_specs=(row, vec),
                          out_shape=(_sds((n, d), out_dtype), _sds((1, d), F32)), name=name,
                          compiler_params=pltpu.CompilerParams(dimension_semantics=("arbitrary",)))(*args)


def _mm(a, b, *, ta, tb, grid, a_blk, a_map, b_blk, b_map, o_blk, o_map, o_shape, o_dtype, name, acc=None):
    nk = grid[2]
    dims = (((0,) if ta else (1,), (1,) if tb else (0,)), ((), ()))

    def body(*refs):
        if acc is None:
            a_ref, b_ref, o_ref = refs[:3]
            c_ref = None
        else:
            a_ref, b_ref, c_ref, o_ref = refs[:4]
        part = lax.dot_general(a_ref[...], b_ref[...], dims, preferred_element_type=F32)
        if nk == 1:
            if c_ref is not None:
                part = part + c_ref[...].astype(F32)
            o_ref[...] = part.astype(o_ref.dtype)
            return
        acc_ref = refs[-1]
        k = pl.program_id(2)

        @pl.when(k == 0)
        def _():
            acc_ref[...] = jnp.zeros_like(acc_ref) if c_ref is None else c_ref[...].astype(F32)

        acc_ref[...] += part

        @pl.when(k == nk - 1)
        def _():
            o_ref[...] = acc_ref[...].astype(o_ref.dtype)

    ins = [pl.BlockSpec(a_blk, a_map), pl.BlockSpec(b_blk, b_map)]
    args = [a, b]
    if acc is not None:
        ins.append(pl.BlockSpec(o_blk, o_map))
        args.append(acc)
    return pl.pallas_call(body, grid=grid, in_specs=ins, out_specs=pl.BlockSpec(o_blk, o_map), out_shape=_sds(o_shape, o_dtype),
                          scratch_shapes=[pltpu.VMEM(o_blk, F32)] if nk > 1 else [], name=name,
                          compiler_params=pltpu.CompilerParams(dimension_semantics=("parallel", "parallel", "arbitrary")))(*args)


TOKEN_TILE = 1056
WIDE_TILE = 1408


def _mm_nn(x, w, o_dtype, tn, tk, name):
    n, k = x.shape
    tm = _row_tile(n, TOKEN_TILE)
    return _mm(x, w, ta=False, tb=False, grid=(n // tm, w.shape[1] // tn, k // tk), a_blk=(tm, tk), a_map=lambda i, j, kk: (i, kk),
               b_blk=(tk, tn), b_map=lambda i, j, kk: (kk, j), o_blk=(tm, tn), o_map=lambda i, j, kk: (i, j),
               o_shape=(n, w.shape[1]), o_dtype=o_dtype, name=name)


def _mm_nt(dy, w, o_dtype, tn, tk, name, acc=None):
    n, k = dy.shape
    tm = _row_tile(n, TOKEN_TILE)
    return _mm(dy, w, ta=False, tb=True, grid=(n // tm, w.shape[0] // tn, k // tk), a_blk=(tm, tk), a_map=lambda i, j, kk: (i, kk),
               b_blk=(tn, tk), b_map=lambda i, j, kk: (j, kk), o_blk=(tm, tn), o_map=lambda i, j, kk: (i, j),
               o_shape=(n, w.shape[0]), o_dtype=o_dtype, name=name, acc=acc)


def _mm_tn(x, dy, tm, tn, name):
    n, m = x.shape
    tk = _row_tile(n, TOKEN_TILE)
    return _mm(x, dy, ta=True, tb=False, grid=(m // tm, dy.shape[1] // tn, n // tk), a_blk=(tk, tm), a_map=lambda i, j, kk: (kk, i),
               b_blk=(tk, tn), b_map=lambda i, j, kk: (kk, j), o_blk=(tm, tn), o_map=lambda i, j, kk: (i, j),
               o_shape=(m, dy.shape[1]), o_dtype=BF16, name=name)


def _rope_tables(n):
    pos = (jnp.arange(n) - PAD_FRONT).astype(F32)
    half = ROT_DIM // 2
    ang = pos[:, None] * (ROPE_THETA ** (-jnp.arange(half, dtype=F32) / half))[None, :]
    c, s = jnp.cos(ang), jnp.sin(ang)
    rest = ATT_HEAD_DIM - ROT_DIM
    cos_a = jnp.concatenate([c, c, jnp.ones((n, rest), F32)], axis=1)
    sin_a = jnp.concatenate([-s, s, jnp.zeros((n, rest), F32)], axis=1)
    half = RET_HEAD_DIM // 2
    ang = pos[:, None] * (RET_THETA ** (-jnp.arange(half, dtype=F32) / half))[None, :]
    c, s = jnp.cos(ang), jnp.sin(ang)
    perm = np.zeros((ATT_HEAD_DIM, ATT_HEAD_DIM), np.float32)
    for i in range(ROT_DIM):
        perm[(i + ROT_DIM // 2) % ROT_DIM, i] = 1.0
    return cos_a, sin_a, jnp.concatenate([c, c], axis=1), jnp.concatenate([-s, s], axis=1), jnp.asarray(perm, BF16)


def _rope_att(x, col0, heads, cos, sin, perm, name):
    n = x.shape[0]
    tr = _row_tile(n, 1056)

    def body(x_ref, c_ref, s_ref, p_ref, o_ref):
        xb = x_ref[...].astype(BF16)
        sw = jnp.dot(xb, p_ref[...], preferred_element_type=F32)
        o_ref[...] = (xb.astype(F32) * c_ref[...] + sw * s_ref[...]).astype(o_ref.dtype)

    hd = ATT_HEAD_DIM
    tab = pl.BlockSpec((tr, hd), lambda i, h: (i, 0))
    return pl.pallas_call(body, grid=(n // tr, heads),
                          in_specs=[pl.BlockSpec((tr, hd), lambda i, h: (i, col0 + h)), tab, tab, pl.BlockSpec((hd, hd), lambda i, h: (0, 0))],
                          out_specs=pl.BlockSpec((tr, hd), lambda i, h: (i, h)), out_shape=_sds((n, heads * hd), BF16), name=name)(x, cos, sin, perm)


def _rope_ret(x, col0, cos, sin, name):
    p, n, _ = x.shape
    tr = _row_tile(n, 1056)
    hd = RET_HEAD_DIM

    def body(*refs):
        x_ref, o_ref = refs[0], refs[-1]
        xv = x_ref[0].astype(F32)
        for q in range(1, p):
            xv = xv + x_ref[q].astype(F32)
        if cos is not None:
            sw = jnp.concatenate([xv[:, hd // 2:], xv[:, :hd // 2]], axis=1)
            xv = xv * refs[1][...] + sw * refs[2][...]
        o_ref[...] = xv.astype(o_ref.dtype)

    tab = pl.BlockSpec((tr, hd), lambda i, h: (i, 0))
    ins = [pl.BlockSpec((p, tr, hd), lambda i, h: (0, i, col0 + h))] + ([tab, tab] if cos is not None else [])
    args = (x,) + ((cos, sin) if cos is not None else ())
    return pl.pallas_call(body, grid=(n // tr, RET_HEADS), in_specs=ins, out_specs=pl.BlockSpec((tr, hd), lambda i, h: (i, h)),
                          out_shape=_sds((n, RET_WIDTH), BF16), name=name)(*args)


def _att_mask(nblk, n_tot):
    row = lax.broadcasted_iota(jnp.int32, (BLOCK, 4 * BLOCK), 0)
    col = lax.broadcasted_iota(jnp.int32, (BLOCK, 4 * BLOCK), 1)
    qi = nblk * BLOCK + row
    seg = col // BLOCK
    cj = col % BLOCK
    kj = (nblk - 1 + seg) * BLOCK + cj
    band = (jnp.abs(qi - kj) <= WINDOW) & (kj >= PAD_FRONT) & (kj < n_tot) & (seg < 3)
    meta = (seg == 3) & (cj >= PAD_FRONT) & (jnp.abs(qi - cj) > WINDOW)
    return band | meta


def _att_specs(nb, v_col):
    kv = lambda f, cb: pl.BlockSpec((BLOCK, KV_WIDTH), lambda n: (f(n), cb))
    prev, own, nxt, first = (lambda n: jnp.maximum(n - 1, 0)), (lambda n: n), (lambda n: jnp.minimum(n + 1, nb - 1)), (lambda n: 0)
    return [kv(f, 0) for f in (prev, own, nxt, first)] + [kv(f, v_col) for f in (prev, own, nxt, first)]


def _att_probs(s, ok, snk):
    s = jnp.where(ok, s, NEG)
    m = jnp.maximum(jnp.max(s, axis=-1, keepdims=True), snk)
    p = jnp.exp(s - m)
    ps = jnp.exp(snk - m)
    inv = 1.0 / (jnp.sum(p, axis=-1, keepdims=True) + ps)
    return p * inv, ps * inv


def _att_fwd(q, k, proj, sink_b, name):
    n = q.shape[0]
    nb = n // BLOCK
    hd = ATT_HEAD_DIM

    def body(q_ref, kp, ko, kn, km, vp, vo, vn, vm, sink_ref, o_ref):
        nblk = pl.program_id(0)
        ok = _att_mask(nblk, n)
        keep = (nblk * BLOCK + lax.broadcasted_iota(jnp.int32, (BLOCK, 1), 0)) >= PAD_FRONT
        for kh in range(ATT_KV_HEADS):
            cs = slice(kh * hd, (kh + 1) * hd)
            kk = jnp.concatenate([r[:, cs] for r in (kp, ko, kn, km)], axis=0)
            vv = jnp.concatenate([r[:, cs] for r in (vp, vo, vn, vm)], axis=0)
            heads = [kh * ATT_GROUP + g for g in range(ATT_GROUP)]
            q4 = jnp.concatenate([q_ref[:, h * hd:(h + 1) * hd] for h in heads], axis=0)
            s = lax.dot_general(q4, kk, (((1,), (1,)), ((), ())), preferred_element_type=F32) * ATT_SCALE
            ps = []
            for g, h in enumerate(heads):
                p, _ = _att_probs(s[g * BLOCK:(g + 1) * BLOCK], ok, sink_ref[h:h + 1, 0:1])
                ps.append(p)
            o = jnp.dot(jnp.concatenate(ps, axis=0).astype(BF16), vv, preferred_element_type=F32)
            for g, h in enumerate(heads):
                o_ref[:, h * hd:(h + 1) * hd] = jnp.where(keep, o[g * BLOCK:(g + 1) * BLOCK], 0.0).astype(o_ref.dtype)

    qspec = pl.BlockSpec((BLOCK, ATT_WIDTH), lambda i: (i, 0))
    return pl.pallas_call(body, grid=(nb,), in_specs=[qspec] + _att_specs(nb, COL_AV256) + [pl.BlockSpec((ATT_HEADS, 128), lambda i: (0, 0))],
                          out_specs=qspec, out_shape=_sds((n, ATT_WIDTH), BF16), name=name)(q, k, k, k, k, proj, proj, proj, proj, sink_b)


def _att_bwd(q, k, proj, sink_b, dmixed, name):
    n = q.shape[0]
    nb = n // BLOCK
    hd = ATT_HEAD_DIM

    def body(q_ref, kp, ko, kn, km, vp, vo, vn, vm, sink_ref, do_ref, dq_ref, dk_ref, dv_ref, dsink_ref):
        nblk = pl.program_id(0)

        @pl.when(nblk == 0)
        def _():
            dk_ref[...] = jnp.zeros_like(dk_ref)
            dv_ref[...] = jnp.zeros_like(dv_ref)
            dsink_ref[...] = jnp.zeros_like(dsink_ref)

        ok = _att_mask(nblk, n)
        rows = [jnp.maximum(nblk - 1, 0), nblk, jnp.minimum(nblk + 1, nb - 1), 0]
        for kh in range(ATT_KV_HEADS):
            cs = slice(kh * hd, (kh + 1) * hd)
            kk = jnp.concatenate([r[:, cs] for r in (kp, ko, kn, km)], axis=0)
            vv = jnp.concatenate([r[:, cs] for r in (vp, vo, vn, vm)], axis=0)
            heads = [kh * ATT_GROUP + g for g in range(ATT_GROUP)]
            q4 = jnp.concatenate([q_ref[:, h * hd:(h + 1) * hd] for h in heads], axis=0)
            do4 = jnp.concatenate([do_ref[:, h * hd:(h + 1) * hd] for h in heads], axis=0)
            s = lax.dot_general(q4, kk, (((1,), (1,)), ((), ())), preferred_element_type=F32) * ATT_SCALE
            dp = lax.dot_general(do4, vv, (((1,), (1,)), ((), ())), preferred_element_type=F32)
            ps, dss = [], []
            for g, h in enumerate(heads):
                p, psink = _att_probs(s[g * BLOCK:(g + 1) * BLOCK], ok, sink_ref[h:h + 1, 0:1])
                dpg = dp[g * BLOCK:(g + 1) * BLOCK]
                delta = jnp.sum(p * dpg, axis=-1, keepdims=True)
                ps.append(p)
                dss.append(p * (dpg - delta) * ATT_SCALE)
                dsink_ref[h:h + 1, :] = dsink_ref[h:h + 1, :] - jnp.sum(psink * delta, axis=0, keepdims=True)
            ds = jnp.concatenate(dss, axis=0).astype(BF16)
            pb = jnp.concatenate(ps, axis=0).astype(BF16)
            dq = jnp.dot(ds, kk, preferred_element_type=F32)
            for g, h in enumerate(heads):
                dq_ref[:, h * hd:(h + 1) * hd] = dq[g * BLOCK:(g + 1) * BLOCK].astype(dq_ref.dtype)
            dk = lax.dot_general(ds, q4, (((0,), (0,)), ((), ())), preferred_element_type=F32)
            dv = lax.dot_general(pb, do4, (((0,), (0,)), ((), ())), preferred_element_type=F32)
            for seg, r in enumerate(rows):
                at = (pl.ds(pl.multiple_of(r * BLOCK, BLOCK), BLOCK), cs)
                dk_ref[at] += dk[seg * BLOCK:(seg + 1) * BLOCK]
                dv_ref[at] += dv[seg * BLOCK:(seg + 1) * BLOCK]

    qspec = pl.BlockSpec((BLOCK, ATT_WIDTH), lambda i: (i, 0))
    whole = pl.BlockSpec((n, KV_WIDTH), lambda i: (0, 0))
    sinks = pl.BlockSpec((ATT_HEADS, 128), lambda i: (0, 0))
    return pl.pallas_call(body, grid=(nb,), in_specs=[qspec] + _att_specs(nb, COL_AV256) + [sinks, qspec], out_specs=(qspec, whole, whole, sinks),
                          out_shape=(_sds((n, ATT_WIDTH), BF16), _sds((n, KV_WIDTH), F32), _sds((n, KV_WIDTH), F32), _sds((ATT_HEADS, 128), F32)),
                          name=name, compiler_params=pltpu.CompilerParams(dimension_semantics=("arbitrary",)))(
                              q, k, k, k, k, proj, proj, proj, proj, sink_b, dmixed)


def _ret_decay(lg, d):
    a = lax.broadcasted_iota(jnp.int32, (BLOCK, 1), 0)
    b = lax.broadcasted_iota(jnp.int32, (1, BLOCK), 1)
    t_col = a + d * (BLOCK - 1 - 2 * a)
    t_row = b + d * (BLOCK - 1 - 2 * b)
    diff = t_col - t_row
    dist = jnp.maximum(diff, 0).astype(F32)
    dmask = jnp.where(diff >= d, jnp.exp(lg * dist), 0.0)
    tf = t_col.astype(F32)
    xi = jnp.exp(lg * (tf + 1.0))
    zeta = jnp.exp(lg * (BLOCK - 1.0 - tf))
    gam = jnp.exp(jnp.full((1, 1), BLOCK, F32) * lg)
    return dmask, dist, xi, zeta, gam, tf


def _ret_fwd(q, k, proj, lg, name):
    n = q.shape[0]
    nc = n // BLOCK
    hd = RET_HEAD_DIM
    chunk = lambda d, c: c + d * (nc - 1 - 2 * c)

    def body(lg_ref, q_ref, k_ref, v_ref, o_ref, st_ref, s_ref):
        h, d, c = pl.program_id(0), pl.program_id(1), pl.program_id(2)

        @pl.when(c == 0)
        def _():
            s_ref[...] = jnp.zeros_like(s_ref)

        dmask, _, xi, zeta, gam, _ = _ret_decay(lg_ref[h, d], d)
        qv = q_ref[...]
        kf = k_ref[...].astype(F32) * RET_K_SCALE
        vv = v_ref[...]
        s = lax.dot_general(qv, kf.astype(BF16), (((1,), (1,)), ((), ())), preferred_element_type=F32)
        sb = s_ref[...]
        o_ref[...] = (jnp.dot((s * dmask).astype(BF16), vv, preferred_element_type=F32)
                      + jnp.dot((qv.astype(F32) * xi).astype(BF16), sb.astype(BF16), preferred_element_type=F32))
        st_ref[...] = sb
        s_ref[...] = gam * sb + lax.dot_general((kf * zeta).astype(BF16), vv, (((0,), (0,)), ((), ())), preferred_element_type=F32)

    blk = lambda col: pl.BlockSpec((BLOCK, hd), lambda h, d, c: (chunk(d, c), col + h))
    return pl.pallas_call(
        body, grid=(RET_HEADS, 2, nc), in_specs=[pl.BlockSpec(memory_space=pltpu.SMEM), blk(0), blk(0), blk(COL_RV)],
        out_specs=(pl.BlockSpec((None, BLOCK, hd), lambda h, d, c: (d, chunk(d, c), h)),
                   pl.BlockSpec((None, None, None, hd, hd), lambda h, d, c: (h, d, c, 0, 0))),
        out_shape=(_sds((2, n, RET_WIDTH), F32), _sds((RET_HEADS, 2, nc, hd, hd), F32)), scratch_shapes=[pltpu.VMEM((hd, hd), F32)], name=name,
        compiler_params=pltpu.CompilerParams(dimension_semantics=("parallel", "parallel", "arbitrary")))(lg, q, k, proj)


def _ret_bwd(q, k, proj, lg, do, states, name):
    n = q.shape[0]
    nc = n // BLOCK
    hd = RET_HEAD_DIM
    chunk = lambda d, r: (nc - 1 - r) + d * (2 * r - (nc - 1))

    def body(lg_ref, q_ref, k_ref, v_ref, do_ref, st_ref, dq_ref, dk_ref, dv_ref, dlg_ref, ds_ref):
        h, d, r = pl.program_id(0), pl.program_id(1), pl.program_id(2)

        @pl.when(r == 0)
        def _():
            ds_ref[...] = jnp.zeros_like(ds_ref)
            dlg_ref[...] = jnp.zeros_like(dlg_ref)

        dmask, dist, xi, zeta, gam, tf = _ret_decay(lg_ref[h, d], d)
        qv, vv, dov = q_ref[...], v_ref[...], do_ref[...]
        qf = qv.astype(F32)
        kf = k_ref[...].astype(F32) * RET_K_SCALE
        kb = kf.astype(BF16)
        sc = st_ref[...]
        dsn = ds_ref[...]
        nt = (((1,), (1,)), ((), ()))
        tn = (((0,), (0,)), ((), ()))
        s = lax.dot_general(qv, kb, nt, preferred_element_type=F32)
        dsc = lax.dot_general(dov, vv, nt, preferred_element_type=F32) * dmask
        dsb = dsc.astype(BF16)
        dq_c = xi * lax.dot_general(dov, sc.astype(BF16), nt, preferred_element_type=F32)
        dk_c = zeta * lax.dot_general(vv, dsn.astype(BF16), nt, preferred_element_type=F32)
        dq = jnp.dot(dsb, kb, preferred_element_type=F32) + dq_c
        dk = lax.dot_general(dsb, qv, tn, preferred_element_type=F32) + dk_c
        dv = (lax.dot_general((s * dmask).astype(BF16), dov, tn, preferred_element_type=F32)
              + jnp.dot((kf * zeta).astype(BF16), dsn.astype(BF16), preferred_element_type=F32))
        ds_ref[...] = gam * dsn + lax.dot_general((qf * xi).astype(BF16), dov, tn, preferred_element_type=F32)
        dlg = (jnp.sum(dsc * s * dist, keepdims=True)
               + jnp.sum((tf + 1.0) * jnp.sum(qf * dq_c, axis=-1, keepdims=True), keepdims=True)
               + jnp.sum((BLOCK - 1.0 - tf) * jnp.sum(kf * dk_c, axis=-1, keepdims=True), keepdims=True)
               + BLOCK * gam * jnp.sum(dsn * sc, keepdims=True))
        dlg_ref[...] += dlg
        row = lax.broadcasted_iota(jnp.int32, (BLOCK, 1), 0) + chunk(d, r) * BLOCK
        keep = row >= PAD_FRONT
        dq_ref[...] = dq
        dk_ref[...] = jnp.where(keep, dk * RET_K_SCALE, 0.0)
        dv_ref[...] = jnp.where(keep, dv, 0.0)

    blk = lambda col: pl.BlockSpec((BLOCK, hd), lambda h, d, r: (chunk(d, r), col + h))
    plane = pl.BlockSpec((None, BLOCK, hd), lambda h, d, r: (d, chunk(d, r), h))
    return pl.pallas_call(
        body, grid=(RET_HEADS, 2, nc),
        in_specs=[pl.BlockSpec(memory_space=pltpu.SMEM), blk(0), blk(0), blk(COL_RV), blk(0),
                  pl.BlockSpec((None, None, None, hd, hd), lambda h, d, r: (h, d, nc - 1 - r, 0, 0))],
        out_specs=(plane, plane, plane, pl.BlockSpec((None, None, 8, 128), lambda h, d, r: (h, d, 0, 0))),
        out_shape=(_sds((2, n, RET_WIDTH), F32),) * 3 + (_sds((RET_HEADS, 2, 8, 128), F32),), scratch_shapes=[pltpu.VMEM((hd, hd), F32)], name=name,
        compiler_params=pltpu.CompilerParams(dimension_semantics=("parallel", "parallel", "arbitrary")))(lg, q, k, proj, do, states)


def _retgate_fwd(o, proj, gain, name):
    _, n, _ = o.shape
    tr = _row_tile(n, 1056)
    hd = RET_HEAD_DIM

    def body(o_ref, rg_ref, g_ref, y_ref):
        ov = o_ref[0] + o_ref[1]
        r = lax.rsqrt(jnp.mean(ov * ov, axis=-1, keepdims=True) + EPS)
        y_ref[...] = (_silu(rg_ref[...].astype(F32)) * (ov * r * g_ref[...])).astype(y_ref.dtype)

    return pl.pallas_call(body, grid=(n // tr, RET_HEADS),
                          in_specs=[pl.BlockSpec((2, tr, hd), lambda i, h: (0, i, h)), pl.BlockSpec((tr, hd), lambda i, h: (i, COL_RG + h)),
                                    pl.BlockSpec((1, hd), lambda i, h: (0, h))],
                          out_specs=pl.BlockSpec((tr, hd), lambda i, h: (i, h)), out_shape=_sds((n, RET_WIDTH), BF16), name=name)(o, proj, gain)


def _retgate_bwd(dmixed, o, proj, gain, name):
    _, n, _ = o.shape
    tr = _row_tile(n, 1056)
    hd = RET_HEAD_DIM

    def body(dy_ref, o_ref, rg_ref, g_ref, do_ref, drg_ref, dg_ref):
        i = pl.program_id(1)
        ov = o_ref[0] + o_ref[1]
        r = lax.rsqrt(jnp.mean(ov * ov, axis=-1, keepdims=True) + EPS)
        xhat = ov * r
        rg = rg_ref[...].astype(F32)
        dy = dy_ref[...].astype(F32)
        drg_ref[...] = (dy * (xhat * g_ref[...]) * _dsilu(rg)).astype(drg_ref.dtype)
        dn = dy * _silu(rg)
        dxh = dn * g_ref[...]
        do_ref[...] = (r * (dxh - xhat * jnp.mean(dxh * xhat, axis=-1, keepdims=True))).astype(do_ref.dtype)

        @pl.when(i == 0)
        def _():
            dg_ref[...] = jnp.zeros_like(dg_ref)

        dg_ref[...] += jnp.sum(dn * xhat, axis=0, keepdims=True)

    tile = pl.BlockSpec((tr, hd), lambda h, i: (i, h))
    vec = pl.BlockSpec((1, hd), lambda h, i: (0, h))
    return pl.pallas_call(body, grid=(RET_HEADS, n // tr),
                          in_specs=[pl.BlockSpec((tr, hd), lambda h, i: (i, RET_HEADS + h)), pl.BlockSpec((2, tr, hd), lambda h, i: (0, i, h)),
                                    pl.BlockSpec((tr, hd), lambda h, i: (i, COL_RG + h)), vec],
                          out_specs=(tile, tile, vec), out_shape=(_sds((n, RET_WIDTH), BF16), _sds((n, RET_WIDTH), BF16), _sds((1, RET_WIDTH), F32)),
                          name=name, compiler_params=pltpu.CompilerParams(dimension_semantics=("parallel", "arbitrary")))(dmixed, o, proj, gain)


def _swiglu_fwd(gate, up, name):
    n, f = gate.shape
    tr, tc = _row_tile(n, 384), f // 2

    def body(g_ref, u_ref, o_ref):
        o_ref[...] = (_silu(g_ref[...].astype(F32)) * u_ref[...].astype(F32)).astype(o_ref.dtype)

    t = pl.BlockSpec((tr, tc), lambda i, j: (i, j))
    return pl.pallas_call(body, grid=(n // tr, 2), in_specs=[t, t], out_specs=t, out_shape=_sds((n, f), BF16), name=name)(gate, up)


def _swiglu_bwd(df, gate, up, name):
    n, f = gate.shape
    tr, tc = _row_tile(n, 384), f // 2

    def body(d_ref, g_ref, u_ref, dg_ref, du_ref):
        d, g = d_ref[...].astype(F32), g_ref[...].astype(F32)
        dg_ref[...] = (d * u_ref[...].astype(F32) * _dsilu(g)).astype(dg_ref.dtype)
        du_ref[...] = (d * _silu(g)).astype(du_ref.dtype)

    t = pl.BlockSpec((tr, tc), lambda i, j: (i, j))
    return pl.pallas_call(body, grid=(n // tr, 2), in_specs=[t, t, t], out_specs=(t, t), out_shape=(_sds((n, f), BF16),) * 2, name=name)(df, gate, up)


def _loss_head(h, target, name):
    n, d = h.shape
    nb = n // BLOCK

    def body(h_ref, t_ref, dh_ref, l_ref):
        i = pl.program_id(0)

        @pl.when(i == 0)
        def _():
            l_ref[...] = jnp.zeros_like(l_ref)
            dh_ref[...] = jnp.zeros_like(dh_ref)

        @pl.when(i > 0)
        def _():
            e = h_ref[...] - t_ref[...]
            dh_ref[...] = e * (1.0 / d)
            l_ref[...] += 0.5 * jnp.sum(jnp.mean(e * e, axis=-1, keepdims=True), keepdims=True)

    blk = pl.BlockSpec((BLOCK, d), lambda i: (i, 0))
    return pl.pallas_call(body, grid=(nb,), in_specs=[blk, pl.BlockSpec((BLOCK, d), lambda i: (jnp.maximum(i - 1, 0), 0))],
                          out_specs=(blk, pl.BlockSpec((8, 128), lambda i: (0, 0))), out_shape=(_sds((n, d), F32), _sds((8, 128), F32)), name=name,
                          compiler_params=pltpu.CompilerParams(dimension_semantics=("arbitrary",)))(h, target)


def _adamw(parts, w, m, v, name, sel=None, layer=None, prev=None):
    s, (r, c) = parts.shape[0], parts.shape[-2:]
    tr = _row_tile(r, max(16, (ADAMW_TILE_ELEMS // c) // 16 * 16))
    b1c, b2c = 1.0 - ADAM_B1 ** ADAM_STEP, 1.0 - ADAM_B2 ** ADAM_STEP

    def body(p_ref, w_ref, m_ref, v_ref, *rest):
        g_ref, d_ref, mo_ref, vo_ref = rest[-4:]
        g = p_ref[0].astype(F32)
        for q in range(1, s):
            g = g + p_ref[q].astype(F32)
        mn = ADAM_B1 * m_ref[...] + (1.0 - ADAM_B1) * g
        vn = ADAM_B2 * v_ref[...] + (1.0 - ADAM_B2) * jnp.square(g)
        g_ref[...] = g
        mo_ref[...] = mn
        vo_ref[...] = vn
        d_ref[...] = -ADAM_LR * ((mn / b1c) / (jnp.sqrt(vn / b2c) + ADAM_EPS) + ADAM_WD * w_ref[...])

    pspec = (pl.BlockSpec((s, tr, c), lambda i: (0, i, 0)) if sel is None else pl.BlockSpec((s, None, tr, c), lambda i: (0, sel, i, 0)))
    if layer is None:
        t = pl.BlockSpec((tr, c), lambda i: (i, 0))
        return pl.pallas_call(body, grid=(r // tr,), in_specs=[pspec, t, t, t], out_specs=(t, t, t, t), out_shape=(_sds((r, c), F32),) * 4,
                              name=name)(parts, w, m, v)
    t = pl.BlockSpec((None, tr, c), lambda i: (layer, i, 0))
    prev = prev if prev is not None else tuple(lax.empty(w.shape, F32) for _ in range(4))
    return pl.pallas_call(body, grid=(r // tr,), in_specs=[pspec, t, t, t] + [ANY] * 4, out_specs=(t, t, t, t), out_shape=(_sds(w.shape, F32),) * 4,
                          input_output_aliases={4 + i: i for i in range(4)}, name=name)(parts, w, m, v, *prev)


def _allgather(xs, name):
    na = len(xs)

    def body(*refs):
        x_refs, o_refs = refs[:na], refs[na:2 * na]
        send, recv, lsem = refs[2 * na:]
        x, y, c = lax.axis_index("x"), lax.axis_index("y"), lax.axis_index("c")
        me, sib = (x, y, c), (x, y, 1 - c)
        chips = [(1 - x, y), (x, 1 - y), (1 - x, 1 - y)]
        slot = lambda p: 4 * p[0] + 2 * p[1] + p[2]

        def copy(a, k, block, to, src=None):
            dst = o_refs[a].at[slot(block)]
            return pltpu.make_async_remote_copy(src_ref=dst if src is None else src, dst_ref=dst, send_sem=send.at[a, k], recv_sem=recv.at[a, k],
                                                device_id=to, device_id_type=MESH)

        mine = [pltpu.make_async_copy(x_refs[a], o_refs[a].at[slot(me)], lsem.at[a]) for a in range(na)]
        for cp in mine:
            cp.start()
        first = []
        for a in range(na):
            first.append(copy(a, 0, me, sib, src=x_refs[a]))
            first += [copy(a, 1 + j, me, (*chip, c), src=x_refs[a]) for j, chip in enumerate(chips)]
        for cp in first:
            cp.start()
        passed = []
        for j, chip in enumerate(chips):
            for a in range(na):
                copy(a, 1 + j, (*chip, c), me).wait_recv()
                passed.append(copy(a, 4 + j, (*chip, c), sib))
                passed[-1].start()
        for a in range(na):
            copy(a, 0, sib, me).wait_recv()
            for j, chip in enumerate(chips):
                copy(a, 4 + j, (*chip, 1 - c), me).wait_recv()
        for cp in first + passed:
            cp.wait_send()
        for cp in mine:
            cp.wait()

    return pl.pallas_call(body, in_specs=[ANY] * na, out_specs=[ANY] * na, out_shape=[_sds((N_DEV,) + t.shape, t.dtype) for t in xs],
                          scratch_shapes=[pltpu.SemaphoreType.DMA((na, 7)), pltpu.SemaphoreType.DMA((na, 7)), pltpu.SemaphoreType.DMA((na,))],
                          name=name)(*xs)


HBM = pl.BlockSpec(memory_space=pltpu.HBM)
SEM = pl.BlockSpec(memory_space=pltpu.SEMAPHORE)
EFFECT = pltpu.SideEffectType.DATAFLOW_SIDE_EFFECTING


def _split_copies(x_refs, land_refs, send, recv, own, scatter, landing):
    x, y, c = lax.axis_index("x"), lax.axis_index("y"), lax.axis_index("c")
    me = 4 * x + 2 * y + c
    local, remote = [], []
    for a in range(len(x_refs)):
        local.append(pltpu.make_async_copy(x_refs[a].at[me] if scatter else x_refs[a], land_refs[a].at[me], own.at[a]))
        for r in range(1, N_DEV):
            peer = ((1 - x if r & 4 else x), (1 - y if r & 2 else y), (1 - c if r & 1 else c))
            slot = 4 * peer[0] + 2 * peer[1] + peer[2]
            remote.append(pltpu.make_async_remote_copy(src_ref=x_refs[a].at[slot] if scatter else x_refs[a],
                                                       dst_ref=land_refs[a].at[slot if landing else me], send_sem=send.at[7 * a + r - 1],
                                                       recv_sem=recv.at[7 * a + r - 1], device_id=peer, device_id_type=MESH))
    return local, remote


def _send_start(xs, scatter, after, name):
    na = len(xs)
    lands = [lax.empty(t.shape if scatter else (N_DEV,) + t.shape, t.dtype) for t in xs]

    def body(*refs):
        x_refs, land_refs = refs[:na], refs[na:2 * na]
        send, recv, own, token = refs[2 * na + 1], refs[2 * na + 2], refs[2 * na + 3], refs[-1]
        local, remote = _split_copies(x_refs, land_refs, send, recv, own, scatter, False)
        for cp in remote + local:
            cp.start()
        token[...] = jnp.zeros_like(token)

    hbm = lambda t: pltpu.with_memory_space_constraint(t, pltpu.HBM)
    outs = pl.pallas_call(
        body, name=name,
        out_shape=(pltpu.SemaphoreType.DMA((7 * na,)), pltpu.SemaphoreType.DMA((7 * na,)), pltpu.SemaphoreType.DMA((na,)),
                   *[pltpu.HBM(t.shape, t.dtype) for t in xs], *[pltpu.HBM(t.shape, t.dtype) for t in lands], _sds((8, 128), F32)),
        in_specs=[HBM] * (2 * na) + [ANY], out_specs=(SEM, SEM, SEM, *[HBM] * (2 * na), pl.BlockSpec(memory_space=pltpu.VMEM)),
        input_output_aliases={i: 3 + i for i in range(2 * na)},
        compiler_params=pltpu.CompilerParams(has_side_effects=EFFECT))(*[hbm(t) for t in xs], *[hbm(t) for t in lands], after)
    return outs[:3], list(outs[3:3 + na]), list(outs[3 + na:3 + 2 * na]), outs[-1]


def _send_wait(started, scatter, after, name):
    sems, xs, lands, _ = started
    na = len(xs)

    def body(*refs):
        local, remote = _split_copies(refs[:na], refs[na:2 * na], refs[2 * na], refs[2 * na + 1], refs[2 * na + 2], scatter, True)
        for cp in remote:
            cp.wait_send()
            cp.wait_recv()
        for cp in local:
            cp.wait()

    outs = pl.pallas_call(body, name=name, out_shape=tuple(pltpu.HBM(t.shape, t.dtype) for t in xs + lands),
                          in_specs=[HBM] * (2 * na) + [SEM, SEM, SEM, ANY], out_specs=[HBM] * (2 * na),
                          input_output_aliases={i: i for i in range(2 * na)},
                          compiler_params=pltpu.CompilerParams(has_side_effects=EFFECT))(*xs, *lands, *sems, after)
    return list(outs[na:])


def _local_step(x, meta, target, mix_weights_fn, ffn_weights_fn, grads_fn, sink, dec_f, dec_b, ret_norm, n_mix_pre, n_mix_post, n_ffn_pre,
                n_ffn_post):
    depth = n_mix_pre.shape[0]
    d = D_MODEL
    h = jnp.concatenate([jnp.zeros((PAD_FRONT, d), F32), meta, x], axis=0)
    n = h.shape[0]
    cos_a, sin_a, cos_r, sin_r, perm = _rope_tables(n)
    lg_all = jnp.stack([-jnp.exp(dec_f), -jnp.exp(dec_b)], axis=-1)
    saved = []
    for l in range(depth):
        t = f"l{l}_"
        sink_b = jnp.broadcast_to(sink[l][:, None], (ATT_HEADS, 128))
        wi, wo, tok = mix_weights_fn(l, h)
        u = _norm_fwd(h, (n_mix_pre[l] + tok)[None], None, BF16, t + "norm_mix_pre")
        proj = _mm_nn(u, wi, BF16, WIDE_TILE, d, t + "proj")
        aq = _rope_att(proj, 0, ATT_HEADS, cos_a, sin_a, perm, t + "rope_aq")
        ak = _rope_att(proj, COL_AK, ATT_KV_HEADS, cos_a, sin_a, perm, t + "rope_ak")
        att = _att_fwd(aq, ak, proj, sink_b, t + "att")
        proj3 = proj[None]
        rq = _rope_ret(proj3, COL_RQ, cos_r, sin_r, t + "rope_rq")
        rk = _rope_ret(proj3, COL_RK, cos_r, sin_r, t + "rope_rk")
        o_ret, states = _ret_fwd(rq, rk, proj, lg_all[l], t + "ret")
        retg = _retgate_fwd(o_ret, proj, ret_norm[l][None], t + "retgate")
        mixed = jnp.concatenate([att, retg], axis=1)
        mo = _mm_nn(mixed, wo, F32, 1024, d, t + "out_proj")
        h_mid = _norm_fwd(mo, n_mix_post[l][None], h, F32, t + "norm_mix_post")
        wg, wu, wd, tok = ffn_weights_fn(l, h_mid)
        u2 = _norm_fwd(h_mid, (n_ffn_pre[l] + tok)[None], None, BF16, t + "norm_ffn_pre")
        gate = _mm_nn(u2, wg, BF16, WIDE_TILE, d, t + "gate")
        up = _mm_nn(u2, wu, BF16, WIDE_TILE, d, t + "up")
        f = _swiglu_fwd(gate, up, t + "swiglu")
        dn = _mm_nn(f, wd, F32, 1024, WIDE_TILE, t + "down")
        h_out = _norm_fwd(dn, n_ffn_post[l][None], h_mid, F32, t + "norm_ffn_post")
        saved.append(dict(h=h, u=u, proj=proj, aq=aq, ak=ak, rq=rq, rk=rk, o_ret=o_ret, states=states, mixed=mixed, mo=mo, h_mid=h_mid, u2=u2,
                          gate=gate, up=up, f=f, dn=dn, sink_b=sink_b, wi=wi, wo=wo, wg=wg, wu=wu, wd=wd))
        h = h_out

    dh, loss_part = _loss_head(h, target, "loss_head")
    gs = dict(sink=[None] * depth, dec_f=[None] * depth, dec_b=[None] * depth, ret_norm=[None] * depth, mix_pre=[None] * depth,
              mix_post=[None] * depth, ffn_pre=[None] * depth, ffn_post=[None] * depth)
    tok_b = jnp.zeros((), F32)
    for l in reversed(range(depth)):
        t = f"l{l}_b_"
        sv = saved[l]
        proj = sv["proj"]
        d_dn, gs["ffn_post"][l] = _norm_bwd(dh, sv["dn"], (n_ffn_post[l] + tok_b)[None], None, BF16, t + "norm_ffn_post")
        gw = {}
        d_f = _mm_nt(d_dn, sv["wd"], BF16, WIDE_TILE, d, t + "d_f")
        gw["wd"] = _mm_tn(sv["f"], d_dn, WIDE_TILE, 1024, t + "dw_down")
        d_gate, d_up = _swiglu_bwd(d_f, sv["gate"], sv["up"], t + "swiglu")
        du2 = _mm_nt(d_gate, sv["wg"], F32, 1024, WIDE_TILE, t + "du2_gate")
        du2 = _mm_nt(d_up, sv["wu"], F32, 1024, WIDE_TILE, t + "du2_up", acc=du2)
        gw["wg"] = _mm_tn(sv["u2"], d_gate, 1024, WIDE_TILE, t + "dw_gate")
        gw["wu"] = _mm_tn(sv["u2"], d_up, 1024, WIDE_TILE, t + "dw_up")
        tok_b = grads_fn(l, "ffn", gw, du2)
        dh, gs["ffn_pre"][l] = _norm_bwd(du2, sv["h_mid"], (n_ffn_pre[l] + tok_b)[None], dh, F32, t + "norm_ffn_pre")
        d_mo, gs["mix_post"][l] = _norm_bwd(dh, sv["mo"], n_mix_post[l][None], None, BF16, t + "norm_mix_post")
        d_mixed = _mm_nt(d_mo, sv["wo"], BF16, 1024, d, t + "d_mixed")
        gw["wo"] = _mm_tn(sv["mixed"], d_mo, 1024, 1024, t + "dw_out")
        d_o, d_rg, gs["ret_norm"][l] = _retgate_bwd(d_mixed, sv["o_ret"], proj, ret_norm[l][None], t + "retgate")
        dq_r, dk_r, dv_r, dlg = _ret_bwd(sv["rq"], sv["rk"], proj, lg_all[l], d_o, sv["states"], t + "ret")
        draw = dlg[:, :, 0, 0] * lg_all[l]
        gs["dec_f"][l], gs["dec_b"][l] = draw[:, 0], draw[:, 1]
        dq_a, dk_a, dv_a, dsink = _att_bwd(sv["aq"], sv["ak"], proj, sv["sink_b"], d_mixed, t + "att")
        gs["sink"][l] = dsink[:, 0]
        dproj = jnp.concatenate([
            _rope_att(dq_a, 0, ATT_HEADS, cos_a, -sin_a, perm, t + "rope_aq"),
            _rope_att(dk_a, 0, ATT_KV_HEADS, cos_a, -sin_a, perm, t + "rope_ak"),
            dv_a.astype(BF16),
            _rope_ret(dq_r, 0, cos_r, -sin_r, t + "rope_rq"),
            _rope_ret(dk_r, 0, cos_r, -sin_r, t + "rope_rk"),
            _rope_ret(dv_r, 0, None, None, t + "sum_rv"),
            d_rg], axis=1)
        du = _mm_nt(dproj, sv["wi"], F32, 1024, WIDE_TILE, t + "du")
        gw["wi"] = _mm_tn(sv["u"], dproj, 1024, WIDE_TILE, t + "dw_in")
        dh, gs["mix_pre"][l] = _norm_bwd(du, sv["h"], n_mix_pre[l][None], dh, F32, t + "norm_mix_pre")
        tok_b = grads_fn(l, "mix", gw, dh)
    return loss_part[0, 0], dh, gs


def _cols_from_blocks(g):
    return g.transpose(1, 0, 2).reshape(g.shape[1], N_DEV * SHARD_COLS)


def _blocks_from_cols(w):
    return w.reshape(w.shape[0], N_DEV, SHARD_COLS).transpose(1, 0, 2)


def _pack_small(mix_pre, mix_post, ffn_pre, ffn_post, ret_norm, sink, dec_f, dec_b, loss, meta):
    d = D_MODEL

    def tile(a, rows=8):
        a = jnp.reshape(a, (-1, a.shape[-1])) if a.ndim else jnp.reshape(a, (1, 1))
        return jnp.pad(a, ((0, rows - a.shape[0]), (0, d - a.shape[1])))

    return jnp.concatenate([tile(mix_pre), tile(mix_post), tile(ffn_pre), tile(ffn_post), tile(ret_norm.reshape(-1, d)), tile(sink), tile(dec_f),
                            tile(dec_b), tile(loss), tile(meta, SMALL_ROWS - ROW_META)], axis=0)


def _unpack_small(p, depth):
    rows = lambda r0, cols: p[r0:r0 + depth, :cols]
    return dict(mix_pre=rows(ROW_MIX_PRE, D_MODEL), mix_post=rows(ROW_MIX_POST, D_MODEL), ffn_pre=rows(ROW_FFN_PRE, D_MODEL),
                ffn_post=rows(ROW_FFN_POST, D_MODEL), ret_norm=p[ROW_RET_NORM:ROW_RET_NORM + depth * RET_WIDTH // D_MODEL].reshape(depth, RET_WIDTH),
                sink=rows(ROW_SINK, ATT_HEADS), dec_f=rows(ROW_DEC_F, RET_HEADS), dec_b=rows(ROW_DEC_B, RET_HEADS), loss=p[ROW_LOSS, 0])


def kernel(x, meta_tokens, w_in, w_out, attn_sink, ret_decay_fwd, ret_decay_bwd, ret_norm, norm_mix_pre, norm_mix_post, w_gate, w_up, w_down, norm_ffn_pre, norm_ffn_post, loss_target, m_meta_tokens, m_w_in, m_w_out, m_attn_sink, m_ret_decay_fwd, m_ret_decay_bwd, m_ret_norm, m_norm_mix_pre, m_norm_mix_post, m_w_gate, m_w_up, m_w_down, m_norm_ffn_pre, m_norm_ffn_post, v_meta_tokens, v_w_in, v_w_out, v_attn_sink, v_ret_decay_fwd, v_ret_decay_bwd, v_ret_norm, v_norm_mix_pre, v_norm_mix_post, v_w_gate, v_w_up, v_w_down, v_norm_ffn_pre, v_norm_ffn_post):
    depth, d = w_in.shape[0], D_MODEL
    me = 4 * lax.axis_index("x") + 2 * lax.axis_index("y") + lax.axis_index("c")
    zero = jnp.zeros((), F32)

    meta_g, = _allgather([meta_tokens], "gather_meta")
    meta = meta_g.transpose(1, 0, 2).reshape(N_META, d)

    def shards(k):
        l = k // 2
        if k % 2 == 0:
            return [w_in[l].astype(BF16), w_out[l].astype(BF16)]
        return [jnp.stack([w_gate[l], w_up[l]]).astype(BF16), w_down[l].astype(BF16)]

    gathers, ahead = {}, 2
    for k in range(min(ahead + 1, 2 * depth)):
        gathers[k] = _send_start(shards(k), False, gathers[k - 1][3] if k else meta_g, f"gather_start_g{k}")

    def take(k, h):
        if k >= 1 and k + ahead < 2 * depth:
            gathers[k + ahead] = _send_start(shards(k + ahead), False, h, f"gather_start_g{k + ahead}")
        latest = max(gathers)
        return _send_wait(gathers.pop(k), False, gathers[latest][3] if latest > k else h, f"gather_wait_g{k}")

    def mix_weights_fn(l, h):
        cols, rows_o = take(2 * l, h)
        return _cols_from_blocks(cols), rows_o.reshape(d, d), zero

    def ffn_weights_fn(l, h):
        cols, rows_d = take(2 * l + 1, h)
        return _cols_from_blocks(cols[:, 0]), _cols_from_blocks(cols[:, 1]), rows_d.reshape(D_FF, d), zero

    exchanges, adam, order = {}, {}, []
    big = dict(wi=(w_in, m_w_in, v_w_in), wg=(w_gate, m_w_gate, v_w_gate), wu=(w_up, m_w_up, v_w_up), wd=(w_down, m_w_down, v_w_down),
               wo=(w_out, m_w_out, v_w_out))

    def finish(key, after):
        l, part = key
        a, b = _send_wait(exchanges.pop(key), True, after, f"exchange_wait_{part}_l{l}")
        todo = (("wg", a, 0), ("wu", a, 1), ("wd", b, None)) if part == "ffn" else (("wi", a, None), ("wo", b, None))
        for kind, parts, sel in todo:
            adam[kind] = _adamw(parts, *big[kind], f"adamw_{kind}_l{l}", sel=sel, layer=l, prev=adam.get(kind))

    def grads_fn(l, part, gw, after):
        if part == "ffn":
            packed = [jnp.stack([_blocks_from_cols(gw["wg"]), _blocks_from_cols(gw["wu"])], axis=1), gw["wd"].reshape(N_DEV, SHARD_COLS, d)]
        else:
            packed = [_blocks_from_cols(gw["wi"]), gw["wo"].reshape(N_DEV, d // N_DEV, d)]
        exchanges[(l, part)] = _send_start(packed, True, after, f"exchange_start_{part}_l{l}")
        order.append((l, part))
        token = exchanges[(l, part)][3]
        if len(order) > 2:
            finish(order[-3], token)
        return token[0, 0]

    loss_part, dh, gs = _local_step(x[0], meta, loss_target[0], mix_weights_fn, ffn_weights_fn, grads_fn, attn_sink, ret_decay_fwd, ret_decay_bwd, ret_norm,
                                    norm_mix_pre, norm_mix_post, norm_ffn_pre, norm_ffn_post)
    grad_x = dh[BLOCK:][None]

    st = lambda xs: jnp.stack([t.reshape(-1) if t.ndim == 1 else t[0] for t in xs])
    small = _pack_small(st(gs["mix_pre"]), st(gs["mix_post"]), st(gs["ffn_pre"]), st(gs["ffn_post"]), st(gs["ret_norm"]), st(gs["sink"]),
                        st(gs["dec_f"]), st(gs["dec_b"]), loss_part, dh[PAD_FRONT:BLOCK])
    small_g, = _allgather([small], "gather_small")
    for key in order[-2:]:
        finish(key, small_g)
    o_wi, o_wo, o_wg, o_wu, o_wd = adam["wi"], adam["wo"], adam["wg"], adam["wu"], adam["wd"]
    zmeta = jnp.zeros((N_META, d), F32)
    packs = [_pack_small(a[0], a[1], a[2], a[3], a[4], a[5], a[6], a[7], zero, zmeta) for a in (
        (norm_mix_pre, norm_mix_post, norm_ffn_pre, norm_ffn_post, ret_norm, attn_sink, ret_decay_fwd, ret_decay_bwd),
        (m_norm_mix_pre, m_norm_mix_post, m_norm_ffn_pre, m_norm_ffn_post, m_ret_norm, m_attn_sink, m_ret_decay_fwd, m_ret_decay_bwd),
        (v_norm_mix_pre, v_norm_mix_post, v_norm_ffn_pre, v_norm_ffn_post, v_ret_norm, v_attn_sink, v_ret_decay_fwd, v_ret_decay_bwd))]
    o_small = [_unpack_small(o, depth) for o in _adamw(small_g, packs[0], packs[1], packs[2], "adamw_small")]
    meta_parts = lax.dynamic_slice(small_g, (0, ROW_META, me * (d // N_DEV)), (N_DEV, N_META, d // N_DEV))
    o_meta = _adamw(meta_parts, meta_tokens, m_meta_tokens, v_meta_tokens, "adamw_meta")

    outs = []
    for i in range(4):
        s = o_small[i]
        outs += [o_meta[i], o_wi[i], o_wo[i], s["sink"], s["dec_f"], s["dec_b"], s["ret_norm"], s["mix_pre"], s["mix_post"], o_wg[i], o_wu[i],
                 o_wd[i], s["ffn_pre"], s["ffn_post"]]
    return (o_small[0]["loss"], grad_x, *outs)
```

```python
import jax
import jax.numpy as jnp
import numpy as np
from jax import lax
from jax.experimental import pallas as pl
from jax.experimental.pallas import tpu as pltpu

F32, BF16 = jnp.float32, jnp.bfloat16

D_MODEL = 2048
N_META = 16
BLOCK = 128
WINDOW = 128
PAD_FRONT = BLOCK - N_META
ATT_HEAD_DIM = 128
ATT_WIDTH = D_MODEL // 2
ATT_HEADS = ATT_WIDTH // ATT_HEAD_DIM
ATT_KV_HEADS = 2
ATT_GROUP = ATT_HEADS // ATT_KV_HEADS
KV_WIDTH = ATT_KV_HEADS * ATT_HEAD_DIM
ROT_DIM = ATT_HEAD_DIM // 4
ROPE_THETA = 500000.0
RET_WIDTH = D_MODEL - ATT_WIDTH
RET_HEAD_DIM = 256
RET_HEADS = RET_WIDTH // RET_HEAD_DIM
RET_THETA = 10000.0
D_FF = 5632
IN_COLS = ATT_WIDTH + 2 * KV_WIDTH + 4 * RET_WIDTH
N_DEV = 8
SHARD_COLS = IN_COLS // N_DEV
EPS = 1e-6
NEG = -1e30
RET_K_SCALE = RET_HEAD_DIM ** -0.5
ATT_SCALE = ATT_HEAD_DIM ** -0.5

COL_AK = ATT_WIDTH // ATT_HEAD_DIM
COL_AV256 = (ATT_WIDTH + KV_WIDTH) // 256
COL_RQ = (ATT_WIDTH + 2 * KV_WIDTH) // RET_HEAD_DIM
COL_RK = COL_RQ + RET_HEADS
COL_RV = COL_RK + RET_HEADS
COL_RG = COL_RV + RET_HEADS

ADAM_LR, ADAM_B1, ADAM_B2, ADAM_EPS, ADAM_WD, ADAM_STEP = 0.001, 0.9, 0.999, 1e-08, 0.01, 10

ROW_MIX_PRE, ROW_MIX_POST, ROW_FFN_PRE, ROW_FFN_POST, ROW_RET_NORM, ROW_SINK, ROW_DEC_F, ROW_DEC_B, ROW_LOSS, ROW_META, SMALL_ROWS = (
    0, 8, 16, 24, 32, 40, 48, 56, 64, 72, 96)
ADAMW_TILE_ELEMS = 128 * 1024

MESH = pl.DeviceIdType.MESH
ANY = pl.BlockSpec(memory_space=pl.ANY)


def _row_tile(n, cap):
    for t in range(cap - cap % 16, 0, -16):
        if n % t == 0:
            return t
    raise ValueError(n)


def _sds(shape, dtype):
    return jax.ShapeDtypeStruct(shape, dtype)


def _silu(x):
    return x * jax.nn.sigmoid(x)


def _dsilu(x):
    s = jax.nn.sigmoid(x)
    return s * (1.0 + x * (1.0 - s))


def _norm_fwd(x, g, res, out_dtype, name):
    n, d = x.shape
    tr = _row_tile(n, 384)

    def body(*refs):
        if res is None:
            x_ref, g_ref, o_ref = refs
        else:
            x_ref, g_ref, r_ref, o_ref = refs
        xv = x_ref[...]
        r = lax.rsqrt(jnp.mean(xv * xv, axis=-1, keepdims=True) + EPS)
        y = xv * r * g_ref[...]
        if res is not None:
            y = y + r_ref[...]
        o_ref[...] = y.astype(o_ref.dtype)

    row = pl.BlockSpec((tr, d), lambda i: (i, 0))
    ins = [row, pl.BlockSpec((1, d), lambda i: (0, 0))] + ([row] if res is not None else [])
    args = (x, g) + ((res,) if res is not None else ())
    return pl.pallas_call(body, grid=(n // tr,), in_specs=ins, out_specs=row, out_shape=_sds((n, d), out_dtype), name=name)(*args)


def _norm_bwd(dy, x, g, res, out_dtype, name):
    n, d = x.shape
    tr = _row_tile(n, 384)

    def body(*refs):
        if res is None:
            dy_ref, x_ref, g_ref, dx_ref, dg_ref = refs
        else:
            dy_ref, x_ref, g_ref, r_ref, dx_ref, dg_ref = refs
        i = pl.program_id(0)
        xv = x_ref[...]
        r = lax.rsqrt(jnp.mean(xv * xv, axis=-1, keepdims=True) + EPS)
        xhat = xv * r
        dyf = dy_ref[...].astype(F32)
        gdy = dyf * g_ref[...]
        dx = r * (gdy - xhat * jnp.mean(gdy * xhat, axis=-1, keepdims=True))
        if res is not None:
            dx = dx + r_ref[...]
        dx_ref[...] = dx.astype(dx_ref.dtype)

        @pl.when(i == 0)
        def _():
            dg_ref[...] = jnp.zeros_like(dg_ref)

        dg_ref[...] += jnp.sum(dyf * xhat, axis=0, keepdims=True)

    row = pl.BlockSpec((tr, d), lambda i: (i, 0))
    vec = pl.BlockSpec((1, d), lambda i: (0, 0))
    ins = [row, row, vec] + ([row] if res is not None else [])
    args = (dy, x, g) + ((res,) if res is not None else ())
    return pl.pallas_call(body, grid=(n // tr,), in_specs=ins, out_specs=(row, vec),
                          out_shape=(_sds((n, d), out_dtype), _sds((1, d), F32)), name=name,
                          compiler_params=pltpu.CompilerParams(dimension_semantics=("arbitrary",)))(*args)


def _mm(a, b, *, ta, tb, grid, a_blk, a_map, b_blk, b_map, o_blk, o_map, o_shape, o_dtype, name, acc=None):
    nk = grid[2]
    dims = (((0,) if ta else (1,), (1,) if tb else (0,)), ((), ()))

    def body(*refs):
        if acc is None:
            a_ref, b_ref, o_ref = refs[:3]
            c_ref = None
        else:
            a_ref, b_ref, c_ref, o_ref = refs[:4]
        part = lax.dot_general(a_ref[...], b_ref[...], dims, preferred_element_type=F32)
        if nk == 1:
            if c_ref is not None:
                part = part + c_ref[...].astype(F32)
            o_ref[...] = part.astype(o_ref.dtype)
            return
        acc_ref = refs[-1]
        k = pl.program_id(2)

        @pl.when(k == 0)
        def _():
            acc_ref[...] = jnp.zeros_like(acc_ref) if c_ref is None else c_ref[...].astype(F32)

        acc_ref[...] += part

        @pl.when(k == nk - 1)
        def _():
            o_ref[...] = acc_ref[...].astype(o_ref.dtype)

    ins = [pl.BlockSpec(a_blk, a_map), pl.BlockSpec(b_blk, b_map)]
    args = [a, b]
    if acc is not None:
        ins.append(pl.BlockSpec(o_blk, o_map))
        args.append(acc)
    return pl.pallas_call(body, grid=grid, in_specs=ins, out_specs=pl.BlockSpec(o_blk, o_map), out_shape=_sds(o_shape, o_dtype),
                          scratch_shapes=[pltpu.VMEM(o_blk, F32)] if nk > 1 else [], name=name,
                          compiler_params=pltpu.CompilerParams(dimension_semantics=("parallel", "parallel", "arbitrary")))(*args)


TOKEN_TILE = 1056
WIDE_TILE = 1408


def _mm_nn(x, w, o_dtype, tn, tk, name, acc=None):
    n, k = x.shape
    tm = _row_tile(n, TOKEN_TILE)
    return _mm(x, w, ta=False, tb=False, grid=(n // tm, w.shape[1] // tn, k // tk), a_blk=(tm, tk), a_map=lambda i, j, kk: (i, kk),
               b_blk=(tk, tn), b_map=lambda i, j, kk: (kk, j), o_blk=(tm, tn), o_map=lambda i, j, kk: (i, j),
               o_shape=(n, w.shape[1]), o_dtype=o_dtype, name=name, acc=acc)


def _mm_nt(dy, w, o_dtype, tn, tk, name, acc=None):
    n, k = dy.shape
    tm = _row_tile(n, TOKEN_TILE)
    return _mm(dy, w, ta=False, tb=True, grid=(n // tm, w.shape[0] // tn, k // tk), a_blk=(tm, tk), a_map=lambda i, j, kk: (i, kk),
               b_blk=(tn, tk), b_map=lambda i, j, kk: (j, kk), o_blk=(tm, tn), o_map=lambda i, j, kk: (i, j),
               o_shape=(n, w.shape[0]), o_dtype=o_dtype, name=name, acc=acc)


def _mm_tn(x, dy, tm, tn, name):
    n, m = x.shape
    tk = _row_tile(n, TOKEN_TILE)
    return _mm(x, dy, ta=True, tb=False, grid=(m // tm, dy.shape[1] // tn, n // tk), a_blk=(tk, tm), a_map=lambda i, j, kk: (kk, i),
               b_blk=(tk, tn), b_map=lambda i, j, kk: (kk, j), o_blk=(tm, tn), o_map=lambda i, j, kk: (i, j),
               o_shape=(m, dy.shape[1]), o_dtype=BF16, name=name)


def _rope_tables(n):
    pos = (jnp.arange(n) - PAD_FRONT).astype(F32)
    half = ROT_DIM // 2
    ang = pos[:, None] * (ROPE_THETA ** (-jnp.arange(half, dtype=F32) / half))[None, :]
    c, s = jnp.cos(ang), jnp.sin(ang)
    rest = ATT_HEAD_DIM - ROT_DIM
    cos_a = jnp.concatenate([c, c, jnp.ones((n, rest), F32)], axis=1)
    sin_a = jnp.concatenate([-s, s, jnp.zeros((n, rest), F32)], axis=1)
    half = RET_HEAD_DIM // 2
    ang = pos[:, None] * (RET_THETA ** (-jnp.arange(half, dtype=F32) / half))[None, :]
    c, s = jnp.cos(ang), jnp.sin(ang)
    perm = np.zeros((ATT_HEAD_DIM, ATT_HEAD_DIM), np.float32)
    for i in range(ROT_DIM):
        perm[(i + ROT_DIM // 2) % ROT_DIM, i] = 1.0
    return cos_a, sin_a, jnp.concatenate([c, c], axis=1), jnp.concatenate([-s, s], axis=1), jnp.asarray(perm, BF16)


def _rope_att(x, col0, heads, cos, sin, perm, name):
    n = x.shape[0]
    tr = _row_tile(n, 1056)

    def body(x_ref, c_ref, s_ref, p_ref, o_ref):
        xb = x_ref[...].astype(BF16)
        sw = jnp.dot(xb, p_ref[...], preferred_element_type=F32)
        o_ref[...] = (xb.astype(F32) * c_ref[...] + sw * s_ref[...]).astype(o_ref.dtype)

    hd = ATT_HEAD_DIM
    tab = pl.BlockSpec((tr, hd), lambda i, h: (i, 0))
    return pl.pallas_call(body, grid=(n // tr, heads),
                          in_specs=[pl.BlockSpec((tr, hd), lambda i, h: (i, col0 + h)), tab, tab, pl.BlockSpec((hd, hd), lambda i, h: (0, 0))],
                          out_specs=pl.BlockSpec((tr, hd), lambda i, h: (i, h)), out_shape=_sds((n, heads * hd), BF16), name=name)(x, cos, sin, perm)


def _rope_ret(x, col0, cos, sin, name):
    p, n, _ = x.shape
    tr = _row_tile(n, 1056)
    hd = RET_HEAD_DIM

    def body(*refs):
        x_ref, o_ref = refs[0], refs[-1]
        xv = x_ref[0].astype(F32)
        for q in range(1, p):
            xv = xv + x_ref[q].astype(F32)
        if cos is not None:
            sw = jnp.concatenate([xv[:, hd // 2:], xv[:, :hd // 2]], axis=1)
            xv = xv * refs[1][...] + sw * refs[2][...]
        o_ref[...] = xv.astype(o_ref.dtype)

    tab = pl.BlockSpec((tr, hd), lambda i, h: (i, 0))
    ins = [pl.BlockSpec((p, tr, hd), lambda i, h: (0, i, col0 + h))] + ([tab, tab] if cos is not None else [])
    args = (x,) + ((cos, sin) if cos is not None else ())
    return pl.pallas_call(body, grid=(n // tr, RET_HEADS), in_specs=ins, out_specs=pl.BlockSpec((tr, hd), lambda i, h: (i, h)),
                          out_shape=_sds((n, RET_WIDTH), BF16), name=name)(*args)


def _att_mask(nblk, n_tot):
    row = lax.broadcasted_iota(jnp.int32, (BLOCK, 4 * BLOCK), 0)
    col = lax.broadcasted_iota(jnp.int32, (BLOCK, 4 * BLOCK), 1)
    qi = nblk * BLOCK + row
    seg = col // BLOCK
    cj = col % BLOCK
    kj = (nblk - 1 + seg) * BLOCK + cj
    band = (jnp.abs(qi - kj) <= WINDOW) & (kj >= PAD_FRONT) & (kj < n_tot) & (seg < 3)
    meta = (seg == 3) & (cj >= PAD_FRONT) & (jnp.abs(qi - cj) > WINDOW)
    return band | meta


def _att_specs(nb, v_col):
    kv = lambda f, cb: pl.BlockSpec((BLOCK, KV_WIDTH), lambda n: (f(n), cb))
    prev, own, nxt, first = (lambda n: jnp.maximum(n - 1, 0)), (lambda n: n), (lambda n: jnp.minimum(n + 1, nb - 1)), (lambda n: 0)
    return [kv(f, 0) for f in (prev, own, nxt, first)] + [kv(f, v_col) for f in (prev, own, nxt, first)]


def _att_probs(s, ok, snk):
    s = jnp.where(ok, s, NEG)
    m = jnp.maximum(jnp.max(s, axis=-1, keepdims=True), snk)
    p = jnp.exp(s - m)
    ps = jnp.exp(snk - m)
    inv = 1.0 / (jnp.sum(p, axis=-1, keepdims=True) + ps)
    return p * inv, ps * inv


def _att_fwd(q, k, proj, sink_b, name):
    n = q.shape[0]
    nb = n // BLOCK
    hd = ATT_HEAD_DIM

    def body(q_ref, kp, ko, kn, km, vp, vo, vn, vm, sink_ref, o_ref):
        nblk = pl.program_id(0)
        ok = _att_mask(nblk, n)
        keep = (nblk * BLOCK + lax.broadcasted_iota(jnp.int32, (BLOCK, 1), 0)) >= PAD_FRONT
        for kh in range(ATT_KV_HEADS):
            cs = slice(kh * hd, (kh + 1) * hd)
            kk = jnp.concatenate([r[:, cs] for r in (kp, ko, kn, km)], axis=0)
            vv = jnp.concatenate([r[:, cs] for r in (vp, vo, vn, vm)], axis=0)
            heads = [kh * ATT_GROUP + g for g in range(ATT_GROUP)]
            q4 = jnp.concatenate([q_ref[:, h * hd:(h + 1) * hd] for h in heads], axis=0)
            s = lax.dot_general(q4, kk, (((1,), (1,)), ((), ())), preferred_element_type=F32) * ATT_SCALE
            ps = []
            for g, h in enumerate(heads):
                p, _ = _att_probs(s[g * BLOCK:(g + 1) * BLOCK], ok, sink_ref[h:h + 1, 0:1])
                ps.append(p)
            o = jnp.dot(jnp.concatenate(ps, axis=0).astype(BF16), vv, preferred_element_type=F32)
            for g, h in enumerate(heads):
                o_ref[:, h * hd:(h + 1) * hd] = jnp.where(keep, o[g * BLOCK:(g + 1) * BLOCK], 0.0).astype(o_ref.dtype)

    qspec = pl.BlockSpec((BLOCK, ATT_WIDTH), lambda i: (i, 0))
    return pl.pallas_call(body, grid=(nb,), in_specs=[qspec] + _att_specs(nb, COL_AV256) + [pl.BlockSpec((ATT_HEADS, 128), lambda i: (0, 0))],
                          out_specs=qspec, out_shape=_sds((n, ATT_WIDTH), BF16), name=name)(q, k, k, k, k, proj, proj, proj, proj, sink_b)


def _att_bwd(q, k, proj, sink_b, dmixed, name):
    n = q.shape[0]
    nb = n // BLOCK
    hd = ATT_HEAD_DIM

    def body(q_ref, kp, ko, kn, km, vp, vo, vn, vm, sink_ref, do_ref, dq_ref, dk_ref, dv_ref, dsink_ref):
        nblk = pl.program_id(0)

        @pl.when(nblk == 0)
        def _():
            dk_ref[...] = jnp.zeros_like(dk_ref)
            dv_ref[...] = jnp.zeros_like(dv_ref)
            dsink_ref[...] = jnp.zeros_like(dsink_ref)

        ok = _att_mask(nblk, n)
        rows = [jnp.maximum(nblk - 1, 0), nblk, jnp.minimum(nblk + 1, nb - 1), 0]
        for kh in range(ATT_KV_HEADS):
            cs = slice(kh * hd, (kh + 1) * hd)
            kk = jnp.concatenate([r[:, cs] for r in (kp, ko, kn, km)], axis=0)
            vv = jnp.concatenate([r[:, cs] for r in (vp, vo, vn, vm)], axis=0)
            heads = [kh * ATT_GROUP + g for g in range(ATT_GROUP)]
            q4 = jnp.concatenate([q_ref[:, h * hd:(h + 1) * hd] for h in heads], axis=0)
            do4 = jnp.concatenate([do_ref[:, h * hd:(h + 1) * hd] for h in heads], axis=0)
            s = lax.dot_general(q4, kk, (((1,), (1,)), ((), ())), preferred_element_type=F32) * ATT_SCALE
            dp = lax.dot_general(do4, vv, (((1,), (1,)), ((), ())), preferred_element_type=F32)
            ps, dss = [], []
            for g, h in enumerate(heads):
                p, psink = _att_probs(s[g * BLOCK:(g + 1) * BLOCK], ok, sink_ref[h:h + 1, 0:1])
                dpg = dp[g * BLOCK:(g + 1) * BLOCK]
                delta = jnp.sum(p * dpg, axis=-1, keepdims=True)
                ps.append(p)
                dss.append(p * (dpg - delta) * ATT_SCALE)
                dsink_ref[h:h + 1, :] = dsink_ref[h:h + 1, :] - jnp.sum(psink * delta, axis=0, keepdims=True)
            ds = jnp.concatenate(dss, axis=0).astype(BF16)
            pb = jnp.concatenate(ps, axis=0).astype(BF16)
            dq = jnp.dot(ds, kk, preferred_element_type=F32)
            for g, h in enumerate(heads):
                dq_ref[:, h * hd:(h + 1) * hd] = dq[g * BLOCK:(g + 1) * BLOCK].astype(dq_ref.dtype)
            dk = lax.dot_general(ds, q4, (((0,), (0,)), ((), ())), preferred_element_type=F32)
            dv = lax.dot_general(pb, do4, (((0,), (0,)), ((), ())), preferred_element_type=F32)
            for seg, r in enumerate(rows):
                at = (pl.ds(pl.multiple_of(r * BLOCK, BLOCK), BLOCK), cs)
                dk_ref[at] += dk[seg * BLOCK:(seg + 1) * BLOCK]
                dv_ref[at] += dv[seg * BLOCK:(seg + 1) * BLOCK]

    qspec = pl.BlockSpec((BLOCK, ATT_WIDTH), lambda i: (i, 0))
    whole = pl.BlockSpec((n, KV_WIDTH), lambda i: (0, 0))
    sinks = pl.BlockSpec((ATT_HEADS, 128), lambda i: (0, 0))
    return pl.pallas_call(body, grid=(nb,), in_specs=[qspec] + _att_specs(nb, COL_AV256) + [sinks, qspec], out_specs=(qspec, whole, whole, sinks),
                          out_shape=(_sds((n, ATT_WIDTH), BF16), _sds((n, KV_WIDTH), F32), _sds((n, KV_WIDTH), F32), _sds((ATT_HEADS, 128), F32)),
                          name=name, compiler_params=pltpu.CompilerParams(dimension_semantics=("arbitrary",)))(
                              q, k, k, k, k, proj, proj, proj, proj, sink_b, dmixed)


def _ret_decay(lg, d):
    a = lax.broadcasted_iota(jnp.int32, (BLOCK, 1), 0)
    b = lax.broadcasted_iota(jnp.int32, (1, BLOCK), 1)
    t_col = a + d * (BLOCK - 1 - 2 * a)
    t_row = b + d * (BLOCK - 1 - 2 * b)
    diff = t_col - t_row
    dist = jnp.maximum(diff, 0).astype(F32)
    dmask = jnp.where(diff >= d, jnp.exp(lg * dist), 0.0)
    tf = t_col.astype(F32)
    xi = jnp.exp(lg * (tf + 1.0))
    zeta = jnp.exp(lg * (BLOCK - 1.0 - tf))
    gam = jnp.exp(jnp.full((1, 1), BLOCK, F32) * lg)
    return dmask, dist, xi, zeta, gam, tf


def _ret_fwd(q, k, proj, lg, name):
    n = q.shape[0]
    nc = n // BLOCK
    hd = RET_HEAD_DIM
    chunk = lambda d, c: c + d * (nc - 1 - 2 * c)

    def body(lg_ref, q_ref, k_ref, v0, v1, v2, v3, o_ref, st_ref, s_ref):
        d, c = pl.program_id(0), pl.program_id(1)

        @pl.when(c == 0)
        def _():
            s_ref[...] = jnp.zeros_like(s_ref)

        for h, v_ref in enumerate((v0, v1, v2, v3)):
            cs = slice(h * hd, (h + 1) * hd)
            dmask, _, xi, zeta, gam, _ = _ret_decay(lg_ref[h, d], d)
            qv = q_ref[:, cs]
            kf = k_ref[:, cs].astype(F32) * RET_K_SCALE
            vv = v_ref[...]
            s = lax.dot_general(qv, kf.astype(BF16), (((1,), (1,)), ((), ())), preferred_element_type=F32)
            sb = s_ref[h]
            o_ref[:, cs] = (jnp.dot((s * dmask).astype(BF16), vv, preferred_element_type=F32)
                            + jnp.dot((qv.astype(F32) * xi).astype(BF16), sb.astype(BF16), preferred_element_type=F32))
            st_ref[h] = sb
            s_ref[h] = gam * sb + lax.dot_general((kf * zeta).astype(BF16), vv, (((0,), (0,)), ((), ())), preferred_element_type=F32)

    wide = pl.BlockSpec((BLOCK, RET_WIDTH), lambda d, c: (chunk(d, c), 0))
    vblk = lambda h: pl.BlockSpec((BLOCK, hd), lambda d, c: (chunk(d, c), COL_RV + h))
    return pl.pallas_call(
        body, grid=(2, nc), in_specs=[pl.BlockSpec(memory_space=pltpu.SMEM), wide, wide] + [vblk(h) for h in range(RET_HEADS)],
        out_specs=(pl.BlockSpec((None, BLOCK, RET_WIDTH), lambda d, c: (d, chunk(d, c), 0)),
                   pl.BlockSpec((RET_HEADS, None, None, hd, hd), lambda d, c: (0, d, c, 0, 0))),
        out_shape=(_sds((2, n, RET_WIDTH), F32), _sds((RET_HEADS, 2, nc, hd, hd), F32)), scratch_shapes=[pltpu.VMEM((RET_HEADS, hd, hd), F32)],
        name=name, compiler_params=pltpu.CompilerParams(dimension_semantics=("parallel", "arbitrary")))(lg, q, k, proj, proj, proj, proj)


def _ret_bwd(q, k, proj, lg, do, states, name):
    n = q.shape[0]
    nc = n // BLOCK
    hd = RET_HEAD_DIM
    chunk = lambda d, r: (nc - 1 - r) + d * (2 * r - (nc - 1))

    def body(lg_ref, q_ref, k_ref, v0, v1, v2, v3, do_ref, st_ref, dq_ref, dk_ref, dv_ref, dlg_ref, ds_ref):
        d, r = pl.program_id(0), pl.program_id(1)

        @pl.when(r == 0)
        def _():
            ds_ref[...] = jnp.zeros_like(ds_ref)
            dlg_ref[...] = jnp.zeros_like(dlg_ref)

        row = lax.broadcasted_iota(jnp.int32, (BLOCK, 1), 0) + chunk(d, r) * BLOCK
        keep = row >= PAD_FRONT
        nt = (((1,), (1,)), ((), ()))
        tn = (((0,), (0,)), ((), ()))
        for h, v_ref in enumerate((v0, v1, v2, v3)):
            cs = slice(h * hd, (h + 1) * hd)
            dmask, dist, xi, zeta, gam, tf = _ret_decay(lg_ref[h, d], d)
            qv, vv, dov = q_ref[:, cs], v_ref[...], do_ref[:, cs]
            qf = qv.astype(F32)
            kf = k_ref[:, cs].astype(F32) * RET_K_SCALE
            kb = kf.astype(BF16)
            sc = st_ref[h]
            dsn = ds_ref[h]
            s = lax.dot_general(qv, kb, nt, preferred_element_type=F32)
            dsc = lax.dot_general(dov, vv, nt, preferred_element_type=F32) * dmask
            dsb = dsc.astype(BF16)
            dq_c = xi * lax.dot_general(dov, sc.astype(BF16), nt, preferred_element_type=F32)
            dk_c = zeta * lax.dot_general(vv, dsn.astype(BF16), nt, preferred_element_type=F32)
            dq = jnp.dot(dsb, kb, preferred_element_type=F32) + dq_c
            dk = lax.dot_general(dsb, qv, tn, preferred_element_type=F32) + dk_c
            dv = (lax.dot_general((s * dmask).astype(BF16), dov, tn, preferred_element_type=F32)
                  + jnp.dot((kf * zeta).astype(BF16), dsn.astype(BF16), preferred_element_type=F32))
            ds_ref[h] = gam * dsn + lax.dot_general((qf * xi).astype(BF16), dov, tn, preferred_element_type=F32)
            dlg = (jnp.sum(dsc * s * dist, keepdims=True)
                   + jnp.sum((tf + 1.0) * jnp.sum(qf * dq_c, axis=-1, keepdims=True), keepdims=True)
                   + jnp.sum((BLOCK - 1.0 - tf) * jnp.sum(kf * dk_c, axis=-1, keepdims=True), keepdims=True)
                   + BLOCK * gam * jnp.sum(dsn * sc, keepdims=True))
            dlg_ref[h] += dlg
            dq_ref[:, cs] = dq
            dk_ref[:, cs] = jnp.where(keep, dk * RET_K_SCALE, 0.0)
            dv_ref[:, cs] = jnp.where(keep, dv, 0.0)

    wide = pl.BlockSpec((BLOCK, RET_WIDTH), lambda d, r: (chunk(d, r), 0))
    vblk = lambda h: pl.BlockSpec((BLOCK, hd), lambda d, r: (chunk(d, r), COL_RV + h))
    plane = pl.BlockSpec((None, BLOCK, RET_WIDTH), lambda d, r: (d, chunk(d, r), 0))
    return pl.pallas_call(
        body, grid=(2, nc),
        in_specs=[pl.BlockSpec(memory_space=pltpu.SMEM), wide, wide] + [vblk(h) for h in range(RET_HEADS)]
        + [wide, pl.BlockSpec((RET_HEADS, None, None, hd, hd), lambda d, r: (0, d, nc - 1 - r, 0, 0))],
        out_specs=(plane, plane, plane, pl.BlockSpec((RET_HEADS, None, 8, 128), lambda d, r: (0, d, 0, 0))),
        out_shape=(_sds((2, n, RET_WIDTH), F32),) * 3 + (_sds((RET_HEADS, 2, 8, 128), F32),),
        scratch_shapes=[pltpu.VMEM((RET_HEADS, hd, hd), F32)], name=name,
        compiler_params=pltpu.CompilerParams(dimension_semantics=("parallel", "arbitrary")))(lg, q, k, proj, proj, proj, proj, do, states)


def _retgate_fwd(o, proj, gain, name):
    _, n, _ = o.shape
    tr = _row_tile(n, 1056)
    hd = RET_HEAD_DIM

    def body(o_ref, rg_ref, g_ref, y_ref):
        ov = o_ref[0] + o_ref[1]
        r = lax.rsqrt(jnp.mean(ov * ov, axis=-1, keepdims=True) + EPS)
        y_ref[...] = (_silu(rg_ref[...].astype(F32)) * (ov * r * g_ref[...])).astype(y_ref.dtype)

    return pl.pallas_call(body, grid=(n // tr, RET_HEADS),
                          in_specs=[pl.BlockSpec((2, tr, hd), lambda i, h: (0, i, h)), pl.BlockSpec((tr, hd), lambda i, h: (i, COL_RG + h)),
                                    pl.BlockSpec((1, hd), lambda i, h: (0, h))],
                          out_specs=pl.BlockSpec((tr, hd), lambda i, h: (i, h)), out_shape=_sds((n, RET_WIDTH), BF16), name=name)(o, proj, gain)


def _retgate_bwd(dmixed, o, proj, gain, name):
    _, n, _ = o.shape
    tr = _row_tile(n, 1056)
    hd = RET_HEAD_DIM

    def body(dy_ref, o_ref, rg_ref, g_ref, do_ref, drg_ref, dg_ref):
        i = pl.program_id(1)
        ov = o_ref[0] + o_ref[1]
        r = lax.rsqrt(jnp.mean(ov * ov, axis=-1, keepdims=True) + EPS)
        xhat = ov * r
        rg = rg_ref[...].astype(F32)
        dy = dy_ref[...].astype(F32)
        drg_ref[...] = (dy * (xhat * g_ref[...]) * _dsilu(rg)).astype(drg_ref.dtype)
        dn = dy * _silu(rg)
        dxh = dn * g_ref[...]
        do_ref[...] = (r * (dxh - xhat * jnp.mean(dxh * xhat, axis=-1, keepdims=True))).astype(do_ref.dtype)

        @pl.when(i == 0)
        def _():
            dg_ref[...] = jnp.zeros_like(dg_ref)

        dg_ref[...] += jnp.sum(dn * xhat, axis=0, keepdims=True)

    tile = pl.BlockSpec((tr, hd), lambda h, i: (i, h))
    vec = pl.BlockSpec((1, hd), lambda h, i: (0, h))
    return pl.pallas_call(body, grid=(RET_HEADS, n // tr),
                          in_specs=[pl.BlockSpec((tr, hd), lambda h, i: (i, RET_HEADS + h)), pl.BlockSpec((2, tr, hd), lambda h, i: (0, i, h)),
                                    pl.BlockSpec((tr, hd), lambda h, i: (i, COL_RG + h)), vec],
                          out_specs=(tile, tile, vec), out_shape=(_sds((n, RET_WIDTH), BF16), _sds((n, RET_WIDTH), BF16), _sds((1, RET_WIDTH), F32)),
                          name=name, compiler_params=pltpu.CompilerParams(dimension_semantics=("parallel", "arbitrary")))(dmixed, o, proj, gain)


def _swiglu_fwd(gate, up, name):
    n, f = gate.shape
    tr, tc = _row_tile(n, 384), f // 2

    def body(g_ref, u_ref, o_ref):
        o_ref[...] = (_silu(g_ref[...].astype(F32)) * u_ref[...].astype(F32)).astype(o_ref.dtype)

    t = pl.BlockSpec((tr, tc), lambda i, j: (i, j))
    return pl.pallas_call(body, grid=(n // tr, 2), in_specs=[t, t], out_specs=t, out_shape=_sds((n, f), BF16), name=name)(gate, up)


def _swiglu_bwd(df, gate, up, name):
    n, f = gate.shape
    tr, tc = _row_tile(n, 384), f // 2

    def body(d_ref, g_ref, u_ref, dg_ref, du_ref):
        d, g = d_ref[...].astype(F32), g_ref[...].astype(F32)
        dg_ref[...] = (d * u_ref[...].astype(F32) * _dsilu(g)).astype(dg_ref.dtype)
        du_ref[...] = (d * _silu(g)).astype(du_ref.dtype)

    t = pl.BlockSpec((tr, tc), lambda i, j: (i, j))
    return pl.pallas_call(body, grid=(n // tr, 2), in_specs=[t, t, t], out_specs=(t, t), out_shape=(_sds((n, f), BF16),) * 2, name=name)(df, gate, up)


def _loss_head(h, target, name):
    n, d = h.shape
    nb = n // BLOCK

    def body(h_ref, t_ref, dh_ref, l_ref):
        i = pl.program_id(0)

        @pl.when(i == 0)
        def _():
            l_ref[...] = jnp.zeros_like(l_ref)
            dh_ref[...] = jnp.zeros_like(dh_ref)

        @pl.when(i > 0)
        def _():
            e = h_ref[...] - t_ref[...]
            dh_ref[...] = e * (1.0 / d)
            l_ref[...] += 0.5 * jnp.sum(jnp.mean(e * e, axis=-1, keepdims=True), keepdims=True)

    blk = pl.BlockSpec((BLOCK, d), lambda i: (i, 0))
    return pl.pallas_call(body, grid=(nb,), in_specs=[blk, pl.BlockSpec((BLOCK, d), lambda i: (jnp.maximum(i - 1, 0), 0))],
                          out_specs=(blk, pl.BlockSpec((8, 128), lambda i: (0, 0))), out_shape=(_sds((n, d), F32), _sds((8, 128), F32)), name=name,
                          compiler_params=pltpu.CompilerParams(dimension_semantics=("arbitrary",)))(h, target)


def _adamw(parts, w, m, v, name, sel=None, layer=None, prev=None):
    s, (r, c) = parts.shape[0], parts.shape[-2:]
    tr = _row_tile(r, max(16, (ADAMW_TILE_ELEMS // c) // 16 * 16))
    b1c, b2c = 1.0 - ADAM_B1 ** ADAM_STEP, 1.0 - ADAM_B2 ** ADAM_STEP

    def body(p_ref, w_ref, m_ref, v_ref, *rest):
        g_ref, d_ref, mo_ref, vo_ref = rest[-4:]
        g = p_ref[0].astype(F32)
        for q in range(1, s):
            g = g + p_ref[q].astype(F32)
        mn = ADAM_B1 * m_ref[...] + (1.0 - ADAM_B1) * g
        vn = ADAM_B2 * v_ref[...] + (1.0 - ADAM_B2) * jnp.square(g)
        g_ref[...] = g
        mo_ref[...] = mn
        vo_ref[...] = vn
        d_ref[...] = -ADAM_LR * ((mn / b1c) / (jnp.sqrt(vn / b2c) + ADAM_EPS) + ADAM_WD * w_ref[...])

    pspec = (pl.BlockSpec((s, tr, c), lambda i: (0, i, 0)) if sel is None else pl.BlockSpec((s, None, tr, c), lambda i: (0, sel, i, 0)))
    if layer is None:
        t = pl.BlockSpec((tr, c), lambda i: (i, 0))
        return pl.pallas_call(body, grid=(r // tr,), in_specs=[pspec, t, t, t], out_specs=(t, t, t, t), out_shape=(_sds((r, c), F32),) * 4,
                              name=name)(parts, w, m, v)
    t = pl.BlockSpec((None, tr, c), lambda i: (layer, i, 0))
    prev = prev if prev is not None else tuple(lax.empty(w.shape, F32) for _ in range(4))
    return pl.pallas_call(body, grid=(r // tr,), in_specs=[pspec, t, t, t] + [ANY] * 4, out_specs=(t, t, t, t), out_shape=(_sds(w.shape, F32),) * 4,
                          input_output_aliases={4 + i: i for i in range(4)}, name=name)(parts, w, m, v, *prev)


def _allgather(xs, name):
    na = len(xs)

    def body(*refs):
        x_refs, o_refs = refs[:na], refs[na:2 * na]
        send, recv, lsem = refs[2 * na:]
        x, y, c = lax.axis_index("x"), lax.axis_index("y"), lax.axis_index("c")
        me, sib = (x, y, c), (x, y, 1 - c)
        chips = [(1 - x, y), (x, 1 - y), (1 - x, 1 - y)]
        slot = lambda p: 4 * p[0] + 2 * p[1] + p[2]

        def copy(a, k, block, to, src=None):
            dst = o_refs[a].at[slot(block)]
            return pltpu.make_async_remote_copy(src_ref=dst if src is None else src, dst_ref=dst, send_sem=send.at[a, k], recv_sem=recv.at[a, k],
                                                device_id=to, device_id_type=MESH)

        mine = [pltpu.make_async_copy(x_refs[a], o_refs[a].at[slot(me)], lsem.at[a]) for a in range(na)]
        for cp in mine:
            cp.start()
        first = []
        for a in range(na):
            first.append(copy(a, 0, me, sib, src=x_refs[a]))
            first += [copy(a, 1 + j, me, (*chip, c), src=x_refs[a]) for j, chip in enumerate(chips)]
        for cp in first:
            cp.start()
        passed = []
        for j, chip in enumerate(chips):
            for a in range(na):
                copy(a, 1 + j, (*chip, c), me).wait_recv()
                passed.append(copy(a, 4 + j, (*chip, c), sib))
                passed[-1].start()
        for a in range(na):
            copy(a, 0, sib, me).wait_recv()
            for j, chip in enumerate(chips):
                copy(a, 4 + j, (*chip, 1 - c), me).wait_recv()
        for cp in first + passed:
            cp.wait_send()
        for cp in mine:
            cp.wait()

    return pl.pallas_call(body, in_specs=[ANY] * na, out_specs=[ANY] * na, out_shape=[_sds((N_DEV,) + t.shape, t.dtype) for t in xs],
                          scratch_shapes=[pltpu.SemaphoreType.DMA((na, 7)), pltpu.SemaphoreType.DMA((na, 7)), pltpu.SemaphoreType.DMA((na,))],
                          name=name)(*xs)


HBM = pl.BlockSpec(memory_space=pltpu.HBM)
SEM = pl.BlockSpec(memory_space=pltpu.SEMAPHORE)
EFFECT = pltpu.SideEffectType.DATAFLOW_SIDE_EFFECTING


def _split_copies(x_refs, land_refs, send, recv, own, scatter, landing):
    x, y, c = lax.axis_index("x"), lax.axis_index("y"), lax.axis_index("c")
    me = 4 * x + 2 * y + c
    local, remote = [], []
    for a in range(len(x_refs)):
        local.append(pltpu.make_async_copy(x_refs[a].at[me] if scatter else x_refs[a], land_refs[a].at[me], own.at[a]))
        for r in range(1, N_DEV):
            peer = ((1 - x if r & 4 else x), (1 - y if r & 2 else y), (1 - c if r & 1 else c))
            slot = 4 * peer[0] + 2 * peer[1] + peer[2]
            remote.append(pltpu.make_async_remote_copy(src_ref=x_refs[a].at[slot] if scatter else x_refs[a],
                                                       dst_ref=land_refs[a].at[slot if landing else me], send_sem=send.at[7 * a + r - 1],
                                                       recv_sem=recv.at[7 * a + r - 1], device_id=peer, device_id_type=MESH))
    return local, remote


def _send_start(xs, scatter, after, name):
    na = len(xs)
    lands = [lax.empty(t.shape if scatter else (N_DEV,) + t.shape, t.dtype) for t in xs]

    def body(*refs):
        x_refs, land_refs = refs[:na], refs[na:2 * na]
        send, recv, own, token = refs[2 * na + 1], refs[2 * na + 2], refs[2 * na + 3], refs[-1]
        local, remote = _split_copies(x_refs, land_refs, send, recv, own, scatter, False)
        for cp in remote + local:
            cp.start()
        token[...] = jnp.zeros_like(token)

    hbm = lambda t: pltpu.with_memory_space_constraint(t, pltpu.HBM)
    outs = pl.pallas_call(
        body, name=name,
        out_shape=(pltpu.SemaphoreType.DMA((7 * na,)), pltpu.SemaphoreType.DMA((7 * na,)), pltpu.SemaphoreType.DMA((na,)),
                   *[pltpu.HBM(t.shape, t.dtype) for t in xs], *[pltpu.HBM(t.shape, t.dtype) for t in lands], _sds((8, 128), F32)),
        in_specs=[HBM] * (2 * na) + [ANY], out_specs=(SEM, SEM, SEM, *[HBM] * (2 * na), pl.BlockSpec(memory_space=pltpu.VMEM)),
        input_output_aliases={i: 3 + i for i in range(2 * na)},
        compiler_params=pltpu.CompilerParams(has_side_effects=EFFECT))(*[hbm(t) for t in xs], *[hbm(t) for t in lands], after)
    return outs[:3], list(outs[3:3 + na]), list(outs[3 + na:3 + 2 * na]), outs[-1]


def _send_wait(started, scatter, after, name):
    sems, xs, lands, _ = started
    na = len(xs)

    def body(*refs):
        local, remote = _split_copies(refs[:na], refs[na:2 * na], refs[2 * na], refs[2 * na + 1], refs[2 * na + 2], scatter, True)
        for cp in remote:
            cp.wait_send()
            cp.wait_recv()
        for cp in local:
            cp.wait()

    outs = pl.pallas_call(body, name=name, out_shape=tuple(pltpu.HBM(t.shape, t.dtype) for t in xs + lands),
                          in_specs=[HBM] * (2 * na) + [SEM, SEM, SEM, ANY], out_specs=[HBM] * (2 * na),
                          input_output_aliases={i: i for i in range(2 * na)},
                          compiler_params=pltpu.CompilerParams(has_side_effects=EFFECT))(*xs, *lands, *sems, after)
    return list(outs[na:])


def _local_step(x, meta, target, mix_weights_fn, ffn_weights_fn, grads_fn, sink, dec_f, dec_b, ret_norm, n_mix_pre, n_mix_post, n_ffn_pre,
                n_ffn_post):
    depth = n_mix_pre.shape[0]
    d = D_MODEL
    h = jnp.concatenate([jnp.zeros((PAD_FRONT, d), F32), meta, x], axis=0)
    n = h.shape[0]
    cos_a, sin_a, cos_r, sin_r, perm = _rope_tables(n)
    lg_all = jnp.stack([-jnp.exp(dec_f), -jnp.exp(dec_b)], axis=-1)
    saved = []
    for l in range(depth):
        t = f"l{l}_"
        sink_b = jnp.broadcast_to(sink[l][:, None], (ATT_HEADS, 128))
        wi, wo, tok = mix_weights_fn(l, h)
        u = _norm_fwd(h, (n_mix_pre[l] + tok)[None], None, BF16, t + "norm_mix_pre")
        proj = _mm_nt(u, wi, BF16, WIDE_TILE, d, t + "proj")
        aq = _rope_att(proj, 0, ATT_HEADS, cos_a, sin_a, perm, t + "rope_aq")
        ak = _rope_att(proj, COL_AK, ATT_KV_HEADS, cos_a, sin_a, perm, t + "rope_ak")
        att = _att_fwd(aq, ak, proj, sink_b, t + "att")
        proj3 = proj[None]
        rq = _rope_ret(proj3, COL_RQ, cos_r, sin_r, t + "rope_rq")
        rk = _rope_ret(proj3, COL_RK, cos_r, sin_r, t + "rope_rk")
        o_ret, states = _ret_fwd(rq, rk, proj, lg_all[l], t + "ret")
        retg = _retgate_fwd(o_ret, proj, ret_norm[l][None], t + "retgate")
        mixed = jnp.concatenate([att, retg], axis=1)
        mo = _mm_nn(mixed, wo, F32, 1024, d, t + "out_proj")
        h_mid = _norm_fwd(mo, n_mix_post[l][None], h, F32, t + "norm_mix_post")
        wg, wu, wd, tok = ffn_weights_fn(l, h_mid)
        u2 = _norm_fwd(h_mid, (n_ffn_pre[l] + tok)[None], None, BF16, t + "norm_ffn_pre")
        gate = _mm_nt(u2, wg, BF16, WIDE_TILE, d, t + "gate")
        up = _mm_nt(u2, wu, BF16, WIDE_TILE, d, t + "up")
        f = _swiglu_fwd(gate, up, t + "swiglu")
        dn = _mm_nn(f, wd, F32, 1024, WIDE_TILE, t + "down")
        h_out = _norm_fwd(dn, n_ffn_post[l][None], h_mid, F32, t + "norm_ffn_post")
        saved.append(dict(h=h, u=u, proj=proj, aq=aq, ak=ak, rq=rq, rk=rk, o_ret=o_ret, states=states, mixed=mixed, mo=mo, h_mid=h_mid, u2=u2,
                          gate=gate, up=up, f=f, dn=dn, sink_b=sink_b, wi=wi, wo=wo, wg=wg, wu=wu, wd=wd))
        h = h_out

    dh, loss_part = _loss_head(h, target, "loss_head")
    gs = dict(sink=[None] * depth, dec_f=[None] * depth, dec_b=[None] * depth, ret_norm=[None] * depth, mix_pre=[None] * depth,
              mix_post=[None] * depth, ffn_pre=[None] * depth, ffn_post=[None] * depth)
    tok_b = jnp.zeros((), F32)
    for l in reversed(range(depth)):
        t = f"l{l}_b_"
        sv = saved[l]
        proj = sv["proj"]
        d_dn, gs["ffn_post"][l] = _norm_bwd(dh, sv["dn"], (n_ffn_post[l] + tok_b)[None], None, BF16, t + "norm_ffn_post")
        gw = {}
        d_f = _mm_nt(d_dn, sv["wd"], BF16, WIDE_TILE, d, t + "d_f")
        gw["wd"] = _mm_tn(sv["f"], d_dn, WIDE_TILE, 1024, t + "dw_down")
        d_gate, d_up = _swiglu_bwd(d_f, sv["gate"], sv["up"], t + "swiglu")
        du2 = _mm_nn(d_gate, sv["wg"], F32, 1024, WIDE_TILE, t + "du2_gate")
        du2 = _mm_nn(d_up, sv["wu"], F32, 1024, WIDE_TILE, t + "du2_up", acc=du2)
        gw["wg"] = _mm_tn(d_gate, sv["u2"], WIDE_TILE, 1024, t + "dw_gate")
        gw["wu"] = _mm_tn(d_up, sv["u2"], WIDE_TILE, 1024, t + "dw_up")
        tok_b = grads_fn(l, "ffn", gw, du2)
        dh, gs["ffn_pre"][l] = _norm_bwd(du2, sv["h_mid"], (n_ffn_pre[l] + tok_b)[None], dh, F32, t + "norm_ffn_pre")
        d_mo, gs["mix_post"][l] = _norm_bwd(dh, sv["mo"], n_mix_post[l][None], None, BF16, t + "norm_mix_post")
        d_mixed = _mm_nt(d_mo, sv["wo"], BF16, 1024, d, t + "d_mixed")
        gw["wo"] = _mm_tn(sv["mixed"], d_mo, 1024, 1024, t + "dw_out")
        d_o, d_rg, gs["ret_norm"][l] = _retgate_bwd(d_mixed, sv["o_ret"], proj, ret_norm[l][None], t + "retgate")
        dq_r, dk_r, dv_r, dlg = _ret_bwd(sv["rq"], sv["rk"], proj, lg_all[l], d_o, sv["states"], t + "ret")
        draw = dlg[:, :, 0, 0] * lg_all[l]
        gs["dec_f"][l], gs["dec_b"][l] = draw[:, 0], draw[:, 1]
        dq_a, dk_a, dv_a, dsink = _att_bwd(sv["aq"], sv["ak"], proj, sv["sink_b"], d_mixed, t + "att")
        gs["sink"][l] = dsink[:, 0]
        dproj = jnp.concatenate([
            _rope_att(dq_a, 0, ATT_HEADS, cos_a, -sin_a, perm, t + "rope_aq"),
            _rope_att(dk_a, 0, ATT_KV_HEADS, cos_a, -sin_a, perm, t + "rope_ak"),
            dv_a.astype(BF16),
            _rope_ret(dq_r, 0, cos_r, -sin_r, t + "rope_rq"),
            _rope_ret(dk_r, 0, cos_r, -sin_r, t + "rope_rk"),
            _rope_ret(dv_r, 0, None, None, t + "sum_rv"),
            d_rg], axis=1)
        du = _mm_nn(dproj, sv["wi"], F32, 1024, WIDE_TILE, t + "du")
        gw["wi"] = _mm_tn(dproj, sv["u"], WIDE_TILE, 1024, t + "dw_in")
        dh, gs["mix_pre"][l] = _norm_bwd(du, sv["h"], n_mix_pre[l][None], dh, F32, t + "norm_mix_pre")
        tok_b = grads_fn(l, "mix", gw, dh)
    return loss_part[0, 0], dh, gs


def _pack_small(mix_pre, mix_post, ffn_pre, ffn_post, ret_norm, sink, dec_f, dec_b, loss, meta):
    d = D_MODEL

    def tile(a, rows=8):
        a = jnp.reshape(a, (-1, a.shape[-1])) if a.ndim else jnp.reshape(a, (1, 1))
        return jnp.pad(a, ((0, rows - a.shape[0]), (0, d - a.shape[1])))

    return jnp.concatenate([tile(mix_pre), tile(mix_post), tile(ffn_pre), tile(ffn_post), tile(ret_norm.reshape(-1, d)), tile(sink), tile(dec_f),
                            tile(dec_b), tile(loss), tile(meta, SMALL_ROWS - ROW_META)], axis=0)


def _unpack_small(p, depth):
    rows = lambda r0, cols: p[r0:r0 + depth, :cols]
    return dict(mix_pre=rows(ROW_MIX_PRE, D_MODEL), mix_post=rows(ROW_MIX_POST, D_MODEL), ffn_pre=rows(ROW_FFN_PRE, D_MODEL),
                ffn_post=rows(ROW_FFN_POST, D_MODEL), ret_norm=p[ROW_RET_NORM:ROW_RET_NORM + depth * RET_WIDTH // D_MODEL].reshape(depth, RET_WIDTH),
                sink=rows(ROW_SINK, ATT_HEADS), dec_f=rows(ROW_DEC_F, RET_HEADS), dec_b=rows(ROW_DEC_B, RET_HEADS), loss=p[ROW_LOSS, 0])


def kernel(x, meta_tokens, w_in, w_out, attn_sink, ret_decay_fwd, ret_decay_bwd, ret_norm, norm_mix_pre, norm_mix_post, w_gate, w_up, w_down, norm_ffn_pre, norm_ffn_post, loss_target, m_meta_tokens, m_w_in, m_w_out, m_attn_sink, m_ret_decay_fwd, m_ret_decay_bwd, m_ret_norm, m_norm_mix_pre, m_norm_mix_post, m_w_gate, m_w_up, m_w_down, m_norm_ffn_pre, m_norm_ffn_post, v_meta_tokens, v_w_in, v_w_out, v_attn_sink, v_ret_decay_fwd, v_ret_decay_bwd, v_ret_norm, v_norm_mix_pre, v_norm_mix_post, v_w_gate, v_w_up, v_w_down, v_norm_ffn_pre, v_norm_ffn_post):
    depth, d = w_in.shape[0], D_MODEL
    me = 4 * lax.axis_index("x") + 2 * lax.axis_index("y") + lax.axis_index("c")
    zero = jnp.zeros((), F32)

    meta_g, = _allgather([meta_tokens], "gather_meta")
    meta = meta_g.transpose(1, 0, 2).reshape(N_META, d)

    def shards(k):
        l = k // 2
        if k % 2 == 0:
            return [w_in[l].T.astype(BF16), w_out[l].astype(BF16)]
        return [w_gate[l].T.astype(BF16), w_up[l].T.astype(BF16), w_down[l].astype(BF16)]

    gathers, ahead = {}, 2
    for k in range(min(ahead + 1, 2 * depth)):
        gathers[k] = _send_start(shards(k), False, gathers[k - 1][3] if k else meta_g, f"gather_start_g{k}")

    def take(k, h):
        if k >= 1 and k + ahead < 2 * depth:
            gathers[k + ahead] = _send_start(shards(k + ahead), False, h, f"gather_start_g{k + ahead}")
        latest = max(gathers)
        return _send_wait(gathers.pop(k), False, gathers[latest][3] if latest > k else h, f"gather_wait_g{k}")

    def mix_weights_fn(l, h):
        wi_t, wo = take(2 * l, h)
        return wi_t.reshape(IN_COLS, d), wo.reshape(d, d), zero

    def ffn_weights_fn(l, h):
        wg_t, wu_t, wd = take(2 * l + 1, h)
        return wg_t.reshape(D_FF, d), wu_t.reshape(D_FF, d), wd.reshape(D_FF, d), zero

    exchanges, adam, order = {}, {}, []
    tr = lambda *ts: tuple(jnp.swapaxes(t, 1, 2) for t in ts)
    big = dict(wi=tr(w_in, m_w_in, v_w_in), wg=tr(w_gate, m_w_gate, v_w_gate), wu=tr(w_up, m_w_up, v_w_up), wd=(w_down, m_w_down, v_w_down),
               wo=(w_out, m_w_out, v_w_out))
    kinds = dict(ffn=("wg", "wu", "wd"), mix=("wi", "wo"))

    def finish(key, after):
        l, part = key
        arrived = _send_wait(exchanges.pop(key), True, after, f"exchange_wait_{part}_l{l}")
        for kind, parts in zip(kinds[part], arrived):
            adam[kind] = _adamw(parts, *big[kind], f"adamw_{kind}_l{l}", layer=l, prev=adam.get(kind))

    def grads_fn(l, part, gw, after):
        packed = [gw[kind].reshape(N_DEV, -1, d) for kind in kinds[part]]
        exchanges[(l, part)] = _send_start(packed, True, after, f"exchange_start_{part}_l{l}")
        order.append((l, part))
        token = exchanges[(l, part)][3]
        if len(order) > 2:
            finish(order[-3], token)
        return token[0, 0]

    loss_part, dh, gs = _local_step(x[0], meta, loss_target[0], mix_weights_fn, ffn_weights_fn, grads_fn, attn_sink, ret_decay_fwd, ret_decay_bwd, ret_norm,
                                    norm_mix_pre, norm_mix_post, norm_ffn_pre, norm_ffn_post)
    grad_x = dh[BLOCK:][None]

    st = lambda xs: jnp.stack([t.reshape(-1) if t.ndim == 1 else t[0] for t in xs])
    small = _pack_small(st(gs["mix_pre"]), st(gs["mix_post"]), st(gs["ffn_pre"]), st(gs["ffn_post"]), st(gs["ret_norm"]), st(gs["sink"]),
                        st(gs["dec_f"]), st(gs["dec_b"]), loss_part, dh[PAD_FRONT:BLOCK])
    small_g, = _allgather([small], "gather_small")
    for key in order[-2:]:
        finish(key, small_g)
    o_wi, o_wo, o_wg, o_wu, o_wd = tr(*adam["wi"]), adam["wo"], tr(*adam["wg"]), tr(*adam["wu"]), adam["wd"]
    zmeta = jnp.zeros((N_META, d), F32)
    packs = [_pack_small(a[0], a[1], a[2], a[3], a[4], a[5], a[6], a[7], zero, zmeta) for a in (
        (norm_mix_pre, norm_mix_post, norm_ffn_pre, norm_ffn_post, ret_norm, attn_sink, ret_decay_fwd, ret_decay_bwd),
        (m_norm_mix_pre, m_norm_mix_post, m_norm_ffn_pre, m_norm_ffn_post, m_ret_norm, m_attn_sink, m_ret_decay_fwd, m_ret_decay_bwd),
        (v_norm_mix_pre, v_norm_mix_post, v_norm_ffn_pre, v_norm_ffn_post, v_ret_norm, v_attn_sink, v_ret_decay_fwd, v_ret_decay_bwd))]
    o_small = [_unpack_small(o, depth) for o in _adamw(small_g, packs[0], packs[1], packs[2], "adamw_small")]
    meta_parts = lax.dynamic_slice(small_g, (0, ROW_META, me * (d // N_DEV)), (N_DEV, N_META, d // N_DEV))
    o_meta = _adamw(meta_parts, meta_tokens, m_meta_tokens, v_meta_tokens, "adamw_meta")

    outs = []
    for i in range(4):
        s = o_small[i]
        outs += [o_meta[i], o_wi[i], o_wo[i], s["sink"], s["dec_f"], s["dec_b"], s["ret_norm"], s["mix_pre"], s["mix_post"], o_wg[i], o_wu[i],
                 o_wd[i], s["ffn_pre"], s["ffn_post"]]
    return (o_small[0]["loss"], grad_x, *outs)
```

```python
import jax
import jax.numpy as jnp
import numpy as np
from jax import lax
from jax.experimental import pallas as pl
from jax.experimental.pallas import tpu as pltpu

F32, BF16 = jnp.float32, jnp.bfloat16

D_MODEL = 2048
N_META = 16
BLOCK = 128
WINDOW = 128
PAD_FRONT = BLOCK - N_META
ATT_HEAD_DIM = 128
ATT_WIDTH = D_MODEL // 2
ATT_HEADS = ATT_WIDTH // ATT_HEAD_DIM
ATT_KV_HEADS = 2
ATT_GROUP = ATT_HEADS // ATT_KV_HEADS
KV_WIDTH = ATT_KV_HEADS * ATT_HEAD_DIM
ROT_DIM = ATT_HEAD_DIM // 4
ROPE_THETA = 500000.0
RET_WIDTH = D_MODEL - ATT_WIDTH
RET_HEAD_DIM = 256
RET_HEADS = RET_WIDTH // RET_HEAD_DIM
RET_THETA = 10000.0
D_FF = 5632
IN_COLS = ATT_WIDTH + 2 * KV_WIDTH + 4 * RET_WIDTH
N_DEV = 8
SHARD_COLS = IN_COLS // N_DEV
EPS = 1e-6
NEG = -1e30
RET_K_SCALE = RET_HEAD_DIM ** -0.5
ATT_SCALE = ATT_HEAD_DIM ** -0.5

COL_AK = ATT_WIDTH // ATT_HEAD_DIM
COL_AV256 = (ATT_WIDTH + KV_WIDTH) // 256
COL_RQ = (ATT_WIDTH + 2 * KV_WIDTH) // RET_HEAD_DIM
COL_RK = COL_RQ + RET_HEADS
COL_RV = COL_RK + RET_HEADS
COL_RG = COL_RV + RET_HEADS

ADAM_LR, ADAM_B1, ADAM_B2, ADAM_EPS, ADAM_WD, ADAM_STEP = 0.001, 0.9, 0.999, 1e-08, 0.01, 10

ROW_MIX_PRE, ROW_MIX_POST, ROW_FFN_PRE, ROW_FFN_POST, ROW_RET_NORM, ROW_SINK, ROW_DEC_F, ROW_DEC_B, ROW_LOSS, ROW_META, SMALL_ROWS = (
    0, 8, 16, 24, 32, 40, 48, 56, 64, 72, 96)
ADAMW_TILE_ELEMS = 128 * 1024

MESH = pl.DeviceIdType.MESH
ANY = pl.BlockSpec(memory_space=pl.ANY)


def _row_tile(n, cap):
    for t in range(cap - cap % 16, 0, -16):
        if n % t == 0:
            return t
    raise ValueError(n)


def _sds(shape, dtype):
    return jax.ShapeDtypeStruct(shape, dtype)


def _silu(x):
    return x * jax.nn.sigmoid(x)


def _dsilu(x):
    s = jax.nn.sigmoid(x)
    return s * (1.0 + x * (1.0 - s))


def _norm_fwd(x, g, res, out_dtype, name):
    n, d = x.shape
    tr = _row_tile(n, 384)

    def body(*refs):
        if res is None:
            x_ref, g_ref, o_ref = refs
        else:
            x_ref, g_ref, r_ref, o_ref = refs
        xv = x_ref[...]
        r = lax.rsqrt(jnp.mean(xv * xv, axis=-1, keepdims=True) + EPS)
        y = xv * r * g_ref[...]
        if res is not None:
            y = y + r_ref[...]
        o_ref[...] = y.astype(o_ref.dtype)

    row = pl.BlockSpec((tr, d), lambda i: (i, 0))
    ins = [row, pl.BlockSpec((1, d), lambda i: (0, 0))] + ([row] if res is not None else [])
    args = (x, g) + ((res,) if res is not None else ())
    return pl.pallas_call(body, grid=(n // tr,), in_specs=ins, out_specs=row, out_shape=_sds((n, d), out_dtype), name=name)(*args)


def _norm_bwd(dy, x, g, res, out_dtype, name):
    n, d = x.shape
    tr = _row_tile(n, 384)

    def body(*refs):
        if res is None:
            dy_ref, x_ref, g_ref, dx_ref, dg_ref = refs
        else:
            dy_ref, x_ref, g_ref, r_ref, dx_ref, dg_ref = refs
        i = pl.program_id(0)
        xv = x_ref[...]
        r = lax.rsqrt(jnp.mean(xv * xv, axis=-1, keepdims=True) + EPS)
        xhat = xv * r
        dyf = dy_ref[...].astype(F32)
        gdy = dyf * g_ref[...]
        dx = r * (gdy - xhat * jnp.mean(gdy * xhat, axis=-1, keepdims=True))
        if res is not None:
            dx = dx + r_ref[...]
        dx_ref[...] = dx.astype(dx_ref.dtype)

        @pl.when(i == 0)
        def _():
            dg_ref[...] = jnp.zeros_like(dg_ref)

        dg_ref[...] += jnp.sum(dyf * xhat, axis=0, keepdims=True)

    row = pl.BlockSpec((tr, d), lambda i: (i, 0))
    vec = pl.BlockSpec((1, d), lambda i: (0, 0))
    ins = [row, row, vec] + ([row] if res is not None else [])
    args = (dy, x, g) + ((res,) if res is not None else ())
    return pl.pallas_call(body, grid=(n // tr,), in_specs=ins, out_specs=(row, vec),
                          out_shape=(_sds((n, d), out_dtype), _sds((1, d), F32)), name=name,
                          compiler_params=pltpu.CompilerParams(dimension_semantics=("arbitrary",)))(*args)


def _mm(a, b, *, ta, tb, grid, a_blk, a_map, b_blk, b_map, o_blk, o_map, o_shape, o_dtype, name, acc=None):
    nk = grid[2]
    dims = (((0,) if ta else (1,), (1,) if tb else (0,)), ((), ()))

    def body(*refs):
        if acc is None:
            a_ref, b_ref, o_ref = refs[:3]
            c_ref = None
        else:
            a_ref, b_ref, c_ref, o_ref = refs[:4]
        part = lax.dot_general(a_ref[...], b_ref[...], dims, preferred_element_type=F32)
        if nk == 1:
            if c_ref is not None:
                part = part + c_ref[...].astype(F32)
            o_ref[...] = part.astype(o_ref.dtype)
            return
        acc_ref = refs[-1]
        k = pl.program_id(2)

        @pl.when(k == 0)
        def _():
            acc_ref[...] = jnp.zeros_like(acc_ref) if c_ref is None else c_ref[...].astype(F32)

        acc_ref[...] += part

        @pl.when(k == nk - 1)
        def _():
            o_ref[...] = acc_ref[...].astype(o_ref.dtype)

    ins = [pl.BlockSpec(a_blk, a_map), pl.BlockSpec(b_blk, b_map)]
    args = [a, b]
    if acc is not None:
        ins.append(pl.BlockSpec(o_blk, o_map))
        args.append(acc)
    return pl.pallas_call(body, grid=grid, in_specs=ins, out_specs=pl.BlockSpec(o_blk, o_map), out_shape=_sds(o_shape, o_dtype),
                          scratch_shapes=[pltpu.VMEM(o_blk, F32)] if nk > 1 else [], name=name,
                          compiler_params=pltpu.CompilerParams(dimension_semantics=("parallel", "parallel", "arbitrary")))(*args)


TOKEN_TILE = 1056
WIDE_TILE = 1408
DEEP_K_TOKEN_TILE = 528


def _mm_nn(x, w, o_dtype, tn, tk, name, acc=None, tm_cap=TOKEN_TILE):
    n, k = x.shape
    tm = _row_tile(n, tm_cap)
    return _mm(x, w, ta=False, tb=False, grid=(n // tm, w.shape[1] // tn, k // tk), a_blk=(tm, tk), a_map=lambda i, j, kk: (i, kk),
               b_blk=(tk, tn), b_map=lambda i, j, kk: (kk, j), o_blk=(tm, tn), o_map=lambda i, j, kk: (i, j),
               o_shape=(n, w.shape[1]), o_dtype=o_dtype, name=name, acc=acc)


def _mm_nt(dy, w, o_dtype, tn, tk, name, acc=None):
    n, k = dy.shape
    tm = _row_tile(n, TOKEN_TILE)
    return _mm(dy, w, ta=False, tb=True, grid=(n // tm, w.shape[0] // tn, k // tk), a_blk=(tm, tk), a_map=lambda i, j, kk: (i, kk),
               b_blk=(tn, tk), b_map=lambda i, j, kk: (j, kk), o_blk=(tm, tn), o_map=lambda i, j, kk: (i, j),
               o_shape=(n, w.shape[0]), o_dtype=o_dtype, name=name, acc=acc)


def _mm_tn(x, dy, tm, tn, name):
    n, m = x.shape
    tk = _row_tile(n, TOKEN_TILE)
    return _mm(x, dy, ta=True, tb=False, grid=(m // tm, dy.shape[1] // tn, n // tk), a_blk=(tk, tm), a_map=lambda i, j, kk: (kk, i),
               b_blk=(tk, tn), b_map=lambda i, j, kk: (kk, j), o_blk=(tm, tn), o_map=lambda i, j, kk: (i, j),
               o_shape=(m, dy.shape[1]), o_dtype=BF16, name=name)


def _rope_tables(n):
    pos = (jnp.arange(n) - PAD_FRONT).astype(F32)
    half = ROT_DIM // 2
    ang = pos[:, None] * (ROPE_THETA ** (-jnp.arange(half, dtype=F32) / half))[None, :]
    c, s = jnp.cos(ang), jnp.sin(ang)
    rest = ATT_HEAD_DIM - ROT_DIM
    cos_a = jnp.concatenate([c, c, jnp.ones((n, rest), F32)], axis=1)
    sin_a = jnp.concatenate([-s, s, jnp.zeros((n, rest), F32)], axis=1)
    half = RET_HEAD_DIM // 2
    ang = pos[:, None] * (RET_THETA ** (-jnp.arange(half, dtype=F32) / half))[None, :]
    c, s = jnp.cos(ang), jnp.sin(ang)
    perm = np.zeros((ATT_HEAD_DIM, ATT_HEAD_DIM), np.float32)
    for i in range(ROT_DIM):
        perm[(i + ROT_DIM // 2) % ROT_DIM, i] = 1.0
    return cos_a, sin_a, jnp.concatenate([c, c], axis=1), jnp.concatenate([-s, s], axis=1), jnp.asarray(perm, BF16)


def _rope_att(x, col0, heads, cos, sin, perm, name):
    n = x.shape[0]
    tr = _row_tile(n, 1056)

    def body(x_ref, c_ref, s_ref, p_ref, o_ref):
        xb = x_ref[...].astype(BF16)
        sw = jnp.dot(xb, p_ref[...], preferred_element_type=F32)
        o_ref[...] = (xb.astype(F32) * c_ref[...] + sw * s_ref[...]).astype(o_ref.dtype)

    hd = ATT_HEAD_DIM
    tab = pl.BlockSpec((tr, hd), lambda i, h: (i, 0))
    return pl.pallas_call(body, grid=(n // tr, heads),
                          in_specs=[pl.BlockSpec((tr, hd), lambda i, h: (i, col0 + h)), tab, tab, pl.BlockSpec((hd, hd), lambda i, h: (0, 0))],
                          out_specs=pl.BlockSpec((tr, hd), lambda i, h: (i, h)), out_shape=_sds((n, heads * hd), BF16), name=name)(x, cos, sin, perm)


def _rope_ret(x, col0, cos, sin, name):
    p, n, _ = x.shape
    tr = _row_tile(n, 1056)
    hd = RET_HEAD_DIM

    def body(*refs):
        x_ref, o_ref = refs[0], refs[-1]
        xv = x_ref[0].astype(F32)
        for q in range(1, p):
            xv = xv + x_ref[q].astype(F32)
        if cos is not None:
            sw = jnp.concatenate([xv[:, hd // 2:], xv[:, :hd // 2]], axis=1)
            xv = xv * refs[1][...] + sw * refs[2][...]
        o_ref[...] = xv.astype(o_ref.dtype)

    tab = pl.BlockSpec((tr, hd), lambda i, h: (i, 0))
    ins = [pl.BlockSpec((p, tr, hd), lambda i, h: (0, i, col0 + h))] + ([tab, tab] if cos is not None else [])
    args = (x,) + ((cos, sin) if cos is not None else ())
    return pl.pallas_call(body, grid=(n // tr, RET_HEADS), in_specs=ins, out_specs=pl.BlockSpec((tr, hd), lambda i, h: (i, h)),
                          out_shape=_sds((n, RET_WIDTH), BF16), name=name)(*args)


def _att_mask(nblk, n_tot):
    row = lax.broadcasted_iota(jnp.int32, (BLOCK, 4 * BLOCK), 0)
    col = lax.broadcasted_iota(jnp.int32, (BLOCK, 4 * BLOCK), 1)
    qi = nblk * BLOCK + row
    seg = col // BLOCK
    cj = col % BLOCK
    kj = (nblk - 1 + seg) * BLOCK + cj
    band = (jnp.abs(qi - kj) <= WINDOW) & (kj >= PAD_FRONT) & (kj < n_tot) & (seg < 3)
    meta = (seg == 3) & (cj >= PAD_FRONT) & (jnp.abs(qi - cj) > WINDOW)
    return band | meta


def _att_specs(nb, v_col):
    kv = lambda f, cb: pl.BlockSpec((BLOCK, KV_WIDTH), lambda n: (f(n), cb))
    prev, own, nxt, first = (lambda n: jnp.maximum(n - 1, 0)), (lambda n: n), (lambda n: jnp.minimum(n + 1, nb - 1)), (lambda n: 0)
    return [kv(f, 0) for f in (prev, own, nxt, first)] + [kv(f, v_col) for f in (prev, own, nxt, first)]


def _att_probs(s, ok, snk):
    s = jnp.where(ok, s, NEG)
    m = jnp.maximum(jnp.max(s, axis=-1, keepdims=True), snk)
    p = jnp.exp(s - m)
    ps = jnp.exp(snk - m)
    inv = 1.0 / (jnp.sum(p, axis=-1, keepdims=True) + ps)
    return p * inv, ps * inv


def _att_fwd(q, k, proj, sink_b, name):
    n = q.shape[0]
    nb = n // BLOCK
    hd = ATT_HEAD_DIM

    def body(q_ref, kp, ko, kn, km, vp, vo, vn, vm, sink_ref, o_ref):
        nblk = pl.program_id(0)
        ok = _att_mask(nblk, n)
        keep = (nblk * BLOCK + lax.broadcasted_iota(jnp.int32, (BLOCK, 1), 0)) >= PAD_FRONT
        for kh in range(ATT_KV_HEADS):
            cs = slice(kh * hd, (kh + 1) * hd)
            kk = jnp.concatenate([r[:, cs] for r in (kp, ko, kn, km)], axis=0)
            vv = jnp.concatenate([r[:, cs] for r in (vp, vo, vn, vm)], axis=0)
            heads = [kh * ATT_GROUP + g for g in range(ATT_GROUP)]
            q4 = jnp.concatenate([q_ref[:, h * hd:(h + 1) * hd] for h in heads], axis=0)
            s = lax.dot_general(q4, kk, (((1,), (1,)), ((), ())), preferred_element_type=F32) * ATT_SCALE
            ps = []
            for g, h in enumerate(heads):
                p, _ = _att_probs(s[g * BLOCK:(g + 1) * BLOCK], ok, sink_ref[h:h + 1, 0:1])
                ps.append(p)
            o = jnp.dot(jnp.concatenate(ps, axis=0).astype(BF16), vv, preferred_element_type=F32)
            for g, h in enumerate(heads):
                o_ref[:, h * hd:(h + 1) * hd] = jnp.where(keep, o[g * BLOCK:(g + 1) * BLOCK], 0.0).astype(o_ref.dtype)

    qspec = pl.BlockSpec((BLOCK, ATT_WIDTH), lambda i: (i, 0))
    return pl.pallas_call(body, grid=(nb,), in_specs=[qspec] + _att_specs(nb, COL_AV256) + [pl.BlockSpec((ATT_HEADS, 128), lambda i: (0, 0))],
                          out_specs=qspec, out_shape=_sds((n, ATT_WIDTH), BF16), name=name)(q, k, k, k, k, proj, proj, proj, proj, sink_b)


def _att_bwd(q, k, proj, sink_b, dmixed, name):
    n = q.shape[0]
    nb = n // BLOCK
    hd = ATT_HEAD_DIM

    def body(q_ref, kp, ko, kn, km, vp, vo, vn, vm, sink_ref, do_ref, dq_ref, dk_ref, dv_ref, dsink_ref):
        nblk = pl.program_id(0)

        @pl.when(nblk == 0)
        def _():
            dk_ref[...] = jnp.zeros_like(dk_ref)
            dv_ref[...] = jnp.zeros_like(dv_ref)
            dsink_ref[...] = jnp.zeros_like(dsink_ref)

        ok = _att_mask(nblk, n)
        rows = [jnp.maximum(nblk - 1, 0), nblk, jnp.minimum(nblk + 1, nb - 1), 0]
        for kh in range(ATT_KV_HEADS):
            cs = slice(kh * hd, (kh + 1) * hd)
            kk = jnp.concatenate([r[:, cs] for r in (kp, ko, kn, km)], axis=0)
            vv = jnp.concatenate([r[:, cs] for r in (vp, vo, vn, vm)], axis=0)
            heads = [kh * ATT_GROUP + g for g in range(ATT_GROUP)]
            q4 = jnp.concatenate([q_ref[:, h * hd:(h + 1) * hd] for h in heads], axis=0)
            do4 = jnp.concatenate([do_ref[:, h * hd:(h + 1) * hd] for h in heads], axis=0)
            s = lax.dot_general(q4, kk, (((1,), (1,)), ((), ())), preferred_element_type=F32) * ATT_SCALE
            dp = lax.dot_general(do4, vv, (((1,), (1,)), ((), ())), preferred_element_type=F32)
            ps, dss = [], []
            for g, h in enumerate(heads):
                p, psink = _att_probs(s[g * BLOCK:(g + 1) * BLOCK], ok, sink_ref[h:h + 1, 0:1])
                dpg = dp[g * BLOCK:(g + 1) * BLOCK]
                delta = jnp.sum(p * dpg, axis=-1, keepdims=True)
                ps.append(p)
                dss.append(p * (dpg - delta) * ATT_SCALE)
                dsink_ref[h:h + 1, :] = dsink_ref[h:h + 1, :] - jnp.sum(psink * delta, axis=0, keepdims=True)
            ds = jnp.concatenate(dss, axis=0).astype(BF16)
            pb = jnp.concatenate(ps, axis=0).astype(BF16)
            dq = jnp.dot(ds, kk, preferred_element_type=F32)
            for g, h in enumerate(heads):
                dq_ref[:, h * hd:(h + 1) * hd] = dq[g * BLOCK:(g + 1) * BLOCK].astype(dq_ref.dtype)
            dk = lax.dot_general(ds, q4, (((0,), (0,)), ((), ())), preferred_element_type=F32)
            dv = lax.dot_general(pb, do4, (((0,), (0,)), ((), ())), preferred_element_type=F32)
            for seg, r in enumerate(rows):
                at = (pl.ds(pl.multiple_of(r * BLOCK, BLOCK), BLOCK), cs)
                dk_ref[at] += dk[seg * BLOCK:(seg + 1) * BLOCK]
                dv_ref[at] += dv[seg * BLOCK:(seg + 1) * BLOCK]

    qspec = pl.BlockSpec((BLOCK, ATT_WIDTH), lambda i: (i, 0))
    whole = pl.BlockSpec((n, KV_WIDTH), lambda i: (0, 0))
    sinks = pl.BlockSpec((ATT_HEADS, 128), lambda i: (0, 0))
    return pl.pallas_call(body, grid=(nb,), in_specs=[qspec] + _att_specs(nb, COL_AV256) + [sinks, qspec], out_specs=(qspec, whole, whole, sinks),
                          out_shape=(_sds((n, ATT_WIDTH), BF16), _sds((n, KV_WIDTH), F32), _sds((n, KV_WIDTH), F32), _sds((ATT_HEADS, 128), F32)),
                          name=name, compiler_params=pltpu.CompilerParams(dimension_semantics=("arbitrary",)))(
                              q, k, k, k, k, proj, proj, proj, proj, sink_b, dmixed)


def _ret_decay(lg, d):
    a = lax.broadcasted_iota(jnp.int32, (BLOCK, 1), 0)
    b = lax.broadcasted_iota(jnp.int32, (1, BLOCK), 1)
    t_col = a + d * (BLOCK - 1 - 2 * a)
    t_row = b + d * (BLOCK - 1 - 2 * b)
    diff = t_col - t_row
    dist = jnp.maximum(diff, 0).astype(F32)
    dmask = jnp.where(diff >= d, jnp.exp(lg * dist), 0.0)
    tf = t_col.astype(F32)
    xi = jnp.exp(lg * (tf + 1.0))
    zeta = jnp.exp(lg * (BLOCK - 1.0 - tf))
    gam = jnp.exp(jnp.full((1, 1), BLOCK, F32) * lg)
    return dmask, dist, xi, zeta, gam, tf


def _ret_fwd(q, k, proj, lg, name):
    n = q.shape[0]
    nc = n // BLOCK
    hd = RET_HEAD_DIM
    chunk = lambda d, c: c + d * (nc - 1 - 2 * c)

    def body(lg_ref, q_ref, k_ref, v0, v1, v2, v3, o_ref, st_ref, s_ref):
        d, c = pl.program_id(0), pl.program_id(1)

        @pl.when(c == 0)
        def _():
            s_ref[...] = jnp.zeros_like(s_ref)

        for h, v_ref in enumerate((v0, v1, v2, v3)):
            cs = slice(h * hd, (h + 1) * hd)
            dmask, _, xi, zeta, gam, _ = _ret_decay(lg_ref[h, d], d)
            qv = q_ref[:, cs]
            kf = k_ref[:, cs].astype(F32) * RET_K_SCALE
            vv = v_ref[...]
            s = lax.dot_general(qv, kf.astype(BF16), (((1,), (1,)), ((), ())), preferred_element_type=F32)
            sb = s_ref[h]
            o_ref[:, cs] = (jnp.dot((s * dmask).astype(BF16), vv, preferred_element_type=F32)
                            + jnp.dot((qv.astype(F32) * xi).astype(BF16), sb.astype(BF16), preferred_element_type=F32))
            st_ref[h] = sb
            s_ref[h] = gam * sb + lax.dot_general((kf * zeta).astype(BF16), vv, (((0,), (0,)), ((), ())), preferred_element_type=F32)

    wide = pl.BlockSpec((BLOCK, RET_WIDTH), lambda d, c: (chunk(d, c), 0))
    vblk = lambda h: pl.BlockSpec((BLOCK, hd), lambda d, c: (chunk(d, c), COL_RV + h))
    return pl.pallas_call(
        body, grid=(2, nc), in_specs=[pl.BlockSpec(memory_space=pltpu.SMEM), wide, wide] + [vblk(h) for h in range(RET_HEADS)],
        out_specs=(pl.BlockSpec((None, BLOCK, RET_WIDTH), lambda d, c: (d, chunk(d, c), 0)),
                   pl.BlockSpec((RET_HEADS, None, None, hd, hd), lambda d, c: (0, d, c, 0, 0))),
        out_shape=(_sds((2, n, RET_WIDTH), F32), _sds((RET_HEADS, 2, nc, hd, hd), F32)), scratch_shapes=[pltpu.VMEM((RET_HEADS, hd, hd), F32)],
        name=name, compiler_params=pltpu.CompilerParams(dimension_semantics=("parallel", "arbitrary")))(lg, q, k, proj, proj, proj, proj)


def _ret_bwd(q, k, proj, lg, do, states, name):
    n = q.shape[0]
    nc = n // BLOCK
    hd = RET_HEAD_DIM
    chunk = lambda d, r: (nc - 1 - r) + d * (2 * r - (nc - 1))

    def body(lg_ref, q_ref, k_ref, v0, v1, v2, v3, do_ref, st_ref, dq_ref, dk_ref, dv_ref, dlg_ref, ds_ref):
        d, r = pl.program_id(0), pl.program_id(1)

        @pl.when(r == 0)
        def _():
            ds_ref[...] = jnp.zeros_like(ds_ref)
            dlg_ref[...] = jnp.zeros_like(dlg_ref)

        row = lax.broadcasted_iota(jnp.int32, (BLOCK, 1), 0) + chunk(d, r) * BLOCK
        keep = row >= PAD_FRONT
        nt = (((1,), (1,)), ((), ()))
        tn = (((0,), (0,)), ((), ()))
        for h, v_ref in enumerate((v0, v1, v2, v3)):
            cs = slice(h * hd, (h + 1) * hd)
            dmask, dist, xi, zeta, gam, tf = _ret_decay(lg_ref[h, d], d)
            qv, vv, dov = q_ref[:, cs], v_ref[...], do_ref[:, cs]
            qf = qv.astype(F32)
            kf = k_ref[:, cs].astype(F32) * RET_K_SCALE
            kb = kf.astype(BF16)
            sc = st_ref[h]
            dsn = ds_ref[h]
            s = lax.dot_general(qv, kb, nt, preferred_element_type=F32)
            dsc = lax.dot_general(dov, vv, nt, preferred_element_type=F32) * dmask
            dsb = dsc.astype(BF16)
            dq_c = xi * lax.dot_general(dov, sc.astype(BF16), nt, preferred_element_type=F32)
            dk_c = zeta * lax.dot_general(vv, dsn.astype(BF16), nt, preferred_element_type=F32)
            dq = jnp.dot(dsb, kb, preferred_element_type=F32) + dq_c
            dk = lax.dot_general(dsb, qv, tn, preferred_element_type=F32) + dk_c
            dv = (lax.dot_general((s * dmask).astype(BF16), dov, tn, preferred_element_type=F32)
                  + jnp.dot((kf * zeta).astype(BF16), dsn.astype(BF16), preferred_element_type=F32))
            ds_ref[h] = gam * dsn + lax.dot_general((qf * xi).astype(BF16), dov, tn, preferred_element_type=F32)
            dlg = (jnp.sum(dsc * s * dist, keepdims=True)
                   + jnp.sum((tf + 1.0) * jnp.sum(qf * dq_c, axis=-1, keepdims=True), keepdims=True)
                   + jnp.sum((BLOCK - 1.0 - tf) * jnp.sum(kf * dk_c, axis=-1, keepdims=True), keepdims=True)
                   + BLOCK * gam * jnp.sum(dsn * sc, keepdims=True))
            dlg_ref[h] += dlg
            dq_ref[:, cs] = dq
            dk_ref[:, cs] = jnp.where(keep, dk * RET_K_SCALE, 0.0)
            dv_ref[:, cs] = jnp.where(keep, dv, 0.0)

    wide = pl.BlockSpec((BLOCK, RET_WIDTH), lambda d, r: (chunk(d, r), 0))
    vblk = lambda h: pl.BlockSpec((BLOCK, hd), lambda d, r: (chunk(d, r), COL_RV + h))
    plane = pl.BlockSpec((None, BLOCK, RET_WIDTH), lambda d, r: (d, chunk(d, r), 0))
    return pl.pallas_call(
        body, grid=(2, nc),
        in_specs=[pl.BlockSpec(memory_space=pltpu.SMEM), wide, wide] + [vblk(h) for h in range(RET_HEADS)]
        + [wide, pl.BlockSpec((RET_HEADS, None, None, hd, hd), lambda d, r: (0, d, nc - 1 - r, 0, 0))],
        out_specs=(plane, plane, plane, pl.BlockSpec((RET_HEADS, None, 8, 128), lambda d, r: (0, d, 0, 0))),
        out_shape=(_sds((2, n, RET_WIDTH), F32),) * 3 + (_sds((RET_HEADS, 2, 8, 128), F32),),
        scratch_shapes=[pltpu.VMEM((RET_HEADS, hd, hd), F32)], name=name,
        compiler_params=pltpu.CompilerParams(dimension_semantics=("parallel", "arbitrary")))(lg, q, k, proj, proj, proj, proj, do, states)


def _retgate_fwd(o, proj, gain, name):
    _, n, _ = o.shape
    tr = _row_tile(n, 1056)
    hd = RET_HEAD_DIM

    def body(o_ref, rg_ref, g_ref, y_ref):
        ov = o_ref[0] + o_ref[1]
        r = lax.rsqrt(jnp.mean(ov * ov, axis=-1, keepdims=True) + EPS)
        y_ref[...] = (_silu(rg_ref[...].astype(F32)) * (ov * r * g_ref[...])).astype(y_ref.dtype)

    return pl.pallas_call(body, grid=(n // tr, RET_HEADS),
                          in_specs=[pl.BlockSpec((2, tr, hd), lambda i, h: (0, i, h)), pl.BlockSpec((tr, hd), lambda i, h: (i, COL_RG + h)),
                                    pl.BlockSpec((1, hd), lambda i, h: (0, h))],
                          out_specs=pl.BlockSpec((tr, hd), lambda i, h: (i, h)), out_shape=_sds((n, RET_WIDTH), BF16), name=name)(o, proj, gain)


def _retgate_bwd(dmixed, o, proj, gain, name):
    _, n, _ = o.shape
    tr = _row_tile(n, 1056)
    hd = RET_HEAD_DIM

    def body(dy_ref, o_ref, rg_ref, g_ref, do_ref, drg_ref, dg_ref):
        i = pl.program_id(1)
        ov = o_ref[0] + o_ref[1]
        r = lax.rsqrt(jnp.mean(ov * ov, axis=-1, keepdims=True) + EPS)
        xhat = ov * r
        rg = rg_ref[...].astype(F32)
        dy = dy_ref[...].astype(F32)
        drg_ref[...] = (dy * (xhat * g_ref[...]) * _dsilu(rg)).astype(drg_ref.dtype)
        dn = dy * _silu(rg)
        dxh = dn * g_ref[...]
        do_ref[...] = (r * (dxh - xhat * jnp.mean(dxh * xhat, axis=-1, keepdims=True))).astype(do_ref.dtype)

        @pl.when(i == 0)
        def _():
            dg_ref[...] = jnp.zeros_like(dg_ref)

        dg_ref[...] += jnp.sum(dn * xhat, axis=0, keepdims=True)

    tile = pl.BlockSpec((tr, hd), lambda h, i: (i, h))
    vec = pl.BlockSpec((1, hd), lambda h, i: (0, h))
    return pl.pallas_call(body, grid=(RET_HEADS, n // tr),
                          in_specs=[pl.BlockSpec((tr, hd), lambda h, i: (i, RET_HEADS + h)), pl.BlockSpec((2, tr, hd), lambda h, i: (0, i, h)),
                                    pl.BlockSpec((tr, hd), lambda h, i: (i, COL_RG + h)), vec],
                          out_specs=(tile, tile, vec), out_shape=(_sds((n, RET_WIDTH), BF16), _sds((n, RET_WIDTH), BF16), _sds((1, RET_WIDTH), F32)),
                          name=name, compiler_params=pltpu.CompilerParams(dimension_semantics=("parallel", "arbitrary")))(dmixed, o, proj, gain)


def _swiglu_fwd(gate, up, name):
    n, f = gate.shape
    tr, tc = _row_tile(n, 384), f // 2

    def body(g_ref, u_ref, o_ref):
        o_ref[...] = (_silu(g_ref[...].astype(F32)) * u_ref[...].astype(F32)).astype(o_ref.dtype)

    t = pl.BlockSpec((tr, tc), lambda i, j: (i, j))
    return pl.pallas_call(body, grid=(n // tr, 2), in_specs=[t, t], out_specs=t, out_shape=_sds((n, f), BF16), name=name)(gate, up)


def _swiglu_bwd(df, gate, up, name):
    n, f = gate.shape
    tr, tc = _row_tile(n, 384), f // 2

    def body(d_ref, g_ref, u_ref, dg_ref, du_ref):
        d, g = d_ref[...].astype(F32), g_ref[...].astype(F32)
        dg_ref[...] = (d * u_ref[...].astype(F32) * _dsilu(g)).astype(dg_ref.dtype)
        du_ref[...] = (d * _silu(g)).astype(du_ref.dtype)

    t = pl.BlockSpec((tr, tc), lambda i, j: (i, j))
    return pl.pallas_call(body, grid=(n // tr, 2), in_specs=[t, t, t], out_specs=(t, t), out_shape=(_sds((n, f), BF16),) * 2, name=name)(df, gate, up)


def _loss_head(h, target, name):
    n, d = h.shape
    nb = n // BLOCK

    def body(h_ref, t_ref, dh_ref, l_ref):
        i = pl.program_id(0)

        @pl.when(i == 0)
        def _():
            l_ref[...] = jnp.zeros_like(l_ref)
            dh_ref[...] = jnp.zeros_like(dh_ref)

        @pl.when(i > 0)
        def _():
            e = h_ref[...] - t_ref[...]
            dh_ref[...] = e * (1.0 / d)
            l_ref[...] += 0.5 * jnp.sum(jnp.mean(e * e, axis=-1, keepdims=True), keepdims=True)

    blk = pl.BlockSpec((BLOCK, d), lambda i: (i, 0))
    return pl.pallas_call(body, grid=(nb,), in_specs=[blk, pl.BlockSpec((BLOCK, d), lambda i: (jnp.maximum(i - 1, 0), 0))],
                          out_specs=(blk, pl.BlockSpec((8, 128), lambda i: (0, 0))), out_shape=(_sds((n, d), F32), _sds((8, 128), F32)), name=name,
                          compiler_params=pltpu.CompilerParams(dimension_semantics=("arbitrary",)))(h, target)


def _adamw(parts, w, m, v, name, sel=None, layer=None, prev=None):
    s, (r, c) = parts.shape[0], parts.shape[-2:]
    tr = _row_tile(r, max(16, (ADAMW_TILE_ELEMS // c) // 16 * 16))
    b1c, b2c = 1.0 - ADAM_B1 ** ADAM_STEP, 1.0 - ADAM_B2 ** ADAM_STEP

    def body(p_ref, w_ref, m_ref, v_ref, *rest):
        g_ref, d_ref, mo_ref, vo_ref = rest[-4:]
        g = p_ref[0].astype(F32)
        for q in range(1, s):
            g = g + p_ref[q].astype(F32)
        mn = ADAM_B1 * m_ref[...] + (1.0 - ADAM_B1) * g
        vn = ADAM_B2 * v_ref[...] + (1.0 - ADAM_B2) * jnp.square(g)
        g_ref[...] = g
        mo_ref[...] = mn
        vo_ref[...] = vn
        d_ref[...] = -ADAM_LR * ((mn / b1c) / (jnp.sqrt(vn / b2c) + ADAM_EPS) + ADAM_WD * w_ref[...])

    pspec = (pl.BlockSpec((s, tr, c), lambda i: (0, i, 0)) if sel is None else pl.BlockSpec((s, None, tr, c), lambda i: (0, sel, i, 0)))
    if layer is None:
        t = pl.BlockSpec((tr, c), lambda i: (i, 0))
        return pl.pallas_call(body, grid=(r // tr,), in_specs=[pspec, t, t, t], out_specs=(t, t, t, t), out_shape=(_sds((r, c), F32),) * 4,
                              name=name)(parts, w, m, v)
    t = pl.BlockSpec((None, tr, c), lambda i: (layer, i, 0))
    prev = prev if prev is not None else tuple(lax.empty(w.shape, F32) for _ in range(4))
    return pl.pallas_call(body, grid=(r // tr,), in_specs=[pspec, t, t, t] + [ANY] * 4, out_specs=(t, t, t, t), out_shape=(_sds(w.shape, F32),) * 4,
                          input_output_aliases={4 + i: i for i in range(4)}, name=name)(parts, w, m, v, *prev)


def _allgather(xs, name):
    na = len(xs)

    def body(*refs):
        x_refs, o_refs = refs[:na], refs[na:2 * na]
        send, recv, lsem = refs[2 * na:]
        x, y, c = lax.axis_index("x"), lax.axis_index("y"), lax.axis_index("c")
        me, sib = (x, y, c), (x, y, 1 - c)
        chips = [(1 - x, y), (x, 1 - y), (1 - x, 1 - y)]
        slot = lambda p: 4 * p[0] + 2 * p[1] + p[2]

        def copy(a, k, block, to, src=None):
            dst = o_refs[a].at[slot(block)]
            return pltpu.make_async_remote_copy(src_ref=dst if src is None else src, dst_ref=dst, send_sem=send.at[a, k], recv_sem=recv.at[a, k],
                                                device_id=to, device_id_type=MESH)

        mine = [pltpu.make_async_copy(x_refs[a], o_refs[a].at[slot(me)], lsem.at[a]) for a in range(na)]
        for cp in mine:
            cp.start()
        first = []
        for a in range(na):
            first.append(copy(a, 0, me, sib, src=x_refs[a]))
            first += [copy(a, 1 + j, me, (*chip, c), src=x_refs[a]) for j, chip in enumerate(chips)]
        for cp in first:
            cp.start()
        passed = []
        for j, chip in enumerate(chips):
            for a in range(na):
                copy(a, 1 + j, (*chip, c), me).wait_recv()
                passed.append(copy(a, 4 + j, (*chip, c), sib))
                passed[-1].start()
        for a in range(na):
            copy(a, 0, sib, me).wait_recv()
            for j, chip in enumerate(chips):
                copy(a, 4 + j, (*chip, 1 - c), me).wait_recv()
        for cp in first + passed:
            cp.wait_send()
        for cp in mine:
            cp.wait()

    return pl.pallas_call(body, in_specs=[ANY] * na, out_specs=[ANY] * na, out_shape=[_sds((N_DEV,) + t.shape, t.dtype) for t in xs],
                          scratch_shapes=[pltpu.SemaphoreType.DMA((na, 7)), pltpu.SemaphoreType.DMA((na, 7)), pltpu.SemaphoreType.DMA((na,))],
                          name=name)(*xs)


HBM = pl.BlockSpec(memory_space=pltpu.HBM)
SEM = pl.BlockSpec(memory_space=pltpu.SEMAPHORE)
EFFECT = pltpu.SideEffectType.DATAFLOW_SIDE_EFFECTING


SPLIT_RELATIONS = dict(gather=(1, 2, 4, 6),
                       forward=(2, 4, 6),
                       scatter=tuple(range(1, N_DEV)))


def _split_copies(mode, x_refs, land_refs, send, recv, own, landing):
    x, y, c = lax.axis_index("x"), lax.axis_index("y"), lax.axis_index("c")
    flip = lambda r: ((1 - x if r & 4 else x), (1 - y if r & 2 else y), (1 - c if r & 1 else c))
    slot = lambda p: 4 * p[0] + 2 * p[1] + p[2]
    me = slot((x, y, c))
    rel = SPLIT_RELATIONS[mode]
    local, remote = [], []
    for a in range(len(land_refs)):
        if mode != "forward":
            local.append(pltpu.make_async_copy(x_refs[a].at[me] if mode == "scatter" else x_refs[a], land_refs[a].at[me], own.at[a]))
        for j, r in enumerate(rel):
            if mode == "forward":
                to = flip(1)
                src = dst = land_refs[a].at[slot(flip(r ^ 1) if landing else flip(r))]
            else:
                to = flip(r)
                src = x_refs[a].at[slot(to)] if mode == "scatter" else x_refs[a]
                dst = land_refs[a].at[slot(to) if landing else me]
            remote.append(pltpu.make_async_remote_copy(src_ref=src, dst_ref=dst, send_sem=send.at[len(rel) * a + j],
                                                       recv_sem=recv.at[len(rel) * a + j], device_id=to, device_id_type=MESH))
    return local, remote


def _send_start(mode, xs, lands, after, name):
    if lands is None:
        lands = [lax.empty(t.shape if mode == "scatter" else (N_DEV,) + t.shape, t.dtype) for t in xs]
    nx, na, nr = len(xs), len(lands), len(SPLIT_RELATIONS[mode])
    nsem = 2 if mode == "forward" else 3

    def body(*refs):
        x_refs, land_refs = refs[:nx], refs[nx:nx + na]
        sems = refs[nx + na + 1:nx + na + 1 + nsem]
        local, remote = _split_copies(mode, x_refs, land_refs, sems[0], sems[1], sems[2] if nsem == 3 else None, False)
        for cp in remote + local:
            cp.start()
        refs[-1][...] = jnp.zeros_like(refs[-1])

    hbm = lambda t: pltpu.with_memory_space_constraint(t, pltpu.HBM)
    sem_shapes = [pltpu.SemaphoreType.DMA((nr * na,)), pltpu.SemaphoreType.DMA((nr * na,)), pltpu.SemaphoreType.DMA((na,))][:nsem]
    outs = pl.pallas_call(
        body, name=name,
        out_shape=(*sem_shapes, *[pltpu.HBM(t.shape, t.dtype) for t in list(xs) + list(lands)], _sds((8, 128), F32)),
        in_specs=[HBM] * (nx + na) + [ANY], out_specs=(*[SEM] * nsem, *[HBM] * (nx + na), pl.BlockSpec(memory_space=pltpu.VMEM)),
        input_output_aliases={i: nsem + i for i in range(nx + na)},
        compiler_params=pltpu.CompilerParams(has_side_effects=EFFECT))(*[hbm(t) for t in list(xs) + list(lands)], after)
    return outs[:nsem], list(outs[nsem:nsem + nx]), list(outs[nsem + nx:nsem + nx + na]), outs[-1]


def _send_wait(mode, started, after, name):
    sems, xs, lands, _ = started
    nx, na, nsem = len(xs), len(lands), len(sems)

    def body(*refs):
        s = refs[nx + na:nx + na + nsem]
        local, remote = _split_copies(mode, refs[:nx], refs[nx:nx + na], s[0], s[1], s[2] if nsem == 3 else None, True)
        for cp in remote:
            cp.wait_send()
            cp.wait_recv()
        for cp in local:
            cp.wait()

    outs = pl.pallas_call(body, name=name, out_shape=tuple(pltpu.HBM(t.shape, t.dtype) for t in xs + lands),
                          in_specs=[HBM] * (nx + na) + [SEM] * nsem + [ANY], out_specs=[HBM] * (nx + na),
                          input_output_aliases={i: i for i in range(nx + na)},
                          compiler_params=pltpu.CompilerParams(has_side_effects=EFFECT))(*xs, *lands, *sems, after)
    return list(outs[nx:])


def _local_step(x, meta, target, mix_weights_fn, ffn_weights_fn, grads_fn, sink, dec_f, dec_b, ret_norm, n_mix_pre, n_mix_post, n_ffn_pre,
                n_ffn_post):
    depth = n_mix_pre.shape[0]
    d = D_MODEL
    h = jnp.concatenate([jnp.zeros((PAD_FRONT, d), F32), meta, x], axis=0)
    n = h.shape[0]
    cos_a, sin_a, cos_r, sin_r, perm = _rope_tables(n)
    lg_all = jnp.stack([-jnp.exp(dec_f), -jnp.exp(dec_b)], axis=-1)
    saved = []
    for l in range(depth):
        t = f"l{l}_"
        sink_b = jnp.broadcast_to(sink[l][:, None], (ATT_HEADS, 128))
        wi, wo, tok = mix_weights_fn(l, h)
        u = _norm_fwd(h, (n_mix_pre[l] + tok)[None], None, BF16, t + "norm_mix_pre")
        proj = _mm_nt(u, wi, BF16, WIDE_TILE, d, t + "proj")
        aq = _rope_att(proj, 0, ATT_HEADS, cos_a, sin_a, perm, t + "rope_aq")
        ak = _rope_att(proj, COL_AK, ATT_KV_HEADS, cos_a, sin_a, perm, t + "rope_ak")
        att = _att_fwd(aq, ak, proj, sink_b, t + "att")
        proj3 = proj[None]
        rq = _rope_ret(proj3, COL_RQ, cos_r, sin_r, t + "rope_rq")
        rk = _rope_ret(proj3, COL_RK, cos_r, sin_r, t + "rope_rk")
        o_ret, states = _ret_fwd(rq, rk, proj, lg_all[l], t + "ret")
        retg = _retgate_fwd(o_ret, proj, ret_norm[l][None], t + "retgate")
        mixed = jnp.concatenate([att, retg], axis=1)
        mo = _mm_nn(mixed, wo, F32, 1024, d, t + "out_proj")
        h_mid = _norm_fwd(mo, n_mix_post[l][None], h, F32, t + "norm_mix_post")
        wg, wu, wd, tok = ffn_weights_fn(l, h_mid)
        u2 = _norm_fwd(h_mid, (n_ffn_pre[l] + tok)[None], None, BF16, t + "norm_ffn_pre")
        gate = _mm_nt(u2, wg, BF16, WIDE_TILE, d, t + "gate")
        up = _mm_nt(u2, wu, BF16, WIDE_TILE, d, t + "up")
        f = _swiglu_fwd(gate, up, t + "swiglu")
        dn = _mm_nn(f, wd, F32, 512, D_FF, t + "down", tm_cap=DEEP_K_TOKEN_TILE)
        h_out = _norm_fwd(dn, n_ffn_post[l][None], h_mid, F32, t + "norm_ffn_post")
        saved.append(dict(h=h, u=u, proj=proj, aq=aq, ak=ak, rq=rq, rk=rk, o_ret=o_ret, states=states, mixed=mixed, mo=mo, h_mid=h_mid, u2=u2,
                          gate=gate, up=up, f=f, dn=dn, sink_b=sink_b, wi=wi, wo=wo, wg=wg, wu=wu, wd=wd))
        h = h_out

    dh, loss_part = _loss_head(h, target, "loss_head")
    gs = dict(sink=[None] * depth, dec_f=[None] * depth, dec_b=[None] * depth, ret_norm=[None] * depth, mix_pre=[None] * depth,
              mix_post=[None] * depth, ffn_pre=[None] * depth, ffn_post=[None] * depth)
    tok_b = jnp.zeros((), F32)
    for l in reversed(range(depth)):
        t = f"l{l}_b_"
        sv = saved[l]
        proj = sv["proj"]
        d_dn, gs["ffn_post"][l] = _norm_bwd(dh, sv["dn"], (n_ffn_post[l] + tok_b)[None], None, BF16, t + "norm_ffn_post")
        gw = {}
        d_f = _mm_nt(d_dn, sv["wd"], BF16, WIDE_TILE, d, t + "d_f")
        gw["wd"] = _mm_tn(sv["f"], d_dn, WIDE_TILE, 1024, t + "dw_down")
        d_gate, d_up = _swiglu_bwd(d_f, sv["gate"], sv["up"], t + "swiglu")
        du2 = _mm_nn(d_gate, sv["wg"], F32, 512, D_FF, t + "du2_gate", tm_cap=DEEP_K_TOKEN_TILE)
        du2 = _mm_nn(d_up, sv["wu"], F32, 512, D_FF, t + "du2_up", acc=du2, tm_cap=DEEP_K_TOKEN_TILE)
        gw["wg"] = _mm_tn(d_gate, sv["u2"], WIDE_TILE, 1024, t + "dw_gate")
        gw["wu"] = _mm_tn(d_up, sv["u2"], WIDE_TILE, 1024, t + "dw_up")
        tok_b = grads_fn(l, "ffn", gw, du2)
        dh, gs["ffn_pre"][l] = _norm_bwd(du2, sv["h_mid"], (n_ffn_pre[l] + tok_b)[None], dh, F32, t + "norm_ffn_pre")
        d_mo, gs["mix_post"][l] = _norm_bwd(dh, sv["mo"], n_mix_post[l][None], None, BF16, t + "norm_mix_post")
        d_mixed = _mm_nt(d_mo, sv["wo"], BF16, 1024, d, t + "d_mixed")
        gw["wo"] = _mm_tn(sv["mixed"], d_mo, 1024, 1024, t + "dw_out")
        d_o, d_rg, gs["ret_norm"][l] = _retgate_bwd(d_mixed, sv["o_ret"], proj, ret_norm[l][None], t + "retgate")
        dq_r, dk_r, dv_r, dlg = _ret_bwd(sv["rq"], sv["rk"], proj, lg_all[l], d_o, sv["states"], t + "ret")
        draw = dlg[:, :, 0, 0] * lg_all[l]
        gs["dec_f"][l], gs["dec_b"][l] = draw[:, 0], draw[:, 1]
        dq_a, dk_a, dv_a, dsink = _att_bwd(sv["aq"], sv["ak"], proj, sv["sink_b"], d_mixed, t + "att")
        gs["sink"][l] = dsink[:, 0]
        dproj = jnp.concatenate([
            _rope_att(dq_a, 0, ATT_HEADS, cos_a, -sin_a, perm, t + "rope_aq"),
            _rope_att(dk_a, 0, ATT_KV_HEADS, cos_a, -sin_a, perm, t + "rope_ak"),
            dv_a.astype(BF16),
            _rope_ret(dq_r, 0, cos_r, -sin_r, t + "rope_rq"),
            _rope_ret(dk_r, 0, cos_r, -sin_r, t + "rope_rk"),
            _rope_ret(dv_r, 0, None, None, t + "sum_rv"),
            d_rg], axis=1)
        du = _mm_nn(dproj, sv["wi"], F32, 512, IN_COLS, t + "du", tm_cap=DEEP_K_TOKEN_TILE)
        gw["wi"] = _mm_tn(dproj, sv["u"], WIDE_TILE, 1024, t + "dw_in")
        dh, gs["mix_pre"][l] = _norm_bwd(du, sv["h"], n_mix_pre[l][None], dh, F32, t + "norm_mix_pre")
        tok_b = grads_fn(l, "mix", gw, dh)
    return loss_part[0, 0], dh, gs


def _pack_small(mix_pre, mix_post, ffn_pre, ffn_post, ret_norm, sink, dec_f, dec_b, loss, meta):
    d = D_MODEL

    def tile(a, rows=8):
        a = jnp.reshape(a, (-1, a.shape[-1])) if a.ndim else jnp.reshape(a, (1, 1))
        return jnp.pad(a, ((0, rows - a.shape[0]), (0, d - a.shape[1])))

    return jnp.concatenate([tile(mix_pre), tile(mix_post), tile(ffn_pre), tile(ffn_post), tile(ret_norm.reshape(-1, d)), tile(sink), tile(dec_f),
                            tile(dec_b), tile(loss), tile(meta, SMALL_ROWS - ROW_META)], axis=0)


def _unpack_small(p, depth):
    rows = lambda r0, cols: p[r0:r0 + depth, :cols]
    return dict(mix_pre=rows(ROW_MIX_PRE, D_MODEL), mix_post=rows(ROW_MIX_POST, D_MODEL), ffn_pre=rows(ROW_FFN_PRE, D_MODEL),
                ffn_post=rows(ROW_FFN_POST, D_MODEL), ret_norm=p[ROW_RET_NORM:ROW_RET_NORM + depth * RET_WIDTH // D_MODEL].reshape(depth, RET_WIDTH),
                sink=rows(ROW_SINK, ATT_HEADS), dec_f=rows(ROW_DEC_F, RET_HEADS), dec_b=rows(ROW_DEC_B, RET_HEADS), loss=p[ROW_LOSS, 0])


def kernel(x, meta_tokens, w_in, w_out, attn_sink, ret_decay_fwd, ret_decay_bwd, ret_norm, norm_mix_pre, norm_mix_post, w_gate, w_up, w_down, norm_ffn_pre, norm_ffn_post, loss_target, m_meta_tokens, m_w_in, m_w_out, m_attn_sink, m_ret_decay_fwd, m_ret_decay_bwd, m_ret_norm, m_norm_mix_pre, m_norm_mix_post, m_w_gate, m_w_up, m_w_down, m_norm_ffn_pre, m_norm_ffn_post, v_meta_tokens, v_w_in, v_w_out, v_attn_sink, v_ret_decay_fwd, v_ret_decay_bwd, v_ret_norm, v_norm_mix_pre, v_norm_mix_post, v_w_gate, v_w_up, v_w_down, v_norm_ffn_pre, v_norm_ffn_post):
    depth, d = w_in.shape[0], D_MODEL
    me = 4 * lax.axis_index("x") + 2 * lax.axis_index("y") + lax.axis_index("c")
    zero = jnp.zeros((), F32)

    meta_g, = _allgather([meta_tokens], "gather_meta")
    meta = meta_g.transpose(1, 0, 2).reshape(N_META, d)

    def shards(k):
        l = k // 2
        if k % 2 == 0:
            return [w_in[l].T.astype(BF16), w_out[l].astype(BF16)]
        return [w_gate[l].T.astype(BF16), w_up[l].T.astype(BF16), w_down[l].astype(BF16)]

    gathers, ahead = {}, 2
    for k in range(min(ahead + 1, 2 * depth)):
        gathers[k] = _send_start("gather", shards(k), None, gathers[k - 1][3] if k else meta_g, f"gather_start_g{k}")

    def take(k, h):
        if k >= 1 and k + ahead < 2 * depth:
            gathers[k + ahead] = _send_start("gather", shards(k + ahead), None, h, f"gather_start_g{k + ahead}")
        latest = max(gathers)
        lands = _send_wait("gather", gathers.pop(k), gathers[latest][3] if latest > k else h, f"gather_wait_g{k}")
        passing = _send_start("forward", [], lands, h, f"forward_start_g{k}")
        return _send_wait("forward", passing, passing[3], f"forward_wait_g{k}")

    def mix_weights_fn(l, h):
        wi_t, wo = take(2 * l, h)
        return wi_t.reshape(IN_COLS, d), wo.reshape(d, d), zero

    def ffn_weights_fn(l, h):
        wg_t, wu_t, wd = take(2 * l + 1, h)
        return wg_t.reshape(D_FF, d), wu_t.reshape(D_FF, d), wd.reshape(D_FF, d), zero

    exchanges, adam, order = {}, {}, []
    tr = lambda *ts: tuple(jnp.swapaxes(t, 1, 2) for t in ts)
    big = dict(wi=tr(w_in, m_w_in, v_w_in), wg=tr(w_gate, m_w_gate, v_w_gate), wu=tr(w_up, m_w_up, v_w_up), wd=(w_down, m_w_down, v_w_down),
               wo=(w_out, m_w_out, v_w_out))
    kinds = dict(ffn=("wg", "wu", "wd"), mix=("wi", "wo"))

    def finish(key, after):
        l, part = key
        arrived = _send_wait("scatter", exchanges.pop(key), after, f"exchange_wait_{part}_l{l}")
        for kind, parts in zip(kinds[part], arrived):
            adam[kind] = _adamw(parts, *big[kind], f"adamw_{kind}_l{l}", layer=l, prev=adam.get(kind))

    def grads_fn(l, part, gw, after):
        packed = [gw[kind].reshape(N_DEV, -1, d) for kind in kinds[part]]
        exchanges[(l, part)] = _send_start("scatter", packed, None, after, f"exchange_start_{part}_l{l}")
        order.append((l, part))
        token = exchanges[(l, part)][3]
        if len(order) > 2:
            finish(order[-3], token)
        return token[0, 0]

    loss_part, dh, gs = _local_step(x[0], meta, loss_target[0], mix_weights_fn, ffn_weights_fn, grads_fn, attn_sink, ret_decay_fwd, ret_decay_bwd, ret_norm,
                                    norm_mix_pre, norm_mix_post, norm_ffn_pre, norm_ffn_post)
    grad_x = dh[BLOCK:][None]

    st = lambda xs: jnp.stack([t.reshape(-1) if t.ndim == 1 else t[0] for t in xs])
    small = _pack_small(st(gs["mix_pre"]), st(gs["mix_post"]), st(gs["ffn_pre"]), st(gs["ffn_post"]), st(gs["ret_norm"]), st(gs["sink"]),
                        st(gs["dec_f"]), st(gs["dec_b"]), loss_part, dh[PAD_FRONT:BLOCK])
    small_g, = _allgather([small], "gather_small")
    for key in order[-2:]:
        finish(key, small_g)
    o_wi, o_wo, o_wg, o_wu, o_wd = tr(*adam["wi"]), adam["wo"], tr(*adam["wg"]), tr(*adam["wu"]), adam["wd"]
    zmeta = jnp.zeros((N_META, d), F32)
    packs = [_pack_small(a[0], a[1], a[2], a[3], a[4], a[5], a[6], a[7], zero, zmeta) for a in (
        (norm_mix_pre, norm_mix_post, norm_ffn_pre, norm_ffn_post, ret_norm, attn_sink, ret_decay_fwd, ret_decay_bwd),
        (m_norm_mix_pre, m_norm_mix_post, m_norm_ffn_pre, m_norm_ffn_post, m_ret_norm, m_attn_sink, m_ret_decay_fwd, m_ret_decay_bwd),
        (v_norm_mix_pre, v_norm_mix_post, v_norm_ffn_pre, v_norm_ffn_post, v_ret_norm, v_attn_sink, v_ret_decay_fwd, v_ret_decay_bwd))]
    o_small = [_unpack_small(o, depth) for o in _adamw(small_g, packs[0], packs[1], packs[2], "adamw_small")]
    meta_parts = lax.dynamic_slice(small_g, (0, ROW_META, me * (d // N_DEV)), (N_DEV, N_META, d // N_DEV))
    o_meta = _adamw(meta_parts, meta_tokens, m_meta_tokens, v_meta_tokens, "adamw_meta")

    outs = []
    for i in range(4):
        s = o_small[i]
        outs += [o_meta[i], o_wi[i], o_wo[i], s["sink"], s["dec_f"], s["dec_b"], s["ret_norm"], s["mix_pre"], s["mix_post"], o_wg[i], o_wu[i],
                 o_wd[i], s["ffn_pre"], s["ffn_post"]]
    return (o_small[0]["loss"], grad_x, *outs)
```

```python
import jax
import jax.numpy as jnp
import numpy as np
from jax import lax
from jax.experimental import pallas as pl
from jax.experimental.pallas import tpu as pltpu

F32, BF16 = jnp.float32, jnp.bfloat16

D_MODEL = 2048
N_META = 16
BLOCK = 128
WINDOW = 128
PAD_FRONT = BLOCK - N_META
ATT_HEAD_DIM = 128
ATT_WIDTH = D_MODEL // 2
ATT_HEADS = ATT_WIDTH // ATT_HEAD_DIM
ATT_KV_HEADS = 2
ATT_GROUP = ATT_HEADS // ATT_KV_HEADS
KV_WIDTH = ATT_KV_HEADS * ATT_HEAD_DIM
ROT_DIM = ATT_HEAD_DIM // 4
ROPE_THETA = 500000.0
RET_WIDTH = D_MODEL - ATT_WIDTH
RET_HEAD_DIM = 256
RET_HEADS = RET_WIDTH // RET_HEAD_DIM
RET_THETA = 10000.0
D_FF = 5632
IN_COLS = ATT_WIDTH + 2 * KV_WIDTH + 4 * RET_WIDTH
N_DEV = 8
SHARD_COLS = IN_COLS // N_DEV
EPS = 1e-6
NEG = -1e30
RET_K_SCALE = RET_HEAD_DIM ** -0.5
ATT_SCALE = ATT_HEAD_DIM ** -0.5

COL_AV256 = (ATT_WIDTH + KV_WIDTH) // 256
COL_RQ = (ATT_WIDTH + 2 * KV_WIDTH) // RET_HEAD_DIM
COL_RK = COL_RQ + RET_HEADS
COL_RV = COL_RK + RET_HEADS
COL_RG = COL_RV + RET_HEADS

ADAM_LR, ADAM_B1, ADAM_B2, ADAM_EPS, ADAM_WD, ADAM_STEP = 0.001, 0.9, 0.999, 1e-08, 0.01, 10

ROW_MIX_PRE, ROW_MIX_POST, ROW_FFN_PRE, ROW_FFN_POST, ROW_RET_NORM, ROW_SINK, ROW_DEC_F, ROW_DEC_B, ROW_LOSS, ROW_META, SMALL_ROWS = (
    0, 8, 16, 24, 32, 40, 48, 56, 64, 72, 96)
ADAMW_TILE_ELEMS = 128 * 1024

MESH = pl.DeviceIdType.MESH
ANY = pl.BlockSpec(memory_space=pl.ANY)


def _row_tile(n, cap):
    for t in range(cap - cap % 16, 0, -16):
        if n % t == 0:
            return t
    raise ValueError(n)


def _sds(shape, dtype):
    return jax.ShapeDtypeStruct(shape, dtype)


def _silu(x):
    return x * jax.nn.sigmoid(x)


def _dsilu(x):
    s = jax.nn.sigmoid(x)
    return s * (1.0 + x * (1.0 - s))


def _norm_fwd(x, g, res, out_dtype, name):
    n, d = x.shape
    tr = _row_tile(n, 384)

    def body(*refs):
        if res is None:
            x_ref, g_ref, o_ref = refs
        else:
            x_ref, g_ref, r_ref, o_ref = refs
        xv = x_ref[...]
        r = lax.rsqrt(jnp.mean(xv * xv, axis=-1, keepdims=True) + EPS)
        y = xv * r * g_ref[...]
        if res is not None:
            y = y + r_ref[...]
        o_ref[...] = y.astype(o_ref.dtype)

    row = pl.BlockSpec((tr, d), lambda i: (i, 0))
    ins = [row, pl.BlockSpec((1, d), lambda i: (0, 0))] + ([row] if res is not None else [])
    args = (x, g) + ((res,) if res is not None else ())
    return pl.pallas_call(body, grid=(n // tr,), in_specs=ins, out_specs=row, out_shape=_sds((n, d), out_dtype), name=name)(*args)


def _norm_bwd(dy, x, g, res, out_dtype, name):
    n, d = x.shape
    tr = _row_tile(n, 384)

    def body(*refs):
        if res is None:
            dy_ref, x_ref, g_ref, dx_ref, dg_ref = refs
        else:
            dy_ref, x_ref, g_ref, r_ref, dx_ref, dg_ref = refs
        i = pl.program_id(0)
        xv = x_ref[...]
        r = lax.rsqrt(jnp.mean(xv * xv, axis=-1, keepdims=True) + EPS)
        xhat = xv * r
        dyf = dy_ref[...].astype(F32)
        gdy = dyf * g_ref[...]
        dx = r * (gdy - xhat * jnp.mean(gdy * xhat, axis=-1, keepdims=True))
        if res is not None:
            dx = dx + r_ref[...]
        dx_ref[...] = dx.astype(dx_ref.dtype)

        @pl.when(i == 0)
        def _():
            dg_ref[...] = jnp.zeros_like(dg_ref)

        dg_ref[...] += jnp.sum(dyf * xhat, axis=0, keepdims=True)

    row = pl.BlockSpec((tr, d), lambda i: (i, 0))
    vec = pl.BlockSpec((1, d), lambda i: (0, 0))
    ins = [row, row, vec] + ([row] if res is not None else [])
    args = (dy, x, g) + ((res,) if res is not None else ())
    return pl.pallas_call(body, grid=(n // tr,), in_specs=ins, out_specs=(row, vec),
                          out_shape=(_sds((n, d), out_dtype), _sds((1, d), F32)), name=name,
                          compiler_params=pltpu.CompilerParams(dimension_semantics=("arbitrary",)))(*args)


def _mm(a, b, *, ta, tb, grid, a_blk, a_map, b_blk, b_map, o_blk, o_map, o_shape, o_dtype, name, acc=None):
    nk = grid[2]
    dims = (((0,) if ta else (1,), (1,) if tb else (0,)), ((), ()))

    def body(*refs):
        if acc is None:
            a_ref, b_ref, o_ref = refs[:3]
            c_ref = None
        else:
            a_ref, b_ref, c_ref, o_ref = refs[:4]
        part = lax.dot_general(a_ref[...], b_ref[...], dims, preferred_element_type=F32)
        if nk == 1:
            if c_ref is not None:
                part = part + c_ref[...].astype(F32)
            o_ref[...] = part.astype(o_ref.dtype)
            return
        acc_ref = refs[-1]
        k = pl.program_id(2)

        @pl.when(k == 0)
        def _():
            acc_ref[...] = jnp.zeros_like(acc_ref) if c_ref is None else c_ref[...].astype(F32)

        acc_ref[...] += part

        @pl.when(k == nk - 1)
        def _():
            o_ref[...] = acc_ref[...].astype(o_ref.dtype)

    ins = [pl.BlockSpec(a_blk, a_map), pl.BlockSpec(b_blk, b_map)]
    args = [a, b]
    if acc is not None:
        ins.append(pl.BlockSpec(o_blk, o_map))
        args.append(acc)
    return pl.pallas_call(body, grid=grid, in_specs=ins, out_specs=pl.BlockSpec(o_blk, o_map), out_shape=_sds(o_shape, o_dtype),
                          scratch_shapes=[pltpu.VMEM(o_blk, F32)] if nk > 1 else [], name=name,
                          compiler_params=pltpu.CompilerParams(dimension_semantics=("parallel", "parallel", "arbitrary")))(*args)


TOKEN_TILE = 1056
WIDE_TILE = 1408
DEEP_K_TOKEN_TILE = 528


def _mm_nn(x, w, o_dtype, tn, tk, name, acc=None, tm_cap=TOKEN_TILE):
    n, k = x.shape
    tm = _row_tile(n, tm_cap)
    return _mm(x, w, ta=False, tb=False, grid=(n // tm, w.shape[1] // tn, k // tk), a_blk=(tm, tk), a_map=lambda i, j, kk: (i, kk),
               b_blk=(tk, tn), b_map=lambda i, j, kk: (kk, j), o_blk=(tm, tn), o_map=lambda i, j, kk: (i, j),
               o_shape=(n, w.shape[1]), o_dtype=o_dtype, name=name, acc=acc)


def _mm_nt(dy, w, o_dtype, tn, tk, name, acc=None):
    n, k = dy.shape
    tm = _row_tile(n, TOKEN_TILE)
    return _mm(dy, w, ta=False, tb=True, grid=(n // tm, w.shape[0] // tn, k // tk), a_blk=(tm, tk), a_map=lambda i, j, kk: (i, kk),
               b_blk=(tn, tk), b_map=lambda i, j, kk: (j, kk), o_blk=(tm, tn), o_map=lambda i, j, kk: (i, j),
               o_shape=(n, w.shape[0]), o_dtype=o_dtype, name=name, acc=acc)


def _mm_tn(x, dy, tm, tn, name):
    n, m = x.shape
    tk = _row_tile(n, 2 * TOKEN_TILE)
    return _mm(x, dy, ta=True, tb=False, grid=(m // tm, dy.shape[1] // tn, n // tk), a_blk=(tk, tm), a_map=lambda i, j, kk: (kk, i),
               b_blk=(tk, tn), b_map=lambda i, j, kk: (kk, j), o_blk=(tm, tn), o_map=lambda i, j, kk: (i, j),
               o_shape=(m, dy.shape[1]), o_dtype=BF16, name=name)


def _rope_tables(n):
    pos = (jnp.arange(n) - PAD_FRONT).astype(F32)
    half = ROT_DIM // 2
    ang = pos[:, None] * (ROPE_THETA ** (-jnp.arange(half, dtype=F32) / half))[None, :]
    c, s = jnp.cos(ang), jnp.sin(ang)
    rest = ATT_HEAD_DIM - ROT_DIM
    cos_a = jnp.concatenate([c, c, jnp.ones((n, rest), F32)], axis=1)
    sin_a = jnp.concatenate([-s, s, jnp.zeros((n, rest), F32)], axis=1)
    half = RET_HEAD_DIM // 2
    ang = pos[:, None] * (RET_THETA ** (-jnp.arange(half, dtype=F32) / half))[None, :]
    c, s = jnp.cos(ang), jnp.sin(ang)
    perm = np.zeros((ATT_HEAD_DIM, ATT_HEAD_DIM), np.float32)
    for i in range(ROT_DIM):
        perm[(i + ROT_DIM // 2) % ROT_DIM, i] = 1.0
    return cos_a, sin_a, jnp.concatenate([c, c], axis=1), jnp.concatenate([-s, s], axis=1), jnp.asarray(perm, BF16)


def _rope_att(x, col0, heads, cos, sin, perm, name):
    n = x.shape[0]
    tr = _row_tile(n, 1056)
    hd = ATT_HEAD_DIM

    def body(x_ref, c_ref, s_ref, p_ref, o_ref):
        for h in range(heads):
            cs = slice(h * hd, (h + 1) * hd)
            xb = x_ref[:, cs].astype(BF16)
            sw = jnp.dot(xb, p_ref[...], preferred_element_type=F32)
            o_ref[:, cs] = (xb.astype(F32) * c_ref[...] + sw * s_ref[...]).astype(o_ref.dtype)

    tab = pl.BlockSpec((tr, hd), lambda i: (i, 0))
    return pl.pallas_call(body, grid=(n // tr,),
                          in_specs=[pl.BlockSpec((tr, heads * hd), lambda i: (i, col0)), tab, tab, pl.BlockSpec((hd, hd), lambda i: (0, 0))],
                          out_specs=pl.BlockSpec((tr, heads * hd), lambda i: (i, 0)), out_shape=_sds((n, heads * hd), BF16), name=name)(x, cos, sin, perm)


def _rope_ret(x, col0, cos, sin, name):
    p, n, _ = x.shape
    tr = _row_tile(n, 1056)
    hd = RET_HEAD_DIM

    def body(*refs):
        x_ref, o_ref = refs[0], refs[-1]
        for h in range(2):
            cs = slice(h * hd, (h + 1) * hd)
            xv = x_ref[0, :, cs].astype(F32)
            for q in range(1, p):
                xv = xv + x_ref[q, :, cs].astype(F32)
            if cos is not None:
                sw = jnp.concatenate([xv[:, hd // 2:], xv[:, :hd // 2]], axis=1)
                xv = xv * refs[1][...] + sw * refs[2][...]
            o_ref[:, cs] = xv.astype(o_ref.dtype)

    tab = pl.BlockSpec((tr, hd), lambda i, j: (i, 0))
    ins = [pl.BlockSpec((p, tr, 2 * hd), lambda i, j: (0, i, col0 + j))] + ([tab, tab] if cos is not None else [])
    args = (x,) + ((cos, sin) if cos is not None else ())
    return pl.pallas_call(body, grid=(n // tr, RET_HEADS // 2), in_specs=ins, out_specs=pl.BlockSpec((tr, 2 * hd), lambda i, j: (i, j)),
                          out_shape=_sds((n, RET_WIDTH), BF16), name=name)(*args)


def _att_mask(nblk, n_tot):
    row = lax.broadcasted_iota(jnp.int32, (BLOCK, 4 * BLOCK), 0)
    col = lax.broadcasted_iota(jnp.int32, (BLOCK, 4 * BLOCK), 1)
    qi = nblk * BLOCK + row
    seg = col // BLOCK
    cj = col % BLOCK
    kj = (nblk - 1 + seg) * BLOCK + cj
    band = (jnp.abs(qi - kj) <= WINDOW) & (kj >= PAD_FRONT) & (kj < n_tot) & (seg < 3)
    meta = (seg == 3) & (cj >= PAD_FRONT) & (jnp.abs(qi - cj) > WINDOW)
    return band | meta


def _att_specs(nb, v_col):
    kv = lambda f, cb: pl.BlockSpec((BLOCK, KV_WIDTH), lambda n: (f(n), cb))
    prev, own, nxt, first = (lambda n: jnp.maximum(n - 1, 0)), (lambda n: n), (lambda n: jnp.minimum(n + 1, nb - 1)), (lambda n: 0)
    return [kv(f, 0) for f in (prev, own, nxt, first)] + [kv(f, v_col) for f in (prev, own, nxt, first)]


def _att_probs(s, ok, snk):
    s = jnp.where(ok, s, NEG)
    m = jnp.maximum(jnp.max(s, axis=-1, keepdims=True), snk)
    p = jnp.exp(s - m)
    ps = jnp.exp(snk - m)
    inv = 1.0 / (jnp.sum(p, axis=-1, keepdims=True) + ps)
    return p * inv, ps * inv


def _att_fwd(q, k, proj, sink_b, name):
    n = q.shape[0]
    nb = n // BLOCK
    hd = ATT_HEAD_DIM

    def body(q_ref, kp, ko, kn, km, vp, vo, vn, vm, sink_ref, o_ref):
        nblk = pl.program_id(0)
        ok = _att_mask(nblk, n)
        keep = (nblk * BLOCK + lax.broadcasted_iota(jnp.int32, (BLOCK, 1), 0)) >= PAD_FRONT
        for kh in range(ATT_KV_HEADS):
            cs = slice(kh * hd, (kh + 1) * hd)
            kk = jnp.concatenate([r[:, cs] for r in (kp, ko, kn, km)], axis=0)
            vv = jnp.concatenate([r[:, cs] for r in (vp, vo, vn, vm)], axis=0)
            heads = [kh * ATT_GROUP + g for g in range(ATT_GROUP)]
            q4 = jnp.concatenate([q_ref[:, h * hd:(h + 1) * hd] for h in heads], axis=0)
            s = lax.dot_general(q4, kk, (((1,), (1,)), ((), ())), preferred_element_type=F32) * ATT_SCALE
            ps = []
            for g, h in enumerate(heads):
                p, _ = _att_probs(s[g * BLOCK:(g + 1) * BLOCK], ok, sink_ref[h:h + 1, 0:1])
                ps.append(p)
            o = jnp.dot(jnp.concatenate(ps, axis=0).astype(BF16), vv, preferred_element_type=F32)
            for g, h in enumerate(heads):
                o_ref[:, h * hd:(h + 1) * hd] = jnp.where(keep, o[g * BLOCK:(g + 1) * BLOCK], 0.0).astype(o_ref.dtype)

    qspec = pl.BlockSpec((BLOCK, ATT_WIDTH), lambda i: (i, 0))
    return pl.pallas_call(body, grid=(nb,), in_specs=[qspec] + _att_specs(nb, COL_AV256) + [pl.BlockSpec((ATT_HEADS, 128), lambda i: (0, 0))],
                          out_specs=qspec, out_shape=_sds((n, ATT_WIDTH), BF16), name=name)(q, k, k, k, k, proj, proj, proj, proj, sink_b)


def _att_bwd(q, k, proj, sink_b, dmixed, name):
    n = q.shape[0]
    nb = n // BLOCK
    hd = ATT_HEAD_DIM

    def body(q_ref, kp, ko, kn, km, vp, vo, vn, vm, sink_ref, do_ref, dq_ref, dk_ref, dv_ref, dsink_ref):
        nblk = pl.program_id(0)

        @pl.when(nblk == 0)
        def _():
            dk_ref[...] = jnp.zeros_like(dk_ref)
            dv_ref[...] = jnp.zeros_like(dv_ref)
            dsink_ref[...] = jnp.zeros_like(dsink_ref)

        ok = _att_mask(nblk, n)
        rows = [jnp.maximum(nblk - 1, 0), nblk, jnp.minimum(nblk + 1, nb - 1), 0]
        for kh in range(ATT_KV_HEADS):
            cs = slice(kh * hd, (kh + 1) * hd)
            kk = jnp.concatenate([r[:, cs] for r in (kp, ko, kn, km)], axis=0)
            vv = jnp.concatenate([r[:, cs] for r in (vp, vo, vn, vm)], axis=0)
            heads = [kh * ATT_GROUP + g for g in range(ATT_GROUP)]
            q4 = jnp.concatenate([q_ref[:, h * hd:(h + 1) * hd] for h in heads], axis=0)
            do4 = jnp.concatenate([do_ref[:, h * hd:(h + 1) * hd] for h in heads], axis=0)
            s = lax.dot_general(q4, kk, (((1,), (1,)), ((), ())), preferred_element_type=F32) * ATT_SCALE
            dp = lax.dot_general(do4, vv, (((1,), (1,)), ((), ())), preferred_element_type=F32)
            ps, dss = [], []
            for g, h in enumerate(heads):
                p, psink = _att_probs(s[g * BLOCK:(g + 1) * BLOCK], ok, sink_ref[h:h + 1, 0:1])
                dpg = dp[g * BLOCK:(g + 1) * BLOCK]
                delta = jnp.sum(p * dpg, axis=-1, keepdims=True)
                ps.append(p)
                dss.append(p * (dpg - delta) * ATT_SCALE)
                dsink_ref[h:h + 1, :] = dsink_ref[h:h + 1, :] - jnp.sum(psink * delta, axis=0, keepdims=True)
            ds = jnp.concatenate(dss, axis=0).astype(BF16)
            pb = jnp.concatenate(ps, axis=0).astype(BF16)
            dq = jnp.dot(ds, kk, preferred_element_type=F32)
            for g, h in enumerate(heads):
                dq_ref[:, h * hd:(h + 1) * hd] = dq[g * BLOCK:(g + 1) * BLOCK].astype(dq_ref.dtype)
            dk = lax.dot_general(ds, q4, (((0,), (0,)), ((), ())), preferred_element_type=F32)
            dv = lax.dot_general(pb, do4, (((0,), (0,)), ((), ())), preferred_element_type=F32)
            for seg, r in enumerate(rows):
                at = (pl.ds(pl.multiple_of(r * BLOCK, BLOCK), BLOCK), cs)
                dk_ref[at] += dk[seg * BLOCK:(seg + 1) * BLOCK]
                dv_ref[at] += dv[seg * BLOCK:(seg + 1) * BLOCK]

    qspec = pl.BlockSpec((BLOCK, ATT_WIDTH), lambda i: (i, 0))
    whole = pl.BlockSpec((n, KV_WIDTH), lambda i: (0, 0))
    sinks = pl.BlockSpec((ATT_HEADS, 128), lambda i: (0, 0))
    return pl.pallas_call(body, grid=(nb,), in_specs=[qspec] + _att_specs(nb, COL_AV256) + [sinks, qspec], out_specs=(qspec, whole, whole, sinks),
                          out_shape=(_sds((n, ATT_WIDTH), BF16), _sds((n, KV_WIDTH), F32), _sds((n, KV_WIDTH), F32), _sds((ATT_HEADS, 128), F32)),
                          name=name, compiler_params=pltpu.CompilerParams(dimension_semantics=("arbitrary",)))(
                              q, k, k, k, k, proj, proj, proj, proj, sink_b, dmixed)


def _ret_decay(lg, d):
    a = lax.broadcasted_iota(jnp.int32, (BLOCK, 1), 0)
    b = lax.broadcasted_iota(jnp.int32, (1, BLOCK), 1)
    t_col = a + d * (BLOCK - 1 - 2 * a)
    t_row = b + d * (BLOCK - 1 - 2 * b)
    diff = t_col - t_row
    dist = jnp.maximum(diff, 0).astype(F32)
    dmask = jnp.where(diff >= d, jnp.exp(lg * dist), 0.0)
    tf = t_col.astype(F32)
    xi = jnp.exp(lg * (tf + 1.0))
    zeta = jnp.exp(lg * (BLOCK - 1.0 - tf))
    gam = jnp.exp(jnp.full((1, 1), BLOCK, F32) * lg)
    return dmask, dist, xi, zeta, gam, tf


def _ret_fwd(q, k, proj, lg, name):
    n = q.shape[0]
    nc = n // BLOCK
    hd = RET_HEAD_DIM
    chunk = lambda d, c: c + d * (nc - 1 - 2 * c)

    def body(lg_ref, q_ref, k_ref, v0, v1, v2, v3, o_ref, st_ref, s_ref):
        d, c = pl.program_id(0), pl.program_id(1)

        @pl.when(c == 0)
        def _():
            s_ref[...] = jnp.zeros_like(s_ref)

        for h, v_ref in enumerate((v0, v1, v2, v3)):
            cs = slice(h * hd, (h + 1) * hd)
            dmask, _, xi, zeta, gam, _ = _ret_decay(lg_ref[h, d], d)
            qv = q_ref[:, cs]
            kf = k_ref[:, cs].astype(F32) * RET_K_SCALE
            vv = v_ref[...]
            s = lax.dot_general(qv, kf.astype(BF16), (((1,), (1,)), ((), ())), preferred_element_type=F32)
            sb = s_ref[h]
            o_ref[:, cs] = (jnp.dot((s * dmask).astype(BF16), vv, preferred_element_type=F32)
                            + jnp.dot((qv.astype(F32) * xi).astype(BF16), sb.astype(BF16), preferred_element_type=F32))
            st_ref[h] = sb
            s_ref[h] = gam * sb + lax.dot_general((kf * zeta).astype(BF16), vv, (((0,), (0,)), ((), ())), preferred_element_type=F32)

    wide = pl.BlockSpec((BLOCK, RET_WIDTH), lambda d, c: (chunk(d, c), 0))
    vblk = lambda h: pl.BlockSpec((BLOCK, hd), lambda d, c: (chunk(d, c), COL_RV + h))
    return pl.pallas_call(
        body, grid=(2, nc), in_specs=[pl.BlockSpec(memory_space=pltpu.SMEM), wide, wide] + [vblk(h) for h in range(RET_HEADS)],
        out_specs=(pl.BlockSpec((None, BLOCK, RET_WIDTH), lambda d, c: (d, chunk(d, c), 0)),
                   pl.BlockSpec((RET_HEADS, None, None, hd, hd), lambda d, c: (0, d, c, 0, 0))),
        out_shape=(_sds((2, n, RET_WIDTH), F32), _sds((RET_HEADS, 2, nc, hd, hd), F32)), scratch_shapes=[pltpu.VMEM((RET_HEADS, hd, hd), F32)],
        name=name, compiler_params=pltpu.CompilerParams(dimension_semantics=("parallel", "arbitrary")))(lg, q, k, proj, proj, proj, proj)


def _ret_bwd(q, k, proj, lg, do, states, name):
    n = q.shape[0]
    nc = n // BLOCK
    hd = RET_HEAD_DIM
    chunk = lambda d, r: (nc - 1 - r) + d * (2 * r - (nc - 1))

    def body(lg_ref, q_ref, k_ref, v0, v1, v2, v3, do_ref, st_ref, dq_ref, dk_ref, dv_ref, dlg_ref, ds_ref):
        d, r = pl.program_id(0), pl.program_id(1)

        @pl.when(r == 0)
        def _():
            ds_ref[...] = jnp.zeros_like(ds_ref)
            dlg_ref[...] = jnp.zeros_like(dlg_ref)

        row = lax.broadcasted_iota(jnp.int32, (BLOCK, 1), 0) + chunk(d, r) * BLOCK
        keep = row >= PAD_FRONT
        nt = (((1,), (1,)), ((), ()))
        tn = (((0,), (0,)), ((), ()))
        for h, v_ref in enumerate((v0, v1, v2, v3)):
            cs = slice(h * hd, (h + 1) * hd)
            dmask, dist, xi, zeta, gam, tf = _ret_decay(lg_ref[h, d], d)
            qv, vv, dov = q_ref[:, cs], v_ref[...], do_ref[:, cs]
            qf = qv.astype(F32)
            kf = k_ref[:, cs].astype(F32) * RET_K_SCALE
            kb = kf.astype(BF16)
            sc = st_ref[h]
            dsn = ds_ref[h]
            s = lax.dot_general(qv, kb, nt, preferred_element_type=F32)
            dsc = lax.dot_general(dov, vv, nt, preferred_element_type=F32) * dmask
            dsb = dsc.astype(BF16)
            dq_c = xi * lax.dot_general(dov, sc.astype(BF16), nt, preferred_element_type=F32)
            dk_c = zeta * lax.dot_general(vv, dsn.astype(BF16), nt, preferred_element_type=F32)
            dq = jnp.dot(dsb, kb, preferred_element_type=F32) + dq_c
            dk = lax.dot_general(dsb, qv, tn, preferred_element_type=F32) + dk_c
            dv = (lax.dot_general((s * dmask).astype(BF16), dov, tn, preferred_element_type=F32)
                  + jnp.dot((kf * zeta).astype(BF16), dsn.astype(BF16), preferred_element_type=F32))
            ds_ref[h] = gam * dsn + lax.dot_general((qf * xi).astype(BF16), dov, tn, preferred_element_type=F32)
            dlg = (jnp.sum(dsc * s * dist, keepdims=True)
                   + jnp.sum((tf + 1.0) * jnp.sum(qf * dq_c, axis=-1, keepdims=True), keepdims=True)
                   + jnp.sum((BLOCK - 1.0 - tf) * jnp.sum(kf * dk_c, axis=-1, keepdims=True), keepdims=True)
                   + BLOCK * gam * jnp.sum(dsn * sc, keepdims=True))
            dlg_ref[h] += dlg
            dq_ref[:, cs] = dq
            dk_ref[:, cs] = jnp.where(keep, dk * RET_K_SCALE, 0.0)
            dv_ref[:, cs] = jnp.where(keep, dv, 0.0)

    wide = pl.BlockSpec((BLOCK, RET_WIDTH), lambda d, r: (chunk(d, r), 0))
    vblk = lambda h: pl.BlockSpec((BLOCK, hd), lambda d, r: (chunk(d, r), COL_RV + h))
    plane = pl.BlockSpec((None, BLOCK, RET_WIDTH), lambda d, r: (d, chunk(d, r), 0))
    return pl.pallas_call(
        body, grid=(2, nc),
        in_specs=[pl.BlockSpec(memory_space=pltpu.SMEM), wide, wide] + [vblk(h) for h in range(RET_HEADS)]
        + [wide, pl.BlockSpec((RET_HEADS, None, None, hd, hd), lambda d, r: (0, d, nc - 1 - r, 0, 0))],
        out_specs=(plane, plane, plane, pl.BlockSpec((RET_HEADS, None, 8, 128), lambda d, r: (0, d, 0, 0))),
        out_shape=(_sds((2, n, RET_WIDTH), F32),) * 3 + (_sds((RET_HEADS, 2, 8, 128), F32),),
        scratch_shapes=[pltpu.VMEM((RET_HEADS, hd, hd), F32)], name=name,
        compiler_params=pltpu.CompilerParams(dimension_semantics=("parallel", "arbitrary")))(lg, q, k, proj, proj, proj, proj, do, states)


def _retgate_fwd(o, proj, gain, name):
    _, n, _ = o.shape
    tr = _row_tile(n, 1056)
    hd = RET_HEAD_DIM

    def body(o_ref, rg_ref, g_ref, y_ref):
        ov = o_ref[0] + o_ref[1]
        r = lax.rsqrt(jnp.mean(ov * ov, axis=-1, keepdims=True) + EPS)
        y_ref[...] = (_silu(rg_ref[...].astype(F32)) * (ov * r * g_ref[...])).astype(y_ref.dtype)

    return pl.pallas_call(body, grid=(n // tr, RET_HEADS),
                          in_specs=[pl.BlockSpec((2, tr, hd), lambda i, h: (0, i, h)), pl.BlockSpec((tr, hd), lambda i, h: (i, COL_RG + h)),
                                    pl.BlockSpec((1, hd), lambda i, h: (0, h))],
                          out_specs=pl.BlockSpec((tr, hd), lambda i, h: (i, h)), out_shape=_sds((n, RET_WIDTH), BF16), name=name)(o, proj, gain)


def _retgate_bwd(dmixed, o, proj, gain, name):
    _, n, _ = o.shape
    tr = _row_tile(n, 1056)
    hd = RET_HEAD_DIM

    def body(dy_ref, o_ref, rg_ref, g_ref, do_ref, drg_ref, dg_ref):
        i = pl.program_id(1)
        ov = o_ref[0] + o_ref[1]
        r = lax.rsqrt(jnp.mean(ov * ov, axis=-1, keepdims=True) + EPS)
        xhat = ov * r
        rg = rg_ref[...].astype(F32)
        dy = dy_ref[...].astype(F32)
        drg_ref[...] = (dy * (xhat * g_ref[...]) * _dsilu(rg)).astype(drg_ref.dtype)
        dn = dy * _silu(rg)
        dxh = dn * g_ref[...]
        do_ref[...] = (r * (dxh - xhat * jnp.mean(dxh * xhat, axis=-1, keepdims=True))).astype(do_ref.dtype)

        @pl.when(i == 0)
        def _():
            dg_ref[...] = jnp.zeros_like(dg_ref)

        dg_ref[...] += jnp.sum(dn * xhat, axis=0, keepdims=True)

    tile = pl.BlockSpec((tr, hd), lambda h, i: (i, h))
    vec = pl.BlockSpec((1, hd), lambda h, i: (0, h))
    return pl.pallas_call(body, grid=(RET_HEADS, n // tr),
                          in_specs=[pl.BlockSpec((tr, hd), lambda h, i: (i, RET_HEADS + h)), pl.BlockSpec((2, tr, hd), lambda h, i: (0, i, h)),
                                    pl.BlockSpec((tr, hd), lambda h, i: (i, COL_RG + h)), vec],
                          out_specs=(tile, tile, vec), out_shape=(_sds((n, RET_WIDTH), BF16), _sds((n, RET_WIDTH), BF16), _sds((1, RET_WIDTH), F32)),
                          name=name, compiler_params=pltpu.CompilerParams(dimension_semantics=("parallel", "arbitrary")))(dmixed, o, proj, gain)


def _swiglu_fwd(gate, up, name):
    n, f = gate.shape
    tr, tc = _row_tile(n, 384), f // 2

    def body(g_ref, u_ref, o_ref):
        o_ref[...] = (_silu(g_ref[...].astype(F32)) * u_ref[...].astype(F32)).astype(o_ref.dtype)

    t = pl.BlockSpec((tr, tc), lambda i, j: (i, j))
    return pl.pallas_call(body, grid=(n // tr, 2), in_specs=[t, t], out_specs=t, out_shape=_sds((n, f), BF16), name=name)(gate, up)


def _swiglu_bwd(df, gate, up, name):
    n, f = gate.shape
    tr, tc = _row_tile(n, 384), f // 2

    def body(d_ref, g_ref, u_ref, dg_ref, du_ref):
        d, g = d_ref[...].astype(F32), g_ref[...].astype(F32)
        dg_ref[...] = (d * u_ref[...].astype(F32) * _dsilu(g)).astype(dg_ref.dtype)
        du_ref[...] = (d * _silu(g)).astype(du_ref.dtype)

    t = pl.BlockSpec((tr, tc), lambda i, j: (i, j))
    return pl.pallas_call(body, grid=(n // tr, 2), in_specs=[t, t, t], out_specs=(t, t), out_shape=(_sds((n, f), BF16),) * 2, name=name)(df, gate, up)


def _loss_head(h, target, name):
    n, d = h.shape
    nb = n // BLOCK

    def body(h_ref, t_ref, dh_ref, l_ref):
        i = pl.program_id(0)

        @pl.when(i == 0)
        def _():
            l_ref[...] = jnp.zeros_like(l_ref)
            dh_ref[...] = jnp.zeros_like(dh_ref)

        @pl.when(i > 0)
        def _():
            e = h_ref[...] - t_ref[...]
            dh_ref[...] = e * (1.0 / d)
            l_ref[...] += 0.5 * jnp.sum(jnp.mean(e * e, axis=-1, keepdims=True), keepdims=True)

    blk = pl.BlockSpec((BLOCK, d), lambda i: (i, 0))
    return pl.pallas_call(body, grid=(nb,), in_specs=[blk, pl.BlockSpec((BLOCK, d), lambda i: (jnp.maximum(i - 1, 0), 0))],
                          out_specs=(blk, pl.BlockSpec((8, 128), lambda i: (0, 0))), out_shape=(_sds((n, d), F32), _sds((8, 128), F32)), name=name,
                          compiler_params=pltpu.CompilerParams(dimension_semantics=("arbitrary",)))(h, target)


def _adamw(parts, w, m, v, name, sel=None, layer=None, prev=None):
    s, (r, c) = parts.shape[0], parts.shape[-2:]
    tr = _row_tile(r, max(16, (ADAMW_TILE_ELEMS // c) // 16 * 16))
    b1c, b2c = 1.0 - ADAM_B1 ** ADAM_STEP, 1.0 - ADAM_B2 ** ADAM_STEP

    def body(p_ref, w_ref, m_ref, v_ref, *rest):
        g_ref, d_ref, mo_ref, vo_ref = rest[-4:]
        g = p_ref[0].astype(F32)
        for q in range(1, s):
            g = g + p_ref[q].astype(F32)
        mn = ADAM_B1 * m_ref[...] + (1.0 - ADAM_B1) * g
        vn = ADAM_B2 * v_ref[...] + (1.0 - ADAM_B2) * jnp.square(g)
        g_ref[...] = g
        mo_ref[...] = mn
        vo_ref[...] = vn
        d_ref[...] = -ADAM_LR * ((mn / b1c) / (jnp.sqrt(vn / b2c) + ADAM_EPS) + ADAM_WD * w_ref[...])

    pspec = (pl.BlockSpec((s, tr, c), lambda i: (0, i, 0)) if sel is None else pl.BlockSpec((s, None, tr, c), lambda i: (0, sel, i, 0)))
    if layer is None:
        t = pl.BlockSpec((tr, c), lambda i: (i, 0))
        return pl.pallas_call(body, grid=(r // tr,), in_specs=[pspec, t, t, t], out_specs=(t, t, t, t), out_shape=(_sds((r, c), F32),) * 4,
                              name=name)(parts, w, m, v)
    t = pl.BlockSpec((None, tr, c), lambda i: (layer, i, 0))
    prev = prev if prev is not None else tuple(lax.empty(w.shape, F32) for _ in range(4))
    return pl.pallas_call(body, grid=(r // tr,), in_specs=[pspec, t, t, t] + [ANY] * 4, out_specs=(t, t, t, t), out_shape=(_sds(w.shape, F32),) * 4,
                          input_output_aliases={4 + i: i for i in range(4)}, name=name)(parts, w, m, v, *prev)


def _allgather(xs, name):
    na = len(xs)

    def body(*refs):
        x_refs, o_refs = refs[:na], refs[na:2 * na]
        send, recv, lsem = refs[2 * na:]
        x, y, c = lax.axis_index("x"), lax.axis_index("y"), lax.axis_index("c")
        me, sib = (x, y, c), (x, y, 1 - c)
        chips = [(1 - x, y), (x, 1 - y), (1 - x, 1 - y)]
        slot = lambda p: 4 * p[0] + 2 * p[1] + p[2]

        def copy(a, k, block, to, src=None):
            dst = o_refs[a].at[slot(block)]
            return pltpu.make_async_remote_copy(src_ref=dst if src is None else src, dst_ref=dst, send_sem=send.at[a, k], recv_sem=recv.at[a, k],
                                                device_id=to, device_id_type=MESH)

        mine = [pltpu.make_async_copy(x_refs[a], o_refs[a].at[slot(me)], lsem.at[a]) for a in range(na)]
        for cp in mine:
            cp.start()
        first = []
        for a in range(na):
            first.append(copy(a, 0, me, sib, src=x_refs[a]))
            first += [copy(a, 1 + j, me, (*chip, c), src=x_refs[a]) for j, chip in enumerate(chips)]
        for cp in first:
            cp.start()
        passed = []
        for j, chip in enumerate(chips):
            for a in range(na):
                copy(a, 1 + j, (*chip, c), me).wait_recv()
                passed.append(copy(a, 4 + j, (*chip, c), sib))
                passed[-1].start()
        for a in range(na):
            copy(a, 0, sib, me).wait_recv()
            for j, chip in enumerate(chips):
                copy(a, 4 + j, (*chip, 1 - c), me).wait_recv()
        for cp in first + passed:
            cp.wait_send()
        for cp in mine:
            cp.wait()

    return pl.pallas_call(body, in_specs=[ANY] * na, out_specs=[ANY] * na, out_shape=[_sds((N_DEV,) + t.shape, t.dtype) for t in xs],
                          scratch_shapes=[pltpu.SemaphoreType.DMA((na, 7)), pltpu.SemaphoreType.DMA((na, 7)), pltpu.SemaphoreType.DMA((na,))],
                          name=name)(*xs)


HBM = pl.BlockSpec(memory_space=pltpu.HBM)
SEM = pl.BlockSpec(memory_space=pltpu.SEMAPHORE)
EFFECT = pltpu.SideEffectType.DATAFLOW_SIDE_EFFECTING


SPLIT_RELATIONS = dict(gather=(1, 2, 4, 6),
                       forward=(2, 4, 6),
                       scatter=tuple(range(1, N_DEV)))


def _split_copies(mode, x_refs, land_refs, send, recv, own, landing):
    x, y, c = lax.axis_index("x"), lax.axis_index("y"), lax.axis_index("c")
    flip = lambda r: ((1 - x if r & 4 else x), (1 - y if r & 2 else y), (1 - c if r & 1 else c))
    slot = lambda p: 4 * p[0] + 2 * p[1] + p[2]
    me = slot((x, y, c))
    rel = SPLIT_RELATIONS[mode]
    local, remote = [], []
    for a in range(len(land_refs)):
        if mode != "forward":
            local.append(pltpu.make_async_copy(x_refs[a].at[me] if mode == "scatter" else x_refs[a], land_refs[a].at[me], own.at[a]))
        for j, r in enumerate(rel):
            if mode == "forward":
                to = flip(1)
                src = dst = land_refs[a].at[slot(flip(r ^ 1) if landing else flip(r))]
            else:
                to = flip(r)
                src = x_refs[a].at[slot(to)] if mode == "scatter" else x_refs[a]
                dst = land_refs[a].at[slot(to) if landing else me]
            remote.append(pltpu.make_async_remote_copy(src_ref=src, dst_ref=dst, send_sem=send.at[len(rel) * a + j],
                                                       recv_sem=recv.at[len(rel) * a + j], device_id=to, device_id_type=MESH))
    return local, remote


def _send_start(mode, xs, lands, after, name):
    if lands is None:
        lands = [lax.empty(t.shape if mode == "scatter" else (N_DEV,) + t.shape, t.dtype) for t in xs]
    nx, na, nr = len(xs), len(lands), len(SPLIT_RELATIONS[mode])
    nsem = 2 if mode == "forward" else 3

    def body(*refs):
        x_refs, land_refs = refs[:nx], refs[nx:nx + na]
        sems = refs[nx + na + 1:nx + na + 1 + nsem]
        local, remote = _split_copies(mode, x_refs, land_refs, sems[0], sems[1], sems[2] if nsem == 3 else None, False)
        for cp in remote + local:
            cp.start()
        refs[-1][...] = jnp.zeros_like(refs[-1])

    hbm = lambda t: pltpu.with_memory_space_constraint(t, pltpu.HBM)
    sem_shapes = [pltpu.SemaphoreType.DMA((nr * na,)), pltpu.SemaphoreType.DMA((nr * na,)), pltpu.SemaphoreType.DMA((na,))][:nsem]
    outs = pl.pallas_call(
        body, name=name,
        out_shape=(*sem_shapes, *[pltpu.HBM(t.shape, t.dtype) for t in list(xs) + list(lands)], _sds((8, 128), F32)),
        in_specs=[HBM] * (nx + na) + [ANY], out_specs=(*[SEM] * nsem, *[HBM] * (nx + na), pl.BlockSpec(memory_space=pltpu.VMEM)),
        input_output_aliases={i: nsem + i for i in range(nx + na)},
        compiler_params=pltpu.CompilerParams(has_side_effects=EFFECT))(*[hbm(t) for t in list(xs) + list(lands)], after)
    return outs[:nsem], list(outs[nsem:nsem + nx]), list(outs[nsem + nx:nsem + nx + na]), outs[-1]


def _send_wait(mode, started, after, name):
    sems, xs, lands, _ = started
    nx, na, nsem = len(xs), len(lands), len(sems)

    def body(*refs):
        s = refs[nx + na:nx + na + nsem]
        local, remote = _split_copies(mode, refs[:nx], refs[nx:nx + na], s[0], s[1], s[2] if nsem == 3 else None, True)
        for cp in remote:
            cp.wait_send()
            cp.wait_recv()
        for cp in local:
            cp.wait()

    outs = pl.pallas_call(body, name=name, out_shape=tuple(pltpu.HBM(t.shape, t.dtype) for t in xs + lands),
                          in_specs=[HBM] * (nx + na) + [SEM] * nsem + [ANY], out_specs=[HBM] * (nx + na),
                          input_output_aliases={i: i for i in range(nx + na)},
                          compiler_params=pltpu.CompilerParams(has_side_effects=EFFECT))(*xs, *lands, *sems, after)
    return list(outs[nx:])


def _local_step(x, meta, target, mix_weights_fn, ffn_weights_fn, grads_fn, sink, dec_f, dec_b, ret_norm, n_mix_pre, n_mix_post, n_ffn_pre,
                n_ffn_post):
    depth = n_mix_pre.shape[0]
    d = D_MODEL
    h = jnp.concatenate([jnp.zeros((PAD_FRONT, d), F32), meta, x], axis=0)
    n = h.shape[0]
    cos_a, sin_a, cos_r, sin_r, perm = _rope_tables(n)
    lg_all = jnp.stack([-jnp.exp(dec_f), -jnp.exp(dec_b)], axis=-1)
    saved = []
    for l in range(depth):
        t = f"l{l}_"
        sink_b = jnp.broadcast_to(sink[l][:, None], (ATT_HEADS, 128))
        wi, wo, tok = mix_weights_fn(l, h)
        u = _norm_fwd(h, (n_mix_pre[l] + tok)[None], None, BF16, t + "norm_mix_pre")
        proj = _mm_nt(u, wi, BF16, WIDE_TILE, d, t + "proj")
        aq = _rope_att(proj, 0, ATT_HEADS, cos_a, sin_a, perm, t + "rope_aq")
        ak = _rope_att(proj, ATT_WIDTH // KV_WIDTH, ATT_KV_HEADS, cos_a, sin_a, perm, t + "rope_ak")
        att = _att_fwd(aq, ak, proj, sink_b, t + "att")
        proj3 = proj[None]
        rq = _rope_ret(proj3, COL_RQ // 2, cos_r, sin_r, t + "rope_rq")
        rk = _rope_ret(proj3, COL_RK // 2, cos_r, sin_r, t + "rope_rk")
        o_ret, states = _ret_fwd(rq, rk, proj, lg_all[l], t + "ret")
        retg = _retgate_fwd(o_ret, proj, ret_norm[l][None], t + "retgate")
        mixed = jnp.concatenate([att, retg], axis=1)
        mo = _mm_nn(mixed, wo, F32, 1024, d, t + "out_proj")
        h_mid = _norm_fwd(mo, n_mix_post[l][None], h, F32, t + "norm_mix_post")
        wg, wu, wd, tok = ffn_weights_fn(l, h_mid)
        u2 = _norm_fwd(h_mid, (n_ffn_pre[l] + tok)[None], None, BF16, t + "norm_ffn_pre")
        gate = _mm_nt(u2, wg, BF16, WIDE_TILE, d, t + "gate")
        up = _mm_nt(u2, wu, BF16, WIDE_TILE, d, t + "up")
        f = _swiglu_fwd(gate, up, t + "swiglu")
        dn = _mm_nn(f, wd, F32, 512, D_FF, t + "down", tm_cap=DEEP_K_TOKEN_TILE)
        h_out = _norm_fwd(dn, n_ffn_post[l][None], h_mid, F32, t + "norm_ffn_post")
        saved.append(dict(h=h, u=u, proj=proj, aq=aq, ak=ak, rq=rq, rk=rk, o_ret=o_ret, states=states, mixed=mixed, mo=mo, h_mid=h_mid, u2=u2,
                          gate=gate, up=up, f=f, dn=dn, sink_b=sink_b, wi=wi, wo=wo, wg=wg, wu=wu, wd=wd))
        h = h_out

    dh, loss_part = _loss_head(h, target, "loss_head")
    gs = dict(sink=[None] * depth, dec_f=[None] * depth, dec_b=[None] * depth, ret_norm=[None] * depth, mix_pre=[None] * depth,
              mix_post=[None] * depth, ffn_pre=[None] * depth, ffn_post=[None] * depth)
    tok_b = jnp.zeros((), F32)
    for l in reversed(range(depth)):
        t = f"l{l}_b_"
        sv = saved[l]
        proj = sv["proj"]
        d_dn, gs["ffn_post"][l] = _norm_bwd(dh, sv["dn"], (n_ffn_post[l] + tok_b)[None], None, BF16, t + "norm_ffn_post")
        gw = {}
        d_f = _mm_nt(d_dn, sv["wd"], BF16, WIDE_TILE, d, t + "d_f")
        gw["wd"] = _mm_tn(sv["f"], d_dn, WIDE_TILE, 1024, t + "dw_down")
        d_gate, d_up = _swiglu_bwd(d_f, sv["gate"], sv["up"], t + "swiglu")
        du2 = _mm_nn(d_gate, sv["wg"], F32, 512, D_FF, t + "du2_gate", tm_cap=DEEP_K_TOKEN_TILE)
        du2 = _mm_nn(d_up, sv["wu"], F32, 512, D_FF, t + "du2_up", acc=du2, tm_cap=DEEP_K_TOKEN_TILE)
        gw["wg"] = _mm_tn(d_gate, sv["u2"], WIDE_TILE, 1024, t + "dw_gate")
        gw["wu"] = _mm_tn(d_up, sv["u2"], WIDE_TILE, 1024, t + "dw_up")
        tok_b = grads_fn(l, "ffn", gw, du2)
        dh, gs["ffn_pre"][l] = _norm_bwd(du2, sv["h_mid"], (n_ffn_pre[l] + tok_b)[None], dh, F32, t + "norm_ffn_pre")
        d_mo, gs["mix_post"][l] = _norm_bwd(dh, sv["mo"], n_mix_post[l][None], None, BF16, t + "norm_mix_post")
        d_mixed = _mm_nt(d_mo, sv["wo"], BF16, 1024, d, t + "d_mixed")
        gw["wo"] = _mm_tn(sv["mixed"], d_mo, 1024, 1024, t + "dw_out")
        d_o, d_rg, gs["ret_norm"][l] = _retgate_bwd(d_mixed, sv["o_ret"], proj, ret_norm[l][None], t + "retgate")
        dq_r, dk_r, dv_r, dlg = _ret_bwd(sv["rq"], sv["rk"], proj, lg_all[l], d_o, sv["states"], t + "ret")
        draw = dlg[:, :, 0, 0] * lg_all[l]
        gs["dec_f"][l], gs["dec_b"][l] = draw[:, 0], draw[:, 1]
        dq_a, dk_a, dv_a, dsink = _att_bwd(sv["aq"], sv["ak"], proj, sv["sink_b"], d_mixed, t + "att")
        gs["sink"][l] = dsink[:, 0]
        dproj = jnp.concatenate([
            _rope_att(dq_a, 0, ATT_HEADS, cos_a, -sin_a, perm, t + "rope_aq"),
            _rope_att(dk_a, 0, ATT_KV_HEADS, cos_a, -sin_a, perm, t + "rope_ak"),
            dv_a.astype(BF16),
            _rope_ret(dq_r, 0, cos_r, -sin_r, t + "rope_rq"),
            _rope_ret(dk_r, 0, cos_r, -sin_r, t + "rope_rk"),
            _rope_ret(dv_r, 0, None, None, t + "sum_rv"),
            d_rg], axis=1)
        gw["wi"] = _mm_tn(dproj, sv["u"], WIDE_TILE, 1024, t + "dw_in")
        tok_b = grads_fn(l, "mix", gw, dproj)
        du = _mm_nn(dproj, sv["wi"], F32, 512, IN_COLS, t + "du", tm_cap=DEEP_K_TOKEN_TILE)
        dh, gs["mix_pre"][l] = _norm_bwd(du, sv["h"], (n_mix_pre[l] + tok_b)[None], dh, F32, t + "norm_mix_pre")
    return loss_part[0, 0], dh, gs


def _pack_small(mix_pre, mix_post, ffn_pre, ffn_post, ret_norm, sink, dec_f, dec_b, loss, meta):
    d = D_MODEL

    def tile(a, rows=8):
        a = jnp.reshape(a, (-1, a.shape[-1])) if a.ndim else jnp.reshape(a, (1, 1))
        return jnp.pad(a, ((0, rows - a.shape[0]), (0, d - a.shape[1])))

    return jnp.concatenate([tile(mix_pre), tile(mix_post), tile(ffn_pre), tile(ffn_post), tile(ret_norm.reshape(-1, d)), tile(sink), tile(dec_f),
                            tile(dec_b), tile(loss), tile(meta, SMALL_ROWS - ROW_META)], axis=0)


def _unpack_small(p, depth):
    rows = lambda r0, cols: p[r0:r0 + depth, :cols]
    return dict(mix_pre=rows(ROW_MIX_PRE, D_MODEL), mix_post=rows(ROW_MIX_POST, D_MODEL), ffn_pre=rows(ROW_FFN_PRE, D_MODEL),
                ffn_post=rows(ROW_FFN_POST, D_MODEL), ret_norm=p[ROW_RET_NORM:ROW_RET_NORM + depth * RET_WIDTH // D_MODEL].reshape(depth, RET_WIDTH),
                sink=rows(ROW_SINK, ATT_HEADS), dec_f=rows(ROW_DEC_F, RET_HEADS), dec_b=rows(ROW_DEC_B, RET_HEADS), loss=p[ROW_LOSS, 0])


def kernel(x, meta_tokens, w_in, w_out, attn_sink, ret_decay_fwd, ret_decay_bwd, ret_norm, norm_mix_pre, norm_mix_post, w_gate, w_up, w_down, norm_ffn_pre, norm_ffn_post, loss_target, m_meta_tokens, m_w_in, m_w_out, m_attn_sink, m_ret_decay_fwd, m_ret_decay_bwd, m_ret_norm, m_norm_mix_pre, m_norm_mix_post, m_w_gate, m_w_up, m_w_down, m_norm_ffn_pre, m_norm_ffn_post, v_meta_tokens, v_w_in, v_w_out, v_attn_sink, v_ret_decay_fwd, v_ret_decay_bwd, v_ret_norm, v_norm_mix_pre, v_norm_mix_post, v_w_gate, v_w_up, v_w_down, v_norm_ffn_pre, v_norm_ffn_post):
    depth, d = w_in.shape[0], D_MODEL
    me = 4 * lax.axis_index("x") + 2 * lax.axis_index("y") + lax.axis_index("c")
    zero = jnp.zeros((), F32)

    meta_g, = _allgather([meta_tokens], "gather_meta")
    meta = meta_g.transpose(1, 0, 2).reshape(N_META, d)

    def shards(k):
        l = k // 2
        if k % 2 == 0:
            return [w_in[l].T.astype(BF16), w_out[l].astype(BF16)]
        return [w_gate[l].T.astype(BF16), w_up[l].T.astype(BF16), w_down[l].astype(BF16)]

    gathers, ahead = {}, 2
    for k in range(min(ahead + 1, 2 * depth)):
        gathers[k] = _send_start("gather", shards(k), None, gathers[k - 1][3] if k else meta_g, f"gather_start_g{k}")

    passing = {}

    def pass_on(k, after):
        lands = _send_wait("gather", gathers.pop(k), after, f"gather_wait_g{k}")
        passing[k] = _send_start("forward", [], lands, after, f"forward_start_g{k}")

    def take(k, h):
        after = h
        if k >= 1 and k + ahead < 2 * depth:
            gathers[k + ahead] = _send_start("gather", shards(k + ahead), None, h, f"gather_start_g{k + ahead}")
            after = gathers[k + ahead][3]
        elif k == 0:
            after = gathers[max(gathers)][3]
        if k not in passing:
            pass_on(k, after)
        if k >= 1 and k + 1 < 2 * depth:
            pass_on(k + 1, passing[k][3])
        last = passing[k + 1][3] if k + 1 in passing else passing[k][3]
        return _send_wait("forward", passing.pop(k), last, f"forward_wait_g{k}")

    def mix_weights_fn(l, h):
        wi_t, wo = take(2 * l, h)
        return wi_t.reshape(IN_COLS, d), wo.reshape(d, d), zero

    def ffn_weights_fn(l, h):
        wg_t, wu_t, wd = take(2 * l + 1, h)
        return wg_t.reshape(D_FF, d), wu_t.reshape(D_FF, d), wd.reshape(D_FF, d), zero

    exchanges, adam, order = {}, {}, []
    tr = lambda *ts: tuple(jnp.swapaxes(t, 1, 2) for t in ts)
    big = dict(wi=tr(w_in, m_w_in, v_w_in), wg=tr(w_gate, m_w_gate, v_w_gate), wu=tr(w_up, m_w_up, v_w_up), wd=(w_down, m_w_down, v_w_down),
               wo=(w_out, m_w_out, v_w_out))
    kinds = dict(ffn=("wg", "wu", "wd"), mix=("wi", "wo"))

    def finish(key, after):
        l, part = key
        arrived = _send_wait("scatter", exchanges.pop(key), after, f"exchange_wait_{part}_l{l}")
        for kind, parts in zip(kinds[part], arrived):
            adam[kind] = _adamw(parts, *big[kind], f"adamw_{kind}_l{l}", layer=l, prev=adam.get(kind))

    def grads_fn(l, part, gw, after):
        packed = [gw[kind].reshape(N_DEV, -1, d) for kind in kinds[part]]
        exchanges[(l, part)] = _send_start("scatter", packed, None, after, f"exchange_start_{part}_l{l}")
        order.append((l, part))
        token = exchanges[(l, part)][3]
        if len(order) > 2:
            finish(order[-3], token)
        return token[0, 0]

    loss_part, dh, gs = _local_step(x[0], meta, loss_target[0], mix_weights_fn, ffn_weights_fn, grads_fn, attn_sink, ret_decay_fwd, ret_decay_bwd, ret_norm,
                                    norm_mix_pre, norm_mix_post, norm_ffn_pre, norm_ffn_post)
    grad_x = dh[BLOCK:][None]

    st = lambda xs: jnp.stack([t.reshape(-1) if t.ndim == 1 else t[0] for t in xs])
    small = _pack_small(st(gs["mix_pre"]), st(gs["mix_post"]), st(gs["ffn_pre"]), st(gs["ffn_post"]), st(gs["ret_norm"]), st(gs["sink"]),
                        st(gs["dec_f"]), st(gs["dec_b"]), loss_part, dh[PAD_FRONT:BLOCK])
    small_g, = _allgather([small], "gather_small")
    for key in order[-2:]:
        finish(key, small_g)
    o_wi, o_wo, o_wg, o_wu, o_wd = tr(*adam["wi"]), adam["wo"], tr(*adam["wg"]), tr(*adam["wu"]), adam["wd"]
    zmeta = jnp.zeros((N_META, d), F32)
    packs = [_pack_small(a[0], a[1], a[2], a[3], a[4], a[5], a[6], a[7], zero, zmeta) for a in (
        (norm_mix_pre, norm_mix_post, norm_ffn_pre, norm_ffn_post, ret_norm, attn_sink, ret_decay_fwd, ret_decay_bwd),
        (m_norm_mix_pre, m_norm_mix_post, m_norm_ffn_pre, m_norm_ffn_post, m_ret_norm, m_attn_sink, m_ret_decay_fwd, m_ret_decay_bwd),
        (v_norm_mix_pre, v_norm_mix_post, v_norm_ffn_pre, v_norm_ffn_post, v_ret_norm, v_attn_sink, v_ret_decay_fwd, v_ret_decay_bwd))]
    o_small = [_unpack_small(o, depth) for o in _adamw(small_g, packs[0], packs[1], packs[2], "adamw_small")]
    meta_parts = lax.dynamic_slice(small_g, (0, ROW_META, me * (d // N_DEV)), (N_DEV, N_META, d // N_DEV))
    o_meta = _adamw(meta_parts, meta_tokens, m_meta_tokens, v_meta_tokens, "adamw_meta")

    outs = []
    for i in range(4):
        s = o_small[i]
        outs += [o_meta[i], o_wi[i], o_wo[i], s["sink"], s["dec_f"], s["dec_b"], s["ret_norm"], s["mix_pre"], s["mix_post"], o_wg[i], o_wu[i],
                 o_wd[i], s["ffn_pre"], s["ffn_post"]]
    return (o_small[0]["loss"], grad_x, *outs)
```

```python
import jax
import jax.numpy as jnp
import numpy as np
from jax import lax
from jax.experimental import pallas as pl
from jax.experimental.pallas import tpu as pltpu

F32, BF16 = jnp.float32, jnp.bfloat16

D_MODEL = 2048
N_META = 16
BLOCK = 128
WINDOW = 128
PAD_FRONT = BLOCK - N_META
ATT_HEAD_DIM = 128
ATT_WIDTH = D_MODEL // 2
ATT_HEADS = ATT_WIDTH // ATT_HEAD_DIM
ATT_KV_HEADS = 2
ATT_GROUP = ATT_HEADS // ATT_KV_HEADS
KV_WIDTH = ATT_KV_HEADS * ATT_HEAD_DIM
ROT_DIM = ATT_HEAD_DIM // 4
ROPE_THETA = 500000.0
RET_WIDTH = D_MODEL - ATT_WIDTH
RET_HEAD_DIM = 256
RET_HEADS = RET_WIDTH // RET_HEAD_DIM
RET_THETA = 10000.0
D_FF = 5632
IN_COLS = ATT_WIDTH + 2 * KV_WIDTH + 4 * RET_WIDTH
N_DEV = 8
SHARD_COLS = IN_COLS // N_DEV
EPS = 1e-6
NEG = -1e30
RET_K_SCALE = RET_HEAD_DIM ** -0.5
ATT_SCALE = ATT_HEAD_DIM ** -0.5

COL_AV256 = (ATT_WIDTH + KV_WIDTH) // 256
COL_RQ = (ATT_WIDTH + 2 * KV_WIDTH) // RET_HEAD_DIM
COL_RK = COL_RQ + RET_HEADS
COL_RV = COL_RK + RET_HEADS
COL_RG = COL_RV + RET_HEADS

ADAM_LR, ADAM_B1, ADAM_B2, ADAM_EPS, ADAM_WD, ADAM_STEP = 0.001, 0.9, 0.999, 1e-08, 0.01, 10

ROW_MIX_PRE, ROW_MIX_POST, ROW_FFN_PRE, ROW_FFN_POST, ROW_RET_NORM, ROW_SINK, ROW_DEC_F, ROW_DEC_B, ROW_LOSS, ROW_META, SMALL_ROWS = (
    0, 8, 16, 24, 32, 40, 48, 56, 64, 72, 96)
ADAMW_TILE_ELEMS = 128 * 1024

MESH = pl.DeviceIdType.MESH
ANY = pl.BlockSpec(memory_space=pl.ANY)


def _row_tile(n, cap):
    for t in range(cap - cap % 16, 0, -16):
        if n % t == 0:
            return t
    raise ValueError(n)


def _sds(shape, dtype):
    return jax.ShapeDtypeStruct(shape, dtype)


def _silu(x):
    return x * jax.nn.sigmoid(x)


def _dsilu(x):
    s = jax.nn.sigmoid(x)
    return s * (1.0 + x * (1.0 - s))


def _norm_fwd(x, g, res, out_dtype, name):
    n, d = x.shape
    tr = _row_tile(n, 384)

    def body(*refs):
        if res is None:
            x_ref, g_ref, o_ref = refs
        else:
            x_ref, g_ref, r_ref, o_ref = refs
        xv = x_ref[...]
        r = lax.rsqrt(jnp.mean(xv * xv, axis=-1, keepdims=True) + EPS)
        y = xv * r * g_ref[...]
        if res is not None:
            y = y + r_ref[...]
        o_ref[...] = y.astype(o_ref.dtype)

    row = pl.BlockSpec((tr, d), lambda i: (i, 0))
    ins = [row, pl.BlockSpec((1, d), lambda i: (0, 0))] + ([row] if res is not None else [])
    args = (x, g) + ((res,) if res is not None else ())
    return pl.pallas_call(body, grid=(n // tr,), in_specs=ins, out_specs=row, out_shape=_sds((n, d), out_dtype), name=name)(*args)


def _norm_bwd(dy, x, g, res, out_dtype, name):
    n, d = x.shape
    tr = _row_tile(n, 384)

    def body(*refs):
        if res is None:
            dy_ref, x_ref, g_ref, dx_ref, dg_ref = refs
        else:
            dy_ref, x_ref, g_ref, r_ref, dx_ref, dg_ref = refs
        i = pl.program_id(0)
        xv = x_ref[...]
        r = lax.rsqrt(jnp.mean(xv * xv, axis=-1, keepdims=True) + EPS)
        xhat = xv * r
        dyf = dy_ref[...].astype(F32)
        gdy = dyf * g_ref[...]
        dx = r * (gdy - xhat * jnp.mean(gdy * xhat, axis=-1, keepdims=True))
        if res is not None:
            dx = dx + r_ref[...]
        dx_ref[...] = dx.astype(dx_ref.dtype)

        @pl.when(i == 0)
        def _():
            dg_ref[...] = jnp.zeros_like(dg_ref)

        dg_ref[...] += jnp.sum(dyf * xhat, axis=0, keepdims=True)

    row = pl.BlockSpec((tr, d), lambda i: (i, 0))
    vec = pl.BlockSpec((1, d), lambda i: (0, 0))
    ins = [row, row, vec] + ([row] if res is not None else [])
    args = (dy, x, g) + ((res,) if res is not None else ())
    return pl.pallas_call(body, grid=(n // tr,), in_specs=ins, out_specs=(row, vec),
                          out_shape=(_sds((n, d), out_dtype), _sds((1, d), F32)), name=name,
                          compiler_params=pltpu.CompilerParams(dimension_semantics=("arbitrary",)))(*args)


def _mm(a, b, *, ta, tb, grid, a_blk, a_map, b_blk, b_map, o_blk, o_map, o_shape, o_dtype, name, acc=None):
    nk = grid[2]
    dims = (((0,) if ta else (1,), (1,) if tb else (0,)), ((), ()))

    def body(*refs):
        if acc is None:
            a_ref, b_ref, o_ref = refs[:3]
            c_ref = None
        else:
            a_ref, b_ref, c_ref, o_ref = refs[:4]
        part = lax.dot_general(a_ref[...], b_ref[...], dims, preferred_element_type=F32)
        if nk == 1:
            if c_ref is not None:
                part = part + c_ref[...].astype(F32)
            o_ref[...] = part.astype(o_ref.dtype)
            return
        acc_ref = refs[-1]
        k = pl.program_id(2)

        @pl.when(k == 0)
        def _():
            acc_ref[...] = jnp.zeros_like(acc_ref) if c_ref is None else c_ref[...].astype(F32)

        acc_ref[...] += part

        @pl.when(k == nk - 1)
        def _():
            o_ref[...] = acc_ref[...].astype(o_ref.dtype)

    ins = [pl.BlockSpec(a_blk, a_map), pl.BlockSpec(b_blk, b_map)]
    args = [a, b]
    if acc is not None:
        ins.append(pl.BlockSpec(o_blk, o_map))
        args.append(acc)
    return pl.pallas_call(body, grid=grid, in_specs=ins, out_specs=pl.BlockSpec(o_blk, o_map), out_shape=_sds(o_shape, o_dtype),
                          scratch_shapes=[pltpu.VMEM(o_blk, F32)] if nk > 1 else [], name=name,
                          compiler_params=pltpu.CompilerParams(dimension_semantics=("parallel", "parallel", "arbitrary")))(*args)


TOKEN_TILE = 1056
WIDE_TILE = 1408
DEEP_K_TOKEN_TILE = 528


def _mm_nn(x, w, o_dtype, tn, tk, name, acc=None, tm_cap=TOKEN_TILE):
    n, k = x.shape
    tm = _row_tile(n, tm_cap)
    return _mm(x, w, ta=False, tb=False, grid=(n // tm, w.shape[1] // tn, k // tk), a_blk=(tm, tk), a_map=lambda i, j, kk: (i, kk),
               b_blk=(tk, tn), b_map=lambda i, j, kk: (kk, j), o_blk=(tm, tn), o_map=lambda i, j, kk: (i, j),
               o_shape=(n, w.shape[1]), o_dtype=o_dtype, name=name, acc=acc)


def _mm_nt(dy, w, o_dtype, tn, tk, name, acc=None):
    n, k = dy.shape
    tm = _row_tile(n, TOKEN_TILE)
    return _mm(dy, w, ta=False, tb=True, grid=(n // tm, w.shape[0] // tn, k // tk), a_blk=(tm, tk), a_map=lambda i, j, kk: (i, kk),
               b_blk=(tn, tk), b_map=lambda i, j, kk: (j, kk), o_blk=(tm, tn), o_map=lambda i, j, kk: (i, j),
               o_shape=(n, w.shape[0]), o_dtype=o_dtype, name=name, acc=acc)


def _mm_tn(x, dy, tm, tn, name):
    n, m = x.shape
    tk = _row_tile(n, 2 * TOKEN_TILE)
    return _mm(x, dy, ta=True, tb=False, grid=(m // tm, dy.shape[1] // tn, n // tk), a_blk=(tk, tm), a_map=lambda i, j, kk: (kk, i),
               b_blk=(tk, tn), b_map=lambda i, j, kk: (kk, j), o_blk=(tm, tn), o_map=lambda i, j, kk: (i, j),
               o_shape=(m, dy.shape[1]), o_dtype=BF16, name=name)


def _rope_tables(n):
    pos = (jnp.arange(n) - PAD_FRONT).astype(F32)
    half = ROT_DIM // 2
    ang = pos[:, None] * (ROPE_THETA ** (-jnp.arange(half, dtype=F32) / half))[None, :]
    c, s = jnp.cos(ang), jnp.sin(ang)
    rest = ATT_HEAD_DIM - ROT_DIM
    cos_a = jnp.concatenate([c, c, jnp.ones((n, rest), F32)], axis=1)
    sin_a = jnp.concatenate([-s, s, jnp.zeros((n, rest), F32)], axis=1)
    half = RET_HEAD_DIM // 2
    ang = pos[:, None] * (RET_THETA ** (-jnp.arange(half, dtype=F32) / half))[None, :]
    c, s = jnp.cos(ang), jnp.sin(ang)
    perm = np.zeros((ATT_HEAD_DIM, ATT_HEAD_DIM), np.float32)
    for i in range(ROT_DIM):
        perm[(i + ROT_DIM // 2) % ROT_DIM, i] = 1.0
    return cos_a, sin_a, jnp.concatenate([c, c], axis=1), jnp.concatenate([-s, s], axis=1), jnp.asarray(perm, BF16)


def _rope_att(x, col0, heads, cos, sin, perm, name):
    n = x.shape[0]
    tr = _row_tile(n, 1056)
    hd = ATT_HEAD_DIM

    def body(x_ref, c_ref, s_ref, p_ref, o_ref):
        for h in range(heads):
            cs = slice(h * hd, (h + 1) * hd)
            xb = x_ref[:, cs].astype(BF16)
            sw = jnp.dot(xb, p_ref[...], preferred_element_type=F32)
            o_ref[:, cs] = (xb.astype(F32) * c_ref[...] + sw * s_ref[...]).astype(o_ref.dtype)

    tab = pl.BlockSpec((tr, hd), lambda i: (i, 0))
    return pl.pallas_call(body, grid=(n // tr,),
                          in_specs=[pl.BlockSpec((tr, heads * hd), lambda i: (i, col0)), tab, tab, pl.BlockSpec((hd, hd), lambda i: (0, 0))],
                          out_specs=pl.BlockSpec((tr, heads * hd), lambda i: (i, 0)), out_shape=_sds((n, heads * hd), BF16), name=name)(x, cos, sin, perm)


def _rope_ret(x, col0, cos, sin, name):
    p, n, _ = x.shape
    tr = _row_tile(n, 1056)
    hd = RET_HEAD_DIM

    def body(*refs):
        x_ref, o_ref = refs[0], refs[-1]
        for h in range(2):
            cs = slice(h * hd, (h + 1) * hd)
            xv = x_ref[0, :, cs].astype(F32)
            for q in range(1, p):
                xv = xv + x_ref[q, :, cs].astype(F32)
            if cos is not None:
                sw = jnp.concatenate([xv[:, hd // 2:], xv[:, :hd // 2]], axis=1)
                xv = xv * refs[1][...] + sw * refs[2][...]
            o_ref[:, cs] = xv.astype(o_ref.dtype)

    tab = pl.BlockSpec((tr, hd), lambda i, j: (i, 0))
    ins = [pl.BlockSpec((p, tr, 2 * hd), lambda i, j: (0, i, col0 + j))] + ([tab, tab] if cos is not None else [])
    args = (x,) + ((cos, sin) if cos is not None else ())
    return pl.pallas_call(body, grid=(n // tr, RET_HEADS // 2), in_specs=ins, out_specs=pl.BlockSpec((tr, 2 * hd), lambda i, j: (i, j)),
                          out_shape=_sds((n, RET_WIDTH), BF16), name=name)(*args)


def _att_mask(nblk, n_tot):
    row = lax.broadcasted_iota(jnp.int32, (BLOCK, 4 * BLOCK), 0)
    col = lax.broadcasted_iota(jnp.int32, (BLOCK, 4 * BLOCK), 1)
    qi = nblk * BLOCK + row
    seg = col // BLOCK
    cj = col % BLOCK
    kj = (nblk - 1 + seg) * BLOCK + cj
    band = (jnp.abs(qi - kj) <= WINDOW) & (kj >= PAD_FRONT) & (kj < n_tot) & (seg < 3)
    meta = (seg == 3) & (cj >= PAD_FRONT) & (jnp.abs(qi - cj) > WINDOW)
    return band | meta


def _att_specs(nb, v_col):
    kv = lambda f, cb: pl.BlockSpec((BLOCK, KV_WIDTH), lambda n: (f(n), cb))
    prev, own, nxt, first = (lambda n: jnp.maximum(n - 1, 0)), (lambda n: n), (lambda n: jnp.minimum(n + 1, nb - 1)), (lambda n: 0)
    return [kv(f, 0) for f in (prev, own, nxt, first)] + [kv(f, v_col) for f in (prev, own, nxt, first)]


def _att_probs(s, ok, snk):
    s = jnp.where(ok, s, NEG)
    m = jnp.maximum(jnp.max(s, axis=-1, keepdims=True), snk)
    p = jnp.exp(s - m)
    ps = jnp.exp(snk - m)
    inv = 1.0 / (jnp.sum(p, axis=-1, keepdims=True) + ps)
    return p * inv, ps * inv


def _att_fwd(q, k, proj, sink_b, name):
    n = q.shape[0]
    nb = n // BLOCK
    hd = ATT_HEAD_DIM

    def body(q_ref, kp, ko, kn, km, vp, vo, vn, vm, sink_ref, o_ref):
        nblk = pl.program_id(0)
        ok = _att_mask(nblk, n)
        keep = (nblk * BLOCK + lax.broadcasted_iota(jnp.int32, (BLOCK, 1), 0)) >= PAD_FRONT
        for kh in range(ATT_KV_HEADS):
            cs = slice(kh * hd, (kh + 1) * hd)
            kk = jnp.concatenate([r[:, cs] for r in (kp, ko, kn, km)], axis=0)
            vv = jnp.concatenate([r[:, cs] for r in (vp, vo, vn, vm)], axis=0)
            heads = [kh * ATT_GROUP + g for g in range(ATT_GROUP)]
            q4 = jnp.concatenate([q_ref[:, h * hd:(h + 1) * hd] for h in heads], axis=0)
            s = lax.dot_general(q4, kk, (((1,), (1,)), ((), ())), preferred_element_type=F32) * ATT_SCALE
            ps = []
            for g, h in enumerate(heads):
                p, _ = _att_probs(s[g * BLOCK:(g + 1) * BLOCK], ok, sink_ref[h:h + 1, 0:1])
                ps.append(p)
            o = jnp.dot(jnp.concatenate(ps, axis=0).astype(BF16), vv, preferred_element_type=F32)
            for g, h in enumerate(heads):
                o_ref[:, h * hd:(h + 1) * hd] = jnp.where(keep, o[g * BLOCK:(g + 1) * BLOCK], 0.0).astype(o_ref.dtype)

    qspec = pl.BlockSpec((BLOCK, ATT_WIDTH), lambda i: (i, 0))
    return pl.pallas_call(body, grid=(nb,), in_specs=[qspec] + _att_specs(nb, COL_AV256) + [pl.BlockSpec((ATT_HEADS, 128), lambda i: (0, 0))],
                          out_specs=qspec, out_shape=_sds((n, ATT_WIDTH), BF16), name=name)(q, k, k, k, k, proj, proj, proj, proj, sink_b)


def _att_bwd(q, k, proj, sink_b, dmixed, name):
    n = q.shape[0]
    nb = n // BLOCK
    hd = ATT_HEAD_DIM

    def body(q_ref, kp, ko, kn, km, vp, vo, vn, vm, sink_ref, do_ref, dq_ref, dk_ref, dv_ref, dsink_ref):
        nblk = pl.program_id(0)

        @pl.when(nblk == 0)
        def _():
            dk_ref[...] = jnp.zeros_like(dk_ref)
            dv_ref[...] = jnp.zeros_like(dv_ref)
            dsink_ref[...] = jnp.zeros_like(dsink_ref)

        ok = _att_mask(nblk, n)
        rows = [jnp.maximum(nblk - 1, 0), nblk, jnp.minimum(nblk + 1, nb - 1), 0]
        for kh in range(ATT_KV_HEADS):
            cs = slice(kh * hd, (kh + 1) * hd)
            kk = jnp.concatenate([r[:, cs] for r in (kp, ko, kn, km)], axis=0)
            vv = jnp.concatenate([r[:, cs] for r in (vp, vo, vn, vm)], axis=0)
            heads = [kh * ATT_GROUP + g for g in range(ATT_GROUP)]
            q4 = jnp.concatenate([q_ref[:, h * hd:(h + 1) * hd] for h in heads], axis=0)
            do4 = jnp.concatenate([do_ref[:, h * hd:(h + 1) * hd] for h in heads], axis=0)
            s = lax.dot_general(q4, kk, (((1,), (1,)), ((), ())), preferred_element_type=F32) * ATT_SCALE
            dp = lax.dot_general(do4, vv, (((1,), (1,)), ((), ())), preferred_element_type=F32)
            ps, dss = [], []
            for g, h in enumerate(heads):
                p, psink = _att_probs(s[g * BLOCK:(g + 1) * BLOCK], ok, sink_ref[h:h + 1, 0:1])
                dpg = dp[g * BLOCK:(g + 1) * BLOCK]
                delta = jnp.sum(p * dpg, axis=-1, keepdims=True)
                ps.append(p)
                dss.append(p * (dpg - delta) * ATT_SCALE)
                dsink_ref[h:h + 1, :] = dsink_ref[h:h + 1, :] - jnp.sum(psink * delta, axis=0, keepdims=True)
            ds = jnp.concatenate(dss, axis=0).astype(BF16)
            pb = jnp.concatenate(ps, axis=0).astype(BF16)
            dq = jnp.dot(ds, kk, preferred_element_type=F32)
            for g, h in enumerate(heads):
                dq_ref[:, h * hd:(h + 1) * hd] = dq[g * BLOCK:(g + 1) * BLOCK].astype(dq_ref.dtype)
            dk = lax.dot_general(ds, q4, (((0,), (0,)), ((), ())), preferred_element_type=F32)
            dv = lax.dot_general(pb, do4, (((0,), (0,)), ((), ())), preferred_element_type=F32)
            for seg, r in enumerate(rows):
                at = (pl.ds(pl.multiple_of(r * BLOCK, BLOCK), BLOCK), cs)
                dk_ref[at] += dk[seg * BLOCK:(seg + 1) * BLOCK]
                dv_ref[at] += dv[seg * BLOCK:(seg + 1) * BLOCK]

    qspec = pl.BlockSpec((BLOCK, ATT_WIDTH), lambda i: (i, 0))
    whole = pl.BlockSpec((n, KV_WIDTH), lambda i: (0, 0))
    sinks = pl.BlockSpec((ATT_HEADS, 128), lambda i: (0, 0))
    return pl.pallas_call(body, grid=(nb,), in_specs=[qspec] + _att_specs(nb, COL_AV256) + [sinks, qspec], out_specs=(qspec, whole, whole, sinks),
                          out_shape=(_sds((n, ATT_WIDTH), BF16), _sds((n, KV_WIDTH), F32), _sds((n, KV_WIDTH), F32), _sds((ATT_HEADS, 128), F32)),
                          name=name, compiler_params=pltpu.CompilerParams(dimension_semantics=("arbitrary",)))(
                              q, k, k, k, k, proj, proj, proj, proj, sink_b, dmixed)


def _ret_decay(lg, d):
    a = lax.broadcasted_iota(jnp.int32, (BLOCK, 1), 0)
    b = lax.broadcasted_iota(jnp.int32, (1, BLOCK), 1)
    t_col = a + d * (BLOCK - 1 - 2 * a)
    t_row = b + d * (BLOCK - 1 - 2 * b)
    diff = t_col - t_row
    dist = jnp.maximum(diff, 0).astype(F32)
    dmask = jnp.where(diff >= d, jnp.exp(lg * dist), 0.0)
    tf = t_col.astype(F32)
    xi = jnp.exp(lg * (tf + 1.0))
    zeta = jnp.exp(lg * (BLOCK - 1.0 - tf))
    gam = jnp.exp(jnp.full((1, 1), BLOCK, F32) * lg)
    return dmask, dist, xi, zeta, gam, tf


def _ret_fwd(q, k, proj, lg, name):
    n = q.shape[0]
    nc = n // BLOCK
    hd = RET_HEAD_DIM
    chunk = lambda d, c: c + d * (nc - 1 - 2 * c)

    def body(lg_ref, q_ref, k_ref, v0, v1, v2, v3, o_ref, st_ref, s_ref):
        d, c = pl.program_id(0), pl.program_id(1)

        @pl.when(c == 0)
        def _():
            s_ref[...] = jnp.zeros_like(s_ref)

        for h, v_ref in enumerate((v0, v1, v2, v3)):
            cs = slice(h * hd, (h + 1) * hd)
            dmask, _, xi, zeta, gam, _ = _ret_decay(lg_ref[h, d], d)
            qv = q_ref[:, cs]
            kf = k_ref[:, cs].astype(F32) * RET_K_SCALE
            vv = v_ref[...]
            s = lax.dot_general(qv, kf.astype(BF16), (((1,), (1,)), ((), ())), preferred_element_type=F32)
            sb = s_ref[h]
            o_ref[:, cs] = (jnp.dot((s * dmask).astype(BF16), vv, preferred_element_type=F32)
                            + jnp.dot((qv.astype(F32) * xi).astype(BF16), sb.astype(BF16), preferred_element_type=F32))
            st_ref[h] = sb
            s_ref[h] = gam * sb + lax.dot_general((kf * zeta).astype(BF16), vv, (((0,), (0,)), ((), ())), preferred_element_type=F32)

    wide = pl.BlockSpec((BLOCK, RET_WIDTH), lambda d, c: (chunk(d, c), 0))
    vblk = lambda h: pl.BlockSpec((BLOCK, hd), lambda d, c: (chunk(d, c), COL_RV + h))
    return pl.pallas_call(
        body, grid=(2, nc), in_specs=[pl.BlockSpec(memory_space=pltpu.SMEM), wide, wide] + [vblk(h) for h in range(RET_HEADS)],
        out_specs=(pl.BlockSpec((None, BLOCK, RET_WIDTH), lambda d, c: (d, chunk(d, c), 0)),
                   pl.BlockSpec((RET_HEADS, None, None, hd, hd), lambda d, c: (0, d, c, 0, 0))),
        out_shape=(_sds((2, n, RET_WIDTH), F32), _sds((RET_HEADS, 2, nc, hd, hd), F32)), scratch_shapes=[pltpu.VMEM((RET_HEADS, hd, hd), F32)],
        name=name, compiler_params=pltpu.CompilerParams(dimension_semantics=("parallel", "arbitrary")))(lg, q, k, proj, proj, proj, proj)


def _ret_bwd(q, k, proj, lg, do, states, name):
    n = q.shape[0]
    nc = n // BLOCK
    hd = RET_HEAD_DIM
    chunk = lambda d, r: (nc - 1 - r) + d * (2 * r - (nc - 1))

    def body(lg_ref, q_ref, k_ref, v0, v1, v2, v3, do_ref, st_ref, dq_ref, dk_ref, dv_ref, dlg_ref, ds_ref):
        d, r = pl.program_id(0), pl.program_id(1)

        @pl.when(r == 0)
        def _():
            ds_ref[...] = jnp.zeros_like(ds_ref)
            dlg_ref[...] = jnp.zeros_like(dlg_ref)

        row = lax.broadcasted_iota(jnp.int32, (BLOCK, 1), 0) + chunk(d, r) * BLOCK
        keep = row >= PAD_FRONT
        nt = (((1,), (1,)), ((), ()))
        tn = (((0,), (0,)), ((), ()))
        for h, v_ref in enumerate((v0, v1, v2, v3)):
            cs = slice(h * hd, (h + 1) * hd)
            dmask, dist, xi, zeta, gam, tf = _ret_decay(lg_ref[h, d], d)
            qv, vv, dov = q_ref[:, cs], v_ref[...], do_ref[:, cs]
            qf = qv.astype(F32)
            kf = k_ref[:, cs].astype(F32) * RET_K_SCALE
            kb = kf.astype(BF16)
            sc = st_ref[h]
            dsn = ds_ref[h]
            s = lax.dot_general(qv, kb, nt, preferred_element_type=F32)
            dsc = lax.dot_general(dov, vv, nt, preferred_element_type=F32) * dmask
            dsb = dsc.astype(BF16)
            dq_c = xi * lax.dot_general(dov, sc.astype(BF16), nt, preferred_element_type=F32)
            dk_c = zeta * lax.dot_general(vv, dsn.astype(BF16), nt, preferred_element_type=F32)
            dq = jnp.dot(dsb, kb, preferred_element_type=F32) + dq_c
            dk = lax.dot_general(dsb, qv, tn, preferred_element_type=F32) + dk_c
            dv = (lax.dot_general((s * dmask).astype(BF16), dov, tn, preferred_element_type=F32)
                  + jnp.dot((kf * zeta).astype(BF16), dsn.astype(BF16), preferred_element_type=F32))
            ds_ref[h] = gam * dsn + lax.dot_general((qf * xi).astype(BF16), dov, tn, preferred_element_type=F32)
            dlg = (jnp.sum(dsc * s * dist, keepdims=True)
                   + jnp.sum((tf + 1.0) * jnp.sum(qf * dq_c, axis=-1, keepdims=True), keepdims=True)
                   + jnp.sum((BLOCK - 1.0 - tf) * jnp.sum(kf * dk_c, axis=-1, keepdims=True), keepdims=True)
                   + BLOCK * gam * jnp.sum(dsn * sc, keepdims=True))
            dlg_ref[h] += dlg
            dq_ref[:, cs] = dq
            dk_ref[:, cs] = jnp.where(keep, dk * RET_K_SCALE, 0.0)
            dv_ref[:, cs] = jnp.where(keep, dv, 0.0)

    wide = pl.BlockSpec((BLOCK, RET_WIDTH), lambda d, r: (chunk(d, r), 0))
    vblk = lambda h: pl.BlockSpec((BLOCK, hd), lambda d, r: (chunk(d, r), COL_RV + h))
    plane = pl.BlockSpec((None, BLOCK, RET_WIDTH), lambda d, r: (d, chunk(d, r), 0))
    return pl.pallas_call(
        body, grid=(2, nc),
        in_specs=[pl.BlockSpec(memory_space=pltpu.SMEM), wide, wide] + [vblk(h) for h in range(RET_HEADS)]
        + [wide, pl.BlockSpec((RET_HEADS, None, None, hd, hd), lambda d, r: (0, d, nc - 1 - r, 0, 0))],
        out_specs=(plane, plane, plane, pl.BlockSpec((RET_HEADS, None, 8, 128), lambda d, r: (0, d, 0, 0))),
        out_shape=(_sds((2, n, RET_WIDTH), F32),) * 3 + (_sds((RET_HEADS, 2, 8, 128), F32),),
        scratch_shapes=[pltpu.VMEM((RET_HEADS, hd, hd), F32)], name=name,
        compiler_params=pltpu.CompilerParams(dimension_semantics=("parallel", "arbitrary")))(lg, q, k, proj, proj, proj, proj, do, states)


def _retgate_fwd(o, proj, gain, name):
    _, n, _ = o.shape
    tr = _row_tile(n, 1056)
    hd = RET_HEAD_DIM

    def body(o_ref, rg_ref, g_ref, y_ref):
        ov = o_ref[0] + o_ref[1]
        r = lax.rsqrt(jnp.mean(ov * ov, axis=-1, keepdims=True) + EPS)
        y_ref[...] = (_silu(rg_ref[...].astype(F32)) * (ov * r * g_ref[...])).astype(y_ref.dtype)

    return pl.pallas_call(body, grid=(n // tr, RET_HEADS),
                          in_specs=[pl.BlockSpec((2, tr, hd), lambda i, h: (0, i, h)), pl.BlockSpec((tr, hd), lambda i, h: (i, COL_RG + h)),
                                    pl.BlockSpec((1, hd), lambda i, h: (0, h))],
                          out_specs=pl.BlockSpec((tr, hd), lambda i, h: (i, h)), out_shape=_sds((n, RET_WIDTH), BF16), name=name)(o, proj, gain)


def _retgate_bwd(dmixed, o, proj, gain, name):
    _, n, _ = o.shape
    tr = _row_tile(n, 1056)
    hd = RET_HEAD_DIM

    def body(dy_ref, o_ref, rg_ref, g_ref, do_ref, drg_ref, dg_ref):
        i = pl.program_id(1)
        ov = o_ref[0] + o_ref[1]
        r = lax.rsqrt(jnp.mean(ov * ov, axis=-1, keepdims=True) + EPS)
        xhat = ov * r
        rg = rg_ref[...].astype(F32)
        dy = dy_ref[...].astype(F32)
        drg_ref[...] = (dy * (xhat * g_ref[...]) * _dsilu(rg)).astype(drg_ref.dtype)
        dn = dy * _silu(rg)
        dxh = dn * g_ref[...]
        do_ref[...] = (r * (dxh - xhat * jnp.mean(dxh * xhat, axis=-1, keepdims=True))).astype(do_ref.dtype)

        @pl.when(i == 0)
        def _():
            dg_ref[...] = jnp.zeros_like(dg_ref)

        dg_ref[...] += jnp.sum(dn * xhat, axis=0, keepdims=True)

    tile = pl.BlockSpec((tr, hd), lambda h, i: (i, h))
    vec = pl.BlockSpec((1, hd), lambda h, i: (0, h))
    return pl.pallas_call(body, grid=(RET_HEADS, n // tr),
                          in_specs=[pl.BlockSpec((tr, hd), lambda h, i: (i, RET_HEADS + h)), pl.BlockSpec((2, tr, hd), lambda h, i: (0, i, h)),
                                    pl.BlockSpec((tr, hd), lambda h, i: (i, COL_RG + h)), vec],
                          out_specs=(tile, tile, vec), out_shape=(_sds((n, RET_WIDTH), BF16), _sds((n, RET_WIDTH), BF16), _sds((1, RET_WIDTH), F32)),
                          name=name, compiler_params=pltpu.CompilerParams(dimension_semantics=("parallel", "arbitrary")))(dmixed, o, proj, gain)


def _swiglu_fwd(gate, up, name):
    n, f = gate.shape
    tr, tc = _row_tile(n, 384), f // 2

    def body(g_ref, u_ref, o_ref):
        o_ref[...] = (_silu(g_ref[...].astype(F32)) * u_ref[...].astype(F32)).astype(o_ref.dtype)

    t = pl.BlockSpec((tr, tc), lambda i, j: (i, j))
    return pl.pallas_call(body, grid=(n // tr, 2), in_specs=[t, t], out_specs=t, out_shape=_sds((n, f), BF16), name=name)(gate, up)


def _swiglu_bwd(df, gate, up, name):
    n, f = gate.shape
    tr, tc = _row_tile(n, 384), f // 2

    def body(d_ref, g_ref, u_ref, dg_ref, du_ref):
        d, g = d_ref[...].astype(F32), g_ref[...].astype(F32)
        dg_ref[...] = (d * u_ref[...].astype(F32) * _dsilu(g)).astype(dg_ref.dtype)
        du_ref[...] = (d * _silu(g)).astype(du_ref.dtype)

    t = pl.BlockSpec((tr, tc), lambda i, j: (i, j))
    return pl.pallas_call(body, grid=(n // tr, 2), in_specs=[t, t, t], out_specs=(t, t), out_shape=(_sds((n, f), BF16),) * 2, name=name)(df, gate, up)


def _loss_head(h, target, name):
    n, d = h.shape
    nb = n // BLOCK

    def body(h_ref, t_ref, dh_ref, l_ref):
        i = pl.program_id(0)

        @pl.when(i == 0)
        def _():
            l_ref[...] = jnp.zeros_like(l_ref)
            dh_ref[...] = jnp.zeros_like(dh_ref)

        @pl.when(i > 0)
        def _():
            e = h_ref[...] - t_ref[...]
            dh_ref[...] = e * (1.0 / d)
            l_ref[...] += 0.5 * jnp.sum(jnp.mean(e * e, axis=-1, keepdims=True), keepdims=True)

    blk = pl.BlockSpec((BLOCK, d), lambda i: (i, 0))
    return pl.pallas_call(body, grid=(nb,), in_specs=[blk, pl.BlockSpec((BLOCK, d), lambda i: (jnp.maximum(i - 1, 0), 0))],
                          out_specs=(blk, pl.BlockSpec((8, 128), lambda i: (0, 0))), out_shape=(_sds((n, d), F32), _sds((8, 128), F32)), name=name,
                          compiler_params=pltpu.CompilerParams(dimension_semantics=("arbitrary",)))(h, target)


def _adamw(parts, w, m, v, name, sel=None, layer=None, prev=None):
    s, (r, c) = parts.shape[0], parts.shape[-2:]
    tr = _row_tile(r, max(16, (ADAMW_TILE_ELEMS // c) // 16 * 16))
    b1c, b2c = 1.0 - ADAM_B1 ** ADAM_STEP, 1.0 - ADAM_B2 ** ADAM_STEP

    def body(p_ref, w_ref, m_ref, v_ref, *rest):
        g_ref, d_ref, mo_ref, vo_ref = rest[-4:]
        g = p_ref[0].astype(F32)
        for q in range(1, s):
            g = g + p_ref[q].astype(F32)
        mn = ADAM_B1 * m_ref[...] + (1.0 - ADAM_B1) * g
        vn = ADAM_B2 * v_ref[...] + (1.0 - ADAM_B2) * jnp.square(g)
        g_ref[...] = g
        mo_ref[...] = mn
        vo_ref[...] = vn
        d_ref[...] = -ADAM_LR * ((mn / b1c) / (jnp.sqrt(vn / b2c) + ADAM_EPS) + ADAM_WD * w_ref[...])

    pspec = (pl.BlockSpec((s, tr, c), lambda i: (0, i, 0)) if sel is None else pl.BlockSpec((s, None, tr, c), lambda i: (0, sel, i, 0)))
    if layer is None:
        t = pl.BlockSpec((tr, c), lambda i: (i, 0))
        return pl.pallas_call(body, grid=(r // tr,), in_specs=[pspec, t, t, t], out_specs=(t, t, t, t), out_shape=(_sds((r, c), F32),) * 4,
                              name=name)(parts, w, m, v)
    t = pl.BlockSpec((None, tr, c), lambda i: (layer, i, 0))
    prev = prev if prev is not None else tuple(lax.empty(w.shape, F32) for _ in range(4))
    return pl.pallas_call(body, grid=(r // tr,), in_specs=[pspec, t, t, t] + [ANY] * 4, out_specs=(t, t, t, t), out_shape=(_sds(w.shape, F32),) * 4,
                          input_output_aliases={4 + i: i for i in range(4)}, name=name)(parts, w, m, v, *prev)


def _allgather(xs, name):
    na = len(xs)

    def body(*refs):
        x_refs, o_refs = refs[:na], refs[na:2 * na]
        send, recv, lsem = refs[2 * na:]
        x, y, c = lax.axis_index("x"), lax.axis_index("y"), lax.axis_index("c")
        me, sib = (x, y, c), (x, y, 1 - c)
        chips = [(1 - x, y), (x, 1 - y), (1 - x, 1 - y)]
        slot = lambda p: 4 * p[0] + 2 * p[1] + p[2]

        def copy(a, k, block, to, src=None):
            dst = o_refs[a].at[slot(block)]
            return pltpu.make_async_remote_copy(src_ref=dst if src is None else src, dst_ref=dst, send_sem=send.at[a, k], recv_sem=recv.at[a, k],
                                                device_id=to, device_id_type=MESH)

        mine = [pltpu.make_async_copy(x_refs[a], o_refs[a].at[slot(me)], lsem.at[a]) for a in range(na)]
        for cp in mine:
            cp.start()
        first = []
        for a in range(na):
            first.append(copy(a, 0, me, sib, src=x_refs[a]))
            first += [copy(a, 1 + j, me, (*chip, c), src=x_refs[a]) for j, chip in enumerate(chips)]
        for cp in first:
            cp.start()
        passed = []
        for j, chip in enumerate(chips):
            for a in range(na):
                copy(a, 1 + j, (*chip, c), me).wait_recv()
                passed.append(copy(a, 4 + j, (*chip, c), sib))
                passed[-1].start()
        for a in range(na):
            copy(a, 0, sib, me).wait_recv()
            for j, chip in enumerate(chips):
                copy(a, 4 + j, (*chip, 1 - c), me).wait_recv()
        for cp in first + passed:
            cp.wait_send()
        for cp in mine:
            cp.wait()

    return pl.pallas_call(body, in_specs=[ANY] * na, out_specs=[ANY] * na, out_shape=[_sds((N_DEV,) + t.shape, t.dtype) for t in xs],
                          scratch_shapes=[pltpu.SemaphoreType.DMA((na, 7)), pltpu.SemaphoreType.DMA((na, 7)), pltpu.SemaphoreType.DMA((na,))],
                          name=name)(*xs)


HBM = pl.BlockSpec(memory_space=pltpu.HBM)
SEM = pl.BlockSpec(memory_space=pltpu.SEMAPHORE)
EFFECT = pltpu.SideEffectType.DATAFLOW_SIDE_EFFECTING


SPLIT_RELATIONS = dict(gather=(1, 2, 4, 6),
                       forward=(2, 4, 6),
                       scatter=tuple(range(1, N_DEV)))


def _split_copies(mode, x_refs, land_refs, send, recv, own, landing):
    x, y, c = lax.axis_index("x"), lax.axis_index("y"), lax.axis_index("c")
    flip = lambda r: ((1 - x if r & 4 else x), (1 - y if r & 2 else y), (1 - c if r & 1 else c))
    slot = lambda p: 4 * p[0] + 2 * p[1] + p[2]
    me = slot((x, y, c))
    rel = SPLIT_RELATIONS[mode]
    local, remote = [], []
    for a in range(len(land_refs)):
        if mode != "forward":
            local.append(pltpu.make_async_copy(x_refs[a].at[me] if mode == "scatter" else x_refs[a], land_refs[a].at[me], own.at[a]))
        for j, r in enumerate(rel):
            if mode == "forward":
                to = flip(1)
                src = dst = land_refs[a].at[slot(flip(r ^ 1) if landing else flip(r))]
            else:
                to = flip(r)
                src = x_refs[a].at[slot(to)] if mode == "scatter" else x_refs[a]
                dst = land_refs[a].at[slot(to) if landing else me]
            remote.append(pltpu.make_async_remote_copy(src_ref=src, dst_ref=dst, send_sem=send.at[len(rel) * a + j],
                                                       recv_sem=recv.at[len(rel) * a + j], device_id=to, device_id_type=MESH))
    return local, remote


def _send_start(mode, xs, lands, after, name):
    if lands is None:
        lands = [lax.empty(t.shape if mode == "scatter" else (N_DEV,) + t.shape, t.dtype) for t in xs]
    nx, na, nr = len(xs), len(lands), len(SPLIT_RELATIONS[mode])
    nsem = 2 if mode == "forward" else 3

    def body(*refs):
        x_refs, land_refs = refs[:nx], refs[nx:nx + na]
        sems = refs[nx + na + 1:nx + na + 1 + nsem]
        local, remote = _split_copies(mode, x_refs, land_refs, sems[0], sems[1], sems[2] if nsem == 3 else None, False)
        for cp in remote + local:
            cp.start()
        refs[-1][...] = jnp.zeros_like(refs[-1])

    hbm = lambda t: pltpu.with_memory_space_constraint(t, pltpu.HBM)
    sem_shapes = [pltpu.SemaphoreType.DMA((nr * na,)), pltpu.SemaphoreType.DMA((nr * na,)), pltpu.SemaphoreType.DMA((na,))][:nsem]
    outs = pl.pallas_call(
        body, name=name,
        out_shape=(*sem_shapes, *[pltpu.HBM(t.shape, t.dtype) for t in list(xs) + list(lands)], _sds((8, 128), F32)),
        in_specs=[HBM] * (nx + na) + [ANY], out_specs=(*[SEM] * nsem, *[HBM] * (nx + na), pl.BlockSpec(memory_space=pltpu.VMEM)),
        input_output_aliases={i: nsem + i for i in range(nx + na)},
        compiler_params=pltpu.CompilerParams(has_side_effects=EFFECT))(*[hbm(t) for t in list(xs) + list(lands)], after)
    return outs[:nsem], list(outs[nsem:nsem + nx]), list(outs[nsem + nx:nsem + nx + na]), outs[-1]


def _send_wait(mode, started, after, name):
    sems, xs, lands, _ = started
    nx, na, nsem = len(xs), len(lands), len(sems)

    def body(*refs):
        s = refs[nx + na:nx + na + nsem]
        local, remote = _split_copies(mode, refs[:nx], refs[nx:nx + na], s[0], s[1], s[2] if nsem == 3 else None, True)
        for cp in remote:
            cp.wait_send()
            cp.wait_recv()
        for cp in local:
            cp.wait()

    outs = pl.pallas_call(body, name=name, out_shape=tuple(pltpu.HBM(t.shape, t.dtype) for t in xs + lands),
                          in_specs=[HBM] * (nx + na) + [SEM] * nsem + [ANY], out_specs=[HBM] * (nx + na),
                          input_output_aliases={i: i for i in range(nx + na)},
                          compiler_params=pltpu.CompilerParams(has_side_effects=EFFECT))(*xs, *lands, *sems, after)
    return list(outs[nx:])


def _local_step(x, meta, target, mix_weights_fn, ffn_weights_fn, grads_fn, sink, dec_f, dec_b, ret_norm, n_mix_pre, n_mix_post, n_ffn_pre,
                n_ffn_post):
    depth = n_mix_pre.shape[0]
    d = D_MODEL
    h = jnp.concatenate([jnp.zeros((PAD_FRONT, d), F32), meta, x], axis=0)
    n = h.shape[0]
    cos_a, sin_a, cos_r, sin_r, perm = _rope_tables(n)
    lg_all = jnp.stack([-jnp.exp(dec_f), -jnp.exp(dec_b)], axis=-1)
    saved = []
    for l in range(depth):
        t = f"l{l}_"
        sink_b = jnp.broadcast_to(sink[l][:, None], (ATT_HEADS, 128))
        wi, wo, tok = mix_weights_fn(l, h)
        u = _norm_fwd(h, (n_mix_pre[l] + tok)[None], None, BF16, t + "norm_mix_pre")
        proj = _mm_nt(u, wi, BF16, WIDE_TILE, d, t + "proj")
        aq = _rope_att(proj, 0, ATT_HEADS, cos_a, sin_a, perm, t + "rope_aq")
        ak = _rope_att(proj, ATT_WIDTH // KV_WIDTH, ATT_KV_HEADS, cos_a, sin_a, perm, t + "rope_ak")
        att = _att_fwd(aq, ak, proj, sink_b, t + "att")
        proj3 = proj[None]
        rq = _rope_ret(proj3, COL_RQ // 2, cos_r, sin_r, t + "rope_rq")
        rk = _rope_ret(proj3, COL_RK // 2, cos_r, sin_r, t + "rope_rk")
        o_ret, states = _ret_fwd(rq, rk, proj, lg_all[l], t + "ret")
        retg = _retgate_fwd(o_ret, proj, ret_norm[l][None], t + "retgate")
        mixed = jnp.concatenate([att, retg], axis=1)
        mo = _mm_nn(mixed, wo, F32, 1024, d, t + "out_proj")
        h_mid = _norm_fwd(mo, n_mix_post[l][None], h, F32, t + "norm_mix_post")
        wg, wu, wd, tok = ffn_weights_fn(l, h_mid)
        u2 = _norm_fwd(h_mid, (n_ffn_pre[l] + tok)[None], None, BF16, t + "norm_ffn_pre")
        gate = _mm_nt(u2, wg, BF16, WIDE_TILE, d, t + "gate")
        up = _mm_nt(u2, wu, BF16, WIDE_TILE, d, t + "up")
        f = _swiglu_fwd(gate, up, t + "swiglu")
        dn = _mm_nn(f, wd, F32, 512, D_FF, t + "down", tm_cap=DEEP_K_TOKEN_TILE)
        h_out = _norm_fwd(dn, n_ffn_post[l][None], h_mid, F32, t + "norm_ffn_post")
        saved.append(dict(h=h, u=u, proj=proj, aq=aq, ak=ak, rq=rq, rk=rk, o_ret=o_ret, states=states, mixed=mixed, mo=mo, h_mid=h_mid, u2=u2,
                          gate=gate, up=up, f=f, dn=dn, sink_b=sink_b, wi=wi, wo=wo, wg=wg, wu=wu, wd=wd))
        h = h_out

    dh, loss_part = _loss_head(h, target, "loss_head")
    gs = dict(sink=[None] * depth, dec_f=[None] * depth, dec_b=[None] * depth, ret_norm=[None] * depth, mix_pre=[None] * depth,
              mix_post=[None] * depth, ffn_pre=[None] * depth, ffn_post=[None] * depth)
    tok_b = jnp.zeros((), F32)
    for l in reversed(range(depth)):
        t = f"l{l}_b_"
        sv = saved[l]
        proj = sv["proj"]
        d_dn, gs["ffn_post"][l] = _norm_bwd(dh, sv["dn"], (n_ffn_post[l] + tok_b)[None], None, BF16, t + "norm_ffn_post")
        gw = {}
        d_f = _mm_nt(d_dn, sv["wd"], BF16, WIDE_TILE, d, t + "d_f")
        gw["wd"] = _mm_tn(sv["f"], d_dn, WIDE_TILE, 1024, t + "dw_down")
        d_gate, d_up = _swiglu_bwd(d_f, sv["gate"], sv["up"], t + "swiglu")
        du2 = _mm_nn(d_gate, sv["wg"], F32, 512, D_FF, t + "du2_gate", tm_cap=DEEP_K_TOKEN_TILE)
        du2 = _mm_nn(d_up, sv["wu"], F32, 512, D_FF, t + "du2_up", acc=du2, tm_cap=DEEP_K_TOKEN_TILE)
        gw["wg"] = _mm_tn(d_gate, sv["u2"], WIDE_TILE, 1024, t + "dw_gate")
        gw["wu"] = _mm_tn(d_up, sv["u2"], WIDE_TILE, 1024, t + "dw_up")
        tok_b = grads_fn(l, "ffn", gw, du2)
        dh, gs["ffn_pre"][l] = _norm_bwd(du2, sv["h_mid"], (n_ffn_pre[l] + tok_b)[None], dh, F32, t + "norm_ffn_pre")
        d_mo, gs["mix_post"][l] = _norm_bwd(dh, sv["mo"], n_mix_post[l][None], None, BF16, t + "norm_mix_post")
        d_mixed = _mm_nt(d_mo, sv["wo"], BF16, 1024, d, t + "d_mixed")
        gw["wo"] = _mm_tn(sv["mixed"], d_mo, 1024, 1024, t + "dw_out")
        d_o, d_rg, gs["ret_norm"][l] = _retgate_bwd(d_mixed, sv["o_ret"], proj, ret_norm[l][None], t + "retgate")
        dq_r, dk_r, dv_r, dlg = _ret_bwd(sv["rq"], sv["rk"], proj, lg_all[l], d_o, sv["states"], t + "ret")
        draw = dlg[:, :, 0, 0] * lg_all[l]
        gs["dec_f"][l], gs["dec_b"][l] = draw[:, 0], draw[:, 1]
        dq_a, dk_a, dv_a, dsink = _att_bwd(sv["aq"], sv["ak"], proj, sv["sink_b"], d_mixed, t + "att")
        gs["sink"][l] = dsink[:, 0]
        dproj = jnp.concatenate([
            _rope_att(dq_a, 0, ATT_HEADS, cos_a, -sin_a, perm, t + "rope_aq"),
            _rope_att(dk_a, 0, ATT_KV_HEADS, cos_a, -sin_a, perm, t + "rope_ak"),
            dv_a.astype(BF16),
            _rope_ret(dq_r, 0, cos_r, -sin_r, t + "rope_rq"),
            _rope_ret(dk_r, 0, cos_r, -sin_r, t + "rope_rk"),
            _rope_ret(dv_r, 0, None, None, t + "sum_rv"),
            d_rg], axis=1)
        gw["wi"] = _mm_tn(dproj, sv["u"], WIDE_TILE, 1024, t + "dw_in")
        tok_b = grads_fn(l, "mix", gw, dproj)
        du = _mm_nn(dproj, sv["wi"], F32, 512, IN_COLS, t + "du", tm_cap=DEEP_K_TOKEN_TILE)
        dh, gs["mix_pre"][l] = _norm_bwd(du, sv["h"], (n_mix_pre[l] + tok_b)[None], dh, F32, t + "norm_mix_pre")
    return loss_part[0, 0], dh, gs


def _pack_small(mix_pre, mix_post, ffn_pre, ffn_post, ret_norm, sink, dec_f, dec_b, loss, meta):
    d = D_MODEL

    def tile(a, rows=8):
        a = jnp.reshape(a, (-1, a.shape[-1])) if a.ndim else jnp.reshape(a, (1, 1))
        return jnp.pad(a, ((0, rows - a.shape[0]), (0, d - a.shape[1])))

    return jnp.concatenate([tile(mix_pre), tile(mix_post), tile(ffn_pre), tile(ffn_post), tile(ret_norm.reshape(-1, d)), tile(sink), tile(dec_f),
                            tile(dec_b), tile(loss), tile(meta, SMALL_ROWS - ROW_META)], axis=0)


def _unpack_small(p, depth):
    rows = lambda r0, cols: p[r0:r0 + depth, :cols]
    return dict(mix_pre=rows(ROW_MIX_PRE, D_MODEL), mix_post=rows(ROW_MIX_POST, D_MODEL), ffn_pre=rows(ROW_FFN_PRE, D_MODEL),
                ffn_post=rows(ROW_FFN_POST, D_MODEL), ret_norm=p[ROW_RET_NORM:ROW_RET_NORM + depth * RET_WIDTH // D_MODEL].reshape(depth, RET_WIDTH),
                sink=rows(ROW_SINK, ATT_HEADS), dec_f=rows(ROW_DEC_F, RET_HEADS), dec_b=rows(ROW_DEC_B, RET_HEADS), loss=p[ROW_LOSS, 0])


def kernel(x, meta_tokens, w_in, w_out, attn_sink, ret_decay_fwd, ret_decay_bwd, ret_norm, norm_mix_pre, norm_mix_post, w_gate, w_up, w_down, norm_ffn_pre, norm_ffn_post, loss_target, m_meta_tokens, m_w_in, m_w_out, m_attn_sink, m_ret_decay_fwd, m_ret_decay_bwd, m_ret_norm, m_norm_mix_pre, m_norm_mix_post, m_w_gate, m_w_up, m_w_down, m_norm_ffn_pre, m_norm_ffn_post, v_meta_tokens, v_w_in, v_w_out, v_attn_sink, v_ret_decay_fwd, v_ret_decay_bwd, v_ret_norm, v_norm_mix_pre, v_norm_mix_post, v_w_gate, v_w_up, v_w_down, v_norm_ffn_pre, v_norm_ffn_post):
    depth, d = w_in.shape[0], D_MODEL
    me = 4 * lax.axis_index("x") + 2 * lax.axis_index("y") + lax.axis_index("c")
    zero = jnp.zeros((), F32)

    meta_g, = _allgather([meta_tokens], "gather_meta")
    meta = meta_g.transpose(1, 0, 2).reshape(N_META, d)

    def shards(k):
        l = k // 2
        if k % 2 == 0:
            return [w_in[l].T.astype(BF16), w_out[l].astype(BF16)]
        return [w_gate[l].T.astype(BF16), w_up[l].T.astype(BF16), w_down[l].astype(BF16)]

    gathers, ahead = {}, 2
    for k in range(min(ahead + 1, 2 * depth)):
        gathers[k] = _send_start("gather", shards(k), None, gathers[k - 1][3] if k else meta_g, f"gather_start_g{k}")

    passing = {}

    def pass_on(k, after):
        lands = _send_wait("gather", gathers.pop(k), after, f"gather_wait_g{k}")
        passing[k] = _send_start("forward", [], lands, after, f"forward_start_g{k}")

    def take(k, h):
        after = h
        if k >= 1 and k + ahead < 2 * depth:
            gathers[k + ahead] = _send_start("gather", shards(k + ahead), None, h, f"gather_start_g{k + ahead}")
            after = gathers[k + ahead][3]
        elif k == 0:
            after = gathers[max(gathers)][3]
        if k not in passing:
            pass_on(k, after)
        if k >= 1 and k + 1 < 2 * depth:
            pass_on(k + 1, after)
        last = passing[k + 1][3] if k + 1 in passing else passing[k][3]
        return _send_wait("forward", passing.pop(k), last, f"forward_wait_g{k}")

    def mix_weights_fn(l, h):
        wi_t, wo = take(2 * l, h)
        return wi_t.reshape(IN_COLS, d), wo.reshape(d, d), zero

    def ffn_weights_fn(l, h):
        wg_t, wu_t, wd = take(2 * l + 1, h)
        return wg_t.reshape(D_FF, d), wu_t.reshape(D_FF, d), wd.reshape(D_FF, d), zero

    exchanges, adam, order = {}, {}, []
    tr = lambda *ts: tuple(jnp.swapaxes(t, 1, 2) for t in ts)
    big = dict(wi=tr(w_in, m_w_in, v_w_in), wg=tr(w_gate, m_w_gate, v_w_gate), wu=tr(w_up, m_w_up, v_w_up), wd=(w_down, m_w_down, v_w_down),
               wo=(w_out, m_w_out, v_w_out))
    kinds = dict(ffn=("wg", "wu", "wd"), mix=("wi", "wo"))

    def finish(key, after):
        l, part = key
        arrived = _send_wait("scatter", exchanges.pop(key), after, f"exchange_wait_{part}_l{l}")
        for kind, parts in zip(kinds[part], arrived):
            adam[kind] = _adamw(parts, *big[kind], f"adamw_{kind}_l{l}", layer=l, prev=adam.get(kind))

    def grads_fn(l, part, gw, after):
        packed = [gw[kind].reshape(N_DEV, -1, d) for kind in kinds[part]]
        exchanges[(l, part)] = _send_start("scatter", packed, None, after, f"exchange_start_{part}_l{l}")
        order.append((l, part))
        token = exchanges[(l, part)][3]
        if len(order) > 2:
            finish(order[-3], token)
        return token[0, 0]

    loss_part, dh, gs = _local_step(x[0], meta, loss_target[0], mix_weights_fn, ffn_weights_fn, grads_fn, attn_sink, ret_decay_fwd, ret_decay_bwd, ret_norm,
                                    norm_mix_pre, norm_mix_post, norm_ffn_pre, norm_ffn_post)
    grad_x = dh[BLOCK:][None]

    st = lambda xs: jnp.stack([t.reshape(-1) if t.ndim == 1 else t[0] for t in xs])
    small = _pack_small(st(gs["mix_pre"]), st(gs["mix_post"]), st(gs["ffn_pre"]), st(gs["ffn_post"]), st(gs["ret_norm"]), st(gs["sink"]),
                        st(gs["dec_f"]), st(gs["dec_b"]), loss_part, dh[PAD_FRONT:BLOCK])
    small_g, = _allgather([small], "gather_small")
    for key in order[-2:]:
        finish(key, small_g)
    o_wi, o_wo, o_wg, o_wu, o_wd = tr(*adam["wi"]), adam["wo"], tr(*adam["wg"]), tr(*adam["wu"]), adam["wd"]
    zmeta = jnp.zeros((N_META, d), F32)
    packs = [_pack_small(a[0], a[1], a[2], a[3], a[4], a[5], a[6], a[7], zero, zmeta) for a in (
        (norm_mix_pre, norm_mix_post, norm_ffn_pre, norm_ffn_post, ret_norm, attn_sink, ret_decay_fwd, ret_decay_bwd),
        (m_norm_mix_pre, m_norm_mix_post, m_norm_ffn_pre, m_norm_ffn_post, m_ret_norm, m_attn_sink, m_ret_decay_fwd, m_ret_decay_bwd),
        (v_norm_mix_pre, v_norm_mix_post, v_norm_ffn_pre, v_norm_ffn_post, v_ret_norm, v_attn_sink, v_ret_decay_fwd, v_ret_decay_bwd))]
    o_small = [_unpack_small(o, depth) for o in _adamw(small_g, packs[0], packs[1], packs[2], "adamw_small")]
    meta_parts = lax.dynamic_slice(small_g, (0, ROW_META, me * (d // N_DEV)), (N_DEV, N_META, d // N_DEV))
    o_meta = _adamw(meta_parts, meta_tokens, m_meta_tokens, v_meta_tokens, "adamw_meta")

    outs = []
    for i in range(4):
        s = o_small[i]
        outs += [o_meta[i], o_wi[i], o_wo[i], s["sink"], s["dec_f"], s["dec_b"], s["ret_norm"], s["mix_pre"], s["mix_post"], o_wg[i], o_wu[i],
                 o_wd[i], s["ffn_pre"], s["ffn_post"]]
    return (o_small[0]["loss"], grad_x, *outs)
```

```python
import jax
import jax.numpy as jnp
import numpy as np
from jax import lax
from jax.experimental import pallas as pl
from jax.experimental.pallas import tpu as pltpu

F32, BF16 = jnp.float32, jnp.bfloat16

D_MODEL = 2048
N_META = 16
BLOCK = 128
WINDOW = 128
PAD_FRONT = BLOCK - N_META
ATT_HEAD_DIM = 128
ATT_WIDTH = D_MODEL // 2
ATT_HEADS = ATT_WIDTH // ATT_HEAD_DIM
ATT_KV_HEADS = 2
ATT_GROUP = ATT_HEADS // ATT_KV_HEADS
KV_WIDTH = ATT_KV_HEADS * ATT_HEAD_DIM
ROT_DIM = ATT_HEAD_DIM // 4
ROPE_THETA = 500000.0
RET_WIDTH = D_MODEL - ATT_WIDTH
RET_HEAD_DIM = 256
RET_HEADS = RET_WIDTH // RET_HEAD_DIM
RET_THETA = 10000.0
D_FF = 5632
IN_COLS = ATT_WIDTH + 2 * KV_WIDTH + 4 * RET_WIDTH
N_DEV = 8
SHARD_COLS = IN_COLS // N_DEV
EPS = 1e-6
NEG = -1e30
RET_K_SCALE = RET_HEAD_DIM ** -0.5
ATT_SCALE = ATT_HEAD_DIM ** -0.5

COL_AV256 = (ATT_WIDTH + KV_WIDTH) // 256
COL_RQ = (ATT_WIDTH + 2 * KV_WIDTH) // RET_HEAD_DIM
COL_RK = COL_RQ + RET_HEADS
COL_RV = COL_RK + RET_HEADS
COL_RG = COL_RV + RET_HEADS

ADAM_LR, ADAM_B1, ADAM_B2, ADAM_EPS, ADAM_WD, ADAM_STEP = 0.001, 0.9, 0.999, 1e-08, 0.01, 10

ROW_MIX_PRE, ROW_MIX_POST, ROW_FFN_PRE, ROW_FFN_POST, ROW_RET_NORM, ROW_SINK, ROW_DEC_F, ROW_DEC_B, ROW_LOSS, ROW_META, SMALL_ROWS = (
    0, 8, 16, 24, 32, 40, 48, 56, 64, 72, 96)
ADAMW_TILE_ELEMS = 128 * 1024

MESH = pl.DeviceIdType.MESH
ANY = pl.BlockSpec(memory_space=pl.ANY)


def _row_tile(n, cap):
    for t in range(cap - cap % 16, 0, -16):
        if n % t == 0:
            return t
    raise ValueError(n)


def _sds(shape, dtype):
    return jax.ShapeDtypeStruct(shape, dtype)


def _silu(x):
    return x * jax.nn.sigmoid(x)


def _dsilu(x):
    s = jax.nn.sigmoid(x)
    return s * (1.0 + x * (1.0 - s))


def _norm_fwd(x, g, res, out_dtype, name):
    n, d = x.shape
    tr = _row_tile(n, 384)

    def body(*refs):
        if res is None:
            x_ref, g_ref, o_ref = refs
        else:
            x_ref, g_ref, r_ref, o_ref = refs
        xv = x_ref[...]
        r = lax.rsqrt(jnp.mean(xv * xv, axis=-1, keepdims=True) + EPS)
        y = xv * r * g_ref[...]
        if res is not None:
            y = y + r_ref[...]
        o_ref[...] = y.astype(o_ref.dtype)

    row = pl.BlockSpec((tr, d), lambda i: (i, 0))
    ins = [row, pl.BlockSpec((1, d), lambda i: (0, 0))] + ([row] if res is not None else [])
    args = (x, g) + ((res,) if res is not None else ())
    return pl.pallas_call(body, grid=(n // tr,), in_specs=ins, out_specs=row, out_shape=_sds((n, d), out_dtype), name=name)(*args)


def _norm_bwd(dy, x, g, res, out_dtype, name):
    n, d = x.shape
    tr = _row_tile(n, 384)

    def body(*refs):
        if res is None:
            dy_ref, x_ref, g_ref, dx_ref, dg_ref = refs
        else:
            dy_ref, x_ref, g_ref, r_ref, dx_ref, dg_ref = refs
        i = pl.program_id(0)
        xv = x_ref[...]
        r = lax.rsqrt(jnp.mean(xv * xv, axis=-1, keepdims=True) + EPS)
        xhat = xv * r
        dyf = dy_ref[...].astype(F32)
        gdy = dyf * g_ref[...]
        dx = r * (gdy - xhat * jnp.mean(gdy * xhat, axis=-1, keepdims=True))
        if res is not None:
            dx = dx + r_ref[...]
        dx_ref[...] = dx.astype(dx_ref.dtype)

        @pl.when(i == 0)
        def _():
            dg_ref[...] = jnp.zeros_like(dg_ref)

        dg_ref[...] += jnp.sum(dyf * xhat, axis=0, keepdims=True)

    row = pl.BlockSpec((tr, d), lambda i: (i, 0))
    vec = pl.BlockSpec((1, d), lambda i: (0, 0))
    ins = [row, row, vec] + ([row] if res is not None else [])
    args = (dy, x, g) + ((res,) if res is not None else ())
    return pl.pallas_call(body, grid=(n // tr,), in_specs=ins, out_specs=(row, vec),
                          out_shape=(_sds((n, d), out_dtype), _sds((1, d), F32)), name=name,
                          compiler_params=pltpu.CompilerParams(dimension_semantics=("arbitrary",)))(*args)


def _mm(a, b, *, ta, tb, grid, a_blk, a_map, b_blk, b_map, o_blk, o_map, o_shape, o_dtype, name, acc=None):
    nk = grid[2]
    dims = (((0,) if ta else (1,), (1,) if tb else (0,)), ((), ()))

    def body(*refs):
        if acc is None:
            a_ref, b_ref, o_ref = refs[:3]
            c_ref = None
        else:
            a_ref, b_ref, c_ref, o_ref = refs[:4]
        part = lax.dot_general(a_ref[...], b_ref[...], dims, preferred_element_type=F32)
        if nk == 1:
            if c_ref is not None:
                part = part + c_ref[...].astype(F32)
            o_ref[...] = part.astype(o_ref.dtype)
            return
        acc_ref = refs[-1]
        k = pl.program_id(2)

        @pl.when(k == 0)
        def _():
            acc_ref[...] = jnp.zeros_like(acc_ref) if c_ref is None else c_ref[...].astype(F32)

        acc_ref[...] += part

        @pl.when(k == nk - 1)
        def _():
            o_ref[...] = acc_ref[...].astype(o_ref.dtype)

    ins = [pl.BlockSpec(a_blk, a_map), pl.BlockSpec(b_blk, b_map)]
    args = [a, b]
    if acc is not None:
        ins.append(pl.BlockSpec(o_blk, o_map))
        args.append(acc)
    return pl.pallas_call(body, grid=grid, in_specs=ins, out_specs=pl.BlockSpec(o_blk, o_map), out_shape=_sds(o_shape, o_dtype),
                          scratch_shapes=[pltpu.VMEM(o_blk, F32)] if nk > 1 else [], name=name,
                          compiler_params=pltpu.CompilerParams(dimension_semantics=("parallel", "parallel", "arbitrary")))(*args)


TOKEN_TILE = 1056
WIDE_TILE = 1408
DEEP_K_TOKEN_TILE = 528


def _mm_nn(x, w, o_dtype, tn, tk, name, acc=None, tm_cap=TOKEN_TILE):
    n, k = x.shape
    tm = _row_tile(n, tm_cap)
    return _mm(x, w, ta=False, tb=False, grid=(n // tm, w.shape[1] // tn, k // tk), a_blk=(tm, tk), a_map=lambda i, j, kk: (i, kk),
               b_blk=(tk, tn), b_map=lambda i, j, kk: (kk, j), o_blk=(tm, tn), o_map=lambda i, j, kk: (i, j),
               o_shape=(n, w.shape[1]), o_dtype=o_dtype, name=name, acc=acc)


def _mm_nt(dy, w, o_dtype, tn, tk, name, acc=None):
    n, k = dy.shape
    tm = _row_tile(n, TOKEN_TILE)
    return _mm(dy, w, ta=False, tb=True, grid=(n // tm, w.shape[0] // tn, k // tk), a_blk=(tm, tk), a_map=lambda i, j, kk: (i, kk),
               b_blk=(tn, tk), b_map=lambda i, j, kk: (j, kk), o_blk=(tm, tn), o_map=lambda i, j, kk: (i, j),
               o_shape=(n, w.shape[0]), o_dtype=o_dtype, name=name, acc=acc)


def _mm_tn(x, dy, tm, tn, name):
    n, m = x.shape
    tk = _row_tile(n, 2 * TOKEN_TILE)
    return _mm(x, dy, ta=True, tb=False, grid=(m // tm, dy.shape[1] // tn, n // tk), a_blk=(tk, tm), a_map=lambda i, j, kk: (kk, i),
               b_blk=(tk, tn), b_map=lambda i, j, kk: (kk, j), o_blk=(tm, tn), o_map=lambda i, j, kk: (i, j),
               o_shape=(m, dy.shape[1]), o_dtype=BF16, name=name)


def _rope_tables(n):
    pos = (jnp.arange(n) - PAD_FRONT).astype(F32)
    half = ROT_DIM // 2
    ang = pos[:, None] * (ROPE_THETA ** (-jnp.arange(half, dtype=F32) / half))[None, :]
    c, s = jnp.cos(ang), jnp.sin(ang)
    rest = ATT_HEAD_DIM - ROT_DIM
    cos_a = jnp.concatenate([c, c, jnp.ones((n, rest), F32)], axis=1)
    sin_a = jnp.concatenate([-s, s, jnp.zeros((n, rest), F32)], axis=1)
    half = RET_HEAD_DIM // 2
    ang = pos[:, None] * (RET_THETA ** (-jnp.arange(half, dtype=F32) / half))[None, :]
    c, s = jnp.cos(ang), jnp.sin(ang)
    perm = np.zeros((ATT_HEAD_DIM, ATT_HEAD_DIM), np.float32)
    for i in range(ROT_DIM):
        perm[(i + ROT_DIM // 2) % ROT_DIM, i] = 1.0
    return cos_a, sin_a, jnp.concatenate([c, c], axis=1), jnp.concatenate([-s, s], axis=1), jnp.asarray(perm, BF16)


def _rope_att(x, col0, heads, cos, sin, perm, name):
    n = x.shape[0]
    tr = _row_tile(n, 1056)
    hd = ATT_HEAD_DIM

    def body(x_ref, c_ref, s_ref, p_ref, o_ref):
        for h in range(heads):
            cs = slice(h * hd, (h + 1) * hd)
            xb = x_ref[:, cs].astype(BF16)
            sw = jnp.dot(xb, p_ref[...], preferred_element_type=F32)
            o_ref[:, cs] = (xb.astype(F32) * c_ref[...] + sw * s_ref[...]).astype(o_ref.dtype)

    tab = pl.BlockSpec((tr, hd), lambda i: (i, 0))
    return pl.pallas_call(body, grid=(n // tr,),
                          in_specs=[pl.BlockSpec((tr, heads * hd), lambda i: (i, col0)), tab, tab, pl.BlockSpec((hd, hd), lambda i: (0, 0))],
                          out_specs=pl.BlockSpec((tr, heads * hd), lambda i: (i, 0)), out_shape=_sds((n, heads * hd), BF16), name=name)(x, cos, sin, perm)


def _rope_ret(x, col0, cos, sin, name):
    p, n, _ = x.shape
    tr = _row_tile(n, 1056)
    hd = RET_HEAD_DIM

    def body(*refs):
        x_ref, o_ref = refs[0], refs[-1]
        for h in range(2):
            cs = slice(h * hd, (h + 1) * hd)
            xv = x_ref[0, :, cs].astype(F32)
            for q in range(1, p):
                xv = xv + x_ref[q, :, cs].astype(F32)
            if cos is not None:
                sw = jnp.concatenate([xv[:, hd // 2:], xv[:, :hd // 2]], axis=1)
                xv = xv * refs[1][...] + sw * refs[2][...]
            o_ref[:, cs] = xv.astype(o_ref.dtype)

    tab = pl.BlockSpec((tr, hd), lambda i, j: (i, 0))
    ins = [pl.BlockSpec((p, tr, 2 * hd), lambda i, j: (0, i, col0 + j))] + ([tab, tab] if cos is not None else [])
    args = (x,) + ((cos, sin) if cos is not None else ())
    return pl.pallas_call(body, grid=(n // tr, RET_HEADS // 2), in_specs=ins, out_specs=pl.BlockSpec((tr, 2 * hd), lambda i, j: (i, j)),
                          out_shape=_sds((n, RET_WIDTH), BF16), name=name)(*args)


def _att_mask(nblk, n_tot):
    row = lax.broadcasted_iota(jnp.int32, (BLOCK, 4 * BLOCK), 0)
    col = lax.broadcasted_iota(jnp.int32, (BLOCK, 4 * BLOCK), 1)
    qi = nblk * BLOCK + row
    seg = col // BLOCK
    cj = col % BLOCK
    kj = (nblk - 1 + seg) * BLOCK + cj
    band = (jnp.abs(qi - kj) <= WINDOW) & (kj >= PAD_FRONT) & (kj < n_tot) & (seg < 3)
    meta = (seg == 3) & (cj >= PAD_FRONT) & (jnp.abs(qi - cj) > WINDOW)
    return band | meta


def _att_specs(nb, v_col):
    kv = lambda f, cb: pl.BlockSpec((BLOCK, KV_WIDTH), lambda n: (f(n), cb))
    prev, own, nxt, first = (lambda n: jnp.maximum(n - 1, 0)), (lambda n: n), (lambda n: jnp.minimum(n + 1, nb - 1)), (lambda n: 0)
    return [kv(f, 0) for f in (prev, own, nxt, first)] + [kv(f, v_col) for f in (prev, own, nxt, first)]


def _att_probs(s, ok, snk):
    s = jnp.where(ok, s, NEG)
    m = jnp.maximum(jnp.max(s, axis=-1, keepdims=True), snk)
    p = jnp.exp(s - m)
    ps = jnp.exp(snk - m)
    inv = 1.0 / (jnp.sum(p, axis=-1, keepdims=True) + ps)
    return p * inv, ps * inv


def _att_fwd(q, k, proj, sink_b, name):
    n = q.shape[0]
    nb = n // BLOCK
    hd = ATT_HEAD_DIM

    def body(q_ref, kp, ko, kn, km, vp, vo, vn, vm, sink_ref, o_ref):
        nblk = pl.program_id(0)
        ok = _att_mask(nblk, n)
        keep = (nblk * BLOCK + lax.broadcasted_iota(jnp.int32, (BLOCK, 1), 0)) >= PAD_FRONT
        for kh in range(ATT_KV_HEADS):
            cs = slice(kh * hd, (kh + 1) * hd)
            kk = jnp.concatenate([r[:, cs] for r in (kp, ko, kn, km)], axis=0)
            vv = jnp.concatenate([r[:, cs] for r in (vp, vo, vn, vm)], axis=0)
            heads = [kh * ATT_GROUP + g for g in range(ATT_GROUP)]
            q4 = jnp.concatenate([q_ref[:, h * hd:(h + 1) * hd] for h in heads], axis=0)
            s = lax.dot_general(q4, kk, (((1,), (1,)), ((), ())), preferred_element_type=F32) * ATT_SCALE
            ps = []
            for g, h in enumerate(heads):
                p, _ = _att_probs(s[g * BLOCK:(g + 1) * BLOCK], ok, sink_ref[h:h + 1, 0:1])
                ps.append(p)
            o = jnp.dot(jnp.concatenate(ps, axis=0).astype(BF16), vv, preferred_element_type=F32)
            for g, h in enumerate(heads):
                o_ref[:, h * hd:(h + 1) * hd] = jnp.where(keep, o[g * BLOCK:(g + 1) * BLOCK], 0.0).astype(o_ref.dtype)

    qspec = pl.BlockSpec((BLOCK, ATT_WIDTH), lambda i: (i, 0))
    return pl.pallas_call(body, grid=(nb,), in_specs=[qspec] + _att_specs(nb, COL_AV256) + [pl.BlockSpec((ATT_HEADS, 128), lambda i: (0, 0))],
                          out_specs=qspec, out_shape=_sds((n, ATT_WIDTH), BF16), name=name)(q, k, k, k, k, proj, proj, proj, proj, sink_b)


def _att_bwd(q, k, proj, sink_b, dmixed, name):
    n = q.shape[0]
    nb = n // BLOCK
    hd = ATT_HEAD_DIM

    def body(q_ref, kp, ko, kn, km, vp, vo, vn, vm, sink_ref, do_ref, dq_ref, dk_ref, dv_ref, dsink_ref):
        nblk = pl.program_id(0)

        @pl.when(nblk == 0)
        def _():
            dk_ref[...] = jnp.zeros_like(dk_ref)
            dv_ref[...] = jnp.zeros_like(dv_ref)
            dsink_ref[...] = jnp.zeros_like(dsink_ref)

        ok = _att_mask(nblk, n)
        rows = [jnp.maximum(nblk - 1, 0), nblk, jnp.minimum(nblk + 1, nb - 1), 0]
        for kh in range(ATT_KV_HEADS):
            cs = slice(kh * hd, (kh + 1) * hd)
            kk = jnp.concatenate([r[:, cs] for r in (kp, ko, kn, km)], axis=0)
            vv = jnp.concatenate([r[:, cs] for r in (vp, vo, vn, vm)], axis=0)
            heads = [kh * ATT_GROUP + g for g in range(ATT_GROUP)]
            q4 = jnp.concatenate([q_ref[:, h * hd:(h + 1) * hd] for h in heads], axis=0)
            do4 = jnp.concatenate([do_ref[:, h * hd:(h + 1) * hd] for h in heads], axis=0)
            s = lax.dot_general(q4, kk, (((1,), (1,)), ((), ())), preferred_element_type=F32) * ATT_SCALE
            dp = lax.dot_general(do4, vv, (((1,), (1,)), ((), ())), preferred_element_type=F32)
            ps, dss = [], []
            for g, h in enumerate(heads):
                p, psink = _att_probs(s[g * BLOCK:(g + 1) * BLOCK], ok, sink_ref[h:h + 1, 0:1])
                dpg = dp[g * BLOCK:(g + 1) * BLOCK]
                delta = jnp.sum(p * dpg, axis=-1, keepdims=True)
                ps.append(p)
                dss.append(p * (dpg - delta) * ATT_SCALE)
                dsink_ref[h:h + 1, :] = dsink_ref[h:h + 1, :] - jnp.sum(psink * delta, axis=0, keepdims=True)
            ds = jnp.concatenate(dss, axis=0).astype(BF16)
            pb = jnp.concatenate(ps, axis=0).astype(BF16)
            dq = jnp.dot(ds, kk, preferred_element_type=F32)
            for g, h in enumerate(heads):
                dq_ref[:, h * hd:(h + 1) * hd] = dq[g * BLOCK:(g + 1) * BLOCK].astype(dq_ref.dtype)
            dk = lax.dot_general(ds, q4, (((0,), (0,)), ((), ())), preferred_element_type=F32)
            dv = lax.dot_general(pb, do4, (((0,), (0,)), ((), ())), preferred_element_type=F32)
            for seg, r in enumerate(rows):
                at = (pl.ds(pl.multiple_of(r * BLOCK, BLOCK), BLOCK), cs)
                dk_ref[at] += dk[seg * BLOCK:(seg + 1) * BLOCK]
                dv_ref[at] += dv[seg * BLOCK:(seg + 1) * BLOCK]

    qspec = pl.BlockSpec((BLOCK, ATT_WIDTH), lambda i: (i, 0))
    whole = pl.BlockSpec((n, KV_WIDTH), lambda i: (0, 0))
    sinks = pl.BlockSpec((ATT_HEADS, 128), lambda i: (0, 0))
    return pl.pallas_call(body, grid=(nb,), in_specs=[qspec] + _att_specs(nb, COL_AV256) + [sinks, qspec], out_specs=(qspec, whole, whole, sinks),
                          out_shape=(_sds((n, ATT_WIDTH), BF16), _sds((n, KV_WIDTH), F32), _sds((n, KV_WIDTH), F32), _sds((ATT_HEADS, 128), F32)),
                          name=name, compiler_params=pltpu.CompilerParams(dimension_semantics=("arbitrary",)))(
                              q, k, k, k, k, proj, proj, proj, proj, sink_b, dmixed)


def _ret_decay(lg, d):
    a = lax.broadcasted_iota(jnp.int32, (BLOCK, 1), 0)
    b = lax.broadcasted_iota(jnp.int32, (1, BLOCK), 1)
    t_col = a + d * (BLOCK - 1 - 2 * a)
    t_row = b + d * (BLOCK - 1 - 2 * b)
    diff = t_col - t_row
    dist = jnp.maximum(diff, 0).astype(F32)
    dmask = jnp.where(diff >= d, jnp.exp(lg * dist), 0.0)
    tf = t_col.astype(F32)
    xi = jnp.exp(lg * (tf + 1.0))
    zeta = jnp.exp(lg * (BLOCK - 1.0 - tf))
    gam = jnp.exp(jnp.full((1, 1), BLOCK, F32) * lg)
    return dmask, dist, xi, zeta, gam, tf


def _ret_fwd(q, k, proj, lg, name):
    n = q.shape[0]
    nc = n // BLOCK
    hd = RET_HEAD_DIM
    chunk = lambda d, c: c + d * (nc - 1 - 2 * c)

    def body(lg_ref, q_ref, k_ref, v0, v1, v2, v3, o_ref, st_ref, s_ref):
        d, c = pl.program_id(0), pl.program_id(1)

        @pl.when(c == 0)
        def _():
            s_ref[...] = jnp.zeros_like(s_ref)

        for h, v_ref in enumerate((v0, v1, v2, v3)):
            cs = slice(h * hd, (h + 1) * hd)
            dmask, _, xi, zeta, gam, _ = _ret_decay(lg_ref[h, d], d)
            qv = q_ref[:, cs]
            kf = k_ref[:, cs].astype(F32) * RET_K_SCALE
            vv = v_ref[...]
            s = lax.dot_general(qv, kf.astype(BF16), (((1,), (1,)), ((), ())), preferred_element_type=F32)
            sb = s_ref[h]
            o_ref[:, cs] = (jnp.dot((s * dmask).astype(BF16), vv, preferred_element_type=F32)
                            + jnp.dot((qv.astype(F32) * xi).astype(BF16), sb.astype(BF16), preferred_element_type=F32))
            st_ref[h] = sb
            s_ref[h] = gam * sb + lax.dot_general((kf * zeta).astype(BF16), vv, (((0,), (0,)), ((), ())), preferred_element_type=F32)

    wide = pl.BlockSpec((BLOCK, RET_WIDTH), lambda d, c: (chunk(d, c), 0))
    vblk = lambda h: pl.BlockSpec((BLOCK, hd), lambda d, c: (chunk(d, c), COL_RV + h))
    return pl.pallas_call(
        body, grid=(2, nc), in_specs=[pl.BlockSpec(memory_space=pltpu.SMEM), wide, wide] + [vblk(h) for h in range(RET_HEADS)],
        out_specs=(pl.BlockSpec((None, BLOCK, RET_WIDTH), lambda d, c: (d, chunk(d, c), 0)),
                   pl.BlockSpec((RET_HEADS, None, None, hd, hd), lambda d, c: (0, d, c, 0, 0))),
        out_shape=(_sds((2, n, RET_WIDTH), F32), _sds((RET_HEADS, 2, nc, hd, hd), F32)), scratch_shapes=[pltpu.VMEM((RET_HEADS, hd, hd), F32)],
        name=name, compiler_params=pltpu.CompilerParams(dimension_semantics=("parallel", "arbitrary")))(lg, q, k, proj, proj, proj, proj)


def _ret_bwd(q, k, proj, lg, do, states, name):
    n = q.shape[0]
    nc = n // BLOCK
    hd = RET_HEAD_DIM
    chunk = lambda d, r: (nc - 1 - r) + d * (2 * r - (nc - 1))

    def body(lg_ref, q_ref, k_ref, v0, v1, v2, v3, do_ref, st_ref, dq_ref, dk_ref, dv_ref, dlg_ref, ds_ref):
        d, r = pl.program_id(0), pl.program_id(1)

        @pl.when(r == 0)
        def _():
            ds_ref[...] = jnp.zeros_like(ds_ref)
            dlg_ref[...] = jnp.zeros_like(dlg_ref)

        row = lax.broadcasted_iota(jnp.int32, (BLOCK, 1), 0) + chunk(d, r) * BLOCK
        keep = row >= PAD_FRONT
        nt = (((1,), (1,)), ((), ()))
        tn = (((0,), (0,)), ((), ()))
        for h, v_ref in enumerate((v0, v1, v2, v3)):
            cs = slice(h * hd, (h + 1) * hd)
            dmask, dist, xi, zeta, gam, tf = _ret_decay(lg_ref[h, d], d)
            qv, vv, dov = q_ref[:, cs], v_ref[...], do_ref[:, cs]
            qf = qv.astype(F32)
            kf = k_ref[:, cs].astype(F32) * RET_K_SCALE
            kb = kf.astype(BF16)
            sc = st_ref[h]
            dsn = ds_ref[h]
            s = lax.dot_general(qv, kb, nt, preferred_element_type=F32)
            dsc = lax.dot_general(dov, vv, nt, preferred_element_type=F32) * dmask
            dsb = dsc.astype(BF16)
            dq_c = xi * lax.dot_general(dov, sc.astype(BF16), nt, preferred_element_type=F32)
            dk_c = zeta * lax.dot_general(vv, dsn.astype(BF16), nt, preferred_element_type=F32)
            dq = jnp.dot(dsb, kb, preferred_element_type=F32) + dq_c
            dk = lax.dot_general(dsb, qv, tn, preferred_element_type=F32) + dk_c
            dv = (lax.dot_general((s * dmask).astype(BF16), dov, tn, preferred_element_type=F32)
                  + jnp.dot((kf * zeta).astype(BF16), dsn.astype(BF16), preferred_element_type=F32))
            ds_ref[h] = gam * dsn + lax.dot_general((qf * xi).astype(BF16), dov, tn, preferred_element_type=F32)
            dlg = (jnp.sum(dsc * s * dist, keepdims=True)
                   + jnp.sum((tf + 1.0) * jnp.sum(qf * dq_c, axis=-1, keepdims=True), keepdims=True)
                   + jnp.sum((BLOCK - 1.0 - tf) * jnp.sum(kf * dk_c, axis=-1, keepdims=True), keepdims=True)
                   + BLOCK * gam * jnp.sum(dsn * sc, keepdims=True))
            dlg_ref[h] += dlg
            dq_ref[:, cs] = dq
            dk_ref[:, cs] = jnp.where(keep, dk * RET_K_SCALE, 0.0)
            dv_ref[:, cs] = jnp.where(keep, dv, 0.0)

    wide = pl.BlockSpec((BLOCK, RET_WIDTH), lambda d, r: (chunk(d, r), 0))
    vblk = lambda h: pl.BlockSpec((BLOCK, hd), lambda d, r: (chunk(d, r), COL_RV + h))
    plane = pl.BlockSpec((None, BLOCK, RET_WIDTH), lambda d, r: (d, chunk(d, r), 0))
    return pl.pallas_call(
        body, grid=(2, nc),
        in_specs=[pl.BlockSpec(memory_space=pltpu.SMEM), wide, wide] + [vblk(h) for h in range(RET_HEADS)]
        + [wide, pl.BlockSpec((RET_HEADS, None, None, hd, hd), lambda d, r: (0, d, nc - 1 - r, 0, 0))],
        out_specs=(plane, plane, plane, pl.BlockSpec((RET_HEADS, None, 8, 128), lambda d, r: (0, d, 0, 0))),
        out_shape=(_sds((2, n, RET_WIDTH), F32),) * 3 + (_sds((RET_HEADS, 2, 8, 128), F32),),
        scratch_shapes=[pltpu.VMEM((RET_HEADS, hd, hd), F32)], name=name,
        compiler_params=pltpu.CompilerParams(dimension_semantics=("parallel", "arbitrary")))(lg, q, k, proj, proj, proj, proj, do, states)


def _retgate_fwd(o, proj, gain, name):
    _, n, _ = o.shape
    tr = _row_tile(n, 1056)
    hd = RET_HEAD_DIM

    def body(o_ref, rg_ref, g_ref, y_ref):
        ov = o_ref[0] + o_ref[1]
        r = lax.rsqrt(jnp.mean(ov * ov, axis=-1, keepdims=True) + EPS)
        y_ref[...] = (_silu(rg_ref[...].astype(F32)) * (ov * r * g_ref[...])).astype(y_ref.dtype)

    return pl.pallas_call(body, grid=(n // tr, RET_HEADS),
                          in_specs=[pl.BlockSpec((2, tr, hd), lambda i, h: (0, i, h)), pl.BlockSpec((tr, hd), lambda i, h: (i, COL_RG + h)),
                                    pl.BlockSpec((1, hd), lambda i, h: (0, h))],
                          out_specs=pl.BlockSpec((tr, hd), lambda i, h: (i, h)), out_shape=_sds((n, RET_WIDTH), BF16), name=name)(o, proj, gain)


def _retgate_bwd(dmixed, o, proj, gain, name):
    _, n, _ = o.shape
    tr = _row_tile(n, 1056)
    hd = RET_HEAD_DIM

    def body(dy_ref, o_ref, rg_ref, g_ref, do_ref, drg_ref, dg_ref):
        i = pl.program_id(1)
        ov = o_ref[0] + o_ref[1]
        r = lax.rsqrt(jnp.mean(ov * ov, axis=-1, keepdims=True) + EPS)
        xhat = ov * r
        rg = rg_ref[...].astype(F32)
        dy = dy_ref[...].astype(F32)
        drg_ref[...] = (dy * (xhat * g_ref[...]) * _dsilu(rg)).astype(drg_ref.dtype)
        dn = dy * _silu(rg)
        dxh = dn * g_ref[...]
        do_ref[...] = (r * (dxh - xhat * jnp.mean(dxh * xhat, axis=-1, keepdims=True))).astype(do_ref.dtype)

        @pl.when(i == 0)
        def _():
            dg_ref[...] = jnp.zeros_like(dg_ref)

        dg_ref[...] += jnp.sum(dn * xhat, axis=0, keepdims=True)

    tile = pl.BlockSpec((tr, hd), lambda h, i: (i, h))
    vec = pl.BlockSpec((1, hd), lambda h, i: (0, h))
    return pl.pallas_call(body, grid=(RET_HEADS, n // tr),
                          in_specs=[pl.BlockSpec((tr, hd), lambda h, i: (i, RET_HEADS + h)), pl.BlockSpec((2, tr, hd), lambda h, i: (0, i, h)),
                                    pl.BlockSpec((tr, hd), lambda h, i: (i, COL_RG + h)), vec],
                          out_specs=(tile, tile, vec), out_shape=(_sds((n, RET_WIDTH), BF16), _sds((n, RET_WIDTH), BF16), _sds((1, RET_WIDTH), F32)),
                          name=name, compiler_params=pltpu.CompilerParams(dimension_semantics=("parallel", "arbitrary")))(dmixed, o, proj, gain)


FFN_TILE = 512


def _swiglu_fwd(x, wg_t, wu_t, name):
    n, k = x.shape
    tm = _row_tile(n, TOKEN_TILE)
    nt = (((1,), (1,)), ((), ()))

    def body(x_ref, g_ref, u_ref, go_ref, uo_ref, f_ref):
        g = lax.dot_general(x_ref[...], g_ref[...], nt, preferred_element_type=F32)
        u = lax.dot_general(x_ref[...], u_ref[...], nt, preferred_element_type=F32)
        go_ref[...] = g.astype(go_ref.dtype)
        uo_ref[...] = u.astype(uo_ref.dtype)
        f_ref[...] = (_silu(g) * u).astype(f_ref.dtype)

    w = pl.BlockSpec((FFN_TILE, k), lambda i, j: (j, 0))
    o = pl.BlockSpec((tm, FFN_TILE), lambda i, j: (i, j))
    return pl.pallas_call(body, grid=(n // tm, wg_t.shape[0] // FFN_TILE), in_specs=[pl.BlockSpec((tm, k), lambda i, j: (i, 0)), w, w],
                          out_specs=(o, o, o), out_shape=(_sds((n, wg_t.shape[0]), BF16),) * 3, name=name,
                          compiler_params=pltpu.CompilerParams(dimension_semantics=("parallel", "parallel")))(x, wg_t, wu_t)


def _swiglu_bwd(dy, wd, gate, up, name):
    n, k = dy.shape
    tm = _row_tile(n, TOKEN_TILE)
    nt = (((1,), (1,)), ((), ()))

    def body(dy_ref, w_ref, g_ref, u_ref, dg_ref, du_ref):
        df = lax.dot_general(dy_ref[...], w_ref[...], nt, preferred_element_type=F32)
        g = g_ref[...].astype(F32)
        dg_ref[...] = (df * u_ref[...].astype(F32) * _dsilu(g)).astype(dg_ref.dtype)
        du_ref[...] = (df * _silu(g)).astype(du_ref.dtype)

    o = pl.BlockSpec((tm, FFN_TILE), lambda i, j: (i, j))
    return pl.pallas_call(body, grid=(n // tm, wd.shape[0] // FFN_TILE),
                          in_specs=[pl.BlockSpec((tm, k), lambda i, j: (i, 0)), pl.BlockSpec((FFN_TILE, k), lambda i, j: (j, 0)), o, o],
                          out_specs=(o, o), out_shape=(_sds((n, wd.shape[0]), BF16),) * 2, name=name,
                          compiler_params=pltpu.CompilerParams(dimension_semantics=("parallel", "parallel")))(dy, wd, gate, up)


def _loss_head(h, target, name):
    n, d = h.shape
    nb = n // BLOCK

    def body(h_ref, t_ref, dh_ref, l_ref):
        i = pl.program_id(0)

        @pl.when(i == 0)
        def _():
            l_ref[...] = jnp.zeros_like(l_ref)
            dh_ref[...] = jnp.zeros_like(dh_ref)

        @pl.when(i > 0)
        def _():
            e = h_ref[...] - t_ref[...]
            dh_ref[...] = e * (1.0 / d)
            l_ref[...] += 0.5 * jnp.sum(jnp.mean(e * e, axis=-1, keepdims=True), keepdims=True)

    blk = pl.BlockSpec((BLOCK, d), lambda i: (i, 0))
    return pl.pallas_call(body, grid=(nb,), in_specs=[blk, pl.BlockSpec((BLOCK, d), lambda i: (jnp.maximum(i - 1, 0), 0))],
                          out_specs=(blk, pl.BlockSpec((8, 128), lambda i: (0, 0))), out_shape=(_sds((n, d), F32), _sds((8, 128), F32)), name=name,
                          compiler_params=pltpu.CompilerParams(dimension_semantics=("arbitrary",)))(h, target)


def _adamw(parts, w, m, v, name, sel=None, layer=None, prev=None):
    s, (r, c) = parts.shape[0], parts.shape[-2:]
    tr = _row_tile(r, max(16, (ADAMW_TILE_ELEMS // c) // 16 * 16))
    b1c, b2c = 1.0 - ADAM_B1 ** ADAM_STEP, 1.0 - ADAM_B2 ** ADAM_STEP

    def body(p_ref, w_ref, m_ref, v_ref, *rest):
        g_ref, d_ref, mo_ref, vo_ref = rest[-4:]
        g = p_ref[0].astype(F32)
        for q in range(1, s):
            g = g + p_ref[q].astype(F32)
        mn = ADAM_B1 * m_ref[...] + (1.0 - ADAM_B1) * g
        vn = ADAM_B2 * v_ref[...] + (1.0 - ADAM_B2) * jnp.square(g)
        g_ref[...] = g
        mo_ref[...] = mn
        vo_ref[...] = vn
        d_ref[...] = -ADAM_LR * ((mn / b1c) / (jnp.sqrt(vn / b2c) + ADAM_EPS) + ADAM_WD * w_ref[...])

    pspec = (pl.BlockSpec((s, tr, c), lambda i: (0, i, 0)) if sel is None else pl.BlockSpec((s, None, tr, c), lambda i: (0, sel, i, 0)))
    if layer is None:
        t = pl.BlockSpec((tr, c), lambda i: (i, 0))
        return pl.pallas_call(body, grid=(r // tr,), in_specs=[pspec, t, t, t], out_specs=(t, t, t, t), out_shape=(_sds((r, c), F32),) * 4,
                              name=name)(parts, w, m, v)
    t = pl.BlockSpec((None, tr, c), lambda i: (layer, i, 0))
    prev = prev if prev is not None else tuple(lax.empty(w.shape, F32) for _ in range(4))
    return pl.pallas_call(body, grid=(r // tr,), in_specs=[pspec, t, t, t] + [ANY] * 4, out_specs=(t, t, t, t), out_shape=(_sds(w.shape, F32),) * 4,
                          input_output_aliases={4 + i: i for i in range(4)}, name=name)(parts, w, m, v, *prev)


def _allgather(xs, name, after=None):
    na = len(xs)
    first_out = na + (after is not None)

    def body(*refs):
        x_refs, o_refs = refs[:na], refs[first_out:first_out + na]
        send, recv, lsem = refs[first_out + na:]
        x, y, c = lax.axis_index("x"), lax.axis_index("y"), lax.axis_index("c")
        me, sib = (x, y, c), (x, y, 1 - c)
        chips = [(1 - x, y), (x, 1 - y), (1 - x, 1 - y)]
        slot = lambda p: 4 * p[0] + 2 * p[1] + p[2]

        def copy(a, k, block, to, src=None):
            dst = o_refs[a].at[slot(block)]
            return pltpu.make_async_remote_copy(src_ref=dst if src is None else src, dst_ref=dst, send_sem=send.at[a, k], recv_sem=recv.at[a, k],
                                                device_id=to, device_id_type=MESH)

        mine = [pltpu.make_async_copy(x_refs[a], o_refs[a].at[slot(me)], lsem.at[a]) for a in range(na)]
        for cp in mine:
            cp.start()
        first = []
        for a in range(na):
            first.append(copy(a, 0, me, sib, src=x_refs[a]))
            first += [copy(a, 1 + j, me, (*chip, c), src=x_refs[a]) for j, chip in enumerate(chips)]
        for cp in first:
            cp.start()
        passed = []
        for j, chip in enumerate(chips):
            for a in range(na):
                copy(a, 1 + j, (*chip, c), me).wait_recv()
                passed.append(copy(a, 4 + j, (*chip, c), sib))
                passed[-1].start()
        for a in range(na):
            copy(a, 0, sib, me).wait_recv()
            for j, chip in enumerate(chips):
                copy(a, 4 + j, (*chip, 1 - c), me).wait_recv()
        for cp in first + passed:
            cp.wait_send()
        for cp in mine:
            cp.wait()

    extra = [] if after is None else [after]
    return pl.pallas_call(body, in_specs=[ANY] * (na + len(extra)), out_specs=[ANY] * na,
                          out_shape=[_sds((N_DEV,) + t.shape, t.dtype) for t in xs],
                          scratch_shapes=[pltpu.SemaphoreType.DMA((na, 7)), pltpu.SemaphoreType.DMA((na, 7)), pltpu.SemaphoreType.DMA((na,))],
                          name=name)(*xs, *extra)


HBM = pl.BlockSpec(memory_space=pltpu.HBM)
SEM = pl.BlockSpec(memory_space=pltpu.SEMAPHORE)
EFFECT = pltpu.SideEffectType.DATAFLOW_SIDE_EFFECTING


SPLIT_RELATIONS = dict(gather=(1, 2, 4, 6),
                       forward=(2, 4, 6),
                       scatter=tuple(range(1, N_DEV)))


def _split_copies(mode, x_refs, land_refs, send, recv, own, landing):
    x, y, c = lax.axis_index("x"), lax.axis_index("y"), lax.axis_index("c")
    flip = lambda r: ((1 - x if r & 4 else x), (1 - y if r & 2 else y), (1 - c if r & 1 else c))
    slot = lambda p: 4 * p[0] + 2 * p[1] + p[2]
    me = slot((x, y, c))
    rel = SPLIT_RELATIONS[mode]
    local, remote = [], []
    for a in range(len(land_refs)):
        if mode != "forward":
            local.append(pltpu.make_async_copy(x_refs[a].at[me] if mode == "scatter" else x_refs[a], land_refs[a].at[me], own.at[a]))
        for j, r in enumerate(rel):
            if mode == "forward":
                to = flip(1)
                src = dst = land_refs[a].at[slot(flip(r ^ 1) if landing else flip(r))]
            else:
                to = flip(r)
                src = x_refs[a].at[slot(to)] if mode == "scatter" else x_refs[a]
                dst = land_refs[a].at[slot(to) if landing else me]
            remote.append(pltpu.make_async_remote_copy(src_ref=src, dst_ref=dst, send_sem=send.at[len(rel) * a + j],
                                                       recv_sem=recv.at[len(rel) * a + j], device_id=to, device_id_type=MESH))
    return local, remote


def _send_start(mode, xs, lands, after, name):
    if lands is None:
        lands = [lax.empty(t.shape if mode == "scatter" else (N_DEV,) + t.shape, t.dtype) for t in xs]
    nx, na, nr = len(xs), len(lands), len(SPLIT_RELATIONS[mode])
    nsem = 2 if mode == "forward" else 3

    def body(*refs):
        x_refs, land_refs = refs[:nx], refs[nx:nx + na]
        sems = refs[nx + na + 1:nx + na + 1 + nsem]
        local, remote = _split_copies(mode, x_refs, land_refs, sems[0], sems[1], sems[2] if nsem == 3 else None, False)
        for cp in remote + local:
            cp.start()
        refs[-1][...] = jnp.zeros_like(refs[-1])

    hbm = lambda t: pltpu.with_memory_space_constraint(t, pltpu.HBM)
    sem_shapes = [pltpu.SemaphoreType.DMA((nr * na,)), pltpu.SemaphoreType.DMA((nr * na,)), pltpu.SemaphoreType.DMA((na,))][:nsem]
    outs = pl.pallas_call(
        body, name=name,
        out_shape=(*sem_shapes, *[pltpu.HBM(t.shape, t.dtype) for t in list(xs) + list(lands)], _sds((8, 128), F32)),
        in_specs=[HBM] * (nx + na) + [ANY], out_specs=(*[SEM] * nsem, *[HBM] * (nx + na), pl.BlockSpec(memory_space=pltpu.VMEM)),
        input_output_aliases={i: nsem + i for i in range(nx + na)},
        compiler_params=pltpu.CompilerParams(has_side_effects=EFFECT))(*[hbm(t) for t in list(xs) + list(lands)], after)
    return outs[:nsem], list(outs[nsem:nsem + nx]), list(outs[nsem + nx:nsem + nx + na]), outs[-1]


def _send_wait(mode, started, after, name):
    sems, xs, lands, _ = started
    nx, na, nsem = len(xs), len(lands), len(sems)

    def body(*refs):
        s = refs[nx + na:nx + na + nsem]
        local, remote = _split_copies(mode, refs[:nx], refs[nx:nx + na], s[0], s[1], s[2] if nsem == 3 else None, True)
        for cp in remote:
            cp.wait_send()
            cp.wait_recv()
        for cp in local:
            cp.wait()

    outs = pl.pallas_call(body, name=name, out_shape=tuple(pltpu.HBM(t.shape, t.dtype) for t in xs + lands),
                          in_specs=[HBM] * (nx + na) + [SEM] * nsem + [ANY], out_specs=[HBM] * (nx + na),
                          input_output_aliases={i: i for i in range(nx + na)},
                          compiler_params=pltpu.CompilerParams(has_side_effects=EFFECT))(*xs, *lands, *sems, after)
    return list(outs[nx:])


def _local_step(x, meta, target, mix_weights_fn, ffn_weights_fn, grads_fn, sink, dec_f, dec_b, ret_norm, n_mix_pre, n_mix_post, n_ffn_pre,
                n_ffn_post):
    depth = n_mix_pre.shape[0]
    d = D_MODEL
    h = jnp.concatenate([jnp.zeros((PAD_FRONT, d), F32), meta, x], axis=0)
    n = h.shape[0]
    cos_a, sin_a, cos_r, sin_r, perm = _rope_tables(n)
    lg_all = jnp.stack([-jnp.exp(dec_f), -jnp.exp(dec_b)], axis=-1)
    saved = []
    for l in range(depth):
        t = f"l{l}_"
        sink_b = jnp.broadcast_to(sink[l][:, None], (ATT_HEADS, 128))
        wi, wo, tok = mix_weights_fn(l, h)
        u = _norm_fwd(h, (n_mix_pre[l] + tok)[None], None, BF16, t + "norm_mix_pre")
        proj = _mm_nt(u, wi, BF16, WIDE_TILE, d, t + "proj")
        aq = _rope_att(proj, 0, ATT_HEADS, cos_a, sin_a, perm, t + "rope_aq")
        ak = _rope_att(proj, ATT_WIDTH // KV_WIDTH, ATT_KV_HEADS, cos_a, sin_a, perm, t + "rope_ak")
        att = _att_fwd(aq, ak, proj, sink_b, t + "att")
        proj3 = proj[None]
        rq = _rope_ret(proj3, COL_RQ // 2, cos_r, sin_r, t + "rope_rq")
        rk = _rope_ret(proj3, COL_RK // 2, cos_r, sin_r, t + "rope_rk")
        o_ret, states = _ret_fwd(rq, rk, proj, lg_all[l], t + "ret")
        retg = _retgate_fwd(o_ret, proj, ret_norm[l][None], t + "retgate")
        mixed = jnp.concatenate([att, retg], axis=1)
        mo = _mm_nn(mixed, wo, F32, 1024, d, t + "out_proj")
        h_mid = _norm_fwd(mo, n_mix_post[l][None], h, F32, t + "norm_mix_post")
        wg, wu, wd, tok = ffn_weights_fn(l, h_mid)
        u2 = _norm_fwd(h_mid, (n_ffn_pre[l] + tok)[None], None, BF16, t + "norm_ffn_pre")
        gate, up, f = _swiglu_fwd(u2, wg, wu, t + "gate_up")
        dn = _mm_nn(f, wd, F32, 512, D_FF, t + "down", tm_cap=DEEP_K_TOKEN_TILE)
        h_out = _norm_fwd(dn, n_ffn_post[l][None], h_mid, F32, t + "norm_ffn_post")
        saved.append(dict(h=h, u=u, proj=proj, aq=aq, ak=ak, rq=rq, rk=rk, o_ret=o_ret, states=states, mixed=mixed, mo=mo, h_mid=h_mid, u2=u2,
                          gate=gate, up=up, f=f, dn=dn, sink_b=sink_b, wi=wi, wo=wo, wg=wg, wu=wu, wd=wd))
        h = h_out

    dh, loss_part = _loss_head(h, target, "loss_head")
    gs = dict(sink=[None] * depth, dec_f=[None] * depth, dec_b=[None] * depth, ret_norm=[None] * depth, mix_pre=[None] * depth,
              mix_post=[None] * depth, ffn_pre=[None] * depth, ffn_post=[None] * depth)
    tok_b = jnp.zeros((), F32)
    for l in reversed(range(depth)):
        t = f"l{l}_b_"
        sv = saved[l]
        proj = sv["proj"]
        d_dn, gs["ffn_post"][l] = _norm_bwd(dh, sv["dn"], (n_ffn_post[l] + tok_b)[None], None, BF16, t + "norm_ffn_post")
        gw = {}
        d_gate, d_up = _swiglu_bwd(d_dn, sv["wd"], sv["gate"], sv["up"], t + "d_gate_up")
        gw["wd"] = _mm_tn(sv["f"], d_dn, WIDE_TILE, 1024, t + "dw_down")
        du2 = _mm_nn(d_gate, sv["wg"], F32, 512, D_FF, t + "du2_gate", tm_cap=DEEP_K_TOKEN_TILE)
        du2 = _mm_nn(d_up, sv["wu"], F32, 512, D_FF, t + "du2_up", acc=du2, tm_cap=DEEP_K_TOKEN_TILE)
        gw["wg"] = _mm_tn(d_gate, sv["u2"], WIDE_TILE, 1024, t + "dw_gate")
        gw["wu"] = _mm_tn(d_up, sv["u2"], WIDE_TILE, 1024, t + "dw_up")
        tok_b = grads_fn(l, "ffn", gw, du2)
        dh, gs["ffn_pre"][l] = _norm_bwd(du2, sv["h_mid"], (n_ffn_pre[l] + tok_b)[None], dh, F32, t + "norm_ffn_pre")
        d_mo, gs["mix_post"][l] = _norm_bwd(dh, sv["mo"], n_mix_post[l][None], None, BF16, t + "norm_mix_post")
        d_mixed = _mm_nt(d_mo, sv["wo"], BF16, 1024, d, t + "d_mixed")
        gw["wo"] = _mm_tn(sv["mixed"], d_mo, 1024, 1024, t + "dw_out")
        d_o, d_rg, gs["ret_norm"][l] = _retgate_bwd(d_mixed, sv["o_ret"], proj, ret_norm[l][None], t + "retgate")
        dq_r, dk_r, dv_r, dlg = _ret_bwd(sv["rq"], sv["rk"], proj, lg_all[l], d_o, sv["states"], t + "ret")
        draw = dlg[:, :, 0, 0] * lg_all[l]
        gs["dec_f"][l], gs["dec_b"][l] = draw[:, 0], draw[:, 1]
        dq_a, dk_a, dv_a, dsink = _att_bwd(sv["aq"], sv["ak"], proj, sv["sink_b"], d_mixed, t + "att")
        gs["sink"][l] = dsink[:, 0]
        dproj = jnp.concatenate([
            _rope_att(dq_a, 0, ATT_HEADS, cos_a, -sin_a, perm, t + "rope_aq"),
            _rope_att(dk_a, 0, ATT_KV_HEADS, cos_a, -sin_a, perm, t + "rope_ak"),
            dv_a.astype(BF16),
            _rope_ret(dq_r, 0, cos_r, -sin_r, t + "rope_rq"),
            _rope_ret(dk_r, 0, cos_r, -sin_r, t + "rope_rk"),
            _rope_ret(dv_r, 0, None, None, t + "sum_rv"),
            d_rg], axis=1)
        gw["wi"] = _mm_tn(dproj, sv["u"], WIDE_TILE, 1024, t + "dw_in")
        tok_b = grads_fn(l, "mix", gw, dproj)
        du = _mm_nn(dproj, sv["wi"], F32, 512, IN_COLS, t + "du", tm_cap=DEEP_K_TOKEN_TILE)
        dh, gs["mix_pre"][l] = _norm_bwd(du, sv["h"], (n_mix_pre[l] + tok_b)[None], dh, F32, t + "norm_mix_pre")
    return loss_part[0, 0], dh, gs


def _pack_small(mix_pre, mix_post, ffn_pre, ffn_post, ret_norm, sink, dec_f, dec_b, loss, meta):
    d = D_MODEL

    def tile(a, rows=8):
        a = jnp.reshape(a, (-1, a.shape[-1])) if a.ndim else jnp.reshape(a, (1, 1))
        return jnp.pad(a, ((0, rows - a.shape[0]), (0, d - a.shape[1])))

    return jnp.concatenate([tile(mix_pre), tile(mix_post), tile(ffn_pre), tile(ffn_post), tile(ret_norm.reshape(-1, d)), tile(sink), tile(dec_f),
                            tile(dec_b), tile(loss), tile(meta, SMALL_ROWS - ROW_META)], axis=0)


def _unpack_small(p, depth):
    rows = lambda r0, cols: p[r0:r0 + depth, :cols]
    return dict(mix_pre=rows(ROW_MIX_PRE, D_MODEL), mix_post=rows(ROW_MIX_POST, D_MODEL), ffn_pre=rows(ROW_FFN_PRE, D_MODEL),
                ffn_post=rows(ROW_FFN_POST, D_MODEL), ret_norm=p[ROW_RET_NORM:ROW_RET_NORM + depth * RET_WIDTH // D_MODEL].reshape(depth, RET_WIDTH),
                sink=rows(ROW_SINK, ATT_HEADS), dec_f=rows(ROW_DEC_F, RET_HEADS), dec_b=rows(ROW_DEC_B, RET_HEADS), loss=p[ROW_LOSS, 0])


def kernel(x, meta_tokens, w_in, w_out, attn_sink, ret_decay_fwd, ret_decay_bwd, ret_norm, norm_mix_pre, norm_mix_post, w_gate, w_up, w_down, norm_ffn_pre, norm_ffn_post, loss_target, m_meta_tokens, m_w_in, m_w_out, m_attn_sink, m_ret_decay_fwd, m_ret_decay_bwd, m_ret_norm, m_norm_mix_pre, m_norm_mix_post, m_w_gate, m_w_up, m_w_down, m_norm_ffn_pre, m_norm_ffn_post, v_meta_tokens, v_w_in, v_w_out, v_attn_sink, v_ret_decay_fwd, v_ret_decay_bwd, v_ret_norm, v_norm_mix_pre, v_norm_mix_post, v_w_gate, v_w_up, v_w_down, v_norm_ffn_pre, v_norm_ffn_post):
    depth, d = w_in.shape[0], D_MODEL
    me = 4 * lax.axis_index("x") + 2 * lax.axis_index("y") + lax.axis_index("c")
    zero = jnp.zeros((), F32)

    meta_g, = _allgather([meta_tokens], "gather_meta")
    meta = meta_g.transpose(1, 0, 2).reshape(N_META, d)

    def shards(k):
        l = k // 2
        if k % 2 == 0:
            return [w_in[l].T.astype(BF16), w_out[l].astype(BF16)]
        return [w_gate[l].T.astype(BF16), w_up[l].T.astype(BF16), w_down[l].astype(BF16)]

    gathers, ahead = {}, 2
    for k in range(min(ahead + 1, 2 * depth)):
        gathers[k] = _send_start("gather", shards(k), None, gathers[k - 1][3] if k else meta_g, f"gather_start_g{k}")

    passing = {}

    def pass_on(k, after):
        lands = _send_wait("gather", gathers.pop(k), after, f"gather_wait_g{k}")
        passing[k] = _send_start("forward", [], lands, after, f"forward_start_g{k}")

    def take(k, h):
        after = h
        if k >= 1 and k + ahead < 2 * depth:
            gathers[k + ahead] = _send_start("gather", shards(k + ahead), None, h, f"gather_start_g{k + ahead}")
            after = gathers[k + ahead][3]
        elif k == 0:
            after = gathers[max(gathers)][3]
        if k not in passing:
            pass_on(k, after)
        if k >= 1 and k + 1 < 2 * depth:
            pass_on(k + 1, after)
        last = passing[k + 1][3] if k + 1 in passing else passing[k][3]
        return _send_wait("forward", passing.pop(k), last, f"forward_wait_g{k}")

    def mix_weights_fn(l, h):
        wi_t, wo = take(2 * l, h)
        return wi_t.reshape(IN_COLS, d), wo.reshape(d, d), zero

    def ffn_weights_fn(l, h):
        wg_t, wu_t, wd = take(2 * l + 1, h)
        return wg_t.reshape(D_FF, d), wu_t.reshape(D_FF, d), wd.reshape(D_FF, d), zero

    exchanges, adam, order = {}, {}, []
    tr = lambda *ts: tuple(jnp.swapaxes(t, 1, 2) for t in ts)
    big = dict(wi=tr(w_in, m_w_in, v_w_in), wg=tr(w_gate, m_w_gate, v_w_gate), wu=tr(w_up, m_w_up, v_w_up), wd=(w_down, m_w_down, v_w_down),
               wo=(w_out, m_w_out, v_w_out))
    kinds = dict(ffn=("wg", "wu", "wd"), mix=("wi", "wo"))

    arrivals = []

    def finish(key, after):
        l, part = key
        arrivals.append((key, _send_wait("scatter", exchanges.pop(key), after, f"exchange_wait_{part}_l{l}")))

    def update():
        while arrivals:
            (l, part), arrived = arrivals.pop(0)
            for kind, parts in zip(kinds[part], arrived):
                adam[kind] = _adamw(parts, *big[kind], f"adamw_{kind}_l{l}", layer=l, prev=adam.get(kind))

    def grads_fn(l, part, gw, after):
        packed = [gw[kind].reshape(N_DEV, -1, d) for kind in kinds[part]]
        exchanges[(l, part)] = _send_start("scatter", packed, None, after, f"exchange_start_{part}_l{l}")
        order.append((l, part))
        token = exchanges[(l, part)][3]
        if len(order) > 2:
            finish(order[-3], token)
        return token[0, 0]

    loss_part, dh, gs = _local_step(x[0], meta, loss_target[0], mix_weights_fn, ffn_weights_fn, grads_fn, attn_sink, ret_decay_fwd, ret_decay_bwd, ret_norm,
                                    norm_mix_pre, norm_mix_post, norm_ffn_pre, norm_ffn_post)
    grad_x = dh[BLOCK:][None]

    st = lambda xs: jnp.stack([t.reshape(-1) if t.ndim == 1 else t[0] for t in xs])
    small = _pack_small(st(gs["mix_pre"]), st(gs["mix_post"]), st(gs["ffn_pre"]), st(gs["ffn_post"]), st(gs["ret_norm"]), st(gs["sink"]),
                        st(gs["dec_f"]), st(gs["dec_b"]), loss_part, dh[PAD_FRONT:BLOCK])
    update()
    small_g, = _allgather([small], "gather_small", after=adam["wo"][0])
    for key in order[-2:]:
        finish(key, small_g)
    update()
    o_wi, o_wo, o_wg, o_wu, o_wd = tr(*adam["wi"]), adam["wo"], tr(*adam["wg"]), tr(*adam["wu"]), adam["wd"]
    zmeta = jnp.zeros((N_META, d), F32)
    packs = [_pack_small(a[0], a[1], a[2], a[3], a[4], a[5], a[6], a[7], zero, zmeta) for a in (
        (norm_mix_pre, norm_mix_post, norm_ffn_pre, norm_ffn_post, ret_norm, attn_sink, ret_decay_fwd, ret_decay_bwd),
        (m_norm_mix_pre, m_norm_mix_post, m_norm_ffn_pre, m_norm_ffn_post, m_ret_norm, m_attn_sink, m_ret_decay_fwd, m_ret_decay_bwd),
        (v_norm_mix_pre, v_norm_mix_post, v_norm_ffn_pre, v_norm_ffn_post, v_ret_norm, v_attn_sink, v_ret_decay_fwd, v_ret_decay_bwd))]
    o_small = [_unpack_small(o, depth) for o in _adamw(small_g, packs[0], packs[1], packs[2], "adamw_small")]
    meta_parts = lax.dynamic_slice(small_g, (0, ROW_META, me * (d // N_DEV)), (N_DEV, N_META, d // N_DEV))
    o_meta = _adamw(meta_parts, meta_tokens, m_meta_tokens, v_meta_tokens, "adamw_meta")

    outs = []
    for i in range(4):
        s = o_small[i]
        outs += [o_meta[i], o_wi[i], o_wo[i], s["sink"], s["dec_f"], s["dec_b"], s["ret_norm"], s["mix_pre"], s["mix_post"], o_wg[i], o_wu[i],
                 o_wd[i], s["ffn_pre"], s["ffn_post"]]
    return (o_small[0]["loss"], grad_x, *outs)
```

```python
import jax
import jax.numpy as jnp
import numpy as np
from jax import lax
from jax.experimental import pallas as pl
from jax.experimental.pallas import tpu as pltpu

F32, BF16 = jnp.float32, jnp.bfloat16

D_MODEL = 2048
N_META = 16
BLOCK = 128
WINDOW = 128
PAD_FRONT = BLOCK - N_META
ATT_HEAD_DIM = 128
ATT_WIDTH = D_MODEL // 2
ATT_HEADS = ATT_WIDTH // ATT_HEAD_DIM
ATT_KV_HEADS = 2
ATT_GROUP = ATT_HEADS // ATT_KV_HEADS
KV_WIDTH = ATT_KV_HEADS * ATT_HEAD_DIM
ROT_DIM = ATT_HEAD_DIM // 4
ROPE_THETA = 500000.0
RET_WIDTH = D_MODEL - ATT_WIDTH
RET_HEAD_DIM = 256
RET_HEADS = RET_WIDTH // RET_HEAD_DIM
RET_THETA = 10000.0
D_FF = 5632
IN_COLS = ATT_WIDTH + 2 * KV_WIDTH + 4 * RET_WIDTH
N_DEV = 8
SHARD_COLS = IN_COLS // N_DEV
EPS = 1e-6
NEG = -1e30
RET_K_SCALE = RET_HEAD_DIM ** -0.5
ATT_SCALE = ATT_HEAD_DIM ** -0.5

COL_AV256 = (ATT_WIDTH + KV_WIDTH) // 256
COL_RQ = (ATT_WIDTH + 2 * KV_WIDTH) // RET_HEAD_DIM
COL_RK = COL_RQ + RET_HEADS
COL_RV = COL_RK + RET_HEADS
COL_RG = COL_RV + RET_HEADS

ADAM_LR, ADAM_B1, ADAM_B2, ADAM_EPS, ADAM_WD, ADAM_STEP = 0.001, 0.9, 0.999, 1e-08, 0.01, 10

ROW_MIX_PRE, ROW_MIX_POST, ROW_FFN_PRE, ROW_FFN_POST, ROW_RET_NORM, ROW_SINK, ROW_DEC_F, ROW_DEC_B, ROW_LOSS, ROW_META, SMALL_ROWS = (
    0, 8, 16, 24, 32, 40, 48, 56, 64, 72, 96)
ADAMW_TILE_ELEMS = 128 * 1024

MESH = pl.DeviceIdType.MESH
ANY = pl.BlockSpec(memory_space=pl.ANY)


def _row_tile(n, cap):
    for t in range(cap - cap % 16, 0, -16):
        if n % t == 0:
            return t
    raise ValueError(n)


def _sds(shape, dtype):
    return jax.ShapeDtypeStruct(shape, dtype)


def _silu(x):
    return x * jax.nn.sigmoid(x)


def _dsilu(x):
    s = jax.nn.sigmoid(x)
    return s * (1.0 + x * (1.0 - s))


def _norm_fwd(x, g, res, out_dtype, name):
    n, d = x.shape
    tr = _row_tile(n, 384)

    def body(*refs):
        if res is None:
            x_ref, g_ref, o_ref = refs
        else:
            x_ref, g_ref, r_ref, o_ref = refs
        xv = x_ref[...]
        r = lax.rsqrt(jnp.mean(xv * xv, axis=-1, keepdims=True) + EPS)
        y = xv * r * g_ref[...]
        if res is not None:
            y = y + r_ref[...]
        o_ref[...] = y.astype(o_ref.dtype)

    row = pl.BlockSpec((tr, d), lambda i: (i, 0))
    ins = [row, pl.BlockSpec((1, d), lambda i: (0, 0))] + ([row] if res is not None else [])
    args = (x, g) + ((res,) if res is not None else ())
    return pl.pallas_call(body, grid=(n // tr,), in_specs=ins, out_specs=row, out_shape=_sds((n, d), out_dtype), name=name)(*args)


def _norm_bwd(dy, x, g, res, out_dtype, name):
    n, d = x.shape
    tr = _row_tile(n, 384)

    def body(*refs):
        if res is None:
            dy_ref, x_ref, g_ref, dx_ref, dg_ref = refs
        else:
            dy_ref, x_ref, g_ref, r_ref, dx_ref, dg_ref = refs
        i = pl.program_id(0)
        xv = x_ref[...]
        r = lax.rsqrt(jnp.mean(xv * xv, axis=-1, keepdims=True) + EPS)
        xhat = xv * r
        dyf = dy_ref[...].astype(F32)
        gdy = dyf * g_ref[...]
        dx = r * (gdy - xhat * jnp.mean(gdy * xhat, axis=-1, keepdims=True))
        if res is not None:
            dx = dx + r_ref[...]
        dx_ref[...] = dx.astype(dx_ref.dtype)

        @pl.when(i == 0)
        def _():
            dg_ref[...] = jnp.zeros_like(dg_ref)

        dg_ref[...] += jnp.sum(dyf * xhat, axis=0, keepdims=True)

    row = pl.BlockSpec((tr, d), lambda i: (i, 0))
    vec = pl.BlockSpec((1, d), lambda i: (0, 0))
    ins = [row, row, vec] + ([row] if res is not None else [])
    args = (dy, x, g) + ((res,) if res is not None else ())
    return pl.pallas_call(body, grid=(n // tr,), in_specs=ins, out_specs=(row, vec),
                          out_shape=(_sds((n, d), out_dtype), _sds((1, d), F32)), name=name,
                          compiler_params=pltpu.CompilerParams(dimension_semantics=("arbitrary",)))(*args)


def _mm(a, b, *, ta, tb, grid, a_blk, a_map, b_blk, b_map, o_blk, o_map, o_shape, o_dtype, name, acc=None):
    nk = grid[2]
    dims = (((0,) if ta else (1,), (1,) if tb else (0,)), ((), ()))

    def body(*refs):
        if acc is None:
            a_ref, b_ref, o_ref = refs[:3]
            c_ref = None
        else:
            a_ref, b_ref, c_ref, o_ref = refs[:4]
        part = lax.dot_general(a_ref[...], b_ref[...], dims, preferred_element_type=F32)
        if nk == 1:
            if c_ref is not None:
                part = part + c_ref[...].astype(F32)
            o_ref[...] = part.astype(o_ref.dtype)
            return
        acc_ref = refs[-1]
        k = pl.program_id(2)

        @pl.when(k == 0)
        def _():
            acc_ref[...] = jnp.zeros_like(acc_ref) if c_ref is None else c_ref[...].astype(F32)

        acc_ref[...] += part

        @pl.when(k == nk - 1)
        def _():
            o_ref[...] = acc_ref[...].astype(o_ref.dtype)

    ins = [pl.BlockSpec(a_blk, a_map), pl.BlockSpec(b_blk, b_map)]
    args = [a, b]
    if acc is not None:
        ins.append(pl.BlockSpec(o_blk, o_map))
        args.append(acc)
    return pl.pallas_call(body, grid=grid, in_specs=ins, out_specs=pl.BlockSpec(o_blk, o_map), out_shape=_sds(o_shape, o_dtype),
                          scratch_shapes=[pltpu.VMEM(o_blk, F32)] if nk > 1 else [], name=name,
                          compiler_params=pltpu.CompilerParams(dimension_semantics=("parallel", "parallel", "arbitrary")))(*args)


TOKEN_TILE = 1056
WIDE_TILE = 1408
DEEP_K_TOKEN_TILE = 528


def _mm_nn(x, w, o_dtype, tn, tk, name, acc=None, tm_cap=TOKEN_TILE):
    n, k = x.shape
    tm = _row_tile(n, tm_cap)
    return _mm(x, w, ta=False, tb=False, grid=(n // tm, w.shape[1] // tn, k // tk), a_blk=(tm, tk), a_map=lambda i, j, kk: (i, kk),
               b_blk=(tk, tn), b_map=lambda i, j, kk: (kk, j), o_blk=(tm, tn), o_map=lambda i, j, kk: (i, j),
               o_shape=(n, w.shape[1]), o_dtype=o_dtype, name=name, acc=acc)


def _mm_nt(dy, w, o_dtype, tn, tk, name, acc=None):
    n, k = dy.shape
    tm = _row_tile(n, TOKEN_TILE)
    return _mm(dy, w, ta=False, tb=True, grid=(n // tm, w.shape[0] // tn, k // tk), a_blk=(tm, tk), a_map=lambda i, j, kk: (i, kk),
               b_blk=(tn, tk), b_map=lambda i, j, kk: (j, kk), o_blk=(tm, tn), o_map=lambda i, j, kk: (i, j),
               o_shape=(n, w.shape[0]), o_dtype=o_dtype, name=name, acc=acc)


def _mm_tn(x, dy, tm, tn, name):
    n, m = x.shape
    tk = _row_tile(n, 2 * TOKEN_TILE)
    return _mm(x, dy, ta=True, tb=False, grid=(m // tm, dy.shape[1] // tn, n // tk), a_blk=(tk, tm), a_map=lambda i, j, kk: (kk, i),
               b_blk=(tk, tn), b_map=lambda i, j, kk: (kk, j), o_blk=(tm, tn), o_map=lambda i, j, kk: (i, j),
               o_shape=(m, dy.shape[1]), o_dtype=BF16, name=name)


def _rope_tables(n):
    pos = (jnp.arange(n) - PAD_FRONT).astype(F32)
    half = ROT_DIM // 2
    ang = pos[:, None] * (ROPE_THETA ** (-jnp.arange(half, dtype=F32) / half))[None, :]
    c, s = jnp.cos(ang), jnp.sin(ang)
    rest = ATT_HEAD_DIM - ROT_DIM
    cos_a = jnp.concatenate([c, c, jnp.ones((n, rest), F32)], axis=1)
    sin_a = jnp.concatenate([-s, s, jnp.zeros((n, rest), F32)], axis=1)
    half = RET_HEAD_DIM // 2
    ang = pos[:, None] * (RET_THETA ** (-jnp.arange(half, dtype=F32) / half))[None, :]
    c, s = jnp.cos(ang), jnp.sin(ang)
    perm = np.zeros((ATT_HEAD_DIM, ATT_HEAD_DIM), np.float32)
    for i in range(ROT_DIM):
        perm[(i + ROT_DIM // 2) % ROT_DIM, i] = 1.0
    return cos_a, sin_a, jnp.concatenate([c, c], axis=1), jnp.concatenate([-s, s], axis=1), jnp.asarray(perm, BF16)


def _into(buf, own_shape):
    if buf is None:
        return _sds(own_shape, BF16), [], [], lambda n_inputs: {}
    return _sds(buf.shape, buf.dtype), [ANY], [buf], lambda n_inputs: {n_inputs: 0}


def _rope_att(x, col0, heads, cos, sin, perm, name, plain=None, into=None, out_col=0):
    n = x.shape[0]
    tr = _row_tile(n, 1056)
    hd = ATT_HEAD_DIM
    w2 = 0 if plain is None else plain.shape[1]
    width = heads * hd + w2

    def body(*refs):
        x_ref, c_ref, s_ref, p_ref, o_ref = refs[0], refs[1], refs[2], refs[3], refs[-1]
        for h in range(heads):
            cs = slice(h * hd, (h + 1) * hd)
            xb = x_ref[:, cs].astype(BF16)
            sw = jnp.dot(xb, p_ref[...], preferred_element_type=F32)
            o_ref[:, cs] = (xb.astype(F32) * c_ref[...] + sw * s_ref[...]).astype(o_ref.dtype)
        if plain is not None:
            o_ref[:, heads * hd:] = refs[4][...].astype(o_ref.dtype)

    tab = pl.BlockSpec((tr, hd), lambda i: (i, 0))
    ins = [pl.BlockSpec((tr, heads * hd), lambda i: (i, col0)), tab, tab, pl.BlockSpec((hd, hd), lambda i: (0, 0))]
    args = [x, cos, sin, perm]
    if plain is not None:
        ins.append(pl.BlockSpec((tr, w2), lambda i: (i, 0)))
        args.append(plain)
    shape, extra_specs, extra_args, alias = _into(into, (n, width))
    return pl.pallas_call(body, grid=(n // tr,), in_specs=ins + extra_specs, out_specs=pl.BlockSpec((tr, width), lambda i: (i, out_col)),
                          out_shape=shape, input_output_aliases=alias(len(ins)), name=name)(*args, *extra_args)


def _rope_ret(x, col0, cos, sin, name, into=None, out_col=0):
    p, n, _ = x.shape
    tr = _row_tile(n, 1056)
    hd = RET_HEAD_DIM

    def body(*refs):
        x_ref, o_ref = refs[0], refs[-1]
        for h in range(2):
            cs = slice(h * hd, (h + 1) * hd)
            xv = x_ref[0, :, cs].astype(F32)
            for q in range(1, p):
                xv = xv + x_ref[q, :, cs].astype(F32)
            if cos is not None:
                sw = jnp.concatenate([xv[:, hd // 2:], xv[:, :hd // 2]], axis=1)
                xv = xv * refs[1][...] + sw * refs[2][...]
            o_ref[:, cs] = xv.astype(o_ref.dtype)

    tab = pl.BlockSpec((tr, hd), lambda i, j: (i, 0))
    ins = [pl.BlockSpec((p, tr, 2 * hd), lambda i, j: (0, i, col0 + j))] + ([tab, tab] if cos is not None else [])
    args = (x,) + ((cos, sin) if cos is not None else ())
    shape, extra_specs, extra_args, alias = _into(into, (n, RET_WIDTH))
    return pl.pallas_call(body, grid=(n // tr, RET_HEADS // 2), in_specs=ins + extra_specs,
                          out_specs=pl.BlockSpec((tr, 2 * hd), lambda i, j: (i, out_col + j)), out_shape=shape,
                          input_output_aliases=alias(len(ins)), name=name)(*args, *extra_args)


def _att_mask(nblk, n_tot):
    row = lax.broadcasted_iota(jnp.int32, (BLOCK, 4 * BLOCK), 0)
    col = lax.broadcasted_iota(jnp.int32, (BLOCK, 4 * BLOCK), 1)
    qi = nblk * BLOCK + row
    seg = col // BLOCK
    cj = col % BLOCK
    kj = (nblk - 1 + seg) * BLOCK + cj
    band = (jnp.abs(qi - kj) <= WINDOW) & (kj >= PAD_FRONT) & (kj < n_tot) & (seg < 3)
    meta = (seg == 3) & (cj >= PAD_FRONT) & (jnp.abs(qi - cj) > WINDOW)
    return band | meta


def _att_specs(nb, v_col):
    kv = lambda f, cb: pl.BlockSpec((BLOCK, KV_WIDTH), lambda n: (f(n), cb))
    prev, own, nxt, first = (lambda n: jnp.maximum(n - 1, 0)), (lambda n: n), (lambda n: jnp.minimum(n + 1, nb - 1)), (lambda n: 0)
    return [kv(f, 0) for f in (prev, own, nxt, first)] + [kv(f, v_col) for f in (prev, own, nxt, first)]


def _att_probs(s, ok, snk):
    s = jnp.where(ok, s, NEG)
    m = jnp.maximum(jnp.max(s, axis=-1, keepdims=True), snk)
    p = jnp.exp(s - m)
    ps = jnp.exp(snk - m)
    inv = 1.0 / (jnp.sum(p, axis=-1, keepdims=True) + ps)
    return p * inv, ps * inv


def _att_fwd(q, k, proj, sink_b, name):
    n = q.shape[0]
    nb = n // BLOCK
    hd = ATT_HEAD_DIM

    def body(q_ref, kp, ko, kn, km, vp, vo, vn, vm, sink_ref, o_ref):
        nblk = pl.program_id(0)
        ok = _att_mask(nblk, n)
        keep = (nblk * BLOCK + lax.broadcasted_iota(jnp.int32, (BLOCK, 1), 0)) >= PAD_FRONT
        for kh in range(ATT_KV_HEADS):
            cs = slice(kh * hd, (kh + 1) * hd)
            kk = jnp.concatenate([r[:, cs] for r in (kp, ko, kn, km)], axis=0)
            vv = jnp.concatenate([r[:, cs] for r in (vp, vo, vn, vm)], axis=0)
            heads = [kh * ATT_GROUP + g for g in range(ATT_GROUP)]
            q4 = jnp.concatenate([q_ref[:, h * hd:(h + 1) * hd] for h in heads], axis=0)
            s = lax.dot_general(q4, kk, (((1,), (1,)), ((), ())), preferred_element_type=F32) * ATT_SCALE
            ps = []
            for g, h in enumerate(heads):
                p, _ = _att_probs(s[g * BLOCK:(g + 1) * BLOCK], ok, sink_ref[h:h + 1, 0:1])
                ps.append(p)
            o = jnp.dot(jnp.concatenate(ps, axis=0).astype(BF16), vv, preferred_element_type=F32)
            for g, h in enumerate(heads):
                o_ref[:, h * hd:(h + 1) * hd] = jnp.where(keep, o[g * BLOCK:(g + 1) * BLOCK], 0.0).astype(o_ref.dtype)

    qspec = pl.BlockSpec((BLOCK, ATT_WIDTH), lambda i: (i, 0))
    return pl.pallas_call(body, grid=(nb,), in_specs=[qspec] + _att_specs(nb, COL_AV256) + [pl.BlockSpec((ATT_HEADS, 128), lambda i: (0, 0))],
                          out_specs=qspec, out_shape=_sds((n, D_MODEL), BF16), name=name)(q, k, k, k, k, proj, proj, proj, proj, sink_b)


def _att_bwd(q, k, proj, sink_b, dmixed, name):
    n = q.shape[0]
    nb = n // BLOCK
    hd = ATT_HEAD_DIM

    def body(q_ref, kp, ko, kn, km, vp, vo, vn, vm, sink_ref, do_ref, dq_ref, dk_ref, dv_ref, dsink_ref):
        nblk = pl.program_id(0)

        @pl.when(nblk == 0)
        def _():
            dk_ref[...] = jnp.zeros_like(dk_ref)
            dv_ref[...] = jnp.zeros_like(dv_ref)
            dsink_ref[...] = jnp.zeros_like(dsink_ref)

        ok = _att_mask(nblk, n)
        rows = [jnp.maximum(nblk - 1, 0), nblk, jnp.minimum(nblk + 1, nb - 1), 0]
        for kh in range(ATT_KV_HEADS):
            cs = slice(kh * hd, (kh + 1) * hd)
            kk = jnp.concatenate([r[:, cs] for r in (kp, ko, kn, km)], axis=0)
            vv = jnp.concatenate([r[:, cs] for r in (vp, vo, vn, vm)], axis=0)
            heads = [kh * ATT_GROUP + g for g in range(ATT_GROUP)]
            q4 = jnp.concatenate([q_ref[:, h * hd:(h + 1) * hd] for h in heads], axis=0)
            do4 = jnp.concatenate([do_ref[:, h * hd:(h + 1) * hd] for h in heads], axis=0)
            s = lax.dot_general(q4, kk, (((1,), (1,)), ((), ())), preferred_element_type=F32) * ATT_SCALE
            dp = lax.dot_general(do4, vv, (((1,), (1,)), ((), ())), preferred_element_type=F32)
            ps, dss = [], []
            for g, h in enumerate(heads):
                p, psink = _att_probs(s[g * BLOCK:(g + 1) * BLOCK], ok, sink_ref[h:h + 1, 0:1])
                dpg = dp[g * BLOCK:(g + 1) * BLOCK]
                delta = jnp.sum(p * dpg, axis=-1, keepdims=True)
                ps.append(p)
                dss.append(p * (dpg - delta) * ATT_SCALE)
                dsink_ref[h:h + 1, :] = dsink_ref[h:h + 1, :] - jnp.sum(psink * delta, axis=0, keepdims=True)
            ds = jnp.concatenate(dss, axis=0).astype(BF16)
            pb = jnp.concatenate(ps, axis=0).astype(BF16)
            dq = jnp.dot(ds, kk, preferred_element_type=F32)
            for g, h in enumerate(heads):
                dq_ref[:, h * hd:(h + 1) * hd] = dq[g * BLOCK:(g + 1) * BLOCK].astype(dq_ref.dtype)
            dk = lax.dot_general(ds, q4, (((0,), (0,)), ((), ())), preferred_element_type=F32)
            dv = lax.dot_general(pb, do4, (((0,), (0,)), ((), ())), preferred_element_type=F32)
            for seg, r in enumerate(rows):
                at = (pl.ds(pl.multiple_of(r * BLOCK, BLOCK), BLOCK), cs)
                dk_ref[at] += dk[seg * BLOCK:(seg + 1) * BLOCK]
                dv_ref[at] += dv[seg * BLOCK:(seg + 1) * BLOCK]

    qspec = pl.BlockSpec((BLOCK, ATT_WIDTH), lambda i: (i, 0))
    whole = pl.BlockSpec((n, KV_WIDTH), lambda i: (0, 0))
    sinks = pl.BlockSpec((ATT_HEADS, 128), lambda i: (0, 0))
    return pl.pallas_call(body, grid=(nb,), in_specs=[qspec] + _att_specs(nb, COL_AV256) + [sinks, qspec], out_specs=(qspec, whole, whole, sinks),
                          out_shape=(_sds((n, ATT_WIDTH), BF16), _sds((n, KV_WIDTH), F32), _sds((n, KV_WIDTH), F32), _sds((ATT_HEADS, 128), F32)),
                          name=name, compiler_params=pltpu.CompilerParams(dimension_semantics=("arbitrary",)))(
                              q, k, k, k, k, proj, proj, proj, proj, sink_b, dmixed)


def _ret_decay(lg, d):
    a = lax.broadcasted_iota(jnp.int32, (BLOCK, 1), 0)
    b = lax.broadcasted_iota(jnp.int32, (1, BLOCK), 1)
    t_col = a + d * (BLOCK - 1 - 2 * a)
    t_row = b + d * (BLOCK - 1 - 2 * b)
    diff = t_col - t_row
    dist = jnp.maximum(diff, 0).astype(F32)
    dmask = jnp.where(diff >= d, jnp.exp(lg * dist), 0.0)
    tf = t_col.astype(F32)
    xi = jnp.exp(lg * (tf + 1.0))
    zeta = jnp.exp(lg * (BLOCK - 1.0 - tf))
    gam = jnp.exp(jnp.full((1, 1), BLOCK, F32) * lg)
    return dmask, dist, xi, zeta, gam, tf


def _ret_fwd(q, k, proj, lg, name):
    n = q.shape[0]
    nc = n // BLOCK
    hd = RET_HEAD_DIM
    chunk = lambda d, c: c + d * (nc - 1 - 2 * c)

    def body(lg_ref, q_ref, k_ref, v0, v1, v2, v3, o_ref, st_ref, s_ref):
        d, c = pl.program_id(0), pl.program_id(1)

        @pl.when(c == 0)
        def _():
            s_ref[...] = jnp.zeros_like(s_ref)

        for h, v_ref in enumerate((v0, v1, v2, v3)):
            cs = slice(h * hd, (h + 1) * hd)
            dmask, _, xi, zeta, gam, _ = _ret_decay(lg_ref[h, d], d)
            qv = q_ref[:, cs]
            kf = k_ref[:, cs].astype(F32) * RET_K_SCALE
            vv = v_ref[...]
            s = lax.dot_general(qv, kf.astype(BF16), (((1,), (1,)), ((), ())), preferred_element_type=F32)
            sb = s_ref[h]
            o_ref[:, cs] = (jnp.dot((s * dmask).astype(BF16), vv, preferred_element_type=F32)
                            + jnp.dot((qv.astype(F32) * xi).astype(BF16), sb.astype(BF16), preferred_element_type=F32))
            st_ref[h] = sb
            s_ref[h] = gam * sb + lax.dot_general((kf * zeta).astype(BF16), vv, (((0,), (0,)), ((), ())), preferred_element_type=F32)

    wide = pl.BlockSpec((BLOCK, RET_WIDTH), lambda d, c: (chunk(d, c), 0))
    vblk = lambda h: pl.BlockSpec((BLOCK, hd), lambda d, c: (chunk(d, c), COL_RV + h))
    return pl.pallas_call(
        body, grid=(2, nc), in_specs=[pl.BlockSpec(memory_space=pltpu.SMEM), wide, wide] + [vblk(h) for h in range(RET_HEADS)],
        out_specs=(pl.BlockSpec((None, BLOCK, RET_WIDTH), lambda d, c: (d, chunk(d, c), 0)),
                   pl.BlockSpec((RET_HEADS, None, None, hd, hd), lambda d, c: (0, d, c, 0, 0))),
        out_shape=(_sds((2, n, RET_WIDTH), F32), _sds((RET_HEADS, 2, nc, hd, hd), F32)), scratch_shapes=[pltpu.VMEM((RET_HEADS, hd, hd), F32)],
        name=name, compiler_params=pltpu.CompilerParams(dimension_semantics=("parallel", "arbitrary")))(lg, q, k, proj, proj, proj, proj)


def _ret_bwd(q, k, proj, lg, do, states, name):
    n = q.shape[0]
    nc = n // BLOCK
    hd = RET_HEAD_DIM
    chunk = lambda d, r: (nc - 1 - r) + d * (2 * r - (nc - 1))

    def body(lg_ref, q_ref, k_ref, v0, v1, v2, v3, do_ref, st_ref, dq_ref, dk_ref, dv_ref, dlg_ref, ds_ref):
        d, r = pl.program_id(0), pl.program_id(1)

        @pl.when(r == 0)
        def _():
            ds_ref[...] = jnp.zeros_like(ds_ref)
            dlg_ref[...] = jnp.zeros_like(dlg_ref)

        row = lax.broadcasted_iota(jnp.int32, (BLOCK, 1), 0) + chunk(d, r) * BLOCK
        keep = row >= PAD_FRONT
        nt = (((1,), (1,)), ((), ()))
        tn = (((0,), (0,)), ((), ()))
        for h, v_ref in enumerate((v0, v1, v2, v3)):
            cs = slice(h * hd, (h + 1) * hd)
            dmask, dist, xi, zeta, gam, tf = _ret_decay(lg_ref[h, d], d)
            qv, vv, dov = q_ref[:, cs], v_ref[...], do_ref[:, cs]
            qf = qv.astype(F32)
            kf = k_ref[:, cs].astype(F32) * RET_K_SCALE
            kb = kf.astype(BF16)
            sc = st_ref[h]
            dsn = ds_ref[h]
            s = lax.dot_general(qv, kb, nt, preferred_element_type=F32)
            dsc = lax.dot_general(dov, vv, nt, preferred_element_type=F32) * dmask
            dsb = dsc.astype(BF16)
            dq_c = xi * lax.dot_general(dov, sc.astype(BF16), nt, preferred_element_type=F32)
            dk_c = zeta * lax.dot_general(vv, dsn.astype(BF16), nt, preferred_element_type=F32)
            dq = jnp.dot(dsb, kb, preferred_element_type=F32) + dq_c
            dk = lax.dot_general(dsb, qv, tn, preferred_element_type=F32) + dk_c
            dv = (lax.dot_general((s * dmask).astype(BF16), dov, tn, preferred_element_type=F32)
                  + jnp.dot((kf * zeta).astype(BF16), dsn.astype(BF16), preferred_element_type=F32))
            ds_ref[h] = gam * dsn + lax.dot_general((qf * xi).astype(BF16), dov, tn, preferred_element_type=F32)
            dlg = (jnp.sum(dsc * s * dist, keepdims=True)
                   + jnp.sum((tf + 1.0) * jnp.sum(qf * dq_c, axis=-1, keepdims=True), keepdims=True)
                   + jnp.sum((BLOCK - 1.0 - tf) * jnp.sum(kf * dk_c, axis=-1, keepdims=True), keepdims=True)
                   + BLOCK * gam * jnp.sum(dsn * sc, keepdims=True))
            dlg_ref[h] += dlg
            dq_ref[:, cs] = dq
            dk_ref[:, cs] = jnp.where(keep, dk * RET_K_SCALE, 0.0)
            dv_ref[:, cs] = jnp.where(keep, dv, 0.0)

    wide = pl.BlockSpec((BLOCK, RET_WIDTH), lambda d, r: (chunk(d, r), 0))
    vblk = lambda h: pl.BlockSpec((BLOCK, hd), lambda d, r: (chunk(d, r), COL_RV + h))
    plane = pl.BlockSpec((None, BLOCK, RET_WIDTH), lambda d, r: (d, chunk(d, r), 0))
    return pl.pallas_call(
        body, grid=(2, nc),
        in_specs=[pl.BlockSpec(memory_space=pltpu.SMEM), wide, wide] + [vblk(h) for h in range(RET_HEADS)]
        + [wide, pl.BlockSpec((RET_HEADS, None, None, hd, hd), lambda d, r: (0, d, nc - 1 - r, 0, 0))],
        out_specs=(plane, plane, plane, pl.BlockSpec((RET_HEADS, None, 8, 128), lambda d, r: (0, d, 0, 0))),
        out_shape=(_sds((2, n, RET_WIDTH), F32),) * 3 + (_sds((RET_HEADS, 2, 8, 128), F32),),
        scratch_shapes=[pltpu.VMEM((RET_HEADS, hd, hd), F32)], name=name,
        compiler_params=pltpu.CompilerParams(dimension_semantics=("parallel", "arbitrary")))(lg, q, k, proj, proj, proj, proj, do, states)


def _retgate_fwd(o, proj, gain, mixed, name):
    _, n, _ = o.shape
    tr = _row_tile(n, 1056)
    hd = RET_HEAD_DIM

    def body(o_ref, rg_ref, g_ref, _, y_ref):
        ov = o_ref[0] + o_ref[1]
        r = lax.rsqrt(jnp.mean(ov * ov, axis=-1, keepdims=True) + EPS)
        y_ref[...] = (_silu(rg_ref[...].astype(F32)) * (ov * r * g_ref[...])).astype(y_ref.dtype)

    return pl.pallas_call(body, grid=(n // tr, RET_HEADS),
                          in_specs=[pl.BlockSpec((2, tr, hd), lambda i, h: (0, i, h)), pl.BlockSpec((tr, hd), lambda i, h: (i, COL_RG + h)),
                                    pl.BlockSpec((1, hd), lambda i, h: (0, h)), ANY],
                          out_specs=pl.BlockSpec((tr, hd), lambda i, h: (i, ATT_WIDTH // hd + h)), out_shape=_sds(mixed.shape, mixed.dtype),
                          input_output_aliases={3: 0}, name=name)(o, proj, gain, mixed)


def _retgate_bwd(dmixed, o, proj, gain, name):
    _, n, _ = o.shape
    tr = _row_tile(n, 1056)
    hd = RET_HEAD_DIM

    def body(dy_ref, o_ref, rg_ref, g_ref, do_ref, drg_ref, dg_ref):
        i = pl.program_id(1)
        ov = o_ref[0] + o_ref[1]
        r = lax.rsqrt(jnp.mean(ov * ov, axis=-1, keepdims=True) + EPS)
        xhat = ov * r
        rg = rg_ref[...].astype(F32)
        dy = dy_ref[...].astype(F32)
        drg_ref[...] = (dy * (xhat * g_ref[...]) * _dsilu(rg)).astype(drg_ref.dtype)
        dn = dy * _silu(rg)
        dxh = dn * g_ref[...]
        do_ref[...] = (r * (dxh - xhat * jnp.mean(dxh * xhat, axis=-1, keepdims=True))).astype(do_ref.dtype)

        @pl.when(i == 0)
        def _():
            dg_ref[...] = jnp.zeros_like(dg_ref)

        dg_ref[...] += jnp.sum(dn * xhat, axis=0, keepdims=True)

    tile = pl.BlockSpec((tr, hd), lambda h, i: (i, h))
    vec = pl.BlockSpec((1, hd), lambda h, i: (0, h))
    rg_cols = pl.BlockSpec((tr, hd), lambda h, i: (i, COL_RG + h))
    return pl.pallas_call(body, grid=(RET_HEADS, n // tr),
                          in_specs=[pl.BlockSpec((tr, hd), lambda h, i: (i, ATT_WIDTH // hd + h)), pl.BlockSpec((2, tr, hd), lambda h, i: (0, i, h)),
                                    rg_cols, vec],
                          out_specs=(tile, rg_cols, vec), out_shape=(_sds((n, RET_WIDTH), BF16), _sds((n, IN_COLS), BF16), _sds((1, RET_WIDTH), F32)),
                          name=name, compiler_params=pltpu.CompilerParams(dimension_semantics=("parallel", "arbitrary")))(dmixed, o, proj, gain)


FFN_TILE = 512


def _swiglu_fwd(x, wg_t, wu_t, name):
    n, k = x.shape
    tm = _row_tile(n, TOKEN_TILE)
    nt = (((1,), (1,)), ((), ()))

    def body(x_ref, g_ref, u_ref, go_ref, uo_ref, f_ref):
        g = lax.dot_general(x_ref[...], g_ref[...], nt, preferred_element_type=F32)
        u = lax.dot_general(x_ref[...], u_ref[...], nt, preferred_element_type=F32)
        go_ref[...] = g.astype(go_ref.dtype)
        uo_ref[...] = u.astype(uo_ref.dtype)
        f_ref[...] = (_silu(g) * u).astype(f_ref.dtype)

    w = pl.BlockSpec((FFN_TILE, k), lambda i, j: (j, 0))
    o = pl.BlockSpec((tm, FFN_TILE), lambda i, j: (i, j))
    return pl.pallas_call(body, grid=(n // tm, wg_t.shape[0] // FFN_TILE), in_specs=[pl.BlockSpec((tm, k), lambda i, j: (i, 0)), w, w],
                          out_specs=(o, o, o), out_shape=(_sds((n, wg_t.shape[0]), BF16),) * 3, name=name,
                          compiler_params=pltpu.CompilerParams(dimension_semantics=("parallel", "parallel")))(x, wg_t, wu_t)


def _swiglu_bwd(dy, wd, gate, up, name):
    n, k = dy.shape
    tm = _row_tile(n, TOKEN_TILE)
    nt = (((1,), (1,)), ((), ()))

    def body(dy_ref, w_ref, g_ref, u_ref, dg_ref, du_ref):
        df = lax.dot_general(dy_ref[...], w_ref[...], nt, preferred_element_type=F32)
        g = g_ref[...].astype(F32)
        dg_ref[...] = (df * u_ref[...].astype(F32) * _dsilu(g)).astype(dg_ref.dtype)
        du_ref[...] = (df * _silu(g)).astype(du_ref.dtype)

    o = pl.BlockSpec((tm, FFN_TILE), lambda i, j: (i, j))
    return pl.pallas_call(body, grid=(n // tm, wd.shape[0] // FFN_TILE),
                          in_specs=[pl.BlockSpec((tm, k), lambda i, j: (i, 0)), pl.BlockSpec((FFN_TILE, k), lambda i, j: (j, 0)), o, o],
                          out_specs=(o, o), out_shape=(_sds((n, wd.shape[0]), BF16),) * 2, name=name,
                          compiler_params=pltpu.CompilerParams(dimension_semantics=("parallel", "parallel")))(dy, wd, gate, up)


def _loss_head(h, target, name):
    n, d = h.shape
    nb = n // BLOCK

    def body(h_ref, t_ref, dh_ref, l_ref):
        i = pl.program_id(0)

        @pl.when(i == 0)
        def _():
            l_ref[...] = jnp.zeros_like(l_ref)
            dh_ref[...] = jnp.zeros_like(dh_ref)

        @pl.when(i > 0)
        def _():
            e = h_ref[...] - t_ref[...]
            dh_ref[...] = e * (1.0 / d)
            l_ref[...] += 0.5 * jnp.sum(jnp.mean(e * e, axis=-1, keepdims=True), keepdims=True)

    blk = pl.BlockSpec((BLOCK, d), lambda i: (i, 0))
    return pl.pallas_call(body, grid=(nb,), in_specs=[blk, pl.BlockSpec((BLOCK, d), lambda i: (jnp.maximum(i - 1, 0), 0))],
                          out_specs=(blk, pl.BlockSpec((8, 128), lambda i: (0, 0))), out_shape=(_sds((n, d), F32), _sds((8, 128), F32)), name=name,
                          compiler_params=pltpu.CompilerParams(dimension_semantics=("arbitrary",)))(h, target)


def _adamw(parts, w, m, v, name, sel=None, layer=None, prev=None, after=None):
    s, (r, c) = parts.shape[0], parts.shape[-2:]
    tr = _row_tile(r, max(16, (ADAMW_TILE_ELEMS // c) // 16 * 16))
    b1c, b2c = 1.0 - ADAM_B1 ** ADAM_STEP, 1.0 - ADAM_B2 ** ADAM_STEP

    def body(p_ref, w_ref, m_ref, v_ref, *rest):
        g_ref, d_ref, mo_ref, vo_ref = rest[-4:]
        g = p_ref[0].astype(F32)
        for q in range(1, s):
            g = g + p_ref[q].astype(F32)
        mn = ADAM_B1 * m_ref[...] + (1.0 - ADAM_B1) * g
        vn = ADAM_B2 * v_ref[...] + (1.0 - ADAM_B2) * jnp.square(g)
        g_ref[...] = g
        mo_ref[...] = mn
        vo_ref[...] = vn
        d_ref[...] = -ADAM_LR * ((mn / b1c) / (jnp.sqrt(vn / b2c) + ADAM_EPS) + ADAM_WD * w_ref[...])

    pspec = (pl.BlockSpec((s, tr, c), lambda i: (0, i, 0)) if sel is None else pl.BlockSpec((s, None, tr, c), lambda i: (0, sel, i, 0)))
    if layer is None:
        t = pl.BlockSpec((tr, c), lambda i: (i, 0))
        return pl.pallas_call(body, grid=(r // tr,), in_specs=[pspec, t, t, t], out_specs=(t, t, t, t), out_shape=(_sds((r, c), F32),) * 4,
                              name=name)(parts, w, m, v)
    t = pl.BlockSpec((None, tr, c), lambda i: (layer, i, 0))
    prev = prev if prev is not None else tuple(lax.empty(w.shape, F32) for _ in range(4))
    extra = [] if after is None else [after]
    return pl.pallas_call(body, grid=(r // tr,), in_specs=[pspec, t, t, t] + [ANY] * (4 + len(extra)), out_specs=(t, t, t, t),
                          out_shape=(_sds(w.shape, F32),) * 4, input_output_aliases={4 + i: i for i in range(4)}, name=name)(
                              parts, w, m, v, *prev, *extra)


def _allgather(xs, name, after=None):
    na = len(xs)
    first_out = na + (after is not None)

    def body(*refs):
        x_refs, o_refs = refs[:na], refs[first_out:first_out + na]
        send, recv, lsem = refs[first_out + na:]
        x, y, c = lax.axis_index("x"), lax.axis_index("y"), lax.axis_index("c")
        me, sib = (x, y, c), (x, y, 1 - c)
        chips = [(1 - x, y), (x, 1 - y), (1 - x, 1 - y)]
        slot = lambda p: 4 * p[0] + 2 * p[1] + p[2]

        def copy(a, k, block, to, src=None):
            dst = o_refs[a].at[slot(block)]
            return pltpu.make_async_remote_copy(src_ref=dst if src is None else src, dst_ref=dst, send_sem=send.at[a, k], recv_sem=recv.at[a, k],
                                                device_id=to, device_id_type=MESH)

        mine = [pltpu.make_async_copy(x_refs[a], o_refs[a].at[slot(me)], lsem.at[a]) for a in range(na)]
        for cp in mine:
            cp.start()
        first = []
        for a in range(na):
            first.append(copy(a, 0, me, sib, src=x_refs[a]))
            first += [copy(a, 1 + j, me, (*chip, c), src=x_refs[a]) for j, chip in enumerate(chips)]
        for cp in first:
            cp.start()
        passed = []
        for j, chip in enumerate(chips):
            for a in range(na):
                copy(a, 1 + j, (*chip, c), me).wait_recv()
                passed.append(copy(a, 4 + j, (*chip, c), sib))
                passed[-1].start()
        for a in range(na):
            copy(a, 0, sib, me).wait_recv()
            for j, chip in enumerate(chips):
                copy(a, 4 + j, (*chip, 1 - c), me).wait_recv()
        for cp in first + passed:
            cp.wait_send()
        for cp in mine:
            cp.wait()

    extra = [] if after is None else [after]
    return pl.pallas_call(body, in_specs=[ANY] * (na + len(extra)), out_specs=[ANY] * na,
                          out_shape=[_sds((N_DEV,) + t.shape, t.dtype) for t in xs],
                          scratch_shapes=[pltpu.SemaphoreType.DMA((na, 7)), pltpu.SemaphoreType.DMA((na, 7)), pltpu.SemaphoreType.DMA((na,))],
                          name=name)(*xs, *extra)


HBM = pl.BlockSpec(memory_space=pltpu.HBM)
SEM = pl.BlockSpec(memory_space=pltpu.SEMAPHORE)
EFFECT = pltpu.SideEffectType.DATAFLOW_SIDE_EFFECTING


SPLIT_RELATIONS = dict(gather=(1, 2, 4, 6),
                       forward=(2, 4, 6),
                       scatter=tuple(range(1, N_DEV)))


def _split_copies(mode, x_refs, land_refs, send, recv, own, landing):
    x, y, c = lax.axis_index("x"), lax.axis_index("y"), lax.axis_index("c")
    flip = lambda r: ((1 - x if r & 4 else x), (1 - y if r & 2 else y), (1 - c if r & 1 else c))
    slot = lambda p: 4 * p[0] + 2 * p[1] + p[2]
    me = slot((x, y, c))
    rel = SPLIT_RELATIONS[mode]
    local, remote = [], []
    for a in range(len(land_refs)):
        if mode != "forward":
            local.append(pltpu.make_async_copy(x_refs[a].at[me] if mode == "scatter" else x_refs[a], land_refs[a].at[me], own.at[a]))
        for j, r in enumerate(rel):
            if mode == "forward":
                to = flip(1)
                src = dst = land_refs[a].at[slot(flip(r ^ 1) if landing else flip(r))]
            else:
                to = flip(r)
                src = x_refs[a].at[slot(to)] if mode == "scatter" else x_refs[a]
                dst = land_refs[a].at[slot(to) if landing else me]
            remote.append(pltpu.make_async_remote_copy(src_ref=src, dst_ref=dst, send_sem=send.at[len(rel) * a + j],
                                                       recv_sem=recv.at[len(rel) * a + j], device_id=to, device_id_type=MESH))
    return local, remote


def _send_start(mode, xs, lands, after, name):
    if lands is None:
        lands = [lax.empty(t.shape if mode == "scatter" else (N_DEV,) + t.shape, t.dtype) for t in xs]
    nx, na, nr = len(xs), len(lands), len(SPLIT_RELATIONS[mode])
    nsem = 2 if mode == "forward" else 3

    def body(*refs):
        x_refs, land_refs = refs[:nx], refs[nx:nx + na]
        sems = refs[nx + na + 1:nx + na + 1 + nsem]
        local, remote = _split_copies(mode, x_refs, land_refs, sems[0], sems[1], sems[2] if nsem == 3 else None, False)
        for cp in remote + local:
            cp.start()
        refs[-1][...] = jnp.zeros_like(refs[-1])

    hbm = lambda t: pltpu.with_memory_space_constraint(t, pltpu.HBM)
    sem_shapes = [pltpu.SemaphoreType.DMA((nr * na,)), pltpu.SemaphoreType.DMA((nr * na,)), pltpu.SemaphoreType.DMA((na,))][:nsem]
    outs = pl.pallas_call(
        body, name=name,
        out_shape=(*sem_shapes, *[pltpu.HBM(t.shape, t.dtype) for t in list(xs) + list(lands)], _sds((8, 128), F32)),
        in_specs=[HBM] * (nx + na) + [ANY], out_specs=(*[SEM] * nsem, *[HBM] * (nx + na), pl.BlockSpec(memory_space=pltpu.VMEM)),
        input_output_aliases={i: nsem + i for i in range(nx + na)},
        compiler_params=pltpu.CompilerParams(has_side_effects=EFFECT))(*[hbm(t) for t in list(xs) + list(lands)], after)
    return outs[:nsem], list(outs[nsem:nsem + nx]), list(outs[nsem + nx:nsem + nx + na]), outs[-1]


def _send_wait(mode, started, after, name):
    sems, xs, lands, _ = started
    nx, na, nsem = len(xs), len(lands), len(sems)

    def body(*refs):
        s = refs[nx + na:nx + na + nsem]
        local, remote = _split_copies(mode, refs[:nx], refs[nx:nx + na], s[0], s[1], s[2] if nsem == 3 else None, True)
        for cp in remote:
            cp.wait_send()
            cp.wait_recv()
        for cp in local:
            cp.wait()

    outs = pl.pallas_call(body, name=name, out_shape=tuple(pltpu.HBM(t.shape, t.dtype) for t in xs + lands),
                          in_specs=[HBM] * (nx + na) + [SEM] * nsem + [ANY], out_specs=[HBM] * (nx + na),
                          input_output_aliases={i: i for i in range(nx + na)},
                          compiler_params=pltpu.CompilerParams(has_side_effects=EFFECT))(*xs, *lands, *sems, after)
    return list(outs[nx:])


def _local_step(x, meta, target, mix_weights_fn, ffn_weights_fn, grads_fn, sink, dec_f, dec_b, ret_norm, n_mix_pre, n_mix_post, n_ffn_pre,
                n_ffn_post):
    depth = n_mix_pre.shape[0]
    d = D_MODEL
    h = jnp.concatenate([jnp.zeros((PAD_FRONT, d), F32), meta, x], axis=0)
    n = h.shape[0]
    cos_a, sin_a, cos_r, sin_r, perm = _rope_tables(n)
    lg_all = jnp.stack([-jnp.exp(dec_f), -jnp.exp(dec_b)], axis=-1)
    saved = []
    for l in range(depth):
        t = f"l{l}_"
        sink_b = jnp.broadcast_to(sink[l][:, None], (ATT_HEADS, 128))
        wi, wo, tok = mix_weights_fn(l, h)
        u = _norm_fwd(h, (n_mix_pre[l] + tok)[None], None, BF16, t + "norm_mix_pre")
        proj = _mm_nt(u, wi, BF16, WIDE_TILE, d, t + "proj")
        aq = _rope_att(proj, 0, ATT_HEADS, cos_a, sin_a, perm, t + "rope_aq")
        ak = _rope_att(proj, ATT_WIDTH // KV_WIDTH, ATT_KV_HEADS, cos_a, sin_a, perm, t + "rope_ak")
        mixed = _att_fwd(aq, ak, proj, sink_b, t + "att")
        proj3 = proj[None]
        rq = _rope_ret(proj3, COL_RQ // 2, cos_r, sin_r, t + "rope_rq")
        rk = _rope_ret(proj3, COL_RK // 2, cos_r, sin_r, t + "rope_rk")
        o_ret, states = _ret_fwd(rq, rk, proj, lg_all[l], t + "ret")
        mixed = _retgate_fwd(o_ret, proj, ret_norm[l][None], mixed, t + "retgate")
        mo = _mm_nn(mixed, wo, F32, 1024, d, t + "out_proj")
        h_mid = _norm_fwd(mo, n_mix_post[l][None], h, F32, t + "norm_mix_post")
        wg, wu, wd, tok = ffn_weights_fn(l, h_mid)
        u2 = _norm_fwd(h_mid, (n_ffn_pre[l] + tok)[None], None, BF16, t + "norm_ffn_pre")
        gate, up, f = _swiglu_fwd(u2, wg, wu, t + "gate_up")
        dn = _mm_nn(f, wd, F32, 512, D_FF, t + "down", tm_cap=DEEP_K_TOKEN_TILE)
        h_out = _norm_fwd(dn, n_ffn_post[l][None], h_mid, F32, t + "norm_ffn_post")
        saved.append(dict(h=h, u=u, proj=proj, aq=aq, ak=ak, rq=rq, rk=rk, o_ret=o_ret, states=states, mixed=mixed, mo=mo, h_mid=h_mid, u2=u2,
                          gate=gate, up=up, f=f, dn=dn, sink_b=sink_b, wi=wi, wo=wo, wg=wg, wu=wu, wd=wd))
        h = h_out

    dh, loss_part = _loss_head(h, target, "loss_head")
    gs = dict(sink=[None] * depth, dec_f=[None] * depth, dec_b=[None] * depth, ret_norm=[None] * depth, mix_pre=[None] * depth,
              mix_post=[None] * depth, ffn_pre=[None] * depth, ffn_post=[None] * depth)
    tok_b = jnp.zeros((), F32)
    for l in reversed(range(depth)):
        t = f"l{l}_b_"
        sv = saved[l]
        proj = sv["proj"]
        d_dn, gs["ffn_post"][l] = _norm_bwd(dh, sv["dn"], (n_ffn_post[l] + tok_b)[None], None, BF16, t + "norm_ffn_post")
        gw = {}
        d_gate, d_up = _swiglu_bwd(d_dn, sv["wd"], sv["gate"], sv["up"], t + "d_gate_up")
        gw["wd"] = _mm_tn(sv["f"], d_dn, WIDE_TILE, 1024, t + "dw_down")
        du2 = _mm_nn(d_gate, sv["wg"], F32, 512, D_FF, t + "du2_gate", tm_cap=DEEP_K_TOKEN_TILE)
        du2 = _mm_nn(d_up, sv["wu"], F32, 512, D_FF, t + "du2_up", acc=du2, tm_cap=DEEP_K_TOKEN_TILE)
        gw["wg"] = _mm_tn(d_gate, sv["u2"], WIDE_TILE, 1024, t + "dw_gate")
        gw["wu"] = _mm_tn(d_up, sv["u2"], WIDE_TILE, 1024, t + "dw_up")
        tok_b = grads_fn(l, "ffn", gw, du2)
        dh, gs["ffn_pre"][l] = _norm_bwd(du2, sv["h_mid"], (n_ffn_pre[l] + tok_b)[None], dh, F32, t + "norm_ffn_pre")
        d_mo, gs["mix_post"][l] = _norm_bwd(dh, sv["mo"], n_mix_post[l][None], None, BF16, t + "norm_mix_post")
        d_mixed = _mm_nt(d_mo, sv["wo"], BF16, 1024, d, t + "d_mixed")
        gw["wo"] = _mm_tn(sv["mixed"], d_mo, 1024, 1024, t + "dw_out")
        d_o, dproj, gs["ret_norm"][l] = _retgate_bwd(d_mixed, sv["o_ret"], proj, ret_norm[l][None], t + "retgate")
        dq_r, dk_r, dv_r, dlg = _ret_bwd(sv["rq"], sv["rk"], proj, lg_all[l], d_o, sv["states"], t + "ret")
        draw = dlg[:, :, 0, 0] * lg_all[l]
        gs["dec_f"][l], gs["dec_b"][l] = draw[:, 0], draw[:, 1]
        dproj = _rope_ret(dq_r, 0, cos_r, -sin_r, t + "rope_rq", into=dproj, out_col=COL_RQ // 2)
        dproj = _rope_ret(dk_r, 0, cos_r, -sin_r, t + "rope_rk", into=dproj, out_col=COL_RK // 2)
        dproj = _rope_ret(dv_r, 0, None, None, t + "sum_rv", into=dproj, out_col=COL_RV // 2)
        dq_a, dk_a, dv_a, dsink = _att_bwd(sv["aq"], sv["ak"], proj, sv["sink_b"], d_mixed, t + "att")
        gs["sink"][l] = dsink[:, 0]
        dproj = _rope_att(dq_a, 0, ATT_HEADS, cos_a, -sin_a, perm, t + "rope_aq", into=dproj, out_col=0)
        dproj = _rope_att(dk_a, 0, ATT_KV_HEADS, cos_a, -sin_a, perm, t + "rope_ak", plain=dv_a, into=dproj, out_col=ATT_WIDTH // (2 * KV_WIDTH))
        gw["wi"] = _mm_tn(dproj, sv["u"], WIDE_TILE, 1024, t + "dw_in")
        tok_b = grads_fn(l, "mix", gw, dproj)
        du = _mm_nn(dproj, sv["wi"], F32, 512, IN_COLS, t + "du", tm_cap=DEEP_K_TOKEN_TILE)
        dh, gs["mix_pre"][l] = _norm_bwd(du, sv["h"], (n_mix_pre[l] + tok_b)[None], dh, F32, t + "norm_mix_pre")
    return loss_part[0, 0], dh, gs


def _pack_small(mix_pre, mix_post, ffn_pre, ffn_post, ret_norm, sink, dec_f, dec_b, loss, meta):
    d = D_MODEL

    def tile(a, rows=8):
        a = jnp.reshape(a, (-1, a.shape[-1])) if a.ndim else jnp.reshape(a, (1, 1))
        return jnp.pad(a, ((0, rows - a.shape[0]), (0, d - a.shape[1])))

    return jnp.concatenate([tile(mix_pre), tile(mix_post), tile(ffn_pre), tile(ffn_post), tile(ret_norm.reshape(-1, d)), tile(sink), tile(dec_f),
                            tile(dec_b), tile(loss), tile(meta, SMALL_ROWS - ROW_META)], axis=0)


def _unpack_small(p, depth):
    rows = lambda r0, cols: p[r0:r0 + depth, :cols]
    return dict(mix_pre=rows(ROW_MIX_PRE, D_MODEL), mix_post=rows(ROW_MIX_POST, D_MODEL), ffn_pre=rows(ROW_FFN_PRE, D_MODEL),
                ffn_post=rows(ROW_FFN_POST, D_MODEL), ret_norm=p[ROW_RET_NORM:ROW_RET_NORM + depth * RET_WIDTH // D_MODEL].reshape(depth, RET_WIDTH),
                sink=rows(ROW_SINK, ATT_HEADS), dec_f=rows(ROW_DEC_F, RET_HEADS), dec_b=rows(ROW_DEC_B, RET_HEADS), loss=p[ROW_LOSS, 0])


def kernel(x, meta_tokens, w_in, w_out, attn_sink, ret_decay_fwd, ret_decay_bwd, ret_norm, norm_mix_pre, norm_mix_post, w_gate, w_up, w_down, norm_ffn_pre, norm_ffn_post, loss_target, m_meta_tokens, m_w_in, m_w_out, m_attn_sink, m_ret_decay_fwd, m_ret_decay_bwd, m_ret_norm, m_norm_mix_pre, m_norm_mix_post, m_w_gate, m_w_up, m_w_down, m_norm_ffn_pre, m_norm_ffn_post, v_meta_tokens, v_w_in, v_w_out, v_attn_sink, v_ret_decay_fwd, v_ret_decay_bwd, v_ret_norm, v_norm_mix_pre, v_norm_mix_post, v_w_gate, v_w_up, v_w_down, v_norm_ffn_pre, v_norm_ffn_post):
    depth, d = w_in.shape[0], D_MODEL
    me = 4 * lax.axis_index("x") + 2 * lax.axis_index("y") + lax.axis_index("c")
    zero = jnp.zeros((), F32)

    meta_g, = _allgather([meta_tokens], "gather_meta")
    meta = meta_g.transpose(1, 0, 2).reshape(N_META, d)

    def shards(k):
        l = k // 2
        if k % 2 == 0:
            return [w_in[l].T.astype(BF16), w_out[l].astype(BF16)]
        return [w_gate[l].T.astype(BF16), w_up[l].T.astype(BF16), w_down[l].astype(BF16)]

    gathers, ahead = {}, 2
    for k in range(min(ahead + 1, 2 * depth)):
        gathers[k] = _send_start("gather", shards(k), None, gathers[k - 1][3] if k else meta_g, f"gather_start_g{k}")

    passing = {}

    def pass_on(k, after):
        lands = _send_wait("gather", gathers.pop(k), after, f"gather_wait_g{k}")
        passing[k] = _send_start("forward", [], lands, after, f"forward_start_g{k}")

    def take(k, h):
        after = h
        if k >= 1 and k + ahead < 2 * depth:
            gathers[k + ahead] = _send_start("gather", shards(k + ahead), None, h, f"gather_start_g{k + ahead}")
            after = gathers[k + ahead][3]
        elif k == 0:
            after = gathers[max(gathers)][3]
        if k not in passing:
            pass_on(k, after)
        if k >= 1 and k + 1 < 2 * depth:
            pass_on(k + 1, after)
        last = passing[k + 1][3] if k + 1 in passing else passing[k][3]
        return _send_wait("forward", passing.pop(k), last, f"forward_wait_g{k}")

    def mix_weights_fn(l, h):
        wi_t, wo = take(2 * l, h)
        return wi_t.reshape(IN_COLS, d), wo.reshape(d, d), zero

    def ffn_weights_fn(l, h):
        wg_t, wu_t, wd = take(2 * l + 1, h)
        return wg_t.reshape(D_FF, d), wu_t.reshape(D_FF, d), wd.reshape(D_FF, d), zero

    exchanges, adam, order = {}, {}, []
    tr = lambda *ts: tuple(jnp.swapaxes(t, 1, 2) for t in ts)
    big = dict(wi=tr(w_in, m_w_in, v_w_in), wg=tr(w_gate, m_w_gate, v_w_gate), wu=tr(w_up, m_w_up, v_w_up), wd=(w_down, m_w_down, v_w_down),
               wo=(w_out, m_w_out, v_w_out))
    kinds = dict(ffn=("wg", "wu", "wd"), mix=("wi", "wo"))

    arrivals = []

    def finish(key, after):
        l, part = key
        arrivals.append((key, _send_wait("scatter", exchanges.pop(key), after, f"exchange_wait_{part}_l{l}")))

    def update():
        last_start = exchanges[order[-1]][3] if order[-1] in exchanges else None
        while arrivals:
            (l, part), arrived = arrivals.pop(0)
            for kind, parts in zip(kinds[part], arrived):
                adam[kind] = _adamw(parts, *big[kind], f"adamw_{kind}_l{l}", layer=l, prev=adam.get(kind),
                                    after=None if kind in adam else last_start)

    def grads_fn(l, part, gw, after):
        packed = [gw[kind].reshape(N_DEV, -1, d) for kind in kinds[part]]
        exchanges[(l, part)] = _send_start("scatter", packed, None, after, f"exchange_start_{part}_l{l}")
        order.append((l, part))
        token = exchanges[(l, part)][3]
        if len(order) > 2:
            finish(order[-3], token)
        return token[0, 0]

    loss_part, dh, gs = _local_step(x[0], meta, loss_target[0], mix_weights_fn, ffn_weights_fn, grads_fn, attn_sink, ret_decay_fwd, ret_decay_bwd, ret_norm,
                                    norm_mix_pre, norm_mix_post, norm_ffn_pre, norm_ffn_post)
    grad_x = dh[BLOCK:][None]

    st = lambda xs: jnp.stack([t.reshape(-1) if t.ndim == 1 else t[0] for t in xs])
    small = _pack_small(st(gs["mix_pre"]), st(gs["mix_post"]), st(gs["ffn_pre"]), st(gs["ffn_post"]), st(gs["ret_norm"]), st(gs["sink"]),
                        st(gs["dec_f"]), st(gs["dec_b"]), loss_part, dh[PAD_FRONT:BLOCK])
    update()
    small_g, = _allgather([small], "gather_small", after=adam["wo"][0])
    for key in order[-2:]:
        finish(key, small_g)
    update()
    o_wi, o_wo, o_wg, o_wu, o_wd = tr(*adam["wi"]), adam["wo"], tr(*adam["wg"]), tr(*adam["wu"]), adam["wd"]
    zmeta = jnp.zeros((N_META, d), F32)
    packs = [_pack_small(a[0], a[1], a[2], a[3], a[4], a[5], a[6], a[7], zero, zmeta) for a in (
        (norm_mix_pre, norm_mix_post, norm_ffn_pre, norm_ffn_post, ret_norm, attn_sink, ret_decay_fwd, ret_decay_bwd),
        (m_norm_mix_pre, m_norm_mix_post, m_norm_ffn_pre, m_norm_ffn_post, m_ret_norm, m_attn_sink, m_ret_decay_fwd, m_ret_decay_bwd),
        (v_norm_mix_pre, v_norm_mix_post, v_norm_ffn_pre, v_norm_ffn_post, v_ret_norm, v_attn_sink, v_ret_decay_fwd, v_ret_decay_bwd))]
    o_small = [_unpack_small(o, depth) for o in _adamw(small_g, packs[0], packs[1], packs[2], "adamw_small")]
    meta_parts = lax.dynamic_slice(small_g, (0, ROW_META, me * (d // N_DEV)), (N_DEV, N_META, d // N_DEV))
    o_meta = _adamw(meta_parts, meta_tokens, m_meta_tokens, v_meta_tokens, "adamw_meta")

    outs = []
    for i in range(4):
        s = o_small[i]
        outs += [o_meta[i], o_wi[i], o_wo[i], s["sink"], s["dec_f"], s["dec_b"], s["ret_norm"], s["mix_pre"], s["mix_post"], o_wg[i], o_wu[i],
                 o_wd[i], s["ffn_pre"], s["ffn_post"]]
    return (o_small[0]["loss"], grad_x, *outs)
```

```python
import jax
import jax.numpy as jnp
import numpy as np
from jax import lax
from jax.experimental import pallas as pl
from jax.experimental.pallas import tpu as pltpu

F32, BF16 = jnp.float32, jnp.bfloat16

D_MODEL = 2048
N_META = 16
BLOCK = 128
WINDOW = 128
PAD_FRONT = BLOCK - N_META
ATT_HEAD_DIM = 128
ATT_WIDTH = D_MODEL // 2
ATT_HEADS = ATT_WIDTH // ATT_HEAD_DIM
ATT_KV_HEADS = 2
ATT_GROUP = ATT_HEADS // ATT_KV_HEADS
KV_WIDTH = ATT_KV_HEADS * ATT_HEAD_DIM
ROT_DIM = ATT_HEAD_DIM // 4
ROPE_THETA = 500000.0
RET_WIDTH = D_MODEL - ATT_WIDTH
RET_HEAD_DIM = 256
RET_HEADS = RET_WIDTH // RET_HEAD_DIM
RET_THETA = 10000.0
D_FF = 5632
IN_COLS = ATT_WIDTH + 2 * KV_WIDTH + 4 * RET_WIDTH
N_DEV = 8
SHARD_COLS = IN_COLS // N_DEV
EPS = 1e-6
NEG = -1e30
RET_K_SCALE = RET_HEAD_DIM ** -0.5
ATT_SCALE = ATT_HEAD_DIM ** -0.5

COL_AV256 = (ATT_WIDTH + KV_WIDTH) // 256
COL_RQ = (ATT_WIDTH + 2 * KV_WIDTH) // RET_HEAD_DIM
COL_RK = COL_RQ + RET_HEADS
COL_RV = COL_RK + RET_HEADS
COL_RG = COL_RV + RET_HEADS

ADAM_LR, ADAM_B1, ADAM_B2, ADAM_EPS, ADAM_WD, ADAM_STEP = 0.001, 0.9, 0.999, 1e-08, 0.01, 10

ROW_MIX_PRE, ROW_MIX_POST, ROW_FFN_PRE, ROW_FFN_POST, ROW_RET_NORM, ROW_SINK, ROW_DEC_F, ROW_DEC_B, ROW_LOSS, ROW_META, SMALL_ROWS = (
    0, 8, 16, 24, 32, 40, 48, 56, 64, 72, 96)
ADAMW_TILE_ELEMS = 128 * 1024

MESH = pl.DeviceIdType.MESH
ANY = pl.BlockSpec(memory_space=pl.ANY)


def _row_tile(n, cap):
    for t in range(cap - cap % 16, 0, -16):
        if n % t == 0:
            return t
    raise ValueError(n)


def _sds(shape, dtype):
    return jax.ShapeDtypeStruct(shape, dtype)


def _silu(x):
    return x * jax.nn.sigmoid(x)


def _dsilu(x):
    s = jax.nn.sigmoid(x)
    return s * (1.0 + x * (1.0 - s))


def _norm_fwd(x, g, res, out_dtype, name):
    n, d = x.shape
    tr = _row_tile(n, 384)

    def body(*refs):
        if res is None:
            x_ref, g_ref, o_ref = refs
        else:
            x_ref, g_ref, r_ref, o_ref = refs
        xv = x_ref[...]
        r = lax.rsqrt(jnp.mean(xv * xv, axis=-1, keepdims=True) + EPS)
        y = xv * r * g_ref[...]
        if res is not None:
            y = y + r_ref[...]
        o_ref[...] = y.astype(o_ref.dtype)

    row = pl.BlockSpec((tr, d), lambda i: (i, 0))
    ins = [row, pl.BlockSpec((1, d), lambda i: (0, 0))] + ([row] if res is not None else [])
    args = (x, g) + ((res,) if res is not None else ())
    return pl.pallas_call(body, grid=(n // tr,), in_specs=ins, out_specs=row, out_shape=_sds((n, d), out_dtype), name=name)(*args)


def _norm_res_norm(x, g, res, g_next, name):
    n, d = x.shape
    tr = _row_tile(n, 384)

    def body(x_ref, g_ref, r_ref, gn_ref, h_ref, u_ref):
        xv = x_ref[...]
        hv = r_ref[...] + xv * lax.rsqrt(jnp.mean(xv * xv, axis=-1, keepdims=True) + EPS) * g_ref[...]
        h_ref[...] = hv
        u_ref[...] = (hv * lax.rsqrt(jnp.mean(hv * hv, axis=-1, keepdims=True) + EPS) * gn_ref[...]).astype(u_ref.dtype)

    row = pl.BlockSpec((tr, d), lambda i: (i, 0))
    vec = pl.BlockSpec((1, d), lambda i: (0, 0))
    return pl.pallas_call(body, grid=(n // tr,), in_specs=[row, vec, row, vec], out_specs=(row, row),
                          out_shape=(_sds((n, d), F32), _sds((n, d), BF16)), name=name)(x, g, res, g_next)


OUT_PROJ_TOKEN_TILE = 192


def _out_proj_norms(x, w, res, g, g_next, name):
    n, k = x.shape
    d = w.shape[1]
    tm = _row_tile(n, OUT_PROJ_TOKEN_TILE)

    def body(x_ref, w_ref, r_ref, g_ref, gn_ref, mo_ref, h_ref, u_ref):
        mo = jnp.dot(x_ref[...], w_ref[...], preferred_element_type=F32)
        mo_ref[...] = mo
        hv = r_ref[...] + mo * lax.rsqrt(jnp.mean(mo * mo, axis=-1, keepdims=True) + EPS) * g_ref[...]
        h_ref[...] = hv
        u_ref[...] = (hv * lax.rsqrt(jnp.mean(hv * hv, axis=-1, keepdims=True) + EPS) * gn_ref[...]).astype(u_ref.dtype)

    row = pl.BlockSpec((tm, d), lambda i: (i, 0))
    vec = pl.BlockSpec((1, d), lambda i: (0, 0))
    return pl.pallas_call(body, grid=(n // tm,), in_specs=[pl.BlockSpec((tm, k), lambda i: (i, 0)), pl.BlockSpec((k, d), lambda i: (0, 0)), row, vec, vec],
                          out_specs=(row, row, row), out_shape=(_sds((n, d), F32), _sds((n, d), F32), _sds((n, d), BF16)), name=name)(
                              x, w, res, g, g_next)


def _norm_bwd(dy, x, g, res, out_dtype, name):
    n, d = x.shape
    tr = _row_tile(n, 384)

    def body(*refs):
        if res is None:
            dy_ref, x_ref, g_ref, dx_ref, dg_ref = refs
        else:
            dy_ref, x_ref, g_ref, r_ref, dx_ref, dg_ref = refs
        i = pl.program_id(0)
        xv = x_ref[...]
        r = lax.rsqrt(jnp.mean(xv * xv, axis=-1, keepdims=True) + EPS)
        xhat = xv * r
        dyf = dy_ref[...].astype(F32)
        gdy = dyf * g_ref[...]
        dx = r * (gdy - xhat * jnp.mean(gdy * xhat, axis=-1, keepdims=True))
        if res is not None:
            dx = dx + r_ref[...]
        dx_ref[...] = dx.astype(dx_ref.dtype)

        @pl.when(i == 0)
        def _():
            dg_ref[...] = jnp.zeros_like(dg_ref)

        dg_ref[...] += jnp.sum(dyf * xhat, axis=0, keepdims=True)

    row = pl.BlockSpec((tr, d), lambda i: (i, 0))
    vec = pl.BlockSpec((1, d), lambda i: (0, 0))
    ins = [row, row, vec] + ([row] if res is not None else [])
    args = (dy, x, g) + ((res,) if res is not None else ())
    return pl.pallas_call(body, grid=(n // tr,), in_specs=ins, out_specs=(row, vec),
                          out_shape=(_sds((n, d), out_dtype), _sds((1, d), F32)), name=name,
                          compiler_params=pltpu.CompilerParams(dimension_semantics=("arbitrary",)))(*args)


def _mm(a, b, *, ta, tb, grid, a_blk, a_map, b_blk, b_map, o_blk, o_map, o_shape, o_dtype, name, acc=None):
    nk = grid[2]
    dims = (((0,) if ta else (1,), (1,) if tb else (0,)), ((), ()))

    def body(*refs):
        if acc is None:
            a_ref, b_ref, o_ref = refs[:3]
            c_ref = None
        else:
            a_ref, b_ref, c_ref, o_ref = refs[:4]
        part = lax.dot_general(a_ref[...], b_ref[...], dims, preferred_element_type=F32)
        if nk == 1:
            if c_ref is not None:
                part = part + c_ref[...].astype(F32)
            o_ref[...] = part.astype(o_ref.dtype)
            return
        acc_ref = refs[-1]
        k = pl.program_id(2)

        @pl.when(k == 0)
        def _():
            acc_ref[...] = jnp.zeros_like(acc_ref) if c_ref is None else c_ref[...].astype(F32)

        acc_ref[...] += part

        @pl.when(k == nk - 1)
        def _():
            o_ref[...] = acc_ref[...].astype(o_ref.dtype)

    ins = [pl.BlockSpec(a_blk, a_map), pl.BlockSpec(b_blk, b_map)]
    args = [a, b]
    if acc is not None:
        ins.append(pl.BlockSpec(o_blk, o_map))
        args.append(acc)
    return pl.pallas_call(body, grid=grid, in_specs=ins, out_specs=pl.BlockSpec(o_blk, o_map), out_shape=_sds(o_shape, o_dtype),
                          scratch_shapes=[pltpu.VMEM(o_blk, F32)] if nk > 1 else [], name=name,
                          compiler_params=pltpu.CompilerParams(dimension_semantics=("parallel", "parallel", "arbitrary")))(*args)


TOKEN_TILE = 1056
WIDE_TILE = 1408
DEEP_K_TOKEN_TILE = 528


def _mm_nn(x, w, o_dtype, tn, tk, name, acc=None, tm_cap=TOKEN_TILE):
    n, k = x.shape
    tm = _row_tile(n, tm_cap)
    return _mm(x, w, ta=False, tb=False, grid=(n // tm, w.shape[1] // tn, k // tk), a_blk=(tm, tk), a_map=lambda i, j, kk: (i, kk),
               b_blk=(tk, tn), b_map=lambda i, j, kk: (kk, j), o_blk=(tm, tn), o_map=lambda i, j, kk: (i, j),
               o_shape=(n, w.shape[1]), o_dtype=o_dtype, name=name, acc=acc)


def _mm_nt(dy, w, o_dtype, tn, tk, name, acc=None):
    n, k = dy.shape
    tm = _row_tile(n, TOKEN_TILE)
    return _mm(dy, w, ta=False, tb=True, grid=(n // tm, w.shape[0] // tn, k // tk), a_blk=(tm, tk), a_map=lambda i, j, kk: (i, kk),
               b_blk=(tn, tk), b_map=lambda i, j, kk: (j, kk), o_blk=(tm, tn), o_map=lambda i, j, kk: (i, j),
               o_shape=(n, w.shape[0]), o_dtype=o_dtype, name=name, acc=acc)


def _mm_tn(x, dy, tm, tn, name):
    n, m = x.shape
    tk = _row_tile(n, 2 * TOKEN_TILE)
    return _mm(x, dy, ta=True, tb=False, grid=(m // tm, dy.shape[1] // tn, n // tk), a_blk=(tk, tm), a_map=lambda i, j, kk: (kk, i),
               b_blk=(tk, tn), b_map=lambda i, j, kk: (kk, j), o_blk=(tm, tn), o_map=lambda i, j, kk: (i, j),
               o_shape=(m, dy.shape[1]), o_dtype=BF16, name=name)


def _rope_tables(n):
    pos = (jnp.arange(n) - PAD_FRONT).astype(F32)
    half = ROT_DIM // 2
    ang = pos[:, None] * (ROPE_THETA ** (-jnp.arange(half, dtype=F32) / half))[None, :]
    c, s = jnp.cos(ang), jnp.sin(ang)
    rest = ATT_HEAD_DIM - ROT_DIM
    cos_a = jnp.concatenate([c, c, jnp.ones((n, rest), F32)], axis=1)
    sin_a = jnp.concatenate([-s, s, jnp.zeros((n, rest), F32)], axis=1)
    half = RET_HEAD_DIM // 2
    ang = pos[:, None] * (RET_THETA ** (-jnp.arange(half, dtype=F32) / half))[None, :]
    c, s = jnp.cos(ang), jnp.sin(ang)
    perm = np.zeros((ATT_HEAD_DIM, ATT_HEAD_DIM), np.float32)
    for i in range(ROT_DIM):
        perm[(i + ROT_DIM // 2) % ROT_DIM, i] = 1.0
    return cos_a, sin_a, jnp.concatenate([c, c], axis=1), jnp.concatenate([-s, s], axis=1), jnp.asarray(perm, BF16)


def _into(buf, own_shape):
    if buf is None:
        return _sds(own_shape, BF16), [], [], lambda n_inputs: {}
    return _sds(buf.shape, buf.dtype), [ANY], [buf], lambda n_inputs: {n_inputs: 0}


def _rope_att(x, col0, heads, cos, sin, perm, name, plain=None, into=None, out_col=0):
    n = x.shape[0]
    tr = _row_tile(n, 1056)
    hd = ATT_HEAD_DIM
    w2 = 0 if plain is None else plain.shape[1]
    width = heads * hd + w2

    def body(*refs):
        x_ref, c_ref, s_ref, p_ref, o_ref = refs[0], refs[1], refs[2], refs[3], refs[-1]
        for h in range(heads):
            cs = slice(h * hd, (h + 1) * hd)
            xb = x_ref[:, cs].astype(BF16)
            sw = jnp.dot(xb, p_ref[...], preferred_element_type=F32)
            o_ref[:, cs] = (xb.astype(F32) * c_ref[...] + sw * s_ref[...]).astype(o_ref.dtype)
        if plain is not None:
            o_ref[:, heads * hd:] = refs[4][...].astype(o_ref.dtype)

    tab = pl.BlockSpec((tr, hd), lambda i: (i, 0))
    ins = [pl.BlockSpec((tr, heads * hd), lambda i: (i, col0)), tab, tab, pl.BlockSpec((hd, hd), lambda i: (0, 0))]
    args = [x, cos, sin, perm]
    if plain is not None:
        ins.append(pl.BlockSpec((tr, w2), lambda i: (i, 0)))
        args.append(plain)
    shape, extra_specs, extra_args, alias = _into(into, (n, width))
    return pl.pallas_call(body, grid=(n // tr,), in_specs=ins + extra_specs, out_specs=pl.BlockSpec((tr, width), lambda i: (i, out_col)),
                          out_shape=shape, input_output_aliases=alias(len(ins)), name=name)(*args, *extra_args)


def _rope_ret(x, col0, cos, sin, name, into=None, out_col=0):
    p, n, _ = x.shape
    tr = _row_tile(n, 1056)
    hd = RET_HEAD_DIM

    def body(*refs):
        x_ref, o_ref = refs[0], refs[-1]
        for h in range(2):
            cs = slice(h * hd, (h + 1) * hd)
            xv = x_ref[0, :, cs].astype(F32)
            for q in range(1, p):
                xv = xv + x_ref[q, :, cs].astype(F32)
            if cos is not None:
                sw = jnp.concatenate([xv[:, hd // 2:], xv[:, :hd // 2]], axis=1)
                xv = xv * refs[1][...] + sw * refs[2][...]
            o_ref[:, cs] = xv.astype(o_ref.dtype)

    tab = pl.BlockSpec((tr, hd), lambda i, j: (i, 0))
    ins = [pl.BlockSpec((p, tr, 2 * hd), lambda i, j: (0, i, col0 + j))] + ([tab, tab] if cos is not None else [])
    args = (x,) + ((cos, sin) if cos is not None else ())
    shape, extra_specs, extra_args, alias = _into(into, (n, RET_WIDTH))
    return pl.pallas_call(body, grid=(n // tr, RET_HEADS // 2), in_specs=ins + extra_specs,
                          out_specs=pl.BlockSpec((tr, 2 * hd), lambda i, j: (i, out_col + j)), out_shape=shape,
                          input_output_aliases=alias(len(ins)), name=name)(*args, *extra_args)


def _att_mask(nblk, n_tot):
    row = lax.broadcasted_iota(jnp.int32, (BLOCK, 4 * BLOCK), 0)
    col = lax.broadcasted_iota(jnp.int32, (BLOCK, 4 * BLOCK), 1)
    qi = nblk * BLOCK + row
    seg = col // BLOCK
    cj = col % BLOCK
    kj = (nblk - 1 + seg) * BLOCK + cj
    band = (jnp.abs(qi - kj) <= WINDOW) & (kj >= PAD_FRONT) & (kj < n_tot) & (seg < 3)
    meta = (seg == 3) & (cj >= PAD_FRONT) & (jnp.abs(qi - cj) > WINDOW)
    return band | meta


def _att_specs(nb, v_col):
    kv = lambda f, cb: pl.BlockSpec((BLOCK, KV_WIDTH), lambda n: (f(n), cb))
    prev, own, nxt, first = (lambda n: jnp.maximum(n - 1, 0)), (lambda n: n), (lambda n: jnp.minimum(n + 1, nb - 1)), (lambda n: 0)
    return [kv(f, 0) for f in (prev, own, nxt, first)] + [kv(f, v_col) for f in (prev, own, nxt, first)]


def _att_probs(s, ok, snk):
    s = jnp.where(ok, s, NEG)
    m = jnp.maximum(jnp.max(s, axis=-1, keepdims=True), snk)
    p = jnp.exp(s - m)
    ps = jnp.exp(snk - m)
    inv = 1.0 / (jnp.sum(p, axis=-1, keepdims=True) + ps)
    return p * inv, ps * inv


def _att_fwd(q, k, proj, sink_b, name):
    n = q.shape[0]
    nb = n // BLOCK
    hd = ATT_HEAD_DIM

    def body(q_ref, kp, ko, kn, km, vp, vo, vn, vm, sink_ref, o_ref):
        nblk = pl.program_id(0)
        ok = _att_mask(nblk, n)
        keep = (nblk * BLOCK + lax.broadcasted_iota(jnp.int32, (BLOCK, 1), 0)) >= PAD_FRONT
        for kh in range(ATT_KV_HEADS):
            cs = slice(kh * hd, (kh + 1) * hd)
            kk = jnp.concatenate([r[:, cs] for r in (kp, ko, kn, km)], axis=0)
            vv = jnp.concatenate([r[:, cs] for r in (vp, vo, vn, vm)], axis=0)
            heads = [kh * ATT_GROUP + g for g in range(ATT_GROUP)]
            q4 = jnp.concatenate([q_ref[:, h * hd:(h + 1) * hd] for h in heads], axis=0)
            s = lax.dot_general(q4, kk, (((1,), (1,)), ((), ())), preferred_element_type=F32) * ATT_SCALE
            ps = []
            for g, h in enumerate(heads):
                p, _ = _att_probs(s[g * BLOCK:(g + 1) * BLOCK], ok, sink_ref[h:h + 1, 0:1])
                ps.append(p)
            o = jnp.dot(jnp.concatenate(ps, axis=0).astype(BF16), vv, preferred_element_type=F32)
            for g, h in enumerate(heads):
                o_ref[:, h * hd:(h + 1) * hd] = jnp.where(keep, o[g * BLOCK:(g + 1) * BLOCK], 0.0).astype(o_ref.dtype)

    qspec = pl.BlockSpec((BLOCK, ATT_WIDTH), lambda i: (i, 0))
    return pl.pallas_call(body, grid=(nb,), in_specs=[qspec] + _att_specs(nb, COL_AV256) + [pl.BlockSpec((ATT_HEADS, 128), lambda i: (0, 0))],
                          out_specs=qspec, out_shape=_sds((n, D_MODEL), BF16), name=name)(q, k, k, k, k, proj, proj, proj, proj, sink_b)


def _att_bwd(q, k, proj, sink_b, dmixed, name):
    n = q.shape[0]
    nb = n // BLOCK
    hd = ATT_HEAD_DIM

    def body(q_ref, kp, ko, kn, km, vp, vo, vn, vm, sink_ref, do_ref, dq_ref, dk_ref, dv_ref, dsink_ref):
        nblk = pl.program_id(0)

        @pl.when(nblk == 0)
        def _():
            dk_ref[...] = jnp.zeros_like(dk_ref)
            dv_ref[...] = jnp.zeros_like(dv_ref)
            dsink_ref[...] = jnp.zeros_like(dsink_ref)

        ok = _att_mask(nblk, n)
        rows = [jnp.maximum(nblk - 1, 0), nblk, jnp.minimum(nblk + 1, nb - 1), 0]
        for kh in range(ATT_KV_HEADS):
            cs = slice(kh * hd, (kh + 1) * hd)
            kk = jnp.concatenate([r[:, cs] for r in (kp, ko, kn, km)], axis=0)
            vv = jnp.concatenate([r[:, cs] for r in (vp, vo, vn, vm)], axis=0)
            heads = [kh * ATT_GROUP + g for g in range(ATT_GROUP)]
            q4 = jnp.concatenate([q_ref[:, h * hd:(h + 1) * hd] for h in heads], axis=0)
            do4 = jnp.concatenate([do_ref[:, h * hd:(h + 1) * hd] for h in heads], axis=0)
            s = lax.dot_general(q4, kk, (((1,), (1,)), ((), ())), preferred_element_type=F32) * ATT_SCALE
            dp = lax.dot_general(do4, vv, (((1,), (1,)), ((), ())), preferred_element_type=F32)
            ps, dss = [], []
            for g, h in enumerate(heads):
                p, psink = _att_probs(s[g * BLOCK:(g + 1) * BLOCK], ok, sink_ref[h:h + 1, 0:1])
                dpg = dp[g * BLOCK:(g + 1) * BLOCK]
                delta = jnp.sum(p * dpg, axis=-1, keepdims=True)
                ps.append(p)
                dss.append(p * (dpg - delta) * ATT_SCALE)
                dsink_ref[h:h + 1, :] = dsink_ref[h:h + 1, :] - jnp.sum(psink * delta, axis=0, keepdims=True)
            ds = jnp.concatenate(dss, axis=0).astype(BF16)
            pb = jnp.concatenate(ps, axis=0).astype(BF16)
            dq = jnp.dot(ds, kk, preferred_element_type=F32)
            for g, h in enumerate(heads):
                dq_ref[:, h * hd:(h + 1) * hd] = dq[g * BLOCK:(g + 1) * BLOCK].astype(dq_ref.dtype)
            dk = lax.dot_general(ds, q4, (((0,), (0,)), ((), ())), preferred_element_type=F32)
            dv = lax.dot_general(pb, do4, (((0,), (0,)), ((), ())), preferred_element_type=F32)
            for seg, r in enumerate(rows):
                at = (pl.ds(pl.multiple_of(r * BLOCK, BLOCK), BLOCK), cs)
                dk_ref[at] += dk[seg * BLOCK:(seg + 1) * BLOCK]
                dv_ref[at] += dv[seg * BLOCK:(seg + 1) * BLOCK]

    qspec = pl.BlockSpec((BLOCK, ATT_WIDTH), lambda i: (i, 0))
    whole = pl.BlockSpec((n, KV_WIDTH), lambda i: (0, 0))
    sinks = pl.BlockSpec((ATT_HEADS, 128), lambda i: (0, 0))
    return pl.pallas_call(body, grid=(nb,), in_specs=[qspec] + _att_specs(nb, COL_AV256) + [sinks, qspec], out_specs=(qspec, whole, whole, sinks),
                          out_shape=(_sds((n, ATT_WIDTH), BF16), _sds((n, KV_WIDTH), F32), _sds((n, KV_WIDTH), F32), _sds((ATT_HEADS, 128), F32)),
                          name=name, compiler_params=pltpu.CompilerParams(dimension_semantics=("arbitrary",)))(
                              q, k, k, k, k, proj, proj, proj, proj, sink_b, dmixed)


def _ret_decay(lg, d):
    a = lax.broadcasted_iota(jnp.int32, (BLOCK, 1), 0)
    b = lax.broadcasted_iota(jnp.int32, (1, BLOCK), 1)
    t_col = a + d * (BLOCK - 1 - 2 * a)
    t_row = b + d * (BLOCK - 1 - 2 * b)
    diff = t_col - t_row
    dist = jnp.maximum(diff, 0).astype(F32)
    dmask = jnp.where(diff >= d, jnp.exp(lg * dist), 0.0)
    tf = t_col.astype(F32)
    xi = jnp.exp(lg * (tf + 1.0))
    zeta = jnp.exp(lg * (BLOCK - 1.0 - tf))
    gam = jnp.exp(jnp.full((1, 1), BLOCK, F32) * lg)
    return dmask, dist, xi, zeta, gam, tf


def _ret_fwd(q, k, proj, lg, name):
    n = q.shape[0]
    nc = n // BLOCK
    hd = RET_HEAD_DIM
    chunk = lambda d, c: c + d * (nc - 1 - 2 * c)

    def body(lg_ref, q_ref, k_ref, v0, v1, v2, v3, o_ref, st_ref, s_ref):
        d, c = pl.program_id(0), pl.program_id(1)

        @pl.when(c == 0)
        def _():
            s_ref[...] = jnp.zeros_like(s_ref)

        for h, v_ref in enumerate((v0, v1, v2, v3)):
            cs = slice(h * hd, (h + 1) * hd)
            dmask, _, xi, zeta, gam, _ = _ret_decay(lg_ref[h, d], d)
            qv = q_ref[:, cs]
            kf = k_ref[:, cs].astype(F32) * RET_K_SCALE
            vv = v_ref[...]
            s = lax.dot_general(qv, kf.astype(BF16), (((1,), (1,)), ((), ())), preferred_element_type=F32)
            sb = s_ref[h]
            o_ref[:, cs] = (jnp.dot((s * dmask).astype(BF16), vv, preferred_element_type=F32)
                            + jnp.dot((qv.astype(F32) * xi).astype(BF16), sb.astype(BF16), preferred_element_type=F32))
            st_ref[h] = sb
            s_ref[h] = gam * sb + lax.dot_general((kf * zeta).astype(BF16), vv, (((0,), (0,)), ((), ())), preferred_element_type=F32)

    wide = pl.BlockSpec((BLOCK, RET_WIDTH), lambda d, c: (chunk(d, c), 0))
    vblk = lambda h: pl.BlockSpec((BLOCK, hd), lambda d, c: (chunk(d, c), COL_RV + h))
    return pl.pallas_call(
        body, grid=(2, nc), in_specs=[pl.BlockSpec(memory_space=pltpu.SMEM), wide, wide] + [vblk(h) for h in range(RET_HEADS)],
        out_specs=(pl.BlockSpec((None, BLOCK, RET_WIDTH), lambda d, c: (d, chunk(d, c), 0)),
                   pl.BlockSpec((RET_HEADS, None, None, hd, hd), lambda d, c: (0, d, c, 0, 0))),
        out_shape=(_sds((2, n, RET_WIDTH), F32), _sds((RET_HEADS, 2, nc, hd, hd), F32)), scratch_shapes=[pltpu.VMEM((RET_HEADS, hd, hd), F32)],
        name=name, compiler_params=pltpu.CompilerParams(dimension_semantics=("parallel", "arbitrary")))(lg, q, k, proj, proj, proj, proj)


def _ret_bwd(q, k, proj, lg, do, states, name):
    n = q.shape[0]
    nc = n // BLOCK
    hd = RET_HEAD_DIM
    chunk = lambda d, r: (nc - 1 - r) + d * (2 * r - (nc - 1))

    def body(lg_ref, q_ref, k_ref, v0, v1, v2, v3, do_ref, st_ref, dq_ref, dk_ref, dv_ref, dlg_ref, ds_ref):
        d, r = pl.program_id(0), pl.program_id(1)

        @pl.when(r == 0)
        def _():
            ds_ref[...] = jnp.zeros_like(ds_ref)
            dlg_ref[...] = jnp.zeros_like(dlg_ref)

        row = lax.broadcasted_iota(jnp.int32, (BLOCK, 1), 0) + chunk(d, r) * BLOCK
        keep = row >= PAD_FRONT
        nt = (((1,), (1,)), ((), ()))
        tn = (((0,), (0,)), ((), ()))
        for h, v_ref in enumerate((v0, v1, v2, v3)):
            cs = slice(h * hd, (h + 1) * hd)
            dmask, dist, xi, zeta, gam, tf = _ret_decay(lg_ref[h, d], d)
            qv, vv, dov = q_ref[:, cs], v_ref[...], do_ref[:, cs]
            qf = qv.astype(F32)
            kf = k_ref[:, cs].astype(F32) * RET_K_SCALE
            kb = kf.astype(BF16)
            sc = st_ref[h]
            dsn = ds_ref[h]
            s = lax.dot_general(qv, kb, nt, preferred_element_type=F32)
            dsc = lax.dot_general(dov, vv, nt, preferred_element_type=F32) * dmask
            dsb = dsc.astype(BF16)
            dq_c = xi * lax.dot_general(dov, sc.astype(BF16), nt, preferred_element_type=F32)
            dk_c = zeta * lax.dot_general(vv, dsn.astype(BF16), nt, preferred_element_type=F32)
            dq = jnp.dot(dsb, kb, preferred_element_type=F32) + dq_c
            dk = lax.dot_general(dsb, qv, tn, preferred_element_type=F32) + dk_c
            dv = (lax.dot_general((s * dmask).astype(BF16), dov, tn, preferred_element_type=F32)
                  + jnp.dot((kf * zeta).astype(BF16), dsn.astype(BF16), preferred_element_type=F32))
            ds_ref[h] = gam * dsn + lax.dot_general((qf * xi).astype(BF16), dov, tn, preferred_element_type=F32)
            dlg = (jnp.sum(dsc * s * dist, keepdims=True)
                   + jnp.sum((tf + 1.0) * jnp.sum(qf * dq_c, axis=-1, keepdims=True), keepdims=True)
                   + jnp.sum((BLOCK - 1.0 - tf) * jnp.sum(kf * dk_c, axis=-1, keepdims=True), keepdims=True)
                   + BLOCK * gam * jnp.sum(dsn * sc, keepdims=True))
            dlg_ref[h] += dlg
            dq_ref[:, cs] = dq
            dk_ref[:, cs] = jnp.where(keep, dk * RET_K_SCALE, 0.0)
            dv_ref[:, cs] = jnp.where(keep, dv, 0.0)

    wide = pl.BlockSpec((BLOCK, RET_WIDTH), lambda d, r: (chunk(d, r), 0))
    vblk = lambda h: pl.BlockSpec((BLOCK, hd), lambda d, r: (chunk(d, r), COL_RV + h))
    plane = pl.BlockSpec((None, BLOCK, RET_WIDTH), lambda d, r: (d, chunk(d, r), 0))
    return pl.pallas_call(
        body, grid=(2, nc),
        in_specs=[pl.BlockSpec(memory_space=pltpu.SMEM), wide, wide] + [vblk(h) for h in range(RET_HEADS)]
        + [wide, pl.BlockSpec((RET_HEADS, None, None, hd, hd), lambda d, r: (0, d, nc - 1 - r, 0, 0))],
        out_specs=(plane, plane, plane, pl.BlockSpec((RET_HEADS, None, 8, 128), lambda d, r: (0, d, 0, 0))),
        out_shape=(_sds((2, n, RET_WIDTH), F32),) * 3 + (_sds((RET_HEADS, 2, 8, 128), F32),),
        scratch_shapes=[pltpu.VMEM((RET_HEADS, hd, hd), F32)], name=name,
        compiler_params=pltpu.CompilerParams(dimension_semantics=("parallel", "arbitrary")))(lg, q, k, proj, proj, proj, proj, do, states)


def _retgate_fwd(o, proj, gain, mixed, name):
    _, n, _ = o.shape
    tr = _row_tile(n, 1056)
    hd = RET_HEAD_DIM

    def body(o_ref, rg_ref, g_ref, _, y_ref):
        ov = o_ref[0] + o_ref[1]
        r = lax.rsqrt(jnp.mean(ov * ov, axis=-1, keepdims=True) + EPS)
        y_ref[...] = (_silu(rg_ref[...].astype(F32)) * (ov * r * g_ref[...])).astype(y_ref.dtype)

    return pl.pallas_call(body, grid=(n // tr, RET_HEADS),
                          in_specs=[pl.BlockSpec((2, tr, hd), lambda i, h: (0, i, h)), pl.BlockSpec((tr, hd), lambda i, h: (i, COL_RG + h)),
                                    pl.BlockSpec((1, hd), lambda i, h: (0, h)), ANY],
                          out_specs=pl.BlockSpec((tr, hd), lambda i, h: (i, ATT_WIDTH // hd + h)), out_shape=_sds(mixed.shape, mixed.dtype),
                          input_output_aliases={3: 0}, name=name)(o, proj, gain, mixed)


def _retgate_bwd(dmixed, o, proj, gain, name):
    _, n, _ = o.shape
    tr = _row_tile(n, 1056)
    hd = RET_HEAD_DIM

    def body(dy_ref, o_ref, rg_ref, g_ref, do_ref, drg_ref, dg_ref):
        i = pl.program_id(1)
        ov = o_ref[0] + o_ref[1]
        r = lax.rsqrt(jnp.mean(ov * ov, axis=-1, keepdims=True) + EPS)
        xhat = ov * r
        rg = rg_ref[...].astype(F32)
        dy = dy_ref[...].astype(F32)
        drg_ref[...] = (dy * (xhat * g_ref[...]) * _dsilu(rg)).astype(drg_ref.dtype)
        dn = dy * _silu(rg)
        dxh = dn * g_ref[...]
        do_ref[...] = (r * (dxh - xhat * jnp.mean(dxh * xhat, axis=-1, keepdims=True))).astype(do_ref.dtype)

        @pl.when(i == 0)
        def _():
            dg_ref[...] = jnp.zeros_like(dg_ref)

        dg_ref[...] += jnp.sum(dn * xhat, axis=0, keepdims=True)

    tile = pl.BlockSpec((tr, hd), lambda h, i: (i, h))
    vec = pl.BlockSpec((1, hd), lambda h, i: (0, h))
    rg_cols = pl.BlockSpec((tr, hd), lambda h, i: (i, COL_RG + h))
    return pl.pallas_call(body, grid=(RET_HEADS, n // tr),
                          in_specs=[pl.BlockSpec((tr, hd), lambda h, i: (i, ATT_WIDTH // hd + h)), pl.BlockSpec((2, tr, hd), lambda h, i: (0, i, h)),
                                    rg_cols, vec],
                          out_specs=(tile, rg_cols, vec), out_shape=(_sds((n, RET_WIDTH), BF16), _sds((n, IN_COLS), BF16), _sds((1, RET_WIDTH), F32)),
                          name=name, compiler_params=pltpu.CompilerParams(dimension_semantics=("parallel", "arbitrary")))(dmixed, o, proj, gain)


FFN_TILE = 512


def _swiglu_fwd(x, wg_t, wu_t, name):
    n, k = x.shape
    tm = _row_tile(n, TOKEN_TILE)
    nt = (((1,), (1,)), ((), ()))

    def body(x_ref, g_ref, u_ref, go_ref, uo_ref, f_ref):
        g = lax.dot_general(x_ref[...], g_ref[...], nt, preferred_element_type=F32)
        u = lax.dot_general(x_ref[...], u_ref[...], nt, preferred_element_type=F32)
        go_ref[...] = g.astype(go_ref.dtype)
        uo_ref[...] = u.astype(uo_ref.dtype)
        f_ref[...] = (_silu(g) * u).astype(f_ref.dtype)

    w = pl.BlockSpec((FFN_TILE, k), lambda i, j: (j, 0))
    o = pl.BlockSpec((tm, FFN_TILE), lambda i, j: (i, j))
    return pl.pallas_call(body, grid=(n // tm, wg_t.shape[0] // FFN_TILE), in_specs=[pl.BlockSpec((tm, k), lambda i, j: (i, 0)), w, w],
                          out_specs=(o, o, o), out_shape=(_sds((n, wg_t.shape[0]), BF16),) * 3, name=name,
                          compiler_params=pltpu.CompilerParams(dimension_semantics=("parallel", "parallel")))(x, wg_t, wu_t)


def _swiglu_bwd(dy, wd, gate, up, name):
    n, k = dy.shape
    tm = _row_tile(n, TOKEN_TILE)
    nt = (((1,), (1,)), ((), ()))

    def body(dy_ref, w_ref, g_ref, u_ref, dg_ref, du_ref):
        df = lax.dot_general(dy_ref[...], w_ref[...], nt, preferred_element_type=F32)
        g = g_ref[...].astype(F32)
        dg_ref[...] = (df * u_ref[...].astype(F32) * _dsilu(g)).astype(dg_ref.dtype)
        du_ref[...] = (df * _silu(g)).astype(du_ref.dtype)

    o = pl.BlockSpec((tm, FFN_TILE), lambda i, j: (i, j))
    return pl.pallas_call(body, grid=(n // tm, wd.shape[0] // FFN_TILE),
                          in_specs=[pl.BlockSpec((tm, k), lambda i, j: (i, 0)), pl.BlockSpec((FFN_TILE, k), lambda i, j: (j, 0)), o, o],
                          out_specs=(o, o), out_shape=(_sds((n, wd.shape[0]), BF16),) * 2, name=name,
                          compiler_params=pltpu.CompilerParams(dimension_semantics=("parallel", "parallel")))(dy, wd, gate, up)


def _loss_head(h, target, name):
    n, d = h.shape
    nb = n // BLOCK

    def body(h_ref, t_ref, dh_ref, l_ref):
        i = pl.program_id(0)

        @pl.when(i == 0)
        def _():
            l_ref[...] = jnp.zeros_like(l_ref)
            dh_ref[...] = jnp.zeros_like(dh_ref)

        @pl.when(i > 0)
        def _():
            e = h_ref[...] - t_ref[...]
            dh_ref[...] = e * (1.0 / d)
            l_ref[...] += 0.5 * jnp.sum(jnp.mean(e * e, axis=-1, keepdims=True), keepdims=True)

    blk = pl.BlockSpec((BLOCK, d), lambda i: (i, 0))
    return pl.pallas_call(body, grid=(nb,), in_specs=[blk, pl.BlockSpec((BLOCK, d), lambda i: (jnp.maximum(i - 1, 0), 0))],
                          out_specs=(blk, pl.BlockSpec((8, 128), lambda i: (0, 0))), out_shape=(_sds((n, d), F32), _sds((8, 128), F32)), name=name,
                          compiler_params=pltpu.CompilerParams(dimension_semantics=("arbitrary",)))(h, target)


def _adamw(parts, w, m, v, name, sel=None, layer=None, prev=None, after=None):
    s, (r, c) = parts.shape[0], parts.shape[-2:]
    tr = _row_tile(r, max(16, (ADAMW_TILE_ELEMS // c) // 16 * 16))
    b1c, b2c = 1.0 - ADAM_B1 ** ADAM_STEP, 1.0 - ADAM_B2 ** ADAM_STEP

    def body(p_ref, w_ref, m_ref, v_ref, *rest):
        g_ref, d_ref, mo_ref, vo_ref = rest[-4:]
        g = p_ref[0].astype(F32)
        for q in range(1, s):
            g = g + p_ref[q].astype(F32)
        mn = ADAM_B1 * m_ref[...] + (1.0 - ADAM_B1) * g
        vn = ADAM_B2 * v_ref[...] + (1.0 - ADAM_B2) * jnp.square(g)
        g_ref[...] = g
        mo_ref[...] = mn
        vo_ref[...] = vn
        d_ref[...] = -ADAM_LR * ((mn / b1c) / (jnp.sqrt(vn / b2c) + ADAM_EPS) + ADAM_WD * w_ref[...])

    pspec = (pl.BlockSpec((s, tr, c), lambda i: (0, i, 0)) if sel is None else pl.BlockSpec((s, None, tr, c), lambda i: (0, sel, i, 0)))
    if layer is None:
        t = pl.BlockSpec((tr, c), lambda i: (i, 0))
        return pl.pallas_call(body, grid=(r // tr,), in_specs=[pspec, t, t, t], out_specs=(t, t, t, t), out_shape=(_sds((r, c), F32),) * 4,
                              name=name)(parts, w, m, v)
    t = pl.BlockSpec((None, tr, c), lambda i: (layer, i, 0))
    prev = prev if prev is not None else tuple(lax.empty(w.shape, F32) for _ in range(4))
    extra = [] if after is None else [after]
    return pl.pallas_call(body, grid=(r // tr,), in_specs=[pspec, t, t, t] + [ANY] * (4 + len(extra)), out_specs=(t, t, t, t),
                          out_shape=(_sds(w.shape, F32),) * 4, input_output_aliases={4 + i: i for i in range(4)}, name=name)(
                              parts, w, m, v, *prev, *extra)


def _allgather(xs, name, after=None):
    na = len(xs)
    first_out = na + (after is not None)

    def body(*refs):
        x_refs, o_refs = refs[:na], refs[first_out:first_out + na]
        send, recv, lsem = refs[first_out + na:]
        x, y, c = lax.axis_index("x"), lax.axis_index("y"), lax.axis_index("c")
        me, sib = (x, y, c), (x, y, 1 - c)
        chips = [(1 - x, y), (x, 1 - y), (1 - x, 1 - y)]
        slot = lambda p: 4 * p[0] + 2 * p[1] + p[2]

        def copy(a, k, block, to, src=None):
            dst = o_refs[a].at[slot(block)]
            return pltpu.make_async_remote_copy(src_ref=dst if src is None else src, dst_ref=dst, send_sem=send.at[a, k], recv_sem=recv.at[a, k],
                                                device_id=to, device_id_type=MESH)

        mine = [pltpu.make_async_copy(x_refs[a], o_refs[a].at[slot(me)], lsem.at[a]) for a in range(na)]
        for cp in mine:
            cp.start()
        first = []
        for a in range(na):
            first.append(copy(a, 0, me, sib, src=x_refs[a]))
            first += [copy(a, 1 + j, me, (*chip, c), src=x_refs[a]) for j, chip in enumerate(chips)]
        for cp in first:
            cp.start()
        passed = []
        for j, chip in enumerate(chips):
            for a in range(na):
                copy(a, 1 + j, (*chip, c), me).wait_recv()
                passed.append(copy(a, 4 + j, (*chip, c), sib))
                passed[-1].start()
        for a in range(na):
            copy(a, 0, sib, me).wait_recv()
            for j, chip in enumerate(chips):
                copy(a, 4 + j, (*chip, 1 - c), me).wait_recv()
        for cp in first + passed:
            cp.wait_send()
        for cp in mine:
            cp.wait()

    extra = [] if after is None else [after]
    return pl.pallas_call(body, in_specs=[ANY] * (na + len(extra)), out_specs=[ANY] * na,
                          out_shape=[_sds((N_DEV,) + t.shape, t.dtype) for t in xs],
                          scratch_shapes=[pltpu.SemaphoreType.DMA((na, 7)), pltpu.SemaphoreType.DMA((na, 7)), pltpu.SemaphoreType.DMA((na,))],
                          name=name)(*xs, *extra)


HBM = pl.BlockSpec(memory_space=pltpu.HBM)
SEM = pl.BlockSpec(memory_space=pltpu.SEMAPHORE)
EFFECT = pltpu.SideEffectType.DATAFLOW_SIDE_EFFECTING


SPLIT_RELATIONS = dict(gather=(1, 2, 4, 6),
                       forward=(2, 4, 6),
                       scatter=tuple(range(1, N_DEV)))


def _split_copies(mode, x_refs, land_refs, send, recv, own, landing):
    x, y, c = lax.axis_index("x"), lax.axis_index("y"), lax.axis_index("c")
    flip = lambda r: ((1 - x if r & 4 else x), (1 - y if r & 2 else y), (1 - c if r & 1 else c))
    slot = lambda p: 4 * p[0] + 2 * p[1] + p[2]
    me = slot((x, y, c))
    rel = SPLIT_RELATIONS[mode]
    local, remote = [], []
    for a in range(len(land_refs)):
        if mode != "forward":
            local.append(pltpu.make_async_copy(x_refs[a].at[me] if mode == "scatter" else x_refs[a], land_refs[a].at[me], own.at[a]))
        for j, r in enumerate(rel):
            if mode == "forward":
                to = flip(1)
                src = dst = land_refs[a].at[slot(flip(r ^ 1) if landing else flip(r))]
            else:
                to = flip(r)
                src = x_refs[a].at[slot(to)] if mode == "scatter" else x_refs[a]
                dst = land_refs[a].at[slot(to) if landing else me]
            remote.append(pltpu.make_async_remote_copy(src_ref=src, dst_ref=dst, send_sem=send.at[len(rel) * a + j],
                                                       recv_sem=recv.at[len(rel) * a + j], device_id=to, device_id_type=MESH))
    return local, remote


def _send_start(mode, xs, lands, after, name):
    if lands is None:
        lands = [lax.empty(t.shape if mode == "scatter" else (N_DEV,) + t.shape, t.dtype) for t in xs]
    nx, na, nr = len(xs), len(lands), len(SPLIT_RELATIONS[mode])
    nsem = 2 if mode == "forward" else 3

    def body(*refs):
        x_refs, land_refs = refs[:nx], refs[nx:nx + na]
        sems = refs[nx + na + 1:nx + na + 1 + nsem]
        local, remote = _split_copies(mode, x_refs, land_refs, sems[0], sems[1], sems[2] if nsem == 3 else None, False)
        for cp in remote + local:
            cp.start()
        refs[-1][...] = jnp.zeros_like(refs[-1])

    hbm = lambda t: pltpu.with_memory_space_constraint(t, pltpu.HBM)
    sem_shapes = [pltpu.SemaphoreType.DMA((nr * na,)), pltpu.SemaphoreType.DMA((nr * na,)), pltpu.SemaphoreType.DMA((na,))][:nsem]
    outs = pl.pallas_call(
        body, name=name,
        out_shape=(*sem_shapes, *[pltpu.HBM(t.shape, t.dtype) for t in list(xs) + list(lands)], _sds((8, 128), F32)),
        in_specs=[HBM] * (nx + na) + [ANY], out_specs=(*[SEM] * nsem, *[HBM] * (nx + na), pl.BlockSpec(memory_space=pltpu.VMEM)),
        input_output_aliases={i: nsem + i for i in range(nx + na)},
        compiler_params=pltpu.CompilerParams(has_side_effects=EFFECT))(*[hbm(t) for t in list(xs) + list(lands)], after)
    return outs[:nsem], list(outs[nsem:nsem + nx]), list(outs[nsem + nx:nsem + nx + na]), outs[-1]


def _send_wait(mode, started, after, name):
    sems, xs, lands, _ = started
    nx, na, nsem = len(xs), len(lands), len(sems)

    def body(*refs):
        s = refs[nx + na:nx + na + nsem]
        local, remote = _split_copies(mode, refs[:nx], refs[nx:nx + na], s[0], s[1], s[2] if nsem == 3 else None, True)
        for cp in remote:
            cp.wait_send()
            cp.wait_recv()
        for cp in local:
            cp.wait()

    outs = pl.pallas_call(body, name=name, out_shape=tuple(pltpu.HBM(t.shape, t.dtype) for t in xs + lands),
                          in_specs=[HBM] * (nx + na) + [SEM] * nsem + [ANY], out_specs=[HBM] * (nx + na),
                          input_output_aliases={i: i for i in range(nx + na)},
                          compiler_params=pltpu.CompilerParams(has_side_effects=EFFECT))(*xs, *lands, *sems, after)
    return list(outs[nx:])


def _local_step(x, meta, target, mix_weights_fn, ffn_weights_fn, grads_fn, sink, dec_f, dec_b, ret_norm, n_mix_pre, n_mix_post, n_ffn_pre,
                n_ffn_post):
    depth = n_mix_pre.shape[0]
    d = D_MODEL
    h = jnp.concatenate([jnp.zeros((PAD_FRONT, d), F32), meta, x], axis=0)
    n = h.shape[0]
    cos_a, sin_a, cos_r, sin_r, perm = _rope_tables(n)
    lg_all = jnp.stack([-jnp.exp(dec_f), -jnp.exp(dec_b)], axis=-1)
    saved = []
    wi, wo, tok = mix_weights_fn(0, h)
    u = _norm_fwd(h, (n_mix_pre[0] + tok)[None], None, BF16, "l0_norm_mix_pre")
    for l in range(depth):
        t = f"l{l}_"
        sink_b = jnp.broadcast_to(sink[l][:, None], (ATT_HEADS, 128))
        proj = _mm_nt(u, wi, BF16, WIDE_TILE, d, t + "proj")
        aq = _rope_att(proj, 0, ATT_HEADS, cos_a, sin_a, perm, t + "rope_aq")
        ak = _rope_att(proj, ATT_WIDTH // KV_WIDTH, ATT_KV_HEADS, cos_a, sin_a, perm, t + "rope_ak")
        mixed = _att_fwd(aq, ak, proj, sink_b, t + "att")
        proj3 = proj[None]
        rq = _rope_ret(proj3, COL_RQ // 2, cos_r, sin_r, t + "rope_rq")
        rk = _rope_ret(proj3, COL_RK // 2, cos_r, sin_r, t + "rope_rk")
        o_ret, states = _ret_fwd(rq, rk, proj, lg_all[l], t + "ret")
        mixed = _retgate_fwd(o_ret, proj, ret_norm[l][None], mixed, t + "retgate")
        wg, wu, wd, tok = ffn_weights_fn(l, mixed)
        mo, h_mid, u2 = _out_proj_norms(mixed, wo, h, n_mix_post[l][None], (n_ffn_pre[l] + tok)[None], t + "out_proj")
        gate, up, f = _swiglu_fwd(u2, wg, wu, t + "gate_up")
        dn = _mm_nn(f, wd, F32, 512, D_FF, t + "down", tm_cap=DEEP_K_TOKEN_TILE)
        saved.append(dict(h=h, u=u, proj=proj, aq=aq, ak=ak, rq=rq, rk=rk, o_ret=o_ret, states=states, mixed=mixed, mo=mo, h_mid=h_mid, u2=u2,
                          gate=gate, up=up, f=f, dn=dn, sink_b=sink_b, wi=wi, wo=wo, wg=wg, wu=wu, wd=wd))
        if l + 1 < depth:
            wi, wo, tok = mix_weights_fn(l + 1, dn)
            h, u = _norm_res_norm(dn, n_ffn_post[l][None], h_mid, (n_mix_pre[l + 1] + tok)[None], t + "norm_ffn_post")
        else:
            h = _norm_fwd(dn, n_ffn_post[l][None], h_mid, F32, t + "norm_ffn_post")

    dh, loss_part = _loss_head(h, target, "loss_head")
    gs = dict(sink=[None] * depth, dec_f=[None] * depth, dec_b=[None] * depth, ret_norm=[None] * depth, mix_pre=[None] * depth,
              mix_post=[None] * depth, ffn_pre=[None] * depth, ffn_post=[None] * depth)
    tok_b = jnp.zeros((), F32)
    for l in reversed(range(depth)):
        t = f"l{l}_b_"
        sv = saved[l]
        proj = sv["proj"]
        d_dn, gs["ffn_post"][l] = _norm_bwd(dh, sv["dn"], (n_ffn_post[l] + tok_b)[None], None, BF16, t + "norm_ffn_post")
        gw = {}
        d_gate, d_up = _swiglu_bwd(d_dn, sv["wd"], sv["gate"], sv["up"], t + "d_gate_up")
        gw["wd"] = _mm_tn(sv["f"], d_dn, WIDE_TILE, 1024, t + "dw_down")
        du2 = _mm_nn(d_gate, sv["wg"], F32, 512, D_FF, t + "du2_gate", tm_cap=DEEP_K_TOKEN_TILE)
        du2 = _mm_nn(d_up, sv["wu"], F32, 512, D_FF, t + "du2_up", acc=du2, tm_cap=DEEP_K_TOKEN_TILE)
        gw["wg"] = _mm_tn(d_gate, sv["u2"], WIDE_TILE, 1024, t + "dw_gate")
        gw["wu"] = _mm_tn(d_up, sv["u2"], WIDE_TILE, 1024, t + "dw_up")
        tok_b = grads_fn(l, "ffn", gw, du2)
        dh, gs["ffn_pre"][l] = _norm_bwd(du2, sv["h_mid"], (n_ffn_pre[l] + tok_b)[None], dh, F32, t + "norm_ffn_pre")
        d_mo, gs["mix_post"][l] = _norm_bwd(dh, sv["mo"], n_mix_post[l][None], None, BF16, t + "norm_mix_post")
        d_mixed = _mm_nt(d_mo, sv["wo"], BF16, 1024, d, t + "d_mixed")
        gw["wo"] = _mm_tn(sv["mixed"], d_mo, 1024, 1024, t + "dw_out")
        d_o, dproj, gs["ret_norm"][l] = _retgate_bwd(d_mixed, sv["o_ret"], proj, ret_norm[l][None], t + "retgate")
        dq_r, dk_r, dv_r, dlg = _ret_bwd(sv["rq"], sv["rk"], proj, lg_all[l], d_o, sv["states"], t + "ret")
        draw = dlg[:, :, 0, 0] * lg_all[l]
        gs["dec_f"][l], gs["dec_b"][l] = draw[:, 0], draw[:, 1]
        dproj = _rope_ret(dq_r, 0, cos_r, -sin_r, t + "rope_rq", into=dproj, out_col=COL_RQ // 2)
        dproj = _rope_ret(dk_r, 0, cos_r, -sin_r, t + "rope_rk", into=dproj, out_col=COL_RK // 2)
        dproj = _rope_ret(dv_r, 0, None, None, t + "sum_rv", into=dproj, out_col=COL_RV // 2)
        dq_a, dk_a, dv_a, dsink = _att_bwd(sv["aq"], sv["ak"], proj, sv["sink_b"], d_mixed, t + "att")
        gs["sink"][l] = dsink[:, 0]
        dproj = _rope_att(dq_a, 0, ATT_HEADS, cos_a, -sin_a, perm, t + "rope_aq", into=dproj, out_col=0)
        dproj = _rope_att(dk_a, 0, ATT_KV_HEADS, cos_a, -sin_a, perm, t + "rope_ak", plain=dv_a, into=dproj, out_col=ATT_WIDTH // (2 * KV_WIDTH))
        gw["wi"] = _mm_tn(dproj, sv["u"], WIDE_TILE, 1024, t + "dw_in")
        tok_b = grads_fn(l, "mix", gw, dproj)
        du = _mm_nn(dproj, sv["wi"], F32, 512, IN_COLS, t + "du", tm_cap=DEEP_K_TOKEN_TILE)
        dh, gs["mix_pre"][l] = _norm_bwd(du, sv["h"], (n_mix_pre[l] + tok_b)[None], dh, F32, t + "norm_mix_pre")
    return loss_part[0, 0], dh, gs


def _pack_small(mix_pre, mix_post, ffn_pre, ffn_post, ret_norm, sink, dec_f, dec_b, loss, meta):
    d = D_MODEL

    def tile(a, rows=8):
        a = jnp.reshape(a, (-1, a.shape[-1])) if a.ndim else jnp.reshape(a, (1, 1))
        return jnp.pad(a, ((0, rows - a.shape[0]), (0, d - a.shape[1])))

    return jnp.concatenate([tile(mix_pre), tile(mix_post), tile(ffn_pre), tile(ffn_post), tile(ret_norm.reshape(-1, d)), tile(sink), tile(dec_f),
                            tile(dec_b), tile(loss), tile(meta, SMALL_ROWS - ROW_META)], axis=0)


def _unpack_small(p, depth):
    rows = lambda r0, cols: p[r0:r0 + depth, :cols]
    return dict(mix_pre=rows(ROW_MIX_PRE, D_MODEL), mix_post=rows(ROW_MIX_POST, D_MODEL), ffn_pre=rows(ROW_FFN_PRE, D_MODEL),
                ffn_post=rows(ROW_FFN_POST, D_MODEL), ret_norm=p[ROW_RET_NORM:ROW_RET_NORM + depth * RET_WIDTH // D_MODEL].reshape(depth, RET_WIDTH),
                sink=rows(ROW_SINK, ATT_HEADS), dec_f=rows(ROW_DEC_F, RET_HEADS), dec_b=rows(ROW_DEC_B, RET_HEADS), loss=p[ROW_LOSS, 0])


def kernel(x, meta_tokens, w_in, w_out, attn_sink, ret_decay_fwd, ret_decay_bwd, ret_norm, norm_mix_pre, norm_mix_post, w_gate, w_up, w_down, norm_ffn_pre, norm_ffn_post, loss_target, m_meta_tokens, m_w_in, m_w_out, m_attn_sink, m_ret_decay_fwd, m_ret_decay_bwd, m_ret_norm, m_norm_mix_pre, m_norm_mix_post, m_w_gate, m_w_up, m_w_down, m_norm_ffn_pre, m_norm_ffn_post, v_meta_tokens, v_w_in, v_w_out, v_attn_sink, v_ret_decay_fwd, v_ret_decay_bwd, v_ret_norm, v_norm_mix_pre, v_norm_mix_post, v_w_gate, v_w_up, v_w_down, v_norm_ffn_pre, v_norm_ffn_post):
    depth, d = w_in.shape[0], D_MODEL
    me = 4 * lax.axis_index("x") + 2 * lax.axis_index("y") + lax.axis_index("c")
    zero = jnp.zeros((), F32)

    meta_g, = _allgather([meta_tokens], "gather_meta")
    meta = meta_g.transpose(1, 0, 2).reshape(N_META, d)

    def shards(k):
        l = k // 2
        if k % 2 == 0:
            return [w_in[l].T.astype(BF16), w_out[l].astype(BF16)]
        return [w_gate[l].T.astype(BF16), w_up[l].T.astype(BF16), w_down[l].astype(BF16)]

    gathers, ahead = {}, 2
    for k in range(min(ahead + 1, 2 * depth)):
        gathers[k] = _send_start("gather", shards(k), None, gathers[k - 1][3] if k else meta_g, f"gather_start_g{k}")

    passing = {}

    def pass_on(k, after):
        lands = _send_wait("gather", gathers.pop(k), after, f"gather_wait_g{k}")
        passing[k] = _send_start("forward", [], lands, after, f"forward_start_g{k}")

    def take(k, h):
        after = h
        if k >= 1 and k + ahead < 2 * depth:
            gathers[k + ahead] = _send_start("gather", shards(k + ahead), None, h, f"gather_start_g{k + ahead}")
            after = gathers[k + ahead][3]
        elif k == 0:
            after = gathers[max(gathers)][3]
        if k not in passing:
            pass_on(k, after)
        if k >= 1 and k + 1 < 2 * depth:
            pass_on(k + 1, after)
        last = passing[k + 1][3] if k + 1 in passing else passing[k][3]
        return _send_wait("forward", passing.pop(k), last, f"forward_wait_g{k}")

    def mix_weights_fn(l, h):
        wi_t, wo = take(2 * l, h)
        return wi_t.reshape(IN_COLS, d), wo.reshape(d, d), zero

    def ffn_weights_fn(l, h):
        wg_t, wu_t, wd = take(2 * l + 1, h)
        return wg_t.reshape(D_FF, d), wu_t.reshape(D_FF, d), wd.reshape(D_FF, d), zero

    exchanges, adam, order = {}, {}, []
    tr = lambda *ts: tuple(jnp.swapaxes(t, 1, 2) for t in ts)
    big = dict(wi=tr(w_in, m_w_in, v_w_in), wg=tr(w_gate, m_w_gate, v_w_gate), wu=tr(w_up, m_w_up, v_w_up), wd=(w_down, m_w_down, v_w_down),
               wo=(w_out, m_w_out, v_w_out))
    kinds = dict(ffn=("wg", "wu", "wd"), mix=("wi", "wo"))

    arrivals = []

    def finish(key, after):
        l, part = key
        arrivals.append((key, _send_wait("scatter", exchanges.pop(key), after, f"exchange_wait_{part}_l{l}")))

    def update():
        last_start = exchanges[order[-1]][3] if order[-1] in exchanges else None
        while arrivals:
            (l, part), arrived = arrivals.pop(0)
            for kind, parts in zip(kinds[part], arrived):
                adam[kind] = _adamw(parts, *big[kind], f"adamw_{kind}_l{l}", layer=l, prev=adam.get(kind),
                                    after=None if kind in adam else last_start)

    def grads_fn(l, part, gw, after):
        packed = [gw[kind].reshape(N_DEV, -1, d) for kind in kinds[part]]
        exchanges[(l, part)] = _send_start("scatter", packed, None, after, f"exchange_start_{part}_l{l}")
        order.append((l, part))
        token = exchanges[(l, part)][3]
        if len(order) > 2:
            finish(order[-3], token)
        return token[0, 0]

    loss_part, dh, gs = _local_step(x[0], meta, loss_target[0], mix_weights_fn, ffn_weights_fn, grads_fn, attn_sink, ret_decay_fwd, ret_decay_bwd, ret_norm,
                                    norm_mix_pre, norm_mix_post, norm_ffn_pre, norm_ffn_post)
    grad_x = dh[BLOCK:][None]

    st = lambda xs: jnp.stack([t.reshape(-1) if t.ndim == 1 else t[0] for t in xs])
    small = _pack_small(st(gs["mix_pre"]), st(gs["mix_post"]), st(gs["ffn_pre"]), st(gs["ffn_post"]), st(gs["ret_norm"]), st(gs["sink"]),
                        st(gs["dec_f"]), st(gs["dec_b"]), loss_part, dh[PAD_FRONT:BLOCK])
    update()
    small_g, = _allgather([small], "gather_small", after=adam["wo"][0])
    for key in order[-2:]:
        finish(key, small_g)
    update()
    o_wi, o_wo, o_wg, o_wu, o_wd = tr(*adam["wi"]), adam["wo"], tr(*adam["wg"]), tr(*adam["wu"]), adam["wd"]
    zmeta = jnp.zeros((N_META, d), F32)
    packs = [_pack_small(a[0], a[1], a[2], a[3], a[4], a[5], a[6], a[7], zero, zmeta) for a in (
        (norm_mix_pre, norm_mix_post, norm_ffn_pre, norm_ffn_post, ret_norm, attn_sink, ret_decay_fwd, ret_decay_bwd),
        (m_norm_mix_pre, m_norm_mix_post, m_norm_ffn_pre, m_norm_ffn_post, m_ret_norm, m_attn_sink, m_ret_decay_fwd, m_ret_decay_bwd),
        (v_norm_mix_pre, v_norm_mix_post, v_norm_ffn_pre, v_norm_ffn_post, v_ret_norm, v_attn_sink, v_ret_decay_fwd, v_ret_decay_bwd))]
    o_small = [_unpack_small(o, depth) for o in _adamw(small_g, packs[0], packs[1], packs[2], "adamw_small")]
    meta_parts = lax.dynamic_slice(small_g, (0, ROW_META, me * (d // N_DEV)), (N_DEV, N_META, d // N_DEV))
    o_meta = _adamw(meta_parts, meta_tokens, m_meta_tokens, v_meta_tokens, "adamw_meta")

    outs = []
    for i in range(4):
        s = o_small[i]
        outs += [o_meta[i], o_wi[i], o_wo[i], s["sink"], s["dec_f"], s["dec_b"], s["ret_norm"], s["mix_pre"], s["mix_post"], o_wg[i], o_wu[i],
                 o_wd[i], s["ffn_pre"], s["ffn_post"]]
    return (o_small[0]["loss"], grad_x, *outs)
```

```python
import jax
import jax.numpy as jnp
import numpy as np
from jax import lax
from jax.experimental import pallas as pl
from jax.experimental.pallas import tpu as pltpu

F32, BF16 = jnp.float32, jnp.bfloat16

D_MODEL = 2048
N_META = 16
BLOCK = 128
WINDOW = 128
PAD_FRONT = BLOCK - N_META
ATT_HEAD_DIM = 128
ATT_WIDTH = D_MODEL // 2
ATT_HEADS = ATT_WIDTH // ATT_HEAD_DIM
ATT_KV_HEADS = 2
ATT_GROUP = ATT_HEADS // ATT_KV_HEADS
KV_WIDTH = ATT_KV_HEADS * ATT_HEAD_DIM
ROT_DIM = ATT_HEAD_DIM // 4
ROPE_THETA = 500000.0
RET_WIDTH = D_MODEL - ATT_WIDTH
RET_HEAD_DIM = 256
RET_HEADS = RET_WIDTH // RET_HEAD_DIM
RET_THETA = 10000.0
D_FF = 5632
IN_COLS = ATT_WIDTH + 2 * KV_WIDTH + 4 * RET_WIDTH
N_DEV = 8
SHARD_COLS = IN_COLS // N_DEV
EPS = 1e-6
NEG = -1e30
RET_K_SCALE = RET_HEAD_DIM ** -0.5
ATT_SCALE = ATT_HEAD_DIM ** -0.5

COL_AV256 = (ATT_WIDTH + KV_WIDTH) // 256
COL_RQ = (ATT_WIDTH + 2 * KV_WIDTH) // RET_HEAD_DIM
COL_RK = COL_RQ + RET_HEADS
COL_RV = COL_RK + RET_HEADS
COL_RG = COL_RV + RET_HEADS

ADAM_LR, ADAM_B1, ADAM_B2, ADAM_EPS, ADAM_WD, ADAM_STEP = 0.001, 0.9, 0.999, 1e-08, 0.01, 10

ROW_MIX_PRE, ROW_MIX_POST, ROW_FFN_PRE, ROW_FFN_POST, ROW_RET_NORM, ROW_SINK, ROW_DEC_F, ROW_DEC_B, ROW_LOSS, ROW_META, SMALL_ROWS = (
    0, 8, 16, 24, 32, 40, 48, 56, 64, 72, 96)
ADAMW_TILE_ELEMS = 128 * 1024

MESH = pl.DeviceIdType.MESH
ANY = pl.BlockSpec(memory_space=pl.ANY)


def _row_tile(n, cap):
    for t in range(cap - cap % 16, 0, -16):
        if n % t == 0:
            return t
    raise ValueError(n)


def _sds(shape, dtype):
    return jax.ShapeDtypeStruct(shape, dtype)


def _silu(x):
    return x * jax.nn.sigmoid(x)


def _dsilu(x):
    s = jax.nn.sigmoid(x)
    return s * (1.0 + x * (1.0 - s))


def _norm_fwd(x, g, res, out_dtype, name):
    n, d = x.shape
    tr = _row_tile(n, 384)

    def body(*refs):
        if res is None:
            x_ref, g_ref, o_ref = refs
        else:
            x_ref, g_ref, r_ref, o_ref = refs
        xv = x_ref[...]
        r = lax.rsqrt(jnp.mean(xv * xv, axis=-1, keepdims=True) + EPS)
        y = xv * r * g_ref[...]
        if res is not None:
            y = y + r_ref[...]
        o_ref[...] = y.astype(o_ref.dtype)

    row = pl.BlockSpec((tr, d), lambda i: (i, 0))
    ins = [row, pl.BlockSpec((1, d), lambda i: (0, 0))] + ([row] if res is not None else [])
    args = (x, g) + ((res,) if res is not None else ())
    return pl.pallas_call(body, grid=(n // tr,), in_specs=ins, out_specs=row, out_shape=_sds((n, d), out_dtype), name=name)(*args)


def _norm_res_norm(x, g, res, g_next, name):
    n, d = x.shape
    tr = _row_tile(n, 384)

    def body(x_ref, g_ref, r_ref, gn_ref, h_ref, u_ref):
        xv = x_ref[...]
        hv = r_ref[...] + xv * lax.rsqrt(jnp.mean(xv * xv, axis=-1, keepdims=True) + EPS) * g_ref[...]
        h_ref[...] = hv
        u_ref[...] = (hv * lax.rsqrt(jnp.mean(hv * hv, axis=-1, keepdims=True) + EPS) * gn_ref[...]).astype(u_ref.dtype)

    row = pl.BlockSpec((tr, d), lambda i: (i, 0))
    vec = pl.BlockSpec((1, d), lambda i: (0, 0))
    return pl.pallas_call(body, grid=(n // tr,), in_specs=[row, vec, row, vec], out_specs=(row, row),
                          out_shape=(_sds((n, d), F32), _sds((n, d), BF16)), name=name)(x, g, res, g_next)


OUT_PROJ_TOKEN_TILE = 192


def _out_proj_norms(x, w, res, g, g_next, name):
    n, k = x.shape
    d = w.shape[1]
    tm = _row_tile(n, OUT_PROJ_TOKEN_TILE)

    def body(x_ref, w_ref, r_ref, g_ref, gn_ref, mo_ref, h_ref, u_ref):
        mo = jnp.dot(x_ref[...], w_ref[...], preferred_element_type=F32)
        mo_ref[...] = mo
        hv = r_ref[...] + mo * lax.rsqrt(jnp.mean(mo * mo, axis=-1, keepdims=True) + EPS) * g_ref[...]
        h_ref[...] = hv
        u_ref[...] = (hv * lax.rsqrt(jnp.mean(hv * hv, axis=-1, keepdims=True) + EPS) * gn_ref[...]).astype(u_ref.dtype)

    row = pl.BlockSpec((tm, d), lambda i: (i, 0))
    vec = pl.BlockSpec((1, d), lambda i: (0, 0))
    return pl.pallas_call(body, grid=(n // tm,), in_specs=[pl.BlockSpec((tm, k), lambda i: (i, 0)), pl.BlockSpec((k, d), lambda i: (0, 0)), row, vec, vec],
                          out_specs=(row, row, row), out_shape=(_sds((n, d), F32), _sds((n, d), F32), _sds((n, d), BF16)), name=name)(
                              x, w, res, g, g_next)


def _norm_bwd(dy, x, g, res, out_dtype, name):
    n, d = x.shape
    tr = _row_tile(n, 384)

    def body(*refs):
        if res is None:
            dy_ref, x_ref, g_ref, dx_ref, dg_ref = refs
        else:
            dy_ref, x_ref, g_ref, r_ref, dx_ref, dg_ref = refs
        i = pl.program_id(0)
        xv = x_ref[...]
        r = lax.rsqrt(jnp.mean(xv * xv, axis=-1, keepdims=True) + EPS)
        xhat = xv * r
        dyf = dy_ref[...].astype(F32)
        gdy = dyf * g_ref[...]
        dx = r * (gdy - xhat * jnp.mean(gdy * xhat, axis=-1, keepdims=True))
        if res is not None:
            dx = dx + r_ref[...]
        dx_ref[...] = dx.astype(dx_ref.dtype)

        @pl.when(i == 0)
        def _():
            dg_ref[...] = jnp.zeros_like(dg_ref)

        dg_ref[...] += jnp.sum(dyf * xhat, axis=0, keepdims=True)

    row = pl.BlockSpec((tr, d), lambda i: (i, 0))
    vec = pl.BlockSpec((1, d), lambda i: (0, 0))
    ins = [row, row, vec] + ([row] if res is not None else [])
    args = (dy, x, g) + ((res,) if res is not None else ())
    return pl.pallas_call(body, grid=(n // tr,), in_specs=ins, out_specs=(row, vec),
                          out_shape=(_sds((n, d), out_dtype), _sds((1, d), F32)), name=name,
                          compiler_params=pltpu.CompilerParams(dimension_semantics=("arbitrary",)))(*args)


def _norm_bwd_pair(dy, x1, g1, res, x2, g2, name):
    n, d = x1.shape
    tr = _row_tile(n, 192)

    def rms_bwd(dyf, xv, g):
        r = lax.rsqrt(jnp.mean(xv * xv, axis=-1, keepdims=True) + EPS)
        xhat = xv * r
        gdy = dyf * g
        return r * (gdy - xhat * jnp.mean(gdy * xhat, axis=-1, keepdims=True)), jnp.sum(dyf * xhat, axis=0, keepdims=True)

    def body(dy_ref, x1_ref, g1_ref, r_ref, x2_ref, g2_ref, dh_ref, d2_ref, dg1_ref, dg2_ref):
        @pl.when(pl.program_id(0) == 0)
        def _():
            dg1_ref[...] = jnp.zeros_like(dg1_ref)
            dg2_ref[...] = jnp.zeros_like(dg2_ref)

        dx1, s1 = rms_bwd(dy_ref[...].astype(F32), x1_ref[...], g1_ref[...])
        dh = dx1 + r_ref[...]
        dh_ref[...] = dh
        dx2, s2 = rms_bwd(dh, x2_ref[...], g2_ref[...])
        d2_ref[...] = dx2.astype(d2_ref.dtype)
        dg1_ref[...] += s1
        dg2_ref[...] += s2

    row = pl.BlockSpec((tr, d), lambda i: (i, 0))
    vec = pl.BlockSpec((1, d), lambda i: (0, 0))
    return pl.pallas_call(body, grid=(n // tr,), in_specs=[row, row, vec, row, row, vec], out_specs=(row, row, vec, vec),
                          out_shape=(_sds((n, d), F32), _sds((n, d), BF16), _sds((1, d), F32), _sds((1, d), F32)), name=name,
                          compiler_params=pltpu.CompilerParams(dimension_semantics=("arbitrary",)))(dy, x1, g1, res, x2, g2)


def _mm(a, b, *, ta, tb, grid, a_blk, a_map, b_blk, b_map, o_blk, o_map, o_shape, o_dtype, name, acc=None):
    nk = grid[2]
    dims = (((0,) if ta else (1,), (1,) if tb else (0,)), ((), ()))

    def body(*refs):
        if acc is None:
            a_ref, b_ref, o_ref = refs[:3]
            c_ref = None
        else:
            a_ref, b_ref, c_ref, o_ref = refs[:4]
        part = lax.dot_general(a_ref[...], b_ref[...], dims, preferred_element_type=F32)
        if nk == 1:
            if c_ref is not None:
                part = part + c_ref[...].astype(F32)
            o_ref[...] = part.astype(o_ref.dtype)
            return
        acc_ref = refs[-1]
        k = pl.program_id(2)

        @pl.when(k == 0)
        def _():
            acc_ref[...] = jnp.zeros_like(acc_ref) if c_ref is None else c_ref[...].astype(F32)

        acc_ref[...] += part

        @pl.when(k == nk - 1)
        def _():
            o_ref[...] = acc_ref[...].astype(o_ref.dtype)

    ins = [pl.BlockSpec(a_blk, a_map), pl.BlockSpec(b_blk, b_map)]
    args = [a, b]
    if acc is not None:
        ins.append(pl.BlockSpec(o_blk, o_map))
        args.append(acc)
    return pl.pallas_call(body, grid=grid, in_specs=ins, out_specs=pl.BlockSpec(o_blk, o_map), out_shape=_sds(o_shape, o_dtype),
                          scratch_shapes=[pltpu.VMEM(o_blk, F32)] if nk > 1 else [], name=name,
                          compiler_params=pltpu.CompilerParams(dimension_semantics=("parallel", "parallel", "arbitrary")))(*args)


TOKEN_TILE = 1056
WIDE_TILE = 1408
DEEP_K_TOKEN_TILE = 528


def _mm_nn(x, w, o_dtype, tn, tk, name, acc=None, tm_cap=TOKEN_TILE):
    n, k = x.shape
    tm = _row_tile(n, tm_cap)
    return _mm(x, w, ta=False, tb=False, grid=(n // tm, w.shape[1] // tn, k // tk), a_blk=(tm, tk), a_map=lambda i, j, kk: (i, kk),
               b_blk=(tk, tn), b_map=lambda i, j, kk: (kk, j), o_blk=(tm, tn), o_map=lambda i, j, kk: (i, j),
               o_shape=(n, w.shape[1]), o_dtype=o_dtype, name=name, acc=acc)


def _mm_nt(dy, w, o_dtype, tn, tk, name, acc=None):
    n, k = dy.shape
    tm = _row_tile(n, TOKEN_TILE)
    return _mm(dy, w, ta=False, tb=True, grid=(n // tm, w.shape[0] // tn, k // tk), a_blk=(tm, tk), a_map=lambda i, j, kk: (i, kk),
               b_blk=(tn, tk), b_map=lambda i, j, kk: (j, kk), o_blk=(tm, tn), o_map=lambda i, j, kk: (i, j),
               o_shape=(n, w.shape[0]), o_dtype=o_dtype, name=name, acc=acc)


def _mm_tn(x, dy, tm, tn, name):
    n, m = x.shape
    tk = _row_tile(n, 2 * TOKEN_TILE)
    return _mm(x, dy, ta=True, tb=False, grid=(m // tm, dy.shape[1] // tn, n // tk), a_blk=(tk, tm), a_map=lambda i, j, kk: (kk, i),
               b_blk=(tk, tn), b_map=lambda i, j, kk: (kk, j), o_blk=(tm, tn), o_map=lambda i, j, kk: (i, j),
               o_shape=(m, dy.shape[1]), o_dtype=BF16, name=name)


def _rope_tables(n):
    pos = (jnp.arange(n) - PAD_FRONT).astype(F32)
    half = ROT_DIM // 2
    ang = pos[:, None] * (ROPE_THETA ** (-jnp.arange(half, dtype=F32) / half))[None, :]
    c, s = jnp.cos(ang), jnp.sin(ang)
    rest = ATT_HEAD_DIM - ROT_DIM
    cos_a = jnp.concatenate([c, c, jnp.ones((n, rest), F32)], axis=1)
    sin_a = jnp.concatenate([-s, s, jnp.zeros((n, rest), F32)], axis=1)
    half = RET_HEAD_DIM // 2
    ang = pos[:, None] * (RET_THETA ** (-jnp.arange(half, dtype=F32) / half))[None, :]
    c, s = jnp.cos(ang), jnp.sin(ang)
    perm = np.zeros((ATT_HEAD_DIM, ATT_HEAD_DIM), np.float32)
    for i in range(ROT_DIM):
        perm[(i + ROT_DIM // 2) % ROT_DIM, i] = 1.0
    return cos_a, sin_a, jnp.concatenate([c, c], axis=1), jnp.concatenate([-s, s], axis=1), jnp.asarray(perm, BF16)


def _into(buf, own_shape):
    if buf is None:
        return _sds(own_shape, BF16), [], [], lambda n_inputs: {}
    return _sds(buf.shape, buf.dtype), [ANY], [buf], lambda n_inputs: {n_inputs: 0}


def _rope_att(x, col0, heads, cos, sin, perm, name, plain=None, into=None, out_col=0):
    n = x.shape[0]
    tr = _row_tile(n, 1056)
    hd = ATT_HEAD_DIM
    w2 = 0 if plain is None else plain.shape[1]
    width = heads * hd + w2

    def body(*refs):
        x_ref, c_ref, s_ref, p_ref, o_ref = refs[0], refs[1], refs[2], refs[3], refs[-1]
        for h in range(heads):
            cs = slice(h * hd, (h + 1) * hd)
            xb = x_ref[:, cs].astype(BF16)
            sw = jnp.dot(xb, p_ref[...], preferred_element_type=F32)
            o_ref[:, cs] = (xb.astype(F32) * c_ref[...] + sw * s_ref[...]).astype(o_ref.dtype)
        if plain is not None:
            o_ref[:, heads * hd:] = refs[4][...].astype(o_ref.dtype)

    tab = pl.BlockSpec((tr, hd), lambda i: (i, 0))
    ins = [pl.BlockSpec((tr, heads * hd), lambda i: (i, col0)), tab, tab, pl.BlockSpec((hd, hd), lambda i: (0, 0))]
    args = [x, cos, sin, perm]
    if plain is not None:
        ins.append(pl.BlockSpec((tr, w2), lambda i: (i, 0)))
        args.append(plain)
    shape, extra_specs, extra_args, alias = _into(into, (n, width))
    return pl.pallas_call(body, grid=(n // tr,), in_specs=ins + extra_specs, out_specs=pl.BlockSpec((tr, width), lambda i: (i, out_col)),
                          out_shape=shape, input_output_aliases=alias(len(ins)), name=name)(*args, *extra_args)


def _rope_ret(x, col0, cos, sin, name, into=None, out_col=0):
    p, n, _ = x.shape
    tr = _row_tile(n, 1056)
    hd = RET_HEAD_DIM

    def body(*refs):
        x_ref, o_ref = refs[0], refs[-1]
        for h in range(2):
            cs = slice(h * hd, (h + 1) * hd)
            xv = x_ref[0, :, cs].astype(F32)
            for q in range(1, p):
                xv = xv + x_ref[q, :, cs].astype(F32)
            if cos is not None:
                sw = jnp.concatenate([xv[:, hd // 2:], xv[:, :hd // 2]], axis=1)
                xv = xv * refs[1][...] + sw * refs[2][...]
            o_ref[:, cs] = xv.astype(o_ref.dtype)

    tab = pl.BlockSpec((tr, hd), lambda i, j: (i, 0))
    ins = [pl.BlockSpec((p, tr, 2 * hd), lambda i, j: (0, i, col0 + j))] + ([tab, tab] if cos is not None else [])
    args = (x,) + ((cos, sin) if cos is not None else ())
    shape, extra_specs, extra_args, alias = _into(into, (n, RET_WIDTH))
    return pl.pallas_call(body, grid=(n // tr, RET_HEADS // 2), in_specs=ins + extra_specs,
                          out_specs=pl.BlockSpec((tr, 2 * hd), lambda i, j: (i, out_col + j)), out_shape=shape,
                          input_output_aliases=alias(len(ins)), name=name)(*args, *extra_args)


def _att_mask(nblk, n_tot):
    row = lax.broadcasted_iota(jnp.int32, (BLOCK, 4 * BLOCK), 0)
    col = lax.broadcasted_iota(jnp.int32, (BLOCK, 4 * BLOCK), 1)
    qi = nblk * BLOCK + row
    seg = col // BLOCK
    cj = col % BLOCK
    kj = (nblk - 1 + seg) * BLOCK + cj
    band = (jnp.abs(qi - kj) <= WINDOW) & (kj >= PAD_FRONT) & (kj < n_tot) & (seg < 3)
    meta = (seg == 3) & (cj >= PAD_FRONT) & (jnp.abs(qi - cj) > WINDOW)
    return band | meta


def _att_specs(nb, v_col):
    kv = lambda f, cb: pl.BlockSpec((BLOCK, KV_WIDTH), lambda n: (f(n), cb))
    prev, own, nxt, first = (lambda n: jnp.maximum(n - 1, 0)), (lambda n: n), (lambda n: jnp.minimum(n + 1, nb - 1)), (lambda n: 0)
    return [kv(f, 0) for f in (prev, own, nxt, first)] + [kv(f, v_col) for f in (prev, own, nxt, first)]


def _att_probs(s, ok, snk):
    s = jnp.where(ok, s, NEG)
    m = jnp.maximum(jnp.max(s, axis=-1, keepdims=True), snk)
    e = jnp.exp(s - m)
    es = jnp.exp(snk - m)
    return e, es, 1.0 / (jnp.sum(e, axis=-1, keepdims=True) + es)


def _att_fwd(q, k, proj, sink_b, name):
    n = q.shape[0]
    nb = n // BLOCK
    hd = ATT_HEAD_DIM

    def body(q_ref, kp, ko, kn, km, vp, vo, vn, vm, sink_ref, o_ref):
        nblk = pl.program_id(0)
        ok = _att_mask(nblk, n)
        keep = (nblk * BLOCK + lax.broadcasted_iota(jnp.int32, (BLOCK, 1), 0)) >= PAD_FRONT
        for kh in range(ATT_KV_HEADS):
            cs = slice(kh * hd, (kh + 1) * hd)
            kk = jnp.concatenate([r[:, cs] for r in (kp, ko, kn, km)], axis=0)
            vv = jnp.concatenate([r[:, cs] for r in (vp, vo, vn, vm)], axis=0)
            heads = [kh * ATT_GROUP + g for g in range(ATT_GROUP)]
            q4 = jnp.concatenate([q_ref[:, h * hd:(h + 1) * hd] for h in heads], axis=0)
            s = lax.dot_general(q4, kk, (((1,), (1,)), ((), ())), preferred_element_type=F32) * ATT_SCALE
            es, invs = [], []
            for g, h in enumerate(heads):
                e, _, inv = _att_probs(s[g * BLOCK:(g + 1) * BLOCK], ok, sink_ref[h:h + 1, 0:1])
                es.append(e)
                invs.append(inv)
            o = jnp.dot(jnp.concatenate(es, axis=0).astype(BF16), vv, preferred_element_type=F32)
            for g, h in enumerate(heads):
                o_ref[:, h * hd:(h + 1) * hd] = jnp.where(keep, o[g * BLOCK:(g + 1) * BLOCK] * invs[g], 0.0).astype(o_ref.dtype)

    qspec = pl.BlockSpec((BLOCK, ATT_WIDTH), lambda i: (i, 0))
    return pl.pallas_call(body, grid=(nb,), in_specs=[qspec] + _att_specs(nb, COL_AV256) + [pl.BlockSpec((ATT_HEADS, 128), lambda i: (0, 0))],
                          out_specs=qspec, out_shape=_sds((n, D_MODEL), BF16), name=name)(q, k, k, k, k, proj, proj, proj, proj, sink_b)


def _att_bwd(q, k, proj, sink_b, dmixed, name):
    n = q.shape[0]
    nb = n // BLOCK
    hd = ATT_HEAD_DIM

    def body(q_ref, kp, ko, kn, km, vp, vo, vn, vm, sink_ref, do_ref, dq_ref, dk_ref, dv_ref, dsink_ref):
        nblk = pl.program_id(0)

        @pl.when(nblk == 0)
        def _():
            dk_ref[...] = jnp.zeros_like(dk_ref)
            dv_ref[...] = jnp.zeros_like(dv_ref)
            dsink_ref[...] = jnp.zeros_like(dsink_ref)

        ok = _att_mask(nblk, n)
        rows = [jnp.maximum(nblk - 1, 0), nblk, jnp.minimum(nblk + 1, nb - 1), 0]
        for kh in range(ATT_KV_HEADS):
            cs = slice(kh * hd, (kh + 1) * hd)
            kk = jnp.concatenate([r[:, cs] for r in (kp, ko, kn, km)], axis=0)
            vv = jnp.concatenate([r[:, cs] for r in (vp, vo, vn, vm)], axis=0)
            heads = [kh * ATT_GROUP + g for g in range(ATT_GROUP)]
            q4 = jnp.concatenate([q_ref[:, h * hd:(h + 1) * hd] for h in heads], axis=0)
            do4 = jnp.concatenate([do_ref[:, h * hd:(h + 1) * hd] for h in heads], axis=0)
            s = lax.dot_general(q4, kk, (((1,), (1,)), ((), ())), preferred_element_type=F32) * ATT_SCALE
            dp = lax.dot_general(do4, vv, (((1,), (1,)), ((), ())), preferred_element_type=F32)
            es, ts, invs = [], [], []
            for g, h in enumerate(heads):
                e, esink, inv = _att_probs(s[g * BLOCK:(g + 1) * BLOCK], ok, sink_ref[h:h + 1, 0:1])
                dpg = dp[g * BLOCK:(g + 1) * BLOCK]
                delta = inv * jnp.sum(e * dpg, axis=-1, keepdims=True)
                es.append(e)
                ts.append(e * (dpg - delta))
                invs.append(inv)
                dsink_ref[h:h + 1, :] = dsink_ref[h:h + 1, :] - jnp.sum(esink * inv * delta, axis=0, keepdims=True)
            tb = jnp.concatenate(ts, axis=0).astype(BF16)
            eb = jnp.concatenate(es, axis=0).astype(BF16)
            inv4 = jnp.concatenate(invs, axis=0)
            dq = jnp.dot(tb, kk, preferred_element_type=F32)
            for g, h in enumerate(heads):
                dq_ref[:, h * hd:(h + 1) * hd] = (dq[g * BLOCK:(g + 1) * BLOCK] * (invs[g] * ATT_SCALE)).astype(dq_ref.dtype)
            dk = lax.dot_general(tb, (q4.astype(F32) * (inv4 * ATT_SCALE)).astype(BF16), (((0,), (0,)), ((), ())), preferred_element_type=F32)
            dv = lax.dot_general(eb, (do4.astype(F32) * inv4).astype(BF16), (((0,), (0,)), ((), ())), preferred_element_type=F32)
            for seg, r in enumerate(rows):
                at = (pl.ds(pl.multiple_of(r * BLOCK, BLOCK), BLOCK), cs)
                dk_ref[at] += dk[seg * BLOCK:(seg + 1) * BLOCK]
                dv_ref[at] += dv[seg * BLOCK:(seg + 1) * BLOCK]

    qspec = pl.BlockSpec((BLOCK, ATT_WIDTH), lambda i: (i, 0))
    whole = pl.BlockSpec((n, KV_WIDTH), lambda i: (0, 0))
    sinks = pl.BlockSpec((ATT_HEADS, 128), lambda i: (0, 0))
    return pl.pallas_call(body, grid=(nb,), in_specs=[qspec] + _att_specs(nb, COL_AV256) + [sinks, qspec], out_specs=(qspec, whole, whole, sinks),
                          out_shape=(_sds((n, ATT_WIDTH), BF16), _sds((n, KV_WIDTH), F32), _sds((n, KV_WIDTH), F32), _sds((ATT_HEADS, 128), F32)),
                          name=name, compiler_params=pltpu.CompilerParams(dimension_semantics=("arbitrary",)))(
                              q, k, k, k, k, proj, proj, proj, proj, sink_b, dmixed)


def _ret_decay(lg, d):
    a = lax.broadcasted_iota(jnp.int32, (BLOCK, 1), 0)
    b = lax.broadcasted_iota(jnp.int32, (1, BLOCK), 1)
    t_col = a + d * (BLOCK - 1 - 2 * a)
    t_row = b + d * (BLOCK - 1 - 2 * b)
    diff = t_col - t_row
    dist = jnp.maximum(diff, 0).astype(F32)
    dmask = jnp.where(diff >= d, jnp.exp(lg * dist), 0.0)
    tf = t_col.astype(F32)
    xi = jnp.exp(lg * (tf + 1.0))
    zeta = jnp.exp(lg * (BLOCK - 1.0 - tf))
    gam = jnp.exp(jnp.full((1, 1), BLOCK, F32) * lg)
    return dmask, dist, xi, zeta, gam, tf


def _ret_fwd(q, k, proj, lg, name):
    n = q.shape[0]
    nc = n // BLOCK
    hd = RET_HEAD_DIM
    chunk = lambda d, c: c + d * (nc - 1 - 2 * c)

    def body(lg_ref, q_ref, k_ref, v0, v1, v2, v3, o_ref, st_ref, s_ref):
        d, c = pl.program_id(0), pl.program_id(1)

        @pl.when(c == 0)
        def _():
            s_ref[...] = jnp.zeros_like(s_ref)

        for h, v_ref in enumerate((v0, v1, v2, v3)):
            cs = slice(h * hd, (h + 1) * hd)
            dmask, _, xi, zeta, gam, _ = _ret_decay(lg_ref[h, d], d)
            qv = q_ref[:, cs]
            kf = k_ref[:, cs].astype(F32) * RET_K_SCALE
            vv = v_ref[...]
            s = lax.dot_general(qv, kf.astype(BF16), (((1,), (1,)), ((), ())), preferred_element_type=F32)
            sb = s_ref[h]
            o_ref[:, cs] = (jnp.dot((s * dmask).astype(BF16), vv, preferred_element_type=F32)
                            + jnp.dot((qv.astype(F32) * xi).astype(BF16), sb.astype(BF16), preferred_element_type=F32))
            st_ref[h] = sb
            s_ref[h] = gam * sb + lax.dot_general((kf * zeta).astype(BF16), vv, (((0,), (0,)), ((), ())), preferred_element_type=F32)

    wide = pl.BlockSpec((BLOCK, RET_WIDTH), lambda d, c: (chunk(d, c), 0))
    vblk = lambda h: pl.BlockSpec((BLOCK, hd), lambda d, c: (chunk(d, c), COL_RV + h))
    return pl.pallas_call(
        body, grid=(2, nc), in_specs=[pl.BlockSpec(memory_space=pltpu.SMEM), wide, wide] + [vblk(h) for h in range(RET_HEADS)],
        out_specs=(pl.BlockSpec((None, BLOCK, RET_WIDTH), lambda d, c: (d, chunk(d, c), 0)),
                   pl.BlockSpec((RET_HEADS, None, None, hd, hd), lambda d, c: (0, d, c, 0, 0))),
        out_shape=(_sds((2, n, RET_WIDTH), F32), _sds((RET_HEADS, 2, nc, hd, hd), F32)), scratch_shapes=[pltpu.VMEM((RET_HEADS, hd, hd), F32)],
        name=name, compiler_params=pltpu.CompilerParams(dimension_semantics=("parallel", "arbitrary")))(lg, q, k, proj, proj, proj, proj)


def _ret_bwd(q, k, proj, lg, do, states, name):
    n = q.shape[0]
    nc = n // BLOCK
    hd = RET_HEAD_DIM
    chunk = lambda d, r: (nc - 1 - r) + d * (2 * r - (nc - 1))

    def body(lg_ref, q_ref, k_ref, v0, v1, v2, v3, do_ref, st_ref, dq_ref, dk_ref, dv_ref, dlg_ref, ds_ref):
        d, r = pl.program_id(0), pl.program_id(1)

        @pl.when(r == 0)
        def _():
            ds_ref[...] = jnp.zeros_like(ds_ref)
            dlg_ref[...] = jnp.zeros_like(dlg_ref)

        row = lax.broadcasted_iota(jnp.int32, (BLOCK, 1), 0) + chunk(d, r) * BLOCK
        keep = row >= PAD_FRONT
        nt = (((1,), (1,)), ((), ()))
        tn = (((0,), (0,)), ((), ()))
        for h, v_ref in enumerate((v0, v1, v2, v3)):
            cs = slice(h * hd, (h + 1) * hd)
            dmask, dist, xi, zeta, gam, tf = _ret_decay(lg_ref[h, d], d)
            qv, vv, dov = q_ref[:, cs], v_ref[...], do_ref[:, cs]
            qf = qv.astype(F32)
            kf = k_ref[:, cs].astype(F32) * RET_K_SCALE
            kb = kf.astype(BF16)
            sc = st_ref[h]
            dsn = ds_ref[h]
            s = lax.dot_general(qv, kb, nt, preferred_element_type=F32)
            dsc = lax.dot_general(dov, vv, nt, preferred_element_type=F32) * dmask
            dsb = dsc.astype(BF16)
            dq_c = xi * lax.dot_general(dov, sc.astype(BF16), nt, preferred_element_type=F32)
            dk_c = zeta * lax.dot_general(vv, dsn.astype(BF16), nt, preferred_element_type=F32)
            dq = jnp.dot(dsb, kb, preferred_element_type=F32) + dq_c
            dk = lax.dot_general(dsb, qv, tn, preferred_element_type=F32) + dk_c
            dv = (lax.dot_general((s * dmask).astype(BF16), dov, tn, preferred_element_type=F32)
                  + jnp.dot((kf * zeta).astype(BF16), dsn.astype(BF16), preferred_element_type=F32))
            ds_ref[h] = gam * dsn + lax.dot_general((qf * xi).astype(BF16), dov, tn, preferred_element_type=F32)
            dlg = (jnp.sum(dsc * s * dist, keepdims=True)
                   + jnp.sum((tf + 1.0) * jnp.sum(qf * dq_c, axis=-1, keepdims=True), keepdims=True)
                   + jnp.sum((BLOCK - 1.0 - tf) * jnp.sum(kf * dk_c, axis=-1, keepdims=True), keepdims=True)
                   + BLOCK * gam * jnp.sum(dsn * sc, keepdims=True))
            dlg_ref[h] += dlg
            dq_ref[:, cs] = dq
            dk_ref[:, cs] = jnp.where(keep, dk * RET_K_SCALE, 0.0)
            dv_ref[:, cs] = jnp.where(keep, dv, 0.0)

    wide = pl.BlockSpec((BLOCK, RET_WIDTH), lambda d, r: (chunk(d, r), 0))
    vblk = lambda h: pl.BlockSpec((BLOCK, hd), lambda d, r: (chunk(d, r), COL_RV + h))
    plane = pl.BlockSpec((None, BLOCK, RET_WIDTH), lambda d, r: (d, chunk(d, r), 0))
    return pl.pallas_call(
        body, grid=(2, nc),
        in_specs=[pl.BlockSpec(memory_space=pltpu.SMEM), wide, wide] + [vblk(h) for h in range(RET_HEADS)]
        + [wide, pl.BlockSpec((RET_HEADS, None, None, hd, hd), lambda d, r: (0, d, nc - 1 - r, 0, 0))],
        out_specs=(plane, plane, plane, pl.BlockSpec((RET_HEADS, None, 8, 128), lambda d, r: (0, d, 0, 0))),
        out_shape=(_sds((2, n, RET_WIDTH), F32),) * 3 + (_sds((RET_HEADS, 2, 8, 128), F32),),
        scratch_shapes=[pltpu.VMEM((RET_HEADS, hd, hd), F32)], name=name,
        compiler_params=pltpu.CompilerParams(dimension_semantics=("parallel", "arbitrary")))(lg, q, k, proj, proj, proj, proj, do, states)


def _retgate_fwd(o, proj, gain, mixed, name):
    _, n, _ = o.shape
    tr = _row_tile(n, 1056)
    hd = RET_HEAD_DIM

    def body(o_ref, rg_ref, g_ref, _, y_ref):
        ov = o_ref[0] + o_ref[1]
        r = lax.rsqrt(jnp.mean(ov * ov, axis=-1, keepdims=True) + EPS)
        y_ref[...] = (_silu(rg_ref[...].astype(F32)) * (ov * r * g_ref[...])).astype(y_ref.dtype)

    return pl.pallas_call(body, grid=(n // tr, RET_HEADS),
                          in_specs=[pl.BlockSpec((2, tr, hd), lambda i, h: (0, i, h)), pl.BlockSpec((tr, hd), lambda i, h: (i, COL_RG + h)),
                                    pl.BlockSpec((1, hd), lambda i, h: (0, h)), ANY],
                          out_specs=pl.BlockSpec((tr, hd), lambda i, h: (i, ATT_WIDTH // hd + h)), out_shape=_sds(mixed.shape, mixed.dtype),
                          input_output_aliases={3: 0}, name=name)(o, proj, gain, mixed)


def _retgate_bwd(dmixed, o, proj, gain, name):
    _, n, _ = o.shape
    tr = _row_tile(n, 1056)
    hd = RET_HEAD_DIM

    def body(dy_ref, o_ref, rg_ref, g_ref, do_ref, drg_ref, dg_ref):
        i = pl.program_id(1)
        ov = o_ref[0] + o_ref[1]
        r = lax.rsqrt(jnp.mean(ov * ov, axis=-1, keepdims=True) + EPS)
        xhat = ov * r
        rg = rg_ref[...].astype(F32)
        dy = dy_ref[...].astype(F32)
        drg_ref[...] = (dy * (xhat * g_ref[...]) * _dsilu(rg)).astype(drg_ref.dtype)
        dn = dy * _silu(rg)
        dxh = dn * g_ref[...]
        do_ref[...] = (r * (dxh - xhat * jnp.mean(dxh * xhat, axis=-1, keepdims=True))).astype(do_ref.dtype)

        @pl.when(i == 0)
        def _():
            dg_ref[...] = jnp.zeros_like(dg_ref)

        dg_ref[...] += jnp.sum(dn * xhat, axis=0, keepdims=True)

    tile = pl.BlockSpec((tr, hd), lambda h, i: (i, h))
    vec = pl.BlockSpec((1, hd), lambda h, i: (0, h))
    rg_cols = pl.BlockSpec((tr, hd), lambda h, i: (i, COL_RG + h))
    return pl.pallas_call(body, grid=(RET_HEADS, n // tr),
                          in_specs=[pl.BlockSpec((tr, hd), lambda h, i: (i, ATT_WIDTH // hd + h)), pl.BlockSpec((2, tr, hd), lambda h, i: (0, i, h)),
                                    rg_cols, vec],
                          out_specs=(tile, rg_cols, vec), out_shape=(_sds((n, RET_WIDTH), BF16), _sds((n, IN_COLS), BF16), _sds((1, RET_WIDTH), F32)),
                          name=name, compiler_params=pltpu.CompilerParams(dimension_semantics=("parallel", "arbitrary")))(dmixed, o, proj, gain)


FFN_TILE = 512


def _swiglu_fwd(x, wg_t, wu_t, name):
    n, k = x.shape
    tm = _row_tile(n, TOKEN_TILE)
    nt = (((1,), (1,)), ((), ()))

    def body(x_ref, g_ref, u_ref, go_ref, uo_ref, f_ref):
        g = lax.dot_general(x_ref[...], g_ref[...], nt, preferred_element_type=F32)
        u = lax.dot_general(x_ref[...], u_ref[...], nt, preferred_element_type=F32)
        go_ref[...] = g.astype(go_ref.dtype)
        uo_ref[...] = u.astype(uo_ref.dtype)
        f_ref[...] = (_silu(g) * u).astype(f_ref.dtype)

    w = pl.BlockSpec((FFN_TILE, k), lambda i, j: (j, 0))
    o = pl.BlockSpec((tm, FFN_TILE), lambda i, j: (i, j))
    return pl.pallas_call(body, grid=(n // tm, wg_t.shape[0] // FFN_TILE), in_specs=[pl.BlockSpec((tm, k), lambda i, j: (i, 0)), w, w],
                          out_specs=(o, o, o), out_shape=(_sds((n, wg_t.shape[0]), BF16),) * 3, name=name,
                          compiler_params=pltpu.CompilerParams(dimension_semantics=("parallel", "parallel")))(x, wg_t, wu_t)


def _swiglu_bwd(dy, wd, gate, up, name):
    n, k = dy.shape
    tm = _row_tile(n, TOKEN_TILE)
    nt = (((1,), (1,)), ((), ()))

    def body(dy_ref, w_ref, g_ref, u_ref, dg_ref, du_ref):
        df = lax.dot_general(dy_ref[...], w_ref[...], nt, preferred_element_type=F32)
        g = g_ref[...].astype(F32)
        dg_ref[...] = (df * u_ref[...].astype(F32) * _dsilu(g)).astype(dg_ref.dtype)
        du_ref[...] = (df * _silu(g)).astype(du_ref.dtype)

    o = pl.BlockSpec((tm, FFN_TILE), lambda i, j: (i, j))
    return pl.pallas_call(body, grid=(n // tm, wd.shape[0] // FFN_TILE),
                          in_specs=[pl.BlockSpec((tm, k), lambda i, j: (i, 0)), pl.BlockSpec((FFN_TILE, k), lambda i, j: (j, 0)), o, o],
                          out_specs=(o, o), out_shape=(_sds((n, wd.shape[0]), BF16),) * 2, name=name,
                          compiler_params=pltpu.CompilerParams(dimension_semantics=("parallel", "parallel")))(dy, wd, gate, up)


def _loss_head(h, target, name):
    n, d = h.shape
    nb = n // BLOCK

    def body(h_ref, t_ref, dh_ref, l_ref):
        i = pl.program_id(0)

        @pl.when(i == 0)
        def _():
            l_ref[...] = jnp.zeros_like(l_ref)
            dh_ref[...] = jnp.zeros_like(dh_ref)

        @pl.when(i > 0)
        def _():
            e = h_ref[...] - t_ref[...]
            dh_ref[...] = e * (1.0 / d)
            l_ref[...] += 0.5 * jnp.sum(jnp.mean(e * e, axis=-1, keepdims=True), keepdims=True)

    blk = pl.BlockSpec((BLOCK, d), lambda i: (i, 0))
    return pl.pallas_call(body, grid=(nb,), in_specs=[blk, pl.BlockSpec((BLOCK, d), lambda i: (jnp.maximum(i - 1, 0), 0))],
                          out_specs=(blk, pl.BlockSpec((8, 128), lambda i: (0, 0))), out_shape=(_sds((n, d), F32), _sds((8, 128), F32)), name=name,
                          compiler_params=pltpu.CompilerParams(dimension_semantics=("arbitrary",)))(h, target)


def _adamw(parts, w, m, v, name, sel=None, layer=None, prev=None, after=None):
    s, (r, c) = parts.shape[0], parts.shape[-2:]
    tr = _row_tile(r, max(16, (ADAMW_TILE_ELEMS // c) // 16 * 16))
    b1c, b2c = 1.0 - ADAM_B1 ** ADAM_STEP, 1.0 - ADAM_B2 ** ADAM_STEP

    def body(p_ref, w_ref, m_ref, v_ref, *rest):
        g_ref, d_ref, mo_ref, vo_ref = rest[-4:]
        g = p_ref[0].astype(F32)
        for q in range(1, s):
            g = g + p_ref[q].astype(F32)
        mn = ADAM_B1 * m_ref[...] + (1.0 - ADAM_B1) * g
        vn = ADAM_B2 * v_ref[...] + (1.0 - ADAM_B2) * jnp.square(g)
        g_ref[...] = g
        mo_ref[...] = mn
        vo_ref[...] = vn
        d_ref[...] = -ADAM_LR * ((mn / b1c) / (jnp.sqrt(vn / b2c) + ADAM_EPS) + ADAM_WD * w_ref[...])

    pspec = (pl.BlockSpec((s, tr, c), lambda i: (0, i, 0)) if sel is None else pl.BlockSpec((s, None, tr, c), lambda i: (0, sel, i, 0)))
    if layer is None:
        t = pl.BlockSpec((tr, c), lambda i: (i, 0))
        return pl.pallas_call(body, grid=(r // tr,), in_specs=[pspec, t, t, t], out_specs=(t, t, t, t), out_shape=(_sds((r, c), F32),) * 4,
                              name=name)(parts, w, m, v)
    t = pl.BlockSpec((None, tr, c), lambda i: (layer, i, 0))
    prev = prev if prev is not None else tuple(lax.empty(w.shape, F32) for _ in range(4))
    extra = [] if after is None else [after]
    return pl.pallas_call(body, grid=(r // tr,), in_specs=[pspec, t, t, t] + [ANY] * (4 + len(extra)), out_specs=(t, t, t, t),
                          out_shape=(_sds(w.shape, F32),) * 4, input_output_aliases={4 + i: i for i in range(4)}, name=name)(
                              parts, w, m, v, *prev, *extra)


def _allgather(xs, name, after=None):
    na = len(xs)
    first_out = na + (after is not None)

    def body(*refs):
        x_refs, o_refs = refs[:na], refs[first_out:first_out + na]
        send, recv, lsem = refs[first_out + na:]
        x, y, c = lax.axis_index("x"), lax.axis_index("y"), lax.axis_index("c")
        me, sib = (x, y, c), (x, y, 1 - c)
        chips = [(1 - x, y), (x, 1 - y), (1 - x, 1 - y)]
        slot = lambda p: 4 * p[0] + 2 * p[1] + p[2]

        def copy(a, k, block, to, src=None):
            dst = o_refs[a].at[slot(block)]
            return pltpu.make_async_remote_copy(src_ref=dst if src is None else src, dst_ref=dst, send_sem=send.at[a, k], recv_sem=recv.at[a, k],
                                                device_id=to, device_id_type=MESH)

        mine = [pltpu.make_async_copy(x_refs[a], o_refs[a].at[slot(me)], lsem.at[a]) for a in range(na)]
        for cp in mine:
            cp.start()
        first = []
        for a in range(na):
            first.append(copy(a, 0, me, sib, src=x_refs[a]))
            first += [copy(a, 1 + j, me, (*chip, c), src=x_refs[a]) for j, chip in enumerate(chips)]
        for cp in first:
            cp.start()
        passed = []
        for j, chip in enumerate(chips):
            for a in range(na):
                copy(a, 1 + j, (*chip, c), me).wait_recv()
                passed.append(copy(a, 4 + j, (*chip, c), sib))
                passed[-1].start()
        for a in range(na):
            copy(a, 0, sib, me).wait_recv()
            for j, chip in enumerate(chips):
                copy(a, 4 + j, (*chip, 1 - c), me).wait_recv()
        for cp in first + passed:
            cp.wait_send()
        for cp in mine:
            cp.wait()

    extra = [] if after is None else [after]
    return pl.pallas_call(body, in_specs=[ANY] * (na + len(extra)), out_specs=[ANY] * na,
                          out_shape=[_sds((N_DEV,) + t.shape, t.dtype) for t in xs],
                          scratch_shapes=[pltpu.SemaphoreType.DMA((na, 7)), pltpu.SemaphoreType.DMA((na, 7)), pltpu.SemaphoreType.DMA((na,))],
                          name=name)(*xs, *extra)


HBM = pl.BlockSpec(memory_space=pltpu.HBM)
SEM = pl.BlockSpec(memory_space=pltpu.SEMAPHORE)
EFFECT = pltpu.SideEffectType.DATAFLOW_SIDE_EFFECTING


SPLIT_RELATIONS = dict(gather=(1, 2, 4, 6),
                       forward=(2, 4, 6),
                       scatter=tuple(range(1, N_DEV)))


def _split_copies(mode, x_refs, land_refs, send, recv, own, landing):
    x, y, c = lax.axis_index("x"), lax.axis_index("y"), lax.axis_index("c")
    flip = lambda r: ((1 - x if r & 4 else x), (1 - y if r & 2 else y), (1 - c if r & 1 else c))
    slot = lambda p: 4 * p[0] + 2 * p[1] + p[2]
    me = slot((x, y, c))
    rel = SPLIT_RELATIONS[mode]
    local, remote = [], []
    for a in range(len(land_refs)):
        if mode != "forward":
            local.append(pltpu.make_async_copy(x_refs[a].at[me] if mode == "scatter" else x_refs[a], land_refs[a].at[me], own.at[a]))
        for j, r in enumerate(rel):
            if mode == "forward":
                to = flip(1)
                src = dst = land_refs[a].at[slot(flip(r ^ 1) if landing else flip(r))]
            else:
                to = flip(r)
                src = x_refs[a].at[slot(to)] if mode == "scatter" else x_refs[a]
                dst = land_refs[a].at[slot(to) if landing else me]
            remote.append(pltpu.make_async_remote_copy(src_ref=src, dst_ref=dst, send_sem=send.at[len(rel) * a + j],
                                                       recv_sem=recv.at[len(rel) * a + j], device_id=to, device_id_type=MESH))
    return local, remote


def _send_start(mode, xs, lands, after, name):
    if lands is None:
        lands = [lax.empty(t.shape if mode == "scatter" else (N_DEV,) + t.shape, t.dtype) for t in xs]
    nx, na, nr = len(xs), len(lands), len(SPLIT_RELATIONS[mode])
    nsem = 2 if mode == "forward" else 3

    def body(*refs):
        x_refs, land_refs = refs[:nx], refs[nx:nx + na]
        sems = refs[nx + na + 1:nx + na + 1 + nsem]
        local, remote = _split_copies(mode, x_refs, land_refs, sems[0], sems[1], sems[2] if nsem == 3 else None, False)
        for cp in remote + local:
            cp.start()
        refs[-1][...] = jnp.zeros_like(refs[-1])

    hbm = lambda t: pltpu.with_memory_space_constraint(t, pltpu.HBM)
    sem_shapes = [pltpu.SemaphoreType.DMA((nr * na,)), pltpu.SemaphoreType.DMA((nr * na,)), pltpu.SemaphoreType.DMA((na,))][:nsem]
    outs = pl.pallas_call(
        body, name=name,
        out_shape=(*sem_shapes, *[pltpu.HBM(t.shape, t.dtype) for t in list(xs) + list(lands)], _sds((8, 128), F32)),
        in_specs=[HBM] * (nx + na) + [ANY], out_specs=(*[SEM] * nsem, *[HBM] * (nx + na), pl.BlockSpec(memory_space=pltpu.VMEM)),
        input_output_aliases={i: nsem + i for i in range(nx + na)},
        compiler_params=pltpu.CompilerParams(has_side_effects=EFFECT))(*[hbm(t) for t in list(xs) + list(lands)], after)
    return outs[:nsem], list(outs[nsem:nsem + nx]), list(outs[nsem + nx:nsem + nx + na]), outs[-1]


def _send_wait(mode, started, after, name):
    sems, xs, lands, _ = started
    nx, na, nsem = len(xs), len(lands), len(sems)

    def body(*refs):
        s = refs[nx + na:nx + na + nsem]
        local, remote = _split_copies(mode, refs[:nx], refs[nx:nx + na], s[0], s[1], s[2] if nsem == 3 else None, True)
        for cp in remote:
            cp.wait_send()
            cp.wait_recv()
        for cp in local:
            cp.wait()

    outs = pl.pallas_call(body, name=name, out_shape=tuple(pltpu.HBM(t.shape, t.dtype) for t in xs + lands),
                          in_specs=[HBM] * (nx + na) + [SEM] * nsem + [ANY], out_specs=[HBM] * (nx + na),
                          input_output_aliases={i: i for i in range(nx + na)},
                          compiler_params=pltpu.CompilerParams(has_side_effects=EFFECT))(*xs, *lands, *sems, after)
    return list(outs[nx:])


def _local_step(x, meta, target, mix_weights_fn, ffn_weights_fn, grads_fn, sink, dec_f, dec_b, ret_norm, n_mix_pre, n_mix_post, n_ffn_pre,
                n_ffn_post):
    depth = n_mix_pre.shape[0]
    d = D_MODEL
    h = jnp.concatenate([jnp.zeros((PAD_FRONT, d), F32), meta, x], axis=0)
    n = h.shape[0]
    cos_a, sin_a, cos_r, sin_r, perm = _rope_tables(n)
    lg_all = jnp.stack([-jnp.exp(dec_f), -jnp.exp(dec_b)], axis=-1)
    saved = []
    wi, wo, tok = mix_weights_fn(0, h)
    u = _norm_fwd(h, (n_mix_pre[0] + tok)[None], None, BF16, "l0_norm_mix_pre")
    for l in range(depth):
        t = f"l{l}_"
        sink_b = jnp.broadcast_to(sink[l][:, None], (ATT_HEADS, 128))
        proj = _mm_nt(u, wi, BF16, WIDE_TILE, d, t + "proj")
        aq = _rope_att(proj, 0, ATT_HEADS, cos_a, sin_a, perm, t + "rope_aq")
        ak = _rope_att(proj, ATT_WIDTH // KV_WIDTH, ATT_KV_HEADS, cos_a, sin_a, perm, t + "rope_ak")
        mixed = _att_fwd(aq, ak, proj, sink_b, t + "att")
        proj3 = proj[None]
        rq = _rope_ret(proj3, COL_RQ // 2, cos_r, sin_r, t + "rope_rq")
        rk = _rope_ret(proj3, COL_RK // 2, cos_r, sin_r, t + "rope_rk")
        o_ret, states = _ret_fwd(rq, rk, proj, lg_all[l], t + "ret")
        mixed = _retgate_fwd(o_ret, proj, ret_norm[l][None], mixed, t + "retgate")
        wg, wu, wd, tok = ffn_weights_fn(l, mixed)
        mo, h_mid, u2 = _out_proj_norms(mixed, wo, h, n_mix_post[l][None], (n_ffn_pre[l] + tok)[None], t + "out_proj")
        gate, up, f = _swiglu_fwd(u2, wg, wu, t + "gate_up")
        dn = _mm_nn(f, wd, F32, 512, D_FF, t + "down", tm_cap=DEEP_K_TOKEN_TILE)
        saved.append(dict(h=h, u=u, proj=proj, aq=aq, ak=ak, rq=rq, rk=rk, o_ret=o_ret, states=states, mixed=mixed, mo=mo, h_mid=h_mid, u2=u2,
                          gate=gate, up=up, f=f, dn=dn, sink_b=sink_b, wi=wi, wo=wo, wg=wg, wu=wu, wd=wd))
        if l + 1 < depth:
            wi, wo, tok = mix_weights_fn(l + 1, dn)
            h, u = _norm_res_norm(dn, n_ffn_post[l][None], h_mid, (n_mix_pre[l + 1] + tok)[None], t + "norm_ffn_post")
        else:
            h = _norm_fwd(dn, n_ffn_post[l][None], h_mid, F32, t + "norm_ffn_post")

    dh, loss_part = _loss_head(h, target, "loss_head")
    gs = dict(sink=[None] * depth, dec_f=[None] * depth, dec_b=[None] * depth, ret_norm=[None] * depth, mix_pre=[None] * depth,
              mix_post=[None] * depth, ffn_pre=[None] * depth, ffn_post=[None] * depth)
    d_dn, gs["ffn_post"][depth - 1] = _norm_bwd(dh, saved[-1]["dn"], n_ffn_post[depth - 1][None], None, BF16, f"l{depth - 1}_b_norm_ffn_post")
    for l in reversed(range(depth)):
        t = f"l{l}_b_"
        sv = saved[l]
        proj = sv["proj"]
        gw = {}
        d_gate, d_up = _swiglu_bwd(d_dn, sv["wd"], sv["gate"], sv["up"], t + "d_gate_up")
        gw["wd"] = _mm_tn(sv["f"], d_dn, WIDE_TILE, 1024, t + "dw_down")
        du2 = _mm_nn(d_gate, sv["wg"], F32, 512, D_FF, t + "du2_gate", tm_cap=DEEP_K_TOKEN_TILE)
        du2 = _mm_nn(d_up, sv["wu"], F32, 512, D_FF, t + "du2_up", acc=du2, tm_cap=DEEP_K_TOKEN_TILE)
        gw["wg"] = _mm_tn(d_gate, sv["u2"], WIDE_TILE, 1024, t + "dw_gate")
        gw["wu"] = _mm_tn(d_up, sv["u2"], WIDE_TILE, 1024, t + "dw_up")
        tok_b = grads_fn(l, "ffn", gw, du2)
        dh, d_mo, gs["ffn_pre"][l], gs["mix_post"][l] = _norm_bwd_pair(du2, sv["h_mid"], (n_ffn_pre[l] + tok_b)[None], dh, sv["mo"],
                                                                         n_mix_post[l][None], t + "norm_ffn_pre")
        d_mixed = _mm_nt(d_mo, sv["wo"], BF16, 1024, d, t + "d_mixed")
        gw["wo"] = _mm_tn(sv["mixed"], d_mo, 1024, 1024, t + "dw_out")
        d_o, dproj, gs["ret_norm"][l] = _retgate_bwd(d_mixed, sv["o_ret"], proj, ret_norm[l][None], t + "retgate")
        dq_r, dk_r, dv_r, dlg = _ret_bwd(sv["rq"], sv["rk"], proj, lg_all[l], d_o, sv["states"], t + "ret")
        draw = dlg[:, :, 0, 0] * lg_all[l]
        gs["dec_f"][l], gs["dec_b"][l] = draw[:, 0], draw[:, 1]
        dproj = _rope_ret(dq_r, 0, cos_r, -sin_r, t + "rope_rq", into=dproj, out_col=COL_RQ // 2)
        dproj = _rope_ret(dk_r, 0, cos_r, -sin_r, t + "rope_rk", into=dproj, out_col=COL_RK // 2)
        dproj = _rope_ret(dv_r, 0, None, None, t + "sum_rv", into=dproj, out_col=COL_RV // 2)
        dq_a, dk_a, dv_a, dsink = _att_bwd(sv["aq"], sv["ak"], proj, sv["sink_b"], d_mixed, t + "att")
        gs["sink"][l] = dsink[:, 0]
        dproj = _rope_att(dq_a, 0, ATT_HEADS, cos_a, -sin_a, perm, t + "rope_aq", into=dproj, out_col=0)
        dproj = _rope_att(dk_a, 0, ATT_KV_HEADS, cos_a, -sin_a, perm, t + "rope_ak", plain=dv_a, into=dproj, out_col=ATT_WIDTH // (2 * KV_WIDTH))
        gw["wi"] = _mm_tn(dproj, sv["u"], WIDE_TILE, 1024, t + "dw_in")
        tok_b = grads_fn(l, "mix", gw, dproj)
        du = _mm_nn(dproj, sv["wi"], F32, 512, IN_COLS, t + "du", tm_cap=DEEP_K_TOKEN_TILE)
        if l > 0:
            dh, d_dn, gs["mix_pre"][l], gs["ffn_post"][l - 1] = _norm_bwd_pair(du, sv["h"], (n_mix_pre[l] + tok_b)[None], dh, saved[l - 1]["dn"],
                                                                              n_ffn_post[l - 1][None], t + "norm_mix_pre")
        else:
            dh, gs["mix_pre"][l] = _norm_bwd(du, sv["h"], (n_mix_pre[l] + tok_b)[None], dh, F32, t + "norm_mix_pre")
    return loss_part[0, 0], dh, gs


def _pack_small(mix_pre, mix_post, ffn_pre, ffn_post, ret_norm, sink, dec_f, dec_b, loss, meta):
    d = D_MODEL

    def tile(a, rows=8):
        a = jnp.reshape(a, (-1, a.shape[-1])) if a.ndim else jnp.reshape(a, (1, 1))
        return jnp.pad(a, ((0, rows - a.shape[0]), (0, d - a.shape[1])))

    return jnp.concatenate([tile(mix_pre), tile(mix_post), tile(ffn_pre), tile(ffn_post), tile(ret_norm.reshape(-1, d)), tile(sink), tile(dec_f),
                            tile(dec_b), tile(loss), tile(meta, SMALL_ROWS - ROW_META)], axis=0)


def _unpack_small(p, depth):
    rows = lambda r0, cols: p[r0:r0 + depth, :cols]
    return dict(mix_pre=rows(ROW_MIX_PRE, D_MODEL), mix_post=rows(ROW_MIX_POST, D_MODEL), ffn_pre=rows(ROW_FFN_PRE, D_MODEL),
                ffn_post=rows(ROW_FFN_POST, D_MODEL), ret_norm=p[ROW_RET_NORM:ROW_RET_NORM + depth * RET_WIDTH // D_MODEL].reshape(depth, RET_WIDTH),
                sink=rows(ROW_SINK, ATT_HEADS), dec_f=rows(ROW_DEC_F, RET_HEADS), dec_b=rows(ROW_DEC_B, RET_HEADS), loss=p[ROW_LOSS, 0])


def kernel(x, meta_tokens, w_in, w_out, attn_sink, ret_decay_fwd, ret_decay_bwd, ret_norm, norm_mix_pre, norm_mix_post, w_gate, w_up, w_down, norm_ffn_pre, norm_ffn_post, loss_target, m_meta_tokens, m_w_in, m_w_out, m_attn_sink, m_ret_decay_fwd, m_ret_decay_bwd, m_ret_norm, m_norm_mix_pre, m_norm_mix_post, m_w_gate, m_w_up, m_w_down, m_norm_ffn_pre, m_norm_ffn_post, v_meta_tokens, v_w_in, v_w_out, v_attn_sink, v_ret_decay_fwd, v_ret_decay_bwd, v_ret_norm, v_norm_mix_pre, v_norm_mix_post, v_w_gate, v_w_up, v_w_down, v_norm_ffn_pre, v_norm_ffn_post):
    depth, d = w_in.shape[0], D_MODEL
    me = 4 * lax.axis_index("x") + 2 * lax.axis_index("y") + lax.axis_index("c")
    zero = jnp.zeros((), F32)

    meta_g, = _allgather([meta_tokens], "gather_meta")
    meta = meta_g.transpose(1, 0, 2).reshape(N_META, d)

    def shards(k):
        l = k // 2
        if k % 2 == 0:
            return [w_in[l].T.astype(BF16), w_out[l].astype(BF16)]
        return [w_gate[l].T.astype(BF16), w_up[l].T.astype(BF16), w_down[l].astype(BF16)]

    gathers, ahead = {}, 2
    for k in range(min(ahead + 1, 2 * depth)):
        gathers[k] = _send_start("gather", shards(k), None, gathers[k - 1][3] if k else meta_g, f"gather_start_g{k}")

    passing = {}

    def pass_on(k, after):
        lands = _send_wait("gather", gathers.pop(k), after, f"gather_wait_g{k}")
        passing[k] = _send_start("forward", [], lands, after, f"forward_start_g{k}")

    def take(k, h):
        after = h
        if k >= 1 and k + ahead < 2 * depth:
            gathers[k + ahead] = _send_start("gather", shards(k + ahead), None, h, f"gather_start_g{k + ahead}")
            after = gathers[k + ahead][3]
        elif k == 0:
            after = gathers[max(gathers)][3]
        if k not in passing:
            pass_on(k, after)
        if k >= 2 and k + 1 < 2 * depth:
            pass_on(k + 1, after)
        last = passing[k + 1][3] if k + 1 in passing else passing[k][3]
        return _send_wait("forward", passing.pop(k), last, f"forward_wait_g{k}")

    def mix_weights_fn(l, h):
        wi_t, wo = take(2 * l, h)
        return wi_t.reshape(IN_COLS, d), wo.reshape(d, d), zero

    def ffn_weights_fn(l, h):
        wg_t, wu_t, wd = take(2 * l + 1, h)
        return wg_t.reshape(D_FF, d), wu_t.reshape(D_FF, d), wd.reshape(D_FF, d), zero

    exchanges, adam, order = {}, {}, []
    tr = lambda *ts: tuple(jnp.swapaxes(t, 1, 2) for t in ts)
    big = dict(wi=tr(w_in, m_w_in, v_w_in), wg=tr(w_gate, m_w_gate, v_w_gate), wu=tr(w_up, m_w_up, v_w_up), wd=(w_down, m_w_down, v_w_down),
               wo=(w_out, m_w_out, v_w_out))
    kinds = dict(ffn=("wg", "wu", "wd"), mix=("wi", "wo"))

    arrivals = []

    def finish(key, after):
        l, part = key
        arrivals.append((key, _send_wait("scatter", exchanges.pop(key), after, f"exchange_wait_{part}_l{l}")))

    def update():
        last_start = exchanges[order[-1]][3] if order[-1] in exchanges else None
        while arrivals:
            (l, part), arrived = arrivals.pop(0)
            for kind, parts in zip(kinds[part], arrived):
                adam[kind] = _adamw(parts, *big[kind], f"adamw_{kind}_l{l}", layer=l, prev=adam.get(kind),
                                    after=None if kind in adam else last_start)

    def grads_fn(l, part, gw, after):
        packed = [gw[kind].reshape(N_DEV, -1, d) for kind in kinds[part]]
        exchanges[(l, part)] = _send_start("scatter", packed, None, after, f"exchange_start_{part}_l{l}")
        order.append((l, part))
        token = exchanges[(l, part)][3]
        if len(order) > 2:
            finish(order[-3], token)
        return token[0, 0]

    loss_part, dh, gs = _local_step(x[0], meta, loss_target[0], mix_weights_fn, ffn_weights_fn, grads_fn, attn_sink, ret_decay_fwd, ret_decay_bwd, ret_norm,
                                    norm_mix_pre, norm_mix_post, norm_ffn_pre, norm_ffn_post)
    grad_x = dh[BLOCK:][None]

    st = lambda xs: jnp.stack([t.reshape(-1) if t.ndim == 1 else t[0] for t in xs])
    small = _pack_small(st(gs["mix_pre"]), st(gs["mix_post"]), st(gs["ffn_pre"]), st(gs["ffn_post"]), st(gs["ret_norm"]), st(gs["sink"]),
                        st(gs["dec_f"]), st(gs["dec_b"]), loss_part, dh[PAD_FRONT:BLOCK])
    update()
    small_g, = _allgather([small], "gather_small", after=adam["wo"][0])
    for key in order[-2:]:
        finish(key, small_g)
    update()
    o_wi, o_wo, o_wg, o_wu, o_wd = tr(*adam["wi"]), adam["wo"], tr(*adam["wg"]), tr(*adam["wu"]), adam["wd"]
    zmeta = jnp.zeros((N_META, d), F32)
    packs = [_pack_small(a[0], a[1], a[2], a[3], a[4], a[5], a[6], a[7], zero, zmeta) for a in (
        (norm_mix_pre, norm_mix_post, norm_ffn_pre, norm_ffn_post, ret_norm, attn_sink, ret_decay_fwd, ret_decay_bwd),
        (m_norm_mix_pre, m_norm_mix_post, m_norm_ffn_pre, m_norm_ffn_post, m_ret_norm, m_attn_sink, m_ret_decay_fwd, m_ret_decay_bwd),
        (v_norm_mix_pre, v_norm_mix_post, v_norm_ffn_pre, v_norm_ffn_post, v_ret_norm, v_attn_sink, v_ret_decay_fwd, v_ret_decay_bwd))]
    o_small = [_unpack_small(o, depth) for o in _adamw(small_g, packs[0], packs[1], packs[2], "adamw_small")]
    meta_parts = lax.dynamic_slice(small_g, (0, ROW_META, me * (d // N_DEV)), (N_DEV, N_META, d // N_DEV))
    o_meta = _adamw(meta_parts, meta_tokens, m_meta_tokens, v_meta_tokens, "adamw_meta")

    outs = []
    for i in range(4):
        s = o_small[i]
        outs += [o_meta[i], o_wi[i], o_wo[i], s["sink"], s["dec_f"], s["dec_b"], s["ret_norm"], s["mix_pre"], s["mix_post"], o_wg[i], o_wu[i],
                 o_wd[i], s["ffn_pre"], s["ffn_post"]]
    return (o_small[0]["loss"], grad_x, *outs)
```

```python
import jax
import jax.numpy as jnp
import numpy as np
from jax import lax
from jax.experimental import pallas as pl
from jax.experimental.pallas import tpu as pltpu

F32, BF16 = jnp.float32, jnp.bfloat16

D_MODEL = 2048
N_META = 16
BLOCK = 128
WINDOW = 128
PAD_FRONT = BLOCK - N_META
ATT_HEAD_DIM = 128
ATT_WIDTH = D_MODEL // 2
ATT_HEADS = ATT_WIDTH // ATT_HEAD_DIM
ATT_KV_HEADS = 2
ATT_GROUP = ATT_HEADS // ATT_KV_HEADS
KV_WIDTH = ATT_KV_HEADS * ATT_HEAD_DIM
ROT_DIM = ATT_HEAD_DIM // 4
ROPE_THETA = 500000.0
RET_WIDTH = D_MODEL - ATT_WIDTH
RET_HEAD_DIM = 256
RET_HEADS = RET_WIDTH // RET_HEAD_DIM
RET_THETA = 10000.0
D_FF = 5632
IN_COLS = ATT_WIDTH + 2 * KV_WIDTH + 4 * RET_WIDTH
N_DEV = 8
SHARD_COLS = IN_COLS // N_DEV
EPS = 1e-6
NEG = -1e30
RET_K_SCALE = RET_HEAD_DIM ** -0.5
ATT_SCALE = ATT_HEAD_DIM ** -0.5

COL_AV256 = (ATT_WIDTH + KV_WIDTH) // 256
COL_RQ = (ATT_WIDTH + 2 * KV_WIDTH) // RET_HEAD_DIM
COL_RK = COL_RQ + RET_HEADS
COL_RV = COL_RK + RET_HEADS
COL_RG = COL_RV + RET_HEADS

ADAM_LR, ADAM_B1, ADAM_B2, ADAM_EPS, ADAM_WD, ADAM_STEP = 0.001, 0.9, 0.999, 1e-08, 0.01, 10

ROW_MIX_PRE, ROW_MIX_POST, ROW_FFN_PRE, ROW_FFN_POST, ROW_RET_NORM, ROW_SINK, ROW_DEC_F, ROW_DEC_B, ROW_LOSS, ROW_META, SMALL_ROWS = (
    0, 8, 16, 24, 32, 40, 48, 56, 64, 72, 96)
ADAMW_TILE_ELEMS = 128 * 1024

MESH = pl.DeviceIdType.MESH
ANY = pl.BlockSpec(memory_space=pl.ANY)


def _row_tile(n, cap):
    for t in range(cap - cap % 16, 0, -16):
        if n % t == 0:
            return t
    raise ValueError(n)


def _sds(shape, dtype):
    return jax.ShapeDtypeStruct(shape, dtype)


def _silu(x):
    return x * jax.nn.sigmoid(x)


def _dsilu(x):
    s = jax.nn.sigmoid(x)
    return s * (1.0 + x * (1.0 - s))


def _norm_fwd(x, g, res, out_dtype, name):
    n, d = x.shape
    tr = _row_tile(n, 384)

    def body(*refs):
        if res is None:
            x_ref, g_ref, o_ref = refs
        else:
            x_ref, g_ref, r_ref, o_ref = refs
        xv = x_ref[...]
        r = lax.rsqrt(jnp.mean(xv * xv, axis=-1, keepdims=True) + EPS)
        y = xv * r * g_ref[...]
        if res is not None:
            y = y + r_ref[...]
        o_ref[...] = y.astype(o_ref.dtype)

    row = pl.BlockSpec((tr, d), lambda i: (i, 0))
    ins = [row, pl.BlockSpec((1, d), lambda i: (0, 0))] + ([row] if res is not None else [])
    args = (x, g) + ((res,) if res is not None else ())
    return pl.pallas_call(body, grid=(n // tr,), in_specs=ins, out_specs=row, out_shape=_sds((n, d), out_dtype), name=name)(*args)


def _norm_res_norm(x, g, res, g_next, name):
    n, d = x.shape
    tr = _row_tile(n, 384)

    def body(x_ref, g_ref, r_ref, gn_ref, h_ref, u_ref):
        xv = x_ref[...]
        hv = r_ref[...] + xv * lax.rsqrt(jnp.mean(xv * xv, axis=-1, keepdims=True) + EPS) * g_ref[...]
        h_ref[...] = hv
        u_ref[...] = (hv * lax.rsqrt(jnp.mean(hv * hv, axis=-1, keepdims=True) + EPS) * gn_ref[...]).astype(u_ref.dtype)

    row = pl.BlockSpec((tr, d), lambda i: (i, 0))
    vec = pl.BlockSpec((1, d), lambda i: (0, 0))
    return pl.pallas_call(body, grid=(n // tr,), in_specs=[row, vec, row, vec], out_specs=(row, row),
                          out_shape=(_sds((n, d), F32), _sds((n, d), BF16)), name=name)(x, g, res, g_next)


OUT_PROJ_TOKEN_TILE = 192


def _out_proj_norms(x, w, res, g, g_next, name):
    n, k = x.shape
    d = w.shape[1]
    tm = _row_tile(n, OUT_PROJ_TOKEN_TILE)

    def body(x_ref, w_ref, r_ref, g_ref, gn_ref, mo_ref, h_ref, u_ref):
        mo = jnp.dot(x_ref[...], w_ref[...], preferred_element_type=F32)
        mo_ref[...] = mo
        hv = r_ref[...] + mo * lax.rsqrt(jnp.mean(mo * mo, axis=-1, keepdims=True) + EPS) * g_ref[...]
        h_ref[...] = hv
        u_ref[...] = (hv * lax.rsqrt(jnp.mean(hv * hv, axis=-1, keepdims=True) + EPS) * gn_ref[...]).astype(u_ref.dtype)

    row = pl.BlockSpec((tm, d), lambda i: (i, 0))
    vec = pl.BlockSpec((1, d), lambda i: (0, 0))
    return pl.pallas_call(body, grid=(n // tm,), in_specs=[pl.BlockSpec((tm, k), lambda i: (i, 0)), pl.BlockSpec((k, d), lambda i: (0, 0)), row, vec, vec],
                          out_specs=(row, row, row), out_shape=(_sds((n, d), F32), _sds((n, d), F32), _sds((n, d), BF16)), name=name)(
                              x, w, res, g, g_next)


def _norm_bwd(dy, x, g, res, out_dtype, name):
    n, d = x.shape
    tr = _row_tile(n, 384)

    def body(*refs):
        if res is None:
            dy_ref, x_ref, g_ref, dx_ref, dg_ref = refs
        else:
            dy_ref, x_ref, g_ref, r_ref, dx_ref, dg_ref = refs
        i = pl.program_id(0)
        xv = x_ref[...]
        r = lax.rsqrt(jnp.mean(xv * xv, axis=-1, keepdims=True) + EPS)
        xhat = xv * r
        dyf = dy_ref[...].astype(F32)
        gdy = dyf * g_ref[...]
        dx = r * (gdy - xhat * jnp.mean(gdy * xhat, axis=-1, keepdims=True))
        if res is not None:
            dx = dx + r_ref[...]
        dx_ref[...] = dx.astype(dx_ref.dtype)

        @pl.when(i == 0)
        def _():
            dg_ref[...] = jnp.zeros_like(dg_ref)

        dg_ref[...] += jnp.sum(dyf * xhat, axis=0, keepdims=True)

    row = pl.BlockSpec((tr, d), lambda i: (i, 0))
    vec = pl.BlockSpec((1, d), lambda i: (0, 0))
    ins = [row, row, vec] + ([row] if res is not None else [])
    args = (dy, x, g) + ((res,) if res is not None else ())
    return pl.pallas_call(body, grid=(n // tr,), in_specs=ins, out_specs=(row, vec),
                          out_shape=(_sds((n, d), out_dtype), _sds((1, d), F32)), name=name,
                          compiler_params=pltpu.CompilerParams(dimension_semantics=("arbitrary",)))(*args)


def _norm_bwd_pair(dy, x1, g1, res, x2, g2, name):
    n, d = x1.shape
    tr = _row_tile(n, 192)

    def rms_bwd(dyf, xv, g):
        r = lax.rsqrt(jnp.mean(xv * xv, axis=-1, keepdims=True) + EPS)
        xhat = xv * r
        gdy = dyf * g
        return r * (gdy - xhat * jnp.mean(gdy * xhat, axis=-1, keepdims=True)), jnp.sum(dyf * xhat, axis=0, keepdims=True)

    def body(dy_ref, x1_ref, g1_ref, r_ref, x2_ref, g2_ref, dh_ref, d2_ref, dg1_ref, dg2_ref):
        @pl.when(pl.program_id(0) == 0)
        def _():
            dg1_ref[...] = jnp.zeros_like(dg1_ref)
            dg2_ref[...] = jnp.zeros_like(dg2_ref)

        dx1, s1 = rms_bwd(dy_ref[...].astype(F32), x1_ref[...], g1_ref[...])
        dh = dx1 + r_ref[...]
        dh_ref[...] = dh
        dx2, s2 = rms_bwd(dh, x2_ref[...], g2_ref[...])
        d2_ref[...] = dx2.astype(d2_ref.dtype)
        dg1_ref[...] += s1
        dg2_ref[...] += s2

    row = pl.BlockSpec((tr, d), lambda i: (i, 0))
    vec = pl.BlockSpec((1, d), lambda i: (0, 0))
    return pl.pallas_call(body, grid=(n // tr,), in_specs=[row, row, vec, row, row, vec], out_specs=(row, row, vec, vec),
                          out_shape=(_sds((n, d), F32), _sds((n, d), BF16), _sds((1, d), F32), _sds((1, d), F32)), name=name,
                          compiler_params=pltpu.CompilerParams(dimension_semantics=("arbitrary",)))(dy, x1, g1, res, x2, g2)


def _mm(a, b, *, ta, tb, grid, a_blk, a_map, b_blk, b_map, o_blk, o_map, o_shape, o_dtype, name, acc=None):
    nk = grid[2]
    dims = (((0,) if ta else (1,), (1,) if tb else (0,)), ((), ()))

    def body(*refs):
        if acc is None:
            a_ref, b_ref, o_ref = refs[:3]
            c_ref = None
        else:
            a_ref, b_ref, c_ref, o_ref = refs[:4]
        part = lax.dot_general(a_ref[...], b_ref[...], dims, preferred_element_type=F32)
        if nk == 1:
            if c_ref is not None:
                part = part + c_ref[...].astype(F32)
            o_ref[...] = part.astype(o_ref.dtype)
            return
        acc_ref = refs[-1]
        k = pl.program_id(2)

        @pl.when(k == 0)
        def _():
            acc_ref[...] = jnp.zeros_like(acc_ref) if c_ref is None else c_ref[...].astype(F32)

        acc_ref[...] += part

        @pl.when(k == nk - 1)
        def _():
            o_ref[...] = acc_ref[...].astype(o_ref.dtype)

    ins = [pl.BlockSpec(a_blk, a_map), pl.BlockSpec(b_blk, b_map)]
    args = [a, b]
    if acc is not None:
        ins.append(pl.BlockSpec(o_blk, o_map))
        args.append(acc)
    return pl.pallas_call(body, grid=grid, in_specs=ins, out_specs=pl.BlockSpec(o_blk, o_map), out_shape=_sds(o_shape, o_dtype),
                          scratch_shapes=[pltpu.VMEM(o_blk, F32)] if nk > 1 else [], name=name,
                          compiler_params=pltpu.CompilerParams(dimension_semantics=("parallel", "parallel", "arbitrary")))(*args)


TOKEN_TILE = 1056
WIDE_TILE = 1408
DEEP_K_TOKEN_TILE = 528


def _mm_nn(x, w, o_dtype, tn, tk, name, acc=None, tm_cap=TOKEN_TILE):
    n, k = x.shape
    tm = _row_tile(n, tm_cap)
    return _mm(x, w, ta=False, tb=False, grid=(n // tm, w.shape[1] // tn, k // tk), a_blk=(tm, tk), a_map=lambda i, j, kk: (i, kk),
               b_blk=(tk, tn), b_map=lambda i, j, kk: (kk, j), o_blk=(tm, tn), o_map=lambda i, j, kk: (i, j),
               o_shape=(n, w.shape[1]), o_dtype=o_dtype, name=name, acc=acc)


def _mm_nt(dy, w, o_dtype, tn, tk, name, acc=None):
    n, k = dy.shape
    tm = _row_tile(n, TOKEN_TILE)
    return _mm(dy, w, ta=False, tb=True, grid=(n // tm, w.shape[0] // tn, k // tk), a_blk=(tm, tk), a_map=lambda i, j, kk: (i, kk),
               b_blk=(tn, tk), b_map=lambda i, j, kk: (j, kk), o_blk=(tm, tn), o_map=lambda i, j, kk: (i, j),
               o_shape=(n, w.shape[0]), o_dtype=o_dtype, name=name, acc=acc)


def _mm_tn(x, dy, tm, tn, name):
    n, m = x.shape
    tk = _row_tile(n, 2 * TOKEN_TILE)
    return _mm(x, dy, ta=True, tb=False, grid=(m // tm, dy.shape[1] // tn, n // tk), a_blk=(tk, tm), a_map=lambda i, j, kk: (kk, i),
               b_blk=(tk, tn), b_map=lambda i, j, kk: (kk, j), o_blk=(tm, tn), o_map=lambda i, j, kk: (i, j),
               o_shape=(m, dy.shape[1]), o_dtype=BF16, name=name)


def _rope_tables(n):
    pos = (jnp.arange(n) - PAD_FRONT).astype(F32)
    half = ROT_DIM // 2
    ang = pos[:, None] * (ROPE_THETA ** (-jnp.arange(half, dtype=F32) / half))[None, :]
    c, s = jnp.cos(ang), jnp.sin(ang)
    rest = ATT_HEAD_DIM - ROT_DIM
    cos_a = jnp.concatenate([c, c, jnp.ones((n, rest), F32)], axis=1)
    sin_a = jnp.concatenate([-s, s, jnp.zeros((n, rest), F32)], axis=1)
    half = RET_HEAD_DIM // 2
    ang = pos[:, None] * (RET_THETA ** (-jnp.arange(half, dtype=F32) / half))[None, :]
    c, s = jnp.cos(ang), jnp.sin(ang)
    perm = np.zeros((ATT_HEAD_DIM, ATT_HEAD_DIM), np.float32)
    for i in range(ROT_DIM):
        perm[(i + ROT_DIM // 2) % ROT_DIM, i] = 1.0
    return cos_a, sin_a, jnp.concatenate([c, c], axis=1), jnp.concatenate([-s, s], axis=1), jnp.asarray(perm, BF16)


def _into(buf, own_shape):
    if buf is None:
        return _sds(own_shape, BF16), [], [], lambda n_inputs: {}
    return _sds(buf.shape, buf.dtype), [ANY], [buf], lambda n_inputs: {n_inputs: 0}


def _rope_att(x, col0, heads, cos, sin, perm, name, plain=None, into=None, out_col=0):
    n = x.shape[0]
    tr = _row_tile(n, 1056)
    hd = ATT_HEAD_DIM
    w2 = 0 if plain is None else plain.shape[1]
    width = heads * hd + w2

    def body(*refs):
        x_ref, c_ref, s_ref, p_ref, o_ref = refs[0], refs[1], refs[2], refs[3], refs[-1]
        for h in range(heads):
            cs = slice(h * hd, (h + 1) * hd)
            xb = x_ref[:, cs].astype(BF16)
            sw = jnp.dot(xb, p_ref[...], preferred_element_type=F32)
            o_ref[:, cs] = (xb.astype(F32) * c_ref[...] + sw * s_ref[...]).astype(o_ref.dtype)
        if plain is not None:
            o_ref[:, heads * hd:] = refs[4][...].astype(o_ref.dtype)

    tab = pl.BlockSpec((tr, hd), lambda i: (i, 0))
    ins = [pl.BlockSpec((tr, heads * hd), lambda i: (i, col0)), tab, tab, pl.BlockSpec((hd, hd), lambda i: (0, 0))]
    args = [x, cos, sin, perm]
    if plain is not None:
        ins.append(pl.BlockSpec((tr, w2), lambda i: (i, 0)))
        args.append(plain)
    shape, extra_specs, extra_args, alias = _into(into, (n, width))
    return pl.pallas_call(body, grid=(n // tr,), in_specs=ins + extra_specs, out_specs=pl.BlockSpec((tr, width), lambda i: (i, out_col)),
                          out_shape=shape, input_output_aliases=alias(len(ins)), name=name)(*args, *extra_args)


def _rope_ret(x, col0, cos, sin, name, into=None, out_col=0):
    p, n, _ = x.shape
    tr = _row_tile(n, 1056)
    hd = RET_HEAD_DIM

    def body(*refs):
        x_ref, o_ref = refs[0], refs[-1]
        for h in range(2):
            cs = slice(h * hd, (h + 1) * hd)
            xv = x_ref[0, :, cs].astype(F32)
            for q in range(1, p):
                xv = xv + x_ref[q, :, cs].astype(F32)
            if cos is not None:
                sw = jnp.concatenate([xv[:, hd // 2:], xv[:, :hd // 2]], axis=1)
                xv = xv * refs[1][...] + sw * refs[2][...]
            o_ref[:, cs] = xv.astype(o_ref.dtype)

    tab = pl.BlockSpec((tr, hd), lambda i, j: (i, 0))
    ins = [pl.BlockSpec((p, tr, 2 * hd), lambda i, j: (0, i, col0 + j))] + ([tab, tab] if cos is not None else [])
    args = (x,) + ((cos, sin) if cos is not None else ())
    shape, extra_specs, extra_args, alias = _into(into, (n, RET_WIDTH))
    return pl.pallas_call(body, grid=(n // tr, RET_HEADS // 2), in_specs=ins + extra_specs,
                          out_specs=pl.BlockSpec((tr, 2 * hd), lambda i, j: (i, out_col + j)), out_shape=shape,
                          input_output_aliases=alias(len(ins)), name=name)(*args, *extra_args)


def _att_mask(nblk, n_tot):
    row = lax.broadcasted_iota(jnp.int32, (BLOCK, 4 * BLOCK), 0)
    col = lax.broadcasted_iota(jnp.int32, (BLOCK, 4 * BLOCK), 1)
    qi = nblk * BLOCK + row
    seg = col // BLOCK
    cj = col % BLOCK
    kj = (nblk - 1 + seg) * BLOCK + cj
    band = (jnp.abs(qi - kj) <= WINDOW) & (kj >= PAD_FRONT) & (kj < n_tot) & (seg < 3)
    meta = (seg == 3) & (cj >= PAD_FRONT) & (jnp.abs(qi - cj) > WINDOW)
    return band | meta


def _att_specs(nb, v_col):
    kv = lambda f, cb: pl.BlockSpec((BLOCK, KV_WIDTH), lambda n: (f(n), cb))
    prev, own, nxt, first = (lambda n: jnp.maximum(n - 1, 0)), (lambda n: n), (lambda n: jnp.minimum(n + 1, nb - 1)), (lambda n: 0)
    return [kv(f, 0) for f in (prev, own, nxt, first)] + [kv(f, v_col) for f in (prev, own, nxt, first)]


def _att_probs(s, ok, snk):
    s = jnp.where(ok, s, NEG)
    m = jnp.maximum(jnp.max(s, axis=-1, keepdims=True), snk)
    p = jnp.exp(s - m)
    ps = jnp.exp(snk - m)
    inv = 1.0 / (jnp.sum(p, axis=-1, keepdims=True) + ps)
    return p * inv, ps * inv


def _att_fwd(q, k, proj, sink_b, name):
    n = q.shape[0]
    nb = n // BLOCK
    hd = ATT_HEAD_DIM

    def body(q_ref, kp, ko, kn, km, vp, vo, vn, vm, sink_ref, o_ref):
        nblk = pl.program_id(0)
        ok = _att_mask(nblk, n)
        keep = (nblk * BLOCK + lax.broadcasted_iota(jnp.int32, (BLOCK, 1), 0)) >= PAD_FRONT
        for kh in range(ATT_KV_HEADS):
            cs = slice(kh * hd, (kh + 1) * hd)
            kk = jnp.concatenate([r[:, cs] for r in (kp, ko, kn, km)], axis=0)
            vv = jnp.concatenate([r[:, cs] for r in (vp, vo, vn, vm)], axis=0)
            heads = [kh * ATT_GROUP + g for g in range(ATT_GROUP)]
            q4 = jnp.concatenate([q_ref[:, h * hd:(h + 1) * hd] for h in heads], axis=0)
            s = lax.dot_general(q4, kk, (((1,), (1,)), ((), ())), preferred_element_type=F32) * ATT_SCALE
            ps = []
            for g, h in enumerate(heads):
                p, _ = _att_probs(s[g * BLOCK:(g + 1) * BLOCK], ok, sink_ref[h:h + 1, 0:1])
                ps.append(p)
            o = jnp.dot(jnp.concatenate(ps, axis=0).astype(BF16), vv, preferred_element_type=F32)
            for g, h in enumerate(heads):
                o_ref[:, h * hd:(h + 1) * hd] = jnp.where(keep, o[g * BLOCK:(g + 1) * BLOCK], 0.0).astype(o_ref.dtype)

    qspec = pl.BlockSpec((BLOCK, ATT_WIDTH), lambda i: (i, 0))
    return pl.pallas_call(body, grid=(nb,), in_specs=[qspec] + _att_specs(nb, COL_AV256) + [pl.BlockSpec((ATT_HEADS, 128), lambda i: (0, 0))],
                          out_specs=qspec, out_shape=_sds((n, D_MODEL), BF16), name=name)(q, k, k, k, k, proj, proj, proj, proj, sink_b)


def _att_bwd(q, k, proj, sink_b, dmixed, name):
    n = q.shape[0]
    nb = n // BLOCK
    hd = ATT_HEAD_DIM

    def body(q_ref, kp, ko, kn, km, vp, vo, vn, vm, sink_ref, do_ref, dq_ref, dk_ref, dv_ref, dsink_ref):
        nblk = pl.program_id(0)

        @pl.when(nblk == 0)
        def _():
            dk_ref[...] = jnp.zeros_like(dk_ref)
            dv_ref[...] = jnp.zeros_like(dv_ref)
            dsink_ref[...] = jnp.zeros_like(dsink_ref)

        ok = _att_mask(nblk, n)
        rows = [jnp.maximum(nblk - 1, 0), nblk, jnp.minimum(nblk + 1, nb - 1), 0]
        for kh in range(ATT_KV_HEADS):
            cs = slice(kh * hd, (kh + 1) * hd)
            kk = jnp.concatenate([r[:, cs] for r in (kp, ko, kn, km)], axis=0)
            vv = jnp.concatenate([r[:, cs] for r in (vp, vo, vn, vm)], axis=0)
            heads = [kh * ATT_GROUP + g for g in range(ATT_GROUP)]
            q4 = jnp.concatenate([q_ref[:, h * hd:(h + 1) * hd] for h in heads], axis=0)
            do4 = jnp.concatenate([do_ref[:, h * hd:(h + 1) * hd] for h in heads], axis=0)
            s = lax.dot_general(q4, kk, (((1,), (1,)), ((), ())), preferred_element_type=F32) * ATT_SCALE
            dp = lax.dot_general(do4, vv, (((1,), (1,)), ((), ())), preferred_element_type=F32)
            ps, dss = [], []
            for g, h in enumerate(heads):
                p, psink = _att_probs(s[g * BLOCK:(g + 1) * BLOCK], ok, sink_ref[h:h + 1, 0:1])
                dpg = dp[g * BLOCK:(g + 1) * BLOCK]
                delta = jnp.sum(p * dpg, axis=-1, keepdims=True)
                ps.append(p)
                dss.append(p * (dpg - delta) * ATT_SCALE)
                dsink_ref[h:h + 1, :] = dsink_ref[h:h + 1, :] - jnp.sum(psink * delta, axis=0, keepdims=True)
            ds = jnp.concatenate(dss, axis=0).astype(BF16)
            pb = jnp.concatenate(ps, axis=0).astype(BF16)
            dq = jnp.dot(ds, kk, preferred_element_type=F32)
            for g, h in enumerate(heads):
                dq_ref[:, h * hd:(h + 1) * hd] = dq[g * BLOCK:(g + 1) * BLOCK].astype(dq_ref.dtype)
            dk = lax.dot_general(ds, q4, (((0,), (0,)), ((), ())), preferred_element_type=F32)
            dv = lax.dot_general(pb, do4, (((0,), (0,)), ((), ())), preferred_element_type=F32)
            for seg, r in enumerate(rows):
                at = (pl.ds(pl.multiple_of(r * BLOCK, BLOCK), BLOCK), cs)
                dk_ref[at] += dk[seg * BLOCK:(seg + 1) * BLOCK]
                dv_ref[at] += dv[seg * BLOCK:(seg + 1) * BLOCK]

    qspec = pl.BlockSpec((BLOCK, ATT_WIDTH), lambda i: (i, 0))
    whole = pl.BlockSpec((n, KV_WIDTH), lambda i: (0, 0))
    sinks = pl.BlockSpec((ATT_HEADS, 128), lambda i: (0, 0))
    return pl.pallas_call(body, grid=(nb,), in_specs=[qspec] + _att_specs(nb, COL_AV256) + [sinks, qspec], out_specs=(qspec, whole, whole, sinks),
                          out_shape=(_sds((n, ATT_WIDTH), BF16), _sds((n, KV_WIDTH), F32), _sds((n, KV_WIDTH), F32), _sds((ATT_HEADS, 128), F32)),
                          name=name, compiler_params=pltpu.CompilerParams(dimension_semantics=("arbitrary",)))(
                              q, k, k, k, k, proj, proj, proj, proj, sink_b, dmixed)


def _ret_decay(lg, d):
    a = lax.broadcasted_iota(jnp.int32, (BLOCK, 1), 0)
    b = lax.broadcasted_iota(jnp.int32, (1, BLOCK), 1)
    t_col = a + d * (BLOCK - 1 - 2 * a)
    t_row = b + d * (BLOCK - 1 - 2 * b)
    diff = t_col - t_row
    dist = jnp.maximum(diff, 0).astype(F32)
    dmask = jnp.where(diff >= d, jnp.exp(lg * dist), 0.0)
    tf = t_col.astype(F32)
    xi = jnp.exp(lg * (tf + 1.0))
    zeta = jnp.exp(lg * (BLOCK - 1.0 - tf))
    gam = jnp.exp(jnp.full((1, 1), BLOCK, F32) * lg)
    return dmask, dist, xi, zeta, gam, tf


RET_GROUP = 3


def _ret_group(nc):
    return next(c for c in (RET_GROUP, 2, 1) if nc % c == 0)


def _ret_fwd(q, k, proj, lg, name):
    n = q.shape[0]
    nc = n // BLOCK
    hd = RET_HEAD_DIM
    ch = _ret_group(nc)
    ns, rows = nc // ch, ch * BLOCK
    group = lambda d, s: s + d * (ns - 1 - 2 * s)

    def body(lg_ref, q_ref, k_ref, v0, v1, v2, v3, o_ref, st_ref, s_ref):
        d, s = pl.program_id(0), pl.program_id(1)

        @pl.when(s == 0)
        def _():
            s_ref[...] = jnp.zeros_like(s_ref)

        for h, v_ref in enumerate((v0, v1, v2, v3)):
            cs = slice(h * hd, (h + 1) * hd)
            dmask, _, xi, zeta, gam, _ = _ret_decay(lg_ref[h, d], d)
            sb = s_ref[h]
            for j in range(ch):
                at = pl.ds(pl.multiple_of((j + d * (ch - 1 - 2 * j)) * BLOCK, BLOCK), BLOCK)
                qv = q_ref[at, cs]
                kf = k_ref[at, cs].astype(F32) * RET_K_SCALE
                vv = v_ref[at, :]
                sc = lax.dot_general(qv, kf.astype(BF16), (((1,), (1,)), ((), ())), preferred_element_type=F32)
                o_ref[at, cs] = (jnp.dot((sc * dmask).astype(BF16), vv, preferred_element_type=F32)
                                 + jnp.dot((qv.astype(F32) * xi).astype(BF16), sb.astype(BF16), preferred_element_type=F32))
                st_ref[h, j] = sb
                sb = gam * sb + lax.dot_general((kf * zeta).astype(BF16), vv, (((0,), (0,)), ((), ())), preferred_element_type=F32)
            s_ref[h] = sb

    wide = pl.BlockSpec((rows, RET_WIDTH), lambda d, s: (group(d, s), 0))
    vblk = lambda h: pl.BlockSpec((rows, hd), lambda d, s: (group(d, s), COL_RV + h))
    return pl.pallas_call(
        body, grid=(2, ns), in_specs=[pl.BlockSpec(memory_space=pltpu.SMEM), wide, wide] + [vblk(h) for h in range(RET_HEADS)],
        out_specs=(pl.BlockSpec((None, rows, RET_WIDTH), lambda d, s: (d, group(d, s), 0)),
                   pl.BlockSpec((RET_HEADS, None, ch, hd, hd), lambda d, s: (0, d, s, 0, 0))),
        out_shape=(_sds((2, n, RET_WIDTH), F32), _sds((RET_HEADS, 2, nc, hd, hd), F32)), scratch_shapes=[pltpu.VMEM((RET_HEADS, hd, hd), F32)],
        name=name, compiler_params=pltpu.CompilerParams(dimension_semantics=("parallel", "arbitrary")))(lg, q, k, proj, proj, proj, proj)


def _ret_bwd(q, k, proj, lg, do, states, name):
    n = q.shape[0]
    nc = n // BLOCK
    hd = RET_HEAD_DIM
    ch = _ret_group(nc)
    ns, rows = nc // ch, ch * BLOCK
    group = lambda d, r: (ns - 1 - r) + d * (2 * r - (ns - 1))

    def body(lg_ref, q_ref, k_ref, v0, v1, v2, v3, do_ref, st_ref, dq_ref, dk_ref, dv_ref, dlg_ref, ds_ref):
        d, r = pl.program_id(0), pl.program_id(1)

        @pl.when(r == 0)
        def _():
            ds_ref[...] = jnp.zeros_like(ds_ref)
            dlg_ref[...] = jnp.zeros_like(dlg_ref)

        first_row = group(d, r) * rows
        nt = (((1,), (1,)), ((), ()))
        tn = (((0,), (0,)), ((), ()))
        for h, v_ref in enumerate((v0, v1, v2, v3)):
            cs = slice(h * hd, (h + 1) * hd)
            dmask, dist, xi, zeta, gam, tf = _ret_decay(lg_ref[h, d], d)
            dsn = ds_ref[h]
            dlg = jnp.zeros((1, 1), F32)
            for j in reversed(range(ch)):
                start = (j + d * (ch - 1 - 2 * j)) * BLOCK
                at = pl.ds(pl.multiple_of(start, BLOCK), BLOCK)
                keep = (lax.broadcasted_iota(jnp.int32, (BLOCK, 1), 0) + first_row + start) >= PAD_FRONT
                qv, vv, dov = q_ref[at, cs], v_ref[at, :], do_ref[at, cs]
                qf = qv.astype(F32)
                kf = k_ref[at, cs].astype(F32) * RET_K_SCALE
                kb = kf.astype(BF16)
                sc = st_ref[h, j]
                s = lax.dot_general(qv, kb, nt, preferred_element_type=F32)
                dsc = lax.dot_general(dov, vv, nt, preferred_element_type=F32) * dmask
                dsb = dsc.astype(BF16)
                dq_c = xi * lax.dot_general(dov, sc.astype(BF16), nt, preferred_element_type=F32)
                dk_c = zeta * lax.dot_general(vv, dsn.astype(BF16), nt, preferred_element_type=F32)
                dq = jnp.dot(dsb, kb, preferred_element_type=F32) + dq_c
                dk = lax.dot_general(dsb, qv, tn, preferred_element_type=F32) + dk_c
                dv = (lax.dot_general((s * dmask).astype(BF16), dov, tn, preferred_element_type=F32)
                      + jnp.dot((kf * zeta).astype(BF16), dsn.astype(BF16), preferred_element_type=F32))
                dlg = dlg + (jnp.sum(dsc * s * dist, keepdims=True)
                             + jnp.sum((tf + 1.0) * jnp.sum(qf * dq_c, axis=-1, keepdims=True), keepdims=True)
                             + jnp.sum((BLOCK - 1.0 - tf) * jnp.sum(kf * dk_c, axis=-1, keepdims=True), keepdims=True)
                             + BLOCK * gam * jnp.sum(dsn * sc, keepdims=True))
                dsn = gam * dsn + lax.dot_general((qf * xi).astype(BF16), dov, tn, preferred_element_type=F32)
                dq_ref[at, cs] = dq
                dk_ref[at, cs] = jnp.where(keep, dk * RET_K_SCALE, 0.0)
                dv_ref[at, cs] = jnp.where(keep, dv, 0.0)
            ds_ref[h] = dsn
            dlg_ref[h] += dlg

    wide = pl.BlockSpec((rows, RET_WIDTH), lambda d, r: (group(d, r), 0))
    vblk = lambda h: pl.BlockSpec((rows, hd), lambda d, r: (group(d, r), COL_RV + h))
    plane = pl.BlockSpec((None, rows, RET_WIDTH), lambda d, r: (d, group(d, r), 0))
    return pl.pallas_call(
        body, grid=(2, ns),
        in_specs=[pl.BlockSpec(memory_space=pltpu.SMEM), wide, wide] + [vblk(h) for h in range(RET_HEADS)]
        + [wide, pl.BlockSpec((RET_HEADS, None, ch, hd, hd), lambda d, r: (0, d, ns - 1 - r, 0, 0))],
        out_specs=(plane, plane, plane, pl.BlockSpec((RET_HEADS, None, 8, 128), lambda d, r: (0, d, 0, 0))),
        out_shape=(_sds((2, n, RET_WIDTH), F32),) * 3 + (_sds((RET_HEADS, 2, 8, 128), F32),),
        scratch_shapes=[pltpu.VMEM((RET_HEADS, hd, hd), F32)], name=name,
        compiler_params=pltpu.CompilerParams(dimension_semantics=("parallel", "arbitrary")))(lg, q, k, proj, proj, proj, proj, do, states)


def _retgate_fwd(o, proj, gain, mixed, name):
    _, n, _ = o.shape
    tr = _row_tile(n, 1056)
    hd = RET_HEAD_DIM

    def body(o_ref, rg_ref, g_ref, _, y_ref):
        ov = o_ref[0] + o_ref[1]
        r = lax.rsqrt(jnp.mean(ov * ov, axis=-1, keepdims=True) + EPS)
        y_ref[...] = (_silu(rg_ref[...].astype(F32)) * (ov * r * g_ref[...])).astype(y_ref.dtype)

    return pl.pallas_call(body, grid=(n // tr, RET_HEADS),
                          in_specs=[pl.BlockSpec((2, tr, hd), lambda i, h: (0, i, h)), pl.BlockSpec((tr, hd), lambda i, h: (i, COL_RG + h)),
                                    pl.BlockSpec((1, hd), lambda i, h: (0, h)), ANY],
                          out_specs=pl.BlockSpec((tr, hd), lambda i, h: (i, ATT_WIDTH // hd + h)), out_shape=_sds(mixed.shape, mixed.dtype),
                          input_output_aliases={3: 0}, name=name)(o, proj, gain, mixed)


def _retgate_bwd(dmixed, o, proj, gain, name):
    _, n, _ = o.shape
    tr = _row_tile(n, 1056)
    hd = RET_HEAD_DIM

    def body(dy_ref, o_ref, rg_ref, g_ref, do_ref, drg_ref, dg_ref):
        i = pl.program_id(1)
        ov = o_ref[0] + o_ref[1]
        r = lax.rsqrt(jnp.mean(ov * ov, axis=-1, keepdims=True) + EPS)
        xhat = ov * r
        rg = rg_ref[...].astype(F32)
        dy = dy_ref[...].astype(F32)
        drg_ref[...] = (dy * (xhat * g_ref[...]) * _dsilu(rg)).astype(drg_ref.dtype)
        dn = dy * _silu(rg)
        dxh = dn * g_ref[...]
        do_ref[...] = (r * (dxh - xhat * jnp.mean(dxh * xhat, axis=-1, keepdims=True))).astype(do_ref.dtype)

        @pl.when(i == 0)
        def _():
            dg_ref[...] = jnp.zeros_like(dg_ref)

        dg_ref[...] += jnp.sum(dn * xhat, axis=0, keepdims=True)

    tile = pl.BlockSpec((tr, hd), lambda h, i: (i, h))
    vec = pl.BlockSpec((1, hd), lambda h, i: (0, h))
    rg_cols = pl.BlockSpec((tr, hd), lambda h, i: (i, COL_RG + h))
    return pl.pallas_call(body, grid=(RET_HEADS, n // tr),
                          in_specs=[pl.BlockSpec((tr, hd), lambda h, i: (i, ATT_WIDTH // hd + h)), pl.BlockSpec((2, tr, hd), lambda h, i: (0, i, h)),
                                    rg_cols, vec],
                          out_specs=(tile, rg_cols, vec), out_shape=(_sds((n, RET_WIDTH), BF16), _sds((n, IN_COLS), BF16), _sds((1, RET_WIDTH), F32)),
                          name=name, compiler_params=pltpu.CompilerParams(dimension_semantics=("parallel", "arbitrary")))(dmixed, o, proj, gain)


FFN_TILE = 512


def _swiglu_fwd(x, wg_t, wu_t, name):
    n, k = x.shape
    tm = _row_tile(n, TOKEN_TILE)
    nt = (((1,), (1,)), ((), ()))

    def body(x_ref, g_ref, u_ref, go_ref, uo_ref, f_ref):
        g = lax.dot_general(x_ref[...], g_ref[...], nt, preferred_element_type=F32)
        u = lax.dot_general(x_ref[...], u_ref[...], nt, preferred_element_type=F32)
        go_ref[...] = g.astype(go_ref.dtype)
        uo_ref[...] = u.astype(uo_ref.dtype)
        f_ref[...] = (_silu(g) * u).astype(f_ref.dtype)

    w = pl.BlockSpec((FFN_TILE, k), lambda i, j: (j, 0))
    o = pl.BlockSpec((tm, FFN_TILE), lambda i, j: (i, j))
    return pl.pallas_call(body, grid=(n // tm, wg_t.shape[0] // FFN_TILE), in_specs=[pl.BlockSpec((tm, k), lambda i, j: (i, 0)), w, w],
                          out_specs=(o, o, o), out_shape=(_sds((n, wg_t.shape[0]), BF16),) * 3, name=name,
                          compiler_params=pltpu.CompilerParams(dimension_semantics=("parallel", "parallel")))(x, wg_t, wu_t)


def _swiglu_bwd(dy, wd, gate, up, name):
    n, k = dy.shape
    tm = _row_tile(n, TOKEN_TILE)
    nt = (((1,), (1,)), ((), ()))

    def body(dy_ref, w_ref, g_ref, u_ref, dg_ref, du_ref):
        df = lax.dot_general(dy_ref[...], w_ref[...], nt, preferred_element_type=F32)
        g = g_ref[...].astype(F32)
        dg_ref[...] = (df * u_ref[...].astype(F32) * _dsilu(g)).astype(dg_ref.dtype)
        du_ref[...] = (df * _silu(g)).astype(du_ref.dtype)

    o = pl.BlockSpec((tm, FFN_TILE), lambda i, j: (i, j))
    return pl.pallas_call(body, grid=(n // tm, wd.shape[0] // FFN_TILE),
                          in_specs=[pl.BlockSpec((tm, k), lambda i, j: (i, 0)), pl.BlockSpec((FFN_TILE, k), lambda i, j: (j, 0)), o, o],
                          out_specs=(o, o), out_shape=(_sds((n, wd.shape[0]), BF16),) * 2, name=name,
                          compiler_params=pltpu.CompilerParams(dimension_semantics=("parallel", "parallel")))(dy, wd, gate, up)


def _loss_head(h, target, name):
    n, d = h.shape
    nb = n // BLOCK

    def body(h_ref, t_ref, dh_ref, l_ref):
        i = pl.program_id(0)

        @pl.when(i == 0)
        def _():
            l_ref[...] = jnp.zeros_like(l_ref)
            dh_ref[...] = jnp.zeros_like(dh_ref)

        @pl.when(i > 0)
        def _():
            e = h_ref[...] - t_ref[...]
            dh_ref[...] = e * (1.0 / d)
            l_ref[...] += 0.5 * jnp.sum(jnp.mean(e * e, axis=-1, keepdims=True), keepdims=True)

    blk = pl.BlockSpec((BLOCK, d), lambda i: (i, 0))
    return pl.pallas_call(body, grid=(nb,), in_specs=[blk, pl.BlockSpec((BLOCK, d), lambda i: (jnp.maximum(i - 1, 0), 0))],
                          out_specs=(blk, pl.BlockSpec((8, 128), lambda i: (0, 0))), out_shape=(_sds((n, d), F32), _sds((8, 128), F32)), name=name,
                          compiler_params=pltpu.CompilerParams(dimension_semantics=("arbitrary",)))(h, target)


def _adamw(parts, w, m, v, name, sel=None, layer=None, prev=None, after=None):
    s, (r, c) = parts.shape[0], parts.shape[-2:]
    tr = _row_tile(r, max(16, (ADAMW_TILE_ELEMS // c) // 16 * 16))
    b1c, b2c = 1.0 - ADAM_B1 ** ADAM_STEP, 1.0 - ADAM_B2 ** ADAM_STEP

    def body(p_ref, w_ref, m_ref, v_ref, *rest):
        g_ref, d_ref, mo_ref, vo_ref = rest[-4:]
        g = p_ref[0].astype(F32)
        for q in range(1, s):
            g = g + p_ref[q].astype(F32)
        mn = ADAM_B1 * m_ref[...] + (1.0 - ADAM_B1) * g
        vn = ADAM_B2 * v_ref[...] + (1.0 - ADAM_B2) * jnp.square(g)
        g_ref[...] = g
        mo_ref[...] = mn
        vo_ref[...] = vn
        d_ref[...] = -ADAM_LR * ((mn / b1c) / (jnp.sqrt(vn / b2c) + ADAM_EPS) + ADAM_WD * w_ref[...])

    pspec = (pl.BlockSpec((s, tr, c), lambda i: (0, i, 0)) if sel is None else pl.BlockSpec((s, None, tr, c), lambda i: (0, sel, i, 0)))
    if layer is None:
        t = pl.BlockSpec((tr, c), lambda i: (i, 0))
        return pl.pallas_call(body, grid=(r // tr,), in_specs=[pspec, t, t, t], out_specs=(t, t, t, t), out_shape=(_sds((r, c), F32),) * 4,
                              name=name)(parts, w, m, v)
    t = pl.BlockSpec((None, tr, c), lambda i: (layer, i, 0))
    prev = prev if prev is not None else tuple(lax.empty(w.shape, F32) for _ in range(4))
    extra = [] if after is None else [after]
    return pl.pallas_call(body, grid=(r // tr,), in_specs=[pspec, t, t, t] + [ANY] * (4 + len(extra)), out_specs=(t, t, t, t),
                          out_shape=(_sds(w.shape, F32),) * 4, input_output_aliases={4 + i: i for i in range(4)}, name=name)(
                              parts, w, m, v, *prev, *extra)


def _allgather(xs, name, after=None):
    na = len(xs)
    first_out = na + (after is not None)

    def body(*refs):
        x_refs, o_refs = refs[:na], refs[first_out:first_out + na]
        send, recv, lsem = refs[first_out + na:]
        x, y, c = lax.axis_index("x"), lax.axis_index("y"), lax.axis_index("c")
        me, sib = (x, y, c), (x, y, 1 - c)
        chips = [(1 - x, y), (x, 1 - y), (1 - x, 1 - y)]
        slot = lambda p: 4 * p[0] + 2 * p[1] + p[2]

        def copy(a, k, block, to, src=None):
            dst = o_refs[a].at[slot(block)]
            return pltpu.make_async_remote_copy(src_ref=dst if src is None else src, dst_ref=dst, send_sem=send.at[a, k], recv_sem=recv.at[a, k],
                                                device_id=to, device_id_type=MESH)

        mine = [pltpu.make_async_copy(x_refs[a], o_refs[a].at[slot(me)], lsem.at[a]) for a in range(na)]
        for cp in mine:
            cp.start()
        first = []
        for a in range(na):
            first.append(copy(a, 0, me, sib, src=x_refs[a]))
            first += [copy(a, 1 + j, me, (*chip, c), src=x_refs[a]) for j, chip in enumerate(chips)]
        for cp in first:
            cp.start()
        passed = []
        for j, chip in enumerate(chips):
            for a in range(na):
                copy(a, 1 + j, (*chip, c), me).wait_recv()
                passed.append(copy(a, 4 + j, (*chip, c), sib))
                passed[-1].start()
        for a in range(na):
            copy(a, 0, sib, me).wait_recv()
            for j, chip in enumerate(chips):
                copy(a, 4 + j, (*chip, 1 - c), me).wait_recv()
        for cp in first + passed:
            cp.wait_send()
        for cp in mine:
            cp.wait()

    extra = [] if after is None else [after]
    return pl.pallas_call(body, in_specs=[ANY] * (na + len(extra)), out_specs=[ANY] * na,
                          out_shape=[_sds((N_DEV,) + t.shape, t.dtype) for t in xs],
                          scratch_shapes=[pltpu.SemaphoreType.DMA((na, 7)), pltpu.SemaphoreType.DMA((na, 7)), pltpu.SemaphoreType.DMA((na,))],
                          name=name)(*xs, *extra)


HBM = pl.BlockSpec(memory_space=pltpu.HBM)
SEM = pl.BlockSpec(memory_space=pltpu.SEMAPHORE)
EFFECT = pltpu.SideEffectType.DATAFLOW_SIDE_EFFECTING


SPLIT_RELATIONS = dict(gather=(1, 2, 4, 6),
                       forward=(2, 4, 6),
                       scatter=tuple(range(1, N_DEV)))


def _split_copies(mode, x_refs, land_refs, send, recv, own, landing):
    x, y, c = lax.axis_index("x"), lax.axis_index("y"), lax.axis_index("c")
    flip = lambda r: ((1 - x if r & 4 else x), (1 - y if r & 2 else y), (1 - c if r & 1 else c))
    slot = lambda p: 4 * p[0] + 2 * p[1] + p[2]
    me = slot((x, y, c))
    rel = SPLIT_RELATIONS[mode]
    local, remote = [], []
    for a in range(len(land_refs)):
        if mode != "forward":
            local.append(pltpu.make_async_copy(x_refs[a].at[me] if mode == "scatter" else x_refs[a], land_refs[a].at[me], own.at[a]))
        for j, r in enumerate(rel):
            if mode == "forward":
                to = flip(1)
                src = dst = land_refs[a].at[slot(flip(r ^ 1) if landing else flip(r))]
            else:
                to = flip(r)
                src = x_refs[a].at[slot(to)] if mode == "scatter" else x_refs[a]
                dst = land_refs[a].at[slot(to) if landing else me]
            remote.append(pltpu.make_async_remote_copy(src_ref=src, dst_ref=dst, send_sem=send.at[len(rel) * a + j],
                                                       recv_sem=recv.at[len(rel) * a + j], device_id=to, device_id_type=MESH))
    return local, remote


def _send_start(mode, xs, lands, after, name):
    if lands is None:
        lands = [lax.empty(t.shape if mode == "scatter" else (N_DEV,) + t.shape, t.dtype) for t in xs]
    nx, na, nr = len(xs), len(lands), len(SPLIT_RELATIONS[mode])
    nsem = 2 if mode == "forward" else 3

    def body(*refs):
        x_refs, land_refs = refs[:nx], refs[nx:nx + na]
        sems = refs[nx + na + 1:nx + na + 1 + nsem]
        local, remote = _split_copies(mode, x_refs, land_refs, sems[0], sems[1], sems[2] if nsem == 3 else None, False)
        for cp in remote + local:
            cp.start()
        refs[-1][...] = jnp.zeros_like(refs[-1])

    hbm = lambda t: pltpu.with_memory_space_constraint(t, pltpu.HBM)
    sem_shapes = [pltpu.SemaphoreType.DMA((nr * na,)), pltpu.SemaphoreType.DMA((nr * na,)), pltpu.SemaphoreType.DMA((na,))][:nsem]
    outs = pl.pallas_call(
        body, name=name,
        out_shape=(*sem_shapes, *[pltpu.HBM(t.shape, t.dtype) for t in list(xs) + list(lands)], _sds((8, 128), F32)),
        in_specs=[HBM] * (nx + na) + [ANY], out_specs=(*[SEM] * nsem, *[HBM] * (nx + na), pl.BlockSpec(memory_space=pltpu.VMEM)),
        input_output_aliases={i: nsem + i for i in range(nx + na)},
        compiler_params=pltpu.CompilerParams(has_side_effects=EFFECT))(*[hbm(t) for t in list(xs) + list(lands)], after)
    return outs[:nsem], list(outs[nsem:nsem + nx]), list(outs[nsem + nx:nsem + nx + na]), outs[-1]


def _send_wait(mode, started, after, name):
    sems, xs, lands, _ = started
    nx, na, nsem = len(xs), len(lands), len(sems)

    def body(*refs):
        s = refs[nx + na:nx + na + nsem]
        local, remote = _split_copies(mode, refs[:nx], refs[nx:nx + na], s[0], s[1], s[2] if nsem == 3 else None, True)
        for cp in remote:
            cp.wait_send()
            cp.wait_recv()
        for cp in local:
            cp.wait()

    outs = pl.pallas_call(body, name=name, out_shape=tuple(pltpu.HBM(t.shape, t.dtype) for t in xs + lands),
                          in_specs=[HBM] * (nx + na) + [SEM] * nsem + [ANY], out_specs=[HBM] * (nx + na),
                          input_output_aliases={i: i for i in range(nx + na)},
                          compiler_params=pltpu.CompilerParams(has_side_effects=EFFECT))(*xs, *lands, *sems, after)
    return list(outs[nx:])


def _local_step(x, meta, target, mix_weights_fn, ffn_weights_fn, grads_fn, sink, dec_f, dec_b, ret_norm, n_mix_pre, n_mix_post, n_ffn_pre,
                n_ffn_post):
    depth = n_mix_pre.shape[0]
    d = D_MODEL
    h = jnp.concatenate([jnp.zeros((PAD_FRONT, d), F32), meta, x], axis=0)
    n = h.shape[0]
    cos_a, sin_a, cos_r, sin_r, perm = _rope_tables(n)
    lg_all = jnp.stack([-jnp.exp(dec_f), -jnp.exp(dec_b)], axis=-1)
    saved = []
    u = _norm_fwd(h, n_mix_pre[0][None], None, BF16, "l0_norm_mix_pre")
    wi, wo, _ = mix_weights_fn(0, h)
    for l in range(depth):
        t = f"l{l}_"
        sink_b = jnp.broadcast_to(sink[l][:, None], (ATT_HEADS, 128))
        proj = _mm_nt(u, wi, BF16, WIDE_TILE, d, t + "proj")
        aq = _rope_att(proj, 0, ATT_HEADS, cos_a, sin_a, perm, t + "rope_aq")
        ak = _rope_att(proj, ATT_WIDTH // KV_WIDTH, ATT_KV_HEADS, cos_a, sin_a, perm, t + "rope_ak")
        mixed = _att_fwd(aq, ak, proj, sink_b, t + "att")
        proj3 = proj[None]
        rq = _rope_ret(proj3, COL_RQ // 2, cos_r, sin_r, t + "rope_rq")
        rk = _rope_ret(proj3, COL_RK // 2, cos_r, sin_r, t + "rope_rk")
        o_ret, states = _ret_fwd(rq, rk, proj, lg_all[l], t + "ret")
        mixed = _retgate_fwd(o_ret, proj, ret_norm[l][None], mixed, t + "retgate")
        wg, wu, wd, tok = ffn_weights_fn(l, mixed)
        mo, h_mid, u2 = _out_proj_norms(mixed, wo, h, n_mix_post[l][None], (n_ffn_pre[l] + tok)[None], t + "out_proj")
        gate, up, f = _swiglu_fwd(u2, wg, wu, t + "gate_up")
        dn = _mm_nn(f, wd, F32, 512, D_FF, t + "down", tm_cap=DEEP_K_TOKEN_TILE)
        saved.append(dict(h=h, u=u, proj=proj, aq=aq, ak=ak, rq=rq, rk=rk, o_ret=o_ret, states=states, mixed=mixed, mo=mo, h_mid=h_mid, u2=u2,
                          gate=gate, up=up, f=f, dn=dn, sink_b=sink_b, wi=wi, wo=wo, wg=wg, wu=wu, wd=wd))
        if l + 1 < depth:
            wi, wo, tok = mix_weights_fn(l + 1, dn)
            h, u = _norm_res_norm(dn, n_ffn_post[l][None], h_mid, (n_mix_pre[l + 1] + tok)[None], t + "norm_ffn_post")
        else:
            h = _norm_fwd(dn, n_ffn_post[l][None], h_mid, F32, t + "norm_ffn_post")

    dh, loss_part = _loss_head(h, target, "loss_head")
    gs = dict(sink=[None] * depth, dec_f=[None] * depth, dec_b=[None] * depth, ret_norm=[None] * depth, mix_pre=[None] * depth,
              mix_post=[None] * depth, ffn_pre=[None] * depth, ffn_post=[None] * depth)
    d_dn, gs["ffn_post"][depth - 1] = _norm_bwd(dh, saved[-1]["dn"], n_ffn_post[depth - 1][None], None, BF16, f"l{depth - 1}_b_norm_ffn_post")
    for l in reversed(range(depth)):
        t = f"l{l}_b_"
        sv = saved[l]
        proj = sv["proj"]
        gw = {}
        d_gate, d_up = _swiglu_bwd(d_dn, sv["wd"], sv["gate"], sv["up"], t + "d_gate_up")
        gw["wd"] = _mm_tn(sv["f"], d_dn, WIDE_TILE, 1024, t + "dw_down")
        du2 = _mm_nn(d_gate, sv["wg"], F32, 512, D_FF, t + "du2_gate", tm_cap=DEEP_K_TOKEN_TILE)
        du2 = _mm_nn(d_up, sv["wu"], F32, 512, D_FF, t + "du2_up", acc=du2, tm_cap=DEEP_K_TOKEN_TILE)
        gw["wg"] = _mm_tn(d_gate, sv["u2"], WIDE_TILE, 1024, t + "dw_gate")
        gw["wu"] = _mm_tn(d_up, sv["u2"], WIDE_TILE, 1024, t + "dw_up")
        tok_b = grads_fn(l, "ffn", gw, du2)
        dh, d_mo, gs["ffn_pre"][l], gs["mix_post"][l] = _norm_bwd_pair(du2, sv["h_mid"], (n_ffn_pre[l] + tok_b)[None], dh, sv["mo"],
                                                                         n_mix_post[l][None], t + "norm_ffn_pre")
        d_mixed = _mm_nt(d_mo, sv["wo"], BF16, 1024, d, t + "d_mixed")
        gw["wo"] = _mm_tn(sv["mixed"], d_mo, 1024, 1024, t + "dw_out")
        d_o, dproj, gs["ret_norm"][l] = _retgate_bwd(d_mixed, sv["o_ret"], proj, ret_norm[l][None], t + "retgate")
        dq_r, dk_r, dv_r, dlg = _ret_bwd(sv["rq"], sv["rk"], proj, lg_all[l], d_o, sv["states"], t + "ret")
        draw = dlg[:, :, 0, 0] * lg_all[l]
        gs["dec_f"][l], gs["dec_b"][l] = draw[:, 0], draw[:, 1]
        dproj = _rope_ret(dq_r, 0, cos_r, -sin_r, t + "rope_rq", into=dproj, out_col=COL_RQ // 2)
        dproj = _rope_ret(dk_r, 0, cos_r, -sin_r, t + "rope_rk", into=dproj, out_col=COL_RK // 2)
        dproj = _rope_ret(dv_r, 0, None, None, t + "sum_rv", into=dproj, out_col=COL_RV // 2)
        dq_a, dk_a, dv_a, dsink = _att_bwd(sv["aq"], sv["ak"], proj, sv["sink_b"], d_mixed, t + "att")
        gs["sink"][l] = dsink[:, 0]
        dproj = _rope_att(dq_a, 0, ATT_HEADS, cos_a, -sin_a, perm, t + "rope_aq", into=dproj, out_col=0)
        dproj = _rope_att(dk_a, 0, ATT_KV_HEADS, cos_a, -sin_a, perm, t + "rope_ak", plain=dv_a, into=dproj, out_col=ATT_WIDTH // (2 * KV_WIDTH))
        gw["wi"] = _mm_tn(dproj, sv["u"], WIDE_TILE, 1024, t + "dw_in")
        tok_b = grads_fn(l, "mix", gw, dproj)
        du = _mm_nn(dproj, sv["wi"], F32, 512, IN_COLS, t + "du", tm_cap=DEEP_K_TOKEN_TILE)
        if l > 0:
            dh, d_dn, gs["mix_pre"][l], gs["ffn_post"][l - 1] = _norm_bwd_pair(du, sv["h"], (n_mix_pre[l] + tok_b)[None], dh, saved[l - 1]["dn"],
                                                                              n_ffn_post[l - 1][None], t + "norm_mix_pre")
        else:
            dh, gs["mix_pre"][l] = _norm_bwd(du, sv["h"], (n_mix_pre[l] + tok_b)[None], dh, F32, t + "norm_mix_pre")
    return loss_part[0, 0], dh, gs


def _pack_small(mix_pre, mix_post, ffn_pre, ffn_post, ret_norm, sink, dec_f, dec_b, loss, meta):
    d = D_MODEL

    def tile(a, rows=8):
        a = jnp.reshape(a, (-1, a.shape[-1])) if a.ndim else jnp.reshape(a, (1, 1))
        return jnp.pad(a, ((0, rows - a.shape[0]), (0, d - a.shape[1])))

    return jnp.concatenate([tile(mix_pre), tile(mix_post), tile(ffn_pre), tile(ffn_post), tile(ret_norm.reshape(-1, d)), tile(sink), tile(dec_f),
                            tile(dec_b), tile(loss), tile(meta, SMALL_ROWS - ROW_META)], axis=0)


def _unpack_small(p, depth):
    rows = lambda r0, cols: p[r0:r0 + depth, :cols]
    return dict(mix_pre=rows(ROW_MIX_PRE, D_MODEL), mix_post=rows(ROW_MIX_POST, D_MODEL), ffn_pre=rows(ROW_FFN_PRE, D_MODEL),
                ffn_post=rows(ROW_FFN_POST, D_MODEL), ret_norm=p[ROW_RET_NORM:ROW_RET_NORM + depth * RET_WIDTH // D_MODEL].reshape(depth, RET_WIDTH),
                sink=rows(ROW_SINK, ATT_HEADS), dec_f=rows(ROW_DEC_F, RET_HEADS), dec_b=rows(ROW_DEC_B, RET_HEADS), loss=p[ROW_LOSS, 0])


def kernel(x, meta_tokens, w_in, w_out, attn_sink, ret_decay_fwd, ret_decay_bwd, ret_norm, norm_mix_pre, norm_mix_post, w_gate, w_up, w_down, norm_ffn_pre, norm_ffn_post, loss_target, m_meta_tokens, m_w_in, m_w_out, m_attn_sink, m_ret_decay_fwd, m_ret_decay_bwd, m_ret_norm, m_norm_mix_pre, m_norm_mix_post, m_w_gate, m_w_up, m_w_down, m_norm_ffn_pre, m_norm_ffn_post, v_meta_tokens, v_w_in, v_w_out, v_attn_sink, v_ret_decay_fwd, v_ret_decay_bwd, v_ret_norm, v_norm_mix_pre, v_norm_mix_post, v_w_gate, v_w_up, v_w_down, v_norm_ffn_pre, v_norm_ffn_post):
    depth, d = w_in.shape[0], D_MODEL
    me = 4 * lax.axis_index("x") + 2 * lax.axis_index("y") + lax.axis_index("c")
    zero = jnp.zeros((), F32)

    meta_g, = _allgather([meta_tokens], "gather_meta")
    meta = meta_g.transpose(1, 0, 2).reshape(N_META, d)

    def shards(k):
        l = k // 2
        if k % 2 == 0:
            return [w_in[l].T.astype(BF16), w_out[l].astype(BF16)]
        return [w_gate[l].T.astype(BF16), w_up[l].T.astype(BF16), w_down[l].astype(BF16)]

    gathers, ahead = {}, 2
    for k in range(min(ahead + 1, 2 * depth)):
        gathers[k] = _send_start("gather", shards(k), None, gathers[k - 1][3] if k else meta_g, f"gather_start_g{k}")

    passing = {}

    def pass_on(k, after):
        lands = _send_wait("gather", gathers.pop(k), after, f"gather_wait_g{k}")
        passing[k] = _send_start("forward", [], lands, after, f"forward_start_g{k}")

    def take(k, h):
        after = h
        if k >= 1 and k + ahead < 2 * depth:
            gathers[k + ahead] = _send_start("gather", shards(k + ahead), None, h, f"gather_start_g{k + ahead}")
            after = gathers[k + ahead][3]
        elif k == 0:
            after = gathers[max(gathers)][3]
        if k not in passing:
            pass_on(k, after)
        if k >= 2 and k + 1 < 2 * depth:
            pass_on(k + 1, after)
        last = passing[k + 1][3] if k + 1 in passing else passing[k][3]
        return _send_wait("forward", passing.pop(k), last, f"forward_wait_g{k}")

    def mix_weights_fn(l, h):
        wi_t, wo = take(2 * l, h)
        return wi_t.reshape(IN_COLS, d), wo.reshape(d, d), zero

    def ffn_weights_fn(l, h):
        wg_t, wu_t, wd = take(2 * l + 1, h)
        return wg_t.reshape(D_FF, d), wu_t.reshape(D_FF, d), wd.reshape(D_FF, d), zero

    exchanges, adam, order = {}, {}, []
    tr = lambda *ts: tuple(jnp.swapaxes(t, 1, 2) for t in ts)
    big = dict(wi=tr(w_in, m_w_in, v_w_in), wg=tr(w_gate, m_w_gate, v_w_gate), wu=tr(w_up, m_w_up, v_w_up), wd=(w_down, m_w_down, v_w_down),
               wo=(w_out, m_w_out, v_w_out))
    kinds = dict(ffn=("wg", "wu", "wd"), mix=("wi", "wo"))

    arrivals = []

    def finish(key, after):
        l, part = key
        arrivals.append((key, _send_wait("scatter", exchanges.pop(key), after, f"exchange_wait_{part}_l{l}")))

    def update():
        last_start = exchanges[order[-1]][3] if order[-1] in exchanges else None
        while arrivals:
            (l, part), arrived = arrivals.pop(0)
            for kind, parts in zip(kinds[part], arrived):
                adam[kind] = _adamw(parts, *big[kind], f"adamw_{kind}_l{l}", layer=l, prev=adam.get(kind),
                                    after=None if kind in adam else last_start)

    def grads_fn(l, part, gw, after):
        packed = [gw[kind].reshape(N_DEV, -1, d) for kind in kinds[part]]
        exchanges[(l, part)] = _send_start("scatter", packed, None, after, f"exchange_start_{part}_l{l}")
        order.append((l, part))
        token = exchanges[(l, part)][3]
        if len(order) > 2:
            finish(order[-3], token)
        return token[0, 0]

    loss_part, dh, gs = _local_step(x[0], meta, loss_target[0], mix_weights_fn, ffn_weights_fn, grads_fn, attn_sink, ret_decay_fwd, ret_decay_bwd, ret_norm,
                                    norm_mix_pre, norm_mix_post, norm_ffn_pre, norm_ffn_post)
    grad_x = dh[BLOCK:][None]

    st = lambda xs: jnp.stack([t.reshape(-1) if t.ndim == 1 else t[0] for t in xs])
    small = _pack_small(st(gs["mix_pre"]), st(gs["mix_post"]), st(gs["ffn_pre"]), st(gs["ffn_post"]), st(gs["ret_norm"]), st(gs["sink"]),
                        st(gs["dec_f"]), st(gs["dec_b"]), loss_part, dh[PAD_FRONT:BLOCK])
    update()
    small_g, = _allgather([small], "gather_small", after=adam["wo"][0])
    for key in order[-2:]:
        finish(key, small_g)
    update()
    o_wi, o_wo, o_wg, o_wu, o_wd = tr(*adam["wi"]), adam["wo"], tr(*adam["wg"]), tr(*adam["wu"]), adam["wd"]
    zmeta = jnp.zeros((N_META, d), F32)
    packs = [_pack_small(a[0], a[1], a[2], a[3], a[4], a[5], a[6], a[7], zero, zmeta) for a in (
        (norm_mix_pre, norm_mix_post, norm_ffn_pre, norm_ffn_post, ret_norm, attn_sink, ret_decay_fwd, ret_decay_bwd),
        (m_norm_mix_pre, m_norm_mix_post, m_norm_ffn_pre, m_norm_ffn_post, m_ret_norm, m_attn_sink, m_ret_decay_fwd, m_ret_decay_bwd),
        (v_norm_mix_pre, v_norm_mix_post, v_norm_ffn_pre, v_norm_ffn_post, v_ret_norm, v_attn_sink, v_ret_decay_fwd, v_ret_decay_bwd))]
    o_small = [_unpack_small(o, depth) for o in _adamw(small_g, packs[0], packs[1], packs[2], "adamw_small")]
    meta_parts = lax.dynamic_slice(small_g, (0, ROW_META, me * (d // N_DEV)), (N_DEV, N_META, d // N_DEV))
    o_meta = _adamw(meta_parts, meta_tokens, m_meta_tokens, v_meta_tokens, "adamw_meta")

    outs = []
    for i in range(4):
        s = o_small[i]
        outs += [o_meta[i], o_wi[i], o_wo[i], s["sink"], s["dec_f"], s["dec_b"], s["ret_norm"], s["mix_pre"], s["mix_post"], o_wg[i], o_wu[i],
                 o_wd[i], s["ffn_pre"], s["ffn_post"]]
    return (o_small[0]["loss"], grad_x, *outs)
```

```python
import jax
import jax.numpy as jnp
import numpy as np
from jax import lax
from jax.experimental import pallas as pl
from jax.experimental.pallas import tpu as pltpu

F32, BF16 = jnp.float32, jnp.bfloat16

D_MODEL = 2048
N_META = 16
BLOCK = 128
WINDOW = 128
PAD_FRONT = BLOCK - N_META
ATT_HEAD_DIM = 128
ATT_WIDTH = D_MODEL // 2
ATT_HEADS = ATT_WIDTH // ATT_HEAD_DIM
ATT_KV_HEADS = 2
ATT_GROUP = ATT_HEADS // ATT_KV_HEADS
KV_WIDTH = ATT_KV_HEADS * ATT_HEAD_DIM
ROT_DIM = ATT_HEAD_DIM // 4
ROPE_THETA = 500000.0
RET_WIDTH = D_MODEL - ATT_WIDTH
RET_HEAD_DIM = 256
RET_HEADS = RET_WIDTH // RET_HEAD_DIM
RET_THETA = 10000.0
D_FF = 5632
IN_COLS = ATT_WIDTH + 2 * KV_WIDTH + 4 * RET_WIDTH
N_DEV = 8
EPS = 1e-6
NEG = -1e30
RET_K_SCALE = RET_HEAD_DIM ** -0.5
ATT_SCALE = ATT_HEAD_DIM ** -0.5

COL_AV256 = (ATT_WIDTH + KV_WIDTH) // 256
COL_RQ = (ATT_WIDTH + 2 * KV_WIDTH) // RET_HEAD_DIM
COL_RK = COL_RQ + RET_HEADS
COL_RV = COL_RK + RET_HEADS
COL_RG = COL_RV + RET_HEADS

ADAM_LR, ADAM_B1, ADAM_B2, ADAM_EPS, ADAM_WD, ADAM_STEP = 0.001, 0.9, 0.999, 1e-08, 0.01, 10

ROW_MIX_PRE, ROW_MIX_POST, ROW_FFN_PRE, ROW_FFN_POST, ROW_RET_NORM, ROW_SINK, ROW_DEC_F, ROW_DEC_B, ROW_LOSS, ROW_META, SMALL_ROWS = (
    0, 8, 16, 24, 32, 40, 48, 56, 64, 72, 96)
ADAMW_TILE_ELEMS = 128 * 1024

MESH = pl.DeviceIdType.MESH
ANY = pl.BlockSpec(memory_space=pl.ANY)


def _row_tile(n, cap):
    for t in range(cap - cap % 16, 0, -16):
        if n % t == 0:
            return t
    raise ValueError(n)


def _sds(shape, dtype):
    return jax.ShapeDtypeStruct(shape, dtype)


def _silu(x):
    return x * jax.nn.sigmoid(x)


def _dsilu(x):
    s = jax.nn.sigmoid(x)
    return s * (1.0 + x * (1.0 - s))


def _norm_fwd(x, g, res, out_dtype, name):
    n, d = x.shape
    tr = _row_tile(n, 384)

    def body(*refs):
        if res is None:
            x_ref, g_ref, o_ref = refs
        else:
            x_ref, g_ref, r_ref, o_ref = refs
        xv = x_ref[...]
        r = lax.rsqrt(jnp.mean(xv * xv, axis=-1, keepdims=True) + EPS)
        y = xv * r * g_ref[...]
        if res is not None:
            y = y + r_ref[...]
        o_ref[...] = y.astype(o_ref.dtype)

    row = pl.BlockSpec((tr, d), lambda i: (i, 0))
    ins = [row, pl.BlockSpec((1, d), lambda i: (0, 0))] + ([row] if res is not None else [])
    args = (x, g) + ((res,) if res is not None else ())
    return pl.pallas_call(body, grid=(n // tr,), in_specs=ins, out_specs=row, out_shape=_sds((n, d), out_dtype), name=name)(*args)


def _norm_res_norm(x, g, res, g_next, name):
    n, d = x.shape
    tr = _row_tile(n, 384)

    def body(x_ref, g_ref, r_ref, gn_ref, h_ref, u_ref):
        xv = x_ref[...]
        hv = r_ref[...] + xv * lax.rsqrt(jnp.mean(xv * xv, axis=-1, keepdims=True) + EPS) * g_ref[...]
        h_ref[...] = hv
        u_ref[...] = (hv * lax.rsqrt(jnp.mean(hv * hv, axis=-1, keepdims=True) + EPS) * gn_ref[...]).astype(u_ref.dtype)

    row = pl.BlockSpec((tr, d), lambda i: (i, 0))
    vec = pl.BlockSpec((1, d), lambda i: (0, 0))
    return pl.pallas_call(body, grid=(n // tr,), in_specs=[row, vec, row, vec], out_specs=(row, row),
                          out_shape=(_sds((n, d), F32), _sds((n, d), BF16)), name=name)(x, g, res, g_next)


OUT_PROJ_TOKEN_TILE = 192


def _out_proj_norms(x, w, res, g, g_next, name):
    n, k = x.shape
    d = w.shape[1]
    tm = _row_tile(n, OUT_PROJ_TOKEN_TILE)

    def body(x_ref, w_ref, r_ref, g_ref, gn_ref, mo_ref, h_ref, u_ref):
        mo = jnp.dot(x_ref[...], w_ref[...], preferred_element_type=F32)
        mo_ref[...] = mo
        hv = r_ref[...] + mo * lax.rsqrt(jnp.mean(mo * mo, axis=-1, keepdims=True) + EPS) * g_ref[...]
        h_ref[...] = hv
        u_ref[...] = (hv * lax.rsqrt(jnp.mean(hv * hv, axis=-1, keepdims=True) + EPS) * gn_ref[...]).astype(u_ref.dtype)

    row = pl.BlockSpec((tm, d), lambda i: (i, 0))
    vec = pl.BlockSpec((1, d), lambda i: (0, 0))
    return pl.pallas_call(body, grid=(n // tm,), in_specs=[pl.BlockSpec((tm, k), lambda i: (i, 0)), pl.BlockSpec((k, d), lambda i: (0, 0)), row, vec, vec],
                          out_specs=(row, row, row), out_shape=(_sds((n, d), F32), _sds((n, d), F32), _sds((n, d), BF16)), name=name)(
                              x, w, res, g, g_next)


def _norm_bwd(dy, x, g, res, out_dtype, name):
    n, d = x.shape
    tr = _row_tile(n, 384)

    def body(*refs):
        if res is None:
            dy_ref, x_ref, g_ref, dx_ref, dg_ref = refs
        else:
            dy_ref, x_ref, g_ref, r_ref, dx_ref, dg_ref = refs
        i = pl.program_id(0)
        xv = x_ref[...]
        r = lax.rsqrt(jnp.mean(xv * xv, axis=-1, keepdims=True) + EPS)
        xhat = xv * r
        dyf = dy_ref[...].astype(F32)
        gdy = dyf * g_ref[...]
        dx = r * (gdy - xhat * jnp.mean(gdy * xhat, axis=-1, keepdims=True))
        if res is not None:
            dx = dx + r_ref[...]
        dx_ref[...] = dx.astype(dx_ref.dtype)

        @pl.when(i == 0)
        def _():
            dg_ref[...] = jnp.zeros_like(dg_ref)

        dg_ref[...] += jnp.sum(dyf * xhat, axis=0, keepdims=True)

    row = pl.BlockSpec((tr, d), lambda i: (i, 0))
    vec = pl.BlockSpec((1, d), lambda i: (0, 0))
    ins = [row, row, vec] + ([row] if res is not None else [])
    args = (dy, x, g) + ((res,) if res is not None else ())
    return pl.pallas_call(body, grid=(n // tr,), in_specs=ins, out_specs=(row, vec),
                          out_shape=(_sds((n, d), out_dtype), _sds((1, d), F32)), name=name,
                          compiler_params=pltpu.CompilerParams(dimension_semantics=("arbitrary",)))(*args)


def _norm_bwd_pair(dy, x1, g1, res, x2, g2, name):
    n, d = x1.shape
    tr = _row_tile(n, 192)

    def rms_bwd(dyf, xv, g):
        r = lax.rsqrt(jnp.mean(xv * xv, axis=-1, keepdims=True) + EPS)
        xhat = xv * r
        gdy = dyf * g
        return r * (gdy - xhat * jnp.mean(gdy * xhat, axis=-1, keepdims=True)), jnp.sum(dyf * xhat, axis=0, keepdims=True)

    def body(dy_ref, x1_ref, g1_ref, r_ref, x2_ref, g2_ref, dh_ref, d2_ref, dg1_ref, dg2_ref):
        @pl.when(pl.program_id(0) == 0)
        def _():
            dg1_ref[...] = jnp.zeros_like(dg1_ref)
            dg2_ref[...] = jnp.zeros_like(dg2_ref)

        dx1, s1 = rms_bwd(dy_ref[...].astype(F32), x1_ref[...], g1_ref[...])
        dh = dx1 + r_ref[...]
        dh_ref[...] = dh
        dx2, s2 = rms_bwd(dh, x2_ref[...], g2_ref[...])
        d2_ref[...] = dx2.astype(d2_ref.dtype)
        dg1_ref[...] += s1
        dg2_ref[...] += s2

    row = pl.BlockSpec((tr, d), lambda i: (i, 0))
    vec = pl.BlockSpec((1, d), lambda i: (0, 0))
    return pl.pallas_call(body, grid=(n // tr,), in_specs=[row, row, vec, row, row, vec], out_specs=(row, row, vec, vec),
                          out_shape=(_sds((n, d), F32), _sds((n, d), BF16), _sds((1, d), F32), _sds((1, d), F32)), name=name,
                          compiler_params=pltpu.CompilerParams(dimension_semantics=("arbitrary",)))(dy, x1, g1, res, x2, g2)


def _mm(a, b, *, ta, tb, grid, a_blk, a_map, b_blk, b_map, o_blk, o_map, o_shape, o_dtype, name, acc=None):
    nk = grid[2]
    dims = (((0,) if ta else (1,), (1,) if tb else (0,)), ((), ()))

    def body(*refs):
        if acc is None:
            a_ref, b_ref, o_ref = refs[:3]
            c_ref = None
        else:
            a_ref, b_ref, c_ref, o_ref = refs[:4]
        part = lax.dot_general(a_ref[...], b_ref[...], dims, preferred_element_type=F32)
        if nk == 1:
            if c_ref is not None:
                part = part + c_ref[...].astype(F32)
            o_ref[...] = part.astype(o_ref.dtype)
            return
        acc_ref = refs[-1]
        k = pl.program_id(2)

        @pl.when(k == 0)
        def _():
            acc_ref[...] = jnp.zeros_like(acc_ref) if c_ref is None else c_ref[...].astype(F32)

        acc_ref[...] += part

        @pl.when(k == nk - 1)
        def _():
            o_ref[...] = acc_ref[...].astype(o_ref.dtype)

    ins = [pl.BlockSpec(a_blk, a_map), pl.BlockSpec(b_blk, b_map)]
    args = [a, b]
    if acc is not None:
        ins.append(pl.BlockSpec(o_blk, o_map))
        args.append(acc)
    return pl.pallas_call(body, grid=grid, in_specs=ins, out_specs=pl.BlockSpec(o_blk, o_map), out_shape=_sds(o_shape, o_dtype),
                          scratch_shapes=[pltpu.VMEM(o_blk, F32)] if nk > 1 else [], name=name,
                          compiler_params=pltpu.CompilerParams(dimension_semantics=("parallel", "parallel", "arbitrary")))(*args)


TOKEN_TILE = 1056
WIDE_TILE = 1408
DEEP_K_TOKEN_TILE = 528


def _mm_nn(x, w, o_dtype, tn, tk, name, acc=None, tm_cap=TOKEN_TILE):
    n, k = x.shape
    tm = _row_tile(n, tm_cap)
    return _mm(x, w, ta=False, tb=False, grid=(n // tm, w.shape[1] // tn, k // tk), a_blk=(tm, tk), a_map=lambda i, j, kk: (i, kk),
               b_blk=(tk, tn), b_map=lambda i, j, kk: (kk, j), o_blk=(tm, tn), o_map=lambda i, j, kk: (i, j),
               o_shape=(n, w.shape[1]), o_dtype=o_dtype, name=name, acc=acc)


def _mm_nt(dy, w, o_dtype, tn, tk, name, acc=None):
    n, k = dy.shape
    tm = _row_tile(n, TOKEN_TILE)
    return _mm(dy, w, ta=False, tb=True, grid=(n // tm, w.shape[0] // tn, k // tk), a_blk=(tm, tk), a_map=lambda i, j, kk: (i, kk),
               b_blk=(tn, tk), b_map=lambda i, j, kk: (j, kk), o_blk=(tm, tn), o_map=lambda i, j, kk: (i, j),
               o_shape=(n, w.shape[0]), o_dtype=o_dtype, name=name, acc=acc)


def _mm_tn(x, dy, tm, tn, name):
    n, m = x.shape
    tk = _row_tile(n, 2 * TOKEN_TILE)
    return _mm(x, dy, ta=True, tb=False, grid=(m // tm, dy.shape[1] // tn, n // tk), a_blk=(tk, tm), a_map=lambda i, j, kk: (kk, i),
               b_blk=(tk, tn), b_map=lambda i, j, kk: (kk, j), o_blk=(tm, tn), o_map=lambda i, j, kk: (i, j),
               o_shape=(m, dy.shape[1]), o_dtype=BF16, name=name)


def _rope_tables(n):
    pos = (jnp.arange(n) - PAD_FRONT).astype(F32)
    half = ROT_DIM // 2
    ang = pos[:, None] * (ROPE_THETA ** (-jnp.arange(half, dtype=F32) / half))[None, :]
    c, s = jnp.cos(ang), jnp.sin(ang)
    rest = ATT_HEAD_DIM - ROT_DIM
    cos_a = jnp.concatenate([c, c, jnp.ones((n, rest), F32)], axis=1)
    sin_a = jnp.concatenate([-s, s, jnp.zeros((n, rest), F32)], axis=1)
    half = RET_HEAD_DIM // 2
    ang = pos[:, None] * (RET_THETA ** (-jnp.arange(half, dtype=F32) / half))[None, :]
    c, s = jnp.cos(ang), jnp.sin(ang)
    perm = np.zeros((ATT_HEAD_DIM, ATT_HEAD_DIM), np.float32)
    for i in range(ROT_DIM):
        perm[(i + ROT_DIM // 2) % ROT_DIM, i] = 1.0
    return cos_a, sin_a, jnp.concatenate([c, c], axis=1), jnp.concatenate([-s, s], axis=1), jnp.asarray(perm, BF16)


def _into(buf, own_shape):
    if buf is None:
        return _sds(own_shape, BF16), [], [], lambda n_inputs: {}
    return _sds(buf.shape, buf.dtype), [ANY], [buf], lambda n_inputs: {n_inputs: 0}


def _rope_att(x, col0, heads, cos, sin, perm, name, plain=None, into=None, out_col=0):
    n = x.shape[0]
    tr = _row_tile(n, 1056)
    hd = ATT_HEAD_DIM
    w2 = 0 if plain is None else plain.shape[1]
    width = heads * hd + w2

    def body(*refs):
        x_ref, c_ref, s_ref, p_ref, o_ref = refs[0], refs[1], refs[2], refs[3], refs[-1]
        for h in range(heads):
            cs = slice(h * hd, (h + 1) * hd)
            xb = x_ref[:, cs].astype(BF16)
            sw = jnp.dot(xb, p_ref[...], preferred_element_type=F32)
            o_ref[:, cs] = (xb.astype(F32) * c_ref[...] + sw * s_ref[...]).astype(o_ref.dtype)
        if plain is not None:
            o_ref[:, heads * hd:] = refs[4][...].astype(o_ref.dtype)

    tab = pl.BlockSpec((tr, hd), lambda i: (i, 0))
    ins = [pl.BlockSpec((tr, heads * hd), lambda i: (i, col0)), tab, tab, pl.BlockSpec((hd, hd), lambda i: (0, 0))]
    args = [x, cos, sin, perm]
    if plain is not None:
        ins.append(pl.BlockSpec((tr, w2), lambda i: (i, 0)))
        args.append(plain)
    shape, extra_specs, extra_args, alias = _into(into, (n, width))
    return pl.pallas_call(body, grid=(n // tr,), in_specs=ins + extra_specs, out_specs=pl.BlockSpec((tr, width), lambda i: (i, out_col)),
                          out_shape=shape, input_output_aliases=alias(len(ins)), name=name)(*args, *extra_args)


def _rope_ret(x, col0, cos, sin, name, into=None, out_col=0):
    p, n, _ = x.shape
    tr = _row_tile(n, 1056)
    hd = RET_HEAD_DIM

    def body(*refs):
        x_ref, o_ref = refs[0], refs[-1]
        for h in range(2):
            cs = slice(h * hd, (h + 1) * hd)
            xv = x_ref[0, :, cs].astype(F32)
            for q in range(1, p):
                xv = xv + x_ref[q, :, cs].astype(F32)
            if cos is not None:
                sw = jnp.concatenate([xv[:, hd // 2:], xv[:, :hd // 2]], axis=1)
                xv = xv * refs[1][...] + sw * refs[2][...]
            o_ref[:, cs] = xv.astype(o_ref.dtype)

    tab = pl.BlockSpec((tr, hd), lambda i, j: (i, 0))
    ins = [pl.BlockSpec((p, tr, 2 * hd), lambda i, j: (0, i, col0 + j))] + ([tab, tab] if cos is not None else [])
    args = (x,) + ((cos, sin) if cos is not None else ())
    shape, extra_specs, extra_args, alias = _into(into, (n, RET_WIDTH))
    return pl.pallas_call(body, grid=(n // tr, RET_HEADS // 2), in_specs=ins + extra_specs,
                          out_specs=pl.BlockSpec((tr, 2 * hd), lambda i, j: (i, out_col + j)), out_shape=shape,
                          input_output_aliases=alias(len(ins)), name=name)(*args, *extra_args)


def _att_mask(nblk, n_tot):
    row = lax.broadcasted_iota(jnp.int32, (BLOCK, 4 * BLOCK), 0)
    col = lax.broadcasted_iota(jnp.int32, (BLOCK, 4 * BLOCK), 1)
    qi = nblk * BLOCK + row
    seg = col // BLOCK
    cj = col % BLOCK
    kj = (nblk - 1 + seg) * BLOCK + cj
    band = (jnp.abs(qi - kj) <= WINDOW) & (kj >= PAD_FRONT) & (kj < n_tot) & (seg < 3)
    meta = (seg == 3) & (cj >= PAD_FRONT) & (jnp.abs(qi - cj) > WINDOW)
    return band | meta


def _att_specs(nb, v_col):
    kv = lambda f, cb: pl.BlockSpec((BLOCK, KV_WIDTH), lambda n: (f(n), cb))
    prev, own, nxt, first = (lambda n: jnp.maximum(n - 1, 0)), (lambda n: n), (lambda n: jnp.minimum(n + 1, nb - 1)), (lambda n: 0)
    return [kv(f, 0) for f in (prev, own, nxt, first)] + [kv(f, v_col) for f in (prev, own, nxt, first)]


def _att_probs(s, ok, snk):
    s = jnp.where(ok, s, NEG)
    m = jnp.maximum(jnp.max(s, axis=-1, keepdims=True), snk)
    p = jnp.exp(s - m)
    ps = jnp.exp(snk - m)
    inv = 1.0 / (jnp.sum(p, axis=-1, keepdims=True) + ps)
    return p * inv, ps * inv


def _att_fwd(q, k, proj, sink_b, name):
    n = q.shape[0]
    nb = n // BLOCK
    hd = ATT_HEAD_DIM

    def body(q_ref, kp, ko, kn, km, vp, vo, vn, vm, sink_ref, o_ref):
        nblk = pl.program_id(0)
        ok = _att_mask(nblk, n)
        keep = (nblk * BLOCK + lax.broadcasted_iota(jnp.int32, (BLOCK, 1), 0)) >= PAD_FRONT
        for kh in range(ATT_KV_HEADS):
            cs = slice(kh * hd, (kh + 1) * hd)
            kk = jnp.concatenate([r[:, cs] for r in (kp, ko, kn, km)], axis=0)
            vv = jnp.concatenate([r[:, cs] for r in (vp, vo, vn, vm)], axis=0)
            heads = [kh * ATT_GROUP + g for g in range(ATT_GROUP)]
            q4 = jnp.concatenate([q_ref[:, h * hd:(h + 1) * hd] for h in heads], axis=0)
            s = lax.dot_general(q4, kk, (((1,), (1,)), ((), ())), preferred_element_type=F32) * ATT_SCALE
            ps = []
            for g, h in enumerate(heads):
                p, _ = _att_probs(s[g * BLOCK:(g + 1) * BLOCK], ok, sink_ref[h:h + 1, 0:1])
                ps.append(p)
            o = jnp.dot(jnp.concatenate(ps, axis=0).astype(BF16), vv, preferred_element_type=F32)
            for g, h in enumerate(heads):
                o_ref[:, h * hd:(h + 1) * hd] = jnp.where(keep, o[g * BLOCK:(g + 1) * BLOCK], 0.0).astype(o_ref.dtype)

    qspec = pl.BlockSpec((BLOCK, ATT_WIDTH), lambda i: (i, 0))
    return pl.pallas_call(body, grid=(nb,), in_specs=[qspec] + _att_specs(nb, COL_AV256) + [pl.BlockSpec((ATT_HEADS, 128), lambda i: (0, 0))],
                          out_specs=qspec, out_shape=_sds((n, D_MODEL), BF16), name=name)(q, k, k, k, k, proj, proj, proj, proj, sink_b)


def _att_bwd(q, k, proj, sink_b, dmixed, name):
    n = q.shape[0]
    nb = n // BLOCK
    hd = ATT_HEAD_DIM

    def body(q_ref, kp, ko, kn, km, vp, vo, vn, vm, sink_ref, do_ref, dq_ref, dk_ref, dv_ref, dsink_ref):
        nblk = pl.program_id(0)

        @pl.when(nblk == 0)
        def _():
            dk_ref[...] = jnp.zeros_like(dk_ref)
            dv_ref[...] = jnp.zeros_like(dv_ref)
            dsink_ref[...] = jnp.zeros_like(dsink_ref)

        ok = _att_mask(nblk, n)
        rows = [jnp.maximum(nblk - 1, 0), nblk, jnp.minimum(nblk + 1, nb - 1), 0]
        for kh in range(ATT_KV_HEADS):
            cs = slice(kh * hd, (kh + 1) * hd)
            kk = jnp.concatenate([r[:, cs] for r in (kp, ko, kn, km)], axis=0)
            vv = jnp.concatenate([r[:, cs] for r in (vp, vo, vn, vm)], axis=0)
            heads = [kh * ATT_GROUP + g for g in range(ATT_GROUP)]
            q4 = jnp.concatenate([q_ref[:, h * hd:(h + 1) * hd] for h in heads], axis=0)
            do4 = jnp.concatenate([do_ref[:, h * hd:(h + 1) * hd] for h in heads], axis=0)
            s = lax.dot_general(q4, kk, (((1,), (1,)), ((), ())), preferred_element_type=F32) * ATT_SCALE
            dp = lax.dot_general(do4, vv, (((1,), (1,)), ((), ())), preferred_element_type=F32)
            ps, dss = [], []
            for g, h in enumerate(heads):
                p, psink = _att_probs(s[g * BLOCK:(g + 1) * BLOCK], ok, sink_ref[h:h + 1, 0:1])
                dpg = dp[g * BLOCK:(g + 1) * BLOCK]
                delta = jnp.sum(p * dpg, axis=-1, keepdims=True)
                ps.append(p)
                dss.append(p * (dpg - delta) * ATT_SCALE)
                dsink_ref[h:h + 1, :] = dsink_ref[h:h + 1, :] - jnp.sum(psink * delta, axis=0, keepdims=True)
            ds = jnp.concatenate(dss, axis=0).astype(BF16)
            pb = jnp.concatenate(ps, axis=0).astype(BF16)
            dq = jnp.dot(ds, kk, preferred_element_type=F32)
            for g, h in enumerate(heads):
                dq_ref[:, h * hd:(h + 1) * hd] = dq[g * BLOCK:(g + 1) * BLOCK].astype(dq_ref.dtype)
            dk = lax.dot_general(ds, q4, (((0,), (0,)), ((), ())), preferred_element_type=F32)
            dv = lax.dot_general(pb, do4, (((0,), (0,)), ((), ())), preferred_element_type=F32)
            for seg, r in enumerate(rows):
                at = (pl.ds(pl.multiple_of(r * BLOCK, BLOCK), BLOCK), cs)
                dk_ref[at] += dk[seg * BLOCK:(seg + 1) * BLOCK]
                dv_ref[at] += dv[seg * BLOCK:(seg + 1) * BLOCK]

    qspec = pl.BlockSpec((BLOCK, ATT_WIDTH), lambda i: (i, 0))
    whole = pl.BlockSpec((n, KV_WIDTH), lambda i: (0, 0))
    sinks = pl.BlockSpec((ATT_HEADS, 128), lambda i: (0, 0))
    return pl.pallas_call(body, grid=(nb,), in_specs=[qspec] + _att_specs(nb, COL_AV256) + [sinks, qspec], out_specs=(qspec, whole, whole, sinks),
                          out_shape=(_sds((n, ATT_WIDTH), BF16), _sds((n, KV_WIDTH), F32), _sds((n, KV_WIDTH), F32), _sds((ATT_HEADS, 128), F32)),
                          name=name, compiler_params=pltpu.CompilerParams(dimension_semantics=("arbitrary",)))(
                              q, k, k, k, k, proj, proj, proj, proj, sink_b, dmixed)


def _ret_decay(lg, d):
    a = lax.broadcasted_iota(jnp.int32, (BLOCK, 1), 0)
    b = lax.broadcasted_iota(jnp.int32, (1, BLOCK), 1)
    t_col = a + d * (BLOCK - 1 - 2 * a)
    t_row = b + d * (BLOCK - 1 - 2 * b)
    diff = t_col - t_row
    dist = jnp.maximum(diff, 0).astype(F32)
    dmask = jnp.where(diff >= d, jnp.exp(lg * dist), 0.0)
    tf = t_col.astype(F32)
    xi = jnp.exp(lg * (tf + 1.0))
    zeta = jnp.exp(lg * (BLOCK - 1.0 - tf))
    gam = jnp.exp(jnp.full((1, 1), BLOCK, F32) * lg)
    return dmask, dist, xi, zeta, gam, tf


RET_GROUP = 3


def _ret_group(nc):
    return next(c for c in (RET_GROUP, 2, 1) if nc % c == 0)


def _ret_fwd(q, k, proj, lg, name):
    n = q.shape[0]
    nc = n // BLOCK
    hd = RET_HEAD_DIM
    ch = _ret_group(nc)
    ns, rows = nc // ch, ch * BLOCK
    group = lambda d, s: s + d * (ns - 1 - 2 * s)

    def body(lg_ref, q_ref, k_ref, v0, v1, v2, v3, o_ref, st_ref, s_ref):
        d, s = pl.program_id(0), pl.program_id(1)

        @pl.when(s == 0)
        def _():
            s_ref[...] = jnp.zeros_like(s_ref)

        for h, v_ref in enumerate((v0, v1, v2, v3)):
            cs = slice(h * hd, (h + 1) * hd)
            dmask, _, xi, zeta, gam, _ = _ret_decay(lg_ref[h, d], d)
            sb = s_ref[h]
            for j in range(ch):
                at = pl.ds(pl.multiple_of((j + d * (ch - 1 - 2 * j)) * BLOCK, BLOCK), BLOCK)
                qv = q_ref[at, cs]
                kf = k_ref[at, cs].astype(F32) * RET_K_SCALE
                vv = v_ref[at, :]
                sc = lax.dot_general(qv, kf.astype(BF16), (((1,), (1,)), ((), ())), preferred_element_type=F32)
                sb16 = sb.astype(BF16)
                o_ref[at, cs] = (jnp.dot((sc * dmask).astype(BF16), vv, preferred_element_type=F32)
                                 + jnp.dot((qv.astype(F32) * xi).astype(BF16), sb16, preferred_element_type=F32))
                st_ref[h, j] = sb16
                sb = gam * sb + lax.dot_general((kf * zeta).astype(BF16), vv, (((0,), (0,)), ((), ())), preferred_element_type=F32)
            s_ref[h] = sb

    wide = pl.BlockSpec((rows, RET_WIDTH), lambda d, s: (group(d, s), 0))
    vblk = lambda h: pl.BlockSpec((rows, hd), lambda d, s: (group(d, s), COL_RV + h))
    return pl.pallas_call(
        body, grid=(2, ns), in_specs=[pl.BlockSpec(memory_space=pltpu.SMEM), wide, wide] + [vblk(h) for h in range(RET_HEADS)],
        out_specs=(pl.BlockSpec((None, rows, RET_WIDTH), lambda d, s: (d, group(d, s), 0)),
                   pl.BlockSpec((RET_HEADS, None, ch, hd, hd), lambda d, s: (0, d, s, 0, 0))),
        out_shape=(_sds((2, n, RET_WIDTH), F32), _sds((RET_HEADS, 2, nc, hd, hd), BF16)), scratch_shapes=[pltpu.VMEM((RET_HEADS, hd, hd), F32)],
        name=name, compiler_params=pltpu.CompilerParams(dimension_semantics=("parallel", "arbitrary")))(lg, q, k, proj, proj, proj, proj)


def _ret_bwd(q, k, proj, lg, do, states, name):
    n = q.shape[0]
    nc = n // BLOCK
    hd = RET_HEAD_DIM
    ch = _ret_group(nc)
    ns, rows = nc // ch, ch * BLOCK
    group = lambda d, r: (ns - 1 - r) + d * (2 * r - (ns - 1))

    def body(lg_ref, q_ref, k_ref, v0, v1, v2, v3, do_ref, st_ref, dq_ref, dk_ref, dv_ref, dlg_ref, ds_ref):
        d, r = pl.program_id(0), pl.program_id(1)

        @pl.when(r == 0)
        def _():
            ds_ref[...] = jnp.zeros_like(ds_ref)
            dlg_ref[...] = jnp.zeros_like(dlg_ref)

        first_row = group(d, r) * rows
        nt = (((1,), (1,)), ((), ()))
        tn = (((0,), (0,)), ((), ()))
        for h, v_ref in enumerate((v0, v1, v2, v3)):
            cs = slice(h * hd, (h + 1) * hd)
            dmask, dist, xi, zeta, gam, tf = _ret_decay(lg_ref[h, d], d)
            dsn = ds_ref[h]
            dlg = jnp.zeros((1, 1), F32)
            for j in reversed(range(ch)):
                start = (j + d * (ch - 1 - 2 * j)) * BLOCK
                at = pl.ds(pl.multiple_of(start, BLOCK), BLOCK)
                keep = (lax.broadcasted_iota(jnp.int32, (BLOCK, 1), 0) + first_row + start) >= PAD_FRONT
                qv, vv, dov = q_ref[at, cs], v_ref[at, :], do_ref[at, cs]
                qf = qv.astype(F32)
                kf = k_ref[at, cs].astype(F32) * RET_K_SCALE
                kb = kf.astype(BF16)
                sc = st_ref[h, j]
                s = lax.dot_general(qv, kb, nt, preferred_element_type=F32)
                dsc = lax.dot_general(dov, vv, nt, preferred_element_type=F32) * dmask
                dsb = dsc.astype(BF16)
                dq_c = xi * lax.dot_general(dov, sc, nt, preferred_element_type=F32)
                dk_c = zeta * lax.dot_general(vv, dsn.astype(BF16), nt, preferred_element_type=F32)
                dq = jnp.dot(dsb, kb, preferred_element_type=F32) + dq_c
                dk = lax.dot_general(dsb, qv, tn, preferred_element_type=F32) + dk_c
                dv = (lax.dot_general((s * dmask).astype(BF16), dov, tn, preferred_element_type=F32)
                      + jnp.dot((kf * zeta).astype(BF16), dsn.astype(BF16), preferred_element_type=F32))
                dlg = dlg + (jnp.sum(dsc * s * dist, keepdims=True)
                             + jnp.sum((tf + 1.0) * jnp.sum(qf * dq_c, axis=-1, keepdims=True), keepdims=True)
                             + jnp.sum((BLOCK - 1.0 - tf) * jnp.sum(kf * dk_c, axis=-1, keepdims=True), keepdims=True)
                             + BLOCK * gam * jnp.sum(dsn * sc.astype(F32), keepdims=True))
                dsn = gam * dsn + lax.dot_general((qf * xi).astype(BF16), dov, tn, preferred_element_type=F32)
                dq_ref[at, cs] = dq
                dk_ref[at, cs] = jnp.where(keep, dk * RET_K_SCALE, 0.0)
                dv_ref[at, cs] = jnp.where(keep, dv, 0.0)
            ds_ref[h] = dsn
            dlg_ref[h] += dlg

    wide = pl.BlockSpec((rows, RET_WIDTH), lambda d, r: (group(d, r), 0))
    vblk = lambda h: pl.BlockSpec((rows, hd), lambda d, r: (group(d, r), COL_RV + h))
    plane = pl.BlockSpec((None, rows, RET_WIDTH), lambda d, r: (d, group(d, r), 0))
    return pl.pallas_call(
        body, grid=(2, ns),
        in_specs=[pl.BlockSpec(memory_space=pltpu.SMEM), wide, wide] + [vblk(h) for h in range(RET_HEADS)]
        + [wide, pl.BlockSpec((RET_HEADS, None, ch, hd, hd), lambda d, r: (0, d, ns - 1 - r, 0, 0))],
        out_specs=(plane, plane, plane, pl.BlockSpec((RET_HEADS, None, 8, 128), lambda d, r: (0, d, 0, 0))),
        out_shape=(_sds((2, n, RET_WIDTH), F32),) * 3 + (_sds((RET_HEADS, 2, 8, 128), F32),),
        scratch_shapes=[pltpu.VMEM((RET_HEADS, hd, hd), F32)], name=name,
        compiler_params=pltpu.CompilerParams(dimension_semantics=("parallel", "arbitrary")))(lg, q, k, proj, proj, proj, proj, do, states)


def _retgate_fwd(o, proj, gain, mixed, name):
    _, n, _ = o.shape
    tr = _row_tile(n, 1056)
    hd = RET_HEAD_DIM

    def body(o_ref, rg_ref, g_ref, _, y_ref):
        ov = o_ref[0] + o_ref[1]
        r = lax.rsqrt(jnp.mean(ov * ov, axis=-1, keepdims=True) + EPS)
        y_ref[...] = (_silu(rg_ref[...].astype(F32)) * (ov * r * g_ref[...])).astype(y_ref.dtype)

    return pl.pallas_call(body, grid=(n // tr, RET_HEADS),
                          in_specs=[pl.BlockSpec((2, tr, hd), lambda i, h: (0, i, h)), pl.BlockSpec((tr, hd), lambda i, h: (i, COL_RG + h)),
                                    pl.BlockSpec((1, hd), lambda i, h: (0, h)), ANY],
                          out_specs=pl.BlockSpec((tr, hd), lambda i, h: (i, ATT_WIDTH // hd + h)), out_shape=_sds(mixed.shape, mixed.dtype),
                          input_output_aliases={3: 0}, name=name)(o, proj, gain, mixed)


def _retgate_bwd(dmixed, o, proj, gain, name):
    _, n, _ = o.shape
    tr = _row_tile(n, 1056)
    hd = RET_HEAD_DIM

    def body(dy_ref, o_ref, rg_ref, g_ref, do_ref, drg_ref, dg_ref):
        i = pl.program_id(1)
        ov = o_ref[0] + o_ref[1]
        r = lax.rsqrt(jnp.mean(ov * ov, axis=-1, keepdims=True) + EPS)
        xhat = ov * r
        rg = rg_ref[...].astype(F32)
        dy = dy_ref[...].astype(F32)
        drg_ref[...] = (dy * (xhat * g_ref[...]) * _dsilu(rg)).astype(drg_ref.dtype)
        dn = dy * _silu(rg)
        dxh = dn * g_ref[...]
        do_ref[...] = (r * (dxh - xhat * jnp.mean(dxh * xhat, axis=-1, keepdims=True))).astype(do_ref.dtype)

        @pl.when(i == 0)
        def _():
            dg_ref[...] = jnp.zeros_like(dg_ref)

        dg_ref[...] += jnp.sum(dn * xhat, axis=0, keepdims=True)

    tile = pl.BlockSpec((tr, hd), lambda h, i: (i, h))
    vec = pl.BlockSpec((1, hd), lambda h, i: (0, h))
    rg_cols = pl.BlockSpec((tr, hd), lambda h, i: (i, COL_RG + h))
    return pl.pallas_call(body, grid=(RET_HEADS, n // tr),
                          in_specs=[pl.BlockSpec((tr, hd), lambda h, i: (i, ATT_WIDTH // hd + h)), pl.BlockSpec((2, tr, hd), lambda h, i: (0, i, h)),
                                    rg_cols, vec],
                          out_specs=(tile, rg_cols, vec), out_shape=(_sds((n, RET_WIDTH), BF16), _sds((n, IN_COLS), BF16), _sds((1, RET_WIDTH), F32)),
                          name=name, compiler_params=pltpu.CompilerParams(dimension_semantics=("parallel", "arbitrary")))(dmixed, o, proj, gain)


FFN_TILE = 512


def _swiglu_fwd(x, wg_t, wu_t, name):
    n, k = x.shape
    tm = _row_tile(n, TOKEN_TILE)
    nt = (((1,), (1,)), ((), ()))

    def body(x_ref, g_ref, u_ref, go_ref, uo_ref, f_ref):
        g = lax.dot_general(x_ref[...], g_ref[...], nt, preferred_element_type=F32)
        u = lax.dot_general(x_ref[...], u_ref[...], nt, preferred_element_type=F32)
        go_ref[...] = g.astype(go_ref.dtype)
        uo_ref[...] = u.astype(uo_ref.dtype)
        f_ref[...] = (_silu(g) * u).astype(f_ref.dtype)

    w = pl.BlockSpec((FFN_TILE, k), lambda i, j: (j, 0))
    o = pl.BlockSpec((tm, FFN_TILE), lambda i, j: (i, j))
    return pl.pallas_call(body, grid=(n // tm, wg_t.shape[0] // FFN_TILE), in_specs=[pl.BlockSpec((tm, k), lambda i, j: (i, 0)), w, w],
                          out_specs=(o, o, o), out_shape=(_sds((n, wg_t.shape[0]), BF16),) * 3, name=name,
                          compiler_params=pltpu.CompilerParams(dimension_semantics=("parallel", "parallel")))(x, wg_t, wu_t)


def _swiglu_bwd(dy, wd, gate, up, name):
    n, k = dy.shape
    tm = _row_tile(n, TOKEN_TILE)
    nt = (((1,), (1,)), ((), ()))

    def body(dy_ref, w_ref, g_ref, u_ref, dg_ref, du_ref):
        df = lax.dot_general(dy_ref[...], w_ref[...], nt, preferred_element_type=F32)
        g = g_ref[...].astype(F32)
        dg_ref[...] = (df * u_ref[...].astype(F32) * _dsilu(g)).astype(dg_ref.dtype)
        du_ref[...] = (df * _silu(g)).astype(du_ref.dtype)

    o = pl.BlockSpec((tm, FFN_TILE), lambda i, j: (i, j))
    return pl.pallas_call(body, grid=(n // tm, wd.shape[0] // FFN_TILE),
                          in_specs=[pl.BlockSpec((tm, k), lambda i, j: (i, 0)), pl.BlockSpec((FFN_TILE, k), lambda i, j: (j, 0)), o, o],
                          out_specs=(o, o), out_shape=(_sds((n, wd.shape[0]), BF16),) * 2, name=name,
                          compiler_params=pltpu.CompilerParams(dimension_semantics=("parallel", "parallel")))(dy, wd, gate, up)


def _loss_head(h, target, name):
    n, d = h.shape
    nb = n // BLOCK

    def body(h_ref, t_ref, dh_ref, l_ref):
        i = pl.program_id(0)

        @pl.when(i == 0)
        def _():
            l_ref[...] = jnp.zeros_like(l_ref)
            dh_ref[...] = jnp.zeros_like(dh_ref)

        @pl.when(i > 0)
        def _():
            e = h_ref[...] - t_ref[...]
            dh_ref[...] = e * (1.0 / d)
            l_ref[...] += 0.5 * jnp.sum(jnp.mean(e * e, axis=-1, keepdims=True), keepdims=True)

    blk = pl.BlockSpec((BLOCK, d), lambda i: (i, 0))
    return pl.pallas_call(body, grid=(nb,), in_specs=[blk, pl.BlockSpec((BLOCK, d), lambda i: (jnp.maximum(i - 1, 0), 0))],
                          out_specs=(blk, pl.BlockSpec((8, 128), lambda i: (0, 0))), out_shape=(_sds((n, d), F32), _sds((8, 128), F32)), name=name,
                          compiler_params=pltpu.CompilerParams(dimension_semantics=("arbitrary",)))(h, target)


def _adamw(parts, w, m, v, name, layer=None, prev=None, after=None):
    s, (r, c) = parts.shape[0], parts.shape[-2:]
    tr = _row_tile(r, max(16, (ADAMW_TILE_ELEMS // c) // 16 * 16))
    b1c, b2c = 1.0 - ADAM_B1 ** ADAM_STEP, 1.0 - ADAM_B2 ** ADAM_STEP

    def body(p_ref, w_ref, m_ref, v_ref, *rest):
        g_ref, d_ref, mo_ref, vo_ref = rest[-4:]
        g = p_ref[0].astype(F32)
        for q in range(1, s):
            g = g + p_ref[q].astype(F32)
        mn = ADAM_B1 * m_ref[...] + (1.0 - ADAM_B1) * g
        vn = ADAM_B2 * v_ref[...] + (1.0 - ADAM_B2) * jnp.square(g)
        g_ref[...] = g
        mo_ref[...] = mn
        vo_ref[...] = vn
        d_ref[...] = -ADAM_LR * ((mn / b1c) / (jnp.sqrt(vn / b2c) + ADAM_EPS) + ADAM_WD * w_ref[...])

    pspec = pl.BlockSpec((s, tr, c), lambda i: (0, i, 0))
    if layer is None:
        t = pl.BlockSpec((tr, c), lambda i: (i, 0))
        return pl.pallas_call(body, grid=(r // tr,), in_specs=[pspec, t, t, t], out_specs=(t, t, t, t), out_shape=(_sds((r, c), F32),) * 4,
                              name=name)(parts, w, m, v)
    t = pl.BlockSpec((None, tr, c), lambda i: (layer, i, 0))
    prev = prev if prev is not None else tuple(lax.empty(w.shape, F32) for _ in range(4))
    extra = [] if after is None else [after]
    return pl.pallas_call(body, grid=(r // tr,), in_specs=[pspec, t, t, t] + [ANY] * (4 + len(extra)), out_specs=(t, t, t, t),
                          out_shape=(_sds(w.shape, F32),) * 4, input_output_aliases={4 + i: i for i in range(4)}, name=name)(
                              parts, w, m, v, *prev, *extra)


def _allgather(xs, name, after=None):
    na = len(xs)
    first_out = na + (after is not None)

    def body(*refs):
        x_refs, o_refs = refs[:na], refs[first_out:first_out + na]
        send, recv, lsem = refs[first_out + na:]
        x, y, c = lax.axis_index("x"), lax.axis_index("y"), lax.axis_index("c")
        me, sib = (x, y, c), (x, y, 1 - c)
        chips = [(1 - x, y), (x, 1 - y), (1 - x, 1 - y)]
        slot = lambda p: 4 * p[0] + 2 * p[1] + p[2]

        def copy(a, k, block, to, src=None):
            dst = o_refs[a].at[slot(block)]
            return pltpu.make_async_remote_copy(src_ref=dst if src is None else src, dst_ref=dst, send_sem=send.at[a, k], recv_sem=recv.at[a, k],
                                                device_id=to, device_id_type=MESH)

        mine = [pltpu.make_async_copy(x_refs[a], o_refs[a].at[slot(me)], lsem.at[a]) for a in range(na)]
        for cp in mine:
            cp.start()
        first = []
        for a in range(na):
            first.append(copy(a, 0, me, sib, src=x_refs[a]))
            first += [copy(a, 1 + j, me, (*chip, c), src=x_refs[a]) for j, chip in enumerate(chips)]
        for cp in first:
            cp.start()
        passed = []
        for j, chip in enumerate(chips):
            for a in range(na):
                copy(a, 1 + j, (*chip, c), me).wait_recv()
                passed.append(copy(a, 4 + j, (*chip, c), sib))
                passed[-1].start()
        for a in range(na):
            copy(a, 0, sib, me).wait_recv()
            for j, chip in enumerate(chips):
                copy(a, 4 + j, (*chip, 1 - c), me).wait_recv()
        for cp in first + passed:
            cp.wait_send()
        for cp in mine:
            cp.wait()

    extra = [] if after is None else [after]
    return pl.pallas_call(body, in_specs=[ANY] * (na + len(extra)), out_specs=[ANY] * na,
                          out_shape=[_sds((N_DEV,) + t.shape, t.dtype) for t in xs],
                          scratch_shapes=[pltpu.SemaphoreType.DMA((na, 7)), pltpu.SemaphoreType.DMA((na, 7)), pltpu.SemaphoreType.DMA((na,))],
                          name=name)(*xs, *extra)


HBM = pl.BlockSpec(memory_space=pltpu.HBM)
SEM = pl.BlockSpec(memory_space=pltpu.SEMAPHORE)
EFFECT = pltpu.SideEffectType.DATAFLOW_SIDE_EFFECTING


SPLIT_RELATIONS = dict(gather=(1, 2, 4, 6),
                       forward=(2, 4, 6),
                       scatter=tuple(range(1, N_DEV)))


def _split_copies(mode, x_refs, land_refs, send, recv, own, landing):
    x, y, c = lax.axis_index("x"), lax.axis_index("y"), lax.axis_index("c")
    flip = lambda r: ((1 - x if r & 4 else x), (1 - y if r & 2 else y), (1 - c if r & 1 else c))
    slot = lambda p: 4 * p[0] + 2 * p[1] + p[2]
    me = slot((x, y, c))
    rel = SPLIT_RELATIONS[mode]
    local, remote = [], []
    for a in range(len(land_refs)):
        if mode != "forward":
            local.append(pltpu.make_async_copy(x_refs[a].at[me] if mode == "scatter" else x_refs[a], land_refs[a].at[me], own.at[a]))
        for j, r in enumerate(rel):
            if mode == "forward":
                to = flip(1)
                src = dst = land_refs[a].at[slot(flip(r ^ 1) if landing else flip(r))]
            else:
                to = flip(r)
                src = x_refs[a].at[slot(to)] if mode == "scatter" else x_refs[a]
                dst = land_refs[a].at[slot(to) if landing else me]
            remote.append(pltpu.make_async_remote_copy(src_ref=src, dst_ref=dst, send_sem=send.at[len(rel) * a + j],
                                                       recv_sem=recv.at[len(rel) * a + j], device_id=to, device_id_type=MESH))
    return local, remote


def _send_start(mode, xs, lands, after, name):
    if lands is None:
        lands = [lax.empty(t.shape if mode == "scatter" else (N_DEV,) + t.shape, t.dtype) for t in xs]
    nx, na, nr = len(xs), len(lands), len(SPLIT_RELATIONS[mode])
    nsem = 2 if mode == "forward" else 3

    def body(*refs):
        x_refs, land_refs = refs[:nx], refs[nx:nx + na]
        sems = refs[nx + na + 1:nx + na + 1 + nsem]
        local, remote = _split_copies(mode, x_refs, land_refs, sems[0], sems[1], sems[2] if nsem == 3 else None, False)
        for cp in remote + local:
            cp.start()
        refs[-1][...] = jnp.zeros_like(refs[-1])

    hbm = lambda t: pltpu.with_memory_space_constraint(t, pltpu.HBM)
    sem_shapes = [pltpu.SemaphoreType.DMA((nr * na,)), pltpu.SemaphoreType.DMA((nr * na,)), pltpu.SemaphoreType.DMA((na,))][:nsem]
    outs = pl.pallas_call(
        body, name=name,
        out_shape=(*sem_shapes, *[pltpu.HBM(t.shape, t.dtype) for t in list(xs) + list(lands)], _sds((8, 128), F32)),
        in_specs=[HBM] * (nx + na) + [ANY], out_specs=(*[SEM] * nsem, *[HBM] * (nx + na), pl.BlockSpec(memory_space=pltpu.VMEM)),
        input_output_aliases={i: nsem + i for i in range(nx + na)},
        compiler_params=pltpu.CompilerParams(has_side_effects=EFFECT))(*[hbm(t) for t in list(xs) + list(lands)], after)
    return outs[:nsem], list(outs[nsem:nsem + nx]), list(outs[nsem + nx:nsem + nx + na]), outs[-1]


def _send_wait(mode, started, after, name):
    sems, xs, lands, _ = started
    nx, na, nsem = len(xs), len(lands), len(sems)

    def body(*refs):
        s = refs[nx + na:nx + na + nsem]
        local, remote = _split_copies(mode, refs[:nx], refs[nx:nx + na], s[0], s[1], s[2] if nsem == 3 else None, True)
        for cp in remote:
            cp.wait_send()
            cp.wait_recv()
        for cp in local:
            cp.wait()

    outs = pl.pallas_call(body, name=name, out_shape=tuple(pltpu.HBM(t.shape, t.dtype) for t in xs + lands),
                          in_specs=[HBM] * (nx + na) + [SEM] * nsem + [ANY], out_specs=[HBM] * (nx + na),
                          input_output_aliases={i: i for i in range(nx + na)},
                          compiler_params=pltpu.CompilerParams(has_side_effects=EFFECT))(*xs, *lands, *sems, after)
    return list(outs[nx:])


def _local_step(x, meta, target, mix_weights_fn, ffn_weights_fn, grads_fn, sink, dec_f, dec_b, ret_norm, n_mix_pre, n_mix_post, n_ffn_pre,
                n_ffn_post):
    depth = n_mix_pre.shape[0]
    d = D_MODEL
    h = jnp.concatenate([jnp.zeros((PAD_FRONT, d), F32), meta, x], axis=0)
    n = h.shape[0]
    cos_a, sin_a, cos_r, sin_r, perm = _rope_tables(n)
    lg_all = jnp.stack([-jnp.exp(dec_f), -jnp.exp(dec_b)], axis=-1)
    saved = []
    u = _norm_fwd(h, n_mix_pre[0][None], None, BF16, "l0_norm_mix_pre")
    wi, wo = mix_weights_fn(0, h)
    for l in range(depth):
        t = f"l{l}_"
        sink_b = jnp.broadcast_to(sink[l][:, None], (ATT_HEADS, 128))
        proj = _mm_nt(u, wi, BF16, WIDE_TILE, d, t + "proj")
        aq = _rope_att(proj, 0, ATT_HEADS, cos_a, sin_a, perm, t + "rope_aq")
        ak = _rope_att(proj, ATT_WIDTH // KV_WIDTH, ATT_KV_HEADS, cos_a, sin_a, perm, t + "rope_ak")
        mixed = _att_fwd(aq, ak, proj, sink_b, t + "att")
        proj3 = proj[None]
        rq = _rope_ret(proj3, COL_RQ // 2, cos_r, sin_r, t + "rope_rq")
        rk = _rope_ret(proj3, COL_RK // 2, cos_r, sin_r, t + "rope_rk")
        o_ret, states = _ret_fwd(rq, rk, proj, lg_all[l], t + "ret")
        mixed = _retgate_fwd(o_ret, proj, ret_norm[l][None], mixed, t + "retgate")
        wg, wu, wd = ffn_weights_fn(l, mixed)
        mo, h_mid, u2 = _out_proj_norms(mixed, wo, h, n_mix_post[l][None], n_ffn_pre[l][None], t + "out_proj")
        gate, up, f = _swiglu_fwd(u2, wg, wu, t + "gate_up")
        dn = _mm_nn(f, wd, F32, 512, D_FF, t + "down", tm_cap=DEEP_K_TOKEN_TILE)
        saved.append(dict(h=h, u=u, proj=proj, aq=aq, ak=ak, rq=rq, rk=rk, o_ret=o_ret, states=states, mixed=mixed, mo=mo, h_mid=h_mid, u2=u2,
                          gate=gate, up=up, f=f, dn=dn, sink_b=sink_b, wi=wi, wo=wo, wg=wg, wu=wu, wd=wd))
        if l + 1 < depth:
            wi, wo = mix_weights_fn(l + 1, dn)
            h, u = _norm_res_norm(dn, n_ffn_post[l][None], h_mid, n_mix_pre[l + 1][None], t + "norm_ffn_post")
        else:
            h = _norm_fwd(dn, n_ffn_post[l][None], h_mid, F32, t + "norm_ffn_post")

    dh, loss_part = _loss_head(h, target, "loss_head")
    gs = dict(sink=[None] * depth, dec_f=[None] * depth, dec_b=[None] * depth, ret_norm=[None] * depth, mix_pre=[None] * depth,
              mix_post=[None] * depth, ffn_pre=[None] * depth, ffn_post=[None] * depth)
    d_dn, gs["ffn_post"][depth - 1] = _norm_bwd(dh, saved[-1]["dn"], n_ffn_post[depth - 1][None], None, BF16, f"l{depth - 1}_b_norm_ffn_post")
    for l in reversed(range(depth)):
        t = f"l{l}_b_"
        sv = saved[l]
        proj = sv["proj"]
        gw = {}
        d_gate, d_up = _swiglu_bwd(d_dn, sv["wd"], sv["gate"], sv["up"], t + "d_gate_up")
        gw["wd"] = _mm_tn(sv["f"], d_dn, WIDE_TILE, 1024, t + "dw_down")
        du2 = _mm_nn(d_gate, sv["wg"], F32, 512, D_FF, t + "du2_gate", tm_cap=DEEP_K_TOKEN_TILE)
        du2 = _mm_nn(d_up, sv["wu"], F32, 512, D_FF, t + "du2_up", acc=du2, tm_cap=DEEP_K_TOKEN_TILE)
        gw["wg"] = _mm_tn(d_gate, sv["u2"], WIDE_TILE, 1024, t + "dw_gate")
        gw["wu"] = _mm_tn(d_up, sv["u2"], WIDE_TILE, 1024, t + "dw_up")
        tok_b = grads_fn(l, "ffn", gw, du2)
        dh, d_mo, gs["ffn_pre"][l], gs["mix_post"][l] = _norm_bwd_pair(du2, sv["h_mid"], (n_ffn_pre[l] + tok_b)[None], dh, sv["mo"],
                                                                         n_mix_post[l][None], t + "norm_ffn_pre")
        d_mixed = _mm_nt(d_mo, sv["wo"], BF16, 1024, d, t + "d_mixed")
        gw["wo"] = _mm_tn(sv["mixed"], d_mo, 1024, 1024, t + "dw_out")
        d_o, dproj, gs["ret_norm"][l] = _retgate_bwd(d_mixed, sv["o_ret"], proj, ret_norm[l][None], t + "retgate")
        dq_r, dk_r, dv_r, dlg = _ret_bwd(sv["rq"], sv["rk"], proj, lg_all[l], d_o, sv["states"], t + "ret")
        draw = dlg[:, :, 0, 0] * lg_all[l]
        gs["dec_f"][l], gs["dec_b"][l] = draw[:, 0], draw[:, 1]
        dproj = _rope_ret(dq_r, 0, cos_r, -sin_r, t + "rope_rq", into=dproj, out_col=COL_RQ // 2)
        dproj = _rope_ret(dk_r, 0, cos_r, -sin_r, t + "rope_rk", into=dproj, out_col=COL_RK // 2)
        dproj = _rope_ret(dv_r, 0, None, None, t + "sum_rv", into=dproj, out_col=COL_RV // 2)
        dq_a, dk_a, dv_a, dsink = _att_bwd(sv["aq"], sv["ak"], proj, sv["sink_b"], d_mixed, t + "att")
        gs["sink"][l] = dsink[:, 0]
        dproj = _rope_att(dq_a, 0, ATT_HEADS, cos_a, -sin_a, perm, t + "rope_aq", into=dproj, out_col=0)
        dproj = _rope_att(dk_a, 0, ATT_KV_HEADS, cos_a, -sin_a, perm, t + "rope_ak", plain=dv_a, into=dproj, out_col=ATT_WIDTH // (2 * KV_WIDTH))
        gw["wi"] = _mm_tn(dproj, sv["u"], WIDE_TILE, 1024, t + "dw_in")
        tok_b = grads_fn(l, "mix", gw, dproj)
        du = _mm_nn(dproj, sv["wi"], F32, 512, IN_COLS, t + "du", tm_cap=DEEP_K_TOKEN_TILE)
        if l > 0:
            dh, d_dn, gs["mix_pre"][l], gs["ffn_post"][l - 1] = _norm_bwd_pair(du, sv["h"], (n_mix_pre[l] + tok_b)[None], dh, saved[l - 1]["dn"],
                                                                              n_ffn_post[l - 1][None], t + "norm_mix_pre")
        else:
            dh, gs["mix_pre"][l] = _norm_bwd(du, sv["h"], (n_mix_pre[l] + tok_b)[None], dh, F32, t + "norm_mix_pre")
    return loss_part[0, 0], dh, gs


def _pack_small(mix_pre, mix_post, ffn_pre, ffn_post, ret_norm, sink, dec_f, dec_b, loss, meta):
    d = D_MODEL

    def tile(a, rows=8):
        a = jnp.reshape(a, (-1, a.shape[-1])) if a.ndim else jnp.reshape(a, (1, 1))
        return jnp.pad(a, ((0, rows - a.shape[0]), (0, d - a.shape[1])))

    return jnp.concatenate([tile(mix_pre), tile(mix_post), tile(ffn_pre), tile(ffn_post), tile(ret_norm.reshape(-1, d)), tile(sink), tile(dec_f),
                            tile(dec_b), tile(loss), tile(meta, SMALL_ROWS - ROW_META)], axis=0)


def _unpack_small(p, depth):
    rows = lambda r0, cols: p[r0:r0 + depth, :cols]
    return dict(mix_pre=rows(ROW_MIX_PRE, D_MODEL), mix_post=rows(ROW_MIX_POST, D_MODEL), ffn_pre=rows(ROW_FFN_PRE, D_MODEL),
                ffn_post=rows(ROW_FFN_POST, D_MODEL), ret_norm=p[ROW_RET_NORM:ROW_RET_NORM + depth * RET_WIDTH // D_MODEL].reshape(depth, RET_WIDTH),
                sink=rows(ROW_SINK, ATT_HEADS), dec_f=rows(ROW_DEC_F, RET_HEADS), dec_b=rows(ROW_DEC_B, RET_HEADS), loss=p[ROW_LOSS, 0])


def kernel(x, meta_tokens, w_in, w_out, attn_sink, ret_decay_fwd, ret_decay_bwd, ret_norm, norm_mix_pre, norm_mix_post, w_gate, w_up, w_down, norm_ffn_pre, norm_ffn_post, loss_target, m_meta_tokens, m_w_in, m_w_out, m_attn_sink, m_ret_decay_fwd, m_ret_decay_bwd, m_ret_norm, m_norm_mix_pre, m_norm_mix_post, m_w_gate, m_w_up, m_w_down, m_norm_ffn_pre, m_norm_ffn_post, v_meta_tokens, v_w_in, v_w_out, v_attn_sink, v_ret_decay_fwd, v_ret_decay_bwd, v_ret_norm, v_norm_mix_pre, v_norm_mix_post, v_w_gate, v_w_up, v_w_down, v_norm_ffn_pre, v_norm_ffn_post):
    depth, d = w_in.shape[0], D_MODEL
    me = 4 * lax.axis_index("x") + 2 * lax.axis_index("y") + lax.axis_index("c")
    zero = jnp.zeros((), F32)

    meta_g, = _allgather([meta_tokens], "gather_meta")
    meta = meta_g.transpose(1, 0, 2).reshape(N_META, d)

    def shards(k):
        l = k // 2
        if k % 2 == 0:
            return [w_in[l].T.astype(BF16), w_out[l].astype(BF16)]
        return [w_gate[l].T.astype(BF16), w_up[l].T.astype(BF16), w_down[l].astype(BF16)]

    gathers, ahead = {}, 2
    for k in range(min(ahead + 1, 2 * depth)):
        gathers[k] = _send_start("gather", shards(k), None, gathers[k - 1][3] if k else meta_g, f"gather_start_g{k}")

    passing = {}

    def pass_on(k, after):
        lands = _send_wait("gather", gathers.pop(k), after, f"gather_wait_g{k}")
        passing[k] = _send_start("forward", [], lands, after, f"forward_start_g{k}")

    def take(k, h):
        after = h
        if k >= 1 and k + ahead < 2 * depth:
            gathers[k + ahead] = _send_start("gather", shards(k + ahead), None, h, f"gather_start_g{k + ahead}")
            after = gathers[k + ahead][3]
        elif k == 0:
            after = gathers[max(gathers)][3]
        if k not in passing:
            pass_on(k, after)
        if k >= 2 and k + 1 < 2 * depth:
            pass_on(k + 1, after)
        last = passing[k + 1][3] if k + 1 in passing else passing[k][3]
        return _send_wait("forward", passing.pop(k), last, f"forward_wait_g{k}")

    def mix_weights_fn(l, h):
        wi_t, wo = take(2 * l, h)
        return wi_t.reshape(IN_COLS, d), wo.reshape(d, d)

    def ffn_weights_fn(l, h):
        wg_t, wu_t, wd = take(2 * l + 1, h)
        return wg_t.reshape(D_FF, d), wu_t.reshape(D_FF, d), wd.reshape(D_FF, d)

    exchanges, adam, order = {}, {}, []
    tr = lambda *ts: tuple(jnp.swapaxes(t, 1, 2) for t in ts)
    big = dict(wi=tr(w_in, m_w_in, v_w_in), wg=tr(w_gate, m_w_gate, v_w_gate), wu=tr(w_up, m_w_up, v_w_up), wd=(w_down, m_w_down, v_w_down),
               wo=(w_out, m_w_out, v_w_out))
    kinds = dict(ffn=("wg", "wu", "wd"), mix=("wi", "wo"))

    arrivals = []

    def finish(key, after):
        l, part = key
        arrivals.append((key, _send_wait("scatter", exchanges.pop(key), after, f"exchange_wait_{part}_l{l}")))

    def update():
        last_start = exchanges[order[-1]][3] if order[-1] in exchanges else None
        while arrivals:
            (l, part), arrived = arrivals.pop(0)
            for kind, parts in zip(kinds[part], arrived):
                adam[kind] = _adamw(parts, *big[kind], f"adamw_{kind}_l{l}", layer=l, prev=adam.get(kind),
                                    after=None if kind in adam else last_start)

    def grads_fn(l, part, gw, after):
        packed = [gw[kind].reshape(N_DEV, -1, d) for kind in kinds[part]]
        exchanges[(l, part)] = _send_start("scatter", packed, None, after, f"exchange_start_{part}_l{l}")
        order.append((l, part))
        token = exchanges[(l, part)][3]
        if len(order) > 2:
            finish(order[-3], token)
        return token[0, 0]

    loss_part, dh, gs = _local_step(x[0], meta, loss_target[0], mix_weights_fn, ffn_weights_fn, grads_fn, attn_sink, ret_decay_fwd, ret_decay_bwd, ret_norm,
                                    norm_mix_pre, norm_mix_post, norm_ffn_pre, norm_ffn_post)
    grad_x = dh[BLOCK:][None]

    st = lambda xs: jnp.stack([t.reshape(-1) if t.ndim == 1 else t[0] for t in xs])
    small = _pack_small(st(gs["mix_pre"]), st(gs["mix_post"]), st(gs["ffn_pre"]), st(gs["ffn_post"]), st(gs["ret_norm"]), st(gs["sink"]),
                        st(gs["dec_f"]), st(gs["dec_b"]), loss_part, dh[PAD_FRONT:BLOCK])
    update()
    small_g, = _allgather([small], "gather_small", after=adam["wo"][0])
    for key in order[-2:]:
        finish(key, small_g)
    update()
    o_wi, o_wo, o_wg, o_wu, o_wd = tr(*adam["wi"]), adam["wo"], tr(*adam["wg"]), tr(*adam["wu"]), adam["wd"]
    zmeta = jnp.zeros((N_META, d), F32)
    packs = [_pack_small(a[0], a[1], a[2], a[3], a[4], a[5], a[6], a[7], zero, zmeta) for a in (
        (norm_mix_pre, norm_mix_post, norm_ffn_pre, norm_ffn_post, ret_norm, attn_sink, ret_decay_fwd, ret_decay_bwd),
        (m_norm_mix_pre, m_norm_mix_post, m_norm_ffn_pre, m_norm_ffn_post, m_ret_norm, m_attn_sink, m_ret_decay_fwd, m_ret_decay_bwd),
        (v_norm_mix_pre, v_norm_mix_post, v_norm_ffn_pre, v_norm_ffn_post, v_ret_norm, v_attn_sink, v_ret_decay_fwd, v_ret_decay_bwd))]
    o_small = [_unpack_small(o, depth) for o in _adamw(small_g, packs[0], packs[1], packs[2], "adamw_small")]
    meta_parts = lax.dynamic_slice(small_g, (0, ROW_META, me * (d // N_DEV)), (N_DEV, N_META, d // N_DEV))
    o_meta = _adamw(meta_parts, meta_tokens, m_meta_tokens, v_meta_tokens, "adamw_meta")

    outs = []
    for i in range(4):
        s = o_small[i]
        outs += [o_meta[i], o_wi[i], o_wo[i], s["sink"], s["dec_f"], s["dec_b"], s["ret_norm"], s["mix_pre"], s["mix_post"], o_wg[i], o_wu[i],
                 o_wd[i], s["ffn_pre"], s["ffn_post"]]
    return (o_small[0]["loss"], grad_x, *outs)
```

```python
import jax
import jax.numpy as jnp
import numpy as np
from jax import lax
from jax.experimental import pallas as pl
from jax.experimental.pallas import tpu as pltpu

F32, BF16 = jnp.float32, jnp.bfloat16

D_MODEL = 2048
N_META = 16
BLOCK = 128
WINDOW = 128
PAD_FRONT = BLOCK - N_META
ATT_HEAD_DIM = 128
ATT_WIDTH = D_MODEL // 2
ATT_HEADS = ATT_WIDTH // ATT_HEAD_DIM
ATT_KV_HEADS = 2
ATT_GROUP = ATT_HEADS // ATT_KV_HEADS
KV_WIDTH = ATT_KV_HEADS * ATT_HEAD_DIM
ROT_DIM = ATT_HEAD_DIM // 4
ROPE_THETA = 500000.0
RET_WIDTH = D_MODEL - ATT_WIDTH
RET_HEAD_DIM = 256
RET_HEADS = RET_WIDTH // RET_HEAD_DIM
RET_THETA = 10000.0
D_FF = 5632
IN_COLS = ATT_WIDTH + 2 * KV_WIDTH + 4 * RET_WIDTH
N_DEV = 8
EPS = 1e-6
NEG = -1e30
RET_K_SCALE = RET_HEAD_DIM ** -0.5
ATT_SCALE = ATT_HEAD_DIM ** -0.5

COL_AV256 = (ATT_WIDTH + KV_WIDTH) // 256
COL_RQ = (ATT_WIDTH + 2 * KV_WIDTH) // RET_HEAD_DIM
COL_RK = COL_RQ + RET_HEADS
COL_RV = COL_RK + RET_HEADS
COL_RG = COL_RV + RET_HEADS

ADAM_LR, ADAM_B1, ADAM_B2, ADAM_EPS, ADAM_WD, ADAM_STEP = 0.001, 0.9, 0.999, 1e-08, 0.01, 10

ROW_MIX_PRE, ROW_MIX_POST, ROW_FFN_PRE, ROW_FFN_POST, ROW_RET_NORM, ROW_SINK, ROW_DEC_F, ROW_DEC_B, ROW_LOSS, ROW_META, SMALL_ROWS = (
    0, 8, 16, 24, 32, 40, 48, 56, 64, 72, 96)
ADAMW_TILE_ELEMS = 128 * 1024

MESH = pl.DeviceIdType.MESH
ANY = pl.BlockSpec(memory_space=pl.ANY)


def _row_tile(n, cap):
    for t in range(cap - cap % 16, 0, -16):
        if n % t == 0:
            return t
    raise ValueError(n)


def _sds(shape, dtype):
    return jax.ShapeDtypeStruct(shape, dtype)


def _silu(x):
    return x * jax.nn.sigmoid(x)


def _dsilu(x):
    s = jax.nn.sigmoid(x)
    return s * (1.0 + x * (1.0 - s))


def _norm_fwd(x, g, res, out_dtype, name):
    n, d = x.shape
    tr = _row_tile(n, 384)

    def body(*refs):
        if res is None:
            x_ref, g_ref, o_ref = refs
        else:
            x_ref, g_ref, r_ref, o_ref = refs
        xv = x_ref[...]
        r = lax.rsqrt(jnp.mean(xv * xv, axis=-1, keepdims=True) + EPS)
        y = xv * r * g_ref[...]
        if res is not None:
            y = y + r_ref[...]
        o_ref[...] = y.astype(o_ref.dtype)

    row = pl.BlockSpec((tr, d), lambda i: (i, 0))
    ins = [row, pl.BlockSpec((1, d), lambda i: (0, 0))] + ([row] if res is not None else [])
    args = (x, g) + ((res,) if res is not None else ())
    return pl.pallas_call(body, grid=(n // tr,), in_specs=ins, out_specs=row, out_shape=_sds((n, d), out_dtype), name=name)(*args)


def _norm_res_norm(x, g, res, g_next, name):
    n, d = x.shape
    tr = _row_tile(n, 384)

    def body(x_ref, g_ref, r_ref, gn_ref, h_ref, u_ref):
        xv = x_ref[...]
        hv = r_ref[...] + xv * lax.rsqrt(jnp.mean(xv * xv, axis=-1, keepdims=True) + EPS) * g_ref[...]
        h_ref[...] = hv
        u_ref[...] = (hv * lax.rsqrt(jnp.mean(hv * hv, axis=-1, keepdims=True) + EPS) * gn_ref[...]).astype(u_ref.dtype)

    row = pl.BlockSpec((tr, d), lambda i: (i, 0))
    vec = pl.BlockSpec((1, d), lambda i: (0, 0))
    return pl.pallas_call(body, grid=(n // tr,), in_specs=[row, vec, row, vec], out_specs=(row, row),
                          out_shape=(_sds((n, d), F32), _sds((n, d), BF16)), name=name)(x, g, res, g_next)


OUT_PROJ_TOKEN_TILE = 192


def _out_proj_norms(x, w, res, g, g_next, name):
    n, k = x.shape
    d = w.shape[1]
    tm = _row_tile(n, OUT_PROJ_TOKEN_TILE)

    def body(x_ref, w_ref, r_ref, g_ref, gn_ref, mo_ref, h_ref, u_ref):
        mo = jnp.dot(x_ref[...], w_ref[...], preferred_element_type=F32)
        mo_ref[...] = mo
        hv = r_ref[...] + mo * lax.rsqrt(jnp.mean(mo * mo, axis=-1, keepdims=True) + EPS) * g_ref[...]
        h_ref[...] = hv
        u_ref[...] = (hv * lax.rsqrt(jnp.mean(hv * hv, axis=-1, keepdims=True) + EPS) * gn_ref[...]).astype(u_ref.dtype)

    row = pl.BlockSpec((tm, d), lambda i: (i, 0))
    vec = pl.BlockSpec((1, d), lambda i: (0, 0))
    return pl.pallas_call(body, grid=(n // tm,), in_specs=[pl.BlockSpec((tm, k), lambda i: (i, 0)), pl.BlockSpec((k, d), lambda i: (0, 0)), row, vec, vec],
                          out_specs=(row, row, row), out_shape=(_sds((n, d), F32), _sds((n, d), F32), _sds((n, d), BF16)), name=name)(
                              x, w, res, g, g_next)


def _norm_bwd(dy, x, g, res, out_dtype, name):
    n, d = x.shape
    tr = _row_tile(n, 384)

    def body(*refs):
        if res is None:
            dy_ref, x_ref, g_ref, dx_ref, dg_ref = refs
        else:
            dy_ref, x_ref, g_ref, r_ref, dx_ref, dg_ref = refs
        i = pl.program_id(0)
        xv = x_ref[...]
        r = lax.rsqrt(jnp.mean(xv * xv, axis=-1, keepdims=True) + EPS)
        xhat = xv * r
        dyf = dy_ref[...].astype(F32)
        gdy = dyf * g_ref[...]
        dx = r * (gdy - xhat * jnp.mean(gdy * xhat, axis=-1, keepdims=True))
        if res is not None:
            dx = dx + r_ref[...]
        dx_ref[...] = dx.astype(dx_ref.dtype)

        @pl.when(i == 0)
        def _():
            dg_ref[...] = jnp.zeros_like(dg_ref)

        dg_ref[...] += jnp.sum(dyf * xhat, axis=0, keepdims=True)

    row = pl.BlockSpec((tr, d), lambda i: (i, 0))
    vec = pl.BlockSpec((1, d), lambda i: (0, 0))
    ins = [row, row, vec] + ([row] if res is not None else [])
    args = (dy, x, g) + ((res,) if res is not None else ())
    return pl.pallas_call(body, grid=(n // tr,), in_specs=ins, out_specs=(row, vec),
                          out_shape=(_sds((n, d), out_dtype), _sds((1, d), F32)), name=name,
                          compiler_params=pltpu.CompilerParams(dimension_semantics=("arbitrary",)))(*args)


def _norm_bwd_pair(dy, x1, g1, res, x2, g2, name):
    n, d = x1.shape
    tr = _row_tile(n, 192)

    def rms_bwd(dyf, xv, g):
        r = lax.rsqrt(jnp.mean(xv * xv, axis=-1, keepdims=True) + EPS)
        xhat = xv * r
        gdy = dyf * g
        return r * (gdy - xhat * jnp.mean(gdy * xhat, axis=-1, keepdims=True)), jnp.sum(dyf * xhat, axis=0, keepdims=True)

    def body(dy_ref, x1_ref, g1_ref, r_ref, x2_ref, g2_ref, dh_ref, d2_ref, dg1_ref, dg2_ref):
        @pl.when(pl.program_id(0) == 0)
        def _():
            dg1_ref[...] = jnp.zeros_like(dg1_ref)
            dg2_ref[...] = jnp.zeros_like(dg2_ref)

        dx1, s1 = rms_bwd(dy_ref[...].astype(F32), x1_ref[...], g1_ref[...])
        dh = dx1 + r_ref[...]
        dh_ref[...] = dh
        dx2, s2 = rms_bwd(dh, x2_ref[...], g2_ref[...])
        d2_ref[...] = dx2.astype(d2_ref.dtype)
        dg1_ref[...] += s1
        dg2_ref[...] += s2

    row = pl.BlockSpec((tr, d), lambda i: (i, 0))
    vec = pl.BlockSpec((1, d), lambda i: (0, 0))
    return pl.pallas_call(body, grid=(n // tr,), in_specs=[row, row, vec, row, row, vec], out_specs=(row, row, vec, vec),
                          out_shape=(_sds((n, d), F32), _sds((n, d), BF16), _sds((1, d), F32), _sds((1, d), F32)), name=name,
                          compiler_params=pltpu.CompilerParams(dimension_semantics=("arbitrary",)))(dy, x1, g1, res, x2, g2)


def _mm(a, b, *, ta, tb, grid, a_blk, a_map, b_blk, b_map, o_blk, o_map, o_shape, o_dtype, name, acc=None):
    nk = grid[2]
    dims = (((0,) if ta else (1,), (1,) if tb else (0,)), ((), ()))

    def body(*refs):
        if acc is None:
            a_ref, b_ref, o_ref = refs[:3]
            c_ref = None
        else:
            a_ref, b_ref, c_ref, o_ref = refs[:4]
        part = lax.dot_general(a_ref[...], b_ref[...], dims, preferred_element_type=F32)
        if nk == 1:
            if c_ref is not None:
                part = part + c_ref[...].astype(F32)
            o_ref[...] = part.astype(o_ref.dtype)
            return
        acc_ref = refs[-1]
        k = pl.program_id(2)

        @pl.when(k == 0)
        def _():
            acc_ref[...] = jnp.zeros_like(acc_ref) if c_ref is None else c_ref[...].astype(F32)

        acc_ref[...] += part

        @pl.when(k == nk - 1)
        def _():
            o_ref[...] = acc_ref[...].astype(o_ref.dtype)

    ins = [pl.BlockSpec(a_blk, a_map), pl.BlockSpec(b_blk, b_map)]
    args = [a, b]
    if acc is not None:
        ins.append(pl.BlockSpec(o_blk, o_map))
        args.append(acc)
    return pl.pallas_call(body, grid=grid, in_specs=ins, out_specs=pl.BlockSpec(o_blk, o_map), out_shape=_sds(o_shape, o_dtype),
                          scratch_shapes=[pltpu.VMEM(o_blk, F32)] if nk > 1 else [], name=name,
                          compiler_params=pltpu.CompilerParams(dimension_semantics=("parallel", "parallel", "arbitrary")))(*args)


TOKEN_TILE = 1056
WIDE_TILE = 1408
DEEP_K_TOKEN_TILE = 528


def _mm_nn(x, w, o_dtype, tn, tk, name, acc=None, tm_cap=TOKEN_TILE):
    n, k = x.shape
    tm = _row_tile(n, tm_cap)
    return _mm(x, w, ta=False, tb=False, grid=(n // tm, w.shape[1] // tn, k // tk), a_blk=(tm, tk), a_map=lambda i, j, kk: (i, kk),
               b_blk=(tk, tn), b_map=lambda i, j, kk: (kk, j), o_blk=(tm, tn), o_map=lambda i, j, kk: (i, j),
               o_shape=(n, w.shape[1]), o_dtype=o_dtype, name=name, acc=acc)


def _mm_nt(dy, w, o_dtype, tn, tk, name, acc=None):
    n, k = dy.shape
    tm = _row_tile(n, TOKEN_TILE)
    return _mm(dy, w, ta=False, tb=True, grid=(n // tm, w.shape[0] // tn, k // tk), a_blk=(tm, tk), a_map=lambda i, j, kk: (i, kk),
               b_blk=(tn, tk), b_map=lambda i, j, kk: (j, kk), o_blk=(tm, tn), o_map=lambda i, j, kk: (i, j),
               o_shape=(n, w.shape[0]), o_dtype=o_dtype, name=name, acc=acc)


def _mm_tn(x, dy, tm, tn, name):
    n, m = x.shape
    tk = _row_tile(n, 2 * TOKEN_TILE)
    return _mm(x, dy, ta=True, tb=False, grid=(m // tm, dy.shape[1] // tn, n // tk), a_blk=(tk, tm), a_map=lambda i, j, kk: (kk, i),
               b_blk=(tk, tn), b_map=lambda i, j, kk: (kk, j), o_blk=(tm, tn), o_map=lambda i, j, kk: (i, j),
               o_shape=(m, dy.shape[1]), o_dtype=BF16, name=name)


def _rope_tables(n):
    pos = (jnp.arange(n) - PAD_FRONT).astype(F32)
    half = ROT_DIM // 2
    ang = pos[:, None] * (ROPE_THETA ** (-jnp.arange(half, dtype=F32) / half))[None, :]
    c, s = jnp.cos(ang), jnp.sin(ang)
    rest = ATT_HEAD_DIM - ROT_DIM
    cos_a = jnp.concatenate([c, c, jnp.ones((n, rest), F32)], axis=1)
    sin_a = jnp.concatenate([-s, s, jnp.zeros((n, rest), F32)], axis=1)
    half = RET_HEAD_DIM // 2
    ang = pos[:, None] * (RET_THETA ** (-jnp.arange(half, dtype=F32) / half))[None, :]
    c, s = jnp.cos(ang), jnp.sin(ang)
    perm = np.zeros((ATT_HEAD_DIM, ATT_HEAD_DIM), np.float32)
    for i in range(ROT_DIM):
        perm[(i + ROT_DIM // 2) % ROT_DIM, i] = 1.0
    return cos_a, sin_a, jnp.concatenate([c, c], axis=1), jnp.concatenate([-s, s], axis=1), jnp.asarray(perm, BF16)


def _into(buf, own_shape):
    if buf is None:
        return _sds(own_shape, BF16), [], [], lambda n_inputs: {}
    return _sds(buf.shape, buf.dtype), [ANY], [buf], lambda n_inputs: {n_inputs: 0}


def _rope_att(x, col0, heads, cos, sin, perm, name, plain=None, into=None, out_col=0):
    n = x.shape[0]
    tr = _row_tile(n, 1056)
    hd = ATT_HEAD_DIM
    w2 = 0 if plain is None else plain.shape[1]
    width = heads * hd + w2

    def body(*refs):
        x_ref, c_ref, s_ref, p_ref, o_ref = refs[0], refs[1], refs[2], refs[3], refs[-1]
        for h in range(heads):
            cs = slice(h * hd, (h + 1) * hd)
            xb = x_ref[:, cs].astype(BF16)
            sw = jnp.dot(xb, p_ref[...], preferred_element_type=F32)
            o_ref[:, cs] = (xb.astype(F32) * c_ref[...] + sw * s_ref[...]).astype(o_ref.dtype)
        if plain is not None:
            o_ref[:, heads * hd:] = refs[4][...].astype(o_ref.dtype)

    tab = pl.BlockSpec((tr, hd), lambda i: (i, 0))
    ins = [pl.BlockSpec((tr, heads * hd), lambda i: (i, col0)), tab, tab, pl.BlockSpec((hd, hd), lambda i: (0, 0))]
    args = [x, cos, sin, perm]
    if plain is not None:
        ins.append(pl.BlockSpec((tr, w2), lambda i: (i, 0)))
        args.append(plain)
    shape, extra_specs, extra_args, alias = _into(into, (n, width))
    return pl.pallas_call(body, grid=(n // tr,), in_specs=ins + extra_specs, out_specs=pl.BlockSpec((tr, width), lambda i: (i, out_col)),
                          out_shape=shape, input_output_aliases=alias(len(ins)), name=name)(*args, *extra_args)


def _rope_ret(x, col0, cos, sin, name, into=None, out_col=0):
    p, n, _ = x.shape
    tr = _row_tile(n, 1056)
    hd = RET_HEAD_DIM

    def body(*refs):
        x_ref, o_ref = refs[0], refs[-1]
        for h in range(2):
            cs = slice(h * hd, (h + 1) * hd)
            xv = x_ref[0, :, cs].astype(F32)
            for q in range(1, p):
                xv = xv + x_ref[q, :, cs].astype(F32)
            if cos is not None:
                sw = jnp.concatenate([xv[:, hd // 2:], xv[:, :hd // 2]], axis=1)
                xv = xv * refs[1][...] + sw * refs[2][...]
            o_ref[:, cs] = xv.astype(o_ref.dtype)

    tab = pl.BlockSpec((tr, hd), lambda i, j: (i, 0))
    ins = [pl.BlockSpec((p, tr, 2 * hd), lambda i, j: (0, i, col0 + j))] + ([tab, tab] if cos is not None else [])
    args = (x,) + ((cos, sin) if cos is not None else ())
    shape, extra_specs, extra_args, alias = _into(into, (n, RET_WIDTH))
    return pl.pallas_call(body, grid=(n // tr, RET_HEADS // 2), in_specs=ins + extra_specs,
                          out_specs=pl.BlockSpec((tr, 2 * hd), lambda i, j: (i, out_col + j)), out_shape=shape,
                          input_output_aliases=alias(len(ins)), name=name)(*args, *extra_args)


def _att_mask(nblk, n_tot):
    row = lax.broadcasted_iota(jnp.int32, (BLOCK, 4 * BLOCK), 0)
    col = lax.broadcasted_iota(jnp.int32, (BLOCK, 4 * BLOCK), 1)
    qi = nblk * BLOCK + row
    seg = col // BLOCK
    cj = col % BLOCK
    kj = (nblk - 1 + seg) * BLOCK + cj
    band = (jnp.abs(qi - kj) <= WINDOW) & (kj >= PAD_FRONT) & (kj < n_tot) & (seg < 3)
    meta = (seg == 3) & (cj >= PAD_FRONT) & (jnp.abs(qi - cj) > WINDOW)
    return band | meta


def _att_specs(nb, v_col):
    kv = lambda f, cb: pl.BlockSpec((BLOCK, KV_WIDTH), lambda n: (f(n), cb))
    prev, own, nxt, first = (lambda n: jnp.maximum(n - 1, 0)), (lambda n: n), (lambda n: jnp.minimum(n + 1, nb - 1)), (lambda n: 0)
    return [kv(f, 0) for f in (prev, own, nxt, first)] + [kv(f, v_col) for f in (prev, own, nxt, first)]


def _att_probs(s, ok, snk):
    s = jnp.where(ok, s, NEG)
    m = jnp.maximum(jnp.max(s, axis=-1, keepdims=True), snk)
    p = jnp.exp(s - m)
    ps = jnp.exp(snk - m)
    inv = 1.0 / (jnp.sum(p, axis=-1, keepdims=True) + ps)
    return p * inv, ps * inv


def _att_fwd(q, k, proj, sink_b, name):
    n = q.shape[0]
    nb = n // BLOCK
    hd = ATT_HEAD_DIM

    def body(q_ref, kp, ko, kn, km, vp, vo, vn, vm, sink_ref, o_ref):
        nblk = pl.program_id(0)
        ok = _att_mask(nblk, n)
        keep = (nblk * BLOCK + lax.broadcasted_iota(jnp.int32, (BLOCK, 1), 0)) >= PAD_FRONT
        for kh in range(ATT_KV_HEADS):
            cs = slice(kh * hd, (kh + 1) * hd)
            kk = jnp.concatenate([r[:, cs] for r in (kp, ko, kn, km)], axis=0)
            vv = jnp.concatenate([r[:, cs] for r in (vp, vo, vn, vm)], axis=0)
            heads = [kh * ATT_GROUP + g for g in range(ATT_GROUP)]
            q4 = jnp.concatenate([q_ref[:, h * hd:(h + 1) * hd] for h in heads], axis=0)
            s = lax.dot_general(q4, kk, (((1,), (1,)), ((), ())), preferred_element_type=F32) * ATT_SCALE
            ps = []
            for g, h in enumerate(heads):
                p, _ = _att_probs(s[g * BLOCK:(g + 1) * BLOCK], ok, sink_ref[h:h + 1, 0:1])
                ps.append(p)
            o = jnp.dot(jnp.concatenate(ps, axis=0).astype(BF16), vv, preferred_element_type=F32)
            for g, h in enumerate(heads):
                o_ref[:, h * hd:(h + 1) * hd] = jnp.where(keep, o[g * BLOCK:(g + 1) * BLOCK], 0.0).astype(o_ref.dtype)

    qspec = pl.BlockSpec((BLOCK, ATT_WIDTH), lambda i: (i, 0))
    return pl.pallas_call(body, grid=(nb,), in_specs=[qspec] + _att_specs(nb, COL_AV256) + [pl.BlockSpec((ATT_HEADS, 128), lambda i: (0, 0))],
                          out_specs=qspec, out_shape=_sds((n, D_MODEL), BF16), name=name)(q, k, k, k, k, proj, proj, proj, proj, sink_b)


def _att_bwd(q, k, proj, sink_b, dmixed, name):
    n = q.shape[0]
    nb = n // BLOCK
    hd = ATT_HEAD_DIM

    def body(q_ref, kp, ko, kn, km, vp, vo, vn, vm, sink_ref, do_ref, dq_ref, dk_ref, dv_ref, dsink_ref):
        nblk = pl.program_id(0)

        @pl.when(nblk == 0)
        def _():
            dk_ref[...] = jnp.zeros_like(dk_ref)
            dv_ref[...] = jnp.zeros_like(dv_ref)
            dsink_ref[...] = jnp.zeros_like(dsink_ref)

        ok = _att_mask(nblk, n)
        rows = [jnp.maximum(nblk - 1, 0), nblk, jnp.minimum(nblk + 1, nb - 1), 0]
        for kh in range(ATT_KV_HEADS):
            cs = slice(kh * hd, (kh + 1) * hd)
            kk = jnp.concatenate([r[:, cs] for r in (kp, ko, kn, km)], axis=0)
            vv = jnp.concatenate([r[:, cs] for r in (vp, vo, vn, vm)], axis=0)
            heads = [kh * ATT_GROUP + g for g in range(ATT_GROUP)]
            q4 = jnp.concatenate([q_ref[:, h * hd:(h + 1) * hd] for h in heads], axis=0)
            do4 = jnp.concatenate([do_ref[:, h * hd:(h + 1) * hd] for h in heads], axis=0)
            s = lax.dot_general(q4, kk, (((1,), (1,)), ((), ())), preferred_element_type=F32) * ATT_SCALE
            dp = lax.dot_general(do4, vv, (((1,), (1,)), ((), ())), preferred_element_type=F32)
            ps, dss = [], []
            for g, h in enumerate(heads):
                p, psink = _att_probs(s[g * BLOCK:(g + 1) * BLOCK], ok, sink_ref[h:h + 1, 0:1])
                dpg = dp[g * BLOCK:(g + 1) * BLOCK]
                delta = jnp.sum(p * dpg, axis=-1, keepdims=True)
                ps.append(p)
                dss.append(p * (dpg - delta) * ATT_SCALE)
                dsink_ref[h:h + 1, :] = dsink_ref[h:h + 1, :] - jnp.sum(psink * delta, axis=0, keepdims=True)
            ds = jnp.concatenate(dss, axis=0).astype(BF16)
            pb = jnp.concatenate(ps, axis=0).astype(BF16)
            dq = jnp.dot(ds, kk, preferred_element_type=F32)
            for g, h in enumerate(heads):
                dq_ref[:, h * hd:(h + 1) * hd] = dq[g * BLOCK:(g + 1) * BLOCK].astype(dq_ref.dtype)
            dk = lax.dot_general(ds, q4, (((0,), (0,)), ((), ())), preferred_element_type=F32)
            dv = lax.dot_general(pb, do4, (((0,), (0,)), ((), ())), preferred_element_type=F32)
            for seg, r in enumerate(rows):
                at = (pl.ds(pl.multiple_of(r * BLOCK, BLOCK), BLOCK), cs)
                dk_ref[at] += dk[seg * BLOCK:(seg + 1) * BLOCK]
                dv_ref[at] += dv[seg * BLOCK:(seg + 1) * BLOCK]

    qspec = pl.BlockSpec((BLOCK, ATT_WIDTH), lambda i: (i, 0))
    whole = pl.BlockSpec((n, KV_WIDTH), lambda i: (0, 0))
    sinks = pl.BlockSpec((ATT_HEADS, 128), lambda i: (0, 0))
    return pl.pallas_call(body, grid=(nb,), in_specs=[qspec] + _att_specs(nb, COL_AV256) + [sinks, qspec], out_specs=(qspec, whole, whole, sinks),
                          out_shape=(_sds((n, ATT_WIDTH), BF16), _sds((n, KV_WIDTH), F32), _sds((n, KV_WIDTH), F32), _sds((ATT_HEADS, 128), F32)),
                          name=name, compiler_params=pltpu.CompilerParams(dimension_semantics=("arbitrary",)))(
                              q, k, k, k, k, proj, proj, proj, proj, sink_b, dmixed)


def _ret_decay(lg, d):
    a = lax.broadcasted_iota(jnp.int32, (BLOCK, 1), 0)
    b = lax.broadcasted_iota(jnp.int32, (1, BLOCK), 1)
    t_col = a + d * (BLOCK - 1 - 2 * a)
    t_row = b + d * (BLOCK - 1 - 2 * b)
    diff = t_col - t_row
    dist = jnp.maximum(diff, 0).astype(F32)
    dmask = jnp.where(diff >= d, jnp.exp(lg * dist), 0.0)
    tf = t_col.astype(F32)
    xi = jnp.exp(lg * (tf + 1.0))
    zeta = jnp.exp(lg * (BLOCK - 1.0 - tf))
    gam = jnp.exp(jnp.full((1, 1), BLOCK, F32) * lg)
    return dmask, dist, xi, zeta, gam, tf


RET_GROUP = 3


def _ret_group(nc):
    return next(c for c in (RET_GROUP, 2, 1) if nc % c == 0)


def _ret_fwd(q, k, proj, lg, name):
    n = q.shape[0]
    nc = n // BLOCK
    hd = RET_HEAD_DIM
    ch = _ret_group(nc)
    ns, rows = nc // ch, ch * BLOCK
    group = lambda d, s: s + d * (ns - 1 - 2 * s)

    def body(lg_ref, q_ref, k_ref, v0, v1, v2, v3, o_ref, st_ref, s_ref):
        d, s = pl.program_id(0), pl.program_id(1)

        @pl.when(s == 0)
        def _():
            s_ref[...] = jnp.zeros_like(s_ref)

        for h, v_ref in enumerate((v0, v1, v2, v3)):
            cs = slice(h * hd, (h + 1) * hd)
            dmask, _, xi, zeta, gam, _ = _ret_decay(lg_ref[h, d], d)
            sb = s_ref[h]
            for j in range(ch):
                at = pl.ds(pl.multiple_of((j + d * (ch - 1 - 2 * j)) * BLOCK, BLOCK), BLOCK)
                qv = q_ref[at, cs]
                kf = k_ref[at, cs].astype(F32) * RET_K_SCALE
                vv = v_ref[at, :]
                sc = lax.dot_general(qv, kf.astype(BF16), (((1,), (1,)), ((), ())), preferred_element_type=F32)
                sb16 = sb.astype(BF16)
                o_ref[at, cs] = (jnp.dot((sc * dmask).astype(BF16), vv, preferred_element_type=F32)
                                 + jnp.dot((qv.astype(F32) * xi).astype(BF16), sb16, preferred_element_type=F32))
                st_ref[h, j] = sb16
                sb = gam * sb + lax.dot_general((kf * zeta).astype(BF16), vv, (((0,), (0,)), ((), ())), preferred_element_type=F32)
            s_ref[h] = sb

    wide = pl.BlockSpec((rows, RET_WIDTH), lambda d, s: (group(d, s), 0))
    vblk = lambda h: pl.BlockSpec((rows, hd), lambda d, s: (group(d, s), COL_RV + h))
    return pl.pallas_call(
        body, grid=(2, ns), in_specs=[pl.BlockSpec(memory_space=pltpu.SMEM), wide, wide] + [vblk(h) for h in range(RET_HEADS)],
        out_specs=(pl.BlockSpec((None, rows, RET_WIDTH), lambda d, s: (d, group(d, s), 0)),
                   pl.BlockSpec((RET_HEADS, None, ch, hd, hd), lambda d, s: (0, d, s, 0, 0))),
        out_shape=(_sds((2, n, RET_WIDTH), F32), _sds((RET_HEADS, 2, nc, hd, hd), BF16)), scratch_shapes=[pltpu.VMEM((RET_HEADS, hd, hd), F32)],
        name=name, compiler_params=pltpu.CompilerParams(dimension_semantics=("parallel", "arbitrary")))(lg, q, k, proj, proj, proj, proj)


def _ret_bwd(q, k, proj, lg, do, states, name):
    n = q.shape[0]
    nc = n // BLOCK
    hd = RET_HEAD_DIM
    ch = _ret_group(nc)
    ns, rows = nc // ch, ch * BLOCK
    group = lambda d, r: (ns - 1 - r) + d * (2 * r - (ns - 1))

    def body(lg_ref, q_ref, k_ref, v0, v1, v2, v3, do_ref, st_ref, dq_ref, dk_ref, dv_ref, dlg_ref, ds_ref):
        d, r = pl.program_id(0), pl.program_id(1)

        @pl.when(r == 0)
        def _():
            ds_ref[...] = jnp.zeros_like(ds_ref)
            dlg_ref[...] = jnp.zeros_like(dlg_ref)

        first_row = group(d, r) * rows
        nt = (((1,), (1,)), ((), ()))
        tn = (((0,), (0,)), ((), ()))
        for h, v_ref in enumerate((v0, v1, v2, v3)):
            cs = slice(h * hd, (h + 1) * hd)
            dmask, dist, xi, zeta, gam, tf = _ret_decay(lg_ref[h, d], d)
            dsn = ds_ref[h]
            dlg = jnp.zeros((1, 1), F32)
            for j in reversed(range(ch)):
                start = (j + d * (ch - 1 - 2 * j)) * BLOCK
                at = pl.ds(pl.multiple_of(start, BLOCK), BLOCK)
                keep = (lax.broadcasted_iota(jnp.int32, (BLOCK, 1), 0) + first_row + start) >= PAD_FRONT
                qv, vv, dov = q_ref[at, cs], v_ref[at, :], do_ref[at, cs]
                qf = qv.astype(F32)
                kf = k_ref[at, cs].astype(F32) * RET_K_SCALE
                kb = kf.astype(BF16)
                sc = st_ref[h, j]
                s = lax.dot_general(qv, kb, nt, preferred_element_type=F32)
                dsc = lax.dot_general(dov, vv, nt, preferred_element_type=F32) * dmask
                dsb = dsc.astype(BF16)
                dq_c = xi * lax.dot_general(dov, sc, nt, preferred_element_type=F32)
                dk_c = zeta * lax.dot_general(vv, dsn.astype(BF16), nt, preferred_element_type=F32)
                dq = jnp.dot(dsb, kb, preferred_element_type=F32) + dq_c
                dk = lax.dot_general(dsb, qv, tn, preferred_element_type=F32) + dk_c
                dv = (lax.dot_general((s * dmask).astype(BF16), dov, tn, preferred_element_type=F32)
                      + jnp.dot((kf * zeta).astype(BF16), dsn.astype(BF16), preferred_element_type=F32))
                dlg = dlg + (jnp.sum(dsc * s * dist, keepdims=True)
                             + jnp.sum((tf + 1.0) * jnp.sum(qf * dq_c, axis=-1, keepdims=True), keepdims=True)
                             + jnp.sum((BLOCK - 1.0 - tf) * jnp.sum(kf * dk_c, axis=-1, keepdims=True), keepdims=True)
                             + BLOCK * gam * jnp.sum(dsn * sc.astype(F32), keepdims=True))
                dsn = gam * dsn + lax.dot_general((qf * xi).astype(BF16), dov, tn, preferred_element_type=F32)
                dq_ref[at, cs] = dq
                dk_ref[at, cs] = jnp.where(keep, dk * RET_K_SCALE, 0.0)
                dv_ref[at, cs] = jnp.where(keep, dv, 0.0)
            ds_ref[h] = dsn
            dlg_ref[h] += dlg

    wide = pl.BlockSpec((rows, RET_WIDTH), lambda d, r: (group(d, r), 0))
    vblk = lambda h: pl.BlockSpec((rows, hd), lambda d, r: (group(d, r), COL_RV + h))
    plane = pl.BlockSpec((None, rows, RET_WIDTH), lambda d, r: (d, group(d, r), 0))
    return pl.pallas_call(
        body, grid=(2, ns),
        in_specs=[pl.BlockSpec(memory_space=pltpu.SMEM), wide, wide] + [vblk(h) for h in range(RET_HEADS)]
        + [wide, pl.BlockSpec((RET_HEADS, None, ch, hd, hd), lambda d, r: (0, d, ns - 1 - r, 0, 0))],
        out_specs=(plane, plane, plane, pl.BlockSpec((RET_HEADS, None, 8, 128), lambda d, r: (0, d, 0, 0))),
        out_shape=(_sds((2, n, RET_WIDTH), F32),) * 3 + (_sds((RET_HEADS, 2, 8, 128), F32),),
        scratch_shapes=[pltpu.VMEM((RET_HEADS, hd, hd), F32)], name=name,
        compiler_params=pltpu.CompilerParams(dimension_semantics=("parallel", "arbitrary")))(lg, q, k, proj, proj, proj, proj, do, states)


def _retgate_fwd(o, proj, gain, mixed, name):
    _, n, _ = o.shape
    tr = _row_tile(n, 1056)
    hd = RET_HEAD_DIM

    def body(o_ref, rg_ref, g_ref, _, y_ref):
        ov = o_ref[0] + o_ref[1]
        r = lax.rsqrt(jnp.mean(ov * ov, axis=-1, keepdims=True) + EPS)
        y_ref[...] = (_silu(rg_ref[...].astype(F32)) * (ov * r * g_ref[...])).astype(y_ref.dtype)

    return pl.pallas_call(body, grid=(n // tr, RET_HEADS),
                          in_specs=[pl.BlockSpec((2, tr, hd), lambda i, h: (0, i, h)), pl.BlockSpec((tr, hd), lambda i, h: (i, COL_RG + h)),
                                    pl.BlockSpec((1, hd), lambda i, h: (0, h)), ANY],
                          out_specs=pl.BlockSpec((tr, hd), lambda i, h: (i, ATT_WIDTH // hd + h)), out_shape=_sds(mixed.shape, mixed.dtype),
                          input_output_aliases={3: 0}, name=name)(o, proj, gain, mixed)


def _retgate_bwd(dmixed, o, proj, gain, name):
    _, n, _ = o.shape
    tr = _row_tile(n, 1056)
    hd = RET_HEAD_DIM

    def body(dy_ref, o_ref, rg_ref, g_ref, do_ref, drg_ref, dg_ref):
        i = pl.program_id(1)
        ov = o_ref[0] + o_ref[1]
        r = lax.rsqrt(jnp.mean(ov * ov, axis=-1, keepdims=True) + EPS)
        xhat = ov * r
        rg = rg_ref[...].astype(F32)
        dy = dy_ref[...].astype(F32)
        drg_ref[...] = (dy * (xhat * g_ref[...]) * _dsilu(rg)).astype(drg_ref.dtype)
        dn = dy * _silu(rg)
        dxh = dn * g_ref[...]
        do_ref[...] = (r * (dxh - xhat * jnp.mean(dxh * xhat, axis=-1, keepdims=True))).astype(do_ref.dtype)

        @pl.when(i == 0)
        def _():
            dg_ref[...] = jnp.zeros_like(dg_ref)

        dg_ref[...] += jnp.sum(dn * xhat, axis=0, keepdims=True)

    tile = pl.BlockSpec((tr, hd), lambda h, i: (i, h))
    vec = pl.BlockSpec((1, hd), lambda h, i: (0, h))
    rg_cols = pl.BlockSpec((tr, hd), lambda h, i: (i, COL_RG + h))
    return pl.pallas_call(body, grid=(RET_HEADS, n // tr),
                          in_specs=[pl.BlockSpec((tr, hd), lambda h, i: (i, ATT_WIDTH // hd + h)), pl.BlockSpec((2, tr, hd), lambda h, i: (0, i, h)),
                                    rg_cols, vec],
                          out_specs=(tile, rg_cols, vec), out_shape=(_sds((n, RET_WIDTH), BF16), _sds((n, IN_COLS), BF16), _sds((1, RET_WIDTH), F32)),
                          name=name, compiler_params=pltpu.CompilerParams(dimension_semantics=("parallel", "arbitrary")))(dmixed, o, proj, gain)


FFN_TILE = 512


def _swiglu_fwd(x, wg_t, wu_t, name):
    n, k = x.shape
    tm = _row_tile(n, TOKEN_TILE)
    nt = (((1,), (1,)), ((), ()))

    def body(x_ref, g_ref, u_ref, go_ref, uo_ref, f_ref):
        g = lax.dot_general(x_ref[...], g_ref[...], nt, preferred_element_type=F32)
        u = lax.dot_general(x_ref[...], u_ref[...], nt, preferred_element_type=F32)
        go_ref[...] = g.astype(go_ref.dtype)
        uo_ref[...] = u.astype(uo_ref.dtype)
        f_ref[...] = (_silu(g) * u).astype(f_ref.dtype)

    w = pl.BlockSpec((FFN_TILE, k), lambda i, j: (j, 0))
    o = pl.BlockSpec((tm, FFN_TILE), lambda i, j: (i, j))
    return pl.pallas_call(body, grid=(n // tm, wg_t.shape[0] // FFN_TILE), in_specs=[pl.BlockSpec((tm, k), lambda i, j: (i, 0)), w, w],
                          out_specs=(o, o, o), out_shape=(_sds((n, wg_t.shape[0]), BF16),) * 3, name=name,
                          compiler_params=pltpu.CompilerParams(dimension_semantics=("parallel", "parallel")))(x, wg_t, wu_t)


def _swiglu_bwd(dy, wd, gate, up, name):
    n, k = dy.shape
    tm = _row_tile(n, TOKEN_TILE)
    nt = (((1,), (1,)), ((), ()))

    def body(dy_ref, w_ref, g_ref, u_ref, dg_ref, du_ref):
        df = lax.dot_general(dy_ref[...], w_ref[...], nt, preferred_element_type=F32)
        g = g_ref[...].astype(F32)
        dg_ref[...] = (df * u_ref[...].astype(F32) * _dsilu(g)).astype(dg_ref.dtype)
        du_ref[...] = (df * _silu(g)).astype(du_ref.dtype)

    o = pl.BlockSpec((tm, FFN_TILE), lambda i, j: (i, j))
    return pl.pallas_call(body, grid=(n // tm, wd.shape[0] // FFN_TILE),
                          in_specs=[pl.BlockSpec((tm, k), lambda i, j: (i, 0)), pl.BlockSpec((FFN_TILE, k), lambda i, j: (j, 0)), o, o],
                          out_specs=(o, o), out_shape=(_sds((n, wd.shape[0]), BF16),) * 2, name=name,
                          compiler_params=pltpu.CompilerParams(dimension_semantics=("parallel", "parallel")))(dy, wd, gate, up)


def _loss_head(h, target, name):
    n, d = h.shape
    nb = n // BLOCK

    def body(h_ref, t_ref, dh_ref, l_ref):
        i = pl.program_id(0)

        @pl.when(i == 0)
        def _():
            l_ref[...] = jnp.zeros_like(l_ref)
            dh_ref[...] = jnp.zeros_like(dh_ref)

        @pl.when(i > 0)
        def _():
            e = h_ref[...] - t_ref[...]
            dh_ref[...] = e * (1.0 / d)
            l_ref[...] += 0.5 * jnp.sum(jnp.mean(e * e, axis=-1, keepdims=True), keepdims=True)

    blk = pl.BlockSpec((BLOCK, d), lambda i: (i, 0))
    return pl.pallas_call(body, grid=(nb,), in_specs=[blk, pl.BlockSpec((BLOCK, d), lambda i: (jnp.maximum(i - 1, 0), 0))],
                          out_specs=(blk, pl.BlockSpec((8, 128), lambda i: (0, 0))), out_shape=(_sds((n, d), F32), _sds((8, 128), F32)), name=name,
                          compiler_params=pltpu.CompilerParams(dimension_semantics=("arbitrary",)))(h, target)


def _adamw(parts, w, m, v, name, layer=None, prev=None, after=None):
    s, (r, c) = parts.shape[0], parts.shape[-2:]
    tr = _row_tile(r, max(16, (ADAMW_TILE_ELEMS // c) // 16 * 16))
    b1c, b2c = 1.0 - ADAM_B1 ** ADAM_STEP, 1.0 - ADAM_B2 ** ADAM_STEP

    def body(p_ref, w_ref, m_ref, v_ref, *rest):
        g_ref, d_ref, mo_ref, vo_ref = rest[-4:]
        g = p_ref[0].astype(F32)
        for q in range(1, s):
            g = g + p_ref[q].astype(F32)
        mn = ADAM_B1 * m_ref[...] + (1.0 - ADAM_B1) * g
        vn = ADAM_B2 * v_ref[...] + (1.0 - ADAM_B2) * jnp.square(g)
        g_ref[...] = g
        mo_ref[...] = mn
        vo_ref[...] = vn
        d_ref[...] = -ADAM_LR * ((mn / b1c) / (jnp.sqrt(vn / b2c) + ADAM_EPS) + ADAM_WD * w_ref[...])

    pspec = pl.BlockSpec((s, tr, c), lambda i: (0, i, 0))
    if layer is None:
        t = pl.BlockSpec((tr, c), lambda i: (i, 0))
        return pl.pallas_call(body, grid=(r // tr,), in_specs=[pspec, t, t, t], out_specs=(t, t, t, t), out_shape=(_sds((r, c), F32),) * 4,
                              name=name)(parts, w, m, v)
    t = pl.BlockSpec((None, tr, c), lambda i: (layer, i, 0))
    prev = prev if prev is not None else tuple(lax.empty(w.shape, F32) for _ in range(4))
    extra = [] if after is None else [after]
    return pl.pallas_call(body, grid=(r // tr,), in_specs=[pspec, t, t, t] + [ANY] * (4 + len(extra)), out_specs=(t, t, t, t),
                          out_shape=(_sds(w.shape, F32),) * 4, input_output_aliases={4 + i: i for i in range(4)}, name=name)(
                              parts, w, m, v, *prev, *extra)


def _allgather(xs, name, after=None):
    na = len(xs)
    first_out = na + (after is not None)

    def body(*refs):
        x_refs, o_refs = refs[:na], refs[first_out:first_out + na]
        send, recv, lsem = refs[first_out + na:]
        x, y, c = lax.axis_index("x"), lax.axis_index("y"), lax.axis_index("c")
        me, sib = (x, y, c), (x, y, 1 - c)
        chips = [(1 - x, y), (x, 1 - y), (1 - x, 1 - y)]
        slot = lambda p: 4 * p[0] + 2 * p[1] + p[2]

        def copy(a, k, block, to, src=None):
            dst = o_refs[a].at[slot(block)]
            return pltpu.make_async_remote_copy(src_ref=dst if src is None else src, dst_ref=dst, send_sem=send.at[a, k], recv_sem=recv.at[a, k],
                                                device_id=to, device_id_type=MESH)

        mine = [pltpu.make_async_copy(x_refs[a], o_refs[a].at[slot(me)], lsem.at[a]) for a in range(na)]
        for cp in mine:
            cp.start()
        first = []
        for a in range(na):
            first.append(copy(a, 0, me, sib, src=x_refs[a]))
            first += [copy(a, 1 + j, me, (*chip, c), src=x_refs[a]) for j, chip in enumerate(chips)]
        for cp in first:
            cp.start()
        passed = []
        for j, chip in enumerate(chips):
            for a in range(na):
                copy(a, 1 + j, (*chip, c), me).wait_recv()
                passed.append(copy(a, 4 + j, (*chip, c), sib))
                passed[-1].start()
        for a in range(na):
            copy(a, 0, sib, me).wait_recv()
            for j, chip in enumerate(chips):
                copy(a, 4 + j, (*chip, 1 - c), me).wait_recv()
        for cp in first + passed:
            cp.wait_send()
        for cp in mine:
            cp.wait()

    extra = [] if after is None else [after]
    return pl.pallas_call(body, in_specs=[ANY] * (na + len(extra)), out_specs=[ANY] * na,
                          out_shape=[_sds((N_DEV,) + t.shape, t.dtype) for t in xs],
                          scratch_shapes=[pltpu.SemaphoreType.DMA((na, 7)), pltpu.SemaphoreType.DMA((na, 7)), pltpu.SemaphoreType.DMA((na,))],
                          name=name)(*xs, *extra)


HBM = pl.BlockSpec(memory_space=pltpu.HBM)
SEM = pl.BlockSpec(memory_space=pltpu.SEMAPHORE)
EFFECT = pltpu.SideEffectType.DATAFLOW_SIDE_EFFECTING


SPLIT_RELATIONS = dict(gather=(1, 2, 4, 6),
                       forward=(2, 4, 6),
                       scatter=tuple(range(1, N_DEV)))


def _split_copies(mode, x_refs, land_refs, send, recv, own, landing):
    x, y, c = lax.axis_index("x"), lax.axis_index("y"), lax.axis_index("c")
    flip = lambda r: ((1 - x if r & 4 else x), (1 - y if r & 2 else y), (1 - c if r & 1 else c))
    slot = lambda p: 4 * p[0] + 2 * p[1] + p[2]
    me = slot((x, y, c))
    rel = SPLIT_RELATIONS[mode]
    local, remote = [], []
    for a in range(len(land_refs)):
        if mode != "forward":
            local.append(pltpu.make_async_copy(x_refs[a].at[me] if mode == "scatter" else x_refs[a], land_refs[a].at[me], own.at[a]))
        for j, r in enumerate(rel):
            if mode == "forward":
                to = flip(1)
                src = dst = land_refs[a].at[slot(flip(r ^ 1) if landing else flip(r))]
            else:
                to = flip(r)
                src = x_refs[a].at[slot(to)] if mode == "scatter" else x_refs[a]
                dst = land_refs[a].at[slot(to) if landing else me]
            remote.append(pltpu.make_async_remote_copy(src_ref=src, dst_ref=dst, send_sem=send.at[len(rel) * a + j],
                                                       recv_sem=recv.at[len(rel) * a + j], device_id=to, device_id_type=MESH))
    return local, remote


def _send_start(mode, xs, lands, after, name):
    if lands is None:
        lands = [lax.empty(t.shape if mode == "scatter" else (N_DEV,) + t.shape, t.dtype) for t in xs]
    nx, na, nr = len(xs), len(lands), len(SPLIT_RELATIONS[mode])
    nsem = 2 if mode == "forward" else 3

    def body(*refs):
        x_refs, land_refs = refs[:nx], refs[nx:nx + na]
        sems = refs[nx + na + 1:nx + na + 1 + nsem]
        local, remote = _split_copies(mode, x_refs, land_refs, sems[0], sems[1], sems[2] if nsem == 3 else None, False)
        for cp in remote + local:
            cp.start()
        refs[-1][...] = jnp.zeros_like(refs[-1])

    hbm = lambda t: pltpu.with_memory_space_constraint(t, pltpu.HBM)
    sem_shapes = [pltpu.SemaphoreType.DMA((nr * na,)), pltpu.SemaphoreType.DMA((nr * na,)), pltpu.SemaphoreType.DMA((na,))][:nsem]
    outs = pl.pallas_call(
        body, name=name,
        out_shape=(*sem_shapes, *[pltpu.HBM(t.shape, t.dtype) for t in list(xs) + list(lands)], _sds((8, 128), F32)),
        in_specs=[HBM] * (nx + na) + [ANY], out_specs=(*[SEM] * nsem, *[HBM] * (nx + na), pl.BlockSpec(memory_space=pltpu.VMEM)),
        input_output_aliases={i: nsem + i for i in range(nx + na)},
        compiler_params=pltpu.CompilerParams(has_side_effects=EFFECT))(*[hbm(t) for t in list(xs) + list(lands)], after)
    return outs[:nsem], list(outs[nsem:nsem + nx]), list(outs[nsem + nx:nsem + nx + na]), outs[-1]


def _send_wait(mode, started, after, name):
    sems, xs, lands, _ = started
    nx, na, nsem = len(xs), len(lands), len(sems)

    def body(*refs):
        s = refs[nx + na:nx + na + nsem]
        local, remote = _split_copies(mode, refs[:nx], refs[nx:nx + na], s[0], s[1], s[2] if nsem == 3 else None, True)
        for cp in remote:
            cp.wait_send()
            cp.wait_recv()
        for cp in local:
            cp.wait()

    outs = pl.pallas_call(body, name=name, out_shape=tuple(pltpu.HBM(t.shape, t.dtype) for t in xs + lands),
                          in_specs=[HBM] * (nx + na) + [SEM] * nsem + [ANY], out_specs=[HBM] * (nx + na),
                          input_output_aliases={i: i for i in range(nx + na)},
                          compiler_params=pltpu.CompilerParams(has_side_effects=EFFECT))(*xs, *lands, *sems, after)
    return list(outs[nx:])


def _local_step(x, meta, target, mix_weights_fn, ffn_weights_fn, prefetch_fn, grads_fn, sink, dec_f, dec_b, ret_norm, n_mix_pre, n_mix_post,
                n_ffn_pre, n_ffn_post):
    depth = n_mix_pre.shape[0]
    d = D_MODEL
    h = jnp.concatenate([jnp.zeros((PAD_FRONT, d), F32), meta, x], axis=0)
    n = h.shape[0]
    cos_a, sin_a, cos_r, sin_r, perm = _rope_tables(n)
    lg_all = jnp.stack([-jnp.exp(dec_f), -jnp.exp(dec_b)], axis=-1)
    saved = []
    u = _norm_fwd(h, n_mix_pre[0][None], None, BF16, "l0_norm_mix_pre")
    wi, wo = mix_weights_fn(0, h)
    for l in range(depth):
        t = f"l{l}_"
        sink_b = jnp.broadcast_to(sink[l][:, None], (ATT_HEADS, 128))
        proj = _mm_nt(u, wi, BF16, WIDE_TILE, d, t + "proj")
        aq = _rope_att(proj, 0, ATT_HEADS, cos_a, sin_a, perm, t + "rope_aq")
        ak = _rope_att(proj, ATT_WIDTH // KV_WIDTH, ATT_KV_HEADS, cos_a, sin_a, perm, t + "rope_ak")
        mixed = _att_fwd(aq, ak, proj, sink_b, t + "att")
        proj3 = proj[None]
        rq = _rope_ret(proj3, COL_RQ // 2, cos_r, sin_r, t + "rope_rq")
        rk = _rope_ret(proj3, COL_RK // 2, cos_r, sin_r, t + "rope_rk")
        o_ret, states = _ret_fwd(rq, rk, proj, lg_all[l], t + "ret")
        prefetch_fn(2 * l + 1, o_ret)
        mixed = _retgate_fwd(o_ret, proj, ret_norm[l][None], mixed, t + "retgate")
        wg, wu, wd = ffn_weights_fn(l, mixed)
        mo, h_mid, u2 = _out_proj_norms(mixed, wo, h, n_mix_post[l][None], n_ffn_pre[l][None], t + "out_proj")
        gate, up, f = _swiglu_fwd(u2, wg, wu, t + "gate_up")
        prefetch_fn(2 * l + 2, f)
        dn = _mm_nn(f, wd, F32, 512, D_FF, t + "down", tm_cap=DEEP_K_TOKEN_TILE)
        saved.append(dict(h=h, u=u, proj=proj, aq=aq, ak=ak, rq=rq, rk=rk, o_ret=o_ret, states=states, mixed=mixed, mo=mo, h_mid=h_mid, u2=u2,
                          gate=gate, up=up, f=f, dn=dn, sink_b=sink_b, wi=wi, wo=wo, wg=wg, wu=wu, wd=wd))
        if l + 1 < depth:
            wi, wo = mix_weights_fn(l + 1, dn)
            h, u = _norm_res_norm(dn, n_ffn_post[l][None], h_mid, n_mix_pre[l + 1][None], t + "norm_ffn_post")
        else:
            h = _norm_fwd(dn, n_ffn_post[l][None], h_mid, F32, t + "norm_ffn_post")

    dh, loss_part = _loss_head(h, target, "loss_head")
    gs = dict(sink=[None] * depth, dec_f=[None] * depth, dec_b=[None] * depth, ret_norm=[None] * depth, mix_pre=[None] * depth,
              mix_post=[None] * depth, ffn_pre=[None] * depth, ffn_post=[None] * depth)
    d_dn, gs["ffn_post"][depth - 1] = _norm_bwd(dh, saved[-1]["dn"], n_ffn_post[depth - 1][None], None, BF16, f"l{depth - 1}_b_norm_ffn_post")
    for l in reversed(range(depth)):
        t = f"l{l}_b_"
        sv = saved[l]
        proj = sv["proj"]
        gw = {}
        d_gate, d_up = _swiglu_bwd(d_dn, sv["wd"], sv["gate"], sv["up"], t + "d_gate_up")
        gw["wd"] = _mm_tn(sv["f"], d_dn, WIDE_TILE, 1024, t + "dw_down")
        du2 = _mm_nn(d_gate, sv["wg"], F32, 512, D_FF, t + "du2_gate", tm_cap=DEEP_K_TOKEN_TILE)
        du2 = _mm_nn(d_up, sv["wu"], F32, 512, D_FF, t + "du2_up", acc=du2, tm_cap=DEEP_K_TOKEN_TILE)
        gw["wg"] = _mm_tn(d_gate, sv["u2"], WIDE_TILE, 1024, t + "dw_gate")
        gw["wu"] = _mm_tn(d_up, sv["u2"], WIDE_TILE, 1024, t + "dw_up")
        tok_b = grads_fn(l, "ffn", gw, du2)
        dh, d_mo, gs["ffn_pre"][l], gs["mix_post"][l] = _norm_bwd_pair(du2, sv["h_mid"], (n_ffn_pre[l] + tok_b)[None], dh, sv["mo"],
                                                                         n_mix_post[l][None], t + "norm_ffn_pre")
        d_mixed = _mm_nt(d_mo, sv["wo"], BF16, 1024, d, t + "d_mixed")
        gw["wo"] = _mm_tn(sv["mixed"], d_mo, 1024, 1024, t + "dw_out")
        d_o, dproj, gs["ret_norm"][l] = _retgate_bwd(d_mixed, sv["o_ret"], proj, ret_norm[l][None], t + "retgate")
        dq_r, dk_r, dv_r, dlg = _ret_bwd(sv["rq"], sv["rk"], proj, lg_all[l], d_o, sv["states"], t + "ret")
        draw = dlg[:, :, 0, 0] * lg_all[l]
        gs["dec_f"][l], gs["dec_b"][l] = draw[:, 0], draw[:, 1]
        dproj = _rope_ret(dq_r, 0, cos_r, -sin_r, t + "rope_rq", into=dproj, out_col=COL_RQ // 2)
        dproj = _rope_ret(dk_r, 0, cos_r, -sin_r, t + "rope_rk", into=dproj, out_col=COL_RK // 2)
        dproj = _rope_ret(dv_r, 0, None, None, t + "sum_rv", into=dproj, out_col=COL_RV // 2)
        dq_a, dk_a, dv_a, dsink = _att_bwd(sv["aq"], sv["ak"], proj, sv["sink_b"], d_mixed, t + "att")
        gs["sink"][l] = dsink[:, 0]
        dproj = _rope_att(dq_a, 0, ATT_HEADS, cos_a, -sin_a, perm, t + "rope_aq", into=dproj, out_col=0)
        dproj = _rope_att(dk_a, 0, ATT_KV_HEADS, cos_a, -sin_a, perm, t + "rope_ak", plain=dv_a, into=dproj, out_col=ATT_WIDTH // (2 * KV_WIDTH))
        gw["wi"] = _mm_tn(dproj, sv["u"], WIDE_TILE, 1024, t + "dw_in")
        tok_b = grads_fn(l, "mix", gw, dproj)
        du = _mm_nn(dproj, sv["wi"], F32, 512, IN_COLS, t + "du", tm_cap=DEEP_K_TOKEN_TILE)
        if l > 0:
            dh, d_dn, gs["mix_pre"][l], gs["ffn_post"][l - 1] = _norm_bwd_pair(du, sv["h"], (n_mix_pre[l] + tok_b)[None], dh, saved[l - 1]["dn"],
                                                                              n_ffn_post[l - 1][None], t + "norm_mix_pre")
        else:
            dh, gs["mix_pre"][l] = _norm_bwd(du, sv["h"], (n_mix_pre[l] + tok_b)[None], dh, F32, t + "norm_mix_pre")
    return loss_part[0, 0], dh, gs


def _pack_small(mix_pre, mix_post, ffn_pre, ffn_post, ret_norm, sink, dec_f, dec_b, loss, meta):
    d = D_MODEL

    def tile(a, rows=8):
        a = jnp.reshape(a, (-1, a.shape[-1])) if a.ndim else jnp.reshape(a, (1, 1))
        return jnp.pad(a, ((0, rows - a.shape[0]), (0, d - a.shape[1])))

    return jnp.concatenate([tile(mix_pre), tile(mix_post), tile(ffn_pre), tile(ffn_post), tile(ret_norm.reshape(-1, d)), tile(sink), tile(dec_f),
                            tile(dec_b), tile(loss), tile(meta, SMALL_ROWS - ROW_META)], axis=0)


def _unpack_small(p, depth):
    rows = lambda r0, cols: p[r0:r0 + depth, :cols]
    return dict(mix_pre=rows(ROW_MIX_PRE, D_MODEL), mix_post=rows(ROW_MIX_POST, D_MODEL), ffn_pre=rows(ROW_FFN_PRE, D_MODEL),
                ffn_post=rows(ROW_FFN_POST, D_MODEL), ret_norm=p[ROW_RET_NORM:ROW_RET_NORM + depth * RET_WIDTH // D_MODEL].reshape(depth, RET_WIDTH),
                sink=rows(ROW_SINK, ATT_HEADS), dec_f=rows(ROW_DEC_F, RET_HEADS), dec_b=rows(ROW_DEC_B, RET_HEADS), loss=p[ROW_LOSS, 0])


def kernel(x, meta_tokens, w_in, w_out, attn_sink, ret_decay_fwd, ret_decay_bwd, ret_norm, norm_mix_pre, norm_mix_post, w_gate, w_up, w_down, norm_ffn_pre, norm_ffn_post, loss_target, m_meta_tokens, m_w_in, m_w_out, m_attn_sink, m_ret_decay_fwd, m_ret_decay_bwd, m_ret_norm, m_norm_mix_pre, m_norm_mix_post, m_w_gate, m_w_up, m_w_down, m_norm_ffn_pre, m_norm_ffn_post, v_meta_tokens, v_w_in, v_w_out, v_attn_sink, v_ret_decay_fwd, v_ret_decay_bwd, v_ret_norm, v_norm_mix_pre, v_norm_mix_post, v_w_gate, v_w_up, v_w_down, v_norm_ffn_pre, v_norm_ffn_post):
    depth, d = w_in.shape[0], D_MODEL
    me = 4 * lax.axis_index("x") + 2 * lax.axis_index("y") + lax.axis_index("c")
    zero = jnp.zeros((), F32)

    meta_g, = _allgather([meta_tokens], "gather_meta")
    meta = meta_g.transpose(1, 0, 2).reshape(N_META, d)

    def shards(k):
        l = k // 2
        if k % 2 == 0:
            return [w_in[l].T.astype(BF16), w_out[l].astype(BF16)]
        return [w_gate[l].T.astype(BF16), w_up[l].T.astype(BF16), w_down[l].astype(BF16)]

    gathers, ahead = {}, 2
    for k in range(min(ahead + 1, 2 * depth)):
        gathers[k] = _send_start("gather", shards(k), None, gathers[k - 1][3] if k else meta_g, f"gather_start_g{k}")

    passing = {}

    def pass_on(k, after):
        lands = _send_wait("gather", gathers.pop(k), after, f"gather_wait_g{k}")
        passing[k] = _send_start("forward", [], lands, after, f"forward_start_g{k}")

    def take(k, h):
        after = h
        if k >= 1 and k + ahead < 2 * depth:
            gathers[k + ahead] = _send_start("gather", shards(k + ahead), None, h, f"gather_start_g{k + ahead}")
            after = gathers[k + ahead][3]
        elif k == 0:
            after = gathers[max(gathers)][3]
        if k not in passing:
            pass_on(k, after)
        return _send_wait("forward", passing.pop(k), after, f"forward_wait_g{k}")

    def prefetch_fn(k, after):
        if k in gathers and k not in passing:
            pass_on(k, after)

    def mix_weights_fn(l, h):
        wi_t, wo = take(2 * l, h)
        return wi_t.reshape(IN_COLS, d), wo.reshape(d, d)

    def ffn_weights_fn(l, h):
        wg_t, wu_t, wd = take(2 * l + 1, h)
        return wg_t.reshape(D_FF, d), wu_t.reshape(D_FF, d), wd.reshape(D_FF, d)

    exchanges, adam, order = {}, {}, []
    tr = lambda *ts: tuple(jnp.swapaxes(t, 1, 2) for t in ts)
    big = dict(wi=tr(w_in, m_w_in, v_w_in), wg=tr(w_gate, m_w_gate, v_w_gate), wu=tr(w_up, m_w_up, v_w_up), wd=(w_down, m_w_down, v_w_down),
               wo=(w_out, m_w_out, v_w_out))
    kinds = dict(ffn=("wg", "wu", "wd"), mix=("wi", "wo"))

    arrivals = []

    def finish(key, after):
        l, part = key
        arrivals.append((key, _send_wait("scatter", exchanges.pop(key), after, f"exchange_wait_{part}_l{l}")))

    def update():
        last_start = exchanges[order[-1]][3] if order[-1] in exchanges else None
        while arrivals:
            (l, part), arrived = arrivals.pop(0)
            for kind, parts in zip(kinds[part], arrived):
                adam[kind] = _adamw(parts, *big[kind], f"adamw_{kind}_l{l}", layer=l, prev=adam.get(kind),
                                    after=None if kind in adam else last_start)

    def grads_fn(l, part, gw, after):
        packed = [gw[kind].reshape(N_DEV, -1, d) for kind in kinds[part]]
        exchanges[(l, part)] = _send_start("scatter", packed, None, after, f"exchange_start_{part}_l{l}")
        order.append((l, part))
        token = exchanges[(l, part)][3]
        if len(order) > 2:
            finish(order[-3], token)
        return token[0, 0]

    loss_part, dh, gs = _local_step(x[0], meta, loss_target[0], mix_weights_fn, ffn_weights_fn, prefetch_fn, grads_fn, attn_sink, ret_decay_fwd, ret_decay_bwd, ret_norm,
                                    norm_mix_pre, norm_mix_post, norm_ffn_pre, norm_ffn_post)
    grad_x = dh[BLOCK:][None]

    st = lambda xs: jnp.stack([t.reshape(-1) if t.ndim == 1 else t[0] for t in xs])
    small = _pack_small(st(gs["mix_pre"]), st(gs["mix_post"]), st(gs["ffn_pre"]), st(gs["ffn_post"]), st(gs["ret_norm"]), st(gs["sink"]),
                        st(gs["dec_f"]), st(gs["dec_b"]), loss_part, dh[PAD_FRONT:BLOCK])
    update()
    small_g, = _allgather([small], "gather_small", after=adam["wo"][0])
    for key in order[-2:]:
        finish(key, small_g)
    update()
    o_wi, o_wo, o_wg, o_wu, o_wd = tr(*adam["wi"]), adam["wo"], tr(*adam["wg"]), tr(*adam["wu"]), adam["wd"]
    zmeta = jnp.zeros((N_META, d), F32)
    packs = [_pack_small(a[0], a[1], a[2], a[3], a[4], a[5], a[6], a[7], zero, zmeta) for a in (
        (norm_mix_pre, norm_mix_post, norm_ffn_pre, norm_ffn_post, ret_norm, attn_sink, ret_decay_fwd, ret_decay_bwd),
        (m_norm_mix_pre, m_norm_mix_post, m_norm_ffn_pre, m_norm_ffn_post, m_ret_norm, m_attn_sink, m_ret_decay_fwd, m_ret_decay_bwd),
        (v_norm_mix_pre, v_norm_mix_post, v_norm_ffn_pre, v_norm_ffn_post, v_ret_norm, v_attn_sink, v_ret_decay_fwd, v_ret_decay_bwd))]
    o_small = [_unpack_small(o, depth) for o in _adamw(small_g, packs[0], packs[1], packs[2], "adamw_small")]
    meta_parts = lax.dynamic_slice(small_g, (0, ROW_META, me * (d // N_DEV)), (N_DEV, N_META, d // N_DEV))
    o_meta = _adamw(meta_parts, meta_tokens, m_meta_tokens, v_meta_tokens, "adamw_meta")

    outs = []
    for i in range(4):
        s = o_small[i]
        outs += [o_meta[i], o_wi[i], o_wo[i], s["sink"], s["dec_f"], s["dec_b"], s["ret_norm"], s["mix_pre"], s["mix_post"], o_wg[i], o_wu[i],
                 o_wd[i], s["ffn_pre"], s["ffn_post"]]
    return (o_small[0]["loss"], grad_x, *outs)
```

```python
import jax
import jax.numpy as jnp
import numpy as np
from jax import lax
from jax.experimental import pallas as pl
from jax.experimental.pallas import tpu as pltpu

F32, BF16 = jnp.float32, jnp.bfloat16

D_MODEL = 2048
N_META = 16
BLOCK = 128
WINDOW = 128
PAD_FRONT = BLOCK - N_META
ATT_HEAD_DIM = 128
ATT_WIDTH = D_MODEL // 2
ATT_HEADS = ATT_WIDTH // ATT_HEAD_DIM
ATT_KV_HEADS = 2
ATT_GROUP = ATT_HEADS // ATT_KV_HEADS
KV_WIDTH = ATT_KV_HEADS * ATT_HEAD_DIM
ROT_DIM = ATT_HEAD_DIM // 4
ROPE_THETA = 500000.0
RET_WIDTH = D_MODEL - ATT_WIDTH
RET_HEAD_DIM = 256
RET_HEADS = RET_WIDTH // RET_HEAD_DIM
RET_THETA = 10000.0
D_FF = 5632
IN_COLS = ATT_WIDTH + 2 * KV_WIDTH + 4 * RET_WIDTH
N_DEV = 8
EPS = 1e-6
NEG = -1e30
RET_K_SCALE = RET_HEAD_DIM ** -0.5
ATT_SCALE = ATT_HEAD_DIM ** -0.5

COL_AV256 = (ATT_WIDTH + KV_WIDTH) // 256
COL_RQ = (ATT_WIDTH + 2 * KV_WIDTH) // RET_HEAD_DIM
COL_RK = COL_RQ + RET_HEADS
COL_RV = COL_RK + RET_HEADS
COL_RG = COL_RV + RET_HEADS

ADAM_LR, ADAM_B1, ADAM_B2, ADAM_EPS, ADAM_WD, ADAM_STEP = 0.001, 0.9, 0.999, 1e-08, 0.01, 10

ROW_MIX_PRE, ROW_MIX_POST, ROW_FFN_PRE, ROW_FFN_POST, ROW_RET_NORM, ROW_SINK, ROW_DEC_F, ROW_DEC_B, ROW_LOSS, ROW_META, SMALL_ROWS = (
    0, 8, 16, 24, 32, 40, 48, 56, 64, 72, 96)
ADAMW_TILE_ELEMS = 128 * 1024

MESH = pl.DeviceIdType.MESH
ANY = pl.BlockSpec(memory_space=pl.ANY)


def _row_tile(n, cap):
    for t in range(cap - cap % 16, 0, -16):
        if n % t == 0:
            return t
    raise ValueError(n)


def _sds(shape, dtype):
    return jax.ShapeDtypeStruct(shape, dtype)


def _silu(x):
    return x * jax.nn.sigmoid(x)


def _dsilu(x):
    s = jax.nn.sigmoid(x)
    return s * (1.0 + x * (1.0 - s))


def _norm_fwd(x, g, res, out_dtype, name):
    n, d = x.shape
    tr = _row_tile(n, 384)

    def body(*refs):
        if res is None:
            x_ref, g_ref, o_ref = refs
        else:
            x_ref, g_ref, r_ref, o_ref = refs
        xv = x_ref[...]
        r = lax.rsqrt(jnp.mean(xv * xv, axis=-1, keepdims=True) + EPS)
        y = xv * r * g_ref[...]
        if res is not None:
            y = y + r_ref[...]
        o_ref[...] = y.astype(o_ref.dtype)

    row = pl.BlockSpec((tr, d), lambda i: (i, 0))
    ins = [row, pl.BlockSpec((1, d), lambda i: (0, 0))] + ([row] if res is not None else [])
    args = (x, g) + ((res,) if res is not None else ())
    return pl.pallas_call(body, grid=(n // tr,), in_specs=ins, out_specs=row, out_shape=_sds((n, d), out_dtype), name=name)(*args)


def _norm_res_norm(x, g, res, g_next, name):
    n, d = x.shape
    tr = _row_tile(n, 384)

    def body(x_ref, g_ref, r_ref, gn_ref, h_ref, u_ref):
        xv = x_ref[...]
        hv = r_ref[...] + xv * lax.rsqrt(jnp.mean(xv * xv, axis=-1, keepdims=True) + EPS) * g_ref[...]
        h_ref[...] = hv
        u_ref[...] = (hv * lax.rsqrt(jnp.mean(hv * hv, axis=-1, keepdims=True) + EPS) * gn_ref[...]).astype(u_ref.dtype)

    row = pl.BlockSpec((tr, d), lambda i: (i, 0))
    vec = pl.BlockSpec((1, d), lambda i: (0, 0))
    return pl.pallas_call(body, grid=(n // tr,), in_specs=[row, vec, row, vec], out_specs=(row, row),
                          out_shape=(_sds((n, d), F32), _sds((n, d), BF16)), name=name)(x, g, res, g_next)


OUT_PROJ_TOKEN_TILE = 192


def _out_proj_norms(x, w, res, g, g_next, name):
    n, k = x.shape
    d = w.shape[1]
    tm = _row_tile(n, OUT_PROJ_TOKEN_TILE)

    def body(x_ref, w_ref, r_ref, g_ref, gn_ref, mo_ref, h_ref, u_ref):
        mo = jnp.dot(x_ref[...], w_ref[...], preferred_element_type=F32)
        mo_ref[...] = mo
        hv = r_ref[...] + mo * lax.rsqrt(jnp.mean(mo * mo, axis=-1, keepdims=True) + EPS) * g_ref[...]
        h_ref[...] = hv
        u_ref[...] = (hv * lax.rsqrt(jnp.mean(hv * hv, axis=-1, keepdims=True) + EPS) * gn_ref[...]).astype(u_ref.dtype)

    row = pl.BlockSpec((tm, d), lambda i: (i, 0))
    vec = pl.BlockSpec((1, d), lambda i: (0, 0))
    return pl.pallas_call(body, grid=(n // tm,), in_specs=[pl.BlockSpec((tm, k), lambda i: (i, 0)), pl.BlockSpec((k, d), lambda i: (0, 0)), row, vec, vec],
                          out_specs=(row, row, row), out_shape=(_sds((n, d), F32), _sds((n, d), F32), _sds((n, d), BF16)), name=name)(
                              x, w, res, g, g_next)


def _norm_bwd(dy, x, g, res, out_dtype, name):
    n, d = x.shape
    tr = _row_tile(n, 384)

    def body(*refs):
        if res is None:
            dy_ref, x_ref, g_ref, dx_ref, dg_ref = refs
        else:
            dy_ref, x_ref, g_ref, r_ref, dx_ref, dg_ref = refs
        i = pl.program_id(0)
        xv = x_ref[...]
        r = lax.rsqrt(jnp.mean(xv * xv, axis=-1, keepdims=True) + EPS)
        xhat = xv * r
        dyf = dy_ref[...].astype(F32)
        gdy = dyf * g_ref[...]
        dx = r * (gdy - xhat * jnp.mean(gdy * xhat, axis=-1, keepdims=True))
        if res is not None:
            dx = dx + r_ref[...]
        dx_ref[...] = dx.astype(dx_ref.dtype)

        @pl.when(i == 0)
        def _():
            dg_ref[...] = jnp.zeros_like(dg_ref)

        dg_ref[...] += jnp.sum(dyf * xhat, axis=0, keepdims=True)

    row = pl.BlockSpec((tr, d), lambda i: (i, 0))
    vec = pl.BlockSpec((1, d), lambda i: (0, 0))
    ins = [row, row, vec] + ([row] if res is not None else [])
    args = (dy, x, g) + ((res,) if res is not None else ())
    return pl.pallas_call(body, grid=(n // tr,), in_specs=ins, out_specs=(row, vec),
                          out_shape=(_sds((n, d), out_dtype), _sds((1, d), F32)), name=name,
                          compiler_params=pltpu.CompilerParams(dimension_semantics=("arbitrary",)))(*args)


def _norm_bwd_pair(dy, x1, g1, res, x2, g2, name):
    n, d = x1.shape
    tr = _row_tile(n, 192)

    def rms_bwd(dyf, xv, g):
        r = lax.rsqrt(jnp.mean(xv * xv, axis=-1, keepdims=True) + EPS)
        xhat = xv * r
        gdy = dyf * g
        return r * (gdy - xhat * jnp.mean(gdy * xhat, axis=-1, keepdims=True)), jnp.sum(dyf * xhat, axis=0, keepdims=True)

    def body(dy_ref, x1_ref, g1_ref, r_ref, x2_ref, g2_ref, dh_ref, d2_ref, dg1_ref, dg2_ref):
        @pl.when(pl.program_id(0) == 0)
        def _():
            dg1_ref[...] = jnp.zeros_like(dg1_ref)
            dg2_ref[...] = jnp.zeros_like(dg2_ref)

        dx1, s1 = rms_bwd(dy_ref[...].astype(F32), x1_ref[...], g1_ref[...])
        dh = dx1 + r_ref[...]
        dh_ref[...] = dh
        dx2, s2 = rms_bwd(dh, x2_ref[...], g2_ref[...])
        d2_ref[...] = dx2.astype(d2_ref.dtype)
        dg1_ref[...] += s1
        dg2_ref[...] += s2

    row = pl.BlockSpec((tr, d), lambda i: (i, 0))
    vec = pl.BlockSpec((1, d), lambda i: (0, 0))
    return pl.pallas_call(body, grid=(n // tr,), in_specs=[row, row, vec, row, row, vec], out_specs=(row, row, vec, vec),
                          out_shape=(_sds((n, d), F32), _sds((n, d), BF16), _sds((1, d), F32), _sds((1, d), F32)), name=name,
                          compiler_params=pltpu.CompilerParams(dimension_semantics=("arbitrary",)))(dy, x1, g1, res, x2, g2)


def _mm(a, b, *, ta, tb, grid, a_blk, a_map, b_blk, b_map, o_blk, o_map, o_shape, o_dtype, name, acc=None):
    nk = grid[2]
    dims = (((0,) if ta else (1,), (1,) if tb else (0,)), ((), ()))

    def body(*refs):
        if acc is None:
            a_ref, b_ref, o_ref = refs[:3]
            c_ref = None
        else:
            a_ref, b_ref, c_ref, o_ref = refs[:4]
        part = lax.dot_general(a_ref[...], b_ref[...], dims, preferred_element_type=F32)
        if nk == 1:
            if c_ref is not None:
                part = part + c_ref[...].astype(F32)
            o_ref[...] = part.astype(o_ref.dtype)
            return
        acc_ref = refs[-1]
        k = pl.program_id(2)

        @pl.when(k == 0)
        def _():
            acc_ref[...] = jnp.zeros_like(acc_ref) if c_ref is None else c_ref[...].astype(F32)

        acc_ref[...] += part

        @pl.when(k == nk - 1)
        def _():
            o_ref[...] = acc_ref[...].astype(o_ref.dtype)

    ins = [pl.BlockSpec(a_blk, a_map), pl.BlockSpec(b_blk, b_map)]
    args = [a, b]
    if acc is not None:
        ins.append(pl.BlockSpec(o_blk, o_map))
        args.append(acc)
    return pl.pallas_call(body, grid=grid, in_specs=ins, out_specs=pl.BlockSpec(o_blk, o_map), out_shape=_sds(o_shape, o_dtype),
                          scratch_shapes=[pltpu.VMEM(o_blk, F32)] if nk > 1 else [], name=name,
                          compiler_params=pltpu.CompilerParams(dimension_semantics=("parallel", "parallel", "arbitrary")))(*args)


TOKEN_TILE = 1056
WIDE_TILE = 1408
DEEP_K_TOKEN_TILE = 528


def _mm_nn(x, w, o_dtype, tn, tk, name, acc=None, tm_cap=TOKEN_TILE):
    n, k = x.shape
    tm = _row_tile(n, tm_cap)
    return _mm(x, w, ta=False, tb=False, grid=(n // tm, w.shape[1] // tn, k // tk), a_blk=(tm, tk), a_map=lambda i, j, kk: (i, kk),
               b_blk=(tk, tn), b_map=lambda i, j, kk: (kk, j), o_blk=(tm, tn), o_map=lambda i, j, kk: (i, j),
               o_shape=(n, w.shape[1]), o_dtype=o_dtype, name=name, acc=acc)


def _mm_nt(dy, w, o_dtype, tn, tk, name, acc=None):
    n, k = dy.shape
    tm = _row_tile(n, TOKEN_TILE)
    return _mm(dy, w, ta=False, tb=True, grid=(n // tm, w.shape[0] // tn, k // tk), a_blk=(tm, tk), a_map=lambda i, j, kk: (i, kk),
               b_blk=(tn, tk), b_map=lambda i, j, kk: (j, kk), o_blk=(tm, tn), o_map=lambda i, j, kk: (i, j),
               o_shape=(n, w.shape[0]), o_dtype=o_dtype, name=name, acc=acc)


def _mm_tn(x, dy, tm, tn, name):
    n, m = x.shape
    tk = _row_tile(n, 2 * TOKEN_TILE)
    return _mm(x, dy, ta=True, tb=False, grid=(m // tm, dy.shape[1] // tn, n // tk), a_blk=(tk, tm), a_map=lambda i, j, kk: (kk, i),
               b_blk=(tk, tn), b_map=lambda i, j, kk: (kk, j), o_blk=(tm, tn), o_map=lambda i, j, kk: (i, j),
               o_shape=(m, dy.shape[1]), o_dtype=BF16, name=name)


def _rope_tables(n):
    pos = (jnp.arange(n) - PAD_FRONT).astype(F32)
    half = ROT_DIM // 2
    ang = pos[:, None] * (ROPE_THETA ** (-jnp.arange(half, dtype=F32) / half))[None, :]
    c, s = jnp.cos(ang), jnp.sin(ang)
    rest = ATT_HEAD_DIM - ROT_DIM
    cos_a = jnp.concatenate([c, c, jnp.ones((n, rest), F32)], axis=1)
    sin_a = jnp.concatenate([-s, s, jnp.zeros((n, rest), F32)], axis=1)
    half = RET_HEAD_DIM // 2
    ang = pos[:, None] * (RET_THETA ** (-jnp.arange(half, dtype=F32) / half))[None, :]
    c, s = jnp.cos(ang), jnp.sin(ang)
    perm = np.zeros((ATT_HEAD_DIM, ATT_HEAD_DIM), np.float32)
    for i in range(ROT_DIM):
        perm[(i + ROT_DIM // 2) % ROT_DIM, i] = 1.0
    return cos_a, sin_a, jnp.concatenate([c, c], axis=1), jnp.concatenate([-s, s], axis=1), jnp.asarray(perm, BF16)


def _into(buf, own_shape):
    if buf is None:
        return _sds(own_shape, BF16), [], [], lambda n_inputs: {}
    return _sds(buf.shape, buf.dtype), [ANY], [buf], lambda n_inputs: {n_inputs: 0}


def _rope_att(x, col0, heads, cos, sin, perm, name, plain=None, into=None, out_col=0):
    n = x.shape[0]
    tr = _row_tile(n, 1056)
    hd = ATT_HEAD_DIM
    w2 = 0 if plain is None else plain.shape[1]
    width = heads * hd + w2

    def body(*refs):
        x_ref, c_ref, s_ref, p_ref, o_ref = refs[0], refs[1], refs[2], refs[3], refs[-1]
        for h in range(heads):
            cs = slice(h * hd, (h + 1) * hd)
            xb = x_ref[:, cs].astype(BF16)
            sw = jnp.dot(xb, p_ref[...], preferred_element_type=F32)
            o_ref[:, cs] = (xb.astype(F32) * c_ref[...] + sw * s_ref[...]).astype(o_ref.dtype)
        if plain is not None:
            o_ref[:, heads * hd:] = refs[4][...].astype(o_ref.dtype)

    tab = pl.BlockSpec((tr, hd), lambda i: (i, 0))
    ins = [pl.BlockSpec((tr, heads * hd), lambda i: (i, col0)), tab, tab, pl.BlockSpec((hd, hd), lambda i: (0, 0))]
    args = [x, cos, sin, perm]
    if plain is not None:
        ins.append(pl.BlockSpec((tr, w2), lambda i: (i, 0)))
        args.append(plain)
    shape, extra_specs, extra_args, alias = _into(into, (n, width))
    return pl.pallas_call(body, grid=(n // tr,), in_specs=ins + extra_specs, out_specs=pl.BlockSpec((tr, width), lambda i: (i, out_col)),
                          out_shape=shape, input_output_aliases=alias(len(ins)), name=name)(*args, *extra_args)


def _rope_ret(x, col0, cos, sin, name, into=None, out_col=0):
    p, n, _ = x.shape
    tr = _row_tile(n, 1056)
    hd = RET_HEAD_DIM

    def body(*refs):
        x_ref, o_ref = refs[0], refs[-1]
        for h in range(2):
            cs = slice(h * hd, (h + 1) * hd)
            xv = x_ref[0, :, cs].astype(F32)
            for q in range(1, p):
                xv = xv + x_ref[q, :, cs].astype(F32)
            if cos is not None:
                sw = jnp.concatenate([xv[:, hd // 2:], xv[:, :hd // 2]], axis=1)
                xv = xv * refs[1][...] + sw * refs[2][...]
            o_ref[:, cs] = xv.astype(o_ref.dtype)

    tab = pl.BlockSpec((tr, hd), lambda i, j: (i, 0))
    ins = [pl.BlockSpec((p, tr, 2 * hd), lambda i, j: (0, i, col0 + j))] + ([tab, tab] if cos is not None else [])
    args = (x,) + ((cos, sin) if cos is not None else ())
    shape, extra_specs, extra_args, alias = _into(into, (n, RET_WIDTH))
    return pl.pallas_call(body, grid=(n // tr, RET_HEADS // 2), in_specs=ins + extra_specs,
                          out_specs=pl.BlockSpec((tr, 2 * hd), lambda i, j: (i, out_col + j)), out_shape=shape,
                          input_output_aliases=alias(len(ins)), name=name)(*args, *extra_args)


def _att_mask(nblk, n_tot):
    row = lax.broadcasted_iota(jnp.int32, (BLOCK, 4 * BLOCK), 0)
    col = lax.broadcasted_iota(jnp.int32, (BLOCK, 4 * BLOCK), 1)
    qi = nblk * BLOCK + row
    seg = col // BLOCK
    cj = col % BLOCK
    kj = (nblk - 1 + seg) * BLOCK + cj
    band = (jnp.abs(qi - kj) <= WINDOW) & (kj >= PAD_FRONT) & (kj < n_tot) & (seg < 3)
    meta = (seg == 3) & (cj >= PAD_FRONT) & (jnp.abs(qi - cj) > WINDOW)
    return band | meta


def _att_specs(nb, v_col):
    kv = lambda f, cb: pl.BlockSpec((BLOCK, KV_WIDTH), lambda n: (f(n), cb))
    prev, own, nxt, first = (lambda n: jnp.maximum(n - 1, 0)), (lambda n: n), (lambda n: jnp.minimum(n + 1, nb - 1)), (lambda n: 0)
    return [kv(f, 0) for f in (prev, own, nxt, first)] + [kv(f, v_col) for f in (prev, own, nxt, first)]


def _att_probs(s, ok, snk):
    s = jnp.where(ok, s, NEG)
    m = jnp.maximum(jnp.max(s, axis=-1, keepdims=True), snk)
    p = jnp.exp(s - m)
    ps = jnp.exp(snk - m)
    inv = 1.0 / (jnp.sum(p, axis=-1, keepdims=True) + ps)
    return p * inv, ps * inv


def _att_fwd(q, k, proj, sink_b, name):
    n = q.shape[0]
    nb = n // BLOCK
    hd = ATT_HEAD_DIM

    def body(q_ref, kp, ko, kn, km, vp, vo, vn, vm, sink_ref, o_ref):
        nblk = pl.program_id(0)
        ok = _att_mask(nblk, n)
        keep = (nblk * BLOCK + lax.broadcasted_iota(jnp.int32, (BLOCK, 1), 0)) >= PAD_FRONT
        for kh in range(ATT_KV_HEADS):
            cs = slice(kh * hd, (kh + 1) * hd)
            kk = jnp.concatenate([r[:, cs] for r in (kp, ko, kn, km)], axis=0)
            vv = jnp.concatenate([r[:, cs] for r in (vp, vo, vn, vm)], axis=0)
            heads = [kh * ATT_GROUP + g for g in range(ATT_GROUP)]
            q4 = jnp.concatenate([q_ref[:, h * hd:(h + 1) * hd] for h in heads], axis=0)
            s = lax.dot_general(q4, kk, (((1,), (1,)), ((), ())), preferred_element_type=F32) * ATT_SCALE
            ps = []
            for g, h in enumerate(heads):
                p, _ = _att_probs(s[g * BLOCK:(g + 1) * BLOCK], ok, sink_ref[h:h + 1, 0:1])
                ps.append(p)
            o = jnp.dot(jnp.concatenate(ps, axis=0).astype(BF16), vv, preferred_element_type=F32)
            for g, h in enumerate(heads):
                o_ref[:, h * hd:(h + 1) * hd] = jnp.where(keep, o[g * BLOCK:(g + 1) * BLOCK], 0.0).astype(o_ref.dtype)

    qspec = pl.BlockSpec((BLOCK, ATT_WIDTH), lambda i: (i, 0))
    return pl.pallas_call(body, grid=(nb,), in_specs=[qspec] + _att_specs(nb, COL_AV256) + [pl.BlockSpec((ATT_HEADS, 128), lambda i: (0, 0))],
                          out_specs=qspec, out_shape=_sds((n, D_MODEL), BF16), name=name)(q, k, k, k, k, proj, proj, proj, proj, sink_b)


def _att_bwd(q, k, proj, sink_b, dmixed, name):
    n = q.shape[0]
    nb = n // BLOCK
    hd = ATT_HEAD_DIM

    def body(q_ref, kp, ko, kn, km, vp, vo, vn, vm, sink_ref, do_ref, dq_ref, dk_ref, dv_ref, dsink_ref):
        nblk = pl.program_id(0)

        @pl.when(nblk == 0)
        def _():
            dk_ref[...] = jnp.zeros_like(dk_ref)
            dv_ref[...] = jnp.zeros_like(dv_ref)
            dsink_ref[...] = jnp.zeros_like(dsink_ref)

        ok = _att_mask(nblk, n)
        rows = [jnp.maximum(nblk - 1, 0), nblk, jnp.minimum(nblk + 1, nb - 1), 0]
        for kh in range(ATT_KV_HEADS):
            cs = slice(kh * hd, (kh + 1) * hd)
            kk = jnp.concatenate([r[:, cs] for r in (kp, ko, kn, km)], axis=0)
            vv = jnp.concatenate([r[:, cs] for r in (vp, vo, vn, vm)], axis=0)
            heads = [kh * ATT_GROUP + g for g in range(ATT_GROUP)]
            q4 = jnp.concatenate([q_ref[:, h * hd:(h + 1) * hd] for h in heads], axis=0)
            do4 = jnp.concatenate([do_ref[:, h * hd:(h + 1) * hd] for h in heads], axis=0)
            s = lax.dot_general(q4, kk, (((1,), (1,)), ((), ())), preferred_element_type=F32) * ATT_SCALE
            dp = lax.dot_general(do4, vv, (((1,), (1,)), ((), ())), preferred_element_type=F32)
            ps, dss = [], []
            for g, h in enumerate(heads):
                p, psink = _att_probs(s[g * BLOCK:(g + 1) * BLOCK], ok, sink_ref[h:h + 1, 0:1])
                dpg = dp[g * BLOCK:(g + 1) * BLOCK]
                delta = jnp.sum(p * dpg, axis=-1, keepdims=True)
                ps.append(p)
                dss.append(p * (dpg - delta) * ATT_SCALE)
                dsink_ref[h:h + 1, :] = dsink_ref[h:h + 1, :] - jnp.sum(psink * delta, axis=0, keepdims=True)
            ds = jnp.concatenate(dss, axis=0).astype(BF16)
            pb = jnp.concatenate(ps, axis=0).astype(BF16)
            dq = jnp.dot(ds, kk, preferred_element_type=F32)
            for g, h in enumerate(heads):
                dq_ref[:, h * hd:(h + 1) * hd] = dq[g * BLOCK:(g + 1) * BLOCK].astype(dq_ref.dtype)
            dk = lax.dot_general(ds, q4, (((0,), (0,)), ((), ())), preferred_element_type=F32)
            dv = lax.dot_general(pb, do4, (((0,), (0,)), ((), ())), preferred_element_type=F32)
            for seg, r in enumerate(rows):
                at = (pl.ds(pl.multiple_of(r * BLOCK, BLOCK), BLOCK), cs)
                dk_ref[at] += dk[seg * BLOCK:(seg + 1) * BLOCK]
                dv_ref[at] += dv[seg * BLOCK:(seg + 1) * BLOCK]

    qspec = pl.BlockSpec((BLOCK, ATT_WIDTH), lambda i: (i, 0))
    whole = pl.BlockSpec((n, KV_WIDTH), lambda i: (0, 0))
    sinks = pl.BlockSpec((ATT_HEADS, 128), lambda i: (0, 0))
    return pl.pallas_call(body, grid=(nb,), in_specs=[qspec] + _att_specs(nb, COL_AV256) + [sinks, qspec], out_specs=(qspec, whole, whole, sinks),
                          out_shape=(_sds((n, ATT_WIDTH), BF16), _sds((n, KV_WIDTH), F32), _sds((n, KV_WIDTH), F32), _sds((ATT_HEADS, 128), F32)),
                          name=name, compiler_params=pltpu.CompilerParams(dimension_semantics=("arbitrary",)))(
                              q, k, k, k, k, proj, proj, proj, proj, sink_b, dmixed)


def _ret_decay(lg, d):
    a = lax.broadcasted_iota(jnp.int32, (BLOCK, 1), 0)
    b = lax.broadcasted_iota(jnp.int32, (1, BLOCK), 1)
    t_col = a + d * (BLOCK - 1 - 2 * a)
    t_row = b + d * (BLOCK - 1 - 2 * b)
    diff = t_col - t_row
    dist = jnp.maximum(diff, 0).astype(F32)
    dmask = jnp.where(diff >= d, jnp.exp(lg * dist), 0.0)
    tf = t_col.astype(F32)
    xi = jnp.exp(lg * (tf + 1.0))
    zeta = jnp.exp(lg * (BLOCK - 1.0 - tf))
    gam = jnp.exp(jnp.full((1, 1), BLOCK, F32) * lg)
    return dmask, dist, xi, zeta, gam, tf


RET_GROUP = 3


def _ret_group(nc):
    return next(c for c in (RET_GROUP, 2, 1) if nc % c == 0)


def _ret_fwd(q, k, proj, lg, name):
    n = q.shape[0]
    nc = n // BLOCK
    hd = RET_HEAD_DIM
    ch = _ret_group(nc)
    ns, rows = nc // ch, ch * BLOCK
    group = lambda d, s: s + d * (ns - 1 - 2 * s)

    def body(lg_ref, q_ref, k_ref, v0, v1, v2, v3, o_ref, st_ref, s_ref):
        d, s = pl.program_id(0), pl.program_id(1)

        @pl.when(s == 0)
        def _():
            s_ref[...] = jnp.zeros_like(s_ref)

        for h, v_ref in enumerate((v0, v1, v2, v3)):
            cs = slice(h * hd, (h + 1) * hd)
            dmask, _, xi, zeta, gam, _ = _ret_decay(lg_ref[h, d], d)
            sb = s_ref[h]
            for j in range(ch):
                at = pl.ds(pl.multiple_of((j + d * (ch - 1 - 2 * j)) * BLOCK, BLOCK), BLOCK)
                qv = q_ref[at, cs]
                kf = k_ref[at, cs].astype(F32) * RET_K_SCALE
                vv = v_ref[at, :]
                sc = lax.dot_general(qv, kf.astype(BF16), (((1,), (1,)), ((), ())), preferred_element_type=F32)
                sb16 = sb.astype(BF16)
                o_ref[at, cs] = (jnp.dot((sc * dmask).astype(BF16), vv, preferred_element_type=F32)
                                 + jnp.dot((qv.astype(F32) * xi).astype(BF16), sb16, preferred_element_type=F32))
                st_ref[h, j] = sb16
                sb = gam * sb + lax.dot_general((kf * zeta).astype(BF16), vv, (((0,), (0,)), ((), ())), preferred_element_type=F32)
            s_ref[h] = sb

    wide = pl.BlockSpec((rows, RET_WIDTH), lambda d, s: (group(d, s), 0))
    vblk = lambda h: pl.BlockSpec((rows, hd), lambda d, s: (group(d, s), COL_RV + h))
    return pl.pallas_call(
        body, grid=(2, ns), in_specs=[pl.BlockSpec(memory_space=pltpu.SMEM), wide, wide] + [vblk(h) for h in range(RET_HEADS)],
        out_specs=(pl.BlockSpec((None, rows, RET_WIDTH), lambda d, s: (d, group(d, s), 0)),
                   pl.BlockSpec((RET_HEADS, None, ch, hd, hd), lambda d, s: (0, d, s, 0, 0))),
        out_shape=(_sds((2, n, RET_WIDTH), F32), _sds((RET_HEADS, 2, nc, hd, hd), BF16)), scratch_shapes=[pltpu.VMEM((RET_HEADS, hd, hd), F32)],
        name=name, compiler_params=pltpu.CompilerParams(dimension_semantics=("parallel", "arbitrary")))(lg, q, k, proj, proj, proj, proj)


def _ret_bwd(q, k, proj, lg, do, states, name):
    n = q.shape[0]
    nc = n // BLOCK
    hd = RET_HEAD_DIM
    ch = _ret_group(nc)
    ns, rows = nc // ch, ch * BLOCK
    group = lambda d, r: (ns - 1 - r) + d * (2 * r - (ns - 1))

    def body(lg_ref, q_ref, k_ref, v0, v1, v2, v3, do_ref, st_ref, dq_ref, dk_ref, dv_ref, dlg_ref, ds_ref):
        d, r = pl.program_id(0), pl.program_id(1)

        @pl.when(r == 0)
        def _():
            ds_ref[...] = jnp.zeros_like(ds_ref)
            dlg_ref[...] = jnp.zeros_like(dlg_ref)

        first_row = group(d, r) * rows
        nt = (((1,), (1,)), ((), ()))
        tn = (((0,), (0,)), ((), ()))
        for h, v_ref in enumerate((v0, v1, v2, v3)):
            cs = slice(h * hd, (h + 1) * hd)
            dmask, dist, xi, zeta, gam, tf = _ret_decay(lg_ref[h, d], d)
            dsn = ds_ref[h]
            dlg = jnp.zeros((1, 1), F32)
            for j in reversed(range(ch)):
                start = (j + d * (ch - 1 - 2 * j)) * BLOCK
                at = pl.ds(pl.multiple_of(start, BLOCK), BLOCK)
                keep = (lax.broadcasted_iota(jnp.int32, (BLOCK, 1), 0) + first_row + start) >= PAD_FRONT
                qv, vv, dov = q_ref[at, cs], v_ref[at, :], do_ref[at, cs]
                qf = qv.astype(F32)
                kf = k_ref[at, cs].astype(F32) * RET_K_SCALE
                kb = kf.astype(BF16)
                sc = st_ref[h, j]
                s = lax.dot_general(qv, kb, nt, preferred_element_type=F32)
                dsc = lax.dot_general(dov, vv, nt, preferred_element_type=F32) * dmask
                dsb = dsc.astype(BF16)
                dq_c = xi * lax.dot_general(dov, sc, nt, preferred_element_type=F32)
                dk_c = zeta * lax.dot_general(vv, dsn.astype(BF16), nt, preferred_element_type=F32)
                dq = jnp.dot(dsb, kb, preferred_element_type=F32) + dq_c
                dk = lax.dot_general(dsb, qv, tn, preferred_element_type=F32) + dk_c
                dv = (lax.dot_general((s * dmask).astype(BF16), dov, tn, preferred_element_type=F32)
                      + jnp.dot((kf * zeta).astype(BF16), dsn.astype(BF16), preferred_element_type=F32))
                dlg = dlg + (jnp.sum(dsc * s * dist, keepdims=True)
                             + jnp.sum((tf + 1.0) * jnp.sum(qf * dq_c, axis=-1, keepdims=True), keepdims=True)
                             + jnp.sum((BLOCK - 1.0 - tf) * jnp.sum(kf * dk_c, axis=-1, keepdims=True), keepdims=True)
                             + BLOCK * gam * jnp.sum(dsn * sc.astype(F32), keepdims=True))
                dsn = gam * dsn + lax.dot_general((qf * xi).astype(BF16), dov, tn, preferred_element_type=F32)
                dq_ref[at, cs] = dq
                dk_ref[at, cs] = jnp.where(keep, dk * RET_K_SCALE, 0.0)
                dv_ref[at, cs] = jnp.where(keep, dv, 0.0)
            ds_ref[h] = dsn
            dlg_ref[h] += dlg

    wide = pl.BlockSpec((rows, RET_WIDTH), lambda d, r: (group(d, r), 0))
    vblk = lambda h: pl.BlockSpec((rows, hd), lambda d, r: (group(d, r), COL_RV + h))
    plane = pl.BlockSpec((None, rows, RET_WIDTH), lambda d, r: (d, group(d, r), 0))
    return pl.pallas_call(
        body, grid=(2, ns),
        in_specs=[pl.BlockSpec(memory_space=pltpu.SMEM), wide, wide] + [vblk(h) for h in range(RET_HEADS)]
        + [wide, pl.BlockSpec((RET_HEADS, None, ch, hd, hd), lambda d, r: (0, d, ns - 1 - r, 0, 0))],
        out_specs=(plane, plane, plane, pl.BlockSpec((RET_HEADS, None, 8, 128), lambda d, r: (0, d, 0, 0))),
        out_shape=(_sds((2, n, RET_WIDTH), F32),) * 3 + (_sds((RET_HEADS, 2, 8, 128), F32),),
        scratch_shapes=[pltpu.VMEM((RET_HEADS, hd, hd), F32)], name=name,
        compiler_params=pltpu.CompilerParams(dimension_semantics=("parallel", "arbitrary")))(lg, q, k, proj, proj, proj, proj, do, states)


def _retgate_fwd(o, proj, gain, mixed, name):
    _, n, _ = o.shape
    tr = _row_tile(n, 1056)
    hd = RET_HEAD_DIM

    def body(o_ref, rg_ref, g_ref, _, y_ref):
        ov = o_ref[0] + o_ref[1]
        r = lax.rsqrt(jnp.mean(ov * ov, axis=-1, keepdims=True) + EPS)
        y_ref[...] = (_silu(rg_ref[...].astype(F32)) * (ov * r * g_ref[...])).astype(y_ref.dtype)

    return pl.pallas_call(body, grid=(n // tr, RET_HEADS),
                          in_specs=[pl.BlockSpec((2, tr, hd), lambda i, h: (0, i, h)), pl.BlockSpec((tr, hd), lambda i, h: (i, COL_RG + h)),
                                    pl.BlockSpec((1, hd), lambda i, h: (0, h)), ANY],
                          out_specs=pl.BlockSpec((tr, hd), lambda i, h: (i, ATT_WIDTH // hd + h)), out_shape=_sds(mixed.shape, mixed.dtype),
                          input_output_aliases={3: 0}, name=name)(o, proj, gain, mixed)


def _retgate_bwd(dmixed, o, proj, gain, name):
    _, n, _ = o.shape
    tr = _row_tile(n, 1056)
    hd = RET_HEAD_DIM

    def body(dy_ref, o_ref, rg_ref, g_ref, do_ref, drg_ref, dg_ref):
        i = pl.program_id(1)
        ov = o_ref[0] + o_ref[1]
        r = lax.rsqrt(jnp.mean(ov * ov, axis=-1, keepdims=True) + EPS)
        xhat = ov * r
        rg = rg_ref[...].astype(F32)
        dy = dy_ref[...].astype(F32)
        drg_ref[...] = (dy * (xhat * g_ref[...]) * _dsilu(rg)).astype(drg_ref.dtype)
        dn = dy * _silu(rg)
        dxh = dn * g_ref[...]
        do_ref[...] = (r * (dxh - xhat * jnp.mean(dxh * xhat, axis=-1, keepdims=True))).astype(do_ref.dtype)

        @pl.when(i == 0)
        def _():
            dg_ref[...] = jnp.zeros_like(dg_ref)

        dg_ref[...] += jnp.sum(dn * xhat, axis=0, keepdims=True)

    tile = pl.BlockSpec((tr, hd), lambda h, i: (i, h))
    vec = pl.BlockSpec((1, hd), lambda h, i: (0, h))
    rg_cols = pl.BlockSpec((tr, hd), lambda h, i: (i, COL_RG + h))
    return pl.pallas_call(body, grid=(RET_HEADS, n // tr),
                          in_specs=[pl.BlockSpec((tr, hd), lambda h, i: (i, ATT_WIDTH // hd + h)), pl.BlockSpec((2, tr, hd), lambda h, i: (0, i, h)),
                                    rg_cols, vec],
                          out_specs=(tile, rg_cols, vec), out_shape=(_sds((n, RET_WIDTH), BF16), _sds((n, IN_COLS), BF16), _sds((1, RET_WIDTH), F32)),
                          name=name, compiler_params=pltpu.CompilerParams(dimension_semantics=("parallel", "arbitrary")))(dmixed, o, proj, gain)


FFN_TILE = 512


def _swiglu_fwd(x, wg_t, wu_t, name):
    n, k = x.shape
    tm = _row_tile(n, TOKEN_TILE)
    nt = (((1,), (1,)), ((), ()))

    def body(x_ref, g_ref, u_ref, go_ref, uo_ref, f_ref):
        g = lax.dot_general(x_ref[...], g_ref[...], nt, preferred_element_type=F32)
        u = lax.dot_general(x_ref[...], u_ref[...], nt, preferred_element_type=F32)
        go_ref[...] = g.astype(go_ref.dtype)
        uo_ref[...] = u.astype(uo_ref.dtype)
        f_ref[...] = (_silu(g) * u).astype(f_ref.dtype)

    w = pl.BlockSpec((FFN_TILE, k), lambda i, j: (j, 0))
    o = pl.BlockSpec((tm, FFN_TILE), lambda i, j: (i, j))
    return pl.pallas_call(body, grid=(n // tm, wg_t.shape[0] // FFN_TILE), in_specs=[pl.BlockSpec((tm, k), lambda i, j: (i, 0)), w, w],
                          out_specs=(o, o, o), out_shape=(_sds((n, wg_t.shape[0]), BF16),) * 3, name=name,
                          compiler_params=pltpu.CompilerParams(dimension_semantics=("parallel", "parallel")))(x, wg_t, wu_t)


def _swiglu_bwd(dy, wd, gate, up, name):
    n, k = dy.shape
    tm = _row_tile(n, TOKEN_TILE)
    nt = (((1,), (1,)), ((), ()))

    def body(dy_ref, w_ref, g_ref, u_ref, dg_ref, du_ref):
        df = lax.dot_general(dy_ref[...], w_ref[...], nt, preferred_element_type=F32)
        g = g_ref[...].astype(F32)
        dg_ref[...] = (df * u_ref[...].astype(F32) * _dsilu(g)).astype(dg_ref.dtype)
        du_ref[...] = (df * _silu(g)).astype(du_ref.dtype)

    o = pl.BlockSpec((tm, FFN_TILE), lambda i, j: (i, j))
    return pl.pallas_call(body, grid=(n // tm, wd.shape[0] // FFN_TILE),
                          in_specs=[pl.BlockSpec((tm, k), lambda i, j: (i, 0)), pl.BlockSpec((FFN_TILE, k), lambda i, j: (j, 0)), o, o],
                          out_specs=(o, o), out_shape=(_sds((n, wd.shape[0]), BF16),) * 2, name=name,
                          compiler_params=pltpu.CompilerParams(dimension_semantics=("parallel", "parallel")))(dy, wd, gate, up)


def _loss_head(h, target, name):
    n, d = h.shape
    nb = n // BLOCK

    def body(h_ref, t_ref, dh_ref, l_ref):
        i = pl.program_id(0)

        @pl.when(i == 0)
        def _():
            l_ref[...] = jnp.zeros_like(l_ref)
            dh_ref[...] = jnp.zeros_like(dh_ref)

        @pl.when(i > 0)
        def _():
            e = h_ref[...] - t_ref[...]
            dh_ref[...] = e * (1.0 / d)
            l_ref[...] += 0.5 * jnp.sum(jnp.mean(e * e, axis=-1, keepdims=True), keepdims=True)

    blk = pl.BlockSpec((BLOCK, d), lambda i: (i, 0))
    return pl.pallas_call(body, grid=(nb,), in_specs=[blk, pl.BlockSpec((BLOCK, d), lambda i: (jnp.maximum(i - 1, 0), 0))],
                          out_specs=(blk, pl.BlockSpec((8, 128), lambda i: (0, 0))), out_shape=(_sds((n, d), F32), _sds((8, 128), F32)), name=name,
                          compiler_params=pltpu.CompilerParams(dimension_semantics=("arbitrary",)))(h, target)


def _adamw(parts, w, m, v, name, layer=None, prev=None, after=None):
    s, (r, c) = parts.shape[0], parts.shape[-2:]
    tr = _row_tile(r, max(16, (ADAMW_TILE_ELEMS // c) // 16 * 16))
    b1c, b2c = 1.0 - ADAM_B1 ** ADAM_STEP, 1.0 - ADAM_B2 ** ADAM_STEP

    def body(p_ref, w_ref, m_ref, v_ref, *rest):
        g_ref, d_ref, mo_ref, vo_ref = rest[-4:]
        g = p_ref[0].astype(F32)
        for q in range(1, s):
            g = g + p_ref[q].astype(F32)
        mn = ADAM_B1 * m_ref[...] + (1.0 - ADAM_B1) * g
        vn = ADAM_B2 * v_ref[...] + (1.0 - ADAM_B2) * jnp.square(g)
        g_ref[...] = g
        mo_ref[...] = mn
        vo_ref[...] = vn
        d_ref[...] = -ADAM_LR * ((mn / b1c) / (jnp.sqrt(vn / b2c) + ADAM_EPS) + ADAM_WD * w_ref[...])

    pspec = pl.BlockSpec((s, tr, c), lambda i: (0, i, 0))
    if layer is None:
        t = pl.BlockSpec((tr, c), lambda i: (i, 0))
        return pl.pallas_call(body, grid=(r // tr,), in_specs=[pspec, t, t, t], out_specs=(t, t, t, t), out_shape=(_sds((r, c), F32),) * 4,
                              name=name)(parts, w, m, v)
    t = pl.BlockSpec((None, tr, c), lambda i: (layer, i, 0))
    prev = prev if prev is not None else tuple(lax.empty(w.shape, F32) for _ in range(4))
    extra = [] if after is None else [after]
    return pl.pallas_call(body, grid=(r // tr,), in_specs=[pspec, t, t, t] + [ANY] * (4 + len(extra)), out_specs=(t, t, t, t),
                          out_shape=(_sds(w.shape, F32),) * 4, input_output_aliases={4 + i: i for i in range(4)}, name=name)(
                              parts, w, m, v, *prev, *extra)


def _allgather(xs, name, after=None):
    na = len(xs)
    first_out = na + (after is not None)

    def body(*refs):
        x_refs, o_refs = refs[:na], refs[first_out:first_out + na]
        send, recv, lsem = refs[first_out + na:]
        x, y, c = lax.axis_index("x"), lax.axis_index("y"), lax.axis_index("c")
        me, sib = (x, y, c), (x, y, 1 - c)
        chips = [(1 - x, y), (x, 1 - y), (1 - x, 1 - y)]
        slot = lambda p: 4 * p[0] + 2 * p[1] + p[2]

        def copy(a, k, block, to, src=None):
            dst = o_refs[a].at[slot(block)]
            return pltpu.make_async_remote_copy(src_ref=dst if src is None else src, dst_ref=dst, send_sem=send.at[a, k], recv_sem=recv.at[a, k],
                                                device_id=to, device_id_type=MESH)

        mine = [pltpu.make_async_copy(x_refs[a], o_refs[a].at[slot(me)], lsem.at[a]) for a in range(na)]
        for cp in mine:
            cp.start()
        first = []
        for a in range(na):
            first.append(copy(a, 0, me, sib, src=x_refs[a]))
            first += [copy(a, 1 + j, me, (*chip, c), src=x_refs[a]) for j, chip in enumerate(chips)]
        for cp in first:
            cp.start()
        passed = []
        for j, chip in enumerate(chips):
            for a in range(na):
                copy(a, 1 + j, (*chip, c), me).wait_recv()
                passed.append(copy(a, 4 + j, (*chip, c), sib))
                passed[-1].start()
        for a in range(na):
            copy(a, 0, sib, me).wait_recv()
            for j, chip in enumerate(chips):
                copy(a, 4 + j, (*chip, 1 - c), me).wait_recv()
        for cp in first + passed:
            cp.wait_send()
        for cp in mine:
            cp.wait()

    extra = [] if after is None else [after]
    return pl.pallas_call(body, in_specs=[ANY] * (na + len(extra)), out_specs=[ANY] * na,
                          out_shape=[_sds((N_DEV,) + t.shape, t.dtype) for t in xs],
                          scratch_shapes=[pltpu.SemaphoreType.DMA((na, 7)), pltpu.SemaphoreType.DMA((na, 7)), pltpu.SemaphoreType.DMA((na,))],
                          name=name)(*xs, *extra)


HBM = pl.BlockSpec(memory_space=pltpu.HBM)
SEM = pl.BlockSpec(memory_space=pltpu.SEMAPHORE)
EFFECT = pltpu.SideEffectType.DATAFLOW_SIDE_EFFECTING


SPLIT_RELATIONS = dict(gather=(1, 2, 4, 6),
                       forward=(2, 4, 6),
                       scatter=tuple(range(1, N_DEV)))


def _split_copies(mode, x_refs, land_refs, send, recv, own, landing):
    x, y, c = lax.axis_index("x"), lax.axis_index("y"), lax.axis_index("c")
    flip = lambda r: ((1 - x if r & 4 else x), (1 - y if r & 2 else y), (1 - c if r & 1 else c))
    slot = lambda p: 4 * p[0] + 2 * p[1] + p[2]
    me = slot((x, y, c))
    rel = SPLIT_RELATIONS[mode]
    local, remote = [], []
    for a in range(len(land_refs)):
        if mode != "forward":
            local.append(pltpu.make_async_copy(x_refs[a].at[me] if mode == "scatter" else x_refs[a], land_refs[a].at[me], own.at[a]))
        for j, r in enumerate(rel):
            if mode == "forward":
                to = flip(1)
                src = dst = land_refs[a].at[slot(flip(r ^ 1) if landing else flip(r))]
            else:
                to = flip(r)
                src = x_refs[a].at[slot(to)] if mode == "scatter" else x_refs[a]
                dst = land_refs[a].at[slot(to) if landing else me]
            remote.append(pltpu.make_async_remote_copy(src_ref=src, dst_ref=dst, send_sem=send.at[len(rel) * a + j],
                                                       recv_sem=recv.at[len(rel) * a + j], device_id=to, device_id_type=MESH))
    return local, remote


def _send_start(mode, xs, lands, after, name):
    if lands is None:
        lands = [lax.empty(t.shape if mode == "scatter" else (N_DEV,) + t.shape, t.dtype) for t in xs]
    nx, na, nr = len(xs), len(lands), len(SPLIT_RELATIONS[mode])
    nsem = 2 if mode == "forward" else 3

    def body(*refs):
        x_refs, land_refs = refs[:nx], refs[nx:nx + na]
        sems = refs[nx + na + 1:nx + na + 1 + nsem]
        local, remote = _split_copies(mode, x_refs, land_refs, sems[0], sems[1], sems[2] if nsem == 3 else None, False)
        for cp in remote + local:
            cp.start()
        refs[-1][...] = jnp.zeros_like(refs[-1])

    hbm = lambda t: pltpu.with_memory_space_constraint(t, pltpu.HBM)
    sem_shapes = [pltpu.SemaphoreType.DMA((nr * na,)), pltpu.SemaphoreType.DMA((nr * na,)), pltpu.SemaphoreType.DMA((na,))][:nsem]
    outs = pl.pallas_call(
        body, name=name,
        out_shape=(*sem_shapes, *[pltpu.HBM(t.shape, t.dtype) for t in list(xs) + list(lands)], _sds((8, 128), F32)),
        in_specs=[HBM] * (nx + na) + [ANY], out_specs=(*[SEM] * nsem, *[HBM] * (nx + na), pl.BlockSpec(memory_space=pltpu.VMEM)),
        input_output_aliases={i: nsem + i for i in range(nx + na)},
        compiler_params=pltpu.CompilerParams(has_side_effects=EFFECT))(*[hbm(t) for t in list(xs) + list(lands)], after)
    return outs[:nsem], list(outs[nsem:nsem + nx]), list(outs[nsem + nx:nsem + nx + na]), outs[-1]


def _send_wait(mode, started, after, name):
    sems, xs, lands, _ = started
    nx, na, nsem = len(xs), len(lands), len(sems)

    def body(*refs):
        s = refs[nx + na:nx + na + nsem]
        local, remote = _split_copies(mode, refs[:nx], refs[nx:nx + na], s[0], s[1], s[2] if nsem == 3 else None, True)
        for cp in remote:
            cp.wait_send()
            cp.wait_recv()
        for cp in local:
            cp.wait()

    outs = pl.pallas_call(body, name=name, out_shape=tuple(pltpu.HBM(t.shape, t.dtype) for t in xs + lands),
                          in_specs=[HBM] * (nx + na) + [SEM] * nsem + [ANY], out_specs=[HBM] * (nx + na),
                          input_output_aliases={i: i for i in range(nx + na)},
                          compiler_params=pltpu.CompilerParams(has_side_effects=EFFECT))(*xs, *lands, *sems, after)
    return list(outs[nx:])


def _local_step(x, meta, target, mix_weights_fn, ffn_weights_fn, prefetch_fn, grads_fn, sink, dec_f, dec_b, ret_norm, n_mix_pre, n_mix_post,
                n_ffn_pre, n_ffn_post):
    depth = n_mix_pre.shape[0]
    d = D_MODEL
    h = jnp.concatenate([jnp.zeros((PAD_FRONT, d), F32), meta, x], axis=0)
    n = h.shape[0]
    cos_a, sin_a, cos_r, sin_r, perm = _rope_tables(n)
    lg_all = jnp.stack([-jnp.exp(dec_f), -jnp.exp(dec_b)], axis=-1)
    saved = []
    u = _norm_fwd(h, n_mix_pre[0][None], None, BF16, "l0_norm_mix_pre")
    wi, wo = mix_weights_fn(0, h)
    for l in range(depth):
        t = f"l{l}_"
        sink_b = jnp.broadcast_to(sink[l][:, None], (ATT_HEADS, 128))
        proj = _mm_nt(u, wi, BF16, WIDE_TILE, d, t + "proj")
        aq = _rope_att(proj, 0, ATT_HEADS, cos_a, sin_a, perm, t + "rope_aq")
        ak = _rope_att(proj, ATT_WIDTH // KV_WIDTH, ATT_KV_HEADS, cos_a, sin_a, perm, t + "rope_ak")
        mixed = _att_fwd(aq, ak, proj, sink_b, t + "att")
        proj3 = proj[None]
        rq = _rope_ret(proj3, COL_RQ // 2, cos_r, sin_r, t + "rope_rq")
        rk = _rope_ret(proj3, COL_RK // 2, cos_r, sin_r, t + "rope_rk")
        o_ret, states = _ret_fwd(rq, rk, proj, lg_all[l], t + "ret")
        tok = prefetch_fn(2 * l + 1, o_ret)
        mixed = _retgate_fwd(o_ret, proj, (ret_norm[l] + tok)[None], mixed, t + "retgate")
        wg, wu, wd = ffn_weights_fn(l, mixed)
        mo, h_mid, u2 = _out_proj_norms(mixed, wo, h, n_mix_post[l][None], n_ffn_pre[l][None], t + "out_proj")
        gate, up, f = _swiglu_fwd(u2, wg, wu, t + "gate_up")
        post = (n_ffn_post[l] + prefetch_fn(2 * l + 2, f))[None]
        dn = _mm_nn(f, wd, F32, 512, D_FF, t + "down", tm_cap=DEEP_K_TOKEN_TILE)
        saved.append(dict(h=h, u=u, proj=proj, aq=aq, ak=ak, rq=rq, rk=rk, o_ret=o_ret, states=states, mixed=mixed, mo=mo, h_mid=h_mid, u2=u2,
                          gate=gate, up=up, f=f, dn=dn, sink_b=sink_b, wi=wi, wo=wo, wg=wg, wu=wu, wd=wd))
        if l + 1 < depth:
            wi, wo = mix_weights_fn(l + 1, dn)
            h, u = _norm_res_norm(dn, post, h_mid, n_mix_pre[l + 1][None], t + "norm_ffn_post")
        else:
            h = _norm_fwd(dn, post, h_mid, F32, t + "norm_ffn_post")

    dh, loss_part = _loss_head(h, target, "loss_head")
    gs = dict(sink=[None] * depth, dec_f=[None] * depth, dec_b=[None] * depth, ret_norm=[None] * depth, mix_pre=[None] * depth,
              mix_post=[None] * depth, ffn_pre=[None] * depth, ffn_post=[None] * depth)
    d_dn, gs["ffn_post"][depth - 1] = _norm_bwd(dh, saved[-1]["dn"], n_ffn_post[depth - 1][None], None, BF16, f"l{depth - 1}_b_norm_ffn_post")
    for l in reversed(range(depth)):
        t = f"l{l}_b_"
        sv = saved[l]
        proj = sv["proj"]
        gw = {}
        d_gate, d_up = _swiglu_bwd(d_dn, sv["wd"], sv["gate"], sv["up"], t + "d_gate_up")
        gw["wd"] = _mm_tn(sv["f"], d_dn, WIDE_TILE, 1024, t + "dw_down")
        du2 = _mm_nn(d_gate, sv["wg"], F32, 512, D_FF, t + "du2_gate", tm_cap=DEEP_K_TOKEN_TILE)
        du2 = _mm_nn(d_up, sv["wu"], F32, 512, D_FF, t + "du2_up", acc=du2, tm_cap=DEEP_K_TOKEN_TILE)
        gw["wg"] = _mm_tn(d_gate, sv["u2"], WIDE_TILE, 1024, t + "dw_gate")
        gw["wu"] = _mm_tn(d_up, sv["u2"], WIDE_TILE, 1024, t + "dw_up")
        tok_b = grads_fn(l, "ffn", gw, du2)
        dh, d_mo, gs["ffn_pre"][l], gs["mix_post"][l] = _norm_bwd_pair(du2, sv["h_mid"], (n_ffn_pre[l] + tok_b)[None], dh, sv["mo"],
                                                                         n_mix_post[l][None], t + "norm_ffn_pre")
        d_mixed = _mm_nt(d_mo, sv["wo"], BF16, 1024, d, t + "d_mixed")
        gw["wo"] = _mm_tn(sv["mixed"], d_mo, 1024, 1024, t + "dw_out")
        d_o, dproj, gs["ret_norm"][l] = _retgate_bwd(d_mixed, sv["o_ret"], proj, ret_norm[l][None], t + "retgate")
        dq_r, dk_r, dv_r, dlg = _ret_bwd(sv["rq"], sv["rk"], proj, lg_all[l], d_o, sv["states"], t + "ret")
        draw = dlg[:, :, 0, 0] * lg_all[l]
        gs["dec_f"][l], gs["dec_b"][l] = draw[:, 0], draw[:, 1]
        dproj = _rope_ret(dq_r, 0, cos_r, -sin_r, t + "rope_rq", into=dproj, out_col=COL_RQ // 2)
        dproj = _rope_ret(dk_r, 0, cos_r, -sin_r, t + "rope_rk", into=dproj, out_col=COL_RK // 2)
        dproj = _rope_ret(dv_r, 0, None, None, t + "sum_rv", into=dproj, out_col=COL_RV // 2)
        dq_a, dk_a, dv_a, dsink = _att_bwd(sv["aq"], sv["ak"], proj, sv["sink_b"], d_mixed, t + "att")
        gs["sink"][l] = dsink[:, 0]
        dproj = _rope_att(dq_a, 0, ATT_HEADS, cos_a, -sin_a, perm, t + "rope_aq", into=dproj, out_col=0)
        dproj = _rope_att(dk_a, 0, ATT_KV_HEADS, cos_a, -sin_a, perm, t + "rope_ak", plain=dv_a, into=dproj, out_col=ATT_WIDTH // (2 * KV_WIDTH))
        gw["wi"] = _mm_tn(dproj, sv["u"], WIDE_TILE, 1024, t + "dw_in")
        tok_b = grads_fn(l, "mix", gw, dproj)
        du = _mm_nn(dproj, sv["wi"], F32, 512, IN_COLS, t + "du", tm_cap=DEEP_K_TOKEN_TILE)
        if l > 0:
            dh, d_dn, gs["mix_pre"][l], gs["ffn_post"][l - 1] = _norm_bwd_pair(du, sv["h"], (n_mix_pre[l] + tok_b)[None], dh, saved[l - 1]["dn"],
                                                                              n_ffn_post[l - 1][None], t + "norm_mix_pre")
        else:
            dh, gs["mix_pre"][l] = _norm_bwd(du, sv["h"], (n_mix_pre[l] + tok_b)[None], dh, F32, t + "norm_mix_pre")
    return loss_part[0, 0], dh, gs


def _pack_small(mix_pre, mix_post, ffn_pre, ffn_post, ret_norm, sink, dec_f, dec_b, loss, meta):
    d = D_MODEL

    def tile(a, rows=8):
        a = jnp.reshape(a, (-1, a.shape[-1])) if a.ndim else jnp.reshape(a, (1, 1))
        return jnp.pad(a, ((0, rows - a.shape[0]), (0, d - a.shape[1])))

    return jnp.concatenate([tile(mix_pre), tile(mix_post), tile(ffn_pre), tile(ffn_post), tile(ret_norm.reshape(-1, d)), tile(sink), tile(dec_f),
                            tile(dec_b), tile(loss), tile(meta, SMALL_ROWS - ROW_META)], axis=0)


def _unpack_small(p, depth):
    rows = lambda r0, cols: p[r0:r0 + depth, :cols]
    return dict(mix_pre=rows(ROW_MIX_PRE, D_MODEL), mix_post=rows(ROW_MIX_POST, D_MODEL), ffn_pre=rows(ROW_FFN_PRE, D_MODEL),
                ffn_post=rows(ROW_FFN_POST, D_MODEL), ret_norm=p[ROW_RET_NORM:ROW_RET_NORM + depth * RET_WIDTH // D_MODEL].reshape(depth, RET_WIDTH),
                sink=rows(ROW_SINK, ATT_HEADS), dec_f=rows(ROW_DEC_F, RET_HEADS), dec_b=rows(ROW_DEC_B, RET_HEADS), loss=p[ROW_LOSS, 0])


def kernel(x, meta_tokens, w_in, w_out, attn_sink, ret_decay_fwd, ret_decay_bwd, ret_norm, norm_mix_pre, norm_mix_post, w_gate, w_up, w_down, norm_ffn_pre, norm_ffn_post, loss_target, m_meta_tokens, m_w_in, m_w_out, m_attn_sink, m_ret_decay_fwd, m_ret_decay_bwd, m_ret_norm, m_norm_mix_pre, m_norm_mix_post, m_w_gate, m_w_up, m_w_down, m_norm_ffn_pre, m_norm_ffn_post, v_meta_tokens, v_w_in, v_w_out, v_attn_sink, v_ret_decay_fwd, v_ret_decay_bwd, v_ret_norm, v_norm_mix_pre, v_norm_mix_post, v_w_gate, v_w_up, v_w_down, v_norm_ffn_pre, v_norm_ffn_post):
    depth, d = w_in.shape[0], D_MODEL
    me = 4 * lax.axis_index("x") + 2 * lax.axis_index("y") + lax.axis_index("c")
    zero = jnp.zeros((), F32)

    meta_g, = _allgather([meta_tokens], "gather_meta")
    meta = meta_g.transpose(1, 0, 2).reshape(N_META, d)

    def shards(k):
        l = k // 2
        if k % 2 == 0:
            return [w_in[l].T.astype(BF16), w_out[l].astype(BF16)]
        return [w_gate[l].T.astype(BF16), w_up[l].T.astype(BF16), w_down[l].astype(BF16)]

    gathers, ahead = {}, 2
    for k in range(min(ahead + 1, 2 * depth)):
        gathers[k] = _send_start("gather", shards(k), None, gathers[k - 1][3] if k else meta_g, f"gather_start_g{k}")

    passing = {}

    def pass_on(k, after):
        lands = _send_wait("gather", gathers.pop(k), after, f"gather_wait_g{k}")
        passing[k] = _send_start("forward", [], lands, after, f"forward_start_g{k}")

    def take(k, h):
        after = h
        if k >= 1 and k + ahead < 2 * depth:
            gathers[k + ahead] = _send_start("gather", shards(k + ahead), None, h, f"gather_start_g{k + ahead}")
            after = gathers[k + ahead][3]
        elif k == 0:
            after = gathers[max(gathers)][3]
        if k not in passing:
            pass_on(k, after)
        return _send_wait("forward", passing.pop(k), after, f"forward_wait_g{k}")

    def prefetch_fn(k, after):
        if k in gathers and k not in passing:
            pass_on(k, after)
            return passing[k][3][0, 0]
        return zero

    def mix_weights_fn(l, h):
        wi_t, wo = take(2 * l, h)
        return wi_t.reshape(IN_COLS, d), wo.reshape(d, d)

    def ffn_weights_fn(l, h):
        wg_t, wu_t, wd = take(2 * l + 1, h)
        return wg_t.reshape(D_FF, d), wu_t.reshape(D_FF, d), wd.reshape(D_FF, d)

    exchanges, adam, order = {}, {}, []
    tr = lambda *ts: tuple(jnp.swapaxes(t, 1, 2) for t in ts)
    big = dict(wi=tr(w_in, m_w_in, v_w_in), wg=tr(w_gate, m_w_gate, v_w_gate), wu=tr(w_up, m_w_up, v_w_up), wd=(w_down, m_w_down, v_w_down),
               wo=(w_out, m_w_out, v_w_out))
    kinds = dict(ffn=("wg", "wu", "wd"), mix=("wi", "wo"))

    arrivals = []

    def finish(key, after):
        l, part = key
        arrivals.append((key, _send_wait("scatter", exchanges.pop(key), after, f"exchange_wait_{part}_l{l}")))

    def update():
        last_start = exchanges[order[-1]][3] if order[-1] in exchanges else None
        while arrivals:
            (l, part), arrived = arrivals.pop(0)
            for kind, parts in zip(kinds[part], arrived):
                adam[kind] = _adamw(parts, *big[kind], f"adamw_{kind}_l{l}", layer=l, prev=adam.get(kind),
                                    after=None if kind in adam else last_start)

    def grads_fn(l, part, gw, after):
        packed = [gw[kind].reshape(N_DEV, -1, d) for kind in kinds[part]]
        exchanges[(l, part)] = _send_start("scatter", packed, None, after, f"exchange_start_{part}_l{l}")
        order.append((l, part))
        token = exchanges[(l, part)][3]
        if len(order) > 2:
            finish(order[-3], token)
        return token[0, 0]

    loss_part, dh, gs = _local_step(x[0], meta, loss_target[0], mix_weights_fn, ffn_weights_fn, prefetch_fn, grads_fn, attn_sink, ret_decay_fwd, ret_decay_bwd, ret_norm,
                                    norm_mix_pre, norm_mix_post, norm_ffn_pre, norm_ffn_post)
    grad_x = dh[BLOCK:][None]

    st = lambda xs: jnp.stack([t.reshape(-1) if t.ndim == 1 else t[0] for t in xs])
    small = _pack_small(st(gs["mix_pre"]), st(gs["mix_post"]), st(gs["ffn_pre"]), st(gs["ffn_post"]), st(gs["ret_norm"]), st(gs["sink"]),
                        st(gs["dec_f"]), st(gs["dec_b"]), loss_part, dh[PAD_FRONT:BLOCK])
    update()
    small_g, = _allgather([small], "gather_small", after=adam["wo"][0])
    for key in order[-2:]:
        finish(key, small_g)
    update()
    o_wi, o_wo, o_wg, o_wu, o_wd = tr(*adam["wi"]), adam["wo"], tr(*adam["wg"]), tr(*adam["wu"]), adam["wd"]
    zmeta = jnp.zeros((N_META, d), F32)
    packs = [_pack_small(a[0], a[1], a[2], a[3], a[4], a[5], a[6], a[7], zero, zmeta) for a in (
        (norm_mix_pre, norm_mix_post, norm_ffn_pre, norm_ffn_post, ret_norm, attn_sink, ret_decay_fwd, ret_decay_bwd),
        (m_norm_mix_pre, m_norm_mix_post, m_norm_ffn_pre, m_norm_ffn_post, m_ret_norm, m_attn_sink, m_ret_decay_fwd, m_ret_decay_bwd),
        (v_norm_mix_pre, v_norm_mix_post, v_norm_ffn_pre, v_norm_ffn_post, v_ret_norm, v_attn_sink, v_ret_decay_fwd, v_ret_decay_bwd))]
    o_small = [_unpack_small(o, depth) for o in _adamw(small_g, packs[0], packs[1], packs[2], "adamw_small")]
    meta_parts = lax.dynamic_slice(small_g, (0, ROW_META, me * (d // N_DEV)), (N_DEV, N_META, d // N_DEV))
    o_meta = _adamw(meta_parts, meta_tokens, m_meta_tokens, v_meta_tokens, "adamw_meta")

    outs = []
    for i in range(4):
        s = o_small[i]
        outs += [o_meta[i], o_wi[i], o_wo[i], s["sink"], s["dec_f"], s["dec_b"], s["ret_norm"], s["mix_pre"], s["mix_post"], o_wg[i], o_wu[i],
                 o_wd[i], s["ffn_pre"], s["ffn_post"]]
    return (o_small[0]["loss"], grad_x, *outs)
```

```python
import jax
import jax.numpy as jnp
import numpy as np
from jax import lax
from jax.experimental import pallas as pl
from jax.experimental.pallas import tpu as pltpu

F32, BF16 = jnp.float32, jnp.bfloat16

D_MODEL = 2048
N_META = 16
BLOCK = 128
WINDOW = 128
PAD_FRONT = BLOCK - N_META
ATT_HEAD_DIM = 128
ATT_WIDTH = D_MODEL // 2
ATT_HEADS = ATT_WIDTH // ATT_HEAD_DIM
ATT_KV_HEADS = 2
ATT_GROUP = ATT_HEADS // ATT_KV_HEADS
KV_WIDTH = ATT_KV_HEADS * ATT_HEAD_DIM
ROT_DIM = ATT_HEAD_DIM // 4
ROPE_THETA = 500000.0
RET_WIDTH = D_MODEL - ATT_WIDTH
RET_HEAD_DIM = 256
RET_HEADS = RET_WIDTH // RET_HEAD_DIM
RET_THETA = 10000.0
D_FF = 5632
IN_COLS = ATT_WIDTH + 2 * KV_WIDTH + 4 * RET_WIDTH
N_DEV = 8
EPS = 1e-6
NEG = -1e30
RET_K_SCALE = RET_HEAD_DIM ** -0.5
ATT_SCALE = ATT_HEAD_DIM ** -0.5

COL_AV256 = (ATT_WIDTH + KV_WIDTH) // 256
COL_RQ = (ATT_WIDTH + 2 * KV_WIDTH) // RET_HEAD_DIM
COL_RK = COL_RQ + RET_HEADS
COL_RV = COL_RK + RET_HEADS
COL_RG = COL_RV + RET_HEADS

ADAM_LR, ADAM_B1, ADAM_B2, ADAM_EPS, ADAM_WD, ADAM_STEP = 0.001, 0.9, 0.999, 1e-08, 0.01, 10

ROW_MIX_PRE, ROW_MIX_POST, ROW_FFN_PRE, ROW_FFN_POST, ROW_RET_NORM, ROW_SINK, ROW_DEC_F, ROW_DEC_B, ROW_LOSS, ROW_META, SMALL_ROWS = (
    0, 8, 16, 24, 32, 40, 48, 56, 64, 72, 96)
ADAMW_TILE_ELEMS = 128 * 1024

MESH = pl.DeviceIdType.MESH
ANY = pl.BlockSpec(memory_space=pl.ANY)


def _row_tile(n, cap):
    for t in range(cap - cap % 16, 0, -16):
        if n % t == 0:
            return t
    raise ValueError(n)


def _sds(shape, dtype):
    return jax.ShapeDtypeStruct(shape, dtype)


def _silu(x):
    return x * jax.nn.sigmoid(x)


def _dsilu(x):
    s = jax.nn.sigmoid(x)
    return s * (1.0 + x * (1.0 - s))


def _norm_fwd(x, g, res, out_dtype, name):
    n, d = x.shape
    tr = _row_tile(n, 384)

    def body(*refs):
        if res is None:
            x_ref, g_ref, o_ref = refs
        else:
            x_ref, g_ref, r_ref, o_ref = refs
        xv = x_ref[...]
        r = lax.rsqrt(jnp.mean(xv * xv, axis=-1, keepdims=True) + EPS)
        y = xv * r * g_ref[...]
        if res is not None:
            y = y + r_ref[...]
        o_ref[...] = y.astype(o_ref.dtype)

    row = pl.BlockSpec((tr, d), lambda i: (i, 0))
    ins = [row, pl.BlockSpec((1, d), lambda i: (0, 0))] + ([row] if res is not None else [])
    args = (x, g) + ((res,) if res is not None else ())
    return pl.pallas_call(body, grid=(n // tr,), in_specs=ins, out_specs=row, out_shape=_sds((n, d), out_dtype), name=name)(*args)


def _norm_res_norm(x, g, res, g_next, name):
    n, d = x.shape
    tr = _row_tile(n, 384)

    def body(x_ref, g_ref, r_ref, gn_ref, h_ref, u_ref):
        xv = x_ref[...]
        hv = r_ref[...] + xv * lax.rsqrt(jnp.mean(xv * xv, axis=-1, keepdims=True) + EPS) * g_ref[...]
        h_ref[...] = hv
        u_ref[...] = (hv * lax.rsqrt(jnp.mean(hv * hv, axis=-1, keepdims=True) + EPS) * gn_ref[...]).astype(u_ref.dtype)

    row = pl.BlockSpec((tr, d), lambda i: (i, 0))
    vec = pl.BlockSpec((1, d), lambda i: (0, 0))
    return pl.pallas_call(body, grid=(n // tr,), in_specs=[row, vec, row, vec], out_specs=(row, row),
                          out_shape=(_sds((n, d), F32), _sds((n, d), BF16)), name=name)(x, g, res, g_next)


OUT_PROJ_TOKEN_TILE = 192


def _out_proj_norms(x, w, res, g, g_next, name):
    n, k = x.shape
    d = w.shape[1]
    tm = _row_tile(n, OUT_PROJ_TOKEN_TILE)

    def body(x_ref, w_ref, r_ref, g_ref, gn_ref, mo_ref, h_ref, u_ref):
        mo = jnp.dot(x_ref[...], w_ref[...], preferred_element_type=F32)
        mo_ref[...] = mo
        hv = r_ref[...] + mo * lax.rsqrt(jnp.mean(mo * mo, axis=-1, keepdims=True) + EPS) * g_ref[...]
        h_ref[...] = hv
        u_ref[...] = (hv * lax.rsqrt(jnp.mean(hv * hv, axis=-1, keepdims=True) + EPS) * gn_ref[...]).astype(u_ref.dtype)

    row = pl.BlockSpec((tm, d), lambda i: (i, 0))
    vec = pl.BlockSpec((1, d), lambda i: (0, 0))
    return pl.pallas_call(body, grid=(n // tm,), in_specs=[pl.BlockSpec((tm, k), lambda i: (i, 0)), pl.BlockSpec((k, d), lambda i: (0, 0)), row, vec, vec],
                          out_specs=(row, row, row), out_shape=(_sds((n, d), F32), _sds((n, d), F32), _sds((n, d), BF16)), name=name)(
                              x, w, res, g, g_next)


def _norm_bwd(dy, x, g, res, out_dtype, name):
    n, d = x.shape
    tr = _row_tile(n, 384)

    def body(*refs):
        if res is None:
            dy_ref, x_ref, g_ref, dx_ref, dg_ref = refs
        else:
            dy_ref, x_ref, g_ref, r_ref, dx_ref, dg_ref = refs
        i = pl.program_id(0)
        xv = x_ref[...]
        r = lax.rsqrt(jnp.mean(xv * xv, axis=-1, keepdims=True) + EPS)
        xhat = xv * r
        dyf = dy_ref[...].astype(F32)
        gdy = dyf * g_ref[...]
        dx = r * (gdy - xhat * jnp.mean(gdy * xhat, axis=-1, keepdims=True))
        if res is not None:
            dx = dx + r_ref[...]
        dx_ref[...] = dx.astype(dx_ref.dtype)

        @pl.when(i == 0)
        def _():
            dg_ref[...] = jnp.zeros_like(dg_ref)

        dg_ref[...] += jnp.sum(dyf * xhat, axis=0, keepdims=True)

    row = pl.BlockSpec((tr, d), lambda i: (i, 0))
    vec = pl.BlockSpec((1, d), lambda i: (0, 0))
    ins = [row, row, vec] + ([row] if res is not None else [])
    args = (dy, x, g) + ((res,) if res is not None else ())
    return pl.pallas_call(body, grid=(n // tr,), in_specs=ins, out_specs=(row, vec),
                          out_shape=(_sds((n, d), out_dtype), _sds((1, d), F32)), name=name,
                          compiler_params=pltpu.CompilerParams(dimension_semantics=("arbitrary",)))(*args)


def _norm_bwd_pair(dy, x1, g1, res, x2, g2, name):
    n, d = x1.shape
    tr = _row_tile(n, 192)

    def rms_bwd(dyf, xv, g):
        r = lax.rsqrt(jnp.mean(xv * xv, axis=-1, keepdims=True) + EPS)
        xhat = xv * r
        gdy = dyf * g
        return r * (gdy - xhat * jnp.mean(gdy * xhat, axis=-1, keepdims=True)), jnp.sum(dyf * xhat, axis=0, keepdims=True)

    def body(dy_ref, x1_ref, g1_ref, r_ref, x2_ref, g2_ref, dh_ref, d2_ref, dg1_ref, dg2_ref):
        @pl.when(pl.program_id(0) == 0)
        def _():
            dg1_ref[...] = jnp.zeros_like(dg1_ref)
            dg2_ref[...] = jnp.zeros_like(dg2_ref)

        dx1, s1 = rms_bwd(dy_ref[...].astype(F32), x1_ref[...], g1_ref[...])
        dh = dx1 + r_ref[...]
        dh_ref[...] = dh
        dx2, s2 = rms_bwd(dh, x2_ref[...], g2_ref[...])
        d2_ref[...] = dx2.astype(d2_ref.dtype)
        dg1_ref[...] += s1
        dg2_ref[...] += s2

    row = pl.BlockSpec((tr, d), lambda i: (i, 0))
    vec = pl.BlockSpec((1, d), lambda i: (0, 0))
    return pl.pallas_call(body, grid=(n // tr,), in_specs=[row, row, vec, row, row, vec], out_specs=(row, row, vec, vec),
                          out_shape=(_sds((n, d), F32), _sds((n, d), BF16), _sds((1, d), F32), _sds((1, d), F32)), name=name,
                          compiler_params=pltpu.CompilerParams(dimension_semantics=("arbitrary",)))(dy, x1, g1, res, x2, g2)


def _mm(a, b, *, ta, tb, grid, a_blk, a_map, b_blk, b_map, o_blk, o_map, o_shape, o_dtype, name, acc=None):
    nk = grid[2]
    dims = (((0,) if ta else (1,), (1,) if tb else (0,)), ((), ()))

    def body(*refs):
        if acc is None:
            a_ref, b_ref, o_ref = refs[:3]
            c_ref = None
        else:
            a_ref, b_ref, c_ref, o_ref = refs[:4]
        part = lax.dot_general(a_ref[...], b_ref[...], dims, preferred_element_type=F32)
        if nk == 1:
            if c_ref is not None:
                part = part + c_ref[...].astype(F32)
            o_ref[...] = part.astype(o_ref.dtype)
            return
        acc_ref = refs[-1]
        k = pl.program_id(2)

        @pl.when(k == 0)
        def _():
            acc_ref[...] = jnp.zeros_like(acc_ref) if c_ref is None else c_ref[...].astype(F32)

        acc_ref[...] += part

        @pl.when(k == nk - 1)
        def _():
            o_ref[...] = acc_ref[...].astype(o_ref.dtype)

    ins = [pl.BlockSpec(a_blk, a_map), pl.BlockSpec(b_blk, b_map)]
    args = [a, b]
    if acc is not None:
        ins.append(pl.BlockSpec(o_blk, o_map))
        args.append(acc)
    return pl.pallas_call(body, grid=grid, in_specs=ins, out_specs=pl.BlockSpec(o_blk, o_map), out_shape=_sds(o_shape, o_dtype),
                          scratch_shapes=[pltpu.VMEM(o_blk, F32)] if nk > 1 else [], name=name,
                          compiler_params=pltpu.CompilerParams(dimension_semantics=("parallel", "parallel", "arbitrary")))(*args)


TOKEN_TILE = 1056
WIDE_TILE = 1408
DEEP_K_TOKEN_TILE = 528


def _mm_nn(x, w, o_dtype, tn, tk, name, acc=None, tm_cap=TOKEN_TILE):
    n, k = x.shape
    tm = _row_tile(n, tm_cap)
    return _mm(x, w, ta=False, tb=False, grid=(n // tm, w.shape[1] // tn, k // tk), a_blk=(tm, tk), a_map=lambda i, j, kk: (i, kk),
               b_blk=(tk, tn), b_map=lambda i, j, kk: (kk, j), o_blk=(tm, tn), o_map=lambda i, j, kk: (i, j),
               o_shape=(n, w.shape[1]), o_dtype=o_dtype, name=name, acc=acc)


def _mm_nt(dy, w, o_dtype, tn, tk, name, acc=None):
    n, k = dy.shape
    tm = _row_tile(n, TOKEN_TILE)
    return _mm(dy, w, ta=False, tb=True, grid=(n // tm, w.shape[0] // tn, k // tk), a_blk=(tm, tk), a_map=lambda i, j, kk: (i, kk),
               b_blk=(tn, tk), b_map=lambda i, j, kk: (j, kk), o_blk=(tm, tn), o_map=lambda i, j, kk: (i, j),
               o_shape=(n, w.shape[0]), o_dtype=o_dtype, name=name, acc=acc)


def _mm_tn(x, dy, tm, tn, name):
    n, m = x.shape
    tk = _row_tile(n, 2 * TOKEN_TILE)
    return _mm(x, dy, ta=True, tb=False, grid=(m // tm, dy.shape[1] // tn, n // tk), a_blk=(tk, tm), a_map=lambda i, j, kk: (kk, i),
               b_blk=(tk, tn), b_map=lambda i, j, kk: (kk, j), o_blk=(tm, tn), o_map=lambda i, j, kk: (i, j),
               o_shape=(m, dy.shape[1]), o_dtype=BF16, name=name)


def _rope_tables(n):
    pos = (jnp.arange(n) - PAD_FRONT).astype(F32)
    half = ROT_DIM // 2
    ang = pos[:, None] * (ROPE_THETA ** (-jnp.arange(half, dtype=F32) / half))[None, :]
    c, s = jnp.cos(ang), jnp.sin(ang)
    rest = ATT_HEAD_DIM - ROT_DIM
    cos_a = jnp.concatenate([c, c, jnp.ones((n, rest), F32)], axis=1)
    sin_a = jnp.concatenate([-s, s, jnp.zeros((n, rest), F32)], axis=1)
    half = RET_HEAD_DIM // 2
    ang = pos[:, None] * (RET_THETA ** (-jnp.arange(half, dtype=F32) / half))[None, :]
    c, s = jnp.cos(ang), jnp.sin(ang)
    perm = np.zeros((ATT_HEAD_DIM, ATT_HEAD_DIM), np.float32)
    for i in range(ROT_DIM):
        perm[(i + ROT_DIM // 2) % ROT_DIM, i] = 1.0
    return cos_a, sin_a, jnp.concatenate([c, c], axis=1), jnp.concatenate([-s, s], axis=1), jnp.asarray(perm, BF16)


def _into(buf, own_shape):
    if buf is None:
        return _sds(own_shape, BF16), [], [], lambda n_inputs: {}
    return _sds(buf.shape, buf.dtype), [ANY], [buf], lambda n_inputs: {n_inputs: 0}


def _rope_att(x, col0, heads, cos, sin, perm, name, plain=None, into=None, out_col=0):
    n = x.shape[0]
    tr = _row_tile(n, 1056)
    hd = ATT_HEAD_DIM
    w2 = 0 if plain is None else plain.shape[1]
    width = heads * hd + w2

    def body(*refs):
        x_ref, c_ref, s_ref, p_ref, o_ref = refs[0], refs[1], refs[2], refs[3], refs[-1]
        for h in range(heads):
            cs = slice(h * hd, (h + 1) * hd)
            xb = x_ref[:, cs].astype(BF16)
            sw = jnp.dot(xb, p_ref[...], preferred_element_type=F32)
            o_ref[:, cs] = (xb.astype(F32) * c_ref[...] + sw * s_ref[...]).astype(o_ref.dtype)
        if plain is not None:
            o_ref[:, heads * hd:] = refs[4][...].astype(o_ref.dtype)

    tab = pl.BlockSpec((tr, hd), lambda i: (i, 0))
    ins = [pl.BlockSpec((tr, heads * hd), lambda i: (i, col0)), tab, tab, pl.BlockSpec((hd, hd), lambda i: (0, 0))]
    args = [x, cos, sin, perm]
    if plain is not None:
        ins.append(pl.BlockSpec((tr, w2), lambda i: (i, 0)))
        args.append(plain)
    shape, extra_specs, extra_args, alias = _into(into, (n, width))
    return pl.pallas_call(body, grid=(n // tr,), in_specs=ins + extra_specs, out_specs=pl.BlockSpec((tr, width), lambda i: (i, out_col)),
                          out_shape=shape, input_output_aliases=alias(len(ins)), name=name)(*args, *extra_args)


def _rope_ret(x, col0, cos, sin, name, into=None, out_col=0):
    p, n, _ = x.shape
    tr = _row_tile(n, 1056)
    hd = RET_HEAD_DIM

    def body(*refs):
        x_ref, o_ref = refs[0], refs[-1]
        for h in range(2):
            cs = slice(h * hd, (h + 1) * hd)
            xv = x_ref[0, :, cs].astype(F32)
            for q in range(1, p):
                xv = xv + x_ref[q, :, cs].astype(F32)
            if cos is not None:
                sw = jnp.concatenate([xv[:, hd // 2:], xv[:, :hd // 2]], axis=1)
                xv = xv * refs[1][...] + sw * refs[2][...]
            o_ref[:, cs] = xv.astype(o_ref.dtype)

    tab = pl.BlockSpec((tr, hd), lambda i, j: (i, 0))
    ins = [pl.BlockSpec((p, tr, 2 * hd), lambda i, j: (0, i, col0 + j))] + ([tab, tab] if cos is not None else [])
    args = (x,) + ((cos, sin) if cos is not None else ())
    shape, extra_specs, extra_args, alias = _into(into, (n, RET_WIDTH))
    return pl.pallas_call(body, grid=(n // tr, RET_HEADS // 2), in_specs=ins + extra_specs,
                          out_specs=pl.BlockSpec((tr, 2 * hd), lambda i, j: (i, out_col + j)), out_shape=shape,
                          input_output_aliases=alias(len(ins)), name=name)(*args, *extra_args)


def _att_mask(nblk, n_tot):
    row = lax.broadcasted_iota(jnp.int32, (BLOCK, 4 * BLOCK), 0)
    col = lax.broadcasted_iota(jnp.int32, (BLOCK, 4 * BLOCK), 1)
    qi = nblk * BLOCK + row
    seg = col // BLOCK
    cj = col % BLOCK
    kj = (nblk - 1 + seg) * BLOCK + cj
    band = (jnp.abs(qi - kj) <= WINDOW) & (kj >= PAD_FRONT) & (kj < n_tot) & (seg < 3)
    meta = (seg == 3) & (cj >= PAD_FRONT) & (jnp.abs(qi - cj) > WINDOW)
    return band | meta


def _att_specs(nb, v_col):
    kv = lambda f, cb: pl.BlockSpec((BLOCK, KV_WIDTH), lambda n: (f(n), cb))
    prev, own, nxt, first = (lambda n: jnp.maximum(n - 1, 0)), (lambda n: n), (lambda n: jnp.minimum(n + 1, nb - 1)), (lambda n: 0)
    return [kv(f, 0) for f in (prev, own, nxt, first)] + [kv(f, v_col) for f in (prev, own, nxt, first)]


def _att_probs(s, ok, snk):
    s = jnp.where(ok, s, NEG)
    m = jnp.maximum(jnp.max(s, axis=-1, keepdims=True), snk)
    p = jnp.exp(s - m)
    ps = jnp.exp(snk - m)
    inv = 1.0 / (jnp.sum(p, axis=-1, keepdims=True) + ps)
    return p * inv, ps * inv


def _att_fwd(q, k, proj, sink_b, name):
    n = q.shape[0]
    nb = n // BLOCK
    hd = ATT_HEAD_DIM

    def body(q_ref, kp, ko, kn, km, vp, vo, vn, vm, sink_ref, o_ref):
        nblk = pl.program_id(0)
        ok = _att_mask(nblk, n)
        keep = (nblk * BLOCK + lax.broadcasted_iota(jnp.int32, (BLOCK, 1), 0)) >= PAD_FRONT
        kk = jnp.concatenate([r[...] for r in (kp, ko, kn, km)], axis=0)
        vv = jnp.concatenate([r[...] for r in (vp, vo, vn, vm)], axis=0)
        q4 = jnp.concatenate([q_ref[:, g * hd:(g + 1) * hd] for g in range(ATT_GROUP)], axis=0)
        s = lax.dot_general(q4, kk, (((1,), (1,)), ((), ())), preferred_element_type=F32) * ATT_SCALE
        ps = [_att_probs(s[g * BLOCK:(g + 1) * BLOCK], ok, sink_ref[g:g + 1, 0:1])[0] for g in range(ATT_GROUP)]
        o = jnp.dot(jnp.concatenate(ps, axis=0).astype(BF16), vv, preferred_element_type=F32)
        for g in range(ATT_GROUP):
            o_ref[:, g * hd:(g + 1) * hd] = jnp.where(keep, o[g * BLOCK:(g + 1) * BLOCK], 0.0).astype(o_ref.dtype)

    gw = ATT_GROUP * hd
    qspec = pl.BlockSpec((BLOCK, gw), lambda i, kh: (i, kh))
    prev, own, nxt, first = (lambda i: jnp.maximum(i - 1, 0)), (lambda i: i), (lambda i: jnp.minimum(i + 1, nb - 1)), (lambda i: 0)
    kv = lambda f, col: pl.BlockSpec((BLOCK, hd), lambda i, kh: (f(i), col + kh))
    v_col = (ATT_WIDTH + KV_WIDTH) // hd
    return pl.pallas_call(body, grid=(nb, ATT_KV_HEADS),
                          in_specs=[qspec] + [kv(f, 0) for f in (prev, own, nxt, first)] + [kv(f, v_col) for f in (prev, own, nxt, first)]
                          + [pl.BlockSpec((None, ATT_GROUP, 128), lambda i, kh: (kh, 0, 0))],
                          out_specs=qspec, out_shape=_sds((n, D_MODEL), BF16), name=name)(
                              q, k, k, k, k, proj, proj, proj, proj, sink_b.reshape(ATT_KV_HEADS, ATT_GROUP, 128))


def _att_bwd(q, k, proj, sink_b, dmixed, name):
    n = q.shape[0]
    nb = n // BLOCK
    hd = ATT_HEAD_DIM

    def body(q_ref, kp, ko, kn, km, vp, vo, vn, vm, sink_ref, do_ref, dq_ref, dk_ref, dv_ref, dsink_ref):
        nblk = pl.program_id(0)

        @pl.when(nblk == 0)
        def _():
            dk_ref[...] = jnp.zeros_like(dk_ref)
            dv_ref[...] = jnp.zeros_like(dv_ref)
            dsink_ref[...] = jnp.zeros_like(dsink_ref)

        ok = _att_mask(nblk, n)
        rows = [jnp.maximum(nblk - 1, 0), nblk, jnp.minimum(nblk + 1, nb - 1), 0]
        for kh in range(ATT_KV_HEADS):
            cs = slice(kh * hd, (kh + 1) * hd)
            kk = jnp.concatenate([r[:, cs] for r in (kp, ko, kn, km)], axis=0)
            vv = jnp.concatenate([r[:, cs] for r in (vp, vo, vn, vm)], axis=0)
            heads = [kh * ATT_GROUP + g for g in range(ATT_GROUP)]
            q4 = jnp.concatenate([q_ref[:, h * hd:(h + 1) * hd] for h in heads], axis=0)
            do4 = jnp.concatenate([do_ref[:, h * hd:(h + 1) * hd] for h in heads], axis=0)
            s = lax.dot_general(q4, kk, (((1,), (1,)), ((), ())), preferred_element_type=F32) * ATT_SCALE
            dp = lax.dot_general(do4, vv, (((1,), (1,)), ((), ())), preferred_element_type=F32)
            ps, dss = [], []
            for g, h in enumerate(heads):
                p, psink = _att_probs(s[g * BLOCK:(g + 1) * BLOCK], ok, sink_ref[h:h + 1, 0:1])
                dpg = dp[g * BLOCK:(g + 1) * BLOCK]
                delta = jnp.sum(p * dpg, axis=-1, keepdims=True)
                ps.append(p)
                dss.append(p * (dpg - delta) * ATT_SCALE)
                dsink_ref[h:h + 1, :] = dsink_ref[h:h + 1, :] - jnp.sum(psink * delta, axis=0, keepdims=True)
            ds = jnp.concatenate(dss, axis=0).astype(BF16)
            pb = jnp.concatenate(ps, axis=0).astype(BF16)
            dq = jnp.dot(ds, kk, preferred_element_type=F32)
            for g, h in enumerate(heads):
                dq_ref[:, h * hd:(h + 1) * hd] = dq[g * BLOCK:(g + 1) * BLOCK].astype(dq_ref.dtype)
            dk = lax.dot_general(ds, q4, (((0,), (0,)), ((), ())), preferred_element_type=F32)
            dv = lax.dot_general(pb, do4, (((0,), (0,)), ((), ())), preferred_element_type=F32)
            for seg, r in enumerate(rows):
                at = (pl.ds(pl.multiple_of(r * BLOCK, BLOCK), BLOCK), cs)
                dk_ref[at] += dk[seg * BLOCK:(seg + 1) * BLOCK]
                dv_ref[at] += dv[seg * BLOCK:(seg + 1) * BLOCK]

    qspec = pl.BlockSpec((BLOCK, ATT_WIDTH), lambda i: (i, 0))
    whole = pl.BlockSpec((n, KV_WIDTH), lambda i: (0, 0))
    sinks = pl.BlockSpec((ATT_HEADS, 128), lambda i: (0, 0))
    return pl.pallas_call(body, grid=(nb,), in_specs=[qspec] + _att_specs(nb, COL_AV256) + [sinks, qspec], out_specs=(qspec, whole, whole, sinks),
                          out_shape=(_sds((n, ATT_WIDTH), BF16), _sds((n, KV_WIDTH), F32), _sds((n, KV_WIDTH), F32), _sds((ATT_HEADS, 128), F32)),
                          name=name, compiler_params=pltpu.CompilerParams(dimension_semantics=("arbitrary",)))(
                              q, k, k, k, k, proj, proj, proj, proj, sink_b, dmixed)


def _ret_decay(lg, d):
    a = lax.broadcasted_iota(jnp.int32, (BLOCK, 1), 0)
    b = lax.broadcasted_iota(jnp.int32, (1, BLOCK), 1)
    t_col = a + d * (BLOCK - 1 - 2 * a)
    t_row = b + d * (BLOCK - 1 - 2 * b)
    diff = t_col - t_row
    dist = jnp.maximum(diff, 0).astype(F32)
    dmask = jnp.where(diff >= d, jnp.exp(lg * dist), 0.0)
    tf = t_col.astype(F32)
    xi = jnp.exp(lg * (tf + 1.0))
    zeta = jnp.exp(lg * (BLOCK - 1.0 - tf))
    gam = jnp.exp(jnp.full((1, 1), BLOCK, F32) * lg)
    return dmask, dist, xi, zeta, gam, tf


RET_GROUP = 3


def _ret_group(nc):
    return next(c for c in (RET_GROUP, 2, 1) if nc % c == 0)


def _ret_fwd(q, k, proj, lg, name):
    n = q.shape[0]
    nc = n // BLOCK
    hd = RET_HEAD_DIM
    ch = _ret_group(nc)
    ns, rows = nc // ch, ch * BLOCK
    group = lambda d, s: s + d * (ns - 1 - 2 * s)

    def body(lg_ref, q_ref, k_ref, v0, v1, v2, v3, o_ref, st_ref, s_ref):
        d, s = pl.program_id(0), pl.program_id(1)

        @pl.when(s == 0)
        def _():
            s_ref[...] = jnp.zeros_like(s_ref)

        for h, v_ref in enumerate((v0, v1, v2, v3)):
            cs = slice(h * hd, (h + 1) * hd)
            dmask, _, xi, zeta, gam, _ = _ret_decay(lg_ref[h, d], d)
            sb = s_ref[h]
            for j in range(ch):
                at = pl.ds(pl.multiple_of((j + d * (ch - 1 - 2 * j)) * BLOCK, BLOCK), BLOCK)
                qv = q_ref[at, cs]
                kf = k_ref[at, cs].astype(F32) * RET_K_SCALE
                vv = v_ref[at, :]
                sc = lax.dot_general(qv, kf.astype(BF16), (((1,), (1,)), ((), ())), preferred_element_type=F32)
                sb16 = sb.astype(BF16)
                o_ref[at, cs] = (jnp.dot((sc * dmask).astype(BF16), vv, preferred_element_type=F32)
                                 + jnp.dot((qv.astype(F32) * xi).astype(BF16), sb16, preferred_element_type=F32))
                st_ref[h, j] = sb16
                sb = gam * sb + lax.dot_general((kf * zeta).astype(BF16), vv, (((0,), (0,)), ((), ())), preferred_element_type=F32)
            s_ref[h] = sb

    wide = pl.BlockSpec((rows, RET_WIDTH), lambda d, s: (group(d, s), 0))
    vblk = lambda h: pl.BlockSpec((rows, hd), lambda d, s: (group(d, s), COL_RV + h))
    return pl.pallas_call(
        body, grid=(2, ns), in_specs=[pl.BlockSpec(memory_space=pltpu.SMEM), wide, wide] + [vblk(h) for h in range(RET_HEADS)],
        out_specs=(pl.BlockSpec((None, rows, RET_WIDTH), lambda d, s: (d, group(d, s), 0)),
                   pl.BlockSpec((RET_HEADS, None, ch, hd, hd), lambda d, s: (0, d, s, 0, 0))),
        out_shape=(_sds((2, n, RET_WIDTH), F32), _sds((RET_HEADS, 2, nc, hd, hd), BF16)), scratch_shapes=[pltpu.VMEM((RET_HEADS, hd, hd), F32)],
        name=name, compiler_params=pltpu.CompilerParams(dimension_semantics=("parallel", "arbitrary")))(lg, q, k, proj, proj, proj, proj)


def _ret_bwd(q, k, proj, lg, do, states, name):
    n = q.shape[0]
    nc = n // BLOCK
    hd = RET_HEAD_DIM
    ch = _ret_group(nc)
    ns, rows = nc // ch, ch * BLOCK
    group = lambda d, r: (ns - 1 - r) + d * (2 * r - (ns - 1))

    def body(lg_ref, q_ref, k_ref, v0, v1, v2, v3, do_ref, st_ref, dq_ref, dk_ref, dv_ref, dlg_ref, ds_ref):
        d, r = pl.program_id(0), pl.program_id(1)

        @pl.when(r == 0)
        def _():
            ds_ref[...] = jnp.zeros_like(ds_ref)
            dlg_ref[...] = jnp.zeros_like(dlg_ref)

        first_row = group(d, r) * rows
        nt = (((1,), (1,)), ((), ()))
        tn = (((0,), (0,)), ((), ()))
        for h, v_ref in enumerate((v0, v1, v2, v3)):
            cs = slice(h * hd, (h + 1) * hd)
            dmask, dist, xi, zeta, gam, tf = _ret_decay(lg_ref[h, d], d)
            dsn = ds_ref[h]
            dlg = jnp.zeros((1, 1), F32)
            for j in reversed(range(ch)):
                start = (j + d * (ch - 1 - 2 * j)) * BLOCK
                at = pl.ds(pl.multiple_of(start, BLOCK), BLOCK)
                keep = (lax.broadcasted_iota(jnp.int32, (BLOCK, 1), 0) + first_row + start) >= PAD_FRONT
                qv, vv, dov = q_ref[at, cs], v_ref[at, :], do_ref[at, cs]
                qf = qv.astype(F32)
                kf = k_ref[at, cs].astype(F32) * RET_K_SCALE
                kb = kf.astype(BF16)
                sc = st_ref[h, j]
                s = lax.dot_general(qv, kb, nt, preferred_element_type=F32)
                dsc = lax.dot_general(dov, vv, nt, preferred_element_type=F32) * dmask
                dsb = dsc.astype(BF16)
                dq_c = xi * lax.dot_general(dov, sc, nt, preferred_element_type=F32)
                dk_c = zeta * lax.dot_general(vv, dsn.astype(BF16), nt, preferred_element_type=F32)
                dq = jnp.dot(dsb, kb, preferred_element_type=F32) + dq_c
                dk = lax.dot_general(dsb, qv, tn, preferred_element_type=F32) + dk_c
                dv = (lax.dot_general((s * dmask).astype(BF16), dov, tn, preferred_element_type=F32)
                      + jnp.dot((kf * zeta).astype(BF16), dsn.astype(BF16), preferred_element_type=F32))
                dlg = dlg + (jnp.sum(dsc * s * dist, keepdims=True)
                             + jnp.sum((tf + 1.0) * jnp.sum(qf * dq_c, axis=-1, keepdims=True), keepdims=True)
                             + jnp.sum((BLOCK - 1.0 - tf) * jnp.sum(kf * dk_c, axis=-1, keepdims=True), keepdims=True)
                             + BLOCK * gam * jnp.sum(dsn * sc.astype(F32), keepdims=True))
                dsn = gam * dsn + lax.dot_general((qf * xi).astype(BF16), dov, tn, preferred_element_type=F32)
                dq_ref[at, cs] = dq
                dk_ref[at, cs] = jnp.where(keep, dk * RET_K_SCALE, 0.0)
                dv_ref[at, cs] = jnp.where(keep, dv, 0.0)
            ds_ref[h] = dsn
            dlg_ref[h] += dlg

    wide = pl.BlockSpec((rows, RET_WIDTH), lambda d, r: (group(d, r), 0))
    vblk = lambda h: pl.BlockSpec((rows, hd), lambda d, r: (group(d, r), COL_RV + h))
    plane = pl.BlockSpec((None, rows, RET_WIDTH), lambda d, r: (d, group(d, r), 0))
    return pl.pallas_call(
        body, grid=(2, ns),
        in_specs=[pl.BlockSpec(memory_space=pltpu.SMEM), wide, wide] + [vblk(h) for h in range(RET_HEADS)]
        + [wide, pl.BlockSpec((RET_HEADS, None, ch, hd, hd), lambda d, r: (0, d, ns - 1 - r, 0, 0))],
        out_specs=(plane, plane, plane, pl.BlockSpec((RET_HEADS, None, 8, 128), lambda d, r: (0, d, 0, 0))),
        out_shape=(_sds((2, n, RET_WIDTH), F32),) * 3 + (_sds((RET_HEADS, 2, 8, 128), F32),),
        scratch_shapes=[pltpu.VMEM((RET_HEADS, hd, hd), F32)], name=name,
        compiler_params=pltpu.CompilerParams(dimension_semantics=("parallel", "arbitrary")))(lg, q, k, proj, proj, proj, proj, do, states)


def _retgate_fwd(o, proj, gain, mixed, name):
    _, n, _ = o.shape
    tr = _row_tile(n, 1056)
    hd = RET_HEAD_DIM

    def body(o_ref, rg_ref, g_ref, _, y_ref):
        ov = o_ref[0] + o_ref[1]
        r = lax.rsqrt(jnp.mean(ov * ov, axis=-1, keepdims=True) + EPS)
        y_ref[...] = (_silu(rg_ref[...].astype(F32)) * (ov * r * g_ref[...])).astype(y_ref.dtype)

    return pl.pallas_call(body, grid=(n // tr, RET_HEADS),
                          in_specs=[pl.BlockSpec((2, tr, hd), lambda i, h: (0, i, h)), pl.BlockSpec((tr, hd), lambda i, h: (i, COL_RG + h)),
                                    pl.BlockSpec((1, hd), lambda i, h: (0, h)), ANY],
                          out_specs=pl.BlockSpec((tr, hd), lambda i, h: (i, ATT_WIDTH // hd + h)), out_shape=_sds(mixed.shape, mixed.dtype),
                          input_output_aliases={3: 0}, name=name)(o, proj, gain, mixed)


def _retgate_bwd(dmixed, o, proj, gain, name):
    _, n, _ = o.shape
    tr = _row_tile(n, 1056)
    hd = RET_HEAD_DIM

    def body(dy_ref, o_ref, rg_ref, g_ref, do_ref, drg_ref, dg_ref):
        i = pl.program_id(1)
        ov = o_ref[0] + o_ref[1]
        r = lax.rsqrt(jnp.mean(ov * ov, axis=-1, keepdims=True) + EPS)
        xhat = ov * r
        rg = rg_ref[...].astype(F32)
        dy = dy_ref[...].astype(F32)
        drg_ref[...] = (dy * (xhat * g_ref[...]) * _dsilu(rg)).astype(drg_ref.dtype)
        dn = dy * _silu(rg)
        dxh = dn * g_ref[...]
        do_ref[...] = (r * (dxh - xhat * jnp.mean(dxh * xhat, axis=-1, keepdims=True))).astype(do_ref.dtype)

        @pl.when(i == 0)
        def _():
            dg_ref[...] = jnp.zeros_like(dg_ref)

        dg_ref[...] += jnp.sum(dn * xhat, axis=0, keepdims=True)

    tile = pl.BlockSpec((tr, hd), lambda h, i: (i, h))
    vec = pl.BlockSpec((1, hd), lambda h, i: (0, h))
    rg_cols = pl.BlockSpec((tr, hd), lambda h, i: (i, COL_RG + h))
    return pl.pallas_call(body, grid=(RET_HEADS, n // tr),
                          in_specs=[pl.BlockSpec((tr, hd), lambda h, i: (i, ATT_WIDTH // hd + h)), pl.BlockSpec((2, tr, hd), lambda h, i: (0, i, h)),
                                    rg_cols, vec],
                          out_specs=(tile, rg_cols, vec), out_shape=(_sds((n, RET_WIDTH), BF16), _sds((n, IN_COLS), BF16), _sds((1, RET_WIDTH), F32)),
                          name=name, compiler_params=pltpu.CompilerParams(dimension_semantics=("parallel", "arbitrary")))(dmixed, o, proj, gain)


FFN_TILE = 512


def _swiglu_fwd(x, wg_t, wu_t, name):
    n, k = x.shape
    tm = _row_tile(n, TOKEN_TILE)
    nt = (((1,), (1,)), ((), ()))

    def body(x_ref, g_ref, u_ref, go_ref, uo_ref, f_ref):
        g = lax.dot_general(x_ref[...], g_ref[...], nt, preferred_element_type=F32)
        u = lax.dot_general(x_ref[...], u_ref[...], nt, preferred_element_type=F32)
        go_ref[...] = g.astype(go_ref.dtype)
        uo_ref[...] = u.astype(uo_ref.dtype)
        f_ref[...] = (_silu(g) * u).astype(f_ref.dtype)

    w = pl.BlockSpec((FFN_TILE, k), lambda i, j: (j, 0))
    o = pl.BlockSpec((tm, FFN_TILE), lambda i, j: (i, j))
    return pl.pallas_call(body, grid=(n // tm, wg_t.shape[0] // FFN_TILE), in_specs=[pl.BlockSpec((tm, k), lambda i, j: (i, 0)), w, w],
                          out_specs=(o, o, o), out_shape=(_sds((n, wg_t.shape[0]), BF16),) * 3, name=name,
                          compiler_params=pltpu.CompilerParams(dimension_semantics=("parallel", "parallel")))(x, wg_t, wu_t)


def _swiglu_bwd(dy, wd, gate, up, name):
    n, k = dy.shape
    tm = _row_tile(n, TOKEN_TILE)
    nt = (((1,), (1,)), ((), ()))

    def body(dy_ref, w_ref, g_ref, u_ref, dg_ref, du_ref):
        df = lax.dot_general(dy_ref[...], w_ref[...], nt, preferred_element_type=F32)
        g = g_ref[...].astype(F32)
        dg_ref[...] = (df * u_ref[...].astype(F32) * _dsilu(g)).astype(dg_ref.dtype)
        du_ref[...] = (df * _silu(g)).astype(du_ref.dtype)

    o = pl.BlockSpec((tm, FFN_TILE), lambda i, j: (i, j))
    return pl.pallas_call(body, grid=(n // tm, wd.shape[0] // FFN_TILE),
                          in_specs=[pl.BlockSpec((tm, k), lambda i, j: (i, 0)), pl.BlockSpec((FFN_TILE, k), lambda i, j: (j, 0)), o, o],
                          out_specs=(o, o), out_shape=(_sds((n, wd.shape[0]), BF16),) * 2, name=name,
                          compiler_params=pltpu.CompilerParams(dimension_semantics=("parallel", "parallel")))(dy, wd, gate, up)


def _loss_head(h, target, name):
    n, d = h.shape
    nb = n // BLOCK

    def body(h_ref, t_ref, dh_ref, l_ref):
        i = pl.program_id(0)

        @pl.when(i == 0)
        def _():
            l_ref[...] = jnp.zeros_like(l_ref)
            dh_ref[...] = jnp.zeros_like(dh_ref)

        @pl.when(i > 0)
        def _():
            e = h_ref[...] - t_ref[...]
            dh_ref[...] = e * (1.0 / d)
            l_ref[...] += 0.5 * jnp.sum(jnp.mean(e * e, axis=-1, keepdims=True), keepdims=True)

    blk = pl.BlockSpec((BLOCK, d), lambda i: (i, 0))
    return pl.pallas_call(body, grid=(nb,), in_specs=[blk, pl.BlockSpec((BLOCK, d), lambda i: (jnp.maximum(i - 1, 0), 0))],
                          out_specs=(blk, pl.BlockSpec((8, 128), lambda i: (0, 0))), out_shape=(_sds((n, d), F32), _sds((8, 128), F32)), name=name,
                          compiler_params=pltpu.CompilerParams(dimension_semantics=("arbitrary",)))(h, target)


def _adamw(parts, w, m, v, name, layer=None, prev=None, after=None):
    s, (r, c) = parts.shape[0], parts.shape[-2:]
    tr = _row_tile(r, max(16, (ADAMW_TILE_ELEMS // c) // 16 * 16))
    b1c, b2c = 1.0 - ADAM_B1 ** ADAM_STEP, 1.0 - ADAM_B2 ** ADAM_STEP

    def body(p_ref, w_ref, m_ref, v_ref, *rest):
        g_ref, d_ref, mo_ref, vo_ref = rest[-4:]
        g = p_ref[0].astype(F32)
        for q in range(1, s):
            g = g + p_ref[q].astype(F32)
        mn = ADAM_B1 * m_ref[...] + (1.0 - ADAM_B1) * g
        vn = ADAM_B2 * v_ref[...] + (1.0 - ADAM_B2) * jnp.square(g)
        g_ref[...] = g
        mo_ref[...] = mn
        vo_ref[...] = vn
        d_ref[...] = -ADAM_LR * ((mn / b1c) / (jnp.sqrt(vn / b2c) + ADAM_EPS) + ADAM_WD * w_ref[...])

    pspec = pl.BlockSpec((s, tr, c), lambda i: (0, i, 0))
    if layer is None:
        t = pl.BlockSpec((tr, c), lambda i: (i, 0))
        return pl.pallas_call(body, grid=(r // tr,), in_specs=[pspec, t, t, t], out_specs=(t, t, t, t), out_shape=(_sds((r, c), F32),) * 4,
                              name=name)(parts, w, m, v)
    t = pl.BlockSpec((None, tr, c), lambda i: (layer, i, 0))
    prev = prev if prev is not None else tuple(lax.empty(w.shape, F32) for _ in range(4))
    extra = [] if after is None else [after]
    return pl.pallas_call(body, grid=(r // tr,), in_specs=[pspec, t, t, t] + [ANY] * (4 + len(extra)), out_specs=(t, t, t, t),
                          out_shape=(_sds(w.shape, F32),) * 4, input_output_aliases={4 + i: i for i in range(4)}, name=name)(
                              parts, w, m, v, *prev, *extra)


def _allgather(xs, name, after=None):
    na = len(xs)
    first_out = na + (after is not None)

    def body(*refs):
        x_refs, o_refs = refs[:na], refs[first_out:first_out + na]
        send, recv, lsem = refs[first_out + na:]
        x, y, c = lax.axis_index("x"), lax.axis_index("y"), lax.axis_index("c")
        me, sib = (x, y, c), (x, y, 1 - c)
        chips = [(1 - x, y), (x, 1 - y), (1 - x, 1 - y)]
        slot = lambda p: 4 * p[0] + 2 * p[1] + p[2]

        def copy(a, k, block, to, src=None):
            dst = o_refs[a].at[slot(block)]
            return pltpu.make_async_remote_copy(src_ref=dst if src is None else src, dst_ref=dst, send_sem=send.at[a, k], recv_sem=recv.at[a, k],
                                                device_id=to, device_id_type=MESH)

        mine = [pltpu.make_async_copy(x_refs[a], o_refs[a].at[slot(me)], lsem.at[a]) for a in range(na)]
        for cp in mine:
            cp.start()
        first = []
        for a in range(na):
            first.append(copy(a, 0, me, sib, src=x_refs[a]))
            first += [copy(a, 1 + j, me, (*chip, c), src=x_refs[a]) for j, chip in enumerate(chips)]
        for cp in first:
            cp.start()
        passed = []
        for j, chip in enumerate(chips):
            for a in range(na):
                copy(a, 1 + j, (*chip, c), me).wait_recv()
                passed.append(copy(a, 4 + j, (*chip, c), sib))
                passed[-1].start()
        for a in range(na):
            copy(a, 0, sib, me).wait_recv()
            for j, chip in enumerate(chips):
                copy(a, 4 + j, (*chip, 1 - c), me).wait_recv()
        for cp in first + passed:
            cp.wait_send()
        for cp in mine:
            cp.wait()

    extra = [] if after is None else [after]
    return pl.pallas_call(body, in_specs=[ANY] * (na + len(extra)), out_specs=[ANY] * na,
                          out_shape=[_sds((N_DEV,) + t.shape, t.dtype) for t in xs],
                          scratch_shapes=[pltpu.SemaphoreType.DMA((na, 7)), pltpu.SemaphoreType.DMA((na, 7)), pltpu.SemaphoreType.DMA((na,))],
                          name=name)(*xs, *extra)


HBM = pl.BlockSpec(memory_space=pltpu.HBM)
SEM = pl.BlockSpec(memory_space=pltpu.SEMAPHORE)
EFFECT = pltpu.SideEffectType.DATAFLOW_SIDE_EFFECTING


SPLIT_RELATIONS = dict(gather=(1, 2, 4, 6),
                       forward=(2, 4, 6),
                       scatter=tuple(range(1, N_DEV)))


def _split_copies(mode, x_refs, land_refs, send, recv, own, landing):
    x, y, c = lax.axis_index("x"), lax.axis_index("y"), lax.axis_index("c")
    flip = lambda r: ((1 - x if r & 4 else x), (1 - y if r & 2 else y), (1 - c if r & 1 else c))
    slot = lambda p: 4 * p[0] + 2 * p[1] + p[2]
    me = slot((x, y, c))
    rel = SPLIT_RELATIONS[mode]
    local, remote = [], []
    for a in range(len(land_refs)):
        if mode != "forward":
            local.append(pltpu.make_async_copy(x_refs[a].at[me] if mode == "scatter" else x_refs[a], land_refs[a].at[me], own.at[a]))
        for j, r in enumerate(rel):
            if mode == "forward":
                to = flip(1)
                src = dst = land_refs[a].at[slot(flip(r ^ 1) if landing else flip(r))]
            else:
                to = flip(r)
                src = x_refs[a].at[slot(to)] if mode == "scatter" else x_refs[a]
                dst = land_refs[a].at[slot(to) if landing else me]
            remote.append(pltpu.make_async_remote_copy(src_ref=src, dst_ref=dst, send_sem=send.at[len(rel) * a + j],
                                                       recv_sem=recv.at[len(rel) * a + j], device_id=to, device_id_type=MESH))
    return local, remote


def _send_start(mode, xs, lands, after, name):
    if lands is None:
        lands = [lax.empty(t.shape if mode == "scatter" else (N_DEV,) + t.shape, t.dtype) for t in xs]
    nx, na, nr = len(xs), len(lands), len(SPLIT_RELATIONS[mode])
    nsem = 2 if mode == "forward" else 3

    def body(*refs):
        x_refs, land_refs = refs[:nx], refs[nx:nx + na]
        sems = refs[nx + na + 1:nx + na + 1 + nsem]
        local, remote = _split_copies(mode, x_refs, land_refs, sems[0], sems[1], sems[2] if nsem == 3 else None, False)
        for cp in remote + local:
            cp.start()
        refs[-1][...] = jnp.zeros_like(refs[-1])

    hbm = lambda t: pltpu.with_memory_space_constraint(t, pltpu.HBM)
    sem_shapes = [pltpu.SemaphoreType.DMA((nr * na,)), pltpu.SemaphoreType.DMA((nr * na,)), pltpu.SemaphoreType.DMA((na,))][:nsem]
    outs = pl.pallas_call(
        body, name=name,
        out_shape=(*sem_shapes, *[pltpu.HBM(t.shape, t.dtype) for t in list(xs) + list(lands)], _sds((8, 128), F32)),
        in_specs=[HBM] * (nx + na) + [ANY], out_specs=(*[SEM] * nsem, *[HBM] * (nx + na), pl.BlockSpec(memory_space=pltpu.VMEM)),
        input_output_aliases={i: nsem + i for i in range(nx + na)},
        compiler_params=pltpu.CompilerParams(has_side_effects=EFFECT))(*[hbm(t) for t in list(xs) + list(lands)], after)
    return outs[:nsem], list(outs[nsem:nsem + nx]), list(outs[nsem + nx:nsem + nx + na]), outs[-1]


def _send_wait(mode, started, after, name):
    sems, xs, lands, _ = started
    nx, na, nsem = len(xs), len(lands), len(sems)

    def body(*refs):
        s = refs[nx + na:nx + na + nsem]
        local, remote = _split_copies(mode, refs[:nx], refs[nx:nx + na], s[0], s[1], s[2] if nsem == 3 else None, True)
        for cp in remote:
            cp.wait_send()
            cp.wait_recv()
        for cp in local:
            cp.wait()

    outs = pl.pallas_call(body, name=name, out_shape=tuple(pltpu.HBM(t.shape, t.dtype) for t in xs + lands),
                          in_specs=[HBM] * (nx + na) + [SEM] * nsem + [ANY], out_specs=[HBM] * (nx + na),
                          input_output_aliases={i: i for i in range(nx + na)},
                          compiler_params=pltpu.CompilerParams(has_side_effects=EFFECT))(*xs, *lands, *sems, after)
    return list(outs[nx:])


def _local_step(x, meta, target, mix_weights_fn, ffn_weights_fn, prefetch_fn, grads_fn, sink, dec_f, dec_b, ret_norm, n_mix_pre, n_mix_post,
                n_ffn_pre, n_ffn_post):
    depth = n_mix_pre.shape[0]
    d = D_MODEL
    h = jnp.concatenate([jnp.zeros((PAD_FRONT, d), F32), meta, x], axis=0)
    n = h.shape[0]
    cos_a, sin_a, cos_r, sin_r, perm = _rope_tables(n)
    lg_all = jnp.stack([-jnp.exp(dec_f), -jnp.exp(dec_b)], axis=-1)
    saved = []
    u = _norm_fwd(h, n_mix_pre[0][None], None, BF16, "l0_norm_mix_pre")
    wi, wo = mix_weights_fn(0, h)
    for l in range(depth):
        t = f"l{l}_"
        sink_b = jnp.broadcast_to(sink[l][:, None], (ATT_HEADS, 128))
        proj = _mm_nt(u, wi, BF16, WIDE_TILE, d, t + "proj")
        aq = _rope_att(proj, 0, ATT_HEADS, cos_a, sin_a, perm, t + "rope_aq")
        ak = _rope_att(proj, ATT_WIDTH // KV_WIDTH, ATT_KV_HEADS, cos_a, sin_a, perm, t + "rope_ak")
        mixed = _att_fwd(aq, ak, proj, sink_b, t + "att")
        proj3 = proj[None]
        rq = _rope_ret(proj3, COL_RQ // 2, cos_r, sin_r, t + "rope_rq")
        rk = _rope_ret(proj3, COL_RK // 2, cos_r, sin_r, t + "rope_rk")
        o_ret, states = _ret_fwd(rq, rk, proj, lg_all[l], t + "ret")
        tok = prefetch_fn(2 * l + 1, o_ret)
        mixed = _retgate_fwd(o_ret, proj, (ret_norm[l] + tok)[None], mixed, t + "retgate")
        wg, wu, wd = ffn_weights_fn(l, mixed)
        mo, h_mid, u2 = _out_proj_norms(mixed, wo, h, n_mix_post[l][None], n_ffn_pre[l][None], t + "out_proj")
        gate, up, f = _swiglu_fwd(u2, wg, wu, t + "gate_up")
        post = (n_ffn_post[l] + prefetch_fn(2 * l + 2, f))[None]
        dn = _mm_nn(f, wd, F32, 512, D_FF, t + "down", tm_cap=DEEP_K_TOKEN_TILE)
        saved.append(dict(h=h, u=u, proj=proj, aq=aq, ak=ak, rq=rq, rk=rk, o_ret=o_ret, states=states, mixed=mixed, mo=mo, h_mid=h_mid, u2=u2,
                          gate=gate, up=up, f=f, dn=dn, sink_b=sink_b, wi=wi, wo=wo, wg=wg, wu=wu, wd=wd))
        if l + 1 < depth:
            wi, wo = mix_weights_fn(l + 1, dn)
            h, u = _norm_res_norm(dn, post, h_mid, n_mix_pre[l + 1][None], t + "norm_ffn_post")
        else:
            h = _norm_fwd(dn, post, h_mid, F32, t + "norm_ffn_post")

    dh, loss_part = _loss_head(h, target, "loss_head")
    gs = dict(sink=[None] * depth, dec_f=[None] * depth, dec_b=[None] * depth, ret_norm=[None] * depth, mix_pre=[None] * depth,
              mix_post=[None] * depth, ffn_pre=[None] * depth, ffn_post=[None] * depth)
    d_dn, gs["ffn_post"][depth - 1] = _norm_bwd(dh, saved[-1]["dn"], n_ffn_post[depth - 1][None], None, BF16, f"l{depth - 1}_b_norm_ffn_post")
    for l in reversed(range(depth)):
        t = f"l{l}_b_"
        sv = saved[l]
        proj = sv["proj"]
        gw = {}
        d_gate, d_up = _swiglu_bwd(d_dn, sv["wd"], sv["gate"], sv["up"], t + "d_gate_up")
        gw["wd"] = _mm_tn(sv["f"], d_dn, WIDE_TILE, 1024, t + "dw_down")
        du2 = _mm_nn(d_gate, sv["wg"], F32, 512, D_FF, t + "du2_gate", tm_cap=DEEP_K_TOKEN_TILE)
        du2 = _mm_nn(d_up, sv["wu"], F32, 512, D_FF, t + "du2_up", acc=du2, tm_cap=DEEP_K_TOKEN_TILE)
        gw["wg"] = _mm_tn(d_gate, sv["u2"], WIDE_TILE, 1024, t + "dw_gate")
        gw["wu"] = _mm_tn(d_up, sv["u2"], WIDE_TILE, 1024, t + "dw_up")
        tok_b = grads_fn(l, "ffn", gw, du2)
        dh, d_mo, gs["ffn_pre"][l], gs["mix_post"][l] = _norm_bwd_pair(du2, sv["h_mid"], (n_ffn_pre[l] + tok_b)[None], dh, sv["mo"],
                                                                         n_mix_post[l][None], t + "norm_ffn_pre")
        d_mixed = _mm_nt(d_mo, sv["wo"], BF16, 1024, d, t + "d_mixed")
        gw["wo"] = _mm_tn(sv["mixed"], d_mo, 1024, 1024, t + "dw_out")
        d_o, dproj, gs["ret_norm"][l] = _retgate_bwd(d_mixed, sv["o_ret"], proj, ret_norm[l][None], t + "retgate")
        dq_r, dk_r, dv_r, dlg = _ret_bwd(sv["rq"], sv["rk"], proj, lg_all[l], d_o, sv["states"], t + "ret")
        draw = dlg[:, :, 0, 0] * lg_all[l]
        gs["dec_f"][l], gs["dec_b"][l] = draw[:, 0], draw[:, 1]
        dproj = _rope_ret(dq_r, 0, cos_r, -sin_r, t + "rope_rq", into=dproj, out_col=COL_RQ // 2)
        dproj = _rope_ret(dk_r, 0, cos_r, -sin_r, t + "rope_rk", into=dproj, out_col=COL_RK // 2)
        dproj = _rope_ret(dv_r, 0, None, None, t + "sum_rv", into=dproj, out_col=COL_RV // 2)
        dq_a, dk_a, dv_a, dsink = _att_bwd(sv["aq"], sv["ak"], proj, sv["sink_b"], d_mixed, t + "att")
        gs["sink"][l] = dsink[:, 0]
        dproj = _rope_att(dq_a, 0, ATT_HEADS, cos_a, -sin_a, perm, t + "rope_aq", into=dproj, out_col=0)
        dproj = _rope_att(dk_a, 0, ATT_KV_HEADS, cos_a, -sin_a, perm, t + "rope_ak", plain=dv_a, into=dproj, out_col=ATT_WIDTH // (2 * KV_WIDTH))
        gw["wi"] = _mm_tn(dproj, sv["u"], WIDE_TILE, 1024, t + "dw_in")
        tok_b = grads_fn(l, "mix", gw, dproj)
        du = _mm_nn(dproj, sv["wi"], F32, 512, IN_COLS, t + "du", tm_cap=DEEP_K_TOKEN_TILE)
        if l > 0:
            dh, d_dn, gs["mix_pre"][l], gs["ffn_post"][l - 1] = _norm_bwd_pair(du, sv["h"], (n_mix_pre[l] + tok_b)[None], dh, saved[l - 1]["dn"],
                                                                              n_ffn_post[l - 1][None], t + "norm_mix_pre")
        else:
            dh, gs["mix_pre"][l] = _norm_bwd(du, sv["h"], (n_mix_pre[l] + tok_b)[None], dh, F32, t + "norm_mix_pre")
    return loss_part[0, 0], dh, gs


def _pack_small(mix_pre, mix_post, ffn_pre, ffn_post, ret_norm, sink, dec_f, dec_b, loss, meta):
    d = D_MODEL

    def tile(a, rows=8):
        a = jnp.reshape(a, (-1, a.shape[-1])) if a.ndim else jnp.reshape(a, (1, 1))
        return jnp.pad(a, ((0, rows - a.shape[0]), (0, d - a.shape[1])))

    return jnp.concatenate([tile(mix_pre), tile(mix_post), tile(ffn_pre), tile(ffn_post), tile(ret_norm.reshape(-1, d)), tile(sink), tile(dec_f),
                            tile(dec_b), tile(loss), tile(meta, SMALL_ROWS - ROW_META)], axis=0)


def _unpack_small(p, depth):
    rows = lambda r0, cols: p[r0:r0 + depth, :cols]
    return dict(mix_pre=rows(ROW_MIX_PRE, D_MODEL), mix_post=rows(ROW_MIX_POST, D_MODEL), ffn_pre=rows(ROW_FFN_PRE, D_MODEL),
                ffn_post=rows(ROW_FFN_POST, D_MODEL), ret_norm=p[ROW_RET_NORM:ROW_RET_NORM + depth * RET_WIDTH // D_MODEL].reshape(depth, RET_WIDTH),
                sink=rows(ROW_SINK, ATT_HEADS), dec_f=rows(ROW_DEC_F, RET_HEADS), dec_b=rows(ROW_DEC_B, RET_HEADS), loss=p[ROW_LOSS, 0])


def kernel(x, meta_tokens, w_in, w_out, attn_sink, ret_decay_fwd, ret_decay_bwd, ret_norm, norm_mix_pre, norm_mix_post, w_gate, w_up, w_down, norm_ffn_pre, norm_ffn_post, loss_target, m_meta_tokens, m_w_in, m_w_out, m_attn_sink, m_ret_decay_fwd, m_ret_decay_bwd, m_ret_norm, m_norm_mix_pre, m_norm_mix_post, m_w_gate, m_w_up, m_w_down, m_norm_ffn_pre, m_norm_ffn_post, v_meta_tokens, v_w_in, v_w_out, v_attn_sink, v_ret_decay_fwd, v_ret_decay_bwd, v_ret_norm, v_norm_mix_pre, v_norm_mix_post, v_w_gate, v_w_up, v_w_down, v_norm_ffn_pre, v_norm_ffn_post):
    depth, d = w_in.shape[0], D_MODEL
    me = 4 * lax.axis_index("x") + 2 * lax.axis_index("y") + lax.axis_index("c")
    zero = jnp.zeros((), F32)

    meta_g, = _allgather([meta_tokens], "gather_meta")
    meta = meta_g.transpose(1, 0, 2).reshape(N_META, d)

    def shards(k):
        l = k // 2
        if k % 2 == 0:
            return [w_in[l].T.astype(BF16), w_out[l].astype(BF16)]
        return [w_gate[l].T.astype(BF16), w_up[l].T.astype(BF16), w_down[l].astype(BF16)]

    gathers, ahead = {}, 2
    for k in range(min(ahead + 1, 2 * depth)):
        gathers[k] = _send_start("gather", shards(k), None, gathers[k - 1][3] if k else meta_g, f"gather_start_g{k}")

    passing = {}

    def pass_on(k, after):
        lands = _send_wait("gather", gathers.pop(k), after, f"gather_wait_g{k}")
        passing[k] = _send_start("forward", [], lands, after, f"forward_start_g{k}")

    def take(k, h):
        after = h
        if k >= 1 and k + ahead < 2 * depth:
            gathers[k + ahead] = _send_start("gather", shards(k + ahead), None, h, f"gather_start_g{k + ahead}")
            after = gathers[k + ahead][3]
        elif k == 0:
            after = gathers[max(gathers)][3]
        if k not in passing:
            pass_on(k, after)
        return _send_wait("forward", passing.pop(k), after, f"forward_wait_g{k}")

    def prefetch_fn(k, after):
        if k in gathers and k not in passing:
            pass_on(k, after)
            return passing[k][3][0, 0]
        return zero

    def mix_weights_fn(l, h):
        wi_t, wo = take(2 * l, h)
        return wi_t.reshape(IN_COLS, d), wo.reshape(d, d)

    def ffn_weights_fn(l, h):
        wg_t, wu_t, wd = take(2 * l + 1, h)
        return wg_t.reshape(D_FF, d), wu_t.reshape(D_FF, d), wd.reshape(D_FF, d)

    exchanges, adam, order = {}, {}, []
    tr = lambda *ts: tuple(jnp.swapaxes(t, 1, 2) for t in ts)
    big = dict(wi=tr(w_in, m_w_in, v_w_in), wg=tr(w_gate, m_w_gate, v_w_gate), wu=tr(w_up, m_w_up, v_w_up), wd=(w_down, m_w_down, v_w_down),
               wo=(w_out, m_w_out, v_w_out))
    kinds = dict(ffn=("wg", "wu", "wd"), mix=("wi", "wo"))

    arrivals = []

    def finish(key, after):
        l, part = key
        arrivals.append((key, _send_wait("scatter", exchanges.pop(key), after, f"exchange_wait_{part}_l{l}")))

    def update():
        last_start = exchanges[order[-1]][3] if order[-1] in exchanges else None
        while arrivals:
            (l, part), arrived = arrivals.pop(0)
            for kind, parts in zip(kinds[part], arrived):
                adam[kind] = _adamw(parts, *big[kind], f"adamw_{kind}_l{l}", layer=l, prev=adam.get(kind),
                                    after=None if kind in adam else last_start)

    def grads_fn(l, part, gw, after):
        packed = [gw[kind].reshape(N_DEV, -1, d) for kind in kinds[part]]
        exchanges[(l, part)] = _send_start("scatter", packed, None, after, f"exchange_start_{part}_l{l}")
        order.append((l, part))
        token = exchanges[(l, part)][3]
        if len(order) > 2:
            finish(order[-3], token)
        return token[0, 0]

    loss_part, dh, gs = _local_step(x[0], meta, loss_target[0], mix_weights_fn, ffn_weights_fn, prefetch_fn, grads_fn, attn_sink, ret_decay_fwd, ret_decay_bwd, ret_norm,
                                    norm_mix_pre, norm_mix_post, norm_ffn_pre, norm_ffn_post)
    grad_x = dh[BLOCK:][None]

    st = lambda xs: jnp.stack([t.reshape(-1) if t.ndim == 1 else t[0] for t in xs])
    small = _pack_small(st(gs["mix_pre"]), st(gs["mix_post"]), st(gs["ffn_pre"]), st(gs["ffn_post"]), st(gs["ret_norm"]), st(gs["sink"]),
                        st(gs["dec_f"]), st(gs["dec_b"]), loss_part, dh[PAD_FRONT:BLOCK])
    update()
    small_g, = _allgather([small], "gather_small", after=adam["wo"][0])
    for key in order[-2:]:
        finish(key, small_g)
    update()
    o_wi, o_wo, o_wg, o_wu, o_wd = tr(*adam["wi"]), adam["wo"], tr(*adam["wg"]), tr(*adam["wu"]), adam["wd"]
    zmeta = jnp.zeros((N_META, d), F32)
    packs = [_pack_small(a[0], a[1], a[2], a[3], a[4], a[5], a[6], a[7], zero, zmeta) for a in (
        (norm_mix_pre, norm_mix_post, norm_ffn_pre, norm_ffn_post, ret_norm, attn_sink, ret_decay_fwd, ret_decay_bwd),
        (m_norm_mix_pre, m_norm_mix_post, m_norm_ffn_pre, m_norm_ffn_post, m_ret_norm, m_attn_sink, m_ret_decay_fwd, m_ret_decay_bwd),
        (v_norm_mix_pre, v_norm_mix_post, v_norm_ffn_pre, v_norm_ffn_post, v_ret_norm, v_attn_sink, v_ret_decay_fwd, v_ret_decay_bwd))]
    o_small = [_unpack_small(o, depth) for o in _adamw(small_g, packs[0], packs[1], packs[2], "adamw_small")]
    meta_parts = lax.dynamic_slice(small_g, (0, ROW_META, me * (d // N_DEV)), (N_DEV, N_META, d // N_DEV))
    o_meta = _adamw(meta_parts, meta_tokens, m_meta_tokens, v_meta_tokens, "adamw_meta")

    outs = []
    for i in range(4):
        s = o_small[i]
        outs += [o_meta[i], o_wi[i], o_wo[i], s["sink"], s["dec_f"], s["dec_b"], s["ret_norm"], s["mix_pre"], s["mix_post"], o_wg[i], o_wu[i],
                 o_wd[i], s["ffn_pre"], s["ffn_post"]]
    return (o_small[0]["loss"], grad_x, *outs)
```
